```python
import math
import jax, jax.numpy as jnp
from jax import lax
import numpy as np

D_MODEL = 1024
BATCH = 8
SEQ = 8192
DEPTH = 1

N_META = 16
BLOCK = 128
PAD = BLOCK - N_META

HG_HEADS = 4
HG_K = 128
HG_V = 128
HG_KW = HG_HEADS * HG_K
HG_VW = HG_HEADS * HG_V
SUB = 16

ATT_HEADS = 8
ATT_KV_HEADS = 2
HEAD_DIM = 64
ATT_QW = ATT_HEADS * HEAD_DIM
ATT_KVW = ATT_KV_HEADS * HEAD_DIM
WINDOW = 128
ROPE_THETA = 10000.0

N_BRANCH = 2
D_FF = ((-(-8 * D_MODEL // 3) + 255) // 256) * 256
EPS = 1e-5
ALPHA = (2.0 * DEPTH) ** 0.25
BETA = (8.0 * DEPTH) ** -0.25
SPLIT_SIZES = (HG_KW, HG_KW, HG_VW, HG_VW, ATT_QW, ATT_KVW, ATT_KVW, N_BRANCH * D_MODEL)
IN_W = sum(SPLIT_SIZES)

kernel_name = "hgrn2_swa_sink_hybrid_deepnorm"


def layer_norm(x, g, b):
    xf = x.astype(jnp.float32)
    mu = jnp.mean(xf, axis=-1, keepdims=True)
    var = jnp.mean(jnp.square(xf - mu), axis=-1, keepdims=True)
    y = (xf - mu) * lax.rsqrt(var + EPS) * g.astype(jnp.float32) + b.astype(jnp.float32)
    return y.astype(x.dtype)


def rope(x, pos):
    half = HEAD_DIM // 2
    inv = ROPE_THETA ** (-jnp.arange(half, dtype=jnp.float32) / half)
    ang = pos.astype(jnp.float32)[:, None] * inv[None, :]
    cos = jnp.cos(ang)[None, :, None, :]
    sin = jnp.sin(ang)[None, :, None, :]
    xf = x.astype(jnp.float32)
    x1, x2 = xf[..., :half], xf[..., half:]
    return jnp.concatenate([x1 * cos - x2 * sin, x2 * cos + x1 * sin], axis=-1).astype(x.dtype)


def hgrn2_chunk(state, inp):
    q, k, v, log_f = inp
    B, H, C, K = q.shape
    V = v.shape[-1]
    n = C // SUB
    b = jnp.cumsum(log_f, axis=2)
    b_last = b[:, :, -1]
    o_inter = jnp.einsum('bhck,bhkv->bhcv', q * jnp.exp(b), state)
    qs = q.reshape(B, H, n, SUB, K)
    ks = k.reshape(B, H, n, SUB, K)
    vs = v.reshape(B, H, n, SUB, V)
    bs = b.reshape(B, H, n, SUB, K)
    tri = jnp.tril(jnp.ones((SUB, SUB), dtype=bool))[:, :, None]
    diff = bs[:, :, :, :, None, :] - bs[:, :, :, None, :, :]
    decay = jnp.exp(jnp.where(tri, diff, -jnp.inf))
    a_diag = jnp.einsum('bhntk,bhnsk,bhntsk->bhnts', qs, ks, decay)
    o_diag = jnp.einsum('bhnts,bhnsv->bhntv', a_diag, vs)
    b_ref = jnp.concatenate([jnp.zeros_like(bs[:, :, :1, 0]), bs[:, :, :-1, -1]], axis=2)
    q_off = qs * jnp.exp(bs - b_ref[:, :, :, None])
    earlier = (jnp.arange(C) // SUB)[None, :] < jnp.arange(n)[:, None]
    k_off = k[:, :, None] * jnp.exp(jnp.where(earlier[:, :, None],
                                              b_ref[:, :, :, None] - b[:, :, None], -jnp.inf))
    a_off = jnp.einsum('bhntk,bhnsk->bhnts', q_off, k_off)
    o_off = jnp.einsum('bhnts,bhsv->bhntv', a_off, v)
    o = o_inter + (o_diag + o_off).reshape(B, H, C, V)
    new_state = state * jnp.exp(b_last)[..., None] + jnp.einsum(
        'bhck,bhcv->bhkv', k * jnp.exp(b_last[:, :, None] - b), v)
    return new_state, o


def hgrn2_mixer(q_raw, f_raw, i_raw, g_raw, lower_bound, norm_g, valid):
    B, P, _ = q_raw.shape
    N = P // BLOCK
    f32 = jnp.float32
    q = jax.nn.silu(q_raw.astype(f32))
    fg = lower_bound + (1.0 - lower_bound) * jax.nn.sigmoid(f_raw.astype(f32))
    m = valid[None, :, None]
    log_f = jnp.where(m, jnp.log(fg), 0.0)
    k = jnp.where(m, 1.0 - fg, 0.0)
    v = i_raw.astype(f32)

    def to_chunks(t, dh):
        return t.reshape(B, N, BLOCK, HG_HEADS, dh).transpose(1, 0, 3, 2, 4)

    xs = (to_chunks(q, HG_K), to_chunks(k, HG_K), to_chunks(v, HG_V), to_chunks(log_f, HG_K))
    s0 = jnp.zeros((B, HG_HEADS, HG_K, HG_V), f32)
    _, o = lax.scan(hgrn2_chunk, s0, xs)
    o = o.transpose(1, 0, 3, 2, 4).reshape(B, P, HG_HEADS, HG_V)
    o = o * lax.rsqrt(jnp.mean(o * o, axis=-1, keepdims=True) + EPS) * norm_g.astype(f32)
    gate = jax.nn.silu(g_raw.astype(f32)).reshape(B, P, HG_HEADS, HG_V)
    return (o * gate).reshape(B, P, HG_VW).astype(q_raw.dtype)


def swa_sink_attention(q_raw, k_raw, v_raw, sinks, pos):
    B, P, _ = q_raw.shape
    NB = P // BLOCK
    G = ATT_HEADS // ATT_KV_HEADS
    f32 = jnp.float32
    q = rope(q_raw.reshape(B, P, ATT_HEADS, HEAD_DIM), pos).astype(f32)
    k = rope(k_raw.reshape(B, P, ATT_KV_HEADS, HEAD_DIM), pos).astype(f32)
    v = v_raw.reshape(B, P, ATT_KV_HEADS, HEAD_DIM).astype(f32)
    scale = HEAD_DIM ** -0.5
    qb = q.reshape(B, NB, BLOCK, ATT_KV_HEADS, G, HEAD_DIM)
    kb = k.reshape(B, NB, BLOCK, ATT_KV_HEADS, HEAD_DIM)
    vb = v.reshape(B, NB, BLOCK, ATT_KV_HEADS, HEAD_DIM)
    shift = lambda t: jnp.pad(t, ((0, 0), (1, 0), (0, 0), (0, 0), (0, 0)))[:, :-1]
    k_band = jnp.concatenate([shift(kb), kb], axis=2)
    v_band = jnp.concatenate([shift(vb), vb], axis=2)
    k_meta = k[:, PAD:BLOCK]
    v_meta = v[:, PAD:BLOCK]
    pos_b = pos.reshape(NB, BLOCK)
    pos_prev = jnp.concatenate([jnp.full((1, BLOCK), -1, pos.dtype), pos_b[:-1]], axis=0)
    key_pos = jnp.concatenate([pos_prev, pos_b], axis=1)[:, None, :]
    meta_pos = pos[PAD:BLOCK][None, None, :]
    qp = pos_b[:, :, None]
    band_ok = (key_pos >= N_META) & (key_pos <= qp) & (qp - key_pos < WINDOW)
    meta_ok = meta_pos <= qp
    neg = jnp.finfo(f32).min
    s_band = jnp.einsum('bnqhgd,bnkhd->bnhgqk', qb, k_band) * scale
    s_meta = jnp.einsum('bnqhgd,bmhd->bnhgqm', qb, k_meta) * scale
    s_band = jnp.where(band_ok[None, :, None, None], s_band, neg)
    s_meta = jnp.where(meta_ok[None, :, None, None], s_meta, neg)
    sink = jnp.broadcast_to(sinks.astype(f32).reshape(ATT_KV_HEADS, G)[None, None, :, :, None, None],
                            s_meta.shape[:-1] + (1,))
    p = jax.nn.softmax(jnp.concatenate([s_meta, s_band, sink], axis=-1), axis=-1)
    p_meta = p[..., :N_META]
    p_band = p[..., N_META:N_META + 2 * BLOCK]
    o = (jnp.einsum('bnhgqm,bmhd->bnqhgd', p_meta, v_meta)
         + jnp.einsum('bnhgqk,bnkhd->bnqhgd', p_band, v_band))
    return o.reshape(B, P, ATT_QW).astype(q_raw.dtype)


def _fwd_setup_inputs(seed: int = 0) -> dict:
    key = jax.random.key(seed)
    ks = jax.random.split(key, 18)
    f32 = jnp.float32
    nrm = lambda k, shape, s: jax.random.normal(k, shape, f32) * s
    return {
        "x": nrm(ks[0], (BATCH, SEQ, D_MODEL), 1.0),
        "meta_tokens": nrm(ks[1], (N_META, D_MODEL), 1.0),
        "ln_emb_g": 1.0 + nrm(ks[2], (D_MODEL,), 0.02),
        "ln_emb_b": nrm(ks[3], (D_MODEL,), 0.02),
        "w_in": nrm(ks[4], (DEPTH, D_MODEL, IN_W), D_MODEL ** -0.5),
        "hg_lower_bounds": nrm(ks[5], (DEPTH + 1, HG_KW), 0.1),
        "hg_norm_g": 1.0 + nrm(ks[6], (DEPTH, HG_V), 0.02),
        "attn_sinks": nrm(ks[7], (DEPTH, ATT_HEADS), 0.5),
        "w_branch_hg": nrm(ks[8], (DEPTH, HG_VW, D_MODEL), HG_VW ** -0.5),
        "w_branch_attn": nrm(ks[9], (DEPTH, ATT_QW, D_MODEL), ATT_QW ** -0.5),
        "w_out": nrm(ks[10], (DEPTH, D_MODEL, D_MODEL), BETA * D_MODEL ** -0.5),
        "ln1_g": 1.0 + nrm(ks[11], (DEPTH, D_MODEL), 0.02),
        "ln1_b": nrm(ks[12], (DEPTH, D_MODEL), 0.02),
        "w_ffn_in": nrm(ks[13], (DEPTH, D_MODEL, 2 * D_FF), D_MODEL ** -0.5),
        "w_ffn_out": nrm(ks[14], (DEPTH, D_FF, D_MODEL), BETA * D_FF ** -0.5),
        "ln2_g": 1.0 + nrm(ks[15], (DEPTH, D_MODEL), 0.02),
        "ln2_b": nrm(ks[16], (DEPTH, D_MODEL), 0.02),
    }


def _fwd_reference(x, meta_tokens, ln_emb_g, ln_emb_b, w_in, hg_lower_bounds, hg_norm_g, attn_sinks,
              w_branch_hg, w_branch_attn, w_out, ln1_g, ln1_b, w_ffn_in, w_ffn_out, ln2_g, ln2_b):
    B, S, D = x.shape
    P = S + BLOCK
    meta = jnp.broadcast_to(meta_tokens.astype(x.dtype)[None], (B, N_META, D))
    h = layer_norm(jnp.concatenate([meta, x], axis=1), ln_emb_g, ln_emb_b)
    h = jnp.pad(h, ((0, 0), (PAD, 0), (0, 0)))
    pos = jnp.arange(P, dtype=jnp.int32) - PAD
    valid = pos >= 0
    lbs = jnp.cumsum(jax.nn.softmax(hg_lower_bounds.astype(jnp.float32), axis=0), axis=0)
    split_idx = [sum(SPLIT_SIZES[:i + 1]) for i in range(len(SPLIT_SIZES) - 1)]
    for l in range(DEPTH):
        proj = h @ w_in[l]
        hq, hf, hi, hg, aq, ak, av, gates = jnp.split(proj, split_idx, axis=-1)
        y_hg = hgrn2_mixer(hq, hf, hi, hg, lbs[l], hg_norm_g[l], valid) @ w_branch_hg[l]
        y_att = swa_sink_attention(aq, ak, av, attn_sinks[l], pos) @ w_branch_attn[l]
        g_hg, g_att = jnp.split(jax.nn.sigmoid(gates), N_BRANCH, axis=-1)
        mix = (g_hg * y_hg + g_att * y_att) @ w_out[l]
        h = layer_norm(ALPHA * h + mix, ln1_g[l], ln1_b[l])
        a, u = jnp.split(h @ w_ffn_in[l], 2, axis=-1)
        h = layer_norm(ALPHA * h + (jax.nn.silu(a) * u) @ w_ffn_out[l], ln2_g[l], ln2_b[l])
    return h[:, BLOCK:]


import jax as _jax
import jax.numpy as _jnp

TWIN_FORMAT = 'train_step'
FWD_PARAMS = ['x', 'meta_tokens', 'ln_emb_g', 'ln_emb_b', 'w_in', 'hg_lower_bounds', 'hg_norm_g', 'attn_sinks', 'w_branch_hg', 'w_branch_attn', 'w_out', 'ln1_g', 'ln1_b', 'w_ffn_in', 'w_ffn_out', 'ln2_g', 'ln2_b']
TWIN_WEIGHTS = ['meta_tokens', 'ln_emb_g', 'ln_emb_b', 'w_in', 'hg_lower_bounds', 'hg_norm_g', 'attn_sinks', 'w_branch_hg', 'w_branch_attn', 'w_out', 'ln1_g', 'ln1_b', 'w_ffn_in', 'w_ffn_out', 'ln2_g', 'ln2_b']
TWIN_DIFF_INPUT = 'x'
TWIN_INPUTS = ['x', 'meta_tokens', 'ln_emb_g', 'ln_emb_b', 'w_in', 'hg_lower_bounds', 'hg_norm_g', 'attn_sinks', 'w_branch_hg', 'w_branch_attn', 'w_out', 'ln1_g', 'ln1_b', 'w_ffn_in', 'w_ffn_out', 'ln2_g', 'ln2_b', 'loss_target', 'm_meta_tokens', 'm_ln_emb_g', 'm_ln_emb_b', 'm_w_in', 'm_hg_lower_bounds', 'm_hg_norm_g', 'm_attn_sinks', 'm_w_branch_hg', 'm_w_branch_attn', 'm_w_out', 'm_ln1_g', 'm_ln1_b', 'm_w_ffn_in', 'm_w_ffn_out', 'm_ln2_g', 'm_ln2_b', 'v_meta_tokens', 'v_ln_emb_g', 'v_ln_emb_b', 'v_w_in', 'v_hg_lower_bounds', 'v_hg_norm_g', 'v_attn_sinks', 'v_w_branch_hg', 'v_w_branch_attn', 'v_w_out', 'v_ln1_g', 'v_ln1_b', 'v_w_ffn_in', 'v_w_ffn_out', 'v_ln2_g', 'v_ln2_b']
TWIN_OUTPUTS = ['loss', 'grad_x', 'grad_meta_tokens', 'grad_ln_emb_g', 'grad_ln_emb_b', 'grad_w_in', 'grad_hg_lower_bounds', 'grad_hg_norm_g', 'grad_attn_sinks', 'grad_w_branch_hg', 'grad_w_branch_attn', 'grad_w_out', 'grad_ln1_g', 'grad_ln1_b', 'grad_w_ffn_in', 'grad_w_ffn_out', 'grad_ln2_g', 'grad_ln2_b', 'delta_meta_tokens', 'delta_ln_emb_g', 'delta_ln_emb_b', 'delta_w_in', 'delta_hg_lower_bounds', 'delta_hg_norm_g', 'delta_attn_sinks', 'delta_w_branch_hg', 'delta_w_branch_attn', 'delta_w_out', 'delta_ln1_g', 'delta_ln1_b', 'delta_w_ffn_in', 'delta_w_ffn_out', 'delta_ln2_g', 'delta_ln2_b', 'new_m_meta_tokens', 'new_m_ln_emb_g', 'new_m_ln_emb_b', 'new_m_w_in', 'new_m_hg_lower_bounds', 'new_m_hg_norm_g', 'new_m_attn_sinks', 'new_m_w_branch_hg', 'new_m_w_branch_attn', 'new_m_w_out', 'new_m_ln1_g', 'new_m_ln1_b', 'new_m_w_ffn_in', 'new_m_w_ffn_out', 'new_m_ln2_g', 'new_m_ln2_b', 'new_v_meta_tokens', 'new_v_ln_emb_g', 'new_v_ln_emb_b', 'new_v_w_in', 'new_v_hg_lower_bounds', 'new_v_hg_norm_g', 'new_v_attn_sinks', 'new_v_w_branch_hg', 'new_v_w_branch_attn', 'new_v_w_out', 'new_v_ln1_g', 'new_v_ln1_b', 'new_v_w_ffn_in', 'new_v_w_ffn_out', 'new_v_ln2_g', 'new_v_ln2_b']
TWIN_LEAF_KINDS = {'loss': 'loss', 'grad_x': 'grad_x', 'grad_meta_tokens': 'grad_w', 'grad_ln_emb_g': 'grad_w', 'grad_ln_emb_b': 'grad_w', 'grad_w_in': 'grad_w', 'grad_hg_lower_bounds': 'grad_w', 'grad_hg_norm_g': 'grad_w', 'grad_attn_sinks': 'grad_w', 'grad_w_branch_hg': 'grad_w', 'grad_w_branch_attn': 'grad_w', 'grad_w_out': 'grad_w', 'grad_ln1_g': 'grad_w', 'grad_ln1_b': 'grad_w', 'grad_w_ffn_in': 'grad_w', 'grad_w_ffn_out': 'grad_w', 'grad_ln2_g': 'grad_w', 'grad_ln2_b': 'grad_w', 'delta_meta_tokens': 'delta_w', 'delta_ln_emb_g': 'delta_w', 'delta_ln_emb_b': 'delta_w', 'delta_w_in': 'delta_w', 'delta_hg_lower_bounds': 'delta_w', 'delta_hg_norm_g': 'delta_w', 'delta_attn_sinks': 'delta_w', 'delta_w_branch_hg': 'delta_w', 'delta_w_branch_attn': 'delta_w', 'delta_w_out': 'delta_w', 'delta_ln1_g': 'delta_w', 'delta_ln1_b': 'delta_w', 'delta_w_ffn_in': 'delta_w', 'delta_w_ffn_out': 'delta_w', 'delta_ln2_g': 'delta_w', 'delta_ln2_b': 'delta_w', 'new_m_meta_tokens': 'new_m', 'new_m_ln_emb_g': 'new_m', 'new_m_ln_emb_b': 'new_m', 'new_m_w_in': 'new_m', 'new_m_hg_lower_bounds': 'new_m', 'new_m_hg_norm_g': 'new_m', 'new_m_attn_sinks': 'new_m', 'new_m_w_branch_hg': 'new_m', 'new_m_w_branch_attn': 'new_m', 'new_m_w_out': 'new_m', 'new_m_ln1_g': 'new_m', 'new_m_ln1_b': 'new_m', 'new_m_w_ffn_in': 'new_m', 'new_m_w_ffn_out': 'new_m', 'new_m_ln2_g': 'new_m', 'new_m_ln2_b': 'new_m', 'new_v_meta_tokens': 'new_v', 'new_v_ln_emb_g': 'new_v', 'new_v_ln_emb_b': 'new_v', 'new_v_w_in': 'new_v', 'new_v_hg_lower_bounds': 'new_v', 'new_v_hg_norm_g': 'new_v', 'new_v_attn_sinks': 'new_v', 'new_v_w_branch_hg': 'new_v', 'new_v_w_branch_attn': 'new_v', 'new_v_w_out': 'new_v', 'new_v_ln1_g': 'new_v', 'new_v_ln1_b': 'new_v', 'new_v_w_ffn_in': 'new_v', 'new_v_w_ffn_out': 'new_v', 'new_v_ln2_g': 'new_v', 'new_v_ln2_b': 'new_v'}


def _forward(args):
    return _fwd_reference(*[args[k] for k in FWD_PARAMS])


def _output_shape():
    out = _jax.eval_shape(lambda: _forward(_fwd_setup_inputs(0)))
    return out.shape, out.dtype

N_MICROBATCH = 1
ADAM_LR = 0.001
ADAM_B1 = 0.9
ADAM_B2 = 0.999
ADAM_EPS = 1e-08
ADAM_WD = 0.01
ADAM_STEP = 10
PER_EXAMPLE_BATCH_AXIS = {'x': 0, 'loss_target': 0}
SHARED_INPUTS = []
_WEIGHT_DTYPES = {'meta_tokens': _jnp.float32, 'ln_emb_g': _jnp.float32, 'ln_emb_b': _jnp.float32, 'w_in': _jnp.float32, 'hg_lower_bounds': _jnp.float32, 'hg_norm_g': _jnp.float32, 'attn_sinks': _jnp.float32, 'w_branch_hg': _jnp.float32, 'w_branch_attn': _jnp.float32, 'w_out': _jnp.float32, 'ln1_g': _jnp.float32, 'ln1_b': _jnp.float32, 'w_ffn_in': _jnp.float32, 'w_ffn_out': _jnp.float32, 'ln2_g': _jnp.float32, 'ln2_b': _jnp.float32}
MOMENT_SCALE = {'meta_tokens': 2.455033e-03, 'ln_emb_g': 1.653291e+00, 'ln_emb_b': 6.143592e-01, 'w_in': 3.008822e-02, 'hg_lower_bounds': 6.597062e-03, 'hg_norm_g': 1.461251e-01, 'attn_sinks': 1.632254e-03, 'w_branch_hg': 4.163392e-02, 'w_branch_attn': 1.212674e-02, 'w_out': 7.277037e-02, 'ln1_g': 1.768036e+00, 'ln1_b': 6.679026e-01, 'w_ffn_in': 4.398203e-02, 'w_ffn_out': 1.206964e-01, 'ln2_g': 6.396434e+01, 'ln2_b': 1.362233e+00}


def _to_microbatches(a, axis):
    t = _jnp.moveaxis(a, axis, 0)
    t = t.reshape((N_MICROBATCH, t.shape[0] // N_MICROBATCH) + t.shape[1:])
    return _jnp.moveaxis(t, 1, axis + 1)


def setup_inputs(seed: int = 0) -> dict:
    inp = _fwd_setup_inputs(seed)
    key = _jax.random.fold_in(_jax.random.key(seed), 7919)
    shape, _ = _output_shape()
    out = dict(inp)
    out["loss_target"] = _jax.random.normal(_jax.random.fold_in(key, 0), shape, _jnp.float32)
    for i, name in enumerate(TWIN_WEIGHTS):
        w = inp[name].astype(_jnp.float32)
        if MOMENT_SCALE is None:
            s = _jnp.sqrt(_jnp.mean(_jnp.square(w)) + 1e-30)
        else:
            s = MOMENT_SCALE[name]
        km, kv = _jax.random.split(_jax.random.fold_in(key, i + 1))
        out[name] = w
        out["m_" + name] = s * _jax.random.normal(km, w.shape, _jnp.float32)
        out["v_" + name] = (s * s) * _jax.random.uniform(kv, w.shape, _jnp.float32, 0.5, 1.5)
    if N_MICROBATCH > 1:
        for name, axis in PER_EXAMPLE_BATCH_AXIS.items():
            out[name] = _to_microbatches(out[name], axis)
    return {'x': out['x'], 'meta_tokens': out['meta_tokens'], 'ln_emb_g': out['ln_emb_g'], 'ln_emb_b': out['ln_emb_b'], 'w_in': out['w_in'], 'hg_lower_bounds': out['hg_lower_bounds'], 'hg_norm_g': out['hg_norm_g'], 'attn_sinks': out['attn_sinks'], 'w_branch_hg': out['w_branch_hg'], 'w_branch_attn': out['w_branch_attn'], 'w_out': out['w_out'], 'ln1_g': out['ln1_g'], 'ln1_b': out['ln1_b'], 'w_ffn_in': out['w_ffn_in'], 'w_ffn_out': out['w_ffn_out'], 'ln2_g': out['ln2_g'], 'ln2_b': out['ln2_b'], 'loss_target': out['loss_target'], 'm_meta_tokens': out['m_meta_tokens'], 'm_ln_emb_g': out['m_ln_emb_g'], 'm_ln_emb_b': out['m_ln_emb_b'], 'm_w_in': out['m_w_in'], 'm_hg_lower_bounds': out['m_hg_lower_bounds'], 'm_hg_norm_g': out['m_hg_norm_g'], 'm_attn_sinks': out['m_attn_sinks'], 'm_w_branch_hg': out['m_w_branch_hg'], 'm_w_branch_attn': out['m_w_branch_attn'], 'm_w_out': out['m_w_out'], 'm_ln1_g': out['m_ln1_g'], 'm_ln1_b': out['m_ln1_b'], 'm_w_ffn_in': out['m_w_ffn_in'], 'm_w_ffn_out': out['m_w_ffn_out'], 'm_ln2_g': out['m_ln2_g'], 'm_ln2_b': out['m_ln2_b'], 'v_meta_tokens': out['v_meta_tokens'], 'v_ln_emb_g': out['v_ln_emb_g'], 'v_ln_emb_b': out['v_ln_emb_b'], 'v_w_in': out['v_w_in'], 'v_hg_lower_bounds': out['v_hg_lower_bounds'], 'v_hg_norm_g': out['v_hg_norm_g'], 'v_attn_sinks': out['v_attn_sinks'], 'v_w_branch_hg': out['v_w_branch_hg'], 'v_w_branch_attn': out['v_w_branch_attn'], 'v_w_out': out['v_w_out'], 'v_ln1_g': out['v_ln1_g'], 'v_ln1_b': out['v_ln1_b'], 'v_w_ffn_in': out['v_w_ffn_in'], 'v_w_ffn_out': out['v_w_ffn_out'], 'v_ln2_g': out['v_ln2_g'], 'v_ln2_b': out['v_ln2_b']}


def _loss(weights, diff, rest, loss_target):
    with _jax.named_scope("forward"):
        args = {**rest, TWIN_DIFF_INPUT: diff, **{k: w.astype(_WEIGHT_DTYPES[k]) for k, w in weights.items()}}
        y = _forward(args)
    with _jax.named_scope("loss_head"):
        err = _jnp.square(y.astype(_jnp.float32) - loss_target)
        return 0.5 * _jnp.sum(_jnp.mean(err, axis=-1)) if err.ndim else 0.5 * err


def _adamw(w, g, m, v):
    m = ADAM_B1 * m + (1.0 - ADAM_B1) * g
    v = ADAM_B2 * v + (1.0 - ADAM_B2) * _jnp.square(g)
    m_hat = m / (1.0 - ADAM_B1 ** ADAM_STEP)
    v_hat = v / (1.0 - ADAM_B2 ** ADAM_STEP)
    delta = -ADAM_LR * (m_hat / (_jnp.sqrt(v_hat) + ADAM_EPS) + ADAM_WD * w)
    return delta, m, v


def reference(x, meta_tokens, ln_emb_g, ln_emb_b, w_in, hg_lower_bounds, hg_norm_g, attn_sinks, w_branch_hg, w_branch_attn, w_out, ln1_g, ln1_b, w_ffn_in, w_ffn_out, ln2_g, ln2_b, loss_target, m_meta_tokens, m_ln_emb_g, m_ln_emb_b, m_w_in, m_hg_lower_bounds, m_hg_norm_g, m_attn_sinks, m_w_branch_hg, m_w_branch_attn, m_w_out, m_ln1_g, m_ln1_b, m_w_ffn_in, m_w_ffn_out, m_ln2_g, m_ln2_b, v_meta_tokens, v_ln_emb_g, v_ln_emb_b, v_w_in, v_hg_lower_bounds, v_hg_norm_g, v_attn_sinks, v_w_branch_hg, v_w_branch_attn, v_w_out, v_ln1_g, v_ln1_b, v_w_ffn_in, v_w_ffn_out, v_ln2_g, v_ln2_b):
    given = dict(x=x, meta_tokens=meta_tokens, ln_emb_g=ln_emb_g, ln_emb_b=ln_emb_b, w_in=w_in, hg_lower_bounds=hg_lower_bounds, hg_norm_g=hg_norm_g, attn_sinks=attn_sinks, w_branch_hg=w_branch_hg, w_branch_attn=w_branch_attn, w_out=w_out, ln1_g=ln1_g, ln1_b=ln1_b, w_ffn_in=w_ffn_in, w_ffn_out=w_ffn_out, ln2_g=ln2_g, ln2_b=ln2_b, loss_target=loss_target, m_meta_tokens=m_meta_tokens, m_ln_emb_g=m_ln_emb_g, m_ln_emb_b=m_ln_emb_b, m_w_in=m_w_in, m_hg_lower_bounds=m_hg_lower_bounds, m_hg_norm_g=m_hg_norm_g, m_attn_sinks=m_attn_sinks, m_w_branch_hg=m_w_branch_hg, m_w_branch_attn=m_w_branch_attn, m_w_out=m_w_out, m_ln1_g=m_ln1_g, m_ln1_b=m_ln1_b, m_w_ffn_in=m_w_ffn_in, m_w_ffn_out=m_w_ffn_out, m_ln2_g=m_ln2_g, m_ln2_b=m_ln2_b, v_meta_tokens=v_meta_tokens, v_ln_emb_g=v_ln_emb_g, v_ln_emb_b=v_ln_emb_b, v_w_in=v_w_in, v_hg_lower_bounds=v_hg_lower_bounds, v_hg_norm_g=v_hg_norm_g, v_attn_sinks=v_attn_sinks, v_w_branch_hg=v_w_branch_hg, v_w_branch_attn=v_w_branch_attn, v_w_out=v_w_out, v_ln1_g=v_ln1_g, v_ln1_b=v_ln1_b, v_w_ffn_in=v_w_ffn_in, v_w_ffn_out=v_w_ffn_out, v_ln2_g=v_ln2_g, v_ln2_b=v_ln2_b)
    weights = {n: given[n] for n in TWIN_WEIGHTS}
    shared = {n: given[n] for n in SHARED_INPUTS}
    per_example = {n: given[n] for n in ['x']}
    grad_fn = _jax.value_and_grad(_loss, argnums=(0, 1))

    def one_microbatch(ex, loss_target):
        ex = dict(ex)
        diff = ex.pop(TWIN_DIFF_INPUT)
        return grad_fn(weights, diff, {**shared, **ex}, loss_target)

    if N_MICROBATCH == 1:
        loss, (grad_w, grad_x) = one_microbatch(per_example, given["loss_target"])
    else:
        def body(carry, xs):
            loss_sum, grad_sum = carry
            l_k, (gw_k, gx_k) = one_microbatch(xs[0], xs[1])
            with _jax.named_scope("update"):
                return (loss_sum + l_k, _jax.tree.map(_jnp.add, grad_sum, gw_k)), gx_k

        init = (_jnp.zeros((), _jnp.float32), _jax.tree.map(_jnp.zeros_like, weights))
        (loss, grad_w), grad_x = _jax.lax.scan(body, init, (per_example, given["loss_target"]))
    with _jax.named_scope("update"):
        delta_w, new_m, new_v = {}, {}, {}
        for n in TWIN_WEIGHTS:
            delta_w[n], new_m[n], new_v[n] = _adamw(weights[n], grad_w[n], given["m_" + n], given["v_" + n])
    return (loss, grad_x, *[grad_w[n] for n in TWIN_WEIGHTS], *[delta_w[n] for n in TWIN_WEIGHTS],
            *[new_m[n] for n in TWIN_WEIGHTS], *[new_v[n] for n in TWIN_WEIGHTS])
```

```python
import functools

import jax
import jax.numpy as jnp
from jax import lax
from jax.experimental import pallas as pl
from jax.experimental.pallas import tpu as pltpu

f32 = jnp.float32
bf16 = jnp.bfloat16

D_MODEL = 1024
BLOCK = 128
N_META = 16
PAD = BLOCK - N_META
HG_HEADS = 4
HG_K = 128
SUB = 16
ATT_HEADS = 8
HEAD_DIM = 64
ATT_QW = ATT_HEADS * HEAD_DIM
D_FF = 2816
EPS = 1e-5
ALPHA = 2.0 ** 0.25
ROPE_THETA = 10000.0
N_A = 2816
N_G = 2048
IN_W = N_A + N_G
N_SHARD = 4
N_DEV = 8

ADAM_LR = 0.001
ADAM_B1 = 0.9
ADAM_B2 = 0.999
ADAM_EPS = 1e-08
ADAM_WD = 0.01
ADAM_STEP = 10

VMEM_LIMIT = 56 * 1024 * 1024
MESH = pl.DeviceIdType.MESH


def _cparams(sem, vmem=VMEM_LIMIT):
    return pltpu.CompilerParams(dimension_semantics=sem, vmem_limit_bytes=vmem)


def _const_spec(shape):
    zeros = (0,) * len(shape)
    return pl.BlockSpec(shape, lambda *_: zeros, pipeline_mode=pl.Buffered(1))


def _dot(a, b, ca, cb):
    return lax.dot_general(a.astype(bf16), b.astype(bf16), (((ca,), (cb,)), ((), ())),
                           preferred_element_type=f32)


@jax.custom_vjp
def mm(a, b):
    return _dot(a, b, 1, 0)


mm.defvjp(lambda a, b: (_dot(a, b, 1, 0), (a, b)),
          lambda r, g: (_dot(g, r[1], 1, 1), _dot(r[0], g, 0, 0)))


@jax.custom_vjp
def mm_nt(a, b):
    return _dot(a, b, 1, 1)


mm_nt.defvjp(lambda a, b: (_dot(a, b, 1, 1), (a, b)),
             lambda r, g: (_dot(g, r[1], 1, 0), _dot(g, r[0], 0, 0)))


@jax.custom_vjp
def mm_tn(a, b):
    return _dot(a, b, 0, 0)


mm_tn.defvjp(lambda a, b: (_dot(a, b, 0, 0), (a, b)),
             lambda r, g: (_dot(r[1], g, 1, 1), _dot(r[0], g, 1, 0)))


@functools.partial(jax.custom_vjp, nondiff_argnums=(1,))
def roll_lanes(x, shift):
    return pltpu.roll(x, shift, 1)


roll_lanes.defvjp(lambda x, shift: (pltpu.roll(x, shift, 1), None),
                  lambda shift, _, g: (pltpu.roll(g, (128 - shift) % 128, 1),))


def _sigmoid(x):
    return 1.0 / (1.0 + jnp.exp(-x))


def _ln_stats(x):
    mu = jnp.mean(x, axis=-1, keepdims=True)
    xc = x - mu
    var = jnp.mean(xc * xc, axis=-1, keepdims=True)
    rs = lax.rsqrt(var + EPS)
    return xc * rs, rs


def _ln_bwd(dy, xh, rs, g):
    dxh = dy * g
    m1 = jnp.mean(dxh, axis=-1, keepdims=True)
    m2 = jnp.mean(dxh * xh, axis=-1, keepdims=True)
    return rs * (dxh - m1 - xh * m2)


def _row_ids(i):
    return i * BLOCK + lax.broadcasted_iota(jnp.int32, (BLOCK, 1), 0)


def emb_inproj(x, metablk, g, b, w_in, nb):
    def body(x_ref, mb_ref, g_ref, b_ref, w_ref, h0_ref, h0b_ref, pa_ref, pg_ref):
        i = pl.program_id(0)
        xb = jnp.where(i == 0, mb_ref[...], x_ref[...])
        xh, _ = _ln_stats(xb)
        y = xh * g_ref[...] + b_ref[...]
        y = jnp.where(_row_ids(i) >= PAD, y, 0.0)
        h0_ref[...] = y
        yb = y.astype(bf16)
        h0b_ref[...] = yb
        pa_ref[...] = jnp.dot(yb, w_ref[:, :N_A], preferred_element_type=f32)
        pg_ref[...] = jnp.dot(yb, w_ref[:, N_A:], preferred_element_type=f32)

    p = nb * BLOCK
    row = lambda n: pl.BlockSpec((BLOCK, n), lambda i: (i, 0))
    return pl.pallas_call(
        body, name="emb_inproj", grid=(nb,),
        in_specs=[pl.BlockSpec((BLOCK, D_MODEL), lambda i: (jnp.maximum(i - 1, 0), 0)),
                  _const_spec((BLOCK, D_MODEL)), _const_spec((1, D_MODEL)), _const_spec((1, D_MODEL)),
                  _const_spec((D_MODEL, IN_W))],
        out_specs=[row(D_MODEL), row(D_MODEL), row(N_A), row(N_G)],
        out_shape=[jax.ShapeDtypeStruct((p, D_MODEL), f32), jax.ShapeDtypeStruct((p, D_MODEL), bf16),
                   jax.ShapeDtypeStruct((p, N_A), f32), jax.ShapeDtypeStruct((p, N_G), f32)],
        compiler_params=_cparams(("parallel",)),
    )(x, metablk, g, b, w_in)


def _hgrn_chunk(valid, st, hq, hf, hi, hg, lbraw, ng):
    lb = _sigmoid(lbraw[0:1] - lbraw[1:2])
    q = hq * _sigmoid(hq)
    fg = lb + (1.0 - lb) * _sigmoid(hf)
    logf = jnp.where(valid, jnp.log(fg), 0.0)
    k = jnp.where(valid, 1.0 - fg, 0.0)
    v = hi
    r = lax.broadcasted_iota(jnp.int32, (BLOCK, BLOCK), 0)
    c = lax.broadcasted_iota(jnp.int32, (BLOCK, BLOCK), 1)
    tril = (c <= r).astype(f32)
    bcum = jnp.dot(tril, logf, precision=lax.Precision.HIGHEST, preferred_element_type=f32)
    blast = bcum[BLOCK - 1:BLOCK]
    o = mm_nt(q * jnp.exp(bcum), st)
    t3 = lax.broadcasted_iota(jnp.int32, (SUB, SUB, HG_K), 0)
    s3 = lax.broadcasted_iota(jnp.int32, (SUB, SUB, HG_K), 1)
    tri3 = s3 <= t3
    pieces = []
    for n in range(BLOCK // SUB):
        r0 = n * SUB
        bs, qs, ks, vs = bcum[r0:r0 + SUB], q[r0:r0 + SUB], k[r0:r0 + SUB], v[r0:r0 + SUB]
        dec = jnp.exp(jnp.where(tri3, bs[:, None, :] - bs[None, :, :], -jnp.inf))
        a_diag = jnp.sum(qs[:, None, :] * ks[None, :, :] * dec, axis=-1)
        od = mm(a_diag, vs)
        if n > 0:
            bref = bcum[r0 - 1:r0]
            q_off = qs * jnp.exp(bs - bref)
            k_off = k * jnp.exp(jnp.where(r < r0, bref - bcum, -jnp.inf))
            od = od + mm(mm_nt(q_off, k_off), v)
        pieces.append(od)
    o = o + jnp.concatenate(pieces, axis=0)
    st_new = st * jnp.exp(blast) + mm_tn(v, k * jnp.exp(blast - bcum))
    on = o * lax.rsqrt(jnp.mean(o * o, axis=-1, keepdims=True) + EPS) * ng
    return st_new, on * (hg * _sigmoid(hg))


def _hgrn_in_specs(rowmap):
    col = lambda off: pl.BlockSpec((BLOCK, HG_K), lambda i, h: (rowmap(i), off + h))
    return [col(0), col(HG_HEADS), col(2 * HG_HEADS), col(3 * HG_HEADS),
            pl.BlockSpec((2, HG_K), lambda i, h: (0, h)), pl.BlockSpec((1, HG_K), lambda i, h: (0, 0))]


def hgrn_fwd(pa, lbraw, ng, nb):
    def body(hq_ref, hf_ref, hi_ref, hg_ref, lb_ref, ng_ref, og_ref, sp_ref, st_ref):
        i, h = pl.program_id(0), pl.program_id(1)

        @pl.when(i == 0)
        def _():
            st_ref[h] = jnp.zeros((HG_K, HG_K), f32)

        st = st_ref[h]
        sp_ref[0, 0] = st
        st_new, out = _hgrn_chunk(_row_ids(i) >= PAD, st, hq_ref[...], hf_ref[...], hi_ref[...], hg_ref[...],
                                  lb_ref[...], ng_ref[...])
        st_ref[h] = st_new
        og_ref[...] = out.astype(bf16)

    p = nb * BLOCK
    return pl.pallas_call(
        body, name="hgrn_fwd", grid=(nb, HG_HEADS),
        in_specs=_hgrn_in_specs(lambda i: i),
        out_specs=[pl.BlockSpec((BLOCK, HG_K), lambda i, h: (i, h)),
                   pl.BlockSpec((1, 1, HG_K, HG_K), lambda i, h: (i, h, 0, 0))],
        out_shape=[jax.ShapeDtypeStruct((p, HG_HEADS * HG_K), bf16),
                   jax.ShapeDtypeStruct((nb, HG_HEADS, HG_K, HG_K), f32)],
        scratch_shapes=[pltpu.VMEM((HG_HEADS, HG_K, HG_K), f32)],
        compiler_params=_cparams(("arbitrary", "arbitrary")),
    )(pa, pa, pa, pa, lbraw, ng)


def hgrn_bwd(pa, lbraw, ng, sprev, dog, nb):
    def body(hq_ref, hf_ref, hi_ref, hg_ref, lb_ref, ng_ref, sp_ref, do_ref,
             dq_ref, df_ref, di_ref, dg_ref, dlb_ref, dng_ref, dst_ref):
        i, h = pl.program_id(0), pl.program_id(1)

        @pl.when(i == 0)
        def _():
            dst_ref[h] = jnp.zeros((HG_K, HG_K), f32)
            dlb_ref[h] = jnp.zeros((2, HG_K), f32)

        @pl.when((i == 0) & (h == 0))
        def _():
            dng_ref[...] = jnp.zeros((1, HG_K), f32)

        valid = _row_ids(nb - 1 - i) >= PAD
        _, vjp = jax.vjp(functools.partial(_hgrn_chunk, valid), sp_ref[0, 0], hq_ref[...], hf_ref[...],
                         hi_ref[...], hg_ref[...], lb_ref[...], ng_ref[...])
        dst, dq, df, di, dg, dlb, dng = vjp((dst_ref[h], do_ref[...]))
        dst_ref[h] = dst
        dq_ref[...] = dq.astype(bf16)
        df_ref[...] = df.astype(bf16)
        di_ref[...] = di.astype(bf16)
        dg_ref[...] = dg.astype(bf16)
        dlb_ref[h] += dlb
        dng_ref[...] += dng

    p = nb * BLOCK
    rev = lambda i: nb - 1 - i
    blk = pl.BlockSpec((BLOCK, HG_K), lambda i, h: (rev(i), h))
    wide = jax.ShapeDtypeStruct((p, HG_HEADS * HG_K), bf16)
    return pl.pallas_call(
        body, name="hgrn_bwd", grid=(nb, HG_HEADS),
        in_specs=_hgrn_in_specs(rev) + [pl.BlockSpec((1, 1, HG_K, HG_K), lambda i, h: (rev(i), h, 0, 0)), blk],
        out_specs=[blk, blk, blk, blk,
                   pl.BlockSpec((HG_HEADS, 2, HG_K), lambda i, h: (0, 0, 0)),
                   pl.BlockSpec((1, HG_K), lambda i, h: (0, 0))],
        out_shape=[wide, wide, wide, wide, jax.ShapeDtypeStruct((HG_HEADS, 2, HG_K), f32),
                   jax.ShapeDtypeStruct((1, HG_K), f32)],
        scratch_shapes=[pltpu.VMEM((HG_HEADS, HG_K, HG_K), f32)],
        compiler_params=_cparams(("arbitrary", "arbitrary")),
    )(pa, pa, pa, pa, lbraw, ng, sprev, dog)


def _rope(x, cos, sin):
    lane = lax.broadcasted_iota(jnp.int32, x.shape, 1)
    rot = jnp.where(lane % HEAD_DIM < HEAD_DIM // 2, -roll_lanes(x, BLOCK - HEAD_DIM // 2),
                    roll_lanes(x, HEAD_DIM // 2))
    return x * cos + rot * sin


def _both_halves(x, g):
    lo = lax.broadcasted_iota(jnp.int32, x.shape, 1) < HEAD_DIM
    sw = roll_lanes(x, HEAD_DIM)
    return jnp.where(lo, x, sw) if g == 0 else jnp.where(lo, sw, x)


def _attn_block(band_ok, meta_ok, tabs, q, kp, kc, vp, vc, km, vm, *sinks):
    cq, sq, cp, sp, cm, sm = tabs
    neg = jnp.finfo(f32).min
    kk = jnp.concatenate([_rope(kp, cp, sp), _rope(kc, cq, sq)], axis=0)
    vv = jnp.concatenate([vp, vc], axis=0)
    kmr = _rope(km, cm, sm)
    lo = lax.broadcasted_iota(jnp.int32, (BLOCK, BLOCK), 1) < HEAD_DIM
    kv = [(_both_halves(kk, g), _both_halves(vv, g), _both_halves(kmr, g), _both_halves(vm, g)) for g in range(2)]
    slabs = []
    for m in range(ATT_HEADS // 2):
        qr = _rope(q[:, m * BLOCK:(m + 1) * BLOCK], cq, sq)
        kk_g, vv_g, km_g, vm_g = kv[m // 2]
        outs = []
        for half in range(2):
            sink = sinks[2 * m + half]
            qm = jnp.where(lo if half == 0 else ~lo, qr, 0.0)
            sb = jnp.where(band_ok, mm_nt(qm, kk_g) * (HEAD_DIM ** -0.5), neg)
            sme = jnp.where(meta_ok, mm_nt(qm, km_g) * (HEAD_DIM ** -0.5), neg)
            mx = jnp.maximum(jnp.maximum(jnp.max(sb, axis=-1, keepdims=True),
                                         jnp.max(sme, axis=-1, keepdims=True)), sink)
            eb, em = jnp.exp(sb - mx), jnp.exp(sme - mx)
            den = jnp.sum(eb, axis=-1, keepdims=True) + jnp.sum(em, axis=-1, keepdims=True) + jnp.exp(sink - mx)
            outs.append(mm(eb / den, vv_g) + mm(em / den, vm_g))
        slabs.append(jnp.where(lo, outs[0], outs[1]))
    return jnp.concatenate(slabs, axis=1)


def _attn_masks(i):
    qpos = _row_ids(i) - PAD
    s = lax.broadcasted_iota(jnp.int32, (1, 2 * BLOCK), 1)
    kpos = jnp.where(s < BLOCK, jnp.where(i > 0, (i - 1) * BLOCK - PAD + s, -1), i * BLOCK - PAD + s - BLOCK)
    band_ok = (kpos >= N_META) & (kpos <= qpos) & (qpos - kpos < BLOCK)
    meta_ok = lax.broadcasted_iota(jnp.int32, (1, N_META), 1) <= qpos
    return band_ok, meta_ok


def _attn_in_specs():
    prev = lambda i: jnp.maximum(i - 1, 0)
    kcol, vcol = N_A // BLOCK - 2, N_A // BLOCK - 1
    blk = lambda rowmap, col: pl.BlockSpec((BLOCK, BLOCK), lambda i: (rowmap(i), col))
    cur, first = (lambda i: i), (lambda i: 0)
    return [pl.BlockSpec((BLOCK, ATT_QW), lambda i: (i, 4)),
            blk(prev, kcol), blk(cur, kcol), blk(prev, vcol), blk(cur, vcol), blk(first, kcol), blk(first, vcol),
            blk(cur, 0), blk(cur, 0), blk(prev, 0), blk(prev, 0), blk(first, 0), blk(first, 0),
            _const_spec((ATT_HEADS, BLOCK))]


def _attn_operands(q_ref, kp_ref, kc_ref, vp_ref, vc_ref, km_ref, vm_ref, cq, sq, cp, sp, cm, sm, sk_ref):
    tabs = (cq[...], sq[...], cp[...], sp[...], cm[PAD:, :], sm[PAD:, :])
    args = (q_ref[...], kp_ref[...], kc_ref[...], vp_ref[...], vc_ref[...], km_ref[PAD:, :], vm_ref[PAD:, :])
    sinks = tuple(sk_ref[j:j + 1, 0:1] for j in range(ATT_HEADS))
    return tabs, args + sinks


def attn_fwd(pa, cos, sin, sinks8, nb):
    def body(*refs):
        o_ref = refs[-1]
        band_ok, meta_ok = _attn_masks(pl.program_id(0))
        tabs, args = _attn_operands(*refs[:-1])
        o_ref[...] = _attn_block(band_ok, meta_ok, tabs, *args).astype(bf16)

    return pl.pallas_call(
        body, name="attn_fwd", grid=(nb,), in_specs=_attn_in_specs(),
        out_specs=pl.BlockSpec((BLOCK, ATT_QW), lambda i: (i, 0)),
        out_shape=jax.ShapeDtypeStruct((nb * BLOCK, ATT_QW), bf16),
        compiler_params=_cparams(("parallel",)),
    )(pa, pa, pa, pa, pa, pa, pa, cos, sin, cos, sin, cos, sin, sinks8)


def attn_bwd(pa, cos, sin, sinks8, do, nb):
    def body(*refs):
        do_ref = refs[14]
        dq_ref, dkc_ref, dkp_ref, dvc_ref, dvp_ref, dkm_ref, dvm_ref, dsk_ref = refs[15:]
        i = pl.program_id(0)

        @pl.when(i == 0)
        def _():
            dkm_ref[...] = jnp.zeros((N_META, BLOCK), f32)
            dvm_ref[...] = jnp.zeros((N_META, BLOCK), f32)
            dsk_ref[...] = jnp.zeros((ATT_HEADS, BLOCK), f32)

        band_ok, meta_ok = _attn_masks(i)
        tabs, args = _attn_operands(*refs[:14])
        _, vjp = jax.vjp(functools.partial(_attn_block, band_ok, meta_ok, tabs), *args)
        grads = vjp(do_ref[...])
        dq_ref[...] = grads[0].astype(bf16)
        dkp_ref[...] = grads[1]
        dkc_ref[...] = grads[2]
        dvp_ref[...] = grads[3]
        dvc_ref[...] = grads[4]
        dkm_ref[...] += grads[5]
        dvm_ref[...] += grads[6]
        for j in range(ATT_HEADS):
            dsk_ref[j:j + 1, :] += jnp.broadcast_to(grads[7 + j], (1, BLOCK))

    p = nb * BLOCK
    row = pl.BlockSpec((BLOCK, BLOCK), lambda i: (i, 0))
    const = lambda r: pl.BlockSpec((r, BLOCK), lambda i: (0, 0))
    part = jax.ShapeDtypeStruct((p, BLOCK), f32)
    return pl.pallas_call(
        body, name="attn_bwd", grid=(nb,),
        in_specs=_attn_in_specs() + [pl.BlockSpec((BLOCK, ATT_QW), lambda i: (i, 0))],
        out_specs=[pl.BlockSpec((BLOCK, ATT_QW), lambda i: (i, 0)), row, row, row, row,
                   const(N_META), const(N_META), const(ATT_HEADS)],
        out_shape=[jax.ShapeDtypeStruct((p, ATT_QW), bf16), part, part, part, part,
                   jax.ShapeDtypeStruct((N_META, BLOCK), f32), jax.ShapeDtypeStruct((N_META, BLOCK), f32),
                   jax.ShapeDtypeStruct((ATT_HEADS, BLOCK), f32)],
        compiler_params=_cparams(("arbitrary",)),
    )(pa, pa, pa, pa, pa, pa, pa, cos, sin, cos, sin, cos, sin, sinks8, do)


def mid_rows(h0, pg, og, oatt, target, wbh, wba, wout, wfi, wfo, ln1g, ln1b, ln2g, ln2b, nb):
    def body(h0_ref, pg_ref, og_ref, oa_ref, t_ref, wbh_ref, wba_ref, wo_ref, wfi_ref, wfo_ref,
             g1_ref, b1_ref, g2_ref, b2_ref,
             dh0_ref, dpg_ref, dog_ref, doa_ref, dyh_ref, dya_ref, mix_ref, dr1_ref, h1b_ref, dau_ref, s_ref, dr2_ref,
             loss_ref, dg1_ref, db1_ref, dg2_ref, db2_ref):
        i = pl.program_id(0)

        @pl.when(i == 0)
        def _():
            loss_ref[...] = jnp.zeros_like(loss_ref)
            for r in (dg1_ref, db1_ref, dg2_ref, db2_ref):
                r[...] = jnp.zeros_like(r)

        g1, b1, g2, b2 = g1_ref[...], b1_ref[...], g2_ref[...], b2_ref[...]
        yh = jnp.dot(og_ref[...], wbh_ref[...], preferred_element_type=f32)
        ya = jnp.dot(oa_ref[...], wba_ref[...], preferred_element_type=f32)
        gh = _sigmoid(pg_ref[:, :D_MODEL])
        ga = _sigmoid(pg_ref[:, D_MODEL:])
        mixin = (gh * yh + ga * ya).astype(bf16)
        mix_ref[...] = mixin
        r1 = ALPHA * h0_ref[...] + jnp.dot(mixin, wo_ref[...], preferred_element_type=f32)
        xh1, rs1 = _ln_stats(r1)
        h1 = xh1 * g1 + b1
        h1b = h1.astype(bf16)
        h1b_ref[...] = h1b
        au = jnp.dot(h1b, wfi_ref[...], preferred_element_type=f32)
        a, u = au[:, :D_FF], au[:, D_FF:]
        sg = _sigmoid(a)
        sa = a * sg
        s = (sa * u).astype(bf16)
        s_ref[...] = s
        r2 = ALPHA * h1 + jnp.dot(s, wfo_ref[...], preferred_element_type=f32)
        xh2, rs2 = _ln_stats(r2)
        diff = jnp.where(i > 0, xh2 * g2 + b2 - t_ref[...], 0.0)
        loss_ref[...] += jnp.sum(diff * diff) * (0.5 / D_MODEL)
        dy = diff * (1.0 / D_MODEL)
        dg2_ref[...] += jnp.sum(dy * xh2, axis=0, keepdims=True)
        db2_ref[...] += jnp.sum(dy, axis=0, keepdims=True)
        dr2 = _ln_bwd(dy, xh2, rs2, g2)
        dr2b = dr2.astype(bf16)
        dr2_ref[...] = dr2b
        ds = _dot(dr2b, wfo_ref[...], 1, 1)
        da = (ds * u) * (sg * (1.0 + a * (1.0 - sg)))
        du = ds * sa
        dau = jnp.concatenate([da, du], axis=1).astype(bf16)
        dau_ref[...] = dau
        dh1 = ALPHA * dr2 + _dot(dau, wfi_ref[...], 1, 1)
        dg1_ref[...] += jnp.sum(dh1 * xh1, axis=0, keepdims=True)
        db1_ref[...] += jnp.sum(dh1, axis=0, keepdims=True)
        dr1 = _ln_bwd(dh1, xh1, rs1, g1)
        dr1b = dr1.astype(bf16)
        dr1_ref[...] = dr1b
        dh0_ref[...] = ALPHA * dr1
        dmix = _dot(dr1b, wo_ref[...], 1, 1)
        dyh = (dmix * gh).astype(bf16)
        dya = (dmix * ga).astype(bf16)
        dyh_ref[...] = dyh
        dya_ref[...] = dya
        dpg_ref[:, :D_MODEL] = (dmix * yh * gh * (1.0 - gh)).astype(bf16)
        dpg_ref[:, D_MODEL:] = (dmix * ya * ga * (1.0 - ga)).astype(bf16)
        dog_ref[...] = _dot(dyh, wbh_ref[...], 1, 1)
        doa_ref[...] = _dot(dya, wba_ref[...], 1, 1)

    p = nb * BLOCK
    row = lambda n: pl.BlockSpec((BLOCK, n), lambda i: (i, 0))
    vec = lambda: pl.BlockSpec((1, D_MODEL), lambda i: (0, 0))
    sds = lambda n, dt: jax.ShapeDtypeStruct((p, n), dt)
    hw = HG_HEADS * HG_K
    return pl.pallas_call(
        body, name="mid_rows", grid=(nb,),
        in_specs=[row(D_MODEL), row(N_G), row(hw), row(ATT_QW),
                  pl.BlockSpec((BLOCK, D_MODEL), lambda i: (jnp.maximum(i - 1, 0), 0)),
                  _const_spec((hw, D_MODEL)), _const_spec((ATT_QW, D_MODEL)), _const_spec((D_MODEL, D_MODEL)),
                  _const_spec((D_MODEL, 2 * D_FF)), _const_spec((D_FF, D_MODEL)),
                  _const_spec((1, D_MODEL)), _const_spec((1, D_MODEL)), _const_spec((1, D_MODEL)),
                  _const_spec((1, D_MODEL))],
        out_specs=[row(D_MODEL), row(N_G), row(hw), row(ATT_QW), row(D_MODEL), row(D_MODEL), row(D_MODEL),
                   row(D_MODEL), row(D_MODEL), row(2 * D_FF), row(D_FF), row(D_MODEL),
                   pl.BlockSpec((1, 1), lambda i: (0, 0)), vec(), vec(), vec(), vec()],
        out_shape=[sds(D_MODEL, f32), sds(N_G, bf16), sds(hw, f32), sds(ATT_QW, f32), sds(D_MODEL, bf16),
                   sds(D_MODEL, bf16), sds(D_MODEL, bf16), sds(D_MODEL, bf16), sds(D_MODEL, bf16),
                   sds(2 * D_FF, bf16), sds(D_FF, bf16), sds(D_MODEL, bf16),
                   jax.ShapeDtypeStruct((1, 1), f32)] + [jax.ShapeDtypeStruct((1, D_MODEL), f32)] * 4,
        compiler_params=_cparams(("arbitrary",)),
    )(h0, pg, og, oatt, target, wbh, wba, wout, wfi, wfo, ln1g, ln1b, ln2g, ln2b)


def inproj_bwd(dh0p, dhq, dhf, dhi, dhg, daq, dkc, dkp, dvc, dvp, dkm, dvm, dpg, w_in, x, metablk, g, b, nb):
    def body(dh0_ref, dq_ref, df_ref, di_ref, dg_ref, daq_ref, dkc_ref, dkp_ref, dvc_ref, dvp_ref, dkm_ref, dvm_ref,
             dpg_ref, w_ref, x_ref, mb_ref, g_ref, b_ref, dproj_ref, dx_ref, dlg_ref, dlb_ref):
        i = pl.program_id(0)

        @pl.when(i == 0)
        def _():
            dlg_ref[...] = jnp.zeros_like(dlg_ref)
            dlb_ref[...] = jnp.zeros_like(dlb_ref)

        zero_pad = jnp.zeros((PAD, BLOCK), f32)
        has_next = i + 1 < nb
        first = i == 0

        def keys(cur_ref, next_ref, meta_ref):
            t = cur_ref[...] + jnp.where(has_next, next_ref[...], 0.0)
            return t + jnp.where(first, jnp.concatenate([zero_pad, meta_ref[...]], axis=0), 0.0)

        dproj = jnp.concatenate(
            [dq_ref[...], df_ref[...], di_ref[...], dg_ref[...], daq_ref[...],
             keys(dkc_ref, dkp_ref, dkm_ref).astype(bf16), keys(dvc_ref, dvp_ref, dvm_ref).astype(bf16),
             dpg_ref[...]], axis=1)
        dproj_ref[...] = dproj
        valid = _row_ids(i) >= PAD
        dh0 = jnp.where(valid, dh0_ref[...] + _dot(dproj, w_ref[...], 1, 1), 0.0)
        xb = jnp.where(first, mb_ref[...], x_ref[...])
        xh, rs = _ln_stats(xb)
        dlg_ref[...] += jnp.sum(dh0 * xh, axis=0, keepdims=True)
        dlb_ref[...] += jnp.sum(dh0, axis=0, keepdims=True)
        dx_ref[...] = jnp.where(valid, _ln_bwd(dh0, xh, rs, g_ref[...]), 0.0)

    p = nb * BLOCK
    row = lambda n: pl.BlockSpec((BLOCK, n), lambda i: (i, 0))
    nxt = pl.BlockSpec((BLOCK, BLOCK), lambda i: (jnp.minimum(i + 1, nb - 1), 0))
    hw = HG_HEADS * HG_K
    vec = lambda: pl.BlockSpec((1, D_MODEL), lambda i: (0, 0))
    return pl.pallas_call(
        body, name="inproj_bwd", grid=(nb,),
        in_specs=[row(D_MODEL), row(hw), row(hw), row(hw), row(hw), row(ATT_QW),
                  row(BLOCK), nxt, row(BLOCK), nxt, _const_spec((N_META, BLOCK)), _const_spec((N_META, BLOCK)),
                  row(N_G), _const_spec((D_MODEL, IN_W)),
                  pl.BlockSpec((BLOCK, D_MODEL), lambda i: (jnp.maximum(i - 1, 0), 0)),
                  _const_spec((BLOCK, D_MODEL)), _const_spec((1, D_MODEL)), _const_spec((1, D_MODEL))],
        out_specs=[row(IN_W), row(D_MODEL), vec(), vec()],
        out_shape=[jax.ShapeDtypeStruct((p, IN_W), bf16), jax.ShapeDtypeStruct((p, D_MODEL), f32),
                   jax.ShapeDtypeStruct((1, D_MODEL), f32), jax.ShapeDtypeStruct((1, D_MODEL), f32)],
        compiler_params=_cparams(("arbitrary",)),
    )(dh0p, dhq, dhf, dhi, dhg, daq, dkc, dkp, dvc, dvp, dkm, dvm, dpg, w_in, x, metablk, g, b)


def wgrad(a, b, name, tk, tn, tp, by_cols):
    p, k = a.shape
    n = b.shape[1]
    nsteps = p // tp

    def body(a_ref, b_ref, o_ref):
        @pl.when(pl.program_id(2) == 0)
        def _():
            o_ref[...] = jnp.zeros_like(o_ref)

        o_ref[0] += _dot(a_ref[...], b_ref[...], 0, 0)

    if by_cols:
        shard_n = n // N_SHARD
        per = shard_n // tn
        out_shape = (N_SHARD, k, shard_n)
        omap = lambda ik, jn, ip: (jn // per, ik, jn % per)
    else:
        out_shape = (1, k, n)
        omap = lambda ik, jn, ip: (0, ik, jn)
    return pl.pallas_call(
        body, name=name, grid=(k // tk, n // tn, nsteps),
        in_specs=[pl.BlockSpec((tp, tk), lambda ik, jn, ip: (ip, ik)),
                  pl.BlockSpec((tp, tn), lambda ik, jn, ip: (ip, jn))],
        out_specs=pl.BlockSpec((1, tk, tn), omap),
        out_shape=jax.ShapeDtypeStruct(out_shape, f32),
        compiler_params=_cparams(("parallel", "parallel", "arbitrary")),
    )(a, b)


def adamw(w, g, m, v, name):
    r, c = w.shape
    tr = r
    for cand in (256, 176, 128):
        if r > cand and r % cand == 0:
            tr = cand
            break

    def body(w_ref, g_ref, m_ref, v_ref, d_ref, mo_ref, vo_ref):
        gg = g_ref[...]
        mn = ADAM_B1 * m_ref[...] + (1.0 - ADAM_B1) * gg
        vn = ADAM_B2 * v_ref[...] + (1.0 - ADAM_B2) * (gg * gg)
        m_hat = mn / (1.0 - ADAM_B1 ** ADAM_STEP)
        v_hat = vn / (1.0 - ADAM_B2 ** ADAM_STEP)
        d_ref[...] = -ADAM_LR * (m_hat / (jnp.sqrt(v_hat) + ADAM_EPS) + ADAM_WD * w_ref[...])
        mo_ref[...] = mn
        vo_ref[...] = vn

    spec = pl.BlockSpec((tr, c), lambda i: (i, 0))
    sds = jax.ShapeDtypeStruct((r, c), f32)
    return pl.pallas_call(
        body, name=name, grid=(r // tr,), in_specs=[spec] * 4, out_specs=[spec] * 3, out_shape=[sds] * 3,
        compiler_params=_cparams(("parallel",)),
    )(w, g, m, v)


def _me():
    return lax.axis_index("x"), lax.axis_index("y"), lax.axis_index("c")


def _chip_peer(x, y, c, k):
    return (x ^ (k >> 1), y ^ (k & 1), c)


ANY = pl.BlockSpec(memory_space=pl.ANY)


def gather_weights(shards):
    n = len(shards)
    out_dtypes = [bf16 if s.size > 16 * 256 else f32 for s in shards]

    def body(*refs):
        ins, outs = refs[:n], refs[n:2 * n]
        stage = refs[2 * n:3 * n]
        send_sems, recv_sems, local_sems = refs[3 * n:]
        x, y, c = _me()
        j = 2 * x + y
        for w in range(n):
            stage[w][...] = ins[w][...].astype(out_dtypes[w])
        sends, locs = [], []
        for w in range(n):
            loc = pltpu.make_async_copy(stage[w], outs[w].at[j], local_sems.at[w])
            loc.start()
            locs.append(loc)
            for k in (1, 2, 3):
                cp = pltpu.make_async_remote_copy(
                    src_ref=stage[w], dst_ref=outs[w].at[j], send_sem=send_sems.at[w, k - 1],
                    recv_sem=recv_sems.at[w, k - 1], device_id=_chip_peer(x, y, c, k), device_id_type=MESH)
                cp.start()
                sends.append(cp)
        for w in range(n):
            for k in (1, 2, 3):
                pltpu.make_async_remote_copy(
                    src_ref=stage[w], dst_ref=outs[w].at[j ^ k], send_sem=send_sems.at[w, k - 1],
                    recv_sem=recv_sems.at[w, k - 1], device_id=_chip_peer(x, y, c, k), device_id_type=MESH).wait_recv()
        for cp in sends:
            cp.wait_send()
        for loc in locs:
            loc.wait()

    return pl.pallas_call(
        body, name="gather_weights",
        in_specs=[pl.BlockSpec(memory_space=pltpu.VMEM)] * n, out_specs=[ANY] * n,
        out_shape=[jax.ShapeDtypeStruct((N_SHARD,) + s.shape, dt) for s, dt in zip(shards, out_dtypes)],
        scratch_shapes=[pltpu.VMEM(s.shape, dt) for s, dt in zip(shards, out_dtypes)]
        + [pltpu.SemaphoreType.DMA((n, 3)), pltpu.SemaphoreType.DMA((n, 3)), pltpu.SemaphoreType.DMA((n,))],
        compiler_params=pltpu.CompilerParams(vmem_limit_bytes=VMEM_LIMIT),
    )(*shards)


def pair_exchange_halves(grads, small):
    n = len(grads)

    def body(*refs):
        ins, small_ref = refs[:n], refs[n]
        outs, gath = refs[n + 1:2 * n + 1], refs[2 * n + 1]
        send_sems, recv_sems, s_send, s_recv, local_sem = refs[2 * n + 2:]
        x, y, c = _me()
        me = 4 * x + 2 * y + c
        sends = []
        for w in range(n):
            half = ins[w].shape[1] // 2
            cp = pltpu.make_async_remote_copy(
                src_ref=ins[w].at[:, pl.ds((1 - c) * half, half), :], dst_ref=outs[w],
                send_sem=send_sems.at[w], recv_sem=recv_sems.at[w], device_id=(x, y, 1 - c), device_id_type=MESH)
            cp.start()
            sends.append(cp)
        loc = pltpu.make_async_copy(small_ref, gath.at[me], local_sem)
        loc.start()
        for k in range(1, N_DEV):
            cp = pltpu.make_async_remote_copy(
                src_ref=small_ref, dst_ref=gath.at[me], send_sem=s_send.at[k - 1], recv_sem=s_recv.at[k - 1],
                device_id=(x ^ (k >> 2), y ^ ((k >> 1) & 1), c ^ (k & 1)), device_id_type=MESH)
            cp.start()
            sends.append(cp)
        for w in range(n):
            half = ins[w].shape[1] // 2
            pltpu.make_async_remote_copy(
                src_ref=ins[w].at[:, pl.ds(0, half), :], dst_ref=outs[w], send_sem=send_sems.at[w],
                recv_sem=recv_sems.at[w], device_id=(x, y, 1 - c), device_id_type=MESH).wait_recv()
        for k in range(1, N_DEV):
            pltpu.make_async_remote_copy(
                src_ref=small_ref, dst_ref=gath.at[me ^ k], send_sem=s_send.at[k - 1], recv_sem=s_recv.at[k - 1],
                device_id=(x ^ (k >> 2), y ^ ((k >> 1) & 1), c ^ (k & 1)), device_id_type=MESH).wait_recv()
        for cp in sends:
            cp.wait_send()
        loc.wait()

    return pl.pallas_call(
        body, name="pair_exchange_halves", in_specs=[ANY] * (n + 1), out_specs=[ANY] * (n + 1),
        out_shape=[jax.ShapeDtypeStruct((g.shape[0], g.shape[1] // 2, g.shape[2]), f32) for g in grads]
        + [jax.ShapeDtypeStruct((N_DEV,) + small.shape, f32)],
        scratch_shapes=[pltpu.SemaphoreType.DMA((n,)), pltpu.SemaphoreType.DMA((n,)),
                        pltpu.SemaphoreType.DMA((N_DEV - 1,)), pltpu.SemaphoreType.DMA((N_DEV - 1,)),
                        pltpu.SemaphoreType.DMA],
    )(*grads, small)


def chip_exchange(sums):
    n = len(sums)

    def body(*refs):
        ins, outs = refs[:n], refs[n:2 * n]
        send_sems, recv_sems, local_sems = refs[2 * n:]
        x, y, c = _me()
        j = 2 * x + y
        sends, locs = [], []
        for w in range(n):
            loc = pltpu.make_async_copy(ins[w].at[j], outs[w].at[0], local_sems.at[w])
            loc.start()
            locs.append(loc)
            for k in (1, 2, 3):
                cp = pltpu.make_async_remote_copy(
                    src_ref=ins[w].at[j ^ k], dst_ref=outs[w].at[k], send_sem=send_sems.at[w, k - 1],
                    recv_sem=recv_sems.at[w, k - 1], device_id=_chip_peer(x, y, c, k), device_id_type=MESH)
                cp.start()
                sends.append(cp)
        for w in range(n):
            for k in (1, 2, 3):
                pltpu.make_async_remote_copy(
                    src_ref=ins[w].at[0], dst_ref=outs[w].at[k], send_sem=send_sems.at[w, k - 1],
                    recv_sem=recv_sems.at[w, k - 1], device_id=_chip_peer(x, y, c, k), device_id_type=MESH).wait_recv()
        for cp in sends:
            cp.wait_send()
        for loc in locs:
            loc.wait()

    return pl.pallas_call(
        body, name="chip_exchange", in_specs=[ANY] * n, out_specs=[ANY] * n,
        out_shape=[jax.ShapeDtypeStruct(s.shape, f32) for s in sums],
        scratch_shapes=[pltpu.SemaphoreType.DMA((n, 3)), pltpu.SemaphoreType.DMA((n, 3)),
                        pltpu.SemaphoreType.DMA((n,))],
    )(*sums)


def pair_exchange_results(halves):
    n = len(halves)

    def body(*refs):
        ins, outs = refs[:n], refs[n:2 * n]
        send_sems, recv_sems, local_sems = refs[2 * n:]
        x, y, c = _me()
        sends, locs = [], []
        for w in range(n):
            half = ins[w].shape[0]
            mine = outs[w].at[pl.ds(c * half, half), :]
            loc = pltpu.make_async_copy(ins[w], mine, local_sems.at[w])
            loc.start()
            locs.append(loc)
            cp = pltpu.make_async_remote_copy(
                src_ref=ins[w], dst_ref=mine, send_sem=send_sems.at[w], recv_sem=recv_sems.at[w],
                device_id=(x, y, 1 - c), device_id_type=MESH)
            cp.start()
            sends.append(cp)
        for w in range(n):
            half = ins[w].shape[0]
            pltpu.make_async_remote_copy(
                src_ref=ins[w], dst_ref=outs[w].at[pl.ds((1 - c) * half, half), :], send_sem=send_sems.at[w],
                recv_sem=recv_sems.at[w], device_id=(x, y, 1 - c), device_id_type=MESH).wait_recv()
        for cp in sends:
            cp.wait_send()
        for loc in locs:
            loc.wait()

    return pl.pallas_call(
        body, name="pair_exchange_results", in_specs=[ANY] * n, out_specs=[ANY] * n,
        out_shape=[jax.ShapeDtypeStruct((2 * h.shape[0], h.shape[1]), f32) for h in halves],
        scratch_shapes=[pltpu.SemaphoreType.DMA((n,)), pltpu.SemaphoreType.DMA((n,)), pltpu.SemaphoreType.DMA((n,))],
    )(*halves)


def add_pair(grad, other, c_idx, name):
    _, r, c = grad.shape
    half = r // 2
    tr = half // 2 if (half // 2) % 8 == 0 else half
    per = half // tr

    def body(c_ref, g_ref, o_ref, out_ref):
        out_ref[...] = g_ref[...] + o_ref[...]

    return pl.pallas_call(
        body, name=name,
        grid_spec=pltpu.PrefetchScalarGridSpec(
            num_scalar_prefetch=1, grid=(N_SHARD, per),
            in_specs=[pl.BlockSpec((1, tr, c), lambda j, t, cr: (j, cr[0] * per + t, 0)),
                      pl.BlockSpec((1, tr, c), lambda j, t, cr: (j, t, 0))],
            out_specs=pl.BlockSpec((1, tr, c), lambda j, t, cr: (j, t, 0))),
        out_shape=jax.ShapeDtypeStruct((N_SHARD, half, c), f32),
        compiler_params=_cparams(("parallel", "parallel")),
    )(c_idx, grad, other)


def add_four(parts, name):
    _, half, c = parts.shape
    tr = half // 2 if (half // 2) % 8 == 0 else half

    def body(p_ref, out_ref):
        out_ref[...] = ((p_ref[0] + p_ref[1]) + p_ref[2]) + p_ref[3]

    return pl.pallas_call(
        body, name=name, grid=(half // tr,),
        in_specs=[pl.BlockSpec((N_SHARD, tr, c), lambda t: (0, t, 0))],
        out_specs=pl.BlockSpec((tr, c), lambda t: (t, 0)),
        out_shape=jax.ShapeDtypeStruct((half, c), f32),
        compiler_params=_cparams(("parallel",)),
    )(parts)


def sum_devices(gathered):
    def body(g_ref, out_ref):
        acc = g_ref[0]
        for d in range(1, N_DEV):
            acc = acc + g_ref[d]
        out_ref[...] = acc

    return pl.pallas_call(body, name="sum_devices", out_shape=jax.ShapeDtypeStruct(gathered.shape[1:], f32))(gathered)


def _rows128(a, rows):
    flat = a.reshape(-1, BLOCK) if a.size % BLOCK == 0 else jnp.pad(a.reshape(1, -1), ((0, 0), (0, BLOCK - a.size)))
    return jnp.pad(flat, ((0, rows - flat.shape[0]), (0, 0)))


def kernel(x, meta_tokens, ln_emb_g, ln_emb_b, w_in, hg_lower_bounds, hg_norm_g, attn_sinks, w_branch_hg, w_branch_attn, w_out, ln1_g, ln1_b, w_ffn_in, w_ffn_out, ln2_g, ln2_b, loss_target, m_meta_tokens, m_ln_emb_g, m_ln_emb_b, m_w_in, m_hg_lower_bounds, m_hg_norm_g, m_attn_sinks, m_w_branch_hg, m_w_branch_attn, m_w_out, m_ln1_g, m_ln1_b, m_w_ffn_in, m_w_ffn_out, m_ln2_g, m_ln2_b, v_meta_tokens, v_ln_emb_g, v_ln_emb_b, v_w_in, v_hg_lower_bounds, v_hg_norm_g, v_attn_sinks, v_w_branch_hg, v_w_branch_attn, v_w_out, v_ln1_g, v_ln1_b, v_w_ffn_in, v_w_ffn_out, v_ln2_g, v_ln2_b):
    seq = x.shape[1]
    nb = seq // BLOCK + 1
    xs = x[0]
    ts = loss_target[0]
    ix, iy, ic = _me()
    shard = 2 * ix + iy
    vec = lambda a: a.reshape(1, D_MODEL)

    big = [w_in[0], w_branch_hg[0], w_branch_attn[0], w_out[0], w_ffn_in[0], w_ffn_out[0]]
    g_in, g_bh, g_ba, g_out, g_fi, g_fo, g_meta = gather_weights(big + [meta_tokens])
    by_cols = lambda g: g.transpose(1, 0, 2).reshape(g.shape[1], N_SHARD * g.shape[2])
    wf_in, wf_bh, wf_ba, wf_fi = by_cols(g_in), by_cols(g_bh), by_cols(g_ba), by_cols(g_fi)
    wf_out = g_out.reshape(D_MODEL, D_MODEL)
    wf_fo = g_fo.reshape(D_FF, D_MODEL)
    metablk = jnp.pad(by_cols(g_meta), ((PAD, 0), (0, 0)))

    pos = jnp.arange(nb * BLOCK, dtype=jnp.int32) - PAD
    half = HEAD_DIM // 2
    inv = ROPE_THETA ** (-jnp.arange(half, dtype=f32) / half)
    ang = pos.astype(f32)[:, None] * inv[None, :]
    cos = jnp.tile(jnp.cos(ang), (1, BLOCK // half))
    sin = jnp.tile(jnp.sin(ang), (1, BLOCK // half))
    sinks8 = jnp.broadcast_to(attn_sinks.reshape(ATT_HEADS, 1), (ATT_HEADS, BLOCK))
    ng = hg_norm_g.reshape(1, HG_K)

    h0, h0b, pa, pg = emb_inproj(xs, metablk, vec(ln_emb_g), vec(ln_emb_b), wf_in, nb)
    og, sprev = hgrn_fwd(pa, hg_lower_bounds, ng, nb)
    oatt = attn_fwd(pa, cos, sin, sinks8, nb)
    (dh0p, dpg, dog, doa, dyh, dya, mixin, dr1, h1b, dau, sact, dr2,
     loss_part, dg1, db1, dg2, db2) = mid_rows(h0, pg, og, oatt, ts, wf_bh, wf_ba, wf_out, wf_fi, wf_fo,
                                              ln1_g, ln1_b, ln2_g, ln2_b, nb)
    dhq, dhf, dhi, dhg, dlb4, dng = hgrn_bwd(pa, hg_lower_bounds, ng, sprev, dog, nb)
    daq, dkc, dkp, dvc, dvp, dkm, dvm, dsk = attn_bwd(pa, cos, sin, sinks8, doa, nb)
    dproj, dxp, dlg, dlb = inproj_bwd(dh0p, dhq, dhf, dhi, dhg, daq, dkc, dkp, dvc, dvp, dkm, dvm, dpg,
                                      wf_in, xs, metablk, vec(ln_emb_g), vec(ln_emb_b), nb)

    tp = BLOCK * (5 if nb % 5 == 0 else 1)
    gw_in = wgrad(h0b, dproj, "wgrad_in", D_MODEL, IN_W // 2, tp, False)
    gw_in = gw_in.reshape(D_MODEL, N_SHARD, IN_W // N_SHARD).transpose(1, 0, 2)
    gw_bh = wgrad(og, dyh, "wgrad_bh", 512, 256, tp, True)
    gw_ba = wgrad(oatt, dya, "wgrad_ba", 512, 256, tp, True)
    gw_out = wgrad(mixin, dr1, "wgrad_out", D_MODEL, D_MODEL, tp, False).reshape(N_SHARD, -1, D_MODEL)
    gw_fi = wgrad(h1b, dau, "wgrad_fi", D_MODEL, 2 * D_FF // N_SHARD, tp, True)
    gw_fo = wgrad(sact, dr2, "wgrad_fo", D_FF // 2, D_MODEL, tp, False).reshape(N_SHARD, -1, D_MODEL)
    grads = [gw_in, gw_bh, gw_ba, gw_out, gw_fi, gw_fo]

    parts = [(dlg, 8), (dlb, 8), (dlb4.transpose(1, 0, 2), 8), (dng, 8), (dsk[:, 0], 8),
             (dg1, 8), (db1, 8), (dg2, 8), (db2, 8), (dxp[PAD:BLOCK], BLOCK)]
    small = jnp.concatenate([_rows128(a, r) for a, r in parts], axis=0)

    c_idx = jnp.reshape(ic, (1,)).astype(jnp.int32)
    *others, gathered = pair_exchange_halves(grads, small)
    sums = [add_pair(g, o, c_idx, "add_pair_%d" % n) for n, (g, o) in enumerate(zip(grads, others))]
    quads = chip_exchange(sums)
    halves = [add_four(q, "add_four_%d" % n) for n, q in enumerate(quads)]
    red = pair_exchange_results(halves)
    small_sum = sum_devices(gathered)

    offs, acc = [], 0
    for _, r in parts:
        offs.append(acc)
        acc += r
    take = lambda n, size: small_sum[offs[n]:offs[n] + parts[n][1]].reshape(-1)[:size]
    g_meta_full = take(9, N_META * D_MODEL).reshape(N_META, D_MODEL)
    g_small = {
        "meta_tokens": lax.dynamic_slice_in_dim(g_meta_full, shard * (D_MODEL // N_SHARD), D_MODEL // N_SHARD, axis=1),
        "ln_emb_g": take(0, D_MODEL), "ln_emb_b": take(1, D_MODEL),
        "hg_lower_bounds": take(2, 2 * HG_HEADS * HG_K).reshape(2, HG_HEADS * HG_K),
        "hg_norm_g": take(3, HG_K).reshape(1, HG_K), "attn_sinks": take(4, ATT_HEADS).reshape(1, ATT_HEADS),
        "ln1_g": take(5, D_MODEL).reshape(1, D_MODEL), "ln1_b": take(6, D_MODEL).reshape(1, D_MODEL),
        "ln2_g": take(7, D_MODEL).reshape(1, D_MODEL), "ln2_b": take(8, D_MODEL).reshape(1, D_MODEL),
    }
    g_big = {"w_in": red[0], "w_branch_hg": red[1], "w_branch_attn": red[2], "w_out": red[3],
             "w_ffn_in": red[4], "w_ffn_out": red[5]}

    names = ["meta_tokens", "ln_emb_g", "ln_emb_b", "w_in", "hg_lower_bounds", "hg_norm_g", "attn_sinks",
             "w_branch_hg", "w_branch_attn", "w_out", "ln1_g", "ln1_b", "w_ffn_in", "w_ffn_out", "ln2_g", "ln2_b"]
    given = dict(
        meta_tokens=(meta_tokens, m_meta_tokens, v_meta_tokens), ln_emb_g=(ln_emb_g, m_ln_emb_g, v_ln_emb_g),
        ln_emb_b=(ln_emb_b, m_ln_emb_b, v_ln_emb_b), w_in=(w_in, m_w_in, v_w_in),
        hg_lower_bounds=(hg_lower_bounds, m_hg_lower_bounds, v_hg_lower_bounds),
        hg_norm_g=(hg_norm_g, m_hg_norm_g, v_hg_norm_g), attn_sinks=(attn_sinks, m_attn_sinks, v_attn_sinks),
        w_branch_hg=(w_branch_hg, m_w_branch_hg, v_w_branch_hg),
        w_branch_attn=(w_branch_attn, m_w_branch_attn, v_w_branch_attn), w_out=(w_out, m_w_out, v_w_out),
        ln1_g=(ln1_g, m_ln1_g, v_ln1_g), ln1_b=(ln1_b, m_ln1_b, v_ln1_b), w_ffn_in=(w_ffn_in, m_w_ffn_in, v_w_ffn_in),
        w_ffn_out=(w_ffn_out, m_w_ffn_out, v_w_ffn_out), ln2_g=(ln2_g, m_ln2_g, v_ln2_g), ln2_b=(ln2_b, m_ln2_b, v_ln2_b))
    out_g, out_d, out_m, out_v = [], [], [], []
    for nm in names:
        w, m, v = given[nm]
        shape = w.shape
        g = g_big[nm] if nm in g_big else g_small[nm]
        two_d = (lambda a: a.reshape(8, BLOCK)) if w.ndim == 1 else (lambda a: a.reshape(a.shape[-2], a.shape[-1]))
        d, mn, vn = adamw(two_d(w), two_d(g), two_d(m), two_d(v), "adamw_" + nm)
        out_g.append(g.reshape(shape))
        out_d.append(d.reshape(shape))
        out_m.append(mn.reshape(shape))
        out_v.append(vn.reshape(shape))

    loss = lax.psum(loss_part[0, 0], ("x", "y", "c"))
    grad_x = dxp[BLOCK:].reshape(x.shape)
    return (loss, grad_x, *out_g, *out_d, *out_m, *out_v)
```

```python
import functools

import jax
import jax.numpy as jnp
from jax import lax
from jax.experimental import pallas as pl
from jax.experimental.pallas import tpu as pltpu

f32 = jnp.float32
bf16 = jnp.bfloat16

D_MODEL = 1024
BLOCK = 128
N_META = 16
PAD = BLOCK - N_META
HG_HEADS = 4
HG_K = 128
SUB = 16
ATT_HEADS = 8
HEAD_DIM = 64
ATT_QW = ATT_HEADS * HEAD_DIM
D_FF = 2816
EPS = 1e-5
ALPHA = 2.0 ** 0.25
ROPE_THETA = 10000.0
N_A = 2816
N_G = 2048
IN_W = N_A + N_G
N_SHARD = 4
N_DEV = 8

ADAM_LR = 0.001
ADAM_B1 = 0.9
ADAM_B2 = 0.999
ADAM_EPS = 1e-08
ADAM_WD = 0.01
ADAM_STEP = 10

VMEM_LIMIT = 56 * 1024 * 1024
MESH = pl.DeviceIdType.MESH


def _cparams(sem, vmem=VMEM_LIMIT):
    return pltpu.CompilerParams(dimension_semantics=sem, vmem_limit_bytes=vmem)


def _const_spec(shape):
    zeros = (0,) * len(shape)
    return pl.BlockSpec(shape, lambda *_: zeros, pipeline_mode=pl.Buffered(1))


def _dot(a, b, ca, cb):
    return lax.dot_general(a.astype(bf16), b.astype(bf16), (((ca,), (cb,)), ((), ())),
                           preferred_element_type=f32)


@jax.custom_vjp
def mm(a, b):
    return _dot(a, b, 1, 0)


mm.defvjp(lambda a, b: (_dot(a, b, 1, 0), (a, b)),
          lambda r, g: (_dot(g, r[1], 1, 1), _dot(r[0], g, 0, 0)))


@jax.custom_vjp
def mm_nt(a, b):
    return _dot(a, b, 1, 1)


mm_nt.defvjp(lambda a, b: (_dot(a, b, 1, 1), (a, b)),
             lambda r, g: (_dot(g, r[1], 1, 0), _dot(g, r[0], 0, 0)))


@jax.custom_vjp
def mm_tn(a, b):
    return _dot(a, b, 0, 0)


mm_tn.defvjp(lambda a, b: (_dot(a, b, 0, 0), (a, b)),
             lambda r, g: (_dot(r[1], g, 1, 1), _dot(r[0], g, 1, 0)))


@functools.partial(jax.custom_vjp, nondiff_argnums=(1,))
def roll_lanes(x, shift):
    return pltpu.roll(x, shift, 1)


roll_lanes.defvjp(lambda x, shift: (pltpu.roll(x, shift, 1), None),
                  lambda shift, _, g: (pltpu.roll(g, (128 - shift) % 128, 1),))


def _sigmoid(x):
    return 1.0 / (1.0 + jnp.exp(-x))


def _ln_stats(x):
    mu = jnp.mean(x, axis=-1, keepdims=True)
    xc = x - mu
    var = jnp.mean(xc * xc, axis=-1, keepdims=True)
    rs = lax.rsqrt(var + EPS)
    return xc * rs, rs


def _ln_bwd(dy, xh, rs, g):
    dxh = dy * g
    m1 = jnp.mean(dxh, axis=-1, keepdims=True)
    m2 = jnp.mean(dxh * xh, axis=-1, keepdims=True)
    return rs * (dxh - m1 - xh * m2)


def _row_ids(i):
    return i * BLOCK + lax.broadcasted_iota(jnp.int32, (BLOCK, 1), 0)


def emb_inproj(x, metablk, g, b, w_in, nb):
    def body(x_ref, mb_ref, g_ref, b_ref, w_ref, h0_ref, h0b_ref, pa_ref, pg_ref):
        i = pl.program_id(0)
        xb = jnp.where(i == 0, mb_ref[...], x_ref[...])
        xh, _ = _ln_stats(xb)
        y = xh * g_ref[...] + b_ref[...]
        y = jnp.where(_row_ids(i) >= PAD, y, 0.0)
        h0_ref[...] = y
        yb = y.astype(bf16)
        h0b_ref[...] = yb
        pa_ref[...] = jnp.dot(yb, w_ref[:, :N_A], preferred_element_type=f32)
        pg_ref[...] = jnp.dot(yb, w_ref[:, N_A:], preferred_element_type=f32)

    p = nb * BLOCK
    row = lambda n: pl.BlockSpec((BLOCK, n), lambda i: (i, 0))
    return pl.pallas_call(
        body, name="emb_inproj", grid=(nb,),
        in_specs=[pl.BlockSpec((BLOCK, D_MODEL), lambda i: (jnp.maximum(i - 1, 0), 0)),
                  _const_spec((BLOCK, D_MODEL)), _const_spec((1, D_MODEL)), _const_spec((1, D_MODEL)),
                  _const_spec((D_MODEL, IN_W))],
        out_specs=[row(D_MODEL), row(D_MODEL), row(N_A), row(N_G)],
        out_shape=[jax.ShapeDtypeStruct((p, D_MODEL), f32), jax.ShapeDtypeStruct((p, D_MODEL), bf16),
                   jax.ShapeDtypeStruct((p, N_A), f32), jax.ShapeDtypeStruct((p, N_G), f32)],
        compiler_params=_cparams(("parallel",)),
    )(x, metablk, g, b, w_in)


def _hgrn_chunk(valid, st, hq, hf, hi, hg, lbraw, ng):
    lb = _sigmoid(lbraw[0:1] - lbraw[1:2])
    q = hq * _sigmoid(hq)
    fg = lb + (1.0 - lb) * _sigmoid(hf)
    logf = jnp.where(valid, jnp.log(fg), 0.0)
    k = jnp.where(valid, 1.0 - fg, 0.0)
    v = hi
    r = lax.broadcasted_iota(jnp.int32, (BLOCK, BLOCK), 0)
    c = lax.broadcasted_iota(jnp.int32, (BLOCK, BLOCK), 1)
    tril = (c <= r).astype(f32)
    bcum = jnp.dot(tril, logf, precision=lax.Precision.HIGHEST, preferred_element_type=f32)
    blast = bcum[BLOCK - 1:BLOCK]
    o = mm_nt(q * jnp.exp(bcum), st)
    t3 = lax.broadcasted_iota(jnp.int32, (SUB, SUB, HG_K), 0)
    s3 = lax.broadcasted_iota(jnp.int32, (SUB, SUB, HG_K), 1)
    tri3 = s3 <= t3
    pieces = []
    for n in range(BLOCK // SUB):
        r0 = n * SUB
        bs, qs, ks, vs = bcum[r0:r0 + SUB], q[r0:r0 + SUB], k[r0:r0 + SUB], v[r0:r0 + SUB]
        dec = jnp.exp(jnp.where(tri3, bs[:, None, :] - bs[None, :, :], -jnp.inf))
        a_diag = jnp.sum(qs[:, None, :] * ks[None, :, :] * dec, axis=-1)
        od = mm(a_diag, vs)
        if n > 0:
            bref = bcum[r0 - 1:r0]
            q_off = qs * jnp.exp(bs - bref)
            k_off = k * jnp.exp(jnp.where(r < r0, bref - bcum, -jnp.inf))
            od = od + mm(mm_nt(q_off, k_off), v)
        pieces.append(od)
    o = o + jnp.concatenate(pieces, axis=0)
    st_new = st * jnp.exp(blast) + mm_tn(v, k * jnp.exp(blast - bcum))
    on = o * lax.rsqrt(jnp.mean(o * o, axis=-1, keepdims=True) + EPS) * ng
    return st_new, on * (hg * _sigmoid(hg))


def _hgrn_in_specs(rowmap):
    wide = lambda col: pl.BlockSpec((BLOCK, HG_HEADS * HG_K), lambda i: (rowmap(i), col))
    return [wide(0), wide(1), wide(2), wide(3), _const_spec((2, HG_HEADS * HG_K)), _const_spec((1, HG_K))]


def _head(ref, h):
    return ref[:, h * HG_K:(h + 1) * HG_K]


def hgrn_fwd(pa, lbraw, ng, nb):
    def body(hq_ref, hf_ref, hi_ref, hg_ref, lb_ref, ng_ref, og_ref, sp_ref, st_ref):
        i = pl.program_id(0)

        @pl.when(i == 0)
        def _():
            st_ref[...] = jnp.zeros_like(st_ref)

        valid = _row_ids(i) >= PAD
        for h in range(HG_HEADS):
            st = st_ref[h]
            sp_ref[0, h] = st
            st_new, out = _hgrn_chunk(valid, st, _head(hq_ref, h), _head(hf_ref, h), _head(hi_ref, h),
                                      _head(hg_ref, h), _head(lb_ref, h), ng_ref[...])
            st_ref[h] = st_new
            og_ref[:, h * HG_K:(h + 1) * HG_K] = out.astype(bf16)

    p = nb * BLOCK
    return pl.pallas_call(
        body, name="hgrn_fwd", grid=(nb,),
        in_specs=_hgrn_in_specs(lambda i: i),
        out_specs=[pl.BlockSpec((BLOCK, HG_HEADS * HG_K), lambda i: (i, 0)),
                   pl.BlockSpec((1, HG_HEADS, HG_K, HG_K), lambda i: (i, 0, 0, 0))],
        out_shape=[jax.ShapeDtypeStruct((p, HG_HEADS * HG_K), bf16),
                   jax.ShapeDtypeStruct((nb, HG_HEADS, HG_K, HG_K), f32)],
        scratch_shapes=[pltpu.VMEM((HG_HEADS, HG_K, HG_K), f32)],
        compiler_params=_cparams(("arbitrary",)),
    )(pa, pa, pa, pa, lbraw, ng)


def hgrn_bwd(pa, lbraw, ng, sprev, dog, nb):
    def body(hq_ref, hf_ref, hi_ref, hg_ref, lb_ref, ng_ref, sp_ref, do_ref,
             dq_ref, df_ref, di_ref, dg_ref, dlb_ref, dng_ref, dst_ref):
        i = pl.program_id(0)

        @pl.when(i == 0)
        def _():
            dst_ref[...] = jnp.zeros_like(dst_ref)
            dlb_ref[...] = jnp.zeros_like(dlb_ref)
            dng_ref[...] = jnp.zeros_like(dng_ref)

        valid = _row_ids(nb - 1 - i) >= PAD
        dng_sum = jnp.zeros((1, HG_K), f32)
        for h in range(HG_HEADS):
            cols = slice(h * HG_K, (h + 1) * HG_K)
            _, vjp = jax.vjp(functools.partial(_hgrn_chunk, valid), sp_ref[0, h], _head(hq_ref, h), _head(hf_ref, h),
                             _head(hi_ref, h), _head(hg_ref, h), _head(lb_ref, h), ng_ref[...])
            dst, dq, df, di, dg, dlb, dng = vjp((dst_ref[h], _head(do_ref, h)))
            dst_ref[h] = dst
            dq_ref[:, cols] = dq.astype(bf16)
            df_ref[:, cols] = df.astype(bf16)
            di_ref[:, cols] = di.astype(bf16)
            dg_ref[:, cols] = dg.astype(bf16)
            dlb_ref[:, cols] += dlb
            dng_sum = dng_sum + dng
        dng_ref[...] += dng_sum

    p = nb * BLOCK
    rev = lambda i: nb - 1 - i
    hw = HG_HEADS * HG_K
    blk = pl.BlockSpec((BLOCK, hw), lambda i: (rev(i), 0))
    wide = jax.ShapeDtypeStruct((p, hw), bf16)
    return pl.pallas_call(
        body, name="hgrn_bwd", grid=(nb,),
        in_specs=_hgrn_in_specs(rev) + [pl.BlockSpec((1, HG_HEADS, HG_K, HG_K), lambda i: (rev(i), 0, 0, 0)), blk],
        out_specs=[blk, blk, blk, blk, pl.BlockSpec((2, hw), lambda i: (0, 0)), pl.BlockSpec((1, HG_K), lambda i: (0, 0))],
        out_shape=[wide, wide, wide, wide, jax.ShapeDtypeStruct((2, hw), f32), jax.ShapeDtypeStruct((1, HG_K), f32)],
        scratch_shapes=[pltpu.VMEM((HG_HEADS, HG_K, HG_K), f32)],
        compiler_params=_cparams(("arbitrary",)),
    )(pa, pa, pa, pa, lbraw, ng, sprev, dog)


def _rope(x, cos, sin):
    lane = lax.broadcasted_iota(jnp.int32, x.shape, 1)
    rot = jnp.where(lane % HEAD_DIM < HEAD_DIM // 2, -roll_lanes(x, BLOCK - HEAD_DIM // 2),
                    roll_lanes(x, HEAD_DIM // 2))
    return x * cos + rot * sin


def _both_halves(x, g):
    lo = lax.broadcasted_iota(jnp.int32, x.shape, 1) < HEAD_DIM
    sw = roll_lanes(x, HEAD_DIM)
    return jnp.where(lo, x, sw) if g == 0 else jnp.where(lo, sw, x)


def _attn_block(band_ok, meta_ok, tabs, q, kp, kc, vp, vc, km, vm, *sinks):
    cq, sq, cp, sp, cm, sm = tabs
    neg = jnp.finfo(f32).min
    kk = jnp.concatenate([_rope(kp, cp, sp), _rope(kc, cq, sq)], axis=0)
    vv = jnp.concatenate([vp, vc], axis=0)
    kmr = _rope(km, cm, sm)
    lo = lax.broadcasted_iota(jnp.int32, (BLOCK, BLOCK), 1) < HEAD_DIM
    kv = [(_both_halves(kk, g), _both_halves(vv, g), _both_halves(kmr, g), _both_halves(vm, g)) for g in range(2)]
    slabs = []
    for m in range(ATT_HEADS // 2):
        qr = _rope(q[:, m * BLOCK:(m + 1) * BLOCK], cq, sq)
        kk_g, vv_g, km_g, vm_g = kv[m // 2]
        outs = []
        for half in range(2):
            sink = sinks[2 * m + half]
            qm = jnp.where(lo if half == 0 else ~lo, qr, 0.0)
            sb = jnp.where(band_ok, mm_nt(qm, kk_g) * (HEAD_DIM ** -0.5), neg)
            sme = jnp.where(meta_ok, mm_nt(qm, km_g) * (HEAD_DIM ** -0.5), neg)
            mx = jnp.maximum(jnp.maximum(jnp.max(sb, axis=-1, keepdims=True),
                                         jnp.max(sme, axis=-1, keepdims=True)), sink)
            eb, em = jnp.exp(sb - mx), jnp.exp(sme - mx)
            den = jnp.sum(eb, axis=-1, keepdims=True) + jnp.sum(em, axis=-1, keepdims=True) + jnp.exp(sink - mx)
            outs.append(mm(eb / den, vv_g) + mm(em / den, vm_g))
        slabs.append(jnp.where(lo, outs[0], outs[1]))
    return jnp.concatenate(slabs, axis=1)


def _attn_masks(i):
    qpos = _row_ids(i) - PAD
    s = lax.broadcasted_iota(jnp.int32, (1, 2 * BLOCK), 1)
    kpos = jnp.where(s < BLOCK, jnp.where(i > 0, (i - 1) * BLOCK - PAD + s, -1), i * BLOCK - PAD + s - BLOCK)
    band_ok = (kpos >= N_META) & (kpos <= qpos) & (qpos - kpos < BLOCK)
    meta_ok = lax.broadcasted_iota(jnp.int32, (1, N_META), 1) <= qpos
    return band_ok, meta_ok


def _attn_in_specs():
    prev = lambda i: jnp.maximum(i - 1, 0)
    kcol, vcol = N_A // BLOCK - 2, N_A // BLOCK - 1
    blk = lambda rowmap, col: pl.BlockSpec((BLOCK, BLOCK), lambda i: (rowmap(i), col))
    cur, first = (lambda i: i), (lambda i: 0)
    return [pl.BlockSpec((BLOCK, ATT_QW), lambda i: (i, 4)),
            blk(prev, kcol), blk(cur, kcol), blk(prev, vcol), blk(cur, vcol), blk(first, kcol), blk(first, vcol),
            blk(cur, 0), blk(cur, 0), blk(prev, 0), blk(prev, 0), blk(first, 0), blk(first, 0),
            _const_spec((ATT_HEADS, BLOCK))]


def _attn_operands(q_ref, kp_ref, kc_ref, vp_ref, vc_ref, km_ref, vm_ref, cq, sq, cp, sp, cm, sm, sk_ref):
    tabs = (cq[...], sq[...], cp[...], sp[...], cm[PAD:, :], sm[PAD:, :])
    args = (q_ref[...], kp_ref[...], kc_ref[...], vp_ref[...], vc_ref[...], km_ref[PAD:, :], vm_ref[PAD:, :])
    sinks = tuple(sk_ref[j:j + 1, 0:1] for j in range(ATT_HEADS))
    return tabs, args + sinks


def attn_fwd(pa, cos, sin, sinks8, nb):
    def body(*refs):
        o_ref = refs[-1]
        band_ok, meta_ok = _attn_masks(pl.program_id(0))
        tabs, args = _attn_operands(*refs[:-1])
        o_ref[...] = _attn_block(band_ok, meta_ok, tabs, *args).astype(bf16)

    return pl.pallas_call(
        body, name="attn_fwd", grid=(nb,), in_specs=_attn_in_specs(),
        out_specs=pl.BlockSpec((BLOCK, ATT_QW), lambda i: (i, 0)),
        out_shape=jax.ShapeDtypeStruct((nb * BLOCK, ATT_QW), bf16),
        compiler_params=_cparams(("parallel",)),
    )(pa, pa, pa, pa, pa, pa, pa, cos, sin, cos, sin, cos, sin, sinks8)


def attn_bwd(pa, cos, sin, sinks8, do, nb):
    def body(*refs):
        do_ref = refs[14]
        dq_ref, dkc_ref, dkp_ref, dvc_ref, dvp_ref, dkm_ref, dvm_ref, dsk_ref = refs[15:]
        i = pl.program_id(0)

        @pl.when(i == 0)
        def _():
            dkm_ref[...] = jnp.zeros((N_META, BLOCK), f32)
            dvm_ref[...] = jnp.zeros((N_META, BLOCK), f32)
            dsk_ref[...] = jnp.zeros((ATT_HEADS, BLOCK), f32)

        band_ok, meta_ok = _attn_masks(i)
        tabs, args = _attn_operands(*refs[:14])
        _, vjp = jax.vjp(functools.partial(_attn_block, band_ok, meta_ok, tabs), *args)
        grads = vjp(do_ref[...])
        dq_ref[...] = grads[0].astype(bf16)
        dkp_ref[...] = grads[1]
        dkc_ref[...] = grads[2]
        dvp_ref[...] = grads[3]
        dvc_ref[...] = grads[4]
        dkm_ref[...] += grads[5]
        dvm_ref[...] += grads[6]
        for j in range(ATT_HEADS):
            dsk_ref[j:j + 1, :] += jnp.broadcast_to(grads[7 + j], (1, BLOCK))

    p = nb * BLOCK
    row = pl.BlockSpec((BLOCK, BLOCK), lambda i: (i, 0))
    const = lambda r: pl.BlockSpec((r, BLOCK), lambda i: (0, 0))
    part = jax.ShapeDtypeStruct((p, BLOCK), f32)
    return pl.pallas_call(
        body, name="attn_bwd", grid=(nb,),
        in_specs=_attn_in_specs() + [pl.BlockSpec((BLOCK, ATT_QW), lambda i: (i, 0))],
        out_specs=[pl.BlockSpec((BLOCK, ATT_QW), lambda i: (i, 0)), row, row, row, row,
                   const(N_META), const(N_META), const(ATT_HEADS)],
        out_shape=[jax.ShapeDtypeStruct((p, ATT_QW), bf16), part, part, part, part,
                   jax.ShapeDtypeStruct((N_META, BLOCK), f32), jax.ShapeDtypeStruct((N_META, BLOCK), f32),
                   jax.ShapeDtypeStruct((ATT_HEADS, BLOCK), f32)],
        compiler_params=_cparams(("arbitrary",)),
    )(pa, pa, pa, pa, pa, pa, pa, cos, sin, cos, sin, cos, sin, sinks8, do)


def mid_rows(h0, pg, og, oatt, target, wbh, wba, wout, wfi, wfo, ln1g, ln1b, ln2g, ln2b, nb):
    def body(h0_ref, pg_ref, og_ref, oa_ref, t_ref, wbh_ref, wba_ref, wo_ref, wfi_ref, wfo_ref,
             g1_ref, b1_ref, g2_ref, b2_ref,
             dh0_ref, dpg_ref, dog_ref, doa_ref, dyh_ref, dya_ref, mix_ref, dr1_ref, h1b_ref, dau_ref, s_ref, dr2_ref,
             loss_ref, dg1_ref, db1_ref, dg2_ref, db2_ref):
        i = pl.program_id(0)

        @pl.when(i == 0)
        def _():
            loss_ref[...] = jnp.zeros_like(loss_ref)
            for r in (dg1_ref, db1_ref, dg2_ref, db2_ref):
                r[...] = jnp.zeros_like(r)

        g1, b1, g2, b2 = g1_ref[...], b1_ref[...], g2_ref[...], b2_ref[...]
        yh = jnp.dot(og_ref[...], wbh_ref[...], preferred_element_type=f32)
        ya = jnp.dot(oa_ref[...], wba_ref[...], preferred_element_type=f32)
        gh = _sigmoid(pg_ref[:, :D_MODEL])
        ga = _sigmoid(pg_ref[:, D_MODEL:])
        mixin = (gh * yh + ga * ya).astype(bf16)
        mix_ref[...] = mixin
        r1 = ALPHA * h0_ref[...] + jnp.dot(mixin, wo_ref[...], preferred_element_type=f32)
        xh1, rs1 = _ln_stats(r1)
        h1 = xh1 * g1 + b1
        h1b = h1.astype(bf16)
        h1b_ref[...] = h1b
        au = jnp.dot(h1b, wfi_ref[...], preferred_element_type=f32)
        a, u = au[:, :D_FF], au[:, D_FF:]
        sg = _sigmoid(a)
        sa = a * sg
        s = (sa * u).astype(bf16)
        s_ref[...] = s
        r2 = ALPHA * h1 + jnp.dot(s, wfo_ref[...], preferred_element_type=f32)
        xh2, rs2 = _ln_stats(r2)
        diff = jnp.where(i > 0, xh2 * g2 + b2 - t_ref[...], 0.0)
        loss_ref[...] += jnp.sum(diff * diff) * (0.5 / D_MODEL)
        dy = diff * (1.0 / D_MODEL)
        dg2_ref[...] += jnp.sum(dy * xh2, axis=0, keepdims=True)
        db2_ref[...] += jnp.sum(dy, axis=0, keepdims=True)
        dr2 = _ln_bwd(dy, xh2, rs2, g2)
        dr2b = dr2.astype(bf16)
        dr2_ref[...] = dr2b
        ds = _dot(dr2b, wfo_ref[...], 1, 1)
        da = (ds * u) * (sg * (1.0 + a * (1.0 - sg)))
        du = ds * sa
        dau = jnp.concatenate([da, du], axis=1).astype(bf16)
        dau_ref[...] = dau
        dh1 = ALPHA * dr2 + _dot(dau, wfi_ref[...], 1, 1)
        dg1_ref[...] += jnp.sum(dh1 * xh1, axis=0, keepdims=True)
        db1_ref[...] += jnp.sum(dh1, axis=0, keepdims=True)
        dr1 = _ln_bwd(dh1, xh1, rs1, g1)
        dr1b = dr1.astype(bf16)
        dr1_ref[...] = dr1b
        dh0_ref[...] = ALPHA * dr1
        dmix = _dot(dr1b, wo_ref[...], 1, 1)
        dyh = (dmix * gh).astype(bf16)
        dya = (dmix * ga).astype(bf16)
        dyh_ref[...] = dyh
        dya_ref[...] = dya
        dpg_ref[:, :D_MODEL] = (dmix * yh * gh * (1.0 - gh)).astype(bf16)
        dpg_ref[:, D_MODEL:] = (dmix * ya * ga * (1.0 - ga)).astype(bf16)
        dog_ref[...] = _dot(dyh, wbh_ref[...], 1, 1)
        doa_ref[...] = _dot(dya, wba_ref[...], 1, 1)

    p = nb * BLOCK
    row = lambda n: pl.BlockSpec((BLOCK, n), lambda i: (i, 0))
    vec = lambda: pl.BlockSpec((1, D_MODEL), lambda i: (0, 0))
    sds = lambda n, dt: jax.ShapeDtypeStruct((p, n), dt)
    hw = HG_HEADS * HG_K
    return pl.pallas_call(
        body, name="mid_rows", grid=(nb,),
        in_specs=[row(D_MODEL), row(N_G), row(hw), row(ATT_QW),
                  pl.BlockSpec((BLOCK, D_MODEL), lambda i: (jnp.maximum(i - 1, 0), 0)),
                  _const_spec((hw, D_MODEL)), _const_spec((ATT_QW, D_MODEL)), _const_spec((D_MODEL, D_MODEL)),
                  _const_spec((D_MODEL, 2 * D_FF)), _const_spec((D_FF, D_MODEL)),
                  _const_spec((1, D_MODEL)), _const_spec((1, D_MODEL)), _const_spec((1, D_MODEL)),
                  _const_spec((1, D_MODEL))],
        out_specs=[row(D_MODEL), row(N_G), row(hw), row(ATT_QW), row(D_MODEL), row(D_MODEL), row(D_MODEL),
                   row(D_MODEL), row(D_MODEL), row(2 * D_FF), row(D_FF), row(D_MODEL),
                   pl.BlockSpec((1, 1), lambda i: (0, 0)), vec(), vec(), vec(), vec()],
        out_shape=[sds(D_MODEL, f32), sds(N_G, bf16), sds(hw, f32), sds(ATT_QW, f32), sds(D_MODEL, bf16),
                   sds(D_MODEL, bf16), sds(D_MODEL, bf16), sds(D_MODEL, bf16), sds(D_MODEL, bf16),
                   sds(2 * D_FF, bf16), sds(D_FF, bf16), sds(D_MODEL, bf16),
                   jax.ShapeDtypeStruct((1, 1), f32)] + [jax.ShapeDtypeStruct((1, D_MODEL), f32)] * 4,
        compiler_params=_cparams(("arbitrary",)),
    )(h0, pg, og, oatt, target, wbh, wba, wout, wfi, wfo, ln1g, ln1b, ln2g, ln2b)


def inproj_bwd(dh0p, dhq, dhf, dhi, dhg, daq, dkc, dkp, dvc, dvp, dkm, dvm, dpg, w_in, x, metablk, g, b, nb):
    def body(dh0_ref, dq_ref, df_ref, di_ref, dg_ref, daq_ref, dkc_ref, dkp_ref, dvc_ref, dvp_ref, dkm_ref, dvm_ref,
             dpg_ref, w_ref, x_ref, mb_ref, g_ref, b_ref, dproj_ref, dx_ref, dlg_ref, dlb_ref):
        i = pl.program_id(0)

        @pl.when(i == 0)
        def _():
            dlg_ref[...] = jnp.zeros_like(dlg_ref)
            dlb_ref[...] = jnp.zeros_like(dlb_ref)

        zero_pad = jnp.zeros((PAD, BLOCK), f32)
        has_next = i + 1 < nb
        first = i == 0

        def keys(cur_ref, next_ref, meta_ref):
            t = cur_ref[...] + jnp.where(has_next, next_ref[...], 0.0)
            return t + jnp.where(first, jnp.concatenate([zero_pad, meta_ref[...]], axis=0), 0.0)

        dproj = jnp.concatenate(
            [dq_ref[...], df_ref[...], di_ref[...], dg_ref[...], daq_ref[...],
             keys(dkc_ref, dkp_ref, dkm_ref).astype(bf16), keys(dvc_ref, dvp_ref, dvm_ref).astype(bf16),
             dpg_ref[...]], axis=1)
        dproj_ref[...] = dproj
        valid = _row_ids(i) >= PAD
        dh0 = jnp.where(valid, dh0_ref[...] + _dot(dproj, w_ref[...], 1, 1), 0.0)
        xb = jnp.where(first, mb_ref[...], x_ref[...])
        xh, rs = _ln_stats(xb)
        dlg_ref[...] += jnp.sum(dh0 * xh, axis=0, keepdims=True)
        dlb_ref[...] += jnp.sum(dh0, axis=0, keepdims=True)
        dx_ref[...] = jnp.where(valid, _ln_bwd(dh0, xh, rs, g_ref[...]), 0.0)

    p = nb * BLOCK
    row = lambda n: pl.BlockSpec((BLOCK, n), lambda i: (i, 0))
    nxt = pl.BlockSpec((BLOCK, BLOCK), lambda i: (jnp.minimum(i + 1, nb - 1), 0))
    hw = HG_HEADS * HG_K
    vec = lambda: pl.BlockSpec((1, D_MODEL), lambda i: (0, 0))
    return pl.pallas_call(
        body, name="inproj_bwd", grid=(nb,),
        in_specs=[row(D_MODEL), row(hw), row(hw), row(hw), row(hw), row(ATT_QW),
                  row(BLOCK), nxt, row(BLOCK), nxt, _const_spec((N_META, BLOCK)), _const_spec((N_META, BLOCK)),
                  row(N_G), _const_spec((D_MODEL, IN_W)),
                  pl.BlockSpec((BLOCK, D_MODEL), lambda i: (jnp.maximum(i - 1, 0), 0)),
                  _const_spec((BLOCK, D_MODEL)), _const_spec((1, D_MODEL)), _const_spec((1, D_MODEL))],
        out_specs=[row(IN_W), row(D_MODEL), vec(), vec()],
        out_shape=[jax.ShapeDtypeStruct((p, IN_W), bf16), jax.ShapeDtypeStruct((p, D_MODEL), f32),
                   jax.ShapeDtypeStruct((1, D_MODEL), f32), jax.ShapeDtypeStruct((1, D_MODEL), f32)],
        compiler_params=_cparams(("arbitrary",)),
    )(dh0p, dhq, dhf, dhi, dhg, daq, dkc, dkp, dvc, dvp, dkm, dvm, dpg, w_in, x, metablk, g, b)


def wgrad(a, b, name, tk, tn, tp, by_cols):
    p, k = a.shape
    n = b.shape[1]
    nsteps = p // tp

    def body(a_ref, b_ref, o_ref):
        @pl.when(pl.program_id(2) == 0)
        def _():
            o_ref[...] = jnp.zeros_like(o_ref)

        o_ref[0] += _dot(a_ref[...], b_ref[...], 0, 0)

    if by_cols:
        shard_n = n // N_SHARD
        per = shard_n // tn
        out_shape = (N_SHARD, k, shard_n)
        omap = lambda ik, jn, ip: (jn // per, ik, jn % per)
    else:
        out_shape = (1, k, n)
        omap = lambda ik, jn, ip: (0, ik, jn)
    return pl.pallas_call(
        body, name=name, grid=(k // tk, n // tn, nsteps),
        in_specs=[pl.BlockSpec((tp, tk), lambda ik, jn, ip: (ip, ik)),
                  pl.BlockSpec((tp, tn), lambda ik, jn, ip: (ip, jn))],
        out_specs=pl.BlockSpec((1, tk, tn), omap),
        out_shape=jax.ShapeDtypeStruct(out_shape, f32),
        compiler_params=_cparams(("parallel", "parallel", "arbitrary")),
    )(a, b)


def adamw(w, g, m, v, name):
    r, c = w.shape
    tr = r
    for cand in (256, 176, 128):
        if r > cand and r % cand == 0:
            tr = cand
            break

    def body(w_ref, g_ref, m_ref, v_ref, d_ref, mo_ref, vo_ref):
        gg = g_ref[...]
        mn = ADAM_B1 * m_ref[...] + (1.0 - ADAM_B1) * gg
        vn = ADAM_B2 * v_ref[...] + (1.0 - ADAM_B2) * (gg * gg)
        m_hat = mn / (1.0 - ADAM_B1 ** ADAM_STEP)
        v_hat = vn / (1.0 - ADAM_B2 ** ADAM_STEP)
        d_ref[...] = -ADAM_LR * (m_hat / (jnp.sqrt(v_hat) + ADAM_EPS) + ADAM_WD * w_ref[...])
        mo_ref[...] = mn
        vo_ref[...] = vn

    spec = pl.BlockSpec((tr, c), lambda i: (i, 0))
    sds = jax.ShapeDtypeStruct((r, c), f32)
    return pl.pallas_call(
        body, name=name, grid=(r // tr,), in_specs=[spec] * 4, out_specs=[spec] * 3, out_shape=[sds] * 3,
        compiler_params=_cparams(("parallel",)),
    )(w, g, m, v)


def _me():
    return lax.axis_index("x"), lax.axis_index("y"), lax.axis_index("c")


def _chip_peer(x, y, c, k):
    return (x ^ (k >> 1), y ^ (k & 1), c)


ANY = pl.BlockSpec(memory_space=pl.ANY)


def gather_weights(shards):
    n = len(shards)
    out_dtypes = [bf16 if s.size > 16 * 256 else f32 for s in shards]

    def body(*refs):
        ins, outs = refs[:n], refs[n:2 * n]
        stage = refs[2 * n:3 * n]
        send_sems, recv_sems, local_sems = refs[3 * n:]
        x, y, c = _me()
        j = 2 * x + y
        for w in range(n):
            stage[w][...] = ins[w][...].astype(out_dtypes[w])
        sends, locs = [], []
        for w in range(n):
            loc = pltpu.make_async_copy(stage[w], outs[w].at[j], local_sems.at[w])
            loc.start()
            locs.append(loc)
            for k in (1, 2, 3):
                cp = pltpu.make_async_remote_copy(
                    src_ref=stage[w], dst_ref=outs[w].at[j], send_sem=send_sems.at[w, k - 1],
                    recv_sem=recv_sems.at[w, k - 1], device_id=_chip_peer(x, y, c, k), device_id_type=MESH)
                cp.start()
                sends.append(cp)
        for w in range(n):
            for k in (1, 2, 3):
                pltpu.make_async_remote_copy(
                    src_ref=stage[w], dst_ref=outs[w].at[j ^ k], send_sem=send_sems.at[w, k - 1],
                    recv_sem=recv_sems.at[w, k - 1], device_id=_chip_peer(x, y, c, k), device_id_type=MESH).wait_recv()
        for cp in sends:
            cp.wait_send()
        for loc in locs:
            loc.wait()

    return pl.pallas_call(
        body, name="gather_weights",
        in_specs=[pl.BlockSpec(memory_space=pltpu.VMEM)] * n, out_specs=[ANY] * n,
        out_shape=[jax.ShapeDtypeStruct((N_SHARD,) + s.shape, dt) for s, dt in zip(shards, out_dtypes)],
        scratch_shapes=[pltpu.VMEM(s.shape, dt) for s, dt in zip(shards, out_dtypes)]
        + [pltpu.SemaphoreType.DMA((n, 3)), pltpu.SemaphoreType.DMA((n, 3)), pltpu.SemaphoreType.DMA((n,))],
        compiler_params=pltpu.CompilerParams(vmem_limit_bytes=VMEM_LIMIT),
    )(*shards)


def pair_exchange_halves(grads, small):
    n = len(grads)

    def body(*refs):
        ins, small_ref = refs[:n], refs[n]
        outs, gath = refs[n + 1:2 * n + 1], refs[2 * n + 1]
        send_sems, recv_sems, s_send, s_recv, local_sem = refs[2 * n + 2:]
        x, y, c = _me()
        me = 4 * x + 2 * y + c
        sends = []
        for w in range(n):
            half = ins[w].shape[1] // 2
            cp = pltpu.make_async_remote_copy(
                src_ref=ins[w].at[:, pl.ds((1 - c) * half, half), :], dst_ref=outs[w],
                send_sem=send_sems.at[w], recv_sem=recv_sems.at[w], device_id=(x, y, 1 - c), device_id_type=MESH)
            cp.start()
            sends.append(cp)
        loc = pltpu.make_async_copy(small_ref, gath.at[me], local_sem)
        loc.start()
        for k in range(1, N_DEV):
            cp = pltpu.make_async_remote_copy(
                src_ref=small_ref, dst_ref=gath.at[me], send_sem=s_send.at[k - 1], recv_sem=s_recv.at[k - 1],
                device_id=(x ^ (k >> 2), y ^ ((k >> 1) & 1), c ^ (k & 1)), device_id_type=MESH)
            cp.start()
            sends.append(cp)
        for w in range(n):
            half = ins[w].shape[1] // 2
            pltpu.make_async_remote_copy(
                src_ref=ins[w].at[:, pl.ds(0, half), :], dst_ref=outs[w], send_sem=send_sems.at[w],
                recv_sem=recv_sems.at[w], device_id=(x, y, 1 - c), device_id_type=MESH).wait_recv()
        for k in range(1, N_DEV):
            pltpu.make_async_remote_copy(
                src_ref=small_ref, dst_ref=gath.at[me ^ k], send_sem=s_send.at[k - 1], recv_sem=s_recv.at[k - 1],
                device_id=(x ^ (k >> 2), y ^ ((k >> 1) & 1), c ^ (k & 1)), device_id_type=MESH).wait_recv()
        for cp in sends:
            cp.wait_send()
        loc.wait()

    return pl.pallas_call(
        body, name="pair_exchange_halves", in_specs=[ANY] * (n + 1), out_specs=[ANY] * (n + 1),
        out_shape=[jax.ShapeDtypeStruct((g.shape[0], g.shape[1] // 2, g.shape[2]), f32) for g in grads]
        + [jax.ShapeDtypeStruct((N_DEV,) + small.shape, f32)],
        scratch_shapes=[pltpu.SemaphoreType.DMA((n,)), pltpu.SemaphoreType.DMA((n,)),
                        pltpu.SemaphoreType.DMA((N_DEV - 1,)), pltpu.SemaphoreType.DMA((N_DEV - 1,)),
                        pltpu.SemaphoreType.DMA],
    )(*grads, small)


def chip_exchange(sums):
    n = len(sums)

    def body(*refs):
        ins, outs = refs[:n], refs[n:2 * n]
        send_sems, recv_sems, local_sems = refs[2 * n:]
        x, y, c = _me()
        j = 2 * x + y
        sends, locs = [], []
        for w in range(n):
            loc = pltpu.make_async_copy(ins[w].at[j], outs[w].at[0], local_sems.at[w])
            loc.start()
            locs.append(loc)
            for k in (1, 2, 3):
                cp = pltpu.make_async_remote_copy(
                    src_ref=ins[w].at[j ^ k], dst_ref=outs[w].at[k], send_sem=send_sems.at[w, k - 1],
                    recv_sem=recv_sems.at[w, k - 1], device_id=_chip_peer(x, y, c, k), device_id_type=MESH)
                cp.start()
                sends.append(cp)
        for w in range(n):
            for k in (1, 2, 3):
                pltpu.make_async_remote_copy(
                    src_ref=ins[w].at[0], dst_ref=outs[w].at[k], send_sem=send_sems.at[w, k - 1],
                    recv_sem=recv_sems.at[w, k - 1], device_id=_chip_peer(x, y, c, k), device_id_type=MESH).wait_recv()
        for cp in sends:
            cp.wait_send()
        for loc in locs:
            loc.wait()

    return pl.pallas_call(
        body, name="chip_exchange", in_specs=[ANY] * n, out_specs=[ANY] * n,
        out_shape=[jax.ShapeDtypeStruct(s.shape, s.dtype) for s in sums],
        scratch_shapes=[pltpu.SemaphoreType.DMA((n, 3)), pltpu.SemaphoreType.DMA((n, 3)),
                        pltpu.SemaphoreType.DMA((n,))],
    )(*sums)


def pair_exchange_results(halves):
    n = len(halves)

    def body(*refs):
        ins, outs = refs[:n], refs[n:2 * n]
        send_sems, recv_sems, local_sems = refs[2 * n:]
        x, y, c = _me()
        sends, locs = [], []
        for w in range(n):
            loc = pltpu.make_async_copy(ins[w], outs[w].at[c], local_sems.at[w])
            loc.start()
            locs.append(loc)
            cp = pltpu.make_async_remote_copy(
                src_ref=ins[w], dst_ref=outs[w].at[c], send_sem=send_sems.at[w], recv_sem=recv_sems.at[w],
                device_id=(x, y, 1 - c), device_id_type=MESH)
            cp.start()
            sends.append(cp)
        for w in range(n):
            pltpu.make_async_remote_copy(
                src_ref=ins[w], dst_ref=outs[w].at[1 - c], send_sem=send_sems.at[w],
                recv_sem=recv_sems.at[w], device_id=(x, y, 1 - c), device_id_type=MESH).wait_recv()
        for cp in sends:
            cp.wait_send()
        for loc in locs:
            loc.wait()

    return pl.pallas_call(
        body, name="pair_exchange_results", in_specs=[ANY] * n, out_specs=[ANY] * n,
        out_shape=[jax.ShapeDtypeStruct((2,) + h.shape, f32) for h in halves],
        scratch_shapes=[pltpu.SemaphoreType.DMA((n,)), pltpu.SemaphoreType.DMA((n,)), pltpu.SemaphoreType.DMA((n,))],
    )(*halves)


def add_pair(grad, other, c_idx, name):
    _, r, c = grad.shape
    half = r // 2
    tr = half // 2 if (half // 2) % 8 == 0 else half
    per = half // tr

    def body(c_ref, g_ref, o_ref, out_ref):
        out_ref[...] = (g_ref[...] + o_ref[...]).astype(bf16)

    return pl.pallas_call(
        body, name=name,
        grid_spec=pltpu.PrefetchScalarGridSpec(
            num_scalar_prefetch=1, grid=(N_SHARD, per),
            in_specs=[pl.BlockSpec((1, tr, c), lambda j, t, cr: (j, cr[0] * per + t, 0)),
                      pl.BlockSpec((1, tr, c), lambda j, t, cr: (j, t, 0))],
            out_specs=pl.BlockSpec((1, tr, c), lambda j, t, cr: (j, t, 0))),
        out_shape=jax.ShapeDtypeStruct((N_SHARD, half, c), bf16),
        compiler_params=_cparams(("parallel", "parallel")),
    )(c_idx, grad, other)


def add_four(parts, name):
    _, half, c = parts.shape
    tr = half // 2 if (half // 2) % 8 == 0 else half

    def body(p_ref, out_ref):
        p = [p_ref[k].astype(f32) for k in range(N_SHARD)]
        out_ref[...] = ((p[0] + p[1]) + p[2]) + p[3]

    return pl.pallas_call(
        body, name=name, grid=(half // tr,),
        in_specs=[pl.BlockSpec((N_SHARD, tr, c), lambda t: (0, t, 0))],
        out_specs=pl.BlockSpec((tr, c), lambda t: (t, 0)),
        out_shape=jax.ShapeDtypeStruct((half, c), f32),
        compiler_params=_cparams(("parallel",)),
    )(parts)


def sum_devices(gathered):
    def body(g_ref, out_ref):
        acc = g_ref[0]
        for d in range(1, N_DEV):
            acc = acc + g_ref[d]
        out_ref[...] = acc

    return pl.pallas_call(body, name="sum_devices", out_shape=jax.ShapeDtypeStruct(gathered.shape[1:], f32))(gathered)


def _rows128(a, rows):
    flat = a.reshape(-1, BLOCK) if a.size % BLOCK == 0 else jnp.pad(a.reshape(1, -1), ((0, 0), (0, BLOCK - a.size)))
    return jnp.pad(flat, ((0, rows - flat.shape[0]), (0, 0)))


def kernel(x, meta_tokens, ln_emb_g, ln_emb_b, w_in, hg_lower_bounds, hg_norm_g, attn_sinks, w_branch_hg, w_branch_attn, w_out, ln1_g, ln1_b, w_ffn_in, w_ffn_out, ln2_g, ln2_b, loss_target, m_meta_tokens, m_ln_emb_g, m_ln_emb_b, m_w_in, m_hg_lower_bounds, m_hg_norm_g, m_attn_sinks, m_w_branch_hg, m_w_branch_attn, m_w_out, m_ln1_g, m_ln1_b, m_w_ffn_in, m_w_ffn_out, m_ln2_g, m_ln2_b, v_meta_tokens, v_ln_emb_g, v_ln_emb_b, v_w_in, v_hg_lower_bounds, v_hg_norm_g, v_attn_sinks, v_w_branch_hg, v_w_branch_attn, v_w_out, v_ln1_g, v_ln1_b, v_w_ffn_in, v_w_ffn_out, v_ln2_g, v_ln2_b):
    seq = x.shape[1]
    nb = seq // BLOCK + 1
    xs = x[0]
    ts = loss_target[0]
    ix, iy, ic = _me()
    shard = 2 * ix + iy
    vec = lambda a: a.reshape(1, D_MODEL)

    big = [w_in[0], w_branch_hg[0], w_branch_attn[0], w_out[0], w_ffn_in[0], w_ffn_out[0]]
    g_in, g_bh, g_ba, g_out, g_fi, g_fo, g_meta = gather_weights(big + [meta_tokens])
    by_cols = lambda g: g.transpose(1, 0, 2).reshape(g.shape[1], N_SHARD * g.shape[2])
    wf_in, wf_bh, wf_ba, wf_fi = by_cols(g_in), by_cols(g_bh), by_cols(g_ba), by_cols(g_fi)
    wf_out = g_out.reshape(D_MODEL, D_MODEL)
    wf_fo = g_fo.reshape(D_FF, D_MODEL)
    metablk = jnp.pad(by_cols(g_meta), ((PAD, 0), (0, 0)))

    pos = jnp.arange(nb * BLOCK, dtype=jnp.int32) - PAD
    half = HEAD_DIM // 2
    inv = ROPE_THETA ** (-jnp.arange(half, dtype=f32) / half)
    ang = pos.astype(f32)[:, None] * inv[None, :]
    cos = jnp.tile(jnp.cos(ang), (1, BLOCK // half))
    sin = jnp.tile(jnp.sin(ang), (1, BLOCK // half))
    sinks8 = jnp.broadcast_to(attn_sinks.reshape(ATT_HEADS, 1), (ATT_HEADS, BLOCK))
    ng = hg_norm_g.reshape(1, HG_K)

    h0, h0b, pa, pg = emb_inproj(xs, metablk, vec(ln_emb_g), vec(ln_emb_b), wf_in, nb)
    og, sprev = hgrn_fwd(pa, hg_lower_bounds, ng, nb)
    oatt = attn_fwd(pa, cos, sin, sinks8, nb)
    (dh0p, dpg, dog, doa, dyh, dya, mixin, dr1, h1b, dau, sact, dr2,
     loss_part, dg1, db1, dg2, db2) = mid_rows(h0, pg, og, oatt, ts, wf_bh, wf_ba, wf_out, wf_fi, wf_fo,
                                              ln1_g, ln1_b, ln2_g, ln2_b, nb)
    dhq, dhf, dhi, dhg, dlb4, dng = hgrn_bwd(pa, hg_lower_bounds, ng, sprev, dog, nb)
    daq, dkc, dkp, dvc, dvp, dkm, dvm, dsk = attn_bwd(pa, cos, sin, sinks8, doa, nb)
    dproj, dxp, dlg, dlb = inproj_bwd(dh0p, dhq, dhf, dhi, dhg, daq, dkc, dkp, dvc, dvp, dkm, dvm, dpg,
                                      wf_in, xs, metablk, vec(ln_emb_g), vec(ln_emb_b), nb)

    tp = BLOCK * (5 if nb % 5 == 0 else 1)
    gw_in = wgrad(h0b, dproj, "wgrad_in", D_MODEL, IN_W // 2, tp, False)
    gw_in = gw_in.reshape(D_MODEL, N_SHARD, IN_W // N_SHARD).transpose(1, 0, 2)
    gw_bh = wgrad(og, dyh, "wgrad_bh", 512, 256, tp, True)
    gw_ba = wgrad(oatt, dya, "wgrad_ba", 512, 256, tp, True)
    gw_out = wgrad(mixin, dr1, "wgrad_out", D_MODEL, D_MODEL, tp, False).reshape(N_SHARD, -1, D_MODEL)
    gw_fi = wgrad(h1b, dau, "wgrad_fi", D_MODEL, 2 * D_FF // N_SHARD, tp, True)
    gw_fo = wgrad(sact, dr2, "wgrad_fo", D_FF // 2, D_MODEL, tp, False).reshape(N_SHARD, -1, D_MODEL)
    grads = [gw_in, gw_bh, gw_ba, gw_out, gw_fi, gw_fo]

    parts = [(dlg, 8), (dlb, 8), (dlb4, 8), (dng, 8), (dsk[:, 0], 8),
             (dg1, 8), (db1, 8), (dg2, 8), (db2, 8), (dxp[PAD:BLOCK], BLOCK)]
    small = jnp.concatenate([_rows128(a, r) for a, r in parts], axis=0)

    c_idx = jnp.reshape(ic, (1,)).astype(jnp.int32)
    *others, gathered = pair_exchange_halves(grads, small)
    sums = [add_pair(g, o, c_idx, "add_pair_%d" % n) for n, (g, o) in enumerate(zip(grads, others))]
    quads = chip_exchange(sums)
    halves = [add_four(q, "add_four_%d" % n) for n, q in enumerate(quads)]
    red = [r.reshape(-1, r.shape[-1]) for r in pair_exchange_results(halves)]
    small_sum = sum_devices(gathered)

    offs, acc = [], 0
    for _, r in parts:
        offs.append(acc)
        acc += r
    take = lambda n, size: small_sum[offs[n]:offs[n] + parts[n][1]].reshape(-1)[:size]
    g_meta_full = take(9, N_META * D_MODEL).reshape(N_META, D_MODEL)
    g_small = {
        "meta_tokens": lax.dynamic_slice_in_dim(g_meta_full, shard * (D_MODEL // N_SHARD), D_MODEL // N_SHARD, axis=1),
        "ln_emb_g": take(0, D_MODEL), "ln_emb_b": take(1, D_MODEL),
        "hg_lower_bounds": take(2, 2 * HG_HEADS * HG_K).reshape(2, HG_HEADS * HG_K),
        "hg_norm_g": take(3, HG_K).reshape(1, HG_K), "attn_sinks": take(4, ATT_HEADS).reshape(1, ATT_HEADS),
        "ln1_g": take(5, D_MODEL).reshape(1, D_MODEL), "ln1_b": take(6, D_MODEL).reshape(1, D_MODEL),
        "ln2_g": take(7, D_MODEL).reshape(1, D_MODEL), "ln2_b": take(8, D_MODEL).reshape(1, D_MODEL),
    }
    g_big = {"w_in": red[0], "w_branch_hg": red[1], "w_branch_attn": red[2], "w_out": red[3],
             "w_ffn_in": red[4], "w_ffn_out": red[5]}

    names = ["meta_tokens", "ln_emb_g", "ln_emb_b", "w_in", "hg_lower_bounds", "hg_norm_g", "attn_sinks",
             "w_branch_hg", "w_branch_attn", "w_out", "ln1_g", "ln1_b", "w_ffn_in", "w_ffn_out", "ln2_g", "ln2_b"]
    given = dict(
        meta_tokens=(meta_tokens, m_meta_tokens, v_meta_tokens), ln_emb_g=(ln_emb_g, m_ln_emb_g, v_ln_emb_g),
        ln_emb_b=(ln_emb_b, m_ln_emb_b, v_ln_emb_b), w_in=(w_in, m_w_in, v_w_in),
        hg_lower_bounds=(hg_lower_bounds, m_hg_lower_bounds, v_hg_lower_bounds),
        hg_norm_g=(hg_norm_g, m_hg_norm_g, v_hg_norm_g), attn_sinks=(attn_sinks, m_attn_sinks, v_attn_sinks),
        w_branch_hg=(w_branch_hg, m_w_branch_hg, v_w_branch_hg),
        w_branch_attn=(w_branch_attn, m_w_branch_attn, v_w_branch_attn), w_out=(w_out, m_w_out, v_w_out),
        ln1_g=(ln1_g, m_ln1_g, v_ln1_g), ln1_b=(ln1_b, m_ln1_b, v_ln1_b), w_ffn_in=(w_ffn_in, m_w_ffn_in, v_w_ffn_in),
        w_ffn_out=(w_ffn_out, m_w_ffn_out, v_w_ffn_out), ln2_g=(ln2_g, m_ln2_g, v_ln2_g), ln2_b=(ln2_b, m_ln2_b, v_ln2_b))
    out_g, out_d, out_m, out_v = [], [], [], []
    for nm in names:
        w, m, v = given[nm]
        shape = w.shape
        g = g_big[nm] if nm in g_big else g_small[nm]
        two_d = (lambda a: a.reshape(8, BLOCK)) if w.ndim == 1 else (lambda a: a.reshape(a.shape[-2], a.shape[-1]))
        d, mn, vn = adamw(two_d(w), two_d(g), two_d(m), two_d(v), "adamw_" + nm)
        out_g.append(g.reshape(shape))
        out_d.append(d.reshape(shape))
        out_m.append(mn.reshape(shape))
        out_v.append(vn.reshape(shape))

    loss = lax.psum(loss_part[0, 0], ("x", "y", "c"))
    grad_x = dxp[BLOCK:].reshape(x.shape)
    return (loss, grad_x, *out_g, *out_d, *out_m, *out_v)
```

```python
import functools

import jax
import jax.numpy as jnp
from jax import lax
from jax.experimental import pallas as pl
from jax.experimental.pallas import tpu as pltpu

f32 = jnp.float32
bf16 = jnp.bfloat16

D_MODEL = 1024
BLOCK = 128
N_META = 16
PAD = BLOCK - N_META
HG_HEADS = 4
HG_K = 128
SUB = 16
ATT_HEADS = 8
HEAD_DIM = 64
ATT_QW = ATT_HEADS * HEAD_DIM
D_FF = 2816
EPS = 1e-5
ALPHA = 2.0 ** 0.25
ROPE_THETA = 10000.0
N_A = 2816
N_G = 2048
IN_W = N_A + N_G
N_SHARD = 4
N_DEV = 8

ADAM_LR = 0.001
ADAM_B1 = 0.9
ADAM_B2 = 0.999
ADAM_EPS = 1e-08
ADAM_WD = 0.01
ADAM_STEP = 10

VMEM_LIMIT = 56 * 1024 * 1024
MESH = pl.DeviceIdType.MESH


def _cparams(sem, vmem=VMEM_LIMIT):
    return pltpu.CompilerParams(dimension_semantics=sem, vmem_limit_bytes=vmem)


def _const_spec(shape):
    zeros = (0,) * len(shape)
    return pl.BlockSpec(shape, lambda *_: zeros, pipeline_mode=pl.Buffered(1))


def _dot(a, b, ca, cb):
    return lax.dot_general(a.astype(bf16), b.astype(bf16), (((ca,), (cb,)), ((), ())),
                           preferred_element_type=f32)


@jax.custom_vjp
def mm(a, b):
    return _dot(a, b, 1, 0)


mm.defvjp(lambda a, b: (_dot(a, b, 1, 0), (a, b)),
          lambda r, g: (_dot(g, r[1], 1, 1), _dot(r[0], g, 0, 0)))


@jax.custom_vjp
def mm_nt(a, b):
    return _dot(a, b, 1, 1)


mm_nt.defvjp(lambda a, b: (_dot(a, b, 1, 1), (a, b)),
             lambda r, g: (_dot(g, r[1], 1, 0), _dot(g, r[0], 0, 0)))


@jax.custom_vjp
def mm_tn(a, b):
    return _dot(a, b, 0, 0)


mm_tn.defvjp(lambda a, b: (_dot(a, b, 0, 0), (a, b)),
             lambda r, g: (_dot(r[1], g, 1, 1), _dot(r[0], g, 1, 0)))


@functools.partial(jax.custom_vjp, nondiff_argnums=(1,))
def roll_lanes(x, shift):
    return pltpu.roll(x, shift, 1)


roll_lanes.defvjp(lambda x, shift: (pltpu.roll(x, shift, 1), None),
                  lambda shift, _, g: (pltpu.roll(g, (128 - shift) % 128, 1),))


def _sigmoid(x):
    return 1.0 / (1.0 + jnp.exp(-x))


def _ln_stats(x):
    mu = jnp.mean(x, axis=-1, keepdims=True)
    xc = x - mu
    var = jnp.mean(xc * xc, axis=-1, keepdims=True)
    rs = lax.rsqrt(var + EPS)
    return xc * rs, rs


def _ln_bwd(dy, xh, rs, g):
    dxh = dy * g
    m1 = jnp.mean(dxh, axis=-1, keepdims=True)
    m2 = jnp.mean(dxh * xh, axis=-1, keepdims=True)
    return rs * (dxh - m1 - xh * m2)


def _row_ids(i):
    return i * BLOCK + lax.broadcasted_iota(jnp.int32, (BLOCK, 1), 0)


def emb_inproj(x, metablk, g, b, w_in, nb):
    def body(x_ref, mb_ref, g_ref, b_ref, w_ref, h0_ref, h0b_ref, pa_ref, pg_ref):
        i = pl.program_id(0)
        xb = jnp.where(i == 0, mb_ref[...], x_ref[...])
        xh, _ = _ln_stats(xb)
        y = xh * g_ref[...] + b_ref[...]
        y = jnp.where(_row_ids(i) >= PAD, y, 0.0)
        h0_ref[...] = y
        yb = y.astype(bf16)
        h0b_ref[...] = yb
        pa_ref[...] = jnp.dot(yb, w_ref[:, :N_A], preferred_element_type=f32)
        pg_ref[...] = jnp.dot(yb, w_ref[:, N_A:], preferred_element_type=f32)

    p = nb * BLOCK
    row = lambda n: pl.BlockSpec((BLOCK, n), lambda i: (i, 0))
    return pl.pallas_call(
        body, name="emb_inproj", grid=(nb,),
        in_specs=[pl.BlockSpec((BLOCK, D_MODEL), lambda i: (jnp.maximum(i - 1, 0), 0)),
                  _const_spec((BLOCK, D_MODEL)), _const_spec((1, D_MODEL)), _const_spec((1, D_MODEL)),
                  _const_spec((D_MODEL, IN_W))],
        out_specs=[row(D_MODEL), row(D_MODEL), row(N_A), row(N_G)],
        out_shape=[jax.ShapeDtypeStruct((p, D_MODEL), f32), jax.ShapeDtypeStruct((p, D_MODEL), bf16),
                   jax.ShapeDtypeStruct((p, N_A), f32), jax.ShapeDtypeStruct((p, N_G), f32)],
        compiler_params=_cparams(("parallel",)),
    )(x, metablk, g, b, w_in)


def _hgrn_chunk(valid, st, hq, hf, hi, hg, lbraw, ng):
    lb = _sigmoid(lbraw[0:1] - lbraw[1:2])
    q = hq * _sigmoid(hq)
    fg = lb + (1.0 - lb) * _sigmoid(hf)
    logf = jnp.where(valid, jnp.log(fg), 0.0)
    k = jnp.where(valid, 1.0 - fg, 0.0)
    v = hi
    r = lax.broadcasted_iota(jnp.int32, (BLOCK, BLOCK), 0)
    c = lax.broadcasted_iota(jnp.int32, (BLOCK, BLOCK), 1)
    tril = (c <= r).astype(f32)
    bcum = jnp.dot(tril, logf, precision=lax.Precision.HIGHEST, preferred_element_type=f32)
    blast = bcum[BLOCK - 1:BLOCK]
    o = mm_nt(q * jnp.exp(bcum), st)
    t3 = lax.broadcasted_iota(jnp.int32, (SUB, SUB, HG_K), 0)
    s3 = lax.broadcasted_iota(jnp.int32, (SUB, SUB, HG_K), 1)
    tri3 = s3 <= t3
    pieces = []
    for n in range(BLOCK // SUB):
        r0 = n * SUB
        bs, qs, ks, vs = bcum[r0:r0 + SUB], q[r0:r0 + SUB], k[r0:r0 + SUB], v[r0:r0 + SUB]
        dec = jnp.exp(jnp.where(tri3, bs[:, None, :] - bs[None, :, :], -jnp.inf))
        a_diag = jnp.sum(qs[:, None, :] * ks[None, :, :] * dec, axis=-1)
        od = mm(a_diag, vs)
        if n > 0:
            bref = bcum[r0 - 1:r0]
            q_off = qs * jnp.exp(bs - bref)
            k_off = k * jnp.exp(jnp.where(r < r0, bref - bcum, -jnp.inf))
            od = od + mm(mm_nt(q_off, k_off), v)
        pieces.append(od)
    o = o + jnp.concatenate(pieces, axis=0)
    st_new = st * jnp.exp(blast) + mm_tn(v, k * jnp.exp(blast - bcum))
    on = o * lax.rsqrt(jnp.mean(o * o, axis=-1, keepdims=True) + EPS) * ng
    return st_new, on * (hg * _sigmoid(hg))


def _hgrn_in_specs(rowmap):
    wide = lambda col: pl.BlockSpec((BLOCK, HG_HEADS * HG_K), lambda i: (rowmap(i), col))
    return [wide(0), wide(1), wide(2), wide(3), _const_spec((2, HG_HEADS * HG_K)), _const_spec((1, HG_K))]


def _head(ref, h):
    return ref[:, h * HG_K:(h + 1) * HG_K]


def hgrn_fwd(pa, lbraw, ng, nb):
    def body(hq_ref, hf_ref, hi_ref, hg_ref, lb_ref, ng_ref, og_ref, sp_ref, st_ref):
        i = pl.program_id(0)

        @pl.when(i == 0)
        def _():
            st_ref[...] = jnp.zeros_like(st_ref)

        valid = _row_ids(i) >= PAD
        for h in range(HG_HEADS):
            st = st_ref[h]
            sp_ref[0, h] = st
            st_new, out = _hgrn_chunk(valid, st, _head(hq_ref, h), _head(hf_ref, h), _head(hi_ref, h),
                                      _head(hg_ref, h), _head(lb_ref, h), ng_ref[...])
            st_ref[h] = st_new
            og_ref[:, h * HG_K:(h + 1) * HG_K] = out.astype(bf16)

    p = nb * BLOCK
    return pl.pallas_call(
        body, name="hgrn_fwd", grid=(nb,),
        in_specs=_hgrn_in_specs(lambda i: i),
        out_specs=[pl.BlockSpec((BLOCK, HG_HEADS * HG_K), lambda i: (i, 0)),
                   pl.BlockSpec((1, HG_HEADS, HG_K, HG_K), lambda i: (i, 0, 0, 0))],
        out_shape=[jax.ShapeDtypeStruct((p, HG_HEADS * HG_K), bf16),
                   jax.ShapeDtypeStruct((nb, HG_HEADS, HG_K, HG_K), f32)],
        scratch_shapes=[pltpu.VMEM((HG_HEADS, HG_K, HG_K), f32)],
        compiler_params=_cparams(("arbitrary",)),
    )(pa, pa, pa, pa, lbraw, ng)


def hgrn_bwd(pa, lbraw, ng, sprev, dog, nb):
    def body(hq_ref, hf_ref, hi_ref, hg_ref, lb_ref, ng_ref, sp_ref, do_ref,
             dq_ref, df_ref, di_ref, dg_ref, dlb_ref, dng_ref, dst_ref):
        i = pl.program_id(0)

        @pl.when(i == 0)
        def _():
            dst_ref[...] = jnp.zeros_like(dst_ref)
            dlb_ref[...] = jnp.zeros_like(dlb_ref)
            dng_ref[...] = jnp.zeros_like(dng_ref)

        valid = _row_ids(nb - 1 - i) >= PAD
        dng_sum = jnp.zeros((1, HG_K), f32)
        for h in range(HG_HEADS):
            cols = slice(h * HG_K, (h + 1) * HG_K)
            _, vjp = jax.vjp(functools.partial(_hgrn_chunk, valid), sp_ref[0, h], _head(hq_ref, h), _head(hf_ref, h),
                             _head(hi_ref, h), _head(hg_ref, h), _head(lb_ref, h), ng_ref[...])
            dst, dq, df, di, dg, dlb, dng = vjp((dst_ref[h], _head(do_ref, h)))
            dst_ref[h] = dst
            dq_ref[:, cols] = dq.astype(bf16)
            df_ref[:, cols] = df.astype(bf16)
            di_ref[:, cols] = di.astype(bf16)
            dg_ref[:, cols] = dg.astype(bf16)
            dlb_ref[:, cols] += dlb
            dng_sum = dng_sum + dng
        dng_ref[...] += dng_sum

    p = nb * BLOCK
    rev = lambda i: nb - 1 - i
    hw = HG_HEADS * HG_K
    blk = pl.BlockSpec((BLOCK, hw), lambda i: (rev(i), 0))
    wide = jax.ShapeDtypeStruct((p, hw), bf16)
    return pl.pallas_call(
        body, name="hgrn_bwd", grid=(nb,),
        in_specs=_hgrn_in_specs(rev) + [pl.BlockSpec((1, HG_HEADS, HG_K, HG_K), lambda i: (rev(i), 0, 0, 0)), blk],
        out_specs=[blk, blk, blk, blk, pl.BlockSpec((2, hw), lambda i: (0, 0)), pl.BlockSpec((1, HG_K), lambda i: (0, 0))],
        out_shape=[wide, wide, wide, wide, jax.ShapeDtypeStruct((2, hw), f32), jax.ShapeDtypeStruct((1, HG_K), f32)],
        scratch_shapes=[pltpu.VMEM((HG_HEADS, HG_K, HG_K), f32)],
        compiler_params=_cparams(("arbitrary",)),
    )(pa, pa, pa, pa, lbraw, ng, sprev, dog)


def _rope(x, cos, sin):
    lane = lax.broadcasted_iota(jnp.int32, x.shape, 1)
    rot = jnp.where(lane % HEAD_DIM < HEAD_DIM // 2, -roll_lanes(x, BLOCK - HEAD_DIM // 2),
                    roll_lanes(x, HEAD_DIM // 2))
    return x * cos + rot * sin


def _both_halves(x, g):
    lo = lax.broadcasted_iota(jnp.int32, x.shape, 1) < HEAD_DIM
    sw = roll_lanes(x, HEAD_DIM)
    return jnp.where(lo, x, sw) if g == 0 else jnp.where(lo, sw, x)


def _attn_block(band_ok, meta_ok, tabs, q, kp, kc, vp, vc, km, vm, *sinks):
    cq, sq, cp, sp, cm, sm = tabs
    neg = jnp.finfo(f32).min
    kk = jnp.concatenate([_rope(kp, cp, sp), _rope(kc, cq, sq)], axis=0)
    vv = jnp.concatenate([vp, vc], axis=0)
    kmr = _rope(km, cm, sm)
    lo = lax.broadcasted_iota(jnp.int32, (BLOCK, BLOCK), 1) < HEAD_DIM
    kv = [(_both_halves(kk, g), _both_halves(vv, g), _both_halves(kmr, g), _both_halves(vm, g)) for g in range(2)]
    slabs = []
    for m in range(ATT_HEADS // 2):
        qr = _rope(q[:, m * BLOCK:(m + 1) * BLOCK], cq, sq)
        kk_g, vv_g, km_g, vm_g = kv[m // 2]
        outs = []
        for half in range(2):
            sink = sinks[2 * m + half]
            qm = jnp.where(lo if half == 0 else ~lo, qr, 0.0)
            sb = jnp.where(band_ok, mm_nt(qm, kk_g) * (HEAD_DIM ** -0.5), neg)
            sme = jnp.where(meta_ok, mm_nt(qm, km_g) * (HEAD_DIM ** -0.5), neg)
            mx = jnp.maximum(jnp.maximum(jnp.max(sb, axis=-1, keepdims=True),
                                         jnp.max(sme, axis=-1, keepdims=True)), sink)
            eb, em = jnp.exp(sb - mx), jnp.exp(sme - mx)
            den = jnp.sum(eb, axis=-1, keepdims=True) + jnp.sum(em, axis=-1, keepdims=True) + jnp.exp(sink - mx)
            outs.append(mm(eb / den, vv_g) + mm(em / den, vm_g))
        slabs.append(jnp.where(lo, outs[0], outs[1]))
    return jnp.concatenate(slabs, axis=1)


def _attn_masks(i):
    qpos = _row_ids(i) - PAD
    s = lax.broadcasted_iota(jnp.int32, (1, 2 * BLOCK), 1)
    kpos = jnp.where(s < BLOCK, jnp.where(i > 0, (i - 1) * BLOCK - PAD + s, -1), i * BLOCK - PAD + s - BLOCK)
    band_ok = (kpos >= N_META) & (kpos <= qpos) & (qpos - kpos < BLOCK)
    meta_ok = lax.broadcasted_iota(jnp.int32, (1, N_META), 1) <= qpos
    return band_ok, meta_ok


def _attn_in_specs():
    prev = lambda i: jnp.maximum(i - 1, 0)
    kcol, vcol = N_A // BLOCK - 2, N_A // BLOCK - 1
    blk = lambda rowmap, col: pl.BlockSpec((BLOCK, BLOCK), lambda i: (rowmap(i), col))
    cur, first = (lambda i: i), (lambda i: 0)
    return [pl.BlockSpec((BLOCK, ATT_QW), lambda i: (i, 4)),
            blk(prev, kcol), blk(cur, kcol), blk(prev, vcol), blk(cur, vcol), blk(first, kcol), blk(first, vcol),
            blk(cur, 0), blk(cur, 0), blk(prev, 0), blk(prev, 0), blk(first, 0), blk(first, 0),
            _const_spec((ATT_HEADS, BLOCK))]


def _attn_operands(q_ref, kp_ref, kc_ref, vp_ref, vc_ref, km_ref, vm_ref, cq, sq, cp, sp, cm, sm, sk_ref):
    tabs = (cq[...], sq[...], cp[...], sp[...], cm[PAD:, :], sm[PAD:, :])
    args = (q_ref[...], kp_ref[...], kc_ref[...], vp_ref[...], vc_ref[...], km_ref[PAD:, :], vm_ref[PAD:, :])
    sinks = tuple(sk_ref[j:j + 1, 0:1] for j in range(ATT_HEADS))
    return tabs, args + sinks


def attn_fwd(pa, cos, sin, sinks8, nb):
    def body(*refs):
        o_ref = refs[-1]
        band_ok, meta_ok = _attn_masks(pl.program_id(0))
        tabs, args = _attn_operands(*refs[:-1])
        o_ref[...] = _attn_block(band_ok, meta_ok, tabs, *args).astype(bf16)

    return pl.pallas_call(
        body, name="attn_fwd", grid=(nb,), in_specs=_attn_in_specs(),
        out_specs=pl.BlockSpec((BLOCK, ATT_QW), lambda i: (i, 0)),
        out_shape=jax.ShapeDtypeStruct((nb * BLOCK, ATT_QW), bf16),
        compiler_params=_cparams(("parallel",)),
    )(pa, pa, pa, pa, pa, pa, pa, cos, sin, cos, sin, cos, sin, sinks8)


def attn_bwd(pa, cos, sin, sinks8, do, nb):
    def body(*refs):
        do_ref = refs[14]
        dq_ref, dkc_ref, dkp_ref, dvc_ref, dvp_ref, dkm_ref, dvm_ref, dsk_ref = refs[15:]
        i = pl.program_id(0)

        @pl.when(i == 0)
        def _():
            dkm_ref[...] = jnp.zeros((N_META, BLOCK), f32)
            dvm_ref[...] = jnp.zeros((N_META, BLOCK), f32)
            dsk_ref[...] = jnp.zeros((ATT_HEADS, BLOCK), f32)

        band_ok, meta_ok = _attn_masks(i)
        tabs, args = _attn_operands(*refs[:14])
        _, vjp = jax.vjp(functools.partial(_attn_block, band_ok, meta_ok, tabs), *args)
        grads = vjp(do_ref[...])
        dq_ref[...] = grads[0].astype(bf16)
        dkp_ref[...] = grads[1]
        dkc_ref[...] = grads[2]
        dvp_ref[...] = grads[3]
        dvc_ref[...] = grads[4]
        dkm_ref[...] += grads[5]
        dvm_ref[...] += grads[6]
        for j in range(ATT_HEADS):
            dsk_ref[j:j + 1, :] += jnp.broadcast_to(grads[7 + j], (1, BLOCK))

    p = nb * BLOCK
    row = pl.BlockSpec((BLOCK, BLOCK), lambda i: (i, 0))
    const = lambda r: pl.BlockSpec((r, BLOCK), lambda i: (0, 0))
    part = jax.ShapeDtypeStruct((p, BLOCK), f32)
    return pl.pallas_call(
        body, name="attn_bwd", grid=(nb,),
        in_specs=_attn_in_specs() + [pl.BlockSpec((BLOCK, ATT_QW), lambda i: (i, 0))],
        out_specs=[pl.BlockSpec((BLOCK, ATT_QW), lambda i: (i, 0)), row, row, row, row,
                   const(N_META), const(N_META), const(ATT_HEADS)],
        out_shape=[jax.ShapeDtypeStruct((p, ATT_QW), bf16), part, part, part, part,
                   jax.ShapeDtypeStruct((N_META, BLOCK), f32), jax.ShapeDtypeStruct((N_META, BLOCK), f32),
                   jax.ShapeDtypeStruct((ATT_HEADS, BLOCK), f32)],
        compiler_params=_cparams(("arbitrary",)),
    )(pa, pa, pa, pa, pa, pa, pa, cos, sin, cos, sin, cos, sin, sinks8, do)


def mid_rows(h0, pg, og, oatt, target, wbh, wba, wout, wfi, wfo, ln1g, ln1b, ln2g, ln2b, nb):
    def body(h0_ref, pg_ref, og_ref, oa_ref, t_ref, wbh_ref, wba_ref, wo_ref, wfi_ref, wfo_ref,
             g1_ref, b1_ref, g2_ref, b2_ref,
             dh0_ref, dpg_ref, dog_ref, doa_ref, dyh_ref, dya_ref, mix_ref, dr1_ref, h1b_ref, dau_ref, s_ref, dr2_ref,
             loss_ref, dg1_ref, db1_ref, dg2_ref, db2_ref):
        i = pl.program_id(0)

        @pl.when(i == 0)
        def _():
            loss_ref[...] = jnp.zeros_like(loss_ref)
            for r in (dg1_ref, db1_ref, dg2_ref, db2_ref):
                r[...] = jnp.zeros_like(r)

        g1, b1, g2, b2 = g1_ref[...], b1_ref[...], g2_ref[...], b2_ref[...]
        yh = jnp.dot(og_ref[...], wbh_ref[...], preferred_element_type=f32)
        ya = jnp.dot(oa_ref[...], wba_ref[...], preferred_element_type=f32)
        gh = _sigmoid(pg_ref[:, :D_MODEL])
        ga = _sigmoid(pg_ref[:, D_MODEL:])
        mixin = (gh * yh + ga * ya).astype(bf16)
        mix_ref[...] = mixin
        r1 = ALPHA * h0_ref[...] + jnp.dot(mixin, wo_ref[...], preferred_element_type=f32)
        xh1, rs1 = _ln_stats(r1)
        h1 = xh1 * g1 + b1
        h1b = h1.astype(bf16)
        h1b_ref[...] = h1b
        au = jnp.dot(h1b, wfi_ref[...], preferred_element_type=f32)
        a, u = au[:, :D_FF], au[:, D_FF:]
        sg = _sigmoid(a)
        sa = a * sg
        s = (sa * u).astype(bf16)
        s_ref[...] = s
        r2 = ALPHA * h1 + jnp.dot(s, wfo_ref[...], preferred_element_type=f32)
        xh2, rs2 = _ln_stats(r2)
        diff = jnp.where(i > 0, xh2 * g2 + b2 - t_ref[...], 0.0)
        loss_ref[...] += jnp.sum(diff * diff) * (0.5 / D_MODEL)
        dy = diff * (1.0 / D_MODEL)
        dg2_ref[...] += jnp.sum(dy * xh2, axis=0, keepdims=True)
        db2_ref[...] += jnp.sum(dy, axis=0, keepdims=True)
        dr2 = _ln_bwd(dy, xh2, rs2, g2)
        dr2b = dr2.astype(bf16)
        dr2_ref[...] = dr2b
        ds = _dot(dr2b, wfo_ref[...], 1, 1)
        da = (ds * u) * (sg * (1.0 + a * (1.0 - sg)))
        du = ds * sa
        dau = jnp.concatenate([da, du], axis=1).astype(bf16)
        dau_ref[...] = dau
        dh1 = ALPHA * dr2 + _dot(dau, wfi_ref[...], 1, 1)
        dg1_ref[...] += jnp.sum(dh1 * xh1, axis=0, keepdims=True)
        db1_ref[...] += jnp.sum(dh1, axis=0, keepdims=True)
        dr1 = _ln_bwd(dh1, xh1, rs1, g1)
        dr1b = dr1.astype(bf16)
        dr1_ref[...] = dr1b
        dh0_ref[...] = ALPHA * dr1
        dmix = _dot(dr1b, wo_ref[...], 1, 1)
        dyh = (dmix * gh).astype(bf16)
        dya = (dmix * ga).astype(bf16)
        dyh_ref[...] = dyh
        dya_ref[...] = dya
        dpg_ref[:, :D_MODEL] = (dmix * yh * gh * (1.0 - gh)).astype(bf16)
        dpg_ref[:, D_MODEL:] = (dmix * ya * ga * (1.0 - ga)).astype(bf16)
        dog_ref[...] = _dot(dyh, wbh_ref[...], 1, 1)
        doa_ref[...] = _dot(dya, wba_ref[...], 1, 1)

    p = nb * BLOCK
    row = lambda n: pl.BlockSpec((BLOCK, n), lambda i: (i, 0))
    vec = lambda: pl.BlockSpec((1, D_MODEL), lambda i: (0, 0))
    sds = lambda n, dt: jax.ShapeDtypeStruct((p, n), dt)
    hw = HG_HEADS * HG_K
    return pl.pallas_call(
        body, name="mid_rows", grid=(nb,),
        in_specs=[row(D_MODEL), row(N_G), row(hw), row(ATT_QW),
                  pl.BlockSpec((BLOCK, D_MODEL), lambda i: (jnp.maximum(i - 1, 0), 0)),
                  _const_spec((hw, D_MODEL)), _const_spec((ATT_QW, D_MODEL)), _const_spec((D_MODEL, D_MODEL)),
                  _const_spec((D_MODEL, 2 * D_FF)), _const_spec((D_FF, D_MODEL)),
                  _const_spec((1, D_MODEL)), _const_spec((1, D_MODEL)), _const_spec((1, D_MODEL)),
                  _const_spec((1, D_MODEL))],
        out_specs=[row(D_MODEL), row(N_G), row(hw), row(ATT_QW), row(D_MODEL), row(D_MODEL), row(D_MODEL),
                   row(D_MODEL), row(D_MODEL), row(2 * D_FF), row(D_FF), row(D_MODEL),
                   pl.BlockSpec((1, 1), lambda i: (0, 0)), vec(), vec(), vec(), vec()],
        out_shape=[sds(D_MODEL, f32), sds(N_G, bf16), sds(hw, f32), sds(ATT_QW, f32), sds(D_MODEL, bf16),
                   sds(D_MODEL, bf16), sds(D_MODEL, bf16), sds(D_MODEL, bf16), sds(D_MODEL, bf16),
                   sds(2 * D_FF, bf16), sds(D_FF, bf16), sds(D_MODEL, bf16),
                   jax.ShapeDtypeStruct((1, 1), f32)] + [jax.ShapeDtypeStruct((1, D_MODEL), f32)] * 4,
        compiler_params=_cparams(("arbitrary",)),
    )(h0, pg, og, oatt, target, wbh, wba, wout, wfi, wfo, ln1g, ln1b, ln2g, ln2b)


def inproj_bwd(dh0p, dhq, dhf, dhi, dhg, daq, dkc, dkp, dvc, dvp, dkm, dvm, dpg, w_in, x, metablk, g, b, nb):
    def body(dh0_ref, dq_ref, df_ref, di_ref, dg_ref, daq_ref, dkc_ref, dkp_ref, dvc_ref, dvp_ref, dkm_ref, dvm_ref,
             dpg_ref, w_ref, x_ref, mb_ref, g_ref, b_ref, dproj_ref, dx_ref, dlg_ref, dlb_ref):
        i = pl.program_id(0)

        @pl.when(i == 0)
        def _():
            dlg_ref[...] = jnp.zeros_like(dlg_ref)
            dlb_ref[...] = jnp.zeros_like(dlb_ref)

        zero_pad = jnp.zeros((PAD, BLOCK), f32)
        has_next = i + 1 < nb
        first = i == 0

        def keys(cur_ref, next_ref, meta_ref):
            t = cur_ref[...] + jnp.where(has_next, next_ref[...], 0.0)
            return t + jnp.where(first, jnp.concatenate([zero_pad, meta_ref[...]], axis=0), 0.0)

        dproj = jnp.concatenate(
            [dq_ref[...], df_ref[...], di_ref[...], dg_ref[...], daq_ref[...],
             keys(dkc_ref, dkp_ref, dkm_ref).astype(bf16), keys(dvc_ref, dvp_ref, dvm_ref).astype(bf16),
             dpg_ref[...]], axis=1)
        dproj_ref[...] = dproj
        valid = _row_ids(i) >= PAD
        dh0 = jnp.where(valid, dh0_ref[...] + _dot(dproj, w_ref[...], 1, 1), 0.0)
        xb = jnp.where(first, mb_ref[...], x_ref[...])
        xh, rs = _ln_stats(xb)
        dlg_ref[...] += jnp.sum(dh0 * xh, axis=0, keepdims=True)
        dlb_ref[...] += jnp.sum(dh0, axis=0, keepdims=True)
        dx_ref[...] = jnp.where(valid, _ln_bwd(dh0, xh, rs, g_ref[...]), 0.0)

    p = nb * BLOCK
    row = lambda n: pl.BlockSpec((BLOCK, n), lambda i: (i, 0))
    nxt = pl.BlockSpec((BLOCK, BLOCK), lambda i: (jnp.minimum(i + 1, nb - 1), 0))
    hw = HG_HEADS * HG_K
    vec = lambda: pl.BlockSpec((1, D_MODEL), lambda i: (0, 0))
    return pl.pallas_call(
        body, name="inproj_bwd", grid=(nb,),
        in_specs=[row(D_MODEL), row(hw), row(hw), row(hw), row(hw), row(ATT_QW),
                  row(BLOCK), nxt, row(BLOCK), nxt, _const_spec((N_META, BLOCK)), _const_spec((N_META, BLOCK)),
                  row(N_G), _const_spec((D_MODEL, IN_W)),
                  pl.BlockSpec((BLOCK, D_MODEL), lambda i: (jnp.maximum(i - 1, 0), 0)),
                  _const_spec((BLOCK, D_MODEL)), _const_spec((1, D_MODEL)), _const_spec((1, D_MODEL))],
        out_specs=[row(IN_W), row(D_MODEL), vec(), vec()],
        out_shape=[jax.ShapeDtypeStruct((p, IN_W), bf16), jax.ShapeDtypeStruct((p, D_MODEL), f32),
                   jax.ShapeDtypeStruct((1, D_MODEL), f32), jax.ShapeDtypeStruct((1, D_MODEL), f32)],
        compiler_params=_cparams(("arbitrary",)),
    )(dh0p, dhq, dhf, dhi, dhg, daq, dkc, dkp, dvc, dvp, dkm, dvm, dpg, w_in, x, metablk, g, b)


def wgrad(a, b, name, tk, tn, tp, by_cols):
    p, k = a.shape
    n = b.shape[1]
    nsteps = p // tp

    def body(a_ref, b_ref, o_ref):
        @pl.when(pl.program_id(2) == 0)
        def _():
            o_ref[...] = jnp.zeros_like(o_ref)

        o_ref[0] += _dot(a_ref[...], b_ref[...], 0, 0)

    if by_cols:
        shard_n = n // N_SHARD
        per = shard_n // tn
        out_shape = (N_SHARD, k, shard_n)
        omap = lambda ik, jn, ip: (jn // per, ik, jn % per)
    else:
        out_shape = (1, k, n)
        omap = lambda ik, jn, ip: (0, ik, jn)
    return pl.pallas_call(
        body, name=name, grid=(k // tk, n // tn, nsteps),
        in_specs=[pl.BlockSpec((tp, tk), lambda ik, jn, ip: (ip, ik)),
                  pl.BlockSpec((tp, tn), lambda ik, jn, ip: (ip, jn))],
        out_specs=pl.BlockSpec((1, tk, tn), omap),
        out_shape=jax.ShapeDtypeStruct(out_shape, f32),
        compiler_params=_cparams(("parallel", "parallel", "arbitrary")),
    )(a, b)


def adamw(w, g, m, v, name):
    r, c = w.shape
    tr = r
    for cand in (256, 176, 128):
        if r > cand and r % cand == 0:
            tr = cand
            break

    def body(w_ref, g_ref, m_ref, v_ref, d_ref, mo_ref, vo_ref):
        gg = g_ref[...]
        mn = ADAM_B1 * m_ref[...] + (1.0 - ADAM_B1) * gg
        vn = ADAM_B2 * v_ref[...] + (1.0 - ADAM_B2) * (gg * gg)
        m_hat = mn / (1.0 - ADAM_B1 ** ADAM_STEP)
        v_hat = vn / (1.0 - ADAM_B2 ** ADAM_STEP)
        d_ref[...] = -ADAM_LR * (m_hat / (jnp.sqrt(v_hat) + ADAM_EPS) + ADAM_WD * w_ref[...])
        mo_ref[...] = mn
        vo_ref[...] = vn

    spec = pl.BlockSpec((tr, c), lambda i: (i, 0))
    sds = jax.ShapeDtypeStruct((r, c), f32)
    return pl.pallas_call(
        body, name=name, grid=(r // tr,), in_specs=[spec] * 4, out_specs=[spec] * 3, out_shape=[sds] * 3,
        compiler_params=_cparams(("parallel",)),
    )(w, g, m, v)


def _me():
    return lax.axis_index("x"), lax.axis_index("y"), lax.axis_index("c")


def _chip_peer(x, y, c, k):
    return (x ^ (k >> 1), y ^ (k & 1), c)


ANY = pl.BlockSpec(memory_space=pl.ANY)


def gather_weights(shards):
    n = len(shards)
    out_dtypes = [bf16 if s.size > 16 * 256 else f32 for s in shards]

    def body(*refs):
        ins, outs = refs[:n], refs[n:2 * n]
        stage = refs[2 * n:3 * n]
        send_sems, recv_sems, local_sems = refs[3 * n:]
        x, y, c = _me()
        j = 2 * x + y
        for w in range(n):
            stage[w][...] = ins[w][...].astype(out_dtypes[w])
        sends, locs = [], []
        for w in range(n):
            loc = pltpu.make_async_copy(stage[w], outs[w].at[j], local_sems.at[w])
            loc.start()
            locs.append(loc)
            for k in (1, 2, 3):
                cp = pltpu.make_async_remote_copy(
                    src_ref=stage[w], dst_ref=outs[w].at[j], send_sem=send_sems.at[w, k - 1],
                    recv_sem=recv_sems.at[w, k - 1], device_id=_chip_peer(x, y, c, k), device_id_type=MESH)
                cp.start()
                sends.append(cp)
        for w in range(n):
            for k in (1, 2, 3):
                pltpu.make_async_remote_copy(
                    src_ref=stage[w], dst_ref=outs[w].at[j ^ k], send_sem=send_sems.at[w, k - 1],
                    recv_sem=recv_sems.at[w, k - 1], device_id=_chip_peer(x, y, c, k), device_id_type=MESH).wait_recv()
        for cp in sends:
            cp.wait_send()
        for loc in locs:
            loc.wait()

    return pl.pallas_call(
        body, name="gather_weights",
        in_specs=[pl.BlockSpec(memory_space=pltpu.VMEM)] * n, out_specs=[ANY] * n,
        out_shape=[jax.ShapeDtypeStruct((N_SHARD,) + s.shape, dt) for s, dt in zip(shards, out_dtypes)],
        scratch_shapes=[pltpu.VMEM(s.shape, dt) for s, dt in zip(shards, out_dtypes)]
        + [pltpu.SemaphoreType.DMA((n, 3)), pltpu.SemaphoreType.DMA((n, 3)), pltpu.SemaphoreType.DMA((n,))],
        compiler_params=pltpu.CompilerParams(vmem_limit_bytes=VMEM_LIMIT),
    )(*shards)


def pair_exchange_halves(grads, small):
    n = len(grads)

    def body(*refs):
        ins, small_ref = refs[:n], refs[n]
        outs, gath = refs[n + 1:2 * n + 1], refs[2 * n + 1]
        send_sems, recv_sems, s_send, s_recv, local_sem = refs[2 * n + 2:]
        x, y, c = _me()
        me = 4 * x + 2 * y + c
        sends = []
        for w in range(n):
            half = ins[w].shape[1] // 2
            cp = pltpu.make_async_remote_copy(
                src_ref=ins[w].at[:, pl.ds((1 - c) * half, half), :], dst_ref=outs[w],
                send_sem=send_sems.at[w], recv_sem=recv_sems.at[w], device_id=(x, y, 1 - c), device_id_type=MESH)
            cp.start()
            sends.append(cp)
        loc = pltpu.make_async_copy(small_ref, gath.at[me], local_sem)
        loc.start()
        for k in range(1, N_DEV):
            cp = pltpu.make_async_remote_copy(
                src_ref=small_ref, dst_ref=gath.at[me], send_sem=s_send.at[k - 1], recv_sem=s_recv.at[k - 1],
                device_id=(x ^ (k >> 2), y ^ ((k >> 1) & 1), c ^ (k & 1)), device_id_type=MESH)
            cp.start()
            sends.append(cp)
        for w in range(n):
            half = ins[w].shape[1] // 2
            pltpu.make_async_remote_copy(
                src_ref=ins[w].at[:, pl.ds(0, half), :], dst_ref=outs[w], send_sem=send_sems.at[w],
                recv_sem=recv_sems.at[w], device_id=(x, y, 1 - c), device_id_type=MESH).wait_recv()
        for k in range(1, N_DEV):
            pltpu.make_async_remote_copy(
                src_ref=small_ref, dst_ref=gath.at[me ^ k], send_sem=s_send.at[k - 1], recv_sem=s_recv.at[k - 1],
                device_id=(x ^ (k >> 2), y ^ ((k >> 1) & 1), c ^ (k & 1)), device_id_type=MESH).wait_recv()
        for cp in sends:
            cp.wait_send()
        loc.wait()

    return pl.pallas_call(
        body, name="pair_exchange_halves", in_specs=[ANY] * (n + 1), out_specs=[ANY] * (n + 1),
        out_shape=[jax.ShapeDtypeStruct((g.shape[0], g.shape[1] // 2, g.shape[2]), f32) for g in grads]
        + [jax.ShapeDtypeStruct((N_DEV,) + small.shape, f32)],
        scratch_shapes=[pltpu.SemaphoreType.DMA((n,)), pltpu.SemaphoreType.DMA((n,)),
                        pltpu.SemaphoreType.DMA((N_DEV - 1,)), pltpu.SemaphoreType.DMA((N_DEV - 1,)),
                        pltpu.SemaphoreType.DMA],
    )(*grads, small)


def chip_exchange(sums):
    n = len(sums)

    def body(*refs):
        ins, outs = refs[:n], refs[n:2 * n]
        send_sems, recv_sems = refs[2 * n:]
        x, y, c = _me()
        j = 2 * x + y
        sends = []
        for w in range(n):
            for k in (1, 2, 3):
                cp = pltpu.make_async_remote_copy(
                    src_ref=ins[w].at[j ^ k], dst_ref=outs[w].at[k - 1], send_sem=send_sems.at[w, k - 1],
                    recv_sem=recv_sems.at[w, k - 1], device_id=_chip_peer(x, y, c, k), device_id_type=MESH)
                cp.start()
                sends.append(cp)
        for w in range(n):
            for k in (1, 2, 3):
                pltpu.make_async_remote_copy(
                    src_ref=ins[w].at[0], dst_ref=outs[w].at[k - 1], send_sem=send_sems.at[w, k - 1],
                    recv_sem=recv_sems.at[w, k - 1], device_id=_chip_peer(x, y, c, k), device_id_type=MESH).wait_recv()
        for cp in sends:
            cp.wait_send()

    return pl.pallas_call(
        body, name="chip_exchange", in_specs=[ANY] * n, out_specs=[ANY] * n,
        out_shape=[jax.ShapeDtypeStruct((N_SHARD - 1,) + s.shape[1:], s.dtype) for s in sums],
        scratch_shapes=[pltpu.SemaphoreType.DMA((n, 3)), pltpu.SemaphoreType.DMA((n, 3))],
    )(*sums)


def pair_exchange_results(halves):
    n = len(halves)

    def body(*refs):
        ins, outs = refs[:n], refs[n:2 * n]
        send_sems, recv_sems = refs[2 * n:]
        x, y, c = _me()
        sends = []
        for w in range(n):
            cp = pltpu.make_async_remote_copy(
                src_ref=ins[w].at[c], dst_ref=outs[w].at[c], send_sem=send_sems.at[w], recv_sem=recv_sems.at[w],
                device_id=(x, y, 1 - c), device_id_type=MESH)
            cp.start()
            sends.append(cp)
        for w in range(n):
            pltpu.make_async_remote_copy(
                src_ref=ins[w].at[c], dst_ref=outs[w].at[1 - c], send_sem=send_sems.at[w],
                recv_sem=recv_sems.at[w], device_id=(x, y, 1 - c), device_id_type=MESH).wait_recv()
        for cp in sends:
            cp.wait_send()

    return pl.pallas_call(
        body, name="pair_exchange_results", in_specs=[ANY] * n, out_specs=[ANY] * n,
        out_shape=[jax.ShapeDtypeStruct(h.shape, f32) for h in halves],
        input_output_aliases={w: w for w in range(n)},
        scratch_shapes=[pltpu.SemaphoreType.DMA((n,)), pltpu.SemaphoreType.DMA((n,))],
    )(*halves)


def add_pair(grad, other, c_idx, name):
    _, r, c = grad.shape
    half = r // 2
    tr = half // 2 if (half // 2) % 8 == 0 else half
    per = half // tr

    def body(c_ref, g_ref, o_ref, out_ref):
        out_ref[...] = (g_ref[...] + o_ref[...]).astype(bf16)

    return pl.pallas_call(
        body, name=name,
        grid_spec=pltpu.PrefetchScalarGridSpec(
            num_scalar_prefetch=1, grid=(N_SHARD, per),
            in_specs=[pl.BlockSpec((1, tr, c), lambda j, t, cr: (j, cr[0] * per + t, 0)),
                      pl.BlockSpec((1, tr, c), lambda j, t, cr: (j, t, 0))],
            out_specs=pl.BlockSpec((1, tr, c), lambda j, t, cr: (j, t, 0))),
        out_shape=jax.ShapeDtypeStruct((N_SHARD, half, c), bf16),
        compiler_params=_cparams(("parallel", "parallel")),
    )(c_idx, grad, other)


def add_four(own, parts, jc_idx, name):
    _, half, c = parts.shape
    tr = half // 2 if (half // 2) % 8 == 0 else half

    def body(jc_ref, own_ref, p_ref, out_ref):
        acc = own_ref[0].astype(f32)
        for k in range(N_SHARD - 1):
            acc = acc + p_ref[k].astype(f32)
        out_ref[0] = acc

    return pl.pallas_call(
        body, name=name,
        grid_spec=pltpu.PrefetchScalarGridSpec(
            num_scalar_prefetch=1, grid=(half // tr,),
            in_specs=[pl.BlockSpec((1, tr, c), lambda t, jc: (jc[0], t, 0)),
                      pl.BlockSpec((N_SHARD - 1, tr, c), lambda t, jc: (0, t, 0))],
            out_specs=pl.BlockSpec((1, tr, c), lambda t, jc: (jc[1], t, 0))),
        out_shape=jax.ShapeDtypeStruct((2, half, c), f32),
        compiler_params=_cparams(("parallel",)),
    )(jc_idx, own, parts)


def sum_devices(gathered):
    def body(g_ref, out_ref):
        acc = g_ref[0]
        for d in range(1, N_DEV):
            acc = acc + g_ref[d]
        out_ref[...] = acc

    return pl.pallas_call(body, name="sum_devices", out_shape=jax.ShapeDtypeStruct(gathered.shape[1:], f32))(gathered)


def _rows128(a, rows):
    flat = a.reshape(-1, BLOCK) if a.size % BLOCK == 0 else jnp.pad(a.reshape(1, -1), ((0, 0), (0, BLOCK - a.size)))
    return jnp.pad(flat, ((0, rows - flat.shape[0]), (0, 0)))


def kernel(x, meta_tokens, ln_emb_g, ln_emb_b, w_in, hg_lower_bounds, hg_norm_g, attn_sinks, w_branch_hg, w_branch_attn, w_out, ln1_g, ln1_b, w_ffn_in, w_ffn_out, ln2_g, ln2_b, loss_target, m_meta_tokens, m_ln_emb_g, m_ln_emb_b, m_w_in, m_hg_lower_bounds, m_hg_norm_g, m_attn_sinks, m_w_branch_hg, m_w_branch_attn, m_w_out, m_ln1_g, m_ln1_b, m_w_ffn_in, m_w_ffn_out, m_ln2_g, m_ln2_b, v_meta_tokens, v_ln_emb_g, v_ln_emb_b, v_w_in, v_hg_lower_bounds, v_hg_norm_g, v_attn_sinks, v_w_branch_hg, v_w_branch_attn, v_w_out, v_ln1_g, v_ln1_b, v_w_ffn_in, v_w_ffn_out, v_ln2_g, v_ln2_b):
    seq = x.shape[1]
    nb = seq // BLOCK + 1
    xs = x[0]
    ts = loss_target[0]
    ix, iy, ic = _me()
    shard = 2 * ix + iy
    vec = lambda a: a.reshape(1, D_MODEL)

    big = [w_in[0], w_branch_hg[0], w_branch_attn[0], w_out[0], w_ffn_in[0], w_ffn_out[0]]
    g_in, g_bh, g_ba, g_out, g_fi, g_fo, g_meta = gather_weights(big + [meta_tokens])
    by_cols = lambda g: g.transpose(1, 0, 2).reshape(g.shape[1], N_SHARD * g.shape[2])
    wf_in, wf_bh, wf_ba, wf_fi = by_cols(g_in), by_cols(g_bh), by_cols(g_ba), by_cols(g_fi)
    wf_out = g_out.reshape(D_MODEL, D_MODEL)
    wf_fo = g_fo.reshape(D_FF, D_MODEL)
    metablk = jnp.pad(by_cols(g_meta), ((PAD, 0), (0, 0)))

    pos = jnp.arange(nb * BLOCK, dtype=jnp.int32) - PAD
    half = HEAD_DIM // 2
    inv = ROPE_THETA ** (-jnp.arange(half, dtype=f32) / half)
    ang = pos.astype(f32)[:, None] * inv[None, :]
    cos = jnp.tile(jnp.cos(ang), (1, BLOCK // half))
    sin = jnp.tile(jnp.sin(ang), (1, BLOCK // half))
    sinks8 = jnp.broadcast_to(attn_sinks.reshape(ATT_HEADS, 1), (ATT_HEADS, BLOCK))
    ng = hg_norm_g.reshape(1, HG_K)

    h0, h0b, pa, pg = emb_inproj(xs, metablk, vec(ln_emb_g), vec(ln_emb_b), wf_in, nb)
    og, sprev = hgrn_fwd(pa, hg_lower_bounds, ng, nb)
    oatt = attn_fwd(pa, cos, sin, sinks8, nb)
    (dh0p, dpg, dog, doa, dyh, dya, mixin, dr1, h1b, dau, sact, dr2,
     loss_part, dg1, db1, dg2, db2) = mid_rows(h0, pg, og, oatt, ts, wf_bh, wf_ba, wf_out, wf_fi, wf_fo,
                                              ln1_g, ln1_b, ln2_g, ln2_b, nb)
    dhq, dhf, dhi, dhg, dlb4, dng = hgrn_bwd(pa, hg_lower_bounds, ng, sprev, dog, nb)
    daq, dkc, dkp, dvc, dvp, dkm, dvm, dsk = attn_bwd(pa, cos, sin, sinks8, doa, nb)
    dproj, dxp, dlg, dlb = inproj_bwd(dh0p, dhq, dhf, dhi, dhg, daq, dkc, dkp, dvc, dvp, dkm, dvm, dpg,
                                      wf_in, xs, metablk, vec(ln_emb_g), vec(ln_emb_b), nb)

    tp = BLOCK * (5 if nb % 5 == 0 else 1)
    gw_in = wgrad(h0b, dproj, "wgrad_in", D_MODEL, IN_W // 2, tp, False)
    gw_in = gw_in.reshape(D_MODEL, N_SHARD, IN_W // N_SHARD).transpose(1, 0, 2)
    gw_bh = wgrad(og, dyh, "wgrad_bh", 512, 256, tp, True)
    gw_ba = wgrad(oatt, dya, "wgrad_ba", 512, 256, tp, True)
    gw_out = wgrad(mixin, dr1, "wgrad_out", D_MODEL, D_MODEL, tp, False).reshape(N_SHARD, -1, D_MODEL)
    gw_fi = wgrad(h1b, dau, "wgrad_fi", D_MODEL, 2 * D_FF // N_SHARD, tp, True)
    gw_fo = wgrad(sact, dr2, "wgrad_fo", D_FF // 2, D_MODEL, tp, False).reshape(N_SHARD, -1, D_MODEL)
    grads = [gw_in, gw_bh, gw_ba, gw_out, gw_fi, gw_fo]

    parts = [(dlg, 8), (dlb, 8), (dlb4, 8), (dng, 8), (dsk[:, 0], 8),
             (dg1, 8), (db1, 8), (dg2, 8), (db2, 8), (dxp[PAD:BLOCK], BLOCK)]
    small = jnp.concatenate([_rows128(a, r) for a, r in parts], axis=0)

    c_idx = jnp.reshape(ic, (1,)).astype(jnp.int32)
    *others, gathered = pair_exchange_halves(grads, small)
    sums = [add_pair(g, o, c_idx, "add_pair_%d" % n) for n, (g, o) in enumerate(zip(grads, others))]
    quads = chip_exchange(sums)
    jc_idx = jnp.stack([shard, ic]).astype(jnp.int32)
    halves = [add_four(s, q, jc_idx, "add_four_%d" % n) for n, (s, q) in enumerate(zip(sums, quads))]
    red = [r.reshape(-1, r.shape[-1]) for r in pair_exchange_results(halves)]
    small_sum = sum_devices(gathered)

    offs, acc = [], 0
    for _, r in parts:
        offs.append(acc)
        acc += r
    take = lambda n, size: small_sum[offs[n]:offs[n] + parts[n][1]].reshape(-1)[:size]
    g_meta_full = take(9, N_META * D_MODEL).reshape(N_META, D_MODEL)
    g_small = {
        "meta_tokens": lax.dynamic_slice_in_dim(g_meta_full, shard * (D_MODEL // N_SHARD), D_MODEL // N_SHARD, axis=1),
        "ln_emb_g": take(0, D_MODEL), "ln_emb_b": take(1, D_MODEL),
        "hg_lower_bounds": take(2, 2 * HG_HEADS * HG_K).reshape(2, HG_HEADS * HG_K),
        "hg_norm_g": take(3, HG_K).reshape(1, HG_K), "attn_sinks": take(4, ATT_HEADS).reshape(1, ATT_HEADS),
        "ln1_g": take(5, D_MODEL).reshape(1, D_MODEL), "ln1_b": take(6, D_MODEL).reshape(1, D_MODEL),
        "ln2_g": take(7, D_MODEL).reshape(1, D_MODEL), "ln2_b": take(8, D_MODEL).reshape(1, D_MODEL),
    }
    g_big = {"w_in": red[0], "w_branch_hg": red[1], "w_branch_attn": red[2], "w_out": red[3],
             "w_ffn_in": red[4], "w_ffn_out": red[5]}

    names = ["meta_tokens", "ln_emb_g", "ln_emb_b", "w_in", "hg_lower_bounds", "hg_norm_g", "attn_sinks",
             "w_branch_hg", "w_branch_attn", "w_out", "ln1_g", "ln1_b", "w_ffn_in", "w_ffn_out", "ln2_g", "ln2_b"]
    given = dict(
        meta_tokens=(meta_tokens, m_meta_tokens, v_meta_tokens), ln_emb_g=(ln_emb_g, m_ln_emb_g, v_ln_emb_g),
        ln_emb_b=(ln_emb_b, m_ln_emb_b, v_ln_emb_b), w_in=(w_in, m_w_in, v_w_in),
        hg_lower_bounds=(hg_lower_bounds, m_hg_lower_bounds, v_hg_lower_bounds),
        hg_norm_g=(hg_norm_g, m_hg_norm_g, v_hg_norm_g), attn_sinks=(attn_sinks, m_attn_sinks, v_attn_sinks),
        w_branch_hg=(w_branch_hg, m_w_branch_hg, v_w_branch_hg),
        w_branch_attn=(w_branch_attn, m_w_branch_attn, v_w_branch_attn), w_out=(w_out, m_w_out, v_w_out),
        ln1_g=(ln1_g, m_ln1_g, v_ln1_g), ln1_b=(ln1_b, m_ln1_b, v_ln1_b), w_ffn_in=(w_ffn_in, m_w_ffn_in, v_w_ffn_in),
        w_ffn_out=(w_ffn_out, m_w_ffn_out, v_w_ffn_out), ln2_g=(ln2_g, m_ln2_g, v_ln2_g), ln2_b=(ln2_b, m_ln2_b, v_ln2_b))
    out_g, out_d, out_m, out_v = [], [], [], []
    for nm in names:
        w, m, v = given[nm]
        shape = w.shape
        g = g_big[nm] if nm in g_big else g_small[nm]
        two_d = (lambda a: a.reshape(8, BLOCK)) if w.ndim == 1 else (lambda a: a.reshape(a.shape[-2], a.shape[-1]))
        d, mn, vn = adamw(two_d(w), two_d(g), two_d(m), two_d(v), "adamw_" + nm)
        out_g.append(g.reshape(shape))
        out_d.append(d.reshape(shape))
        out_m.append(mn.reshape(shape))
        out_v.append(vn.reshape(shape))

    loss = lax.psum(loss_part[0, 0], ("x", "y", "c"))
    grad_x = dxp[BLOCK:].reshape(x.shape)
    return (loss, grad_x, *out_g, *out_d, *out_m, *out_v)
```

```python
import functools

import jax
import jax.numpy as jnp
from jax import lax
from jax.experimental import pallas as pl
from jax.experimental.pallas import tpu as pltpu

f32 = jnp.float32
bf16 = jnp.bfloat16

D_MODEL = 1024
BLOCK = 128
N_META = 16
PAD = BLOCK - N_META
HG_HEADS = 4
HG_K = 128
SUB = 16
ATT_HEADS = 8
HEAD_DIM = 64
ATT_QW = ATT_HEADS * HEAD_DIM
D_FF = 2816
EPS = 1e-5
ALPHA = 2.0 ** 0.25
ROPE_THETA = 10000.0
N_A = 2816
N_G = 2048
IN_W = N_A + N_G
N_SHARD = 4
N_DEV = 8

ADAM_LR = 0.001
ADAM_B1 = 0.9
ADAM_B2 = 0.999
ADAM_EPS = 1e-08
ADAM_WD = 0.01
ADAM_STEP = 10

VMEM_LIMIT = 56 * 1024 * 1024
MESH = pl.DeviceIdType.MESH


def _cparams(sem, vmem=VMEM_LIMIT):
    return pltpu.CompilerParams(dimension_semantics=sem, vmem_limit_bytes=vmem)


def _const_spec(shape):
    zeros = (0,) * len(shape)
    return pl.BlockSpec(shape, lambda *_: zeros, pipeline_mode=pl.Buffered(1))


def _dot(a, b, ca, cb):
    return lax.dot_general(a.astype(bf16), b.astype(bf16), (((ca,), (cb,)), ((), ())),
                           preferred_element_type=f32)


@jax.custom_vjp
def mm(a, b):
    return _dot(a, b, 1, 0)


mm.defvjp(lambda a, b: (_dot(a, b, 1, 0), (a, b)),
          lambda r, g: (_dot(g, r[1], 1, 1), _dot(r[0], g, 0, 0)))


@jax.custom_vjp
def mm_nt(a, b):
    return _dot(a, b, 1, 1)


mm_nt.defvjp(lambda a, b: (_dot(a, b, 1, 1), (a, b)),
             lambda r, g: (_dot(g, r[1], 1, 0), _dot(g, r[0], 0, 0)))


@jax.custom_vjp
def mm_tn(a, b):
    return _dot(a, b, 0, 0)


mm_tn.defvjp(lambda a, b: (_dot(a, b, 0, 0), (a, b)),
             lambda r, g: (_dot(r[1], g, 1, 1), _dot(r[0], g, 1, 0)))


@functools.partial(jax.custom_vjp, nondiff_argnums=(1,))
def roll_lanes(x, shift):
    return pltpu.roll(x, shift, 1)


roll_lanes.defvjp(lambda x, shift: (pltpu.roll(x, shift, 1), None),
                  lambda shift, _, g: (pltpu.roll(g, (128 - shift) % 128, 1),))


def _sigmoid(x):
    return 1.0 / (1.0 + jnp.exp(-x))


def _ln_stats(x):
    mu = jnp.mean(x, axis=-1, keepdims=True)
    xc = x - mu
    var = jnp.mean(xc * xc, axis=-1, keepdims=True)
    rs = lax.rsqrt(var + EPS)
    return xc * rs, rs


def _ln_bwd(dy, xh, rs, g):
    dxh = dy * g
    m1 = jnp.mean(dxh, axis=-1, keepdims=True)
    m2 = jnp.mean(dxh * xh, axis=-1, keepdims=True)
    return rs * (dxh - m1 - xh * m2)


def _row_ids(i):
    return i * BLOCK + lax.broadcasted_iota(jnp.int32, (BLOCK, 1), 0)


def emb_inproj(x, metablk, g, b, w_in, nb):
    def body(x_ref, mb_ref, g_ref, b_ref, w_ref, h0_ref, h0b_ref, pa_ref, pg_ref):
        i = pl.program_id(0)
        xb = jnp.where(i == 0, mb_ref[...], x_ref[...])
        xh, _ = _ln_stats(xb)
        y = xh * g_ref[...] + b_ref[...]
        y = jnp.where(_row_ids(i) >= PAD, y, 0.0)
        h0_ref[...] = y
        yb = y.astype(bf16)
        h0b_ref[...] = yb
        pa_ref[...] = jnp.dot(yb, w_ref[:, :N_A], preferred_element_type=f32)
        pg_ref[...] = jnp.dot(yb, w_ref[:, N_A:], preferred_element_type=f32)

    p = nb * BLOCK
    row = lambda n: pl.BlockSpec((BLOCK, n), lambda i: (i, 0))
    return pl.pallas_call(
        body, name="emb_inproj", grid=(nb,),
        in_specs=[pl.BlockSpec((BLOCK, D_MODEL), lambda i: (jnp.maximum(i - 1, 0), 0)),
                  _const_spec((BLOCK, D_MODEL)), _const_spec((1, D_MODEL)), _const_spec((1, D_MODEL)),
                  _const_spec((D_MODEL, IN_W))],
        out_specs=[row(D_MODEL), row(D_MODEL), row(N_A), row(N_G)],
        out_shape=[jax.ShapeDtypeStruct((p, D_MODEL), f32), jax.ShapeDtypeStruct((p, D_MODEL), bf16),
                   jax.ShapeDtypeStruct((p, N_A), f32), jax.ShapeDtypeStruct((p, N_G), f32)],
        compiler_params=_cparams(("parallel",)),
    )(x, metablk, g, b, w_in)


def _hgrn_chunk(valid, st, hq, hf, hi, hg, lbraw, ng):
    lb = _sigmoid(lbraw[0:1] - lbraw[1:2])
    q = hq * _sigmoid(hq)
    fg = lb + (1.0 - lb) * _sigmoid(hf)
    logf = jnp.where(valid, jnp.log(fg), 0.0)
    k = jnp.where(valid, 1.0 - fg, 0.0)
    v = hi
    r = lax.broadcasted_iota(jnp.int32, (BLOCK, BLOCK), 0)
    c = lax.broadcasted_iota(jnp.int32, (BLOCK, BLOCK), 1)
    tril = (c <= r).astype(f32)
    bcum = jnp.dot(tril, logf, precision=lax.Precision.HIGHEST, preferred_element_type=f32)
    blast = bcum[BLOCK - 1:BLOCK]
    rows = lax.broadcasted_iota(jnp.int32, (BLOCK, 1), 0)
    sub8 = lax.broadcasted_iota(jnp.int32, (BLOCK // 8, 8, HG_K), 1)
    b8 = bcum.reshape(BLOCK // 8, 8, HG_K)
    row_of_8 = lambda j: jnp.broadcast_to(b8[:, j:j + 1, :], b8.shape)
    a = jnp.where(r == c, jnp.sum(q * k, axis=-1, keepdims=True), 0.0)
    seg = BLOCK
    while seg >= 2:
        half = seg // 2
        if seg >= 8:
            bs = bcum.reshape(BLOCK // seg, seg, HG_K)
            ref = jnp.broadcast_to(bs[:, half - 1:half, :], bs.shape)
        elif seg == 4:
            ref = jnp.where(sub8 < 4, row_of_8(1), row_of_8(5))
        else:
            ref = jnp.where(sub8 < 2, row_of_8(0), jnp.where(sub8 < 4, row_of_8(2),
                                                             jnp.where(sub8 < 6, row_of_8(4), row_of_8(6))))
        ref = ref.reshape(BLOCK, HG_K)
        upper = (rows % seg) >= half
        q_up = q * jnp.exp(jnp.where(upper, bcum - ref, -jnp.inf))
        k_lo = k * jnp.exp(jnp.where(upper, -jnp.inf, ref - bcum))
        a = a + jnp.where((r // seg) == (c // seg), mm_nt(q_up, k_lo), 0.0)
        seg = half
    o = mm_nt(q * jnp.exp(bcum), st) + mm(a, v)
    st_new = st * jnp.exp(blast) + mm_tn(v, k * jnp.exp(blast - bcum))
    on = o * lax.rsqrt(jnp.mean(o * o, axis=-1, keepdims=True) + EPS) * ng
    return st_new, on * (hg * _sigmoid(hg))


def _hgrn_in_specs(rowmap):
    wide = lambda col: pl.BlockSpec((BLOCK, HG_HEADS * HG_K), lambda i: (rowmap(i), col))
    return [wide(0), wide(1), wide(2), wide(3), _const_spec((2, HG_HEADS * HG_K)), _const_spec((1, HG_K))]


def _head(ref, h):
    return ref[:, h * HG_K:(h + 1) * HG_K]


def hgrn_fwd(pa, lbraw, ng, nb):
    def body(hq_ref, hf_ref, hi_ref, hg_ref, lb_ref, ng_ref, og_ref, sp_ref, st_ref):
        i = pl.program_id(0)

        @pl.when(i == 0)
        def _():
            st_ref[...] = jnp.zeros_like(st_ref)

        valid = _row_ids(i) >= PAD
        for h in range(HG_HEADS):
            st = st_ref[h]
            sp_ref[0, h] = st
            st_new, out = _hgrn_chunk(valid, st, _head(hq_ref, h), _head(hf_ref, h), _head(hi_ref, h),
                                      _head(hg_ref, h), _head(lb_ref, h), ng_ref[...])
            st_ref[h] = st_new
            og_ref[:, h * HG_K:(h + 1) * HG_K] = out.astype(bf16)

    p = nb * BLOCK
    return pl.pallas_call(
        body, name="hgrn_fwd", grid=(nb,),
        in_specs=_hgrn_in_specs(lambda i: i),
        out_specs=[pl.BlockSpec((BLOCK, HG_HEADS * HG_K), lambda i: (i, 0)),
                   pl.BlockSpec((1, HG_HEADS, HG_K, HG_K), lambda i: (i, 0, 0, 0))],
        out_shape=[jax.ShapeDtypeStruct((p, HG_HEADS * HG_K), bf16),
                   jax.ShapeDtypeStruct((nb, HG_HEADS, HG_K, HG_K), f32)],
        scratch_shapes=[pltpu.VMEM((HG_HEADS, HG_K, HG_K), f32)],
        compiler_params=_cparams(("arbitrary",)),
    )(pa, pa, pa, pa, lbraw, ng)


def hgrn_bwd(pa, lbraw, ng, sprev, dog, nb):
    def body(hq_ref, hf_ref, hi_ref, hg_ref, lb_ref, ng_ref, sp_ref, do_ref,
             dq_ref, df_ref, di_ref, dg_ref, dlb_ref, dng_ref, dst_ref):
        i = pl.program_id(0)

        @pl.when(i == 0)
        def _():
            dst_ref[...] = jnp.zeros_like(dst_ref)
            dlb_ref[...] = jnp.zeros_like(dlb_ref)
            dng_ref[...] = jnp.zeros_like(dng_ref)

        valid = _row_ids(nb - 1 - i) >= PAD
        dng_sum = jnp.zeros((1, HG_K), f32)
        for h in range(HG_HEADS):
            cols = slice(h * HG_K, (h + 1) * HG_K)
            _, vjp = jax.vjp(functools.partial(_hgrn_chunk, valid), sp_ref[0, h], _head(hq_ref, h), _head(hf_ref, h),
                             _head(hi_ref, h), _head(hg_ref, h), _head(lb_ref, h), ng_ref[...])
            dst, dq, df, di, dg, dlb, dng = vjp((dst_ref[h], _head(do_ref, h)))
            dst_ref[h] = dst
            dq_ref[:, cols] = dq.astype(bf16)
            df_ref[:, cols] = df.astype(bf16)
            di_ref[:, cols] = di.astype(bf16)
            dg_ref[:, cols] = dg.astype(bf16)
            dlb_ref[:, cols] += dlb
            dng_sum = dng_sum + dng
        dng_ref[...] += dng_sum

    p = nb * BLOCK
    rev = lambda i: nb - 1 - i
    hw = HG_HEADS * HG_K
    blk = pl.BlockSpec((BLOCK, hw), lambda i: (rev(i), 0))
    wide = jax.ShapeDtypeStruct((p, hw), bf16)
    return pl.pallas_call(
        body, name="hgrn_bwd", grid=(nb,),
        in_specs=_hgrn_in_specs(rev) + [pl.BlockSpec((1, HG_HEADS, HG_K, HG_K), lambda i: (rev(i), 0, 0, 0)), blk],
        out_specs=[blk, blk, blk, blk, pl.BlockSpec((2, hw), lambda i: (0, 0)), pl.BlockSpec((1, HG_K), lambda i: (0, 0))],
        out_shape=[wide, wide, wide, wide, jax.ShapeDtypeStruct((2, hw), f32), jax.ShapeDtypeStruct((1, HG_K), f32)],
        scratch_shapes=[pltpu.VMEM((HG_HEADS, HG_K, HG_K), f32)],
        compiler_params=_cparams(("arbitrary",)),
    )(pa, pa, pa, pa, lbraw, ng, sprev, dog)


def _rope(x, cos, sin):
    lane = lax.broadcasted_iota(jnp.int32, x.shape, 1)
    rot = jnp.where(lane % HEAD_DIM < HEAD_DIM // 2, -roll_lanes(x, BLOCK - HEAD_DIM // 2),
                    roll_lanes(x, HEAD_DIM // 2))
    return x * cos + rot * sin


def _both_halves(x, g):
    lo = lax.broadcasted_iota(jnp.int32, x.shape, 1) < HEAD_DIM
    sw = roll_lanes(x, HEAD_DIM)
    return jnp.where(lo, x, sw) if g == 0 else jnp.where(lo, sw, x)


def _attn_block(band_ok, meta_ok, tabs, q, kp, kc, vp, vc, km, vm, *sinks):
    cq, sq, cp, sp, cm, sm = tabs
    neg = jnp.finfo(f32).min
    kk = jnp.concatenate([_rope(kp, cp, sp), _rope(kc, cq, sq)], axis=0)
    vv = jnp.concatenate([vp, vc], axis=0)
    kmr = _rope(km, cm, sm)
    lo = lax.broadcasted_iota(jnp.int32, (BLOCK, BLOCK), 1) < HEAD_DIM
    kv = [(_both_halves(kk, g), _both_halves(vv, g), _both_halves(kmr, g), _both_halves(vm, g)) for g in range(2)]
    slabs = []
    for m in range(ATT_HEADS // 2):
        qr = _rope(q[:, m * BLOCK:(m + 1) * BLOCK], cq, sq)
        kk_g, vv_g, km_g, vm_g = kv[m // 2]
        outs = []
        for half in range(2):
            sink = sinks[2 * m + half]
            qm = jnp.where(lo if half == 0 else ~lo, qr, 0.0)
            sb = jnp.where(band_ok, mm_nt(qm, kk_g) * (HEAD_DIM ** -0.5), neg)
            sme = jnp.where(meta_ok, mm_nt(qm, km_g) * (HEAD_DIM ** -0.5), neg)
            mx = jnp.maximum(jnp.maximum(jnp.max(sb, axis=-1, keepdims=True),
                                         jnp.max(sme, axis=-1, keepdims=True)), sink)
            eb, em = jnp.exp(sb - mx), jnp.exp(sme - mx)
            den = jnp.sum(eb, axis=-1, keepdims=True) + jnp.sum(em, axis=-1, keepdims=True) + jnp.exp(sink - mx)
            outs.append(mm(eb / den, vv_g) + mm(em / den, vm_g))
        slabs.append(jnp.where(lo, outs[0], outs[1]))
    return jnp.concatenate(slabs, axis=1)


def _attn_masks(i):
    qpos = _row_ids(i) - PAD
    s = lax.broadcasted_iota(jnp.int32, (1, 2 * BLOCK), 1)
    kpos = jnp.where(s < BLOCK, jnp.where(i > 0, (i - 1) * BLOCK - PAD + s, -1), i * BLOCK - PAD + s - BLOCK)
    band_ok = (kpos >= N_META) & (kpos <= qpos) & (qpos - kpos < BLOCK)
    meta_ok = lax.broadcasted_iota(jnp.int32, (1, N_META), 1) <= qpos
    return band_ok, meta_ok


def _attn_in_specs():
    prev = lambda i: jnp.maximum(i - 1, 0)
    kcol, vcol = N_A // BLOCK - 2, N_A // BLOCK - 1
    blk = lambda rowmap, col: pl.BlockSpec((BLOCK, BLOCK), lambda i: (rowmap(i), col))
    cur, first = (lambda i: i), (lambda i: 0)
    return [pl.BlockSpec((BLOCK, ATT_QW), lambda i: (i, 4)),
            blk(prev, kcol), blk(cur, kcol), blk(prev, vcol), blk(cur, vcol), blk(first, kcol), blk(first, vcol),
            blk(cur, 0), blk(cur, 0), blk(prev, 0), blk(prev, 0), blk(first, 0), blk(first, 0),
            _const_spec((ATT_HEADS, BLOCK))]


def _attn_operands(q_ref, kp_ref, kc_ref, vp_ref, vc_ref, km_ref, vm_ref, cq, sq, cp, sp, cm, sm, sk_ref):
    tabs = (cq[...], sq[...], cp[...], sp[...], cm[PAD:, :], sm[PAD:, :])
    args = (q_ref[...], kp_ref[...], kc_ref[...], vp_ref[...], vc_ref[...], km_ref[PAD:, :], vm_ref[PAD:, :])
    sinks = tuple(sk_ref[j:j + 1, 0:1] for j in range(ATT_HEADS))
    return tabs, args + sinks


def attn_fwd(pa, cos, sin, sinks8, nb):
    def body(*refs):
        o_ref = refs[-1]
        band_ok, meta_ok = _attn_masks(pl.program_id(0))
        tabs, args = _attn_operands(*refs[:-1])
        o_ref[...] = _attn_block(band_ok, meta_ok, tabs, *args).astype(bf16)

    return pl.pallas_call(
        body, name="attn_fwd", grid=(nb,), in_specs=_attn_in_specs(),
        out_specs=pl.BlockSpec((BLOCK, ATT_QW), lambda i: (i, 0)),
        out_shape=jax.ShapeDtypeStruct((nb * BLOCK, ATT_QW), bf16),
        compiler_params=_cparams(("parallel",)),
    )(pa, pa, pa, pa, pa, pa, pa, cos, sin, cos, sin, cos, sin, sinks8)


def attn_bwd(pa, cos, sin, sinks8, do, nb):
    def body(*refs):
        do_ref = refs[14]
        dq_ref, dkc_ref, dkp_ref, dvc_ref, dvp_ref, dkm_ref, dvm_ref, dsk_ref = refs[15:]
        i = pl.program_id(0)

        @pl.when(i == 0)
        def _():
            dkm_ref[...] = jnp.zeros((N_META, BLOCK), f32)
            dvm_ref[...] = jnp.zeros((N_META, BLOCK), f32)
            dsk_ref[...] = jnp.zeros((ATT_HEADS, BLOCK), f32)

        band_ok, meta_ok = _attn_masks(i)
        tabs, args = _attn_operands(*refs[:14])
        _, vjp = jax.vjp(functools.partial(_attn_block, band_ok, meta_ok, tabs), *args)
        grads = vjp(do_ref[...])
        dq_ref[...] = grads[0].astype(bf16)
        dkp_ref[...] = grads[1]
        dkc_ref[...] = grads[2]
        dvp_ref[...] = grads[3]
        dvc_ref[...] = grads[4]
        dkm_ref[...] += grads[5]
        dvm_ref[...] += grads[6]
        for j in range(ATT_HEADS):
            dsk_ref[j:j + 1, :] += jnp.broadcast_to(grads[7 + j], (1, BLOCK))

    p = nb * BLOCK
    row = pl.BlockSpec((BLOCK, BLOCK), lambda i: (i, 0))
    const = lambda r: pl.BlockSpec((r, BLOCK), lambda i: (0, 0))
    part = jax.ShapeDtypeStruct((p, BLOCK), f32)
    return pl.pallas_call(
        body, name="attn_bwd", grid=(nb,),
        in_specs=_attn_in_specs() + [pl.BlockSpec((BLOCK, ATT_QW), lambda i: (i, 0))],
        out_specs=[pl.BlockSpec((BLOCK, ATT_QW), lambda i: (i, 0)), row, row, row, row,
                   const(N_META), const(N_META), const(ATT_HEADS)],
        out_shape=[jax.ShapeDtypeStruct((p, ATT_QW), bf16), part, part, part, part,
                   jax.ShapeDtypeStruct((N_META, BLOCK), f32), jax.ShapeDtypeStruct((N_META, BLOCK), f32),
                   jax.ShapeDtypeStruct((ATT_HEADS, BLOCK), f32)],
        compiler_params=_cparams(("arbitrary",)),
    )(pa, pa, pa, pa, pa, pa, pa, cos, sin, cos, sin, cos, sin, sinks8, do)


def mid_rows(h0, pg, og, oatt, target, wbh, wba, wout, wfi, wfo, ln1g, ln1b, ln2g, ln2b, nb):
    def body(h0_ref, pg_ref, og_ref, oa_ref, t_ref, wbh_ref, wba_ref, wo_ref, wfi_ref, wfo_ref,
             g1_ref, b1_ref, g2_ref, b2_ref,
             dh0_ref, dpg_ref, dog_ref, doa_ref, dyh_ref, dya_ref, mix_ref, dr1_ref, h1b_ref, dau_ref, s_ref, dr2_ref,
             loss_ref, dg1_ref, db1_ref, dg2_ref, db2_ref):
        i = pl.program_id(0)

        @pl.when(i == 0)
        def _():
            loss_ref[...] = jnp.zeros_like(loss_ref)
            for r in (dg1_ref, db1_ref, dg2_ref, db2_ref):
                r[...] = jnp.zeros_like(r)

        g1, b1, g2, b2 = g1_ref[...], b1_ref[...], g2_ref[...], b2_ref[...]
        yh = jnp.dot(og_ref[...], wbh_ref[...], preferred_element_type=f32)
        ya = jnp.dot(oa_ref[...], wba_ref[...], preferred_element_type=f32)
        gh = _sigmoid(pg_ref[:, :D_MODEL])
        ga = _sigmoid(pg_ref[:, D_MODEL:])
        mixin = (gh * yh + ga * ya).astype(bf16)
        mix_ref[...] = mixin
        r1 = ALPHA * h0_ref[...] + jnp.dot(mixin, wo_ref[...], preferred_element_type=f32)
        xh1, rs1 = _ln_stats(r1)
        h1 = xh1 * g1 + b1
        h1b = h1.astype(bf16)
        h1b_ref[...] = h1b
        au = jnp.dot(h1b, wfi_ref[...], preferred_element_type=f32)
        a, u = au[:, :D_FF], au[:, D_FF:]
        sg = _sigmoid(a)
        sa = a * sg
        s = (sa * u).astype(bf16)
        s_ref[...] = s
        r2 = ALPHA * h1 + jnp.dot(s, wfo_ref[...], preferred_element_type=f32)
        xh2, rs2 = _ln_stats(r2)
        diff = jnp.where(i > 0, xh2 * g2 + b2 - t_ref[...], 0.0)
        loss_ref[...] += jnp.sum(diff * diff) * (0.5 / D_MODEL)
        dy = diff * (1.0 / D_MODEL)
        dg2_ref[...] += jnp.sum(dy * xh2, axis=0, keepdims=True)
        db2_ref[...] += jnp.sum(dy, axis=0, keepdims=True)
        dr2 = _ln_bwd(dy, xh2, rs2, g2)
        dr2b = dr2.astype(bf16)
        dr2_ref[...] = dr2b
        ds = _dot(dr2b, wfo_ref[...], 1, 1)
        da = (ds * u) * (sg * (1.0 + a * (1.0 - sg)))
        du = ds * sa
        dau = jnp.concatenate([da, du], axis=1).astype(bf16)
        dau_ref[...] = dau
        dh1 = ALPHA * dr2 + _dot(dau, wfi_ref[...], 1, 1)
        dg1_ref[...] += jnp.sum(dh1 * xh1, axis=0, keepdims=True)
        db1_ref[...] += jnp.sum(dh1, axis=0, keepdims=True)
        dr1 = _ln_bwd(dh1, xh1, rs1, g1)
        dr1b = dr1.astype(bf16)
        dr1_ref[...] = dr1b
        dh0_ref[...] = ALPHA * dr1
        dmix = _dot(dr1b, wo_ref[...], 1, 1)
        dyh = (dmix * gh).astype(bf16)
        dya = (dmix * ga).astype(bf16)
        dyh_ref[...] = dyh
        dya_ref[...] = dya
        dpg_ref[:, :D_MODEL] = (dmix * yh * gh * (1.0 - gh)).astype(bf16)
        dpg_ref[:, D_MODEL:] = (dmix * ya * ga * (1.0 - ga)).astype(bf16)
        dog_ref[...] = _dot(dyh, wbh_ref[...], 1, 1)
        doa_ref[...] = _dot(dya, wba_ref[...], 1, 1)

    p = nb * BLOCK
    row = lambda n: pl.BlockSpec((BLOCK, n), lambda i: (i, 0))
    vec = lambda: pl.BlockSpec((1, D_MODEL), lambda i: (0, 0))
    sds = lambda n, dt: jax.ShapeDtypeStruct((p, n), dt)
    hw = HG_HEADS * HG_K
    return pl.pallas_call(
        body, name="mid_rows", grid=(nb,),
        in_specs=[row(D_MODEL), row(N_G), row(hw), row(ATT_QW),
                  pl.BlockSpec((BLOCK, D_MODEL), lambda i: (jnp.maximum(i - 1, 0), 0)),
                  _const_spec((hw, D_MODEL)), _const_spec((ATT_QW, D_MODEL)), _const_spec((D_MODEL, D_MODEL)),
                  _const_spec((D_MODEL, 2 * D_FF)), _const_spec((D_FF, D_MODEL)),
                  _const_spec((1, D_MODEL)), _const_spec((1, D_MODEL)), _const_spec((1, D_MODEL)),
                  _const_spec((1, D_MODEL))],
        out_specs=[row(D_MODEL), row(N_G), row(hw), row(ATT_QW), row(D_MODEL), row(D_MODEL), row(D_MODEL),
                   row(D_MODEL), row(D_MODEL), row(2 * D_FF), row(D_FF), row(D_MODEL),
                   pl.BlockSpec((1, 1), lambda i: (0, 0)), vec(), vec(), vec(), vec()],
        out_shape=[sds(D_MODEL, f32), sds(N_G, bf16), sds(hw, f32), sds(ATT_QW, f32), sds(D_MODEL, bf16),
                   sds(D_MODEL, bf16), sds(D_MODEL, bf16), sds(D_MODEL, bf16), sds(D_MODEL, bf16),
                   sds(2 * D_FF, bf16), sds(D_FF, bf16), sds(D_MODEL, bf16),
                   jax.ShapeDtypeStruct((1, 1), f32)] + [jax.ShapeDtypeStruct((1, D_MODEL), f32)] * 4,
        compiler_params=_cparams(("arbitrary",)),
    )(h0, pg, og, oatt, target, wbh, wba, wout, wfi, wfo, ln1g, ln1b, ln2g, ln2b)


def inproj_bwd(dh0p, dhq, dhf, dhi, dhg, daq, dkc, dkp, dvc, dvp, dkm, dvm, dpg, w_in, x, metablk, g, b, nb):
    def body(dh0_ref, dq_ref, df_ref, di_ref, dg_ref, daq_ref, dkc_ref, dkp_ref, dvc_ref, dvp_ref, dkm_ref, dvm_ref,
             dpg_ref, w_ref, x_ref, mb_ref, g_ref, b_ref, dproj_ref, dx_ref, dlg_ref, dlb_ref):
        i = pl.program_id(0)

        @pl.when(i == 0)
        def _():
            dlg_ref[...] = jnp.zeros_like(dlg_ref)
            dlb_ref[...] = jnp.zeros_like(dlb_ref)

        zero_pad = jnp.zeros((PAD, BLOCK), f32)
        has_next = i + 1 < nb
        first = i == 0

        def keys(cur_ref, next_ref, meta_ref):
            t = cur_ref[...] + jnp.where(has_next, next_ref[...], 0.0)
            return t + jnp.where(first, jnp.concatenate([zero_pad, meta_ref[...]], axis=0), 0.0)

        dproj = jnp.concatenate(
            [dq_ref[...], df_ref[...], di_ref[...], dg_ref[...], daq_ref[...],
             keys(dkc_ref, dkp_ref, dkm_ref).astype(bf16), keys(dvc_ref, dvp_ref, dvm_ref).astype(bf16),
             dpg_ref[...]], axis=1)
        dproj_ref[...] = dproj
        valid = _row_ids(i) >= PAD
        dh0 = jnp.where(valid, dh0_ref[...] + _dot(dproj, w_ref[...], 1, 1), 0.0)
        xb = jnp.where(first, mb_ref[...], x_ref[...])
        xh, rs = _ln_stats(xb)
        dlg_ref[...] += jnp.sum(dh0 * xh, axis=0, keepdims=True)
        dlb_ref[...] += jnp.sum(dh0, axis=0, keepdims=True)
        dx_ref[...] = jnp.where(valid, _ln_bwd(dh0, xh, rs, g_ref[...]), 0.0)

    p = nb * BLOCK
    row = lambda n: pl.BlockSpec((BLOCK, n), lambda i: (i, 0))
    nxt = pl.BlockSpec((BLOCK, BLOCK), lambda i: (jnp.minimum(i + 1, nb - 1), 0))
    hw = HG_HEADS * HG_K
    vec = lambda: pl.BlockSpec((1, D_MODEL), lambda i: (0, 0))
    return pl.pallas_call(
        body, name="inproj_bwd", grid=(nb,),
        in_specs=[row(D_MODEL), row(hw), row(hw), row(hw), row(hw), row(ATT_QW),
                  row(BLOCK), nxt, row(BLOCK), nxt, _const_spec((N_META, BLOCK)), _const_spec((N_META, BLOCK)),
                  row(N_G), _const_spec((D_MODEL, IN_W)),
                  pl.BlockSpec((BLOCK, D_MODEL), lambda i: (jnp.maximum(i - 1, 0), 0)),
                  _const_spec((BLOCK, D_MODEL)), _const_spec((1, D_MODEL)), _const_spec((1, D_MODEL))],
        out_specs=[row(IN_W), row(D_MODEL), vec(), vec()],
        out_shape=[jax.ShapeDtypeStruct((p, IN_W), bf16), jax.ShapeDtypeStruct((p, D_MODEL), f32),
                   jax.ShapeDtypeStruct((1, D_MODEL), f32), jax.ShapeDtypeStruct((1, D_MODEL), f32)],
        compiler_params=_cparams(("arbitrary",)),
    )(dh0p, dhq, dhf, dhi, dhg, daq, dkc, dkp, dvc, dvp, dkm, dvm, dpg, w_in, x, metablk, g, b)


def wgrad(a, b, name, tk, tn, tp, by_cols):
    p, k = a.shape
    n = b.shape[1]
    nsteps = p // tp

    def body(a_ref, b_ref, o_ref):
        @pl.when(pl.program_id(2) == 0)
        def _():
            o_ref[...] = jnp.zeros_like(o_ref)

        o_ref[0] += _dot(a_ref[...], b_ref[...], 0, 0)

    if by_cols:
        shard_n = n // N_SHARD
        per = shard_n // tn
        out_shape = (N_SHARD, k, shard_n)
        omap = lambda ik, jn, ip: (jn // per, ik, jn % per)
    else:
        out_shape = (1, k, n)
        omap = lambda ik, jn, ip: (0, ik, jn)
    return pl.pallas_call(
        body, name=name, grid=(k // tk, n // tn, nsteps),
        in_specs=[pl.BlockSpec((tp, tk), lambda ik, jn, ip: (ip, ik)),
                  pl.BlockSpec((tp, tn), lambda ik, jn, ip: (ip, jn))],
        out_specs=pl.BlockSpec((1, tk, tn), omap),
        out_shape=jax.ShapeDtypeStruct(out_shape, f32),
        compiler_params=_cparams(("parallel", "parallel", "arbitrary")),
    )(a, b)


def adamw(w, g, m, v, name):
    r, c = w.shape
    tr = r
    for cand in (256, 176, 128):
        if r > cand and r % cand == 0:
            tr = cand
            break

    def body(w_ref, g_ref, m_ref, v_ref, d_ref, mo_ref, vo_ref):
        gg = g_ref[...]
        mn = ADAM_B1 * m_ref[...] + (1.0 - ADAM_B1) * gg
        vn = ADAM_B2 * v_ref[...] + (1.0 - ADAM_B2) * (gg * gg)
        m_hat = mn / (1.0 - ADAM_B1 ** ADAM_STEP)
        v_hat = vn / (1.0 - ADAM_B2 ** ADAM_STEP)
        d_ref[...] = -ADAM_LR * (m_hat / (jnp.sqrt(v_hat) + ADAM_EPS) + ADAM_WD * w_ref[...])
        mo_ref[...] = mn
        vo_ref[...] = vn

    spec = pl.BlockSpec((tr, c), lambda i: (i, 0))
    sds = jax.ShapeDtypeStruct((r, c), f32)
    return pl.pallas_call(
        body, name=name, grid=(r // tr,), in_specs=[spec] * 4, out_specs=[spec] * 3, out_shape=[sds] * 3,
        compiler_params=_cparams(("parallel",)),
    )(w, g, m, v)


def _me():
    return lax.axis_index("x"), lax.axis_index("y"), lax.axis_index("c")


def _chip_peer(x, y, c, k):
    return (x ^ (k >> 1), y ^ (k & 1), c)


ANY = pl.BlockSpec(memory_space=pl.ANY)


def gather_weights(shards):
    n = len(shards)
    out_dtypes = [bf16 if s.size > 16 * 256 else f32 for s in shards]

    def body(*refs):
        ins, outs = refs[:n], refs[n:2 * n]
        stage = refs[2 * n:3 * n]
        send_sems, recv_sems, local_sems = refs[3 * n:]
        x, y, c = _me()
        j = 2 * x + y
        for w in range(n):
            stage[w][...] = ins[w][...].astype(out_dtypes[w])
        sends, locs = [], []
        for w in range(n):
            loc = pltpu.make_async_copy(stage[w], outs[w].at[j], local_sems.at[w])
            loc.start()
            locs.append(loc)
            for k in (1, 2, 3):
                cp = pltpu.make_async_remote_copy(
                    src_ref=stage[w], dst_ref=outs[w].at[j], send_sem=send_sems.at[w, k - 1],
                    recv_sem=recv_sems.at[w, k - 1], device_id=_chip_peer(x, y, c, k), device_id_type=MESH)
                cp.start()
                sends.append(cp)
        for w in range(n):
            for k in (1, 2, 3):
                pltpu.make_async_remote_copy(
                    src_ref=stage[w], dst_ref=outs[w].at[j ^ k], send_sem=send_sems.at[w, k - 1],
                    recv_sem=recv_sems.at[w, k - 1], device_id=_chip_peer(x, y, c, k), device_id_type=MESH).wait_recv()
        for cp in sends:
            cp.wait_send()
        for loc in locs:
            loc.wait()

    return pl.pallas_call(
        body, name="gather_weights",
        in_specs=[pl.BlockSpec(memory_space=pltpu.VMEM)] * n, out_specs=[ANY] * n,
        out_shape=[jax.ShapeDtypeStruct((N_SHARD,) + s.shape, dt) for s, dt in zip(shards, out_dtypes)],
        scratch_shapes=[pltpu.VMEM(s.shape, dt) for s, dt in zip(shards, out_dtypes)]
        + [pltpu.SemaphoreType.DMA((n, 3)), pltpu.SemaphoreType.DMA((n, 3)), pltpu.SemaphoreType.DMA((n,))],
        compiler_params=pltpu.CompilerParams(vmem_limit_bytes=VMEM_LIMIT),
    )(*shards)


def pair_exchange_halves(grads, small):
    n = len(grads)

    def body(*refs):
        ins, small_ref = refs[:n], refs[n]
        outs, gath = refs[n + 1:2 * n + 1], refs[2 * n + 1]
        send_sems, recv_sems, s_send, s_recv, local_sem = refs[2 * n + 2:]
        x, y, c = _me()
        me = 4 * x + 2 * y + c
        sends = []
        for w in range(n):
            half = ins[w].shape[1] // 2
            cp = pltpu.make_async_remote_copy(
                src_ref=ins[w].at[:, pl.ds((1 - c) * half, half), :], dst_ref=outs[w],
                send_sem=send_sems.at[w], recv_sem=recv_sems.at[w], device_id=(x, y, 1 - c), device_id_type=MESH)
            cp.start()
            sends.append(cp)
        loc = pltpu.make_async_copy(small_ref, gath.at[me], local_sem)
        loc.start()
        for k in range(1, N_DEV):
            cp = pltpu.make_async_remote_copy(
                src_ref=small_ref, dst_ref=gath.at[me], send_sem=s_send.at[k - 1], recv_sem=s_recv.at[k - 1],
                device_id=(x ^ (k >> 2), y ^ ((k >> 1) & 1), c ^ (k & 1)), device_id_type=MESH)
            cp.start()
            sends.append(cp)
        for w in range(n):
            half = ins[w].shape[1] // 2
            pltpu.make_async_remote_copy(
                src_ref=ins[w].at[:, pl.ds(0, half), :], dst_ref=outs[w], send_sem=send_sems.at[w],
                recv_sem=recv_sems.at[w], device_id=(x, y, 1 - c), device_id_type=MESH).wait_recv()
        for k in range(1, N_DEV):
            pltpu.make_async_remote_copy(
                src_ref=small_ref, dst_ref=gath.at[me ^ k], send_sem=s_send.at[k - 1], recv_sem=s_recv.at[k - 1],
                device_id=(x ^ (k >> 2), y ^ ((k >> 1) & 1), c ^ (k & 1)), device_id_type=MESH).wait_recv()
        for cp in sends:
            cp.wait_send()
        loc.wait()

    return pl.pallas_call(
        body, name="pair_exchange_halves", in_specs=[ANY] * (n + 1), out_specs=[ANY] * (n + 1),
        out_shape=[jax.ShapeDtypeStruct((g.shape[0], g.shape[1] // 2, g.shape[2]), f32) for g in grads]
        + [jax.ShapeDtypeStruct((N_DEV,) + small.shape, f32)],
        scratch_shapes=[pltpu.SemaphoreType.DMA((n,)), pltpu.SemaphoreType.DMA((n,)),
                        pltpu.SemaphoreType.DMA((N_DEV - 1,)), pltpu.SemaphoreType.DMA((N_DEV - 1,)),
                        pltpu.SemaphoreType.DMA],
    )(*grads, small)


def chip_exchange(sums):
    n = len(sums)

    def body(*refs):
        ins, outs = refs[:n], refs[n:2 * n]
        send_sems, recv_sems = refs[2 * n:]
        x, y, c = _me()
        j = 2 * x + y
        sends = []
        for w in range(n):
            for k in (1, 2, 3):
                cp = pltpu.make_async_remote_copy(
                    src_ref=ins[w].at[j ^ k], dst_ref=outs[w].at[k - 1], send_sem=send_sems.at[w, k - 1],
                    recv_sem=recv_sems.at[w, k - 1], device_id=_chip_peer(x, y, c, k), device_id_type=MESH)
                cp.start()
                sends.append(cp)
        for w in range(n):
            for k in (1, 2, 3):
                pltpu.make_async_remote_copy(
                    src_ref=ins[w].at[0], dst_ref=outs[w].at[k - 1], send_sem=send_sems.at[w, k - 1],
                    recv_sem=recv_sems.at[w, k - 1], device_id=_chip_peer(x, y, c, k), device_id_type=MESH).wait_recv()
        for cp in sends:
            cp.wait_send()

    return pl.pallas_call(
        body, name="chip_exchange", in_specs=[ANY] * n, out_specs=[ANY] * n,
        out_shape=[jax.ShapeDtypeStruct((N_SHARD - 1,) + s.shape[1:], s.dtype) for s in sums],
        scratch_shapes=[pltpu.SemaphoreType.DMA((n, 3)), pltpu.SemaphoreType.DMA((n, 3))],
    )(*sums)


def pair_exchange_results(halves):
    n = len(halves)

    def body(*refs):
        ins, outs = refs[:n], refs[n:2 * n]
        send_sems, recv_sems = refs[2 * n:]
        x, y, c = _me()
        sends = []
        for w in range(n):
            cp = pltpu.make_async_remote_copy(
                src_ref=ins[w].at[c], dst_ref=outs[w].at[c], send_sem=send_sems.at[w], recv_sem=recv_sems.at[w],
                device_id=(x, y, 1 - c), device_id_type=MESH)
            cp.start()
            sends.append(cp)
        for w in range(n):
            pltpu.make_async_remote_copy(
                src_ref=ins[w].at[c], dst_ref=outs[w].at[1 - c], send_sem=send_sems.at[w],
                recv_sem=recv_sems.at[w], device_id=(x, y, 1 - c), device_id_type=MESH).wait_recv()
        for cp in sends:
            cp.wait_send()

    return pl.pallas_call(
        body, name="pair_exchange_results", in_specs=[ANY] * n, out_specs=[ANY] * n,
        out_shape=[jax.ShapeDtypeStruct(h.shape, f32) for h in halves],
        input_output_aliases={w: w for w in range(n)},
        scratch_shapes=[pltpu.SemaphoreType.DMA((n,)), pltpu.SemaphoreType.DMA((n,))],
    )(*halves)


def add_pair(grad, other, c_idx, name):
    _, r, c = grad.shape
    half = r // 2
    tr = half // 2 if (half // 2) % 8 == 0 else half
    per = half // tr

    def body(c_ref, g_ref, o_ref, out_ref):
        out_ref[...] = (g_ref[...] + o_ref[...]).astype(bf16)

    return pl.pallas_call(
        body, name=name,
        grid_spec=pltpu.PrefetchScalarGridSpec(
            num_scalar_prefetch=1, grid=(N_SHARD, per),
            in_specs=[pl.BlockSpec((1, tr, c), lambda j, t, cr: (j, cr[0] * per + t, 0)),
                      pl.BlockSpec((1, tr, c), lambda j, t, cr: (j, t, 0))],
            out_specs=pl.BlockSpec((1, tr, c), lambda j, t, cr: (j, t, 0))),
        out_shape=jax.ShapeDtypeStruct((N_SHARD, half, c), bf16),
        compiler_params=_cparams(("parallel", "parallel")),
    )(c_idx, grad, other)


def add_four(own, parts, jc_idx, name):
    _, half, c = parts.shape
    tr = half // 2 if (half // 2) % 8 == 0 else half

    def body(jc_ref, own_ref, p_ref, out_ref):
        acc = own_ref[0].astype(f32)
        for k in range(N_SHARD - 1):
            acc = acc + p_ref[k].astype(f32)
        out_ref[0] = acc

    return pl.pallas_call(
        body, name=name,
        grid_spec=pltpu.PrefetchScalarGridSpec(
            num_scalar_prefetch=1, grid=(half // tr,),
            in_specs=[pl.BlockSpec((1, tr, c), lambda t, jc: (jc[0], t, 0)),
                      pl.BlockSpec((N_SHARD - 1, tr, c), lambda t, jc: (0, t, 0))],
            out_specs=pl.BlockSpec((1, tr, c), lambda t, jc: (jc[1], t, 0))),
        out_shape=jax.ShapeDtypeStruct((2, half, c), f32),
        compiler_params=_cparams(("parallel",)),
    )(jc_idx, own, parts)


def sum_devices(gathered):
    def body(g_ref, out_ref):
        acc = g_ref[0]
        for d in range(1, N_DEV):
            acc = acc + g_ref[d]
        out_ref[...] = acc

    return pl.pallas_call(body, name="sum_devices", out_shape=jax.ShapeDtypeStruct(gathered.shape[1:], f32))(gathered)


def _rows128(a, rows):
    flat = a.reshape(-1, BLOCK) if a.size % BLOCK == 0 else jnp.pad(a.reshape(1, -1), ((0, 0), (0, BLOCK - a.size)))
    return jnp.pad(flat, ((0, rows - flat.shape[0]), (0, 0)))


def kernel(x, meta_tokens, ln_emb_g, ln_emb_b, w_in, hg_lower_bounds, hg_norm_g, attn_sinks, w_branch_hg, w_branch_attn, w_out, ln1_g, ln1_b, w_ffn_in, w_ffn_out, ln2_g, ln2_b, loss_target, m_meta_tokens, m_ln_emb_g, m_ln_emb_b, m_w_in, m_hg_lower_bounds, m_hg_norm_g, m_attn_sinks, m_w_branch_hg, m_w_branch_attn, m_w_out, m_ln1_g, m_ln1_b, m_w_ffn_in, m_w_ffn_out, m_ln2_g, m_ln2_b, v_meta_tokens, v_ln_emb_g, v_ln_emb_b, v_w_in, v_hg_lower_bounds, v_hg_norm_g, v_attn_sinks, v_w_branch_hg, v_w_branch_attn, v_w_out, v_ln1_g, v_ln1_b, v_w_ffn_in, v_w_ffn_out, v_ln2_g, v_ln2_b):
    seq = x.shape[1]
    nb = seq // BLOCK + 1
    xs = x[0]
    ts = loss_target[0]
    ix, iy, ic = _me()
    shard = 2 * ix + iy
    vec = lambda a: a.reshape(1, D_MODEL)

    big = [w_in[0], w_branch_hg[0], w_branch_attn[0], w_out[0], w_ffn_in[0], w_ffn_out[0]]
    g_in, g_bh, g_ba, g_out, g_fi, g_fo, g_meta = gather_weights(big + [meta_tokens])
    by_cols = lambda g: g.transpose(1, 0, 2).reshape(g.shape[1], N_SHARD * g.shape[2])
    wf_in, wf_bh, wf_ba, wf_fi = by_cols(g_in), by_cols(g_bh), by_cols(g_ba), by_cols(g_fi)
    wf_out = g_out.reshape(D_MODEL, D_MODEL)
    wf_fo = g_fo.reshape(D_FF, D_MODEL)
    metablk = jnp.pad(by_cols(g_meta), ((PAD, 0), (0, 0)))

    pos = jnp.arange(nb * BLOCK, dtype=jnp.int32) - PAD
    half = HEAD_DIM // 2
    inv = ROPE_THETA ** (-jnp.arange(half, dtype=f32) / half)
    ang = pos.astype(f32)[:, None] * inv[None, :]
    cos = jnp.tile(jnp.cos(ang), (1, BLOCK // half))
    sin = jnp.tile(jnp.sin(ang), (1, BLOCK // half))
    sinks8 = jnp.broadcast_to(attn_sinks.reshape(ATT_HEADS, 1), (ATT_HEADS, BLOCK))
    ng = hg_norm_g.reshape(1, HG_K)

    h0, h0b, pa, pg = emb_inproj(xs, metablk, vec(ln_emb_g), vec(ln_emb_b), wf_in, nb)
    og, sprev = hgrn_fwd(pa, hg_lower_bounds, ng, nb)
    oatt = attn_fwd(pa, cos, sin, sinks8, nb)
    (dh0p, dpg, dog, doa, dyh, dya, mixin, dr1, h1b, dau, sact, dr2,
     loss_part, dg1, db1, dg2, db2) = mid_rows(h0, pg, og, oatt, ts, wf_bh, wf_ba, wf_out, wf_fi, wf_fo,
                                              ln1_g, ln1_b, ln2_g, ln2_b, nb)
    dhq, dhf, dhi, dhg, dlb4, dng = hgrn_bwd(pa, hg_lower_bounds, ng, sprev, dog, nb)
    daq, dkc, dkp, dvc, dvp, dkm, dvm, dsk = attn_bwd(pa, cos, sin, sinks8, doa, nb)
    dproj, dxp, dlg, dlb = inproj_bwd(dh0p, dhq, dhf, dhi, dhg, daq, dkc, dkp, dvc, dvp, dkm, dvm, dpg,
                                      wf_in, xs, metablk, vec(ln_emb_g), vec(ln_emb_b), nb)

    tp = BLOCK * (5 if nb % 5 == 0 else 1)
    gw_in = wgrad(h0b, dproj, "wgrad_in", D_MODEL, IN_W // 2, tp, False)
    gw_in = gw_in.reshape(D_MODEL, N_SHARD, IN_W // N_SHARD).transpose(1, 0, 2)
    gw_bh = wgrad(og, dyh, "wgrad_bh", 512, 256, tp, True)
    gw_ba = wgrad(oatt, dya, "wgrad_ba", 512, 256, tp, True)
    gw_out = wgrad(mixin, dr1, "wgrad_out", D_MODEL, D_MODEL, tp, False).reshape(N_SHARD, -1, D_MODEL)
    gw_fi = wgrad(h1b, dau, "wgrad_fi", D_MODEL, 2 * D_FF // N_SHARD, tp, True)
    gw_fo = wgrad(sact, dr2, "wgrad_fo", D_FF // 2, D_MODEL, tp, False).reshape(N_SHARD, -1, D_MODEL)
    grads = [gw_in, gw_bh, gw_ba, gw_out, gw_fi, gw_fo]

    parts = [(dlg, 8), (dlb, 8), (dlb4, 8), (dng, 8), (dsk[:, 0], 8),
             (dg1, 8), (db1, 8), (dg2, 8), (db2, 8), (dxp[PAD:BLOCK], BLOCK)]
    small = jnp.concatenate([_rows128(a, r) for a, r in parts], axis=0)

    c_idx = jnp.reshape(ic, (1,)).astype(jnp.int32)
    *others, gathered = pair_exchange_halves(grads, small)
    sums = [add_pair(g, o, c_idx, "add_pair_%d" % n) for n, (g, o) in enumerate(zip(grads, others))]
    quads = chip_exchange(sums)
    jc_idx = jnp.stack([shard, ic]).astype(jnp.int32)
    halves = [add_four(s, q, jc_idx, "add_four_%d" % n) for n, (s, q) in enumerate(zip(sums, quads))]
    red = [r.reshape(-1, r.shape[-1]) for r in pair_exchange_results(halves)]
    small_sum = sum_devices(gathered)

    offs, acc = [], 0
    for _, r in parts:
        offs.append(acc)
        acc += r
    take = lambda n, size: small_sum[offs[n]:offs[n] + parts[n][1]].reshape(-1)[:size]
    g_meta_full = take(9, N_META * D_MODEL).reshape(N_META, D_MODEL)
    g_small = {
        "meta_tokens": lax.dynamic_slice_in_dim(g_meta_full, shard * (D_MODEL // N_SHARD), D_MODEL // N_SHARD, axis=1),
        "ln_emb_g": take(0, D_MODEL), "ln_emb_b": take(1, D_MODEL),
        "hg_lower_bounds": take(2, 2 * HG_HEADS * HG_K).reshape(2, HG_HEADS * HG_K),
        "hg_norm_g": take(3, HG_K).reshape(1, HG_K), "attn_sinks": take(4, ATT_HEADS).reshape(1, ATT_HEADS),
        "ln1_g": take(5, D_MODEL).reshape(1, D_MODEL), "ln1_b": take(6, D_MODEL).reshape(1, D_MODEL),
        "ln2_g": take(7, D_MODEL).reshape(1, D_MODEL), "ln2_b": take(8, D_MODEL).reshape(1, D_MODEL),
    }
    g_big = {"w_in": red[0], "w_branch_hg": red[1], "w_branch_attn": red[2], "w_out": red[3],
             "w_ffn_in": red[4], "w_ffn_out": red[5]}

    names = ["meta_tokens", "ln_emb_g", "ln_emb_b", "w_in", "hg_lower_bounds", "hg_norm_g", "attn_sinks",
             "w_branch_hg", "w_branch_attn", "w_out", "ln1_g", "ln1_b", "w_ffn_in", "w_ffn_out", "ln2_g", "ln2_b"]
    given = dict(
        meta_tokens=(meta_tokens, m_meta_tokens, v_meta_tokens), ln_emb_g=(ln_emb_g, m_ln_emb_g, v_ln_emb_g),
        ln_emb_b=(ln_emb_b, m_ln_emb_b, v_ln_emb_b), w_in=(w_in, m_w_in, v_w_in),
        hg_lower_bounds=(hg_lower_bounds, m_hg_lower_bounds, v_hg_lower_bounds),
        hg_norm_g=(hg_norm_g, m_hg_norm_g, v_hg_norm_g), attn_sinks=(attn_sinks, m_attn_sinks, v_attn_sinks),
        w_branch_hg=(w_branch_hg, m_w_branch_hg, v_w_branch_hg),
        w_branch_attn=(w_branch_attn, m_w_branch_attn, v_w_branch_attn), w_out=(w_out, m_w_out, v_w_out),
        ln1_g=(ln1_g, m_ln1_g, v_ln1_g), ln1_b=(ln1_b, m_ln1_b, v_ln1_b), w_ffn_in=(w_ffn_in, m_w_ffn_in, v_w_ffn_in),
        w_ffn_out=(w_ffn_out, m_w_ffn_out, v_w_ffn_out), ln2_g=(ln2_g, m_ln2_g, v_ln2_g), ln2_b=(ln2_b, m_ln2_b, v_ln2_b))
    out_g, out_d, out_m, out_v = [], [], [], []
    for nm in names:
        w, m, v = given[nm]
        shape = w.shape
        g = g_big[nm] if nm in g_big else g_small[nm]
        two_d = (lambda a: a.reshape(8, BLOCK)) if w.ndim == 1 else (lambda a: a.reshape(a.shape[-2], a.shape[-1]))
        d, mn, vn = adamw(two_d(w), two_d(g), two_d(m), two_d(v), "adamw_" + nm)
        out_g.append(g.reshape(shape))
        out_d.append(d.reshape(shape))
        out_m.append(mn.reshape(shape))
        out_v.append(vn.reshape(shape))

    loss = lax.psum(loss_part[0, 0], ("x", "y", "c"))
    grad_x = dxp[BLOCK:].reshape(x.shape)
    return (loss, grad_x, *out_g, *out_d, *out_m, *out_v)
```

```python
import functools

import jax
import jax.numpy as jnp
from jax import lax
from jax.experimental import pallas as pl
from jax.experimental.pallas import tpu as pltpu

f32 = jnp.float32
bf16 = jnp.bfloat16

D_MODEL = 1024
BLOCK = 128
N_META = 16
PAD = BLOCK - N_META
HG_HEADS = 4
HG_K = 128
SUB = 16
ATT_HEADS = 8
HEAD_DIM = 64
ATT_QW = ATT_HEADS * HEAD_DIM
D_FF = 2816
EPS = 1e-5
ALPHA = 2.0 ** 0.25
ROPE_THETA = 10000.0
N_A = 2816
N_G = 2048
IN_W = N_A + N_G
N_SHARD = 4
N_DEV = 8

ADAM_LR = 0.001
ADAM_B1 = 0.9
ADAM_B2 = 0.999
ADAM_EPS = 1e-08
ADAM_WD = 0.01
ADAM_STEP = 10

VMEM_LIMIT = 56 * 1024 * 1024
MESH = pl.DeviceIdType.MESH


def _cparams(sem, vmem=VMEM_LIMIT):
    return pltpu.CompilerParams(dimension_semantics=sem, vmem_limit_bytes=vmem)


def _const_spec(shape):
    zeros = (0,) * len(shape)
    return pl.BlockSpec(shape, lambda *_: zeros, pipeline_mode=pl.Buffered(1))


def _dot(a, b, ca, cb):
    return lax.dot_general(a.astype(bf16), b.astype(bf16), (((ca,), (cb,)), ((), ())),
                           preferred_element_type=f32)


@jax.custom_vjp
def mm(a, b):
    return _dot(a, b, 1, 0)


mm.defvjp(lambda a, b: (_dot(a, b, 1, 0), (a, b)),
          lambda r, g: (_dot(g, r[1], 1, 1), _dot(r[0], g, 0, 0)))


@jax.custom_vjp
def mm_nt(a, b):
    return _dot(a, b, 1, 1)


mm_nt.defvjp(lambda a, b: (_dot(a, b, 1, 1), (a, b)),
             lambda r, g: (_dot(g, r[1], 1, 0), _dot(g, r[0], 0, 0)))


@jax.custom_vjp
def mm_tn(a, b):
    return _dot(a, b, 0, 0)


mm_tn.defvjp(lambda a, b: (_dot(a, b, 0, 0), (a, b)),
             lambda r, g: (_dot(r[1], g, 1, 1), _dot(r[0], g, 1, 0)))


@functools.partial(jax.custom_vjp, nondiff_argnums=(1,))
def roll_lanes(x, shift):
    return pltpu.roll(x, shift, 1)


roll_lanes.defvjp(lambda x, shift: (pltpu.roll(x, shift, 1), None),
                  lambda shift, _, g: (pltpu.roll(g, (128 - shift) % 128, 1),))


def _sigmoid(x):
    return 1.0 / (1.0 + jnp.exp(-x))


def _ln_stats(x):
    mu = jnp.mean(x, axis=-1, keepdims=True)
    xc = x - mu
    var = jnp.mean(xc * xc, axis=-1, keepdims=True)
    rs = lax.rsqrt(var + EPS)
    return xc * rs, rs


def _ln_bwd(dy, xh, rs, g):
    dxh = dy * g
    m1 = jnp.mean(dxh, axis=-1, keepdims=True)
    m2 = jnp.mean(dxh * xh, axis=-1, keepdims=True)
    return rs * (dxh - m1 - xh * m2)


def _row_ids(i):
    return i * BLOCK + lax.broadcasted_iota(jnp.int32, (BLOCK, 1), 0)


def emb_inproj(x, metablk, g, b, w_in, nb):
    def body(x_ref, mb_ref, g_ref, b_ref, w_ref, h0_ref, h0b_ref, pa_ref, pg_ref):
        i = pl.program_id(0)
        xb = jnp.where(i == 0, mb_ref[...], x_ref[...])
        xh, _ = _ln_stats(xb)
        y = xh * g_ref[...] + b_ref[...]
        y = jnp.where(_row_ids(i) >= PAD, y, 0.0)
        h0_ref[...] = y
        yb = y.astype(bf16)
        h0b_ref[...] = yb
        pa_ref[...] = jnp.dot(yb, w_ref[:, :N_A], preferred_element_type=f32)
        pg_ref[...] = jnp.dot(yb, w_ref[:, N_A:], preferred_element_type=f32)

    p = nb * BLOCK
    row = lambda n: pl.BlockSpec((BLOCK, n), lambda i: (i, 0))
    return pl.pallas_call(
        body, name="emb_inproj", grid=(nb,),
        in_specs=[pl.BlockSpec((BLOCK, D_MODEL), lambda i: (jnp.maximum(i - 1, 0), 0)),
                  _const_spec((BLOCK, D_MODEL)), _const_spec((1, D_MODEL)), _const_spec((1, D_MODEL)),
                  _const_spec((D_MODEL, IN_W))],
        out_specs=[row(D_MODEL), row(D_MODEL), row(N_A), row(N_G)],
        out_shape=[jax.ShapeDtypeStruct((p, D_MODEL), f32), jax.ShapeDtypeStruct((p, D_MODEL), bf16),
                   jax.ShapeDtypeStruct((p, N_A), f32), jax.ShapeDtypeStruct((p, N_G), f32)],
        compiler_params=_cparams(("parallel",)),
    )(x, metablk, g, b, w_in)


def _hgrn_chunk(valid, st, hq, hf, hi, hg, lbraw, ng):
    lb = _sigmoid(lbraw[0:1] - lbraw[1:2])
    q = hq * _sigmoid(hq)
    fg = lb + (1.0 - lb) * _sigmoid(hf)
    logf = jnp.where(valid, jnp.log(fg), 0.0)
    k = jnp.where(valid, 1.0 - fg, 0.0)
    v = hi
    r = lax.broadcasted_iota(jnp.int32, (BLOCK, BLOCK), 0)
    c = lax.broadcasted_iota(jnp.int32, (BLOCK, BLOCK), 1)
    tril = (c <= r).astype(f32)
    bcum = jnp.dot(tril, logf, precision=lax.Precision.HIGHEST, preferred_element_type=f32)
    blast = bcum[BLOCK - 1:BLOCK]
    rows = lax.broadcasted_iota(jnp.int32, (BLOCK, 1), 0)
    sub8 = lax.broadcasted_iota(jnp.int32, (BLOCK // 8, 8, HG_K), 1)
    b8 = bcum.reshape(BLOCK // 8, 8, HG_K)
    row_of_8 = lambda j: jnp.broadcast_to(b8[:, j:j + 1, :], b8.shape)
    a = jnp.where(r == c, jnp.sum(q * k, axis=-1, keepdims=True), 0.0)
    seg = BLOCK
    while seg >= 2:
        half = seg // 2
        if seg >= 8:
            bs = bcum.reshape(BLOCK // seg, seg, HG_K)
            ref = jnp.broadcast_to(bs[:, half - 1:half, :], bs.shape)
        elif seg == 4:
            ref = jnp.where(sub8 < 4, row_of_8(1), row_of_8(5))
        else:
            ref = jnp.where(sub8 < 2, row_of_8(0), jnp.where(sub8 < 4, row_of_8(2),
                                                             jnp.where(sub8 < 6, row_of_8(4), row_of_8(6))))
        ref = ref.reshape(BLOCK, HG_K)
        upper = (rows % seg) >= half
        q_up = q * jnp.exp(jnp.where(upper, bcum - ref, -jnp.inf))
        k_lo = k * jnp.exp(jnp.where(upper, -jnp.inf, ref - bcum))
        a = a + jnp.where((r // seg) == (c // seg), mm_nt(q_up, k_lo), 0.0)
        seg = half
    o = mm_nt(q * jnp.exp(bcum), st) + mm(a, v)
    st_new = st * jnp.exp(blast) + mm_tn(v, k * jnp.exp(blast - bcum))
    on = o * lax.rsqrt(jnp.mean(o * o, axis=-1, keepdims=True) + EPS) * ng
    return st_new, on * (hg * _sigmoid(hg))


def _hgrn_in_specs(rowmap):
    wide = lambda col: pl.BlockSpec((BLOCK, HG_HEADS * HG_K), lambda i: (rowmap(i), col))
    return [wide(0), wide(1), wide(2), wide(3), _const_spec((2, HG_HEADS * HG_K)), _const_spec((1, HG_K))]


def _head(ref, h):
    return ref[:, h * HG_K:(h + 1) * HG_K]


def hgrn_fwd(pa, lbraw, ng, nb):
    def body(hq_ref, hf_ref, hi_ref, hg_ref, lb_ref, ng_ref, og_ref, sp_ref, st_ref):
        i = pl.program_id(0)

        @pl.when(i == 0)
        def _():
            st_ref[...] = jnp.zeros_like(st_ref)

        valid = _row_ids(i) >= PAD
        for h in range(HG_HEADS):
            st = st_ref[h]
            sp_ref[0, h] = st
            st_new, out = _hgrn_chunk(valid, st, _head(hq_ref, h), _head(hf_ref, h), _head(hi_ref, h),
                                      _head(hg_ref, h), _head(lb_ref, h), ng_ref[...])
            st_ref[h] = st_new
            og_ref[:, h * HG_K:(h + 1) * HG_K] = out.astype(bf16)

    p = nb * BLOCK
    return pl.pallas_call(
        body, name="hgrn_fwd", grid=(nb,),
        in_specs=_hgrn_in_specs(lambda i: i),
        out_specs=[pl.BlockSpec((BLOCK, HG_HEADS * HG_K), lambda i: (i, 0)),
                   pl.BlockSpec((1, HG_HEADS, HG_K, HG_K), lambda i: (i, 0, 0, 0))],
        out_shape=[jax.ShapeDtypeStruct((p, HG_HEADS * HG_K), bf16),
                   jax.ShapeDtypeStruct((nb, HG_HEADS, HG_K, HG_K), f32)],
        scratch_shapes=[pltpu.VMEM((HG_HEADS, HG_K, HG_K), f32)],
        compiler_params=_cparams(("arbitrary",)),
    )(pa, pa, pa, pa, lbraw, ng)


def hgrn_bwd(pa, lbraw, ng, sprev, dog, nb):
    def body(hq_ref, hf_ref, hi_ref, hg_ref, lb_ref, ng_ref, sp_ref, do_ref,
             dq_ref, df_ref, di_ref, dg_ref, dlb_ref, dng_ref, dst_ref):
        i = pl.program_id(0)

        @pl.when(i == 0)
        def _():
            dst_ref[...] = jnp.zeros_like(dst_ref)
            dlb_ref[...] = jnp.zeros_like(dlb_ref)
            dng_ref[...] = jnp.zeros_like(dng_ref)

        valid = _row_ids(nb - 1 - i) >= PAD
        dng_sum = jnp.zeros((1, HG_K), f32)
        for h in range(HG_HEADS):
            cols = slice(h * HG_K, (h + 1) * HG_K)
            _, vjp = jax.vjp(functools.partial(_hgrn_chunk, valid), sp_ref[0, h], _head(hq_ref, h), _head(hf_ref, h),
                             _head(hi_ref, h), _head(hg_ref, h), _head(lb_ref, h), ng_ref[...])
            dst, dq, df, di, dg, dlb, dng = vjp((dst_ref[h], _head(do_ref, h)))
            dst_ref[h] = dst
            dq_ref[:, cols] = dq.astype(bf16)
            df_ref[:, cols] = df.astype(bf16)
            di_ref[:, cols] = di.astype(bf16)
            dg_ref[:, cols] = dg.astype(bf16)
            dlb_ref[:, cols] += dlb
            dng_sum = dng_sum + dng
        dng_ref[...] += dng_sum

    p = nb * BLOCK
    rev = lambda i: nb - 1 - i
    hw = HG_HEADS * HG_K
    blk = pl.BlockSpec((BLOCK, hw), lambda i: (rev(i), 0))
    wide = jax.ShapeDtypeStruct((p, hw), bf16)
    return pl.pallas_call(
        body, name="hgrn_bwd", grid=(nb,),
        in_specs=_hgrn_in_specs(rev) + [pl.BlockSpec((1, HG_HEADS, HG_K, HG_K), lambda i: (rev(i), 0, 0, 0)), blk],
        out_specs=[blk, blk, blk, blk, pl.BlockSpec((2, hw), lambda i: (0, 0)), pl.BlockSpec((1, HG_K), lambda i: (0, 0))],
        out_shape=[wide, wide, wide, wide, jax.ShapeDtypeStruct((2, hw), f32), jax.ShapeDtypeStruct((1, HG_K), f32)],
        scratch_shapes=[pltpu.VMEM((HG_HEADS, HG_K, HG_K), f32)],
        compiler_params=_cparams(("arbitrary",)),
    )(pa, pa, pa, pa, lbraw, ng, sprev, dog)


def _rope(x, cos, sin):
    lane = lax.broadcasted_iota(jnp.int32, x.shape, 1)
    rot = jnp.where(lane % HEAD_DIM < HEAD_DIM // 2, -roll_lanes(x, BLOCK - HEAD_DIM // 2),
                    roll_lanes(x, HEAD_DIM // 2))
    return x * cos + rot * sin


def _both_halves(x, g):
    lo = lax.broadcasted_iota(jnp.int32, x.shape, 1) < HEAD_DIM
    sw = roll_lanes(x, HEAD_DIM)
    return jnp.where(lo, x, sw) if g == 0 else jnp.where(lo, sw, x)


def _attn_block(band_ok, meta_ok, tabs, q, kp, kc, vp, vc, km, vm, *sinks):
    cq, sq, cp, sp, cm, sm = tabs
    neg = jnp.finfo(f32).min
    scale = HEAD_DIM ** -0.5
    group = ATT_HEADS // 2
    kpr, kcr, kmr = _rope(kp, cp, sp), _rope(kc, cq, sq), _rope(km, cm, sm)
    lo = lax.broadcasted_iota(jnp.int32, (BLOCK, BLOCK), 1) < HEAD_DIM
    t = lax.broadcasted_iota(jnp.int32, (group * BLOCK, BLOCK), 0) % BLOCK
    own = lax.broadcasted_iota(jnp.int32, (group * BLOCK, BLOCK), 1) <= t
    qr = [_rope(q[:, m * BLOCK:(m + 1) * BLOCK], cq, sq) for m in range(ATT_HEADS // 2)]
    slabs = []
    for g in range(2):
        kp_g, kc_g, vp_g, vc_g, km_g, vm_g = [_both_halves(a, g) for a in (kpr, kcr, vp, vc, kmr, vm)]
        qs = jnp.concatenate([jnp.where(lo if h % 2 == 0 else ~lo, qr[2 * g + h // 2], 0.0) for h in range(group)],
                             axis=0)
        sink = jnp.concatenate([jnp.broadcast_to(sinks[group * g + h], (BLOCK, 1)) for h in range(group)], axis=0)
        sb = jnp.where(band_ok, jnp.where(own, mm_nt(qs, kc_g), mm_nt(qs, kp_g)) * scale, neg)
        sme = jnp.where(meta_ok, mm_nt(qs, km_g) * scale, neg)
        mx = lax.stop_gradient(jnp.maximum(jnp.maximum(jnp.max(sb, axis=-1, keepdims=True),
                                                       jnp.max(sme, axis=-1, keepdims=True)), sink))
        eb, em = jnp.exp(sb - mx), jnp.exp(sme - mx)
        inv = 1.0 / (jnp.sum(eb, axis=-1, keepdims=True) + jnp.sum(em, axis=-1, keepdims=True) + jnp.exp(sink - mx))
        pb = eb * inv
        o = mm(jnp.where(own, pb, 0.0), vc_g) + mm(jnp.where(own, 0.0, pb), vp_g) + mm(em * inv, vm_g)
        for m in range(2):
            slabs.append(jnp.where(lo, o[2 * m * BLOCK:(2 * m + 1) * BLOCK], o[(2 * m + 1) * BLOCK:(2 * m + 2) * BLOCK]))
    return jnp.concatenate(slabs, axis=1)


def _attn_masks(i):
    group = ATT_HEADS // 2
    t = lax.broadcasted_iota(jnp.int32, (group * BLOCK, BLOCK), 0) % BLOCK
    s = lax.broadcasted_iota(jnp.int32, (group * BLOCK, BLOCK), 1)
    kpos = jnp.where(s <= t, i * BLOCK - PAD + s, jnp.where(i > 0, (i - 1) * BLOCK - PAD + s, -1))
    band_ok = kpos >= N_META
    qpos = i * BLOCK - PAD + lax.broadcasted_iota(jnp.int32, (group * BLOCK, 1), 0) % BLOCK
    meta_ok = lax.broadcasted_iota(jnp.int32, (1, N_META), 1) <= qpos
    return band_ok, meta_ok


def _attn_in_specs():
    prev = lambda i: jnp.maximum(i - 1, 0)
    kcol, vcol = N_A // BLOCK - 2, N_A // BLOCK - 1
    blk = lambda rowmap, col: pl.BlockSpec((BLOCK, BLOCK), lambda i: (rowmap(i), col))
    cur, first = (lambda i: i), (lambda i: 0)
    return [pl.BlockSpec((BLOCK, ATT_QW), lambda i: (i, 4)),
            blk(prev, kcol), blk(cur, kcol), blk(prev, vcol), blk(cur, vcol), blk(first, kcol), blk(first, vcol),
            blk(cur, 0), blk(cur, 0), blk(prev, 0), blk(prev, 0), blk(first, 0), blk(first, 0),
            _const_spec((ATT_HEADS, BLOCK))]


def _attn_operands(q_ref, kp_ref, kc_ref, vp_ref, vc_ref, km_ref, vm_ref, cq, sq, cp, sp, cm, sm, sk_ref):
    tabs = (cq[...], sq[...], cp[...], sp[...], cm[PAD:, :], sm[PAD:, :])
    args = (q_ref[...], kp_ref[...], kc_ref[...], vp_ref[...], vc_ref[...], km_ref[PAD:, :], vm_ref[PAD:, :])
    sinks = tuple(sk_ref[j:j + 1, 0:1] for j in range(ATT_HEADS))
    return tabs, args + sinks


def attn_fwd(pa, cos, sin, sinks8, nb):
    def body(*refs):
        o_ref = refs[-1]
        band_ok, meta_ok = _attn_masks(pl.program_id(0))
        tabs, args = _attn_operands(*refs[:-1])
        o_ref[...] = _attn_block(band_ok, meta_ok, tabs, *args).astype(bf16)

    return pl.pallas_call(
        body, name="attn_fwd", grid=(nb,), in_specs=_attn_in_specs(),
        out_specs=pl.BlockSpec((BLOCK, ATT_QW), lambda i: (i, 0)),
        out_shape=jax.ShapeDtypeStruct((nb * BLOCK, ATT_QW), bf16),
        compiler_params=_cparams(("parallel",)),
    )(pa, pa, pa, pa, pa, pa, pa, cos, sin, cos, sin, cos, sin, sinks8)


def attn_bwd(pa, cos, sin, sinks8, do, nb):
    def body(*refs):
        do_ref = refs[14]
        dq_ref, dkc_ref, dkp_ref, dvc_ref, dvp_ref, dkm_ref, dvm_ref, dsk_ref = refs[15:]
        i = pl.program_id(0)

        @pl.when(i == 0)
        def _():
            dkm_ref[...] = jnp.zeros((N_META, BLOCK), f32)
            dvm_ref[...] = jnp.zeros((N_META, BLOCK), f32)
            dsk_ref[...] = jnp.zeros((ATT_HEADS, BLOCK), f32)

        band_ok, meta_ok = _attn_masks(i)
        tabs, args = _attn_operands(*refs[:14])
        _, vjp = jax.vjp(functools.partial(_attn_block, band_ok, meta_ok, tabs), *args)
        grads = vjp(do_ref[...])
        dq_ref[...] = grads[0].astype(bf16)
        dkp_ref[...] = grads[1]
        dkc_ref[...] = grads[2]
        dvp_ref[...] = grads[3]
        dvc_ref[...] = grads[4]
        dkm_ref[...] += grads[5]
        dvm_ref[...] += grads[6]
        for j in range(ATT_HEADS):
            dsk_ref[j:j + 1, :] += jnp.broadcast_to(grads[7 + j], (1, BLOCK))

    p = nb * BLOCK
    row = pl.BlockSpec((BLOCK, BLOCK), lambda i: (i, 0))
    const = lambda r: pl.BlockSpec((r, BLOCK), lambda i: (0, 0))
    part = jax.ShapeDtypeStruct((p, BLOCK), f32)
    return pl.pallas_call(
        body, name="attn_bwd", grid=(nb,),
        in_specs=_attn_in_specs() + [pl.BlockSpec((BLOCK, ATT_QW), lambda i: (i, 0))],
        out_specs=[pl.BlockSpec((BLOCK, ATT_QW), lambda i: (i, 0)), row, row, row, row,
                   const(N_META), const(N_META), const(ATT_HEADS)],
        out_shape=[jax.ShapeDtypeStruct((p, ATT_QW), bf16), part, part, part, part,
                   jax.ShapeDtypeStruct((N_META, BLOCK), f32), jax.ShapeDtypeStruct((N_META, BLOCK), f32),
                   jax.ShapeDtypeStruct((ATT_HEADS, BLOCK), f32)],
        compiler_params=_cparams(("arbitrary",)),
    )(pa, pa, pa, pa, pa, pa, pa, cos, sin, cos, sin, cos, sin, sinks8, do)


def mid_rows(h0, pg, og, oatt, target, wbh, wba, wout, wfi, wfo, ln1g, ln1b, ln2g, ln2b, nb):
    def body(h0_ref, pg_ref, og_ref, oa_ref, t_ref, wbh_ref, wba_ref, wo_ref, wfi_ref, wfo_ref,
             g1_ref, b1_ref, g2_ref, b2_ref,
             dh0_ref, dpg_ref, dog_ref, doa_ref, dyh_ref, dya_ref, mix_ref, dr1_ref, h1b_ref, dau_ref, s_ref, dr2_ref,
             loss_ref, dg1_ref, db1_ref, dg2_ref, db2_ref):
        i = pl.program_id(0)

        @pl.when(i == 0)
        def _():
            loss_ref[...] = jnp.zeros_like(loss_ref)
            for r in (dg1_ref, db1_ref, dg2_ref, db2_ref):
                r[...] = jnp.zeros_like(r)

        g1, b1, g2, b2 = g1_ref[...], b1_ref[...], g2_ref[...], b2_ref[...]
        yh = jnp.dot(og_ref[...], wbh_ref[...], preferred_element_type=f32)
        ya = jnp.dot(oa_ref[...], wba_ref[...], preferred_element_type=f32)
        gh = _sigmoid(pg_ref[:, :D_MODEL])
        ga = _sigmoid(pg_ref[:, D_MODEL:])
        mixin = (gh * yh + ga * ya).astype(bf16)
        mix_ref[...] = mixin
        r1 = ALPHA * h0_ref[...] + jnp.dot(mixin, wo_ref[...], preferred_element_type=f32)
        xh1, rs1 = _ln_stats(r1)
        h1 = xh1 * g1 + b1
        h1b = h1.astype(bf16)
        h1b_ref[...] = h1b
        au = jnp.dot(h1b, wfi_ref[...], preferred_element_type=f32)
        a, u = au[:, :D_FF], au[:, D_FF:]
        sg = _sigmoid(a)
        sa = a * sg
        s = (sa * u).astype(bf16)
        s_ref[...] = s
        r2 = ALPHA * h1 + jnp.dot(s, wfo_ref[...], preferred_element_type=f32)
        xh2, rs2 = _ln_stats(r2)
        diff = jnp.where(i > 0, xh2 * g2 + b2 - t_ref[...], 0.0)
        loss_ref[...] += jnp.sum(diff * diff) * (0.5 / D_MODEL)
        dy = diff * (1.0 / D_MODEL)
        dg2_ref[...] += jnp.sum(dy * xh2, axis=0, keepdims=True)
        db2_ref[...] += jnp.sum(dy, axis=0, keepdims=True)
        dr2 = _ln_bwd(dy, xh2, rs2, g2)
        dr2b = dr2.astype(bf16)
        dr2_ref[...] = dr2b
        ds = _dot(dr2b, wfo_ref[...], 1, 1)
        da = (ds * u) * (sg * (1.0 + a * (1.0 - sg)))
        du = ds * sa
        dau = jnp.concatenate([da, du], axis=1).astype(bf16)
        dau_ref[...] = dau
        dh1 = ALPHA * dr2 + _dot(dau, wfi_ref[...], 1, 1)
        dg1_ref[...] += jnp.sum(dh1 * xh1, axis=0, keepdims=True)
        db1_ref[...] += jnp.sum(dh1, axis=0, keepdims=True)
        dr1 = _ln_bwd(dh1, xh1, rs1, g1)
        dr1b = dr1.astype(bf16)
        dr1_ref[...] = dr1b
        dh0_ref[...] = ALPHA * dr1
        dmix = _dot(dr1b, wo_ref[...], 1, 1)
        dyh = (dmix * gh).astype(bf16)
        dya = (dmix * ga).astype(bf16)
        dyh_ref[...] = dyh
        dya_ref[...] = dya
        dpg_ref[:, :D_MODEL] = (dmix * yh * gh * (1.0 - gh)).astype(bf16)
        dpg_ref[:, D_MODEL:] = (dmix * ya * ga * (1.0 - ga)).astype(bf16)
        dog_ref[...] = _dot(dyh, wbh_ref[...], 1, 1)
        doa_ref[...] = _dot(dya, wba_ref[...], 1, 1)

    p = nb * BLOCK
    row = lambda n: pl.BlockSpec((BLOCK, n), lambda i: (i, 0))
    vec = lambda: pl.BlockSpec((1, D_MODEL), lambda i: (0, 0))
    sds = lambda n, dt: jax.ShapeDtypeStruct((p, n), dt)
    hw = HG_HEADS * HG_K
    return pl.pallas_call(
        body, name="mid_rows", grid=(nb,),
        in_specs=[row(D_MODEL), row(N_G), row(hw), row(ATT_QW),
                  pl.BlockSpec((BLOCK, D_MODEL), lambda i: (jnp.maximum(i - 1, 0), 0)),
                  _const_spec((hw, D_MODEL)), _const_spec((ATT_QW, D_MODEL)), _const_spec((D_MODEL, D_MODEL)),
                  _const_spec((D_MODEL, 2 * D_FF)), _const_spec((D_FF, D_MODEL)),
                  _const_spec((1, D_MODEL)), _const_spec((1, D_MODEL)), _const_spec((1, D_MODEL)),
                  _const_spec((1, D_MODEL))],
        out_specs=[row(D_MODEL), row(N_G), row(hw), row(ATT_QW), row(D_MODEL), row(D_MODEL), row(D_MODEL),
                   row(D_MODEL), row(D_MODEL), row(2 * D_FF), row(D_FF), row(D_MODEL),
                   pl.BlockSpec((1, 1), lambda i: (0, 0)), vec(), vec(), vec(), vec()],
        out_shape=[sds(D_MODEL, f32), sds(N_G, bf16), sds(hw, f32), sds(ATT_QW, f32), sds(D_MODEL, bf16),
                   sds(D_MODEL, bf16), sds(D_MODEL, bf16), sds(D_MODEL, bf16), sds(D_MODEL, bf16),
                   sds(2 * D_FF, bf16), sds(D_FF, bf16), sds(D_MODEL, bf16),
                   jax.ShapeDtypeStruct((1, 1), f32)] + [jax.ShapeDtypeStruct((1, D_MODEL), f32)] * 4,
        compiler_params=_cparams(("arbitrary",)),
    )(h0, pg, og, oatt, target, wbh, wba, wout, wfi, wfo, ln1g, ln1b, ln2g, ln2b)


def inproj_bwd(dh0p, dhq, dhf, dhi, dhg, daq, dkc, dkp, dvc, dvp, dkm, dvm, dpg, w_in, x, metablk, g, b, nb):
    def body(dh0_ref, dq_ref, df_ref, di_ref, dg_ref, daq_ref, dkc_ref, dkp_ref, dvc_ref, dvp_ref, dkm_ref, dvm_ref,
             dpg_ref, w_ref, x_ref, mb_ref, g_ref, b_ref, dproj_ref, dx_ref, dlg_ref, dlb_ref):
        i = pl.program_id(0)

        @pl.when(i == 0)
        def _():
            dlg_ref[...] = jnp.zeros_like(dlg_ref)
            dlb_ref[...] = jnp.zeros_like(dlb_ref)

        zero_pad = jnp.zeros((PAD, BLOCK), f32)
        has_next = i + 1 < nb
        first = i == 0

        def keys(cur_ref, next_ref, meta_ref):
            t = cur_ref[...] + jnp.where(has_next, next_ref[...], 0.0)
            return t + jnp.where(first, jnp.concatenate([zero_pad, meta_ref[...]], axis=0), 0.0)

        dproj = jnp.concatenate(
            [dq_ref[...], df_ref[...], di_ref[...], dg_ref[...], daq_ref[...],
             keys(dkc_ref, dkp_ref, dkm_ref).astype(bf16), keys(dvc_ref, dvp_ref, dvm_ref).astype(bf16),
             dpg_ref[...]], axis=1)
        dproj_ref[...] = dproj
        valid = _row_ids(i) >= PAD
        dh0 = jnp.where(valid, dh0_ref[...] + _dot(dproj, w_ref[...], 1, 1), 0.0)
        xb = jnp.where(first, mb_ref[...], x_ref[...])
        xh, rs = _ln_stats(xb)
        dlg_ref[...] += jnp.sum(dh0 * xh, axis=0, keepdims=True)
        dlb_ref[...] += jnp.sum(dh0, axis=0, keepdims=True)
        dx_ref[...] = jnp.where(valid, _ln_bwd(dh0, xh, rs, g_ref[...]), 0.0)

    p = nb * BLOCK
    row = lambda n: pl.BlockSpec((BLOCK, n), lambda i: (i, 0))
    nxt = pl.BlockSpec((BLOCK, BLOCK), lambda i: (jnp.minimum(i + 1, nb - 1), 0))
    hw = HG_HEADS * HG_K
    vec = lambda: pl.BlockSpec((1, D_MODEL), lambda i: (0, 0))
    return pl.pallas_call(
        body, name="inproj_bwd", grid=(nb,),
        in_specs=[row(D_MODEL), row(hw), row(hw), row(hw), row(hw), row(ATT_QW),
                  row(BLOCK), nxt, row(BLOCK), nxt, _const_spec((N_META, BLOCK)), _const_spec((N_META, BLOCK)),
                  row(N_G), _const_spec((D_MODEL, IN_W)),
                  pl.BlockSpec((BLOCK, D_MODEL), lambda i: (jnp.maximum(i - 1, 0), 0)),
                  _const_spec((BLOCK, D_MODEL)), _const_spec((1, D_MODEL)), _const_spec((1, D_MODEL))],
        out_specs=[row(IN_W), row(D_MODEL), vec(), vec()],
        out_shape=[jax.ShapeDtypeStruct((p, IN_W), bf16), jax.ShapeDtypeStruct((p, D_MODEL), f32),
                   jax.ShapeDtypeStruct((1, D_MODEL), f32), jax.ShapeDtypeStruct((1, D_MODEL), f32)],
        compiler_params=_cparams(("arbitrary",)),
    )(dh0p, dhq, dhf, dhi, dhg, daq, dkc, dkp, dvc, dvp, dkm, dvm, dpg, w_in, x, metablk, g, b)


def wgrad(a, b, name, tk, tn, tp, by_cols):
    p, k = a.shape
    n = b.shape[1]
    nsteps = p // tp

    def body(a_ref, b_ref, o_ref):
        @pl.when(pl.program_id(2) == 0)
        def _():
            o_ref[...] = jnp.zeros_like(o_ref)

        o_ref[0] += _dot(a_ref[...], b_ref[...], 0, 0)

    if by_cols:
        shard_n = n // N_SHARD
        per = shard_n // tn
        out_shape = (N_SHARD, k, shard_n)
        omap = lambda ik, jn, ip: (jn // per, ik, jn % per)
    else:
        out_shape = (1, k, n)
        omap = lambda ik, jn, ip: (0, ik, jn)
    return pl.pallas_call(
        body, name=name, grid=(k // tk, n // tn, nsteps),
        in_specs=[pl.BlockSpec((tp, tk), lambda ik, jn, ip: (ip, ik)),
                  pl.BlockSpec((tp, tn), lambda ik, jn, ip: (ip, jn))],
        out_specs=pl.BlockSpec((1, tk, tn), omap),
        out_shape=jax.ShapeDtypeStruct(out_shape, f32),
        compiler_params=_cparams(("parallel", "parallel", "arbitrary")),
    )(a, b)


def adamw(w, g, m, v, name):
    r, c = w.shape
    tr = r
    for cand in (256, 176, 128):
        if r > cand and r % cand == 0:
            tr = cand
            break

    def body(w_ref, g_ref, m_ref, v_ref, d_ref, mo_ref, vo_ref):
        gg = g_ref[...]
        mn = ADAM_B1 * m_ref[...] + (1.0 - ADAM_B1) * gg
        vn = ADAM_B2 * v_ref[...] + (1.0 - ADAM_B2) * (gg * gg)
        m_hat = mn / (1.0 - ADAM_B1 ** ADAM_STEP)
        v_hat = vn / (1.0 - ADAM_B2 ** ADAM_STEP)
        d_ref[...] = -ADAM_LR * (m_hat / (jnp.sqrt(v_hat) + ADAM_EPS) + ADAM_WD * w_ref[...])
        mo_ref[...] = mn
        vo_ref[...] = vn

    spec = pl.BlockSpec((tr, c), lambda i: (i, 0))
    sds = jax.ShapeDtypeStruct((r, c), f32)
    return pl.pallas_call(
        body, name=name, grid=(r // tr,), in_specs=[spec] * 4, out_specs=[spec] * 3, out_shape=[sds] * 3,
        compiler_params=_cparams(("parallel",)),
    )(w, g, m, v)


def _me():
    return lax.axis_index("x"), lax.axis_index("y"), lax.axis_index("c")


def _chip_peer(x, y, c, k):
    return (x ^ (k >> 1), y ^ (k & 1), c)


ANY = pl.BlockSpec(memory_space=pl.ANY)


def gather_weights(shards):
    n = len(shards)
    out_dtypes = [bf16 if s.size > 16 * 256 else f32 for s in shards]

    def body(*refs):
        ins, outs = refs[:n], refs[n:2 * n]
        stage = refs[2 * n:3 * n]
        send_sems, recv_sems, local_sems = refs[3 * n:]
        x, y, c = _me()
        j = 2 * x + y
        for w in range(n):
            stage[w][...] = ins[w][...].astype(out_dtypes[w])
        sends, locs = [], []
        for w in range(n):
            loc = pltpu.make_async_copy(stage[w], outs[w].at[j], local_sems.at[w])
            loc.start()
            locs.append(loc)
            for k in (1, 2, 3):
                cp = pltpu.make_async_remote_copy(
                    src_ref=stage[w], dst_ref=outs[w].at[j], send_sem=send_sems.at[w, k - 1],
                    recv_sem=recv_sems.at[w, k - 1], device_id=_chip_peer(x, y, c, k), device_id_type=MESH)
                cp.start()
                sends.append(cp)
        for w in range(n):
            for k in (1, 2, 3):
                pltpu.make_async_remote_copy(
                    src_ref=stage[w], dst_ref=outs[w].at[j ^ k], send_sem=send_sems.at[w, k - 1],
                    recv_sem=recv_sems.at[w, k - 1], device_id=_chip_peer(x, y, c, k), device_id_type=MESH).wait_recv()
        for cp in sends:
            cp.wait_send()
        for loc in locs:
            loc.wait()

    return pl.pallas_call(
        body, name="gather_weights",
        in_specs=[pl.BlockSpec(memory_space=pltpu.VMEM)] * n, out_specs=[ANY] * n,
        out_shape=[jax.ShapeDtypeStruct((N_SHARD,) + s.shape, dt) for s, dt in zip(shards, out_dtypes)],
        scratch_shapes=[pltpu.VMEM(s.shape, dt) for s, dt in zip(shards, out_dtypes)]
        + [pltpu.SemaphoreType.DMA((n, 3)), pltpu.SemaphoreType.DMA((n, 3)), pltpu.SemaphoreType.DMA((n,))],
        compiler_params=pltpu.CompilerParams(vmem_limit_bytes=VMEM_LIMIT),
    )(*shards)


def pair_exchange_halves(grads, small):
    n = len(grads)

    def body(*refs):
        ins, small_ref = refs[:n], refs[n]
        outs, gath = refs[n + 1:2 * n + 1], refs[2 * n + 1]
        send_sems, recv_sems, s_send, s_recv, local_sem = refs[2 * n + 2:]
        x, y, c = _me()
        me = 4 * x + 2 * y + c
        sends = []
        for w in range(n):
            half = ins[w].shape[1] // 2
            cp = pltpu.make_async_remote_copy(
                src_ref=ins[w].at[:, pl.ds((1 - c) * half, half), :], dst_ref=outs[w],
                send_sem=send_sems.at[w], recv_sem=recv_sems.at[w], device_id=(x, y, 1 - c), device_id_type=MESH)
            cp.start()
            sends.append(cp)
        loc = pltpu.make_async_copy(small_ref, gath.at[me], local_sem)
        loc.start()
        for k in range(1, N_DEV):
            cp = pltpu.make_async_remote_copy(
                src_ref=small_ref, dst_ref=gath.at[me], send_sem=s_send.at[k - 1], recv_sem=s_recv.at[k - 1],
                device_id=(x ^ (k >> 2), y ^ ((k >> 1) & 1), c ^ (k & 1)), device_id_type=MESH)
            cp.start()
            sends.append(cp)
        for w in range(n):
            half = ins[w].shape[1] // 2
            pltpu.make_async_remote_copy(
                src_ref=ins[w].at[:, pl.ds(0, half), :], dst_ref=outs[w], send_sem=send_sems.at[w],
                recv_sem=recv_sems.at[w], device_id=(x, y, 1 - c), device_id_type=MESH).wait_recv()
        for k in range(1, N_DEV):
            pltpu.make_async_remote_copy(
                src_ref=small_ref, dst_ref=gath.at[me ^ k], send_sem=s_send.at[k - 1], recv_sem=s_recv.at[k - 1],
                device_id=(x ^ (k >> 2), y ^ ((k >> 1) & 1), c ^ (k & 1)), device_id_type=MESH).wait_recv()
        for cp in sends:
            cp.wait_send()
        loc.wait()

    return pl.pallas_call(
        body, name="pair_exchange_halves", in_specs=[ANY] * (n + 1), out_specs=[ANY] * (n + 1),
        out_shape=[jax.ShapeDtypeStruct((g.shape[0], g.shape[1] // 2, g.shape[2]), f32) for g in grads]
        + [jax.ShapeDtypeStruct((N_DEV,) + small.shape, f32)],
        scratch_shapes=[pltpu.SemaphoreType.DMA((n,)), pltpu.SemaphoreType.DMA((n,)),
                        pltpu.SemaphoreType.DMA((N_DEV - 1,)), pltpu.SemaphoreType.DMA((N_DEV - 1,)),
                        pltpu.SemaphoreType.DMA],
    )(*grads, small)


def chip_exchange(sums):
    n = len(sums)

    def body(*refs):
        ins, outs = refs[:n], refs[n:2 * n]
        send_sems, recv_sems = refs[2 * n:]
        x, y, c = _me()
        j = 2 * x + y
        sends = []
        for w in range(n):
            for k in (1, 2, 3):
                cp = pltpu.make_async_remote_copy(
                    src_ref=ins[w].at[j ^ k], dst_ref=outs[w].at[k - 1], send_sem=send_sems.at[w, k - 1],
                    recv_sem=recv_sems.at[w, k - 1], device_id=_chip_peer(x, y, c, k), device_id_type=MESH)
                cp.start()
                sends.append(cp)
        for w in range(n):
            for k in (1, 2, 3):
                pltpu.make_async_remote_copy(
                    src_ref=ins[w].at[0], dst_ref=outs[w].at[k - 1], send_sem=send_sems.at[w, k - 1],
                    recv_sem=recv_sems.at[w, k - 1], device_id=_chip_peer(x, y, c, k), device_id_type=MESH).wait_recv()
        for cp in sends:
            cp.wait_send()

    return pl.pallas_call(
        body, name="chip_exchange", in_specs=[ANY] * n, out_specs=[ANY] * n,
        out_shape=[jax.ShapeDtypeStruct((N_SHARD - 1,) + s.shape[1:], s.dtype) for s in sums],
        scratch_shapes=[pltpu.SemaphoreType.DMA((n, 3)), pltpu.SemaphoreType.DMA((n, 3))],
    )(*sums)


def pair_exchange_results(halves):
    n = len(halves)

    def body(*refs):
        ins, outs = refs[:n], refs[n:2 * n]
        send_sems, recv_sems = refs[2 * n:]
        x, y, c = _me()
        sends = []
        for w in range(n):
            cp = pltpu.make_async_remote_copy(
                src_ref=ins[w].at[c], dst_ref=outs[w].at[c], send_sem=send_sems.at[w], recv_sem=recv_sems.at[w],
                device_id=(x, y, 1 - c), device_id_type=MESH)
            cp.start()
            sends.append(cp)
        for w in range(n):
            pltpu.make_async_remote_copy(
                src_ref=ins[w].at[c], dst_ref=outs[w].at[1 - c], send_sem=send_sems.at[w],
                recv_sem=recv_sems.at[w], device_id=(x, y, 1 - c), device_id_type=MESH).wait_recv()
        for cp in sends:
            cp.wait_send()

    return pl.pallas_call(
        body, name="pair_exchange_results", in_specs=[ANY] * n, out_specs=[ANY] * n,
        out_shape=[jax.ShapeDtypeStruct(h.shape, f32) for h in halves],
        input_output_aliases={w: w for w in range(n)},
        scratch_shapes=[pltpu.SemaphoreType.DMA((n,)), pltpu.SemaphoreType.DMA((n,))],
    )(*halves)


def add_pair(grad, other, c_idx, name):
    _, r, c = grad.shape
    half = r // 2
    tr = half // 2 if (half // 2) % 8 == 0 else half
    per = half // tr

    def body(c_ref, g_ref, o_ref, out_ref):
        out_ref[...] = (g_ref[...] + o_ref[...]).astype(bf16)

    return pl.pallas_call(
        body, name=name,
        grid_spec=pltpu.PrefetchScalarGridSpec(
            num_scalar_prefetch=1, grid=(N_SHARD, per),
            in_specs=[pl.BlockSpec((1, tr, c), lambda j, t, cr: (j, cr[0] * per + t, 0)),
                      pl.BlockSpec((1, tr, c), lambda j, t, cr: (j, t, 0))],
            out_specs=pl.BlockSpec((1, tr, c), lambda j, t, cr: (j, t, 0))),
        out_shape=jax.ShapeDtypeStruct((N_SHARD, half, c), bf16),
        compiler_params=_cparams(("parallel", "parallel")),
    )(c_idx, grad, other)


def add_four(own, parts, jc_idx, name):
    _, half, c = parts.shape
    tr = half // 2 if (half // 2) % 8 == 0 else half

    def body(jc_ref, own_ref, p_ref, out_ref):
        acc = own_ref[0].astype(f32)
        for k in range(N_SHARD - 1):
            acc = acc + p_ref[k].astype(f32)
        out_ref[0] = acc

    return pl.pallas_call(
        body, name=name,
        grid_spec=pltpu.PrefetchScalarGridSpec(
            num_scalar_prefetch=1, grid=(half // tr,),
            in_specs=[pl.BlockSpec((1, tr, c), lambda t, jc: (jc[0], t, 0)),
                      pl.BlockSpec((N_SHARD - 1, tr, c), lambda t, jc: (0, t, 0))],
            out_specs=pl.BlockSpec((1, tr, c), lambda t, jc: (jc[1], t, 0))),
        out_shape=jax.ShapeDtypeStruct((2, half, c), f32),
        compiler_params=_cparams(("parallel",)),
    )(jc_idx, own, parts)


def sum_devices(gathered):
    def body(g_ref, out_ref):
        acc = g_ref[0]
        for d in range(1, N_DEV):
            acc = acc + g_ref[d]
        out_ref[...] = acc

    return pl.pallas_call(body, name="sum_devices", out_shape=jax.ShapeDtypeStruct(gathered.shape[1:], f32))(gathered)


def _rows128(a, rows):
    flat = a.reshape(-1, BLOCK) if a.size % BLOCK == 0 else jnp.pad(a.reshape(1, -1), ((0, 0), (0, BLOCK - a.size)))
    return jnp.pad(flat, ((0, rows - flat.shape[0]), (0, 0)))


def kernel(x, meta_tokens, ln_emb_g, ln_emb_b, w_in, hg_lower_bounds, hg_norm_g, attn_sinks, w_branch_hg, w_branch_attn, w_out, ln1_g, ln1_b, w_ffn_in, w_ffn_out, ln2_g, ln2_b, loss_target, m_meta_tokens, m_ln_emb_g, m_ln_emb_b, m_w_in, m_hg_lower_bounds, m_hg_norm_g, m_attn_sinks, m_w_branch_hg, m_w_branch_attn, m_w_out, m_ln1_g, m_ln1_b, m_w_ffn_in, m_w_ffn_out, m_ln2_g, m_ln2_b, v_meta_tokens, v_ln_emb_g, v_ln_emb_b, v_w_in, v_hg_lower_bounds, v_hg_norm_g, v_attn_sinks, v_w_branch_hg, v_w_branch_attn, v_w_out, v_ln1_g, v_ln1_b, v_w_ffn_in, v_w_ffn_out, v_ln2_g, v_ln2_b):
    seq = x.shape[1]
    nb = seq // BLOCK + 1
    xs = x[0]
    ts = loss_target[0]
    ix, iy, ic = _me()
    shard = 2 * ix + iy
    vec = lambda a: a.reshape(1, D_MODEL)

    big = [w_in[0], w_branch_hg[0], w_branch_attn[0], w_out[0], w_ffn_in[0], w_ffn_out[0]]
    g_in, g_bh, g_ba, g_out, g_fi, g_fo, g_meta = gather_weights(big + [meta_tokens])
    by_cols = lambda g: g.transpose(1, 0, 2).reshape(g.shape[1], N_SHARD * g.shape[2])
    wf_in, wf_bh, wf_ba, wf_fi = by_cols(g_in), by_cols(g_bh), by_cols(g_ba), by_cols(g_fi)
    wf_out = g_out.reshape(D_MODEL, D_MODEL)
    wf_fo = g_fo.reshape(D_FF, D_MODEL)
    metablk = jnp.pad(by_cols(g_meta), ((PAD, 0), (0, 0)))

    pos = jnp.arange(nb * BLOCK, dtype=jnp.int32) - PAD
    half = HEAD_DIM // 2
    inv = ROPE_THETA ** (-jnp.arange(half, dtype=f32) / half)
    ang = pos.astype(f32)[:, None] * inv[None, :]
    cos = jnp.tile(jnp.cos(ang), (1, BLOCK // half))
    sin = jnp.tile(jnp.sin(ang), (1, BLOCK // half))
    sinks8 = jnp.broadcast_to(attn_sinks.reshape(ATT_HEADS, 1), (ATT_HEADS, BLOCK))
    ng = hg_norm_g.reshape(1, HG_K)

    h0, h0b, pa, pg = emb_inproj(xs, metablk, vec(ln_emb_g), vec(ln_emb_b), wf_in, nb)
    og, sprev = hgrn_fwd(pa, hg_lower_bounds, ng, nb)
    oatt = attn_fwd(pa, cos, sin, sinks8, nb)
    (dh0p, dpg, dog, doa, dyh, dya, mixin, dr1, h1b, dau, sact, dr2,
     loss_part, dg1, db1, dg2, db2) = mid_rows(h0, pg, og, oatt, ts, wf_bh, wf_ba, wf_out, wf_fi, wf_fo,
                                              ln1_g, ln1_b, ln2_g, ln2_b, nb)
    dhq, dhf, dhi, dhg, dlb4, dng = hgrn_bwd(pa, hg_lower_bounds, ng, sprev, dog, nb)
    daq, dkc, dkp, dvc, dvp, dkm, dvm, dsk = attn_bwd(pa, cos, sin, sinks8, doa, nb)
    dproj, dxp, dlg, dlb = inproj_bwd(dh0p, dhq, dhf, dhi, dhg, daq, dkc, dkp, dvc, dvp, dkm, dvm, dpg,
                                      wf_in, xs, metablk, vec(ln_emb_g), vec(ln_emb_b), nb)

    tp = BLOCK * (5 if nb % 5 == 0 else 1)
    gw_in = wgrad(h0b, dproj, "wgrad_in", D_MODEL, IN_W // 2, tp, False)
    gw_in = gw_in.reshape(D_MODEL, N_SHARD, IN_W // N_SHARD).transpose(1, 0, 2)
    gw_bh = wgrad(og, dyh, "wgrad_bh", 512, 256, tp, True)
    gw_ba = wgrad(oatt, dya, "wgrad_ba", 512, 256, tp, True)
    gw_out = wgrad(mixin, dr1, "wgrad_out", D_MODEL, D_MODEL, tp, False).reshape(N_SHARD, -1, D_MODEL)
    gw_fi = wgrad(h1b, dau, "wgrad_fi", D_MODEL, 2 * D_FF // N_SHARD, tp, True)
    gw_fo = wgrad(sact, dr2, "wgrad_fo", D_FF // 2, D_MODEL, tp, False).reshape(N_SHARD, -1, D_MODEL)
    grads = [gw_in, gw_bh, gw_ba, gw_out, gw_fi, gw_fo]

    parts = [(dlg, 8), (dlb, 8), (dlb4, 8), (dng, 8), (dsk[:, 0], 8),
             (dg1, 8), (db1, 8), (dg2, 8), (db2, 8), (dxp[PAD:BLOCK], BLOCK)]
    small = jnp.concatenate([_rows128(a, r) for a, r in parts], axis=0)

    c_idx = jnp.reshape(ic, (1,)).astype(jnp.int32)
    *others, gathered = pair_exchange_halves(grads, small)
    sums = [add_pair(g, o, c_idx, "add_pair_%d" % n) for n, (g, o) in enumerate(zip(grads, others))]
    quads = chip_exchange(sums)
    jc_idx = jnp.stack([shard, ic]).astype(jnp.int32)
    halves = [add_four(s, q, jc_idx, "add_four_%d" % n) for n, (s, q) in enumerate(zip(sums, quads))]
    red = [r.reshape(-1, r.shape[-1]) for r in pair_exchange_results(halves)]
    small_sum = sum_devices(gathered)

    offs, acc = [], 0
    for _, r in parts:
        offs.append(acc)
        acc += r
    take = lambda n, size: small_sum[offs[n]:offs[n] + parts[n][1]].reshape(-1)[:size]
    g_meta_full = take(9, N_META * D_MODEL).reshape(N_META, D_MODEL)
    g_small = {
        "meta_tokens": lax.dynamic_slice_in_dim(g_meta_full, shard * (D_MODEL // N_SHARD), D_MODEL // N_SHARD, axis=1),
        "ln_emb_g": take(0, D_MODEL), "ln_emb_b": take(1, D_MODEL),
        "hg_lower_bounds": take(2, 2 * HG_HEADS * HG_K).reshape(2, HG_HEADS * HG_K),
        "hg_norm_g": take(3, HG_K).reshape(1, HG_K), "attn_sinks": take(4, ATT_HEADS).reshape(1, ATT_HEADS),
        "ln1_g": take(5, D_MODEL).reshape(1, D_MODEL), "ln1_b": take(6, D_MODEL).reshape(1, D_MODEL),
        "ln2_g": take(7, D_MODEL).reshape(1, D_MODEL), "ln2_b": take(8, D_MODEL).reshape(1, D_MODEL),
    }
    g_big = {"w_in": red[0], "w_branch_hg": red[1], "w_branch_attn": red[2], "w_out": red[3],
             "w_ffn_in": red[4], "w_ffn_out": red[5]}

    names = ["meta_tokens", "ln_emb_g", "ln_emb_b", "w_in", "hg_lower_bounds", "hg_norm_g", "attn_sinks",
             "w_branch_hg", "w_branch_attn", "w_out", "ln1_g", "ln1_b", "w_ffn_in", "w_ffn_out", "ln2_g", "ln2_b"]
    given = dict(
        meta_tokens=(meta_tokens, m_meta_tokens, v_meta_tokens), ln_emb_g=(ln_emb_g, m_ln_emb_g, v_ln_emb_g),
        ln_emb_b=(ln_emb_b, m_ln_emb_b, v_ln_emb_b), w_in=(w_in, m_w_in, v_w_in),
        hg_lower_bounds=(hg_lower_bounds, m_hg_lower_bounds, v_hg_lower_bounds),
        hg_norm_g=(hg_norm_g, m_hg_norm_g, v_hg_norm_g), attn_sinks=(attn_sinks, m_attn_sinks, v_attn_sinks),
        w_branch_hg=(w_branch_hg, m_w_branch_hg, v_w_branch_hg),
        w_branch_attn=(w_branch_attn, m_w_branch_attn, v_w_branch_attn), w_out=(w_out, m_w_out, v_w_out),
        ln1_g=(ln1_g, m_ln1_g, v_ln1_g), ln1_b=(ln1_b, m_ln1_b, v_ln1_b), w_ffn_in=(w_ffn_in, m_w_ffn_in, v_w_ffn_in),
        w_ffn_out=(w_ffn_out, m_w_ffn_out, v_w_ffn_out), ln2_g=(ln2_g, m_ln2_g, v_ln2_g), ln2_b=(ln2_b, m_ln2_b, v_ln2_b))
    out_g, out_d, out_m, out_v = [], [], [], []
    for nm in names:
        w, m, v = given[nm]
        shape = w.shape
        g = g_big[nm] if nm in g_big else g_small[nm]
        two_d = (lambda a: a.reshape(8, BLOCK)) if w.ndim == 1 else (lambda a: a.reshape(a.shape[-2], a.shape[-1]))
        d, mn, vn = adamw(two_d(w), two_d(g), two_d(m), two_d(v), "adamw_" + nm)
        out_g.append(g.reshape(shape))
        out_d.append(d.reshape(shape))
        out_m.append(mn.reshape(shape))
        out_v.append(vn.reshape(shape))

    loss = lax.psum(loss_part[0, 0], ("x", "y", "c"))
    grad_x = dxp[BLOCK:].reshape(x.shape)
    return (loss, grad_x, *out_g, *out_d, *out_m, *out_v)
```

```python
import functools

import jax
import jax.numpy as jnp
from jax import lax
from jax.experimental import pallas as pl
from jax.experimental.pallas import tpu as pltpu

f32 = jnp.float32
bf16 = jnp.bfloat16

D_MODEL = 1024
BLOCK = 128
N_META = 16
PAD = BLOCK - N_META
HG_HEADS = 4
HG_K = 128
SUB = 16
ATT_HEADS = 8
HEAD_DIM = 64
ATT_QW = ATT_HEADS * HEAD_DIM
D_FF = 2816
EPS = 1e-5
ALPHA = 2.0 ** 0.25
ROPE_THETA = 10000.0
N_A = 2816
N_G = 2048
IN_W = N_A + N_G
N_SHARD = 4
N_DEV = 8

ADAM_LR = 0.001
ADAM_B1 = 0.9
ADAM_B2 = 0.999
ADAM_EPS = 1e-08
ADAM_WD = 0.01
ADAM_STEP = 10

VMEM_LIMIT = 56 * 1024 * 1024
MESH = pl.DeviceIdType.MESH


def _cparams(sem, vmem=VMEM_LIMIT):
    return pltpu.CompilerParams(dimension_semantics=sem, vmem_limit_bytes=vmem)


def _const_spec(shape):
    zeros = (0,) * len(shape)
    return pl.BlockSpec(shape, lambda *_: zeros, pipeline_mode=pl.Buffered(1))


def _dot(a, b, ca, cb):
    return lax.dot_general(a.astype(bf16), b.astype(bf16), (((ca,), (cb,)), ((), ())),
                           preferred_element_type=f32)


@jax.custom_vjp
def mm(a, b):
    return _dot(a, b, 1, 0)


mm.defvjp(lambda a, b: (_dot(a, b, 1, 0), (a, b)),
          lambda r, g: (_dot(g, r[1], 1, 1), _dot(r[0], g, 0, 0)))


@jax.custom_vjp
def mm_nt(a, b):
    return _dot(a, b, 1, 1)


mm_nt.defvjp(lambda a, b: (_dot(a, b, 1, 1), (a, b)),
             lambda r, g: (_dot(g, r[1], 1, 0), _dot(g, r[0], 0, 0)))


@jax.custom_vjp
def mm_tn(a, b):
    return _dot(a, b, 0, 0)


mm_tn.defvjp(lambda a, b: (_dot(a, b, 0, 0), (a, b)),
             lambda r, g: (_dot(r[1], g, 1, 1), _dot(r[0], g, 1, 0)))


@functools.partial(jax.custom_vjp, nondiff_argnums=(1,))
def roll_lanes(x, shift):
    return pltpu.roll(x, shift, 1)


roll_lanes.defvjp(lambda x, shift: (pltpu.roll(x, shift, 1), None),
                  lambda shift, _, g: (pltpu.roll(g, (128 - shift) % 128, 1),))


def _sigmoid(x):
    return 1.0 / (1.0 + jnp.exp(-x))


def _ln_stats(x):
    mu = jnp.mean(x, axis=-1, keepdims=True)
    xc = x - mu
    var = jnp.mean(xc * xc, axis=-1, keepdims=True)
    rs = lax.rsqrt(var + EPS)
    return xc * rs, rs


def _ln_bwd(dy, xh, rs, g):
    dxh = dy * g
    m1 = jnp.mean(dxh, axis=-1, keepdims=True)
    m2 = jnp.mean(dxh * xh, axis=-1, keepdims=True)
    return rs * (dxh - m1 - xh * m2)


def _row_ids(i):
    return i * BLOCK + lax.broadcasted_iota(jnp.int32, (BLOCK, 1), 0)


def emb_inproj(x, metablk, g, b, w_in, nb):
    def body(x_ref, mb_ref, g_ref, b_ref, w_ref, h0_ref, h0b_ref, pa_ref, pg_ref):
        i = pl.program_id(0)
        xb = jnp.where(i == 0, mb_ref[...], x_ref[...])
        xh, _ = _ln_stats(xb)
        y = xh * g_ref[...] + b_ref[...]
        y = jnp.where(_row_ids(i) >= PAD, y, 0.0)
        h0_ref[...] = y
        yb = y.astype(bf16)
        h0b_ref[...] = yb
        pa_ref[...] = jnp.dot(yb, w_ref[:, :N_A], preferred_element_type=f32)
        pg_ref[...] = jnp.dot(yb, w_ref[:, N_A:], preferred_element_type=f32)

    p = nb * BLOCK
    row = lambda n: pl.BlockSpec((BLOCK, n), lambda i: (i, 0))
    return pl.pallas_call(
        body, name="emb_inproj", grid=(nb,),
        in_specs=[pl.BlockSpec((BLOCK, D_MODEL), lambda i: (jnp.maximum(i - 1, 0), 0)),
                  _const_spec((BLOCK, D_MODEL)), _const_spec((1, D_MODEL)), _const_spec((1, D_MODEL)),
                  _const_spec((D_MODEL, IN_W))],
        out_specs=[row(D_MODEL), row(D_MODEL), row(N_A), row(N_G)],
        out_shape=[jax.ShapeDtypeStruct((p, D_MODEL), f32), jax.ShapeDtypeStruct((p, D_MODEL), bf16),
                   jax.ShapeDtypeStruct((p, N_A), f32), jax.ShapeDtypeStruct((p, N_G), f32)],
        compiler_params=_cparams(("parallel",)),
    )(x, metablk, g, b, w_in)


def _hgrn_chunk(valid, st, hq, hf, hi, hg, lbraw, ng):
    lb = _sigmoid(lbraw[0:1] - lbraw[1:2])
    q = hq * _sigmoid(hq)
    fg = lb + (1.0 - lb) * _sigmoid(hf)
    logf = jnp.where(valid, jnp.log(fg), 0.0)
    k = jnp.where(valid, 1.0 - fg, 0.0)
    v = hi
    r = lax.broadcasted_iota(jnp.int32, (BLOCK, BLOCK), 0)
    c = lax.broadcasted_iota(jnp.int32, (BLOCK, BLOCK), 1)
    tril = (c <= r).astype(f32)
    bcum = jnp.dot(tril, logf, precision=lax.Precision.HIGHEST, preferred_element_type=f32)
    blast = bcum[BLOCK - 1:BLOCK]
    rows = lax.broadcasted_iota(jnp.int32, (BLOCK, 1), 0)
    sub8 = lax.broadcasted_iota(jnp.int32, (BLOCK // 8, 8, HG_K), 1)
    b8 = bcum.reshape(BLOCK // 8, 8, HG_K)
    row_of_8 = lambda j: jnp.broadcast_to(b8[:, j:j + 1, :], b8.shape)
    a = jnp.where(r == c, jnp.sum(q * k, axis=-1, keepdims=True), 0.0)
    seg = BLOCK
    while seg >= 2:
        half = seg // 2
        if seg >= 8:
            bs = bcum.reshape(BLOCK // seg, seg, HG_K)
            ref = jnp.broadcast_to(bs[:, half - 1:half, :], bs.shape)
        elif seg == 4:
            ref = jnp.where(sub8 < 4, row_of_8(1), row_of_8(5))
        else:
            ref = jnp.where(sub8 < 2, row_of_8(0), jnp.where(sub8 < 4, row_of_8(2),
                                                             jnp.where(sub8 < 6, row_of_8(4), row_of_8(6))))
        ref = ref.reshape(BLOCK, HG_K)
        upper = (rows % seg) >= half
        q_up = q * jnp.exp(jnp.where(upper, bcum - ref, -jnp.inf))
        k_lo = k * jnp.exp(jnp.where(upper, -jnp.inf, ref - bcum))
        a = a + jnp.where((r // seg) == (c // seg), mm_nt(q_up, k_lo), 0.0)
        seg = half
    o = mm_nt(q * jnp.exp(bcum), st) + mm(a, v)
    st_new = st * jnp.exp(blast) + mm_tn(v, k * jnp.exp(blast - bcum))
    on = o * lax.rsqrt(jnp.mean(o * o, axis=-1, keepdims=True) + EPS) * ng
    return st_new, on * (hg * _sigmoid(hg))


def _hgrn_in_specs(rowmap):
    wide = lambda col: pl.BlockSpec((BLOCK, HG_HEADS * HG_K), lambda i: (rowmap(i), col))
    return [wide(0), wide(1), wide(2), wide(3), _const_spec((2, HG_HEADS * HG_K)), _const_spec((1, HG_K))]


def _head(ref, h):
    return ref[:, h * HG_K:(h + 1) * HG_K]


def hgrn_fwd(pa, lbraw, ng, nb, shards):
    n = len(shards)

    def body(hq_ref, hf_ref, hi_ref, hg_ref, lb_ref, ng_ref, *rest):
        srcs, (og_ref, sp_ref), dsts = rest[:n], rest[n:n + 2], rest[n + 2:2 * n + 2]
        st_ref = rest[2 * n + 2]
        start, wait = _shard_push(srcs, dsts, *rest[2 * n + 3:])
        i = pl.program_id(0)

        @pl.when(i == 0)
        def _():
            st_ref[...] = jnp.zeros_like(st_ref)
            start()

        @pl.when(i == nb - 1)
        def _():
            wait()

        valid = _row_ids(i) >= PAD
        for h in range(HG_HEADS):
            st = st_ref[h]
            sp_ref[0, h] = st
            st_new, out = _hgrn_chunk(valid, st, _head(hq_ref, h), _head(hf_ref, h), _head(hi_ref, h),
                                      _head(hg_ref, h), _head(lb_ref, h), ng_ref[...])
            st_ref[h] = st_new
            og_ref[:, h * HG_K:(h + 1) * HG_K] = out.astype(bf16)

    p = nb * BLOCK
    push_in, push_out, push_shape, push_scratch = _push_specs(shards)
    return pl.pallas_call(
        body, name="hgrn_fwd", grid=(nb,),
        in_specs=_hgrn_in_specs(lambda i: i) + push_in,
        out_specs=[pl.BlockSpec((BLOCK, HG_HEADS * HG_K), lambda i: (i, 0)),
                   pl.BlockSpec((1, HG_HEADS, HG_K, HG_K), lambda i: (i, 0, 0, 0))] + push_out,
        out_shape=[jax.ShapeDtypeStruct((p, HG_HEADS * HG_K), bf16),
                   jax.ShapeDtypeStruct((nb, HG_HEADS, HG_K, HG_K), f32)] + push_shape,
        scratch_shapes=[pltpu.VMEM((HG_HEADS, HG_K, HG_K), f32)] + push_scratch,
        compiler_params=_cparams(("arbitrary",)),
    )(pa, pa, pa, pa, lbraw, ng, *shards)


def hgrn_bwd(pa, lbraw, ng, sprev, dog, nb):
    def body(hq_ref, hf_ref, hi_ref, hg_ref, lb_ref, ng_ref, sp_ref, do_ref,
             dq_ref, df_ref, di_ref, dg_ref, dlb_ref, dng_ref, dst_ref):
        i = pl.program_id(0)

        @pl.when(i == 0)
        def _():
            dst_ref[...] = jnp.zeros_like(dst_ref)
            dlb_ref[...] = jnp.zeros_like(dlb_ref)
            dng_ref[...] = jnp.zeros_like(dng_ref)

        valid = _row_ids(nb - 1 - i) >= PAD
        dng_sum = jnp.zeros((1, HG_K), f32)
        for h in range(HG_HEADS):
            cols = slice(h * HG_K, (h + 1) * HG_K)
            _, vjp = jax.vjp(functools.partial(_hgrn_chunk, valid), sp_ref[0, h], _head(hq_ref, h), _head(hf_ref, h),
                             _head(hi_ref, h), _head(hg_ref, h), _head(lb_ref, h), ng_ref[...])
            dst, dq, df, di, dg, dlb, dng = vjp((dst_ref[h], _head(do_ref, h)))
            dst_ref[h] = dst
            dq_ref[:, cols] = dq.astype(bf16)
            df_ref[:, cols] = df.astype(bf16)
            di_ref[:, cols] = di.astype(bf16)
            dg_ref[:, cols] = dg.astype(bf16)
            dlb_ref[:, cols] += dlb
            dng_sum = dng_sum + dng
        dng_ref[...] += dng_sum

    p = nb * BLOCK
    rev = lambda i: nb - 1 - i
    hw = HG_HEADS * HG_K
    blk = pl.BlockSpec((BLOCK, hw), lambda i: (rev(i), 0))
    wide = jax.ShapeDtypeStruct((p, hw), bf16)
    return pl.pallas_call(
        body, name="hgrn_bwd", grid=(nb,),
        in_specs=_hgrn_in_specs(rev) + [pl.BlockSpec((1, HG_HEADS, HG_K, HG_K), lambda i: (rev(i), 0, 0, 0)), blk],
        out_specs=[blk, blk, blk, blk, pl.BlockSpec((2, hw), lambda i: (0, 0)), pl.BlockSpec((1, HG_K), lambda i: (0, 0))],
        out_shape=[wide, wide, wide, wide, jax.ShapeDtypeStruct((2, hw), f32), jax.ShapeDtypeStruct((1, HG_K), f32)],
        scratch_shapes=[pltpu.VMEM((HG_HEADS, HG_K, HG_K), f32)],
        compiler_params=_cparams(("arbitrary",)),
    )(pa, pa, pa, pa, lbraw, ng, sprev, dog)


def _rope(x, cos, sin):
    lane = lax.broadcasted_iota(jnp.int32, x.shape, 1)
    rot = jnp.where(lane % HEAD_DIM < HEAD_DIM // 2, -roll_lanes(x, BLOCK - HEAD_DIM // 2),
                    roll_lanes(x, HEAD_DIM // 2))
    return x * cos + rot * sin


def _both_halves(x, g):
    lo = lax.broadcasted_iota(jnp.int32, x.shape, 1) < HEAD_DIM
    sw = roll_lanes(x, HEAD_DIM)
    return jnp.where(lo, x, sw) if g == 0 else jnp.where(lo, sw, x)


def _attn_block(band_ok, meta_ok, tabs, q, kp, kc, vp, vc, km, vm, *sinks):
    cq, sq, cp, sp, cm, sm = tabs
    neg = jnp.finfo(f32).min
    scale = HEAD_DIM ** -0.5
    group = ATT_HEADS // 2
    kpr, kcr, kmr = _rope(kp, cp, sp), _rope(kc, cq, sq), _rope(km, cm, sm)
    lo = lax.broadcasted_iota(jnp.int32, (BLOCK, BLOCK), 1) < HEAD_DIM
    t = lax.broadcasted_iota(jnp.int32, (group * BLOCK, BLOCK), 0) % BLOCK
    own = lax.broadcasted_iota(jnp.int32, (group * BLOCK, BLOCK), 1) <= t
    qr = [_rope(q[:, m * BLOCK:(m + 1) * BLOCK], cq, sq) for m in range(ATT_HEADS // 2)]
    slabs = []
    for g in range(2):
        kp_g, kc_g, vp_g, vc_g, km_g, vm_g = [_both_halves(a, g) for a in (kpr, kcr, vp, vc, kmr, vm)]
        qs = jnp.concatenate([jnp.where(lo if h % 2 == 0 else ~lo, qr[2 * g + h // 2], 0.0) for h in range(group)],
                             axis=0)
        sink = jnp.concatenate([jnp.broadcast_to(sinks[group * g + h], (BLOCK, 1)) for h in range(group)], axis=0)
        sb = jnp.where(band_ok, jnp.where(own, mm_nt(qs, kc_g), mm_nt(qs, kp_g)) * scale, neg)
        sme = jnp.where(meta_ok, mm_nt(qs, km_g) * scale, neg)
        mx = lax.stop_gradient(jnp.maximum(jnp.maximum(jnp.max(sb, axis=-1, keepdims=True),
                                                       jnp.max(sme, axis=-1, keepdims=True)), sink))
        eb, em = jnp.exp(sb - mx), jnp.exp(sme - mx)
        inv = 1.0 / (jnp.sum(eb, axis=-1, keepdims=True) + jnp.sum(em, axis=-1, keepdims=True) + jnp.exp(sink - mx))
        pb = eb * inv
        o = mm(jnp.where(own, pb, 0.0), vc_g) + mm(jnp.where(own, 0.0, pb), vp_g) + mm(em * inv, vm_g)
        for m in range(2):
            slabs.append(jnp.where(lo, o[2 * m * BLOCK:(2 * m + 1) * BLOCK], o[(2 * m + 1) * BLOCK:(2 * m + 2) * BLOCK]))
    return jnp.concatenate(slabs, axis=1)


def _attn_masks(i):
    group = ATT_HEADS // 2
    t = lax.broadcasted_iota(jnp.int32, (group * BLOCK, BLOCK), 0) % BLOCK
    s = lax.broadcasted_iota(jnp.int32, (group * BLOCK, BLOCK), 1)
    kpos = jnp.where(s <= t, i * BLOCK - PAD + s, jnp.where(i > 0, (i - 1) * BLOCK - PAD + s, -1))
    band_ok = kpos >= N_META
    qpos = i * BLOCK - PAD + lax.broadcasted_iota(jnp.int32, (group * BLOCK, 1), 0) % BLOCK
    meta_ok = lax.broadcasted_iota(jnp.int32, (1, N_META), 1) <= qpos
    return band_ok, meta_ok


def _attn_in_specs():
    prev = lambda i: jnp.maximum(i - 1, 0)
    kcol, vcol = N_A // BLOCK - 2, N_A // BLOCK - 1
    blk = lambda rowmap, col: pl.BlockSpec((BLOCK, BLOCK), lambda i: (rowmap(i), col))
    cur, first = (lambda i: i), (lambda i: 0)
    return [pl.BlockSpec((BLOCK, ATT_QW), lambda i: (i, 4)),
            blk(prev, kcol), blk(cur, kcol), blk(prev, vcol), blk(cur, vcol), blk(first, kcol), blk(first, vcol),
            blk(cur, 0), blk(cur, 0), blk(prev, 0), blk(prev, 0), blk(first, 0), blk(first, 0),
            _const_spec((ATT_HEADS, BLOCK))]


def _attn_operands(q_ref, kp_ref, kc_ref, vp_ref, vc_ref, km_ref, vm_ref, cq, sq, cp, sp, cm, sm, sk_ref):
    tabs = (cq[...], sq[...], cp[...], sp[...], cm[PAD:, :], sm[PAD:, :])
    args = (q_ref[...], kp_ref[...], kc_ref[...], vp_ref[...], vc_ref[...], km_ref[PAD:, :], vm_ref[PAD:, :])
    sinks = tuple(sk_ref[j:j + 1, 0:1] for j in range(ATT_HEADS))
    return tabs, args + sinks


def attn_fwd(pa, cos, sin, sinks8, nb, shards):
    n = len(shards)
    n_in = 14

    def body(*refs):
        srcs, o_ref, dsts = refs[n_in:n_in + n], refs[n_in + n], refs[n_in + n + 1:n_in + 2 * n + 1]
        start, wait = _shard_push(srcs, dsts, *refs[n_in + 2 * n + 1:])
        i = pl.program_id(0)
        pl.when(i == 0)(start)
        band_ok, meta_ok = _attn_masks(i)
        tabs, args = _attn_operands(*refs[:n_in])
        o_ref[...] = _attn_block(band_ok, meta_ok, tabs, *args).astype(bf16)
        pl.when(i == nb - 1)(wait)

    push_in, push_out, push_shape, push_scratch = _push_specs(shards)
    return pl.pallas_call(
        body, name="attn_fwd", grid=(nb,), in_specs=_attn_in_specs() + push_in,
        out_specs=[pl.BlockSpec((BLOCK, ATT_QW), lambda i: (i, 0))] + push_out,
        out_shape=[jax.ShapeDtypeStruct((nb * BLOCK, ATT_QW), bf16)] + push_shape,
        scratch_shapes=push_scratch,
        compiler_params=_cparams(("arbitrary",)),
    )(pa, pa, pa, pa, pa, pa, pa, cos, sin, cos, sin, cos, sin, sinks8, *shards)


def attn_bwd(pa, cos, sin, sinks8, do, nb):
    def body(*refs):
        do_ref = refs[14]
        dq_ref, dkc_ref, dkp_ref, dvc_ref, dvp_ref, dkm_ref, dvm_ref, dsk_ref = refs[15:]
        i = pl.program_id(0)

        @pl.when(i == 0)
        def _():
            dkm_ref[...] = jnp.zeros((N_META, BLOCK), f32)
            dvm_ref[...] = jnp.zeros((N_META, BLOCK), f32)
            dsk_ref[...] = jnp.zeros((ATT_HEADS, BLOCK), f32)

        band_ok, meta_ok = _attn_masks(i)
        tabs, args = _attn_operands(*refs[:14])
        _, vjp = jax.vjp(functools.partial(_attn_block, band_ok, meta_ok, tabs), *args)
        grads = vjp(do_ref[...])
        dq_ref[...] = grads[0].astype(bf16)
        dkp_ref[...] = grads[1]
        dkc_ref[...] = grads[2]
        dvp_ref[...] = grads[3]
        dvc_ref[...] = grads[4]
        dkm_ref[...] += grads[5]
        dvm_ref[...] += grads[6]
        for j in range(ATT_HEADS):
            dsk_ref[j:j + 1, :] += jnp.broadcast_to(grads[7 + j], (1, BLOCK))

    p = nb * BLOCK
    row = pl.BlockSpec((BLOCK, BLOCK), lambda i: (i, 0))
    const = lambda r: pl.BlockSpec((r, BLOCK), lambda i: (0, 0))
    part = jax.ShapeDtypeStruct((p, BLOCK), f32)
    return pl.pallas_call(
        body, name="attn_bwd", grid=(nb,),
        in_specs=_attn_in_specs() + [pl.BlockSpec((BLOCK, ATT_QW), lambda i: (i, 0))],
        out_specs=[pl.BlockSpec((BLOCK, ATT_QW), lambda i: (i, 0)), row, row, row, row,
                   const(N_META), const(N_META), const(ATT_HEADS)],
        out_shape=[jax.ShapeDtypeStruct((p, ATT_QW), bf16), part, part, part, part,
                   jax.ShapeDtypeStruct((N_META, BLOCK), f32), jax.ShapeDtypeStruct((N_META, BLOCK), f32),
                   jax.ShapeDtypeStruct((ATT_HEADS, BLOCK), f32)],
        compiler_params=_cparams(("arbitrary",)),
    )(pa, pa, pa, pa, pa, pa, pa, cos, sin, cos, sin, cos, sin, sinks8, do)


def mid_rows(h0, pg, og, oatt, target, wbh, wba, wout, wfi, wfo, ln1g, ln1b, ln2g, ln2b, nb):
    def body(h0_ref, pg_ref, og_ref, oa_ref, t_ref, wbh_ref, wba_ref, wo_ref, wfi_ref, wfo_ref,
             g1_ref, b1_ref, g2_ref, b2_ref,
             dh0_ref, dpg_ref, dog_ref, doa_ref, dyh_ref, dya_ref, mix_ref, dr1_ref, h1b_ref, dau_ref, s_ref, dr2_ref,
             loss_ref, dg1_ref, db1_ref, dg2_ref, db2_ref):
        i = pl.program_id(0)

        @pl.when(i == 0)
        def _():
            loss_ref[...] = jnp.zeros_like(loss_ref)
            for r in (dg1_ref, db1_ref, dg2_ref, db2_ref):
                r[...] = jnp.zeros_like(r)

        g1, b1, g2, b2 = g1_ref[...], b1_ref[...], g2_ref[...], b2_ref[...]
        yh = jnp.dot(og_ref[...], wbh_ref[...], preferred_element_type=f32)
        ya = jnp.dot(oa_ref[...], wba_ref[...], preferred_element_type=f32)
        gh = _sigmoid(pg_ref[:, :D_MODEL])
        ga = _sigmoid(pg_ref[:, D_MODEL:])
        mixin = (gh * yh + ga * ya).astype(bf16)
        mix_ref[...] = mixin
        r1 = ALPHA * h0_ref[...] + jnp.dot(mixin, wo_ref[...], preferred_element_type=f32)
        xh1, rs1 = _ln_stats(r1)
        h1 = xh1 * g1 + b1
        h1b = h1.astype(bf16)
        h1b_ref[...] = h1b
        au = jnp.dot(h1b, wfi_ref[...], preferred_element_type=f32)
        a, u = au[:, :D_FF], au[:, D_FF:]
        sg = _sigmoid(a)
        sa = a * sg
        s = (sa * u).astype(bf16)
        s_ref[...] = s
        r2 = ALPHA * h1 + jnp.dot(s, wfo_ref[...], preferred_element_type=f32)
        xh2, rs2 = _ln_stats(r2)
        diff = jnp.where(i > 0, xh2 * g2 + b2 - t_ref[...], 0.0)
        loss_ref[...] += jnp.sum(diff * diff) * (0.5 / D_MODEL)
        dy = diff * (1.0 / D_MODEL)
        dg2_ref[...] += jnp.sum(dy * xh2, axis=0, keepdims=True)
        db2_ref[...] += jnp.sum(dy, axis=0, keepdims=True)
        dr2 = _ln_bwd(dy, xh2, rs2, g2)
        dr2b = dr2.astype(bf16)
        dr2_ref[...] = dr2b
        ds = _dot(dr2b, wfo_ref[...], 1, 1)
        da = (ds * u) * (sg * (1.0 + a * (1.0 - sg)))
        du = ds * sa
        dau = jnp.concatenate([da, du], axis=1).astype(bf16)
        dau_ref[...] = dau
        dh1 = ALPHA * dr2 + _dot(dau, wfi_ref[...], 1, 1)
        dg1_ref[...] += jnp.sum(dh1 * xh1, axis=0, keepdims=True)
        db1_ref[...] += jnp.sum(dh1, axis=0, keepdims=True)
        dr1 = _ln_bwd(dh1, xh1, rs1, g1)
        dr1b = dr1.astype(bf16)
        dr1_ref[...] = dr1b
        dh0_ref[...] = ALPHA * dr1
        dmix = _dot(dr1b, wo_ref[...], 1, 1)
        dyh = (dmix * gh).astype(bf16)
        dya = (dmix * ga).astype(bf16)
        dyh_ref[...] = dyh
        dya_ref[...] = dya
        dpg_ref[:, :D_MODEL] = (dmix * yh * gh * (1.0 - gh)).astype(bf16)
        dpg_ref[:, D_MODEL:] = (dmix * ya * ga * (1.0 - ga)).astype(bf16)
        dog_ref[...] = _dot(dyh, wbh_ref[...], 1, 1)
        doa_ref[...] = _dot(dya, wba_ref[...], 1, 1)

    p = nb * BLOCK
    row = lambda n: pl.BlockSpec((BLOCK, n), lambda i: (i, 0))
    vec = lambda: pl.BlockSpec((1, D_MODEL), lambda i: (0, 0))
    sds = lambda n, dt: jax.ShapeDtypeStruct((p, n), dt)
    hw = HG_HEADS * HG_K
    return pl.pallas_call(
        body, name="mid_rows", grid=(nb,),
        in_specs=[row(D_MODEL), row(N_G), row(hw), row(ATT_QW),
                  pl.BlockSpec((BLOCK, D_MODEL), lambda i: (jnp.maximum(i - 1, 0), 0)),
                  _const_spec((hw, D_MODEL)), _const_spec((ATT_QW, D_MODEL)), _const_spec((D_MODEL, D_MODEL)),
                  _const_spec((D_MODEL, 2 * D_FF)), _const_spec((D_FF, D_MODEL)),
                  _const_spec((1, D_MODEL)), _const_spec((1, D_MODEL)), _const_spec((1, D_MODEL)),
                  _const_spec((1, D_MODEL))],
        out_specs=[row(D_MODEL), row(N_G), row(hw), row(ATT_QW), row(D_MODEL), row(D_MODEL), row(D_MODEL),
                   row(D_MODEL), row(D_MODEL), row(2 * D_FF), row(D_FF), row(D_MODEL),
                   pl.BlockSpec((1, 1), lambda i: (0, 0)), vec(), vec(), vec(), vec()],
        out_shape=[sds(D_MODEL, f32), sds(N_G, bf16), sds(hw, f32), sds(ATT_QW, f32), sds(D_MODEL, bf16),
                   sds(D_MODEL, bf16), sds(D_MODEL, bf16), sds(D_MODEL, bf16), sds(D_MODEL, bf16),
                   sds(2 * D_FF, bf16), sds(D_FF, bf16), sds(D_MODEL, bf16),
                   jax.ShapeDtypeStruct((1, 1), f32)] + [jax.ShapeDtypeStruct((1, D_MODEL), f32)] * 4,
        compiler_params=_cparams(("arbitrary",)),
    )(h0, pg, og, oatt, target, wbh, wba, wout, wfi, wfo, ln1g, ln1b, ln2g, ln2b)


def inproj_bwd(dh0p, dhq, dhf, dhi, dhg, daq, dkc, dkp, dvc, dvp, dkm, dvm, dpg, w_in, x, metablk, g, b, nb):
    def body(dh0_ref, dq_ref, df_ref, di_ref, dg_ref, daq_ref, dkc_ref, dkp_ref, dvc_ref, dvp_ref, dkm_ref, dvm_ref,
             dpg_ref, w_ref, x_ref, mb_ref, g_ref, b_ref, dproj_ref, dx_ref, dlg_ref, dlb_ref):
        i = pl.program_id(0)

        @pl.when(i == 0)
        def _():
            dlg_ref[...] = jnp.zeros_like(dlg_ref)
            dlb_ref[...] = jnp.zeros_like(dlb_ref)

        zero_pad = jnp.zeros((PAD, BLOCK), f32)
        has_next = i + 1 < nb
        first = i == 0

        def keys(cur_ref, next_ref, meta_ref):
            t = cur_ref[...] + jnp.where(has_next, next_ref[...], 0.0)
            return t + jnp.where(first, jnp.concatenate([zero_pad, meta_ref[...]], axis=0), 0.0)

        dproj = jnp.concatenate(
            [dq_ref[...], df_ref[...], di_ref[...], dg_ref[...], daq_ref[...],
             keys(dkc_ref, dkp_ref, dkm_ref).astype(bf16), keys(dvc_ref, dvp_ref, dvm_ref).astype(bf16),
             dpg_ref[...]], axis=1)
        dproj_ref[...] = dproj
        valid = _row_ids(i) >= PAD
        dh0 = jnp.where(valid, dh0_ref[...] + _dot(dproj, w_ref[...], 1, 1), 0.0)
        xb = jnp.where(first, mb_ref[...], x_ref[...])
        xh, rs = _ln_stats(xb)
        dlg_ref[...] += jnp.sum(dh0 * xh, axis=0, keepdims=True)
        dlb_ref[...] += jnp.sum(dh0, axis=0, keepdims=True)
        dx_ref[...] = jnp.where(valid, _ln_bwd(dh0, xh, rs, g_ref[...]), 0.0)

    p = nb * BLOCK
    row = lambda n: pl.BlockSpec((BLOCK, n), lambda i: (i, 0))
    nxt = pl.BlockSpec((BLOCK, BLOCK), lambda i: (jnp.minimum(i + 1, nb - 1), 0))
    hw = HG_HEADS * HG_K
    vec = lambda: pl.BlockSpec((1, D_MODEL), lambda i: (0, 0))
    return pl.pallas_call(
        body, name="inproj_bwd", grid=(nb,),
        in_specs=[row(D_MODEL), row(hw), row(hw), row(hw), row(hw), row(ATT_QW),
                  row(BLOCK), nxt, row(BLOCK), nxt, _const_spec((N_META, BLOCK)), _const_spec((N_META, BLOCK)),
                  row(N_G), _const_spec((D_MODEL, IN_W)),
                  pl.BlockSpec((BLOCK, D_MODEL), lambda i: (jnp.maximum(i - 1, 0), 0)),
                  _const_spec((BLOCK, D_MODEL)), _const_spec((1, D_MODEL)), _const_spec((1, D_MODEL))],
        out_specs=[row(IN_W), row(D_MODEL), vec(), vec()],
        out_shape=[jax.ShapeDtypeStruct((p, IN_W), bf16), jax.ShapeDtypeStruct((p, D_MODEL), f32),
                   jax.ShapeDtypeStruct((1, D_MODEL), f32), jax.ShapeDtypeStruct((1, D_MODEL), f32)],
        compiler_params=_cparams(("arbitrary",)),
    )(dh0p, dhq, dhf, dhi, dhg, daq, dkc, dkp, dvc, dvp, dkm, dvm, dpg, w_in, x, metablk, g, b)


def wgrad(a, b, name, tk, tn, tp, by_cols):
    p, k = a.shape
    n = b.shape[1]
    nsteps = p // tp

    def body(a_ref, b_ref, o_ref):
        @pl.when(pl.program_id(2) == 0)
        def _():
            o_ref[...] = jnp.zeros_like(o_ref)

        o_ref[0] += _dot(a_ref[...], b_ref[...], 0, 0)

    if by_cols:
        shard_n = n // N_SHARD
        per = shard_n // tn
        out_shape = (N_SHARD, k, shard_n)
        omap = lambda ik, jn, ip: (jn // per, ik, jn % per)
    else:
        out_shape = (1, k, n)
        omap = lambda ik, jn, ip: (0, ik, jn)
    return pl.pallas_call(
        body, name=name, grid=(k // tk, n // tn, nsteps),
        in_specs=[pl.BlockSpec((tp, tk), lambda ik, jn, ip: (ip, ik)),
                  pl.BlockSpec((tp, tn), lambda ik, jn, ip: (ip, jn))],
        out_specs=pl.BlockSpec((1, tk, tn), omap),
        out_shape=jax.ShapeDtypeStruct(out_shape, f32),
        compiler_params=_cparams(("parallel", "parallel", "arbitrary")),
    )(a, b)


def adamw(w, g, m, v, name):
    r, c = w.shape
    tr = r
    for cand in (256, 176, 128):
        if r > cand and r % cand == 0:
            tr = cand
            break

    def body(w_ref, g_ref, m_ref, v_ref, d_ref, mo_ref, vo_ref):
        gg = g_ref[...]
        mn = ADAM_B1 * m_ref[...] + (1.0 - ADAM_B1) * gg
        vn = ADAM_B2 * v_ref[...] + (1.0 - ADAM_B2) * (gg * gg)
        m_hat = mn / (1.0 - ADAM_B1 ** ADAM_STEP)
        v_hat = vn / (1.0 - ADAM_B2 ** ADAM_STEP)
        d_ref[...] = -ADAM_LR * (m_hat / (jnp.sqrt(v_hat) + ADAM_EPS) + ADAM_WD * w_ref[...])
        mo_ref[...] = mn
        vo_ref[...] = vn

    spec = pl.BlockSpec((tr, c), lambda i: (i, 0))
    sds = jax.ShapeDtypeStruct((r, c), f32)
    return pl.pallas_call(
        body, name=name, grid=(r // tr,), in_specs=[spec] * 4, out_specs=[spec] * 3, out_shape=[sds] * 3,
        compiler_params=_cparams(("parallel",)),
    )(w, g, m, v)


def _me():
    return lax.axis_index("x"), lax.axis_index("y"), lax.axis_index("c")


def _chip_peer(x, y, c, k):
    return (x ^ (k >> 1), y ^ (k & 1), c)


ANY = pl.BlockSpec(memory_space=pl.ANY)


def gather_weights(now, later):
    n, n_later = len(now), len(later)
    out_dtypes = [bf16 if s.size > 16 * 256 else f32 for s in now]

    def body(*refs):
        ins, later_ins = refs[:n], refs[n:n + n_later]
        outs, later_outs = refs[n + n_later:2 * n + n_later], refs[2 * n + n_later:2 * (n + n_later)]
        stage = refs[2 * (n + n_later):3 * n + 2 * n_later]
        send_sems, recv_sems, local_sems = refs[3 * n + 2 * n_later:]
        x, y, c = _me()
        j = 2 * x + y
        for w in range(n):
            stage[w][...] = ins[w][...].astype(out_dtypes[w])
        sends, locs = [], []
        for w in range(n):
            loc = pltpu.make_async_copy(stage[w], outs[w].at[j], local_sems.at[w])
            loc.start()
            locs.append(loc)
            for k in (1, 2, 3):
                cp = pltpu.make_async_remote_copy(
                    src_ref=stage[w], dst_ref=outs[w].at[j], send_sem=send_sems.at[w, k - 1],
                    recv_sem=recv_sems.at[w, k - 1], device_id=_chip_peer(x, y, c, k), device_id_type=MESH)
                cp.start()
                sends.append(cp)
        for w in range(n_later):
            later_outs[w][...] = later_ins[w][...].astype(bf16)
        for w in range(n):
            for k in (1, 2, 3):
                pltpu.make_async_remote_copy(
                    src_ref=stage[w], dst_ref=outs[w].at[j ^ k], send_sem=send_sems.at[w, k - 1],
                    recv_sem=recv_sems.at[w, k - 1], device_id=_chip_peer(x, y, c, k), device_id_type=MESH).wait_recv()
        for cp in sends:
            cp.wait_send()
        for loc in locs:
            loc.wait()

    vmem = pl.BlockSpec(memory_space=pltpu.VMEM)
    return pl.pallas_call(
        body, name="gather_weights",
        in_specs=[vmem] * (n + n_later), out_specs=[ANY] * n + [vmem] * n_later,
        out_shape=[jax.ShapeDtypeStruct((N_SHARD,) + s.shape, dt) for s, dt in zip(now, out_dtypes)]
        + [jax.ShapeDtypeStruct(s.shape, bf16) for s in later],
        scratch_shapes=[pltpu.VMEM(s.shape, dt) for s, dt in zip(now, out_dtypes)]
        + [pltpu.SemaphoreType.DMA((n, 3)), pltpu.SemaphoreType.DMA((n, 3)), pltpu.SemaphoreType.DMA((n,))],
        compiler_params=pltpu.CompilerParams(vmem_limit_bytes=VMEM_LIMIT),
    )(*now, *later)


def _shard_push(srcs, dsts, send_sems, recv_sems, local_sems):
    def remote(w, k, slot):
        x, y, c = _me()
        return pltpu.make_async_remote_copy(
            src_ref=srcs[w], dst_ref=dsts[w].at[slot], send_sem=send_sems.at[w, k - 1],
            recv_sem=recv_sems.at[w, k - 1], device_id=_chip_peer(x, y, c, k), device_id_type=MESH)

    def local(w):
        x, y, _ = _me()
        return pltpu.make_async_copy(srcs[w], dsts[w].at[2 * x + y], local_sems.at[w])

    def start():
        x, y, _ = _me()
        for w in range(len(srcs)):
            local(w).start()
            for k in (1, 2, 3):
                remote(w, k, 2 * x + y).start()

    def wait():
        x, y, _ = _me()
        for w in range(len(srcs)):
            for k in (1, 2, 3):
                remote(w, k, (2 * x + y) ^ k).wait_recv()
        for w in range(len(srcs)):
            for k in (1, 2, 3):
                remote(w, k, 2 * x + y).wait_send()
            local(w).wait()

    return start, wait


def _push_specs(shards):
    n = len(shards)
    return ([ANY] * n, [ANY] * n, [jax.ShapeDtypeStruct((N_SHARD,) + s.shape, s.dtype) for s in shards],
            [pltpu.SemaphoreType.DMA((n, 3)), pltpu.SemaphoreType.DMA((n, 3)), pltpu.SemaphoreType.DMA((n,))])


def pair_exchange_halves(grads, small):
    n = len(grads)

    def body(*refs):
        ins, small_ref = refs[:n], refs[n]
        outs, gath = refs[n + 1:2 * n + 1], refs[2 * n + 1]
        send_sems, recv_sems, s_send, s_recv, local_sem = refs[2 * n + 2:]
        x, y, c = _me()
        me = 4 * x + 2 * y + c
        sends = []
        for w in range(n):
            half = ins[w].shape[1] // 2
            cp = pltpu.make_async_remote_copy(
                src_ref=ins[w].at[:, pl.ds((1 - c) * half, half), :], dst_ref=outs[w],
                send_sem=send_sems.at[w], recv_sem=recv_sems.at[w], device_id=(x, y, 1 - c), device_id_type=MESH)
            cp.start()
            sends.append(cp)
        loc = pltpu.make_async_copy(small_ref, gath.at[me], local_sem)
        loc.start()
        for k in range(1, N_DEV):
            cp = pltpu.make_async_remote_copy(
                src_ref=small_ref, dst_ref=gath.at[me], send_sem=s_send.at[k - 1], recv_sem=s_recv.at[k - 1],
                device_id=(x ^ (k >> 2), y ^ ((k >> 1) & 1), c ^ (k & 1)), device_id_type=MESH)
            cp.start()
            sends.append(cp)
        for w in range(n):
            half = ins[w].shape[1] // 2
            pltpu.make_async_remote_copy(
                src_ref=ins[w].at[:, pl.ds(0, half), :], dst_ref=outs[w], send_sem=send_sems.at[w],
                recv_sem=recv_sems.at[w], device_id=(x, y, 1 - c), device_id_type=MESH).wait_recv()
        for k in range(1, N_DEV):
            pltpu.make_async_remote_copy(
                src_ref=small_ref, dst_ref=gath.at[me ^ k], send_sem=s_send.at[k - 1], recv_sem=s_recv.at[k - 1],
                device_id=(x ^ (k >> 2), y ^ ((k >> 1) & 1), c ^ (k & 1)), device_id_type=MESH).wait_recv()
        for cp in sends:
            cp.wait_send()
        loc.wait()

    return pl.pallas_call(
        body, name="pair_exchange_halves", in_specs=[ANY] * (n + 1), out_specs=[ANY] * (n + 1),
        out_shape=[jax.ShapeDtypeStruct((g.shape[0], g.shape[1] // 2, g.shape[2]), f32) for g in grads]
        + [jax.ShapeDtypeStruct((N_DEV,) + small.shape, f32)],
        scratch_shapes=[pltpu.SemaphoreType.DMA((n,)), pltpu.SemaphoreType.DMA((n,)),
                        pltpu.SemaphoreType.DMA((N_DEV - 1,)), pltpu.SemaphoreType.DMA((N_DEV - 1,)),
                        pltpu.SemaphoreType.DMA],
    )(*grads, small)


def chip_exchange(sums):
    n = len(sums)

    def body(*refs):
        ins, outs = refs[:n], refs[n:2 * n]
        send_sems, recv_sems = refs[2 * n:]
        x, y, c = _me()
        j = 2 * x + y
        sends = []
        for w in range(n):
            for k in (1, 2, 3):
                cp = pltpu.make_async_remote_copy(
                    src_ref=ins[w].at[j ^ k], dst_ref=outs[w].at[k - 1], send_sem=send_sems.at[w, k - 1],
                    recv_sem=recv_sems.at[w, k - 1], device_id=_chip_peer(x, y, c, k), device_id_type=MESH)
                cp.start()
                sends.append(cp)
        for w in range(n):
            for k in (1, 2, 3):
                pltpu.make_async_remote_copy(
                    src_ref=ins[w].at[0], dst_ref=outs[w].at[k - 1], send_sem=send_sems.at[w, k - 1],
                    recv_sem=recv_sems.at[w, k - 1], device_id=_chip_peer(x, y, c, k), device_id_type=MESH).wait_recv()
        for cp in sends:
            cp.wait_send()

    return pl.pallas_call(
        body, name="chip_exchange", in_specs=[ANY] * n, out_specs=[ANY] * n,
        out_shape=[jax.ShapeDtypeStruct((N_SHARD - 1,) + s.shape[1:], s.dtype) for s in sums],
        scratch_shapes=[pltpu.SemaphoreType.DMA((n, 3)), pltpu.SemaphoreType.DMA((n, 3))],
    )(*sums)


def pair_exchange_results(halves):
    n = len(halves)

    def body(*refs):
        ins, outs = refs[:n], refs[n:2 * n]
        send_sems, recv_sems = refs[2 * n:]
        x, y, c = _me()
        sends = []
        for w in range(n):
            cp = pltpu.make_async_remote_copy(
                src_ref=ins[w].at[c], dst_ref=outs[w].at[c], send_sem=send_sems.at[w], recv_sem=recv_sems.at[w],
                device_id=(x, y, 1 - c), device_id_type=MESH)
            cp.start()
            sends.append(cp)
        for w in range(n):
            pltpu.make_async_remote_copy(
                src_ref=ins[w].at[c], dst_ref=outs[w].at[1 - c], send_sem=send_sems.at[w],
                recv_sem=recv_sems.at[w], device_id=(x, y, 1 - c), device_id_type=MESH).wait_recv()
        for cp in sends:
            cp.wait_send()

    return pl.pallas_call(
        body, name="pair_exchange_results", in_specs=[ANY] * n, out_specs=[ANY] * n,
        out_shape=[jax.ShapeDtypeStruct(h.shape, f32) for h in halves],
        input_output_aliases={w: w for w in range(n)},
        scratch_shapes=[pltpu.SemaphoreType.DMA((n,)), pltpu.SemaphoreType.DMA((n,))],
    )(*halves)


def add_pair(grad, other, c_idx, name):
    _, r, c = grad.shape
    half = r // 2
    tr = half // 2 if (half // 2) % 8 == 0 else half
    per = half // tr

    def body(c_ref, g_ref, o_ref, out_ref):
        out_ref[...] = (g_ref[...] + o_ref[...]).astype(bf16)

    return pl.pallas_call(
        body, name=name,
        grid_spec=pltpu.PrefetchScalarGridSpec(
            num_scalar_prefetch=1, grid=(N_SHARD, per),
            in_specs=[pl.BlockSpec((1, tr, c), lambda j, t, cr: (j, cr[0] * per + t, 0)),
                      pl.BlockSpec((1, tr, c), lambda j, t, cr: (j, t, 0))],
            out_specs=pl.BlockSpec((1, tr, c), lambda j, t, cr: (j, t, 0))),
        out_shape=jax.ShapeDtypeStruct((N_SHARD, half, c), bf16),
        compiler_params=_cparams(("parallel", "parallel")),
    )(c_idx, grad, other)


def add_four(own, parts, jc_idx, name):
    _, half, c = parts.shape
    tr = half // 2 if (half // 2) % 8 == 0 else half

    def body(jc_ref, own_ref, p_ref, out_ref):
        acc = own_ref[0].astype(f32)
        for k in range(N_SHARD - 1):
            acc = acc + p_ref[k].astype(f32)
        out_ref[0] = acc

    return pl.pallas_call(
        body, name=name,
        grid_spec=pltpu.PrefetchScalarGridSpec(
            num_scalar_prefetch=1, grid=(half // tr,),
            in_specs=[pl.BlockSpec((1, tr, c), lambda t, jc: (jc[0], t, 0)),
                      pl.BlockSpec((N_SHARD - 1, tr, c), lambda t, jc: (0, t, 0))],
            out_specs=pl.BlockSpec((1, tr, c), lambda t, jc: (jc[1], t, 0))),
        out_shape=jax.ShapeDtypeStruct((2, half, c), f32),
        compiler_params=_cparams(("parallel",)),
    )(jc_idx, own, parts)


def sum_devices(gathered):
    def body(g_ref, out_ref):
        acc = g_ref[0]
        for d in range(1, N_DEV):
            acc = acc + g_ref[d]
        out_ref[...] = acc

    return pl.pallas_call(body, name="sum_devices", out_shape=jax.ShapeDtypeStruct(gathered.shape[1:], f32))(gathered)


def _rows128(a, rows):
    flat = a.reshape(-1, BLOCK) if a.size % BLOCK == 0 else jnp.pad(a.reshape(1, -1), ((0, 0), (0, BLOCK - a.size)))
    return jnp.pad(flat, ((0, rows - flat.shape[0]), (0, 0)))


def kernel(x, meta_tokens, ln_emb_g, ln_emb_b, w_in, hg_lower_bounds, hg_norm_g, attn_sinks, w_branch_hg, w_branch_attn, w_out, ln1_g, ln1_b, w_ffn_in, w_ffn_out, ln2_g, ln2_b, loss_target, m_meta_tokens, m_ln_emb_g, m_ln_emb_b, m_w_in, m_hg_lower_bounds, m_hg_norm_g, m_attn_sinks, m_w_branch_hg, m_w_branch_attn, m_w_out, m_ln1_g, m_ln1_b, m_w_ffn_in, m_w_ffn_out, m_ln2_g, m_ln2_b, v_meta_tokens, v_ln_emb_g, v_ln_emb_b, v_w_in, v_hg_lower_bounds, v_hg_norm_g, v_attn_sinks, v_w_branch_hg, v_w_branch_attn, v_w_out, v_ln1_g, v_ln1_b, v_w_ffn_in, v_w_ffn_out, v_ln2_g, v_ln2_b):
    seq = x.shape[1]
    nb = seq // BLOCK + 1
    xs = x[0]
    ts = loss_target[0]
    ix, iy, ic = _me()
    shard = 2 * ix + iy
    vec = lambda a: a.reshape(1, D_MODEL)

    g_in, g_meta, s_bh, s_ba, s_out, s_fi, s_fo = gather_weights(
        [w_in[0], meta_tokens], [w_branch_hg[0], w_branch_attn[0], w_out[0], w_ffn_in[0], w_ffn_out[0]])
    by_cols = lambda g: g.transpose(1, 0, 2).reshape(g.shape[1], N_SHARD * g.shape[2])
    wf_in = by_cols(g_in)
    metablk = jnp.pad(by_cols(g_meta), ((PAD, 0), (0, 0)))

    pos = jnp.arange(nb * BLOCK, dtype=jnp.int32) - PAD
    half = HEAD_DIM // 2
    inv = ROPE_THETA ** (-jnp.arange(half, dtype=f32) / half)
    ang = pos.astype(f32)[:, None] * inv[None, :]
    cos = jnp.tile(jnp.cos(ang), (1, BLOCK // half))
    sin = jnp.tile(jnp.sin(ang), (1, BLOCK // half))
    sinks8 = jnp.broadcast_to(attn_sinks.reshape(ATT_HEADS, 1), (ATT_HEADS, BLOCK))
    ng = hg_norm_g.reshape(1, HG_K)

    h0, h0b, pa, pg = emb_inproj(xs, metablk, vec(ln_emb_g), vec(ln_emb_b), wf_in, nb)
    og, sprev, g_fi = hgrn_fwd(pa, hg_lower_bounds, ng, nb, [s_fi])
    oatt, g_fo, g_out, g_bh, g_ba = attn_fwd(pa, cos, sin, sinks8, nb, [s_fo, s_out, s_bh, s_ba])
    wf_bh, wf_ba, wf_fi = by_cols(g_bh), by_cols(g_ba), by_cols(g_fi)
    wf_out = g_out.reshape(D_MODEL, D_MODEL)
    wf_fo = g_fo.reshape(D_FF, D_MODEL)
    (dh0p, dpg, dog, doa, dyh, dya, mixin, dr1, h1b, dau, sact, dr2,
     loss_part, dg1, db1, dg2, db2) = mid_rows(h0, pg, og, oatt, ts, wf_bh, wf_ba, wf_out, wf_fi, wf_fo,
                                              ln1_g, ln1_b, ln2_g, ln2_b, nb)
    dhq, dhf, dhi, dhg, dlb4, dng = hgrn_bwd(pa, hg_lower_bounds, ng, sprev, dog, nb)
    daq, dkc, dkp, dvc, dvp, dkm, dvm, dsk = attn_bwd(pa, cos, sin, sinks8, doa, nb)
    dproj, dxp, dlg, dlb = inproj_bwd(dh0p, dhq, dhf, dhi, dhg, daq, dkc, dkp, dvc, dvp, dkm, dvm, dpg,
                                      wf_in, xs, metablk, vec(ln_emb_g), vec(ln_emb_b), nb)

    tp = BLOCK * (5 if nb % 5 == 0 else 1)
    gw_in = wgrad(h0b, dproj, "wgrad_in", D_MODEL, IN_W // 2, tp, False)
    gw_in = gw_in.reshape(D_MODEL, N_SHARD, IN_W // N_SHARD).transpose(1, 0, 2)
    gw_bh = wgrad(og, dyh, "wgrad_bh", 512, 256, tp, True)
    gw_ba = wgrad(oatt, dya, "wgrad_ba", 512, 256, tp, True)
    gw_out = wgrad(mixin, dr1, "wgrad_out", D_MODEL, D_MODEL, tp, False).reshape(N_SHARD, -1, D_MODEL)
    gw_fi = wgrad(h1b, dau, "wgrad_fi", D_MODEL, 2 * D_FF // N_SHARD, tp, True)
    gw_fo = wgrad(sact, dr2, "wgrad_fo", D_FF // 2, D_MODEL, tp, False).reshape(N_SHARD, -1, D_MODEL)
    grads = [gw_in, gw_bh, gw_ba, gw_out, gw_fi, gw_fo]

    parts = [(dlg, 8), (dlb, 8), (dlb4, 8), (dng, 8), (dsk[:, 0], 8),
             (dg1, 8), (db1, 8), (dg2, 8), (db2, 8), (dxp[PAD:BLOCK], BLOCK)]
    small = jnp.concatenate([_rows128(a, r) for a, r in parts], axis=0)

    c_idx = jnp.reshape(ic, (1,)).astype(jnp.int32)
    *others, gathered = pair_exchange_halves(grads, small)
    sums = [add_pair(g, o, c_idx, "add_pair_%d" % n) for n, (g, o) in enumerate(zip(grads, others))]
    quads = chip_exchange(sums)
    jc_idx = jnp.stack([shard, ic]).astype(jnp.int32)
    halves = [add_four(s, q, jc_idx, "add_four_%d" % n) for n, (s, q) in enumerate(zip(sums, quads))]
    red = [r.reshape(-1, r.shape[-1]) for r in pair_exchange_results(halves)]
    small_sum = sum_devices(gathered)

    offs, acc = [], 0
    for _, r in parts:
        offs.append(acc)
        acc += r
    take = lambda n, size: small_sum[offs[n]:offs[n] + parts[n][1]].reshape(-1)[:size]
    g_meta_full = take(9, N_META * D_MODEL).reshape(N_META, D_MODEL)
    g_small = {
        "meta_tokens": lax.dynamic_slice_in_dim(g_meta_full, shard * (D_MODEL // N_SHARD), D_MODEL // N_SHARD, axis=1),
        "ln_emb_g": take(0, D_MODEL), "ln_emb_b": take(1, D_MODEL),
        "hg_lower_bounds": take(2, 2 * HG_HEADS * HG_K).reshape(2, HG_HEADS * HG_K),
        "hg_norm_g": take(3, HG_K).reshape(1, HG_K), "attn_sinks": take(4, ATT_HEADS).reshape(1, ATT_HEADS),
        "ln1_g": take(5, D_MODEL).reshape(1, D_MODEL), "ln1_b": take(6, D_MODEL).reshape(1, D_MODEL),
        "ln2_g": take(7, D_MODEL).reshape(1, D_MODEL), "ln2_b": take(8, D_MODEL).reshape(1, D_MODEL),
    }
    g_big = {"w_in": red[0], "w_branch_hg": red[1], "w_branch_attn": red[2], "w_out": red[3],
             "w_ffn_in": red[4], "w_ffn_out": red[5]}

    names = ["meta_tokens", "ln_emb_g", "ln_emb_b", "w_in", "hg_lower_bounds", "hg_norm_g", "attn_sinks",
             "w_branch_hg", "w_branch_attn", "w_out", "ln1_g", "ln1_b", "w_ffn_in", "w_ffn_out", "ln2_g", "ln2_b"]
    given = dict(
        meta_tokens=(meta_tokens, m_meta_tokens, v_meta_tokens), ln_emb_g=(ln_emb_g, m_ln_emb_g, v_ln_emb_g),
        ln_emb_b=(ln_emb_b, m_ln_emb_b, v_ln_emb_b), w_in=(w_in, m_w_in, v_w_in),
        hg_lower_bounds=(hg_lower_bounds, m_hg_lower_bounds, v_hg_lower_bounds),
        hg_norm_g=(hg_norm_g, m_hg_norm_g, v_hg_norm_g), attn_sinks=(attn_sinks, m_attn_sinks, v_attn_sinks),
        w_branch_hg=(w_branch_hg, m_w_branch_hg, v_w_branch_hg),
        w_branch_attn=(w_branch_attn, m_w_branch_attn, v_w_branch_attn), w_out=(w_out, m_w_out, v_w_out),
        ln1_g=(ln1_g, m_ln1_g, v_ln1_g), ln1_b=(ln1_b, m_ln1_b, v_ln1_b), w_ffn_in=(w_ffn_in, m_w_ffn_in, v_w_ffn_in),
        w_ffn_out=(w_ffn_out, m_w_ffn_out, v_w_ffn_out), ln2_g=(ln2_g, m_ln2_g, v_ln2_g), ln2_b=(ln2_b, m_ln2_b, v_ln2_b))
    out_g, out_d, out_m, out_v = [], [], [], []
    for nm in names:
        w, m, v = given[nm]
        shape = w.shape
        g = g_big[nm] if nm in g_big else g_small[nm]
        two_d = (lambda a: a.reshape(8, BLOCK)) if w.ndim == 1 else (lambda a: a.reshape(a.shape[-2], a.shape[-1]))
        d, mn, vn = adamw(two_d(w), two_d(g), two_d(m), two_d(v), "adamw_" + nm)
        out_g.append(g.reshape(shape))
        out_d.append(d.reshape(shape))
        out_m.append(mn.reshape(shape))
        out_v.append(vn.reshape(shape))

    loss = lax.psum(loss_part[0, 0], ("x", "y", "c"))
    grad_x = dxp[BLOCK:].reshape(x.shape)
    return (loss, grad_x, *out_g, *out_d, *out_m, *out_v)
```

```python
import functools

import jax
import jax.numpy as jnp
from jax import lax
from jax.experimental import pallas as pl
from jax.experimental.pallas import tpu as pltpu

f32 = jnp.float32
bf16 = jnp.bfloat16

D_MODEL = 1024
BLOCK = 128
N_META = 16
PAD = BLOCK - N_META
HG_HEADS = 4
HG_K = 128
SUB = 16
ATT_HEADS = 8
HEAD_DIM = 64
ATT_QW = ATT_HEADS * HEAD_DIM
D_FF = 2816
EPS = 1e-5
ALPHA = 2.0 ** 0.25
ROPE_THETA = 10000.0
N_A = 2816
N_G = 2048
IN_W = N_A + N_G
N_SHARD = 4
N_DEV = 8

ADAM_LR = 0.001
ADAM_B1 = 0.9
ADAM_B2 = 0.999
ADAM_EPS = 1e-08
ADAM_WD = 0.01
ADAM_STEP = 10

VMEM_LIMIT = 56 * 1024 * 1024
MESH = pl.DeviceIdType.MESH


def _cparams(sem, vmem=VMEM_LIMIT):
    return pltpu.CompilerParams(dimension_semantics=sem, vmem_limit_bytes=vmem)


def _const_spec(shape):
    zeros = (0,) * len(shape)
    return pl.BlockSpec(shape, lambda *_: zeros, pipeline_mode=pl.Buffered(1))


def _dot(a, b, ca, cb):
    return lax.dot_general(a.astype(bf16), b.astype(bf16), (((ca,), (cb,)), ((), ())),
                           preferred_element_type=f32)


@jax.custom_vjp
def mm(a, b):
    return _dot(a, b, 1, 0)


mm.defvjp(lambda a, b: (_dot(a, b, 1, 0), (a, b)),
          lambda r, g: (_dot(g, r[1], 1, 1), _dot(r[0], g, 0, 0)))


@jax.custom_vjp
def mm_nt(a, b):
    return _dot(a, b, 1, 1)


mm_nt.defvjp(lambda a, b: (_dot(a, b, 1, 1), (a, b)),
             lambda r, g: (_dot(g, r[1], 1, 0), _dot(g, r[0], 0, 0)))


@jax.custom_vjp
def mm_tn(a, b):
    return _dot(a, b, 0, 0)


mm_tn.defvjp(lambda a, b: (_dot(a, b, 0, 0), (a, b)),
             lambda r, g: (_dot(r[1], g, 1, 1), _dot(r[0], g, 1, 0)))


@functools.partial(jax.custom_vjp, nondiff_argnums=(1,))
def roll_lanes(x, shift):
    return pltpu.roll(x, shift, 1)


roll_lanes.defvjp(lambda x, shift: (pltpu.roll(x, shift, 1), None),
                  lambda shift, _, g: (pltpu.roll(g, (128 - shift) % 128, 1),))


def _sigmoid(x):
    return 1.0 / (1.0 + jnp.exp(-x))


def _ln_stats(x):
    mu = jnp.mean(x, axis=-1, keepdims=True)
    xc = x - mu
    var = jnp.mean(xc * xc, axis=-1, keepdims=True)
    rs = lax.rsqrt(var + EPS)
    return xc * rs, rs


def _ln_bwd(dy, xh, rs, g):
    dxh = dy * g
    m1 = jnp.mean(dxh, axis=-1, keepdims=True)
    m2 = jnp.mean(dxh * xh, axis=-1, keepdims=True)
    return rs * (dxh - m1 - xh * m2)


def _row_ids(i):
    return i * BLOCK + lax.broadcasted_iota(jnp.int32, (BLOCK, 1), 0)


def emb_inproj(x, metablk, g, b, w_in, nb):
    def body(x_ref, mb_ref, g_ref, b_ref, w_ref, h0_ref, h0b_ref, pa_ref, pg_ref):
        i = pl.program_id(0)
        xb = jnp.where(i == 0, mb_ref[...], x_ref[...])
        xh, _ = _ln_stats(xb)
        y = xh * g_ref[...] + b_ref[...]
        y = jnp.where(_row_ids(i) >= PAD, y, 0.0)
        h0_ref[...] = y
        yb = y.astype(bf16)
        h0b_ref[...] = yb
        pa_ref[...] = jnp.dot(yb, w_ref[:, :N_A], preferred_element_type=f32)
        pg_ref[...] = jnp.dot(yb, w_ref[:, N_A:], preferred_element_type=f32)

    p = nb * BLOCK
    row = lambda n: pl.BlockSpec((BLOCK, n), lambda i: (i, 0))
    return pl.pallas_call(
        body, name="emb_inproj", grid=(nb,),
        in_specs=[pl.BlockSpec((BLOCK, D_MODEL), lambda i: (jnp.maximum(i - 1, 0), 0)),
                  _const_spec((BLOCK, D_MODEL)), _const_spec((1, D_MODEL)), _const_spec((1, D_MODEL)),
                  _const_spec((D_MODEL, IN_W))],
        out_specs=[row(D_MODEL), row(D_MODEL), row(N_A), row(N_G)],
        out_shape=[jax.ShapeDtypeStruct((p, D_MODEL), f32), jax.ShapeDtypeStruct((p, D_MODEL), bf16),
                   jax.ShapeDtypeStruct((p, N_A), f32), jax.ShapeDtypeStruct((p, N_G), f32)],
        compiler_params=_cparams(("parallel",)),
    )(x, metablk, g, b, w_in)


def _hgrn_chunk(valid, st, hq, hf, hi, hg, lbraw, ng):
    lb = _sigmoid(lbraw[0:1] - lbraw[1:2])
    q = hq * _sigmoid(hq)
    fg = lb + (1.0 - lb) * _sigmoid(hf)
    logf = jnp.where(valid, jnp.log(fg), 0.0)
    k = jnp.where(valid, 1.0 - fg, 0.0)
    v = hi
    r = lax.broadcasted_iota(jnp.int32, (BLOCK, BLOCK), 0)
    c = lax.broadcasted_iota(jnp.int32, (BLOCK, BLOCK), 1)
    tril = (c <= r).astype(f32)
    bcum = jnp.dot(tril, logf, precision=lax.Precision.HIGHEST, preferred_element_type=f32)
    blast = bcum[BLOCK - 1:BLOCK]
    rows = lax.broadcasted_iota(jnp.int32, (BLOCK, 1), 0)
    sub8 = lax.broadcasted_iota(jnp.int32, (BLOCK // 8, 8, HG_K), 1)
    b8 = bcum.reshape(BLOCK // 8, 8, HG_K)
    row_of_8 = lambda j: jnp.broadcast_to(b8[:, j:j + 1, :], b8.shape)
    a = jnp.where(r == c, jnp.sum(q * k, axis=-1, keepdims=True), 0.0)
    seg = BLOCK
    while seg >= 2:
        half = seg // 2
        if seg >= 8:
            bs = bcum.reshape(BLOCK // seg, seg, HG_K)
            ref = jnp.broadcast_to(bs[:, half - 1:half, :], bs.shape)
        elif seg == 4:
            ref = jnp.where(sub8 < 4, row_of_8(1), row_of_8(5))
        else:
            ref = jnp.where(sub8 < 2, row_of_8(0), jnp.where(sub8 < 4, row_of_8(2),
                                                             jnp.where(sub8 < 6, row_of_8(4), row_of_8(6))))
        ref = ref.reshape(BLOCK, HG_K)
        upper = (rows % seg) >= half
        q_up = q * jnp.exp(jnp.where(upper, bcum - ref, -jnp.inf))
        k_lo = k * jnp.exp(jnp.where(upper, -jnp.inf, ref - bcum))
        a = a + jnp.where((r // seg) == (c // seg), mm_nt(q_up, k_lo), 0.0)
        seg = half
    o = mm_nt(q * jnp.exp(bcum), st) + mm(a, v)
    st_new = st * jnp.exp(blast) + mm_tn(v, k * jnp.exp(blast - bcum))
    on = o * lax.rsqrt(jnp.mean(o * o, axis=-1, keepdims=True) + EPS) * ng
    return st_new, on * (hg * _sigmoid(hg))


def _hgrn_in_specs(rowmap):
    wide = lambda col: pl.BlockSpec((BLOCK, HG_HEADS * HG_K), lambda i: (rowmap(i), col))
    return [wide(0), wide(1), wide(2), wide(3), _const_spec((2, HG_HEADS * HG_K)), _const_spec((1, HG_K))]


def _head(ref, h):
    return ref[:, h * HG_K:(h + 1) * HG_K]


def hgrn_fwd(pa, lbraw, ng, nb, shards):
    n = len(shards)

    def body(hq_ref, hf_ref, hi_ref, hg_ref, lb_ref, ng_ref, *rest):
        srcs, (og_ref, sp_ref), dsts = rest[:n], rest[n:n + 2], rest[n + 2:2 * n + 2]
        st_ref = rest[2 * n + 2]
        start, wait = _shard_push(srcs, dsts, *rest[2 * n + 3:])
        i = pl.program_id(0)

        @pl.when(i == 0)
        def _():
            st_ref[...] = jnp.zeros_like(st_ref)
            start()

        @pl.when(i == nb - 1)
        def _():
            wait()

        valid = _row_ids(i) >= PAD
        for h in range(HG_HEADS):
            st = st_ref[h]
            sp_ref[0, h] = st
            st_new, out = _hgrn_chunk(valid, st, _head(hq_ref, h), _head(hf_ref, h), _head(hi_ref, h),
                                      _head(hg_ref, h), _head(lb_ref, h), ng_ref[...])
            st_ref[h] = st_new
            og_ref[:, h * HG_K:(h + 1) * HG_K] = out.astype(bf16)

    p = nb * BLOCK
    push_in, push_out, push_shape, push_scratch = _push_specs(shards)
    return pl.pallas_call(
        body, name="hgrn_fwd", grid=(nb,),
        in_specs=_hgrn_in_specs(lambda i: i) + push_in,
        out_specs=[pl.BlockSpec((BLOCK, HG_HEADS * HG_K), lambda i: (i, 0)),
                   pl.BlockSpec((1, HG_HEADS, HG_K, HG_K), lambda i: (i, 0, 0, 0))] + push_out,
        out_shape=[jax.ShapeDtypeStruct((p, HG_HEADS * HG_K), bf16),
                   jax.ShapeDtypeStruct((nb, HG_HEADS, HG_K, HG_K), f32)] + push_shape,
        scratch_shapes=[pltpu.VMEM((HG_HEADS, HG_K, HG_K), f32)] + push_scratch,
        compiler_params=_cparams(("arbitrary",)),
    )(pa, pa, pa, pa, lbraw, ng, *shards)


def hgrn_bwd(pa, lbraw, ng, sprev, dog, nb, grads):
    n = len(grads)

    def body(hq_ref, hf_ref, hi_ref, hg_ref, lb_ref, ng_ref, sp_ref, do_ref, *rest):
        srcs, rest = rest[:n], rest[n:]
        dq_ref, df_ref, di_ref, dg_ref, dlb_ref, dng_ref = rest[:6]
        dsts, dst_ref = rest[6:6 + n], rest[6 + n]
        start, wait = _grad_push(srcs, dsts, *rest[7 + n:])
        i = pl.program_id(0)

        @pl.when(i == 0)
        def _():
            dst_ref[...] = jnp.zeros_like(dst_ref)
            dlb_ref[...] = jnp.zeros_like(dlb_ref)
            dng_ref[...] = jnp.zeros_like(dng_ref)
            start()

        valid = _row_ids(nb - 1 - i) >= PAD
        dng_sum = jnp.zeros((1, HG_K), f32)
        for h in range(HG_HEADS):
            cols = slice(h * HG_K, (h + 1) * HG_K)
            _, vjp = jax.vjp(functools.partial(_hgrn_chunk, valid), sp_ref[0, h], _head(hq_ref, h), _head(hf_ref, h),
                             _head(hi_ref, h), _head(hg_ref, h), _head(lb_ref, h), ng_ref[...])
            dst, dq, df, di, dg, dlb, dng = vjp((dst_ref[h], _head(do_ref, h)))
            dst_ref[h] = dst
            dq_ref[:, cols] = dq.astype(bf16)
            df_ref[:, cols] = df.astype(bf16)
            di_ref[:, cols] = di.astype(bf16)
            dg_ref[:, cols] = dg.astype(bf16)
            dlb_ref[:, cols] += dlb
            dng_sum = dng_sum + dng
        dng_ref[...] += dng_sum
        pl.when(i == nb - 1)(wait)

    p = nb * BLOCK
    rev = lambda i: nb - 1 - i
    hw = HG_HEADS * HG_K
    blk = pl.BlockSpec((BLOCK, hw), lambda i: (rev(i), 0))
    wide = jax.ShapeDtypeStruct((p, hw), bf16)
    push_in, push_out, push_shape, push_scratch = _grad_push_specs(grads)
    return pl.pallas_call(
        body, name="hgrn_bwd", grid=(nb,),
        in_specs=_hgrn_in_specs(rev) + [pl.BlockSpec((1, HG_HEADS, HG_K, HG_K), lambda i: (rev(i), 0, 0, 0)), blk]
        + push_in,
        out_specs=[blk, blk, blk, blk, pl.BlockSpec((2, hw), lambda i: (0, 0)), pl.BlockSpec((1, HG_K), lambda i: (0, 0))]
        + push_out,
        out_shape=[wide, wide, wide, wide, jax.ShapeDtypeStruct((2, hw), f32), jax.ShapeDtypeStruct((1, HG_K), f32)]
        + push_shape,
        scratch_shapes=[pltpu.VMEM((HG_HEADS, HG_K, HG_K), f32)] + push_scratch,
        compiler_params=_cparams(("arbitrary",)),
    )(pa, pa, pa, pa, lbraw, ng, sprev, dog, *grads)


def _rope(x, cos, sin):
    lane = lax.broadcasted_iota(jnp.int32, x.shape, 1)
    rot = jnp.where(lane % HEAD_DIM < HEAD_DIM // 2, -roll_lanes(x, BLOCK - HEAD_DIM // 2),
                    roll_lanes(x, HEAD_DIM // 2))
    return x * cos + rot * sin


def _both_halves(x, g):
    lo = lax.broadcasted_iota(jnp.int32, x.shape, 1) < HEAD_DIM
    sw = roll_lanes(x, HEAD_DIM)
    return jnp.where(lo, x, sw) if g == 0 else jnp.where(lo, sw, x)


def _attn_block(band_ok, meta_ok, tabs, q, kp, kc, vp, vc, km, vm, *sinks):
    cq, sq, cp, sp, cm, sm = tabs
    neg = jnp.finfo(f32).min
    scale = HEAD_DIM ** -0.5
    group = ATT_HEADS // 2
    kpr, kcr, kmr = _rope(kp, cp, sp), _rope(kc, cq, sq), _rope(km, cm, sm)
    lo = lax.broadcasted_iota(jnp.int32, (BLOCK, BLOCK), 1) < HEAD_DIM
    t = lax.broadcasted_iota(jnp.int32, (group * BLOCK, BLOCK), 0) % BLOCK
    own = lax.broadcasted_iota(jnp.int32, (group * BLOCK, BLOCK), 1) <= t
    qr = [_rope(q[:, m * BLOCK:(m + 1) * BLOCK], cq, sq) for m in range(ATT_HEADS // 2)]
    slabs = []
    for g in range(2):
        kp_g, kc_g, vp_g, vc_g, km_g, vm_g = [_both_halves(a, g) for a in (kpr, kcr, vp, vc, kmr, vm)]
        qs = jnp.concatenate([jnp.where(lo if h % 2 == 0 else ~lo, qr[2 * g + h // 2], 0.0) for h in range(group)],
                             axis=0)
        sink = jnp.concatenate([jnp.broadcast_to(sinks[group * g + h], (BLOCK, 1)) for h in range(group)], axis=0)
        sb = jnp.where(band_ok, jnp.where(own, mm_nt(qs, kc_g), mm_nt(qs, kp_g)) * scale, neg)
        sme = jnp.where(meta_ok, mm_nt(qs, km_g) * scale, neg)
        mx = lax.stop_gradient(jnp.maximum(jnp.maximum(jnp.max(sb, axis=-1, keepdims=True),
                                                       jnp.max(sme, axis=-1, keepdims=True)), sink))
        eb, em = jnp.exp(sb - mx), jnp.exp(sme - mx)
        inv = 1.0 / (jnp.sum(eb, axis=-1, keepdims=True) + jnp.sum(em, axis=-1, keepdims=True) + jnp.exp(sink - mx))
        pb = eb * inv
        o = mm(jnp.where(own, pb, 0.0), vc_g) + mm(jnp.where(own, 0.0, pb), vp_g) + mm(em * inv, vm_g)
        for m in range(2):
            slabs.append(jnp.where(lo, o[2 * m * BLOCK:(2 * m + 1) * BLOCK], o[(2 * m + 1) * BLOCK:(2 * m + 2) * BLOCK]))
    return jnp.concatenate(slabs, axis=1)


def _attn_masks(i):
    group = ATT_HEADS // 2
    t = lax.broadcasted_iota(jnp.int32, (group * BLOCK, BLOCK), 0) % BLOCK
    s = lax.broadcasted_iota(jnp.int32, (group * BLOCK, BLOCK), 1)
    kpos = jnp.where(s <= t, i * BLOCK - PAD + s, jnp.where(i > 0, (i - 1) * BLOCK - PAD + s, -1))
    band_ok = kpos >= N_META
    qpos = i * BLOCK - PAD + lax.broadcasted_iota(jnp.int32, (group * BLOCK, 1), 0) % BLOCK
    meta_ok = lax.broadcasted_iota(jnp.int32, (1, N_META), 1) <= qpos
    return band_ok, meta_ok


def _attn_in_specs():
    prev = lambda i: jnp.maximum(i - 1, 0)
    kcol, vcol = N_A // BLOCK - 2, N_A // BLOCK - 1
    blk = lambda rowmap, col: pl.BlockSpec((BLOCK, BLOCK), lambda i: (rowmap(i), col))
    cur, first = (lambda i: i), (lambda i: 0)
    return [pl.BlockSpec((BLOCK, ATT_QW), lambda i: (i, 4)),
            blk(prev, kcol), blk(cur, kcol), blk(prev, vcol), blk(cur, vcol), blk(first, kcol), blk(first, vcol),
            blk(cur, 0), blk(cur, 0), blk(prev, 0), blk(prev, 0), blk(first, 0), blk(first, 0),
            _const_spec((ATT_HEADS, BLOCK))]


def _attn_operands(q_ref, kp_ref, kc_ref, vp_ref, vc_ref, km_ref, vm_ref, cq, sq, cp, sp, cm, sm, sk_ref):
    tabs = (cq[...], sq[...], cp[...], sp[...], cm[PAD:, :], sm[PAD:, :])
    args = (q_ref[...], kp_ref[...], kc_ref[...], vp_ref[...], vc_ref[...], km_ref[PAD:, :], vm_ref[PAD:, :])
    sinks = tuple(sk_ref[j:j + 1, 0:1] for j in range(ATT_HEADS))
    return tabs, args + sinks


def attn_fwd(pa, cos, sin, sinks8, nb, shards):
    n = len(shards)
    n_in = 14

    def body(*refs):
        srcs, o_ref, dsts = refs[n_in:n_in + n], refs[n_in + n], refs[n_in + n + 1:n_in + 2 * n + 1]
        start, wait = _shard_push(srcs, dsts, *refs[n_in + 2 * n + 1:])
        i = pl.program_id(0)
        pl.when(i == 0)(start)
        band_ok, meta_ok = _attn_masks(i)
        tabs, args = _attn_operands(*refs[:n_in])
        o_ref[...] = _attn_block(band_ok, meta_ok, tabs, *args).astype(bf16)
        pl.when(i == nb - 1)(wait)

    push_in, push_out, push_shape, push_scratch = _push_specs(shards)
    return pl.pallas_call(
        body, name="attn_fwd", grid=(nb,), in_specs=_attn_in_specs() + push_in,
        out_specs=[pl.BlockSpec((BLOCK, ATT_QW), lambda i: (i, 0))] + push_out,
        out_shape=[jax.ShapeDtypeStruct((nb * BLOCK, ATT_QW), bf16)] + push_shape,
        scratch_shapes=push_scratch,
        compiler_params=_cparams(("arbitrary",)),
    )(pa, pa, pa, pa, pa, pa, pa, cos, sin, cos, sin, cos, sin, sinks8, *shards)


def attn_bwd(pa, cos, sin, sinks8, do, nb, grads):
    n = len(grads)

    def body(*refs):
        do_ref, srcs = refs[14], refs[15:15 + n]
        dq_ref, dkc_ref, dkp_ref, dvc_ref, dvp_ref, dkm_ref, dvm_ref, dsk_ref = refs[15 + n:23 + n]
        start, wait = _grad_push(srcs, refs[23 + n:23 + 2 * n], *refs[23 + 2 * n:])
        i = pl.program_id(0)

        @pl.when(i == 0)
        def _():
            dkm_ref[...] = jnp.zeros((N_META, BLOCK), f32)
            dvm_ref[...] = jnp.zeros((N_META, BLOCK), f32)
            dsk_ref[...] = jnp.zeros((ATT_HEADS, BLOCK), f32)
            start()

        band_ok, meta_ok = _attn_masks(i)
        tabs, args = _attn_operands(*refs[:14])
        _, vjp = jax.vjp(functools.partial(_attn_block, band_ok, meta_ok, tabs), *args)
        grads = vjp(do_ref[...])
        dq_ref[...] = grads[0].astype(bf16)
        dkp_ref[...] = grads[1]
        dkc_ref[...] = grads[2]
        dvp_ref[...] = grads[3]
        dvc_ref[...] = grads[4]
        dkm_ref[...] += grads[5]
        dvm_ref[...] += grads[6]
        for j in range(ATT_HEADS):
            dsk_ref[j:j + 1, :] += jnp.broadcast_to(grads[7 + j], (1, BLOCK))
        pl.when(i == nb - 1)(wait)

    p = nb * BLOCK
    row = pl.BlockSpec((BLOCK, BLOCK), lambda i: (i, 0))
    const = lambda r: pl.BlockSpec((r, BLOCK), lambda i: (0, 0))
    part = jax.ShapeDtypeStruct((p, BLOCK), f32)
    push_in, push_out, push_shape, push_scratch = _grad_push_specs(grads)
    return pl.pallas_call(
        body, name="attn_bwd", grid=(nb,),
        in_specs=_attn_in_specs() + [pl.BlockSpec((BLOCK, ATT_QW), lambda i: (i, 0))] + push_in,
        out_specs=[pl.BlockSpec((BLOCK, ATT_QW), lambda i: (i, 0)), row, row, row, row,
                   const(N_META), const(N_META), const(ATT_HEADS)] + push_out,
        out_shape=[jax.ShapeDtypeStruct((p, ATT_QW), bf16), part, part, part, part,
                   jax.ShapeDtypeStruct((N_META, BLOCK), f32), jax.ShapeDtypeStruct((N_META, BLOCK), f32),
                   jax.ShapeDtypeStruct((ATT_HEADS, BLOCK), f32)] + push_shape,
        scratch_shapes=push_scratch,
        compiler_params=_cparams(("arbitrary",)),
    )(pa, pa, pa, pa, pa, pa, pa, cos, sin, cos, sin, cos, sin, sinks8, do, *grads)


def mid_rows(h0, pg, og, oatt, target, wbh, wba, wout, wfi, wfo, ln1g, ln1b, ln2g, ln2b, nb):
    def body(h0_ref, pg_ref, og_ref, oa_ref, t_ref, wbh_ref, wba_ref, wo_ref, wfi_ref, wfo_ref,
             g1_ref, b1_ref, g2_ref, b2_ref,
             dh0_ref, dpg_ref, dog_ref, doa_ref, dyh_ref, dya_ref, mix_ref, dr1_ref, h1b_ref, dau_ref, s_ref, dr2_ref,
             loss_ref, dg1_ref, db1_ref, dg2_ref, db2_ref):
        i = pl.program_id(0)

        @pl.when(i == 0)
        def _():
            loss_ref[...] = jnp.zeros_like(loss_ref)
            for r in (dg1_ref, db1_ref, dg2_ref, db2_ref):
                r[...] = jnp.zeros_like(r)

        g1, b1, g2, b2 = g1_ref[...], b1_ref[...], g2_ref[...], b2_ref[...]
        yh = jnp.dot(og_ref[...], wbh_ref[...], preferred_element_type=f32)
        ya = jnp.dot(oa_ref[...], wba_ref[...], preferred_element_type=f32)
        gh = _sigmoid(pg_ref[:, :D_MODEL])
        ga = _sigmoid(pg_ref[:, D_MODEL:])
        mixin = (gh * yh + ga * ya).astype(bf16)
        mix_ref[...] = mixin
        r1 = ALPHA * h0_ref[...] + jnp.dot(mixin, wo_ref[...], preferred_element_type=f32)
        xh1, rs1 = _ln_stats(r1)
        h1 = xh1 * g1 + b1
        h1b = h1.astype(bf16)
        h1b_ref[...] = h1b
        au = jnp.dot(h1b, wfi_ref[...], preferred_element_type=f32)
        a, u = au[:, :D_FF], au[:, D_FF:]
        sg = _sigmoid(a)
        sa = a * sg
        s = (sa * u).astype(bf16)
        s_ref[...] = s
        r2 = ALPHA * h1 + jnp.dot(s, wfo_ref[...], preferred_element_type=f32)
        xh2, rs2 = _ln_stats(r2)
        diff = jnp.where(i > 0, xh2 * g2 + b2 - t_ref[...], 0.0)
        loss_ref[...] += jnp.sum(diff * diff) * (0.5 / D_MODEL)
        dy = diff * (1.0 / D_MODEL)
        dg2_ref[...] += jnp.sum(dy * xh2, axis=0, keepdims=True)
        db2_ref[...] += jnp.sum(dy, axis=0, keepdims=True)
        dr2 = _ln_bwd(dy, xh2, rs2, g2)
        dr2b = dr2.astype(bf16)
        dr2_ref[...] = dr2b
        ds = _dot(dr2b, wfo_ref[...], 1, 1)
        da = (ds * u) * (sg * (1.0 + a * (1.0 - sg)))
        du = ds * sa
        dau = jnp.concatenate([da, du], axis=1).astype(bf16)
        dau_ref[...] = dau
        dh1 = ALPHA * dr2 + _dot(dau, wfi_ref[...], 1, 1)
        dg1_ref[...] += jnp.sum(dh1 * xh1, axis=0, keepdims=True)
        db1_ref[...] += jnp.sum(dh1, axis=0, keepdims=True)
        dr1 = _ln_bwd(dh1, xh1, rs1, g1)
        dr1b = dr1.astype(bf16)
        dr1_ref[...] = dr1b
        dh0_ref[...] = ALPHA * dr1
        dmix = _dot(dr1b, wo_ref[...], 1, 1)
        dyh = (dmix * gh).astype(bf16)
        dya = (dmix * ga).astype(bf16)
        dyh_ref[...] = dyh
        dya_ref[...] = dya
        dpg_ref[:, :D_MODEL] = (dmix * yh * gh * (1.0 - gh)).astype(bf16)
        dpg_ref[:, D_MODEL:] = (dmix * ya * ga * (1.0 - ga)).astype(bf16)
        dog_ref[...] = _dot(dyh, wbh_ref[...], 1, 1)
        doa_ref[...] = _dot(dya, wba_ref[...], 1, 1)

    p = nb * BLOCK
    row = lambda n: pl.BlockSpec((BLOCK, n), lambda i: (i, 0))
    vec = lambda: pl.BlockSpec((1, D_MODEL), lambda i: (0, 0))
    sds = lambda n, dt: jax.ShapeDtypeStruct((p, n), dt)
    hw = HG_HEADS * HG_K
    return pl.pallas_call(
        body, name="mid_rows", grid=(nb,),
        in_specs=[row(D_MODEL), row(N_G), row(hw), row(ATT_QW),
                  pl.BlockSpec((BLOCK, D_MODEL), lambda i: (jnp.maximum(i - 1, 0), 0)),
                  _const_spec((hw, D_MODEL)), _const_spec((ATT_QW, D_MODEL)), _const_spec((D_MODEL, D_MODEL)),
                  _const_spec((D_MODEL, 2 * D_FF)), _const_spec((D_FF, D_MODEL)),
                  _const_spec((1, D_MODEL)), _const_spec((1, D_MODEL)), _const_spec((1, D_MODEL)),
                  _const_spec((1, D_MODEL))],
        out_specs=[row(D_MODEL), row(N_G), row(hw), row(ATT_QW), row(D_MODEL), row(D_MODEL), row(D_MODEL),
                   row(D_MODEL), row(D_MODEL), row(2 * D_FF), row(D_FF), row(D_MODEL),
                   pl.BlockSpec((1, 1), lambda i: (0, 0)), vec(), vec(), vec(), vec()],
        out_shape=[sds(D_MODEL, f32), sds(N_G, bf16), sds(hw, f32), sds(ATT_QW, f32), sds(D_MODEL, bf16),
                   sds(D_MODEL, bf16), sds(D_MODEL, bf16), sds(D_MODEL, bf16), sds(D_MODEL, bf16),
                   sds(2 * D_FF, bf16), sds(D_FF, bf16), sds(D_MODEL, bf16),
                   jax.ShapeDtypeStruct((1, 1), f32)] + [jax.ShapeDtypeStruct((1, D_MODEL), f32)] * 4,
        compiler_params=_cparams(("arbitrary",)),
    )(h0, pg, og, oatt, target, wbh, wba, wout, wfi, wfo, ln1g, ln1b, ln2g, ln2b)


def inproj_bwd(dh0p, dhq, dhf, dhi, dhg, daq, dkc, dkp, dvc, dvp, dkm, dvm, dpg, w_in, x, metablk, g, b, nb):
    def body(dh0_ref, dq_ref, df_ref, di_ref, dg_ref, daq_ref, dkc_ref, dkp_ref, dvc_ref, dvp_ref, dkm_ref, dvm_ref,
             dpg_ref, w_ref, x_ref, mb_ref, g_ref, b_ref, dproj_ref, dx_ref, dlg_ref, dlb_ref):
        i = pl.program_id(0)

        @pl.when(i == 0)
        def _():
            dlg_ref[...] = jnp.zeros_like(dlg_ref)
            dlb_ref[...] = jnp.zeros_like(dlb_ref)

        zero_pad = jnp.zeros((PAD, BLOCK), f32)
        has_next = i + 1 < nb
        first = i == 0

        def keys(cur_ref, next_ref, meta_ref):
            t = cur_ref[...] + jnp.where(has_next, next_ref[...], 0.0)
            return t + jnp.where(first, jnp.concatenate([zero_pad, meta_ref[...]], axis=0), 0.0)

        dproj = jnp.concatenate(
            [dq_ref[...], df_ref[...], di_ref[...], dg_ref[...], daq_ref[...],
             keys(dkc_ref, dkp_ref, dkm_ref).astype(bf16), keys(dvc_ref, dvp_ref, dvm_ref).astype(bf16),
             dpg_ref[...]], axis=1)
        dproj_ref[...] = dproj
        valid = _row_ids(i) >= PAD
        dh0 = jnp.where(valid, dh0_ref[...] + _dot(dproj, w_ref[...], 1, 1), 0.0)
        xb = jnp.where(first, mb_ref[...], x_ref[...])
        xh, rs = _ln_stats(xb)
        dlg_ref[...] += jnp.sum(dh0 * xh, axis=0, keepdims=True)
        dlb_ref[...] += jnp.sum(dh0, axis=0, keepdims=True)
        dx_ref[...] = jnp.where(valid, _ln_bwd(dh0, xh, rs, g_ref[...]), 0.0)

    p = nb * BLOCK
    row = lambda n: pl.BlockSpec((BLOCK, n), lambda i: (i, 0))
    nxt = pl.BlockSpec((BLOCK, BLOCK), lambda i: (jnp.minimum(i + 1, nb - 1), 0))
    hw = HG_HEADS * HG_K
    vec = lambda: pl.BlockSpec((1, D_MODEL), lambda i: (0, 0))
    return pl.pallas_call(
        body, name="inproj_bwd", grid=(nb,),
        in_specs=[row(D_MODEL), row(hw), row(hw), row(hw), row(hw), row(ATT_QW),
                  row(BLOCK), nxt, row(BLOCK), nxt, _const_spec((N_META, BLOCK)), _const_spec((N_META, BLOCK)),
                  row(N_G), _const_spec((D_MODEL, IN_W)),
                  pl.BlockSpec((BLOCK, D_MODEL), lambda i: (jnp.maximum(i - 1, 0), 0)),
                  _const_spec((BLOCK, D_MODEL)), _const_spec((1, D_MODEL)), _const_spec((1, D_MODEL))],
        out_specs=[row(IN_W), row(D_MODEL), vec(), vec()],
        out_shape=[jax.ShapeDtypeStruct((p, IN_W), bf16), jax.ShapeDtypeStruct((p, D_MODEL), f32),
                   jax.ShapeDtypeStruct((1, D_MODEL), f32), jax.ShapeDtypeStruct((1, D_MODEL), f32)],
        compiler_params=_cparams(("arbitrary",)),
    )(dh0p, dhq, dhf, dhi, dhg, daq, dkc, dkp, dvc, dvp, dkm, dvm, dpg, w_in, x, metablk, g, b)


def wgrad(a, b, name, tk, tn, tp, by_cols, out_dtype=f32):
    p, k = a.shape
    n = b.shape[1]
    nsteps = p // tp

    def body(a_ref, b_ref, o_ref, acc_ref):
        ip = pl.program_id(2)

        @pl.when(ip == 0)
        def _():
            acc_ref[...] = jnp.zeros_like(acc_ref)

        acc_ref[...] += _dot(a_ref[...], b_ref[...], 0, 0)

        @pl.when(ip == nsteps - 1)
        def _():
            o_ref[0] = acc_ref[...].astype(out_dtype)

    if by_cols:
        shard_n = n // N_SHARD
        per = shard_n // tn
        out_shape = (N_SHARD, k, shard_n)
        omap = lambda ik, jn, ip: (jn // per, ik, jn % per)
    else:
        out_shape = (1, k, n)
        omap = lambda ik, jn, ip: (0, ik, jn)
    return pl.pallas_call(
        body, name=name, grid=(k // tk, n // tn, nsteps),
        in_specs=[pl.BlockSpec((tp, tk), lambda ik, jn, ip: (ip, ik)),
                  pl.BlockSpec((tp, tn), lambda ik, jn, ip: (ip, jn))],
        out_specs=pl.BlockSpec((1, tk, tn), omap),
        out_shape=jax.ShapeDtypeStruct(out_shape, out_dtype),
        scratch_shapes=[pltpu.VMEM((tk, tn), f32)],
        compiler_params=_cparams(("parallel", "parallel", "arbitrary")),
    )(a, b)


def adamw(w, g, m, v, name):
    r, c = w.shape
    tr = r
    for cand in (256, 176, 128):
        if r > cand and r % cand == 0:
            tr = cand
            break

    def body(w_ref, g_ref, m_ref, v_ref, d_ref, mo_ref, vo_ref):
        gg = g_ref[...]
        mn = ADAM_B1 * m_ref[...] + (1.0 - ADAM_B1) * gg
        vn = ADAM_B2 * v_ref[...] + (1.0 - ADAM_B2) * (gg * gg)
        m_hat = mn / (1.0 - ADAM_B1 ** ADAM_STEP)
        v_hat = vn / (1.0 - ADAM_B2 ** ADAM_STEP)
        d_ref[...] = -ADAM_LR * (m_hat / (jnp.sqrt(v_hat) + ADAM_EPS) + ADAM_WD * w_ref[...])
        mo_ref[...] = mn
        vo_ref[...] = vn

    spec = pl.BlockSpec((tr, c), lambda i: (i, 0))
    sds = jax.ShapeDtypeStruct((r, c), f32)
    return pl.pallas_call(
        body, name=name, grid=(r // tr,), in_specs=[spec] * 4, out_specs=[spec] * 3, out_shape=[sds] * 3,
        compiler_params=_cparams(("parallel",)),
    )(w, g, m, v)


def _me():
    return lax.axis_index("x"), lax.axis_index("y"), lax.axis_index("c")


def _chip_peer(x, y, c, k):
    return (x ^ (k >> 1), y ^ (k & 1), c)


ANY = pl.BlockSpec(memory_space=pl.ANY)


def gather_weights(now, later):
    n, n_later = len(now), len(later)
    out_dtypes = [bf16 if s.size > 16 * 256 else f32 for s in now]

    def body(*refs):
        ins, later_ins = refs[:n], refs[n:n + n_later]
        outs, later_outs = refs[n + n_later:2 * n + n_later], refs[2 * n + n_later:2 * (n + n_later)]
        stage = refs[2 * (n + n_later):3 * n + 2 * n_later]
        send_sems, recv_sems, local_sems = refs[3 * n + 2 * n_later:]
        x, y, c = _me()
        j = 2 * x + y
        for w in range(n):
            stage[w][...] = ins[w][...].astype(out_dtypes[w])
        sends, locs = [], []
        for w in range(n):
            loc = pltpu.make_async_copy(stage[w], outs[w].at[j], local_sems.at[w])
            loc.start()
            locs.append(loc)
            for k in (1, 2, 3):
                cp = pltpu.make_async_remote_copy(
                    src_ref=stage[w], dst_ref=outs[w].at[j], send_sem=send_sems.at[w, k - 1],
                    recv_sem=recv_sems.at[w, k - 1], device_id=_chip_peer(x, y, c, k), device_id_type=MESH)
                cp.start()
                sends.append(cp)
        for w in range(n_later):
            later_outs[w][...] = later_ins[w][...].astype(bf16)
        for w in range(n):
            for k in (1, 2, 3):
                pltpu.make_async_remote_copy(
                    src_ref=stage[w], dst_ref=outs[w].at[j ^ k], send_sem=send_sems.at[w, k - 1],
                    recv_sem=recv_sems.at[w, k - 1], device_id=_chip_peer(x, y, c, k), device_id_type=MESH).wait_recv()
        for cp in sends:
            cp.wait_send()
        for loc in locs:
            loc.wait()

    vmem = pl.BlockSpec(memory_space=pltpu.VMEM)
    return pl.pallas_call(
        body, name="gather_weights",
        in_specs=[vmem] * (n + n_later), out_specs=[ANY] * n + [vmem] * n_later,
        out_shape=[jax.ShapeDtypeStruct((N_SHARD,) + s.shape, dt) for s, dt in zip(now, out_dtypes)]
        + [jax.ShapeDtypeStruct(s.shape, bf16) for s in later],
        scratch_shapes=[pltpu.VMEM(s.shape, dt) for s, dt in zip(now, out_dtypes)]
        + [pltpu.SemaphoreType.DMA((n, 3)), pltpu.SemaphoreType.DMA((n, 3)), pltpu.SemaphoreType.DMA((n,))],
        compiler_params=pltpu.CompilerParams(vmem_limit_bytes=VMEM_LIMIT),
    )(*now, *later)


def _shard_push(srcs, dsts, send_sems, recv_sems, local_sems):
    def remote(w, k, slot):
        x, y, c = _me()
        return pltpu.make_async_remote_copy(
            src_ref=srcs[w], dst_ref=dsts[w].at[slot], send_sem=send_sems.at[w, k - 1],
            recv_sem=recv_sems.at[w, k - 1], device_id=_chip_peer(x, y, c, k), device_id_type=MESH)

    def local(w):
        x, y, _ = _me()
        return pltpu.make_async_copy(srcs[w], dsts[w].at[2 * x + y], local_sems.at[w])

    def start():
        x, y, _ = _me()
        for w in range(len(srcs)):
            local(w).start()
            for k in (1, 2, 3):
                remote(w, k, 2 * x + y).start()

    def wait():
        x, y, _ = _me()
        for w in range(len(srcs)):
            for k in (1, 2, 3):
                remote(w, k, (2 * x + y) ^ k).wait_recv()
        for w in range(len(srcs)):
            for k in (1, 2, 3):
                remote(w, k, 2 * x + y).wait_send()
            local(w).wait()

    return start, wait


def _grad_push(srcs, dsts, send_sems, recv_sems):
    def copy(w, k):
        x, y, c = _me()
        px, py, pc = x ^ (k >> 2), y ^ ((k >> 1) & 1), c ^ (k & 1)
        return pltpu.make_async_remote_copy(
            src_ref=srcs[w].at[2 * px + py, pc], dst_ref=dsts[w].at[k - 1], send_sem=send_sems.at[w, k - 1],
            recv_sem=recv_sems.at[w, k - 1], device_id=(px, py, pc), device_id_type=MESH)

    def start():
        for w in range(len(srcs)):
            for k in range(1, N_DEV):
                copy(w, k).start()

    def wait():
        for w in range(len(srcs)):
            for k in range(1, N_DEV):
                copy(w, k).wait_recv()
        for w in range(len(srcs)):
            for k in range(1, N_DEV):
                copy(w, k).wait_send()

    return start, wait


def _grad_push_specs(grads):
    n = len(grads)
    return ([ANY] * n, [ANY] * n, [jax.ShapeDtypeStruct((N_DEV - 1,) + g.shape[2:], g.dtype) for g in grads],
            [pltpu.SemaphoreType.DMA((n, N_DEV - 1)), pltpu.SemaphoreType.DMA((n, N_DEV - 1))])


def add_eight(own, parts, jc_idx, name):
    _, half, c = parts.shape
    tr = half // 2 if (half // 2) % 16 == 0 else half

    def body(jc_ref, own_ref, p_ref, out_ref):
        acc = own_ref[0, 0].astype(f32)
        for k in range(N_DEV - 1):
            acc = acc + p_ref[k].astype(f32)
        out_ref[0] = acc

    return pl.pallas_call(
        body, name=name,
        grid_spec=pltpu.PrefetchScalarGridSpec(
            num_scalar_prefetch=1, grid=(half // tr,),
            in_specs=[pl.BlockSpec((1, 1, tr, c), lambda t, jc: (jc[0], jc[1], t, 0)),
                      pl.BlockSpec((N_DEV - 1, tr, c), lambda t, jc: (0, t, 0))],
            out_specs=pl.BlockSpec((1, tr, c), lambda t, jc: (jc[1], t, 0))),
        out_shape=jax.ShapeDtypeStruct((2, half, c), f32),
        compiler_params=_cparams(("parallel",)),
    )(jc_idx, own, parts)


def _push_specs(shards):
    n = len(shards)
    return ([ANY] * n, [ANY] * n, [jax.ShapeDtypeStruct((N_SHARD,) + s.shape, s.dtype) for s in shards],
            [pltpu.SemaphoreType.DMA((n, 3)), pltpu.SemaphoreType.DMA((n, 3)), pltpu.SemaphoreType.DMA((n,))])


def pair_exchange_halves(grads, small):
    n = len(grads)

    def body(*refs):
        ins, small_ref = refs[:n], refs[n]
        outs, gath = refs[n + 1:2 * n + 1], refs[2 * n + 1]
        send_sems, recv_sems, s_send, s_recv, local_sem = refs[2 * n + 2:]
        x, y, c = _me()
        me = 4 * x + 2 * y + c
        sends = []
        for w in range(n):
            half = ins[w].shape[1] // 2
            cp = pltpu.make_async_remote_copy(
                src_ref=ins[w].at[:, pl.ds((1 - c) * half, half), :], dst_ref=outs[w],
                send_sem=send_sems.at[w], recv_sem=recv_sems.at[w], device_id=(x, y, 1 - c), device_id_type=MESH)
            cp.start()
            sends.append(cp)
        loc = pltpu.make_async_copy(small_ref, gath.at[me], local_sem)
        loc.start()
        for k in range(1, N_DEV):
            cp = pltpu.make_async_remote_copy(
                src_ref=small_ref, dst_ref=gath.at[me], send_sem=s_send.at[k - 1], recv_sem=s_recv.at[k - 1],
                device_id=(x ^ (k >> 2), y ^ ((k >> 1) & 1), c ^ (k & 1)), device_id_type=MESH)
            cp.start()
            sends.append(cp)
        for w in range(n):
            half = ins[w].shape[1] // 2
            pltpu.make_async_remote_copy(
                src_ref=ins[w].at[:, pl.ds(0, half), :], dst_ref=outs[w], send_sem=send_sems.at[w],
                recv_sem=recv_sems.at[w], device_id=(x, y, 1 - c), device_id_type=MESH).wait_recv()
        for k in range(1, N_DEV):
            pltpu.make_async_remote_copy(
                src_ref=small_ref, dst_ref=gath.at[me ^ k], send_sem=s_send.at[k - 1], recv_sem=s_recv.at[k - 1],
                device_id=(x ^ (k >> 2), y ^ ((k >> 1) & 1), c ^ (k & 1)), device_id_type=MESH).wait_recv()
        for cp in sends:
            cp.wait_send()
        loc.wait()

    return pl.pallas_call(
        body, name="pair_exchange_halves", in_specs=[ANY] * (n + 1), out_specs=[ANY] * (n + 1),
        out_shape=[jax.ShapeDtypeStruct((g.shape[0], g.shape[1] // 2, g.shape[2]), f32) for g in grads]
        + [jax.ShapeDtypeStruct((N_DEV,) + small.shape, f32)],
        scratch_shapes=[pltpu.SemaphoreType.DMA((n,)), pltpu.SemaphoreType.DMA((n,)),
                        pltpu.SemaphoreType.DMA((N_DEV - 1,)), pltpu.SemaphoreType.DMA((N_DEV - 1,)),
                        pltpu.SemaphoreType.DMA],
    )(*grads, small)


def chip_exchange(sums):
    n = len(sums)

    def body(*refs):
        ins, outs = refs[:n], refs[n:2 * n]
        send_sems, recv_sems = refs[2 * n:]
        x, y, c = _me()
        j = 2 * x + y
        sends = []
        for w in range(n):
            for k in (1, 2, 3):
                cp = pltpu.make_async_remote_copy(
                    src_ref=ins[w].at[j ^ k], dst_ref=outs[w].at[k - 1], send_sem=send_sems.at[w, k - 1],
                    recv_sem=recv_sems.at[w, k - 1], device_id=_chip_peer(x, y, c, k), device_id_type=MESH)
                cp.start()
                sends.append(cp)
        for w in range(n):
            for k in (1, 2, 3):
                pltpu.make_async_remote_copy(
                    src_ref=ins[w].at[0], dst_ref=outs[w].at[k - 1], send_sem=send_sems.at[w, k - 1],
                    recv_sem=recv_sems.at[w, k - 1], device_id=_chip_peer(x, y, c, k), device_id_type=MESH).wait_recv()
        for cp in sends:
            cp.wait_send()

    return pl.pallas_call(
        body, name="chip_exchange", in_specs=[ANY] * n, out_specs=[ANY] * n,
        out_shape=[jax.ShapeDtypeStruct((N_SHARD - 1,) + s.shape[1:], s.dtype) for s in sums],
        scratch_shapes=[pltpu.SemaphoreType.DMA((n, 3)), pltpu.SemaphoreType.DMA((n, 3))],
    )(*sums)


def pair_exchange_results(halves):
    n = len(halves)

    def body(*refs):
        ins, outs = refs[:n], refs[n:2 * n]
        send_sems, recv_sems = refs[2 * n:]
        x, y, c = _me()
        sends = []
        for w in range(n):
            cp = pltpu.make_async_remote_copy(
                src_ref=ins[w].at[c], dst_ref=outs[w].at[c], send_sem=send_sems.at[w], recv_sem=recv_sems.at[w],
                device_id=(x, y, 1 - c), device_id_type=MESH)
            cp.start()
            sends.append(cp)
        for w in range(n):
            pltpu.make_async_remote_copy(
                src_ref=ins[w].at[c], dst_ref=outs[w].at[1 - c], send_sem=send_sems.at[w],
                recv_sem=recv_sems.at[w], device_id=(x, y, 1 - c), device_id_type=MESH).wait_recv()
        for cp in sends:
            cp.wait_send()

    return pl.pallas_call(
        body, name="pair_exchange_results", in_specs=[ANY] * n, out_specs=[ANY] * n,
        out_shape=[jax.ShapeDtypeStruct(h.shape, f32) for h in halves],
        input_output_aliases={w: w for w in range(n)},
        scratch_shapes=[pltpu.SemaphoreType.DMA((n,)), pltpu.SemaphoreType.DMA((n,))],
    )(*halves)


def add_pair(grad, other, c_idx, name):
    _, r, c = grad.shape
    half = r // 2
    tr = half // 2 if (half // 2) % 8 == 0 else half
    per = half // tr

    def body(c_ref, g_ref, o_ref, out_ref):
        out_ref[...] = (g_ref[...] + o_ref[...]).astype(bf16)

    return pl.pallas_call(
        body, name=name,
        grid_spec=pltpu.PrefetchScalarGridSpec(
            num_scalar_prefetch=1, grid=(N_SHARD, per),
            in_specs=[pl.BlockSpec((1, tr, c), lambda j, t, cr: (j, cr[0] * per + t, 0)),
                      pl.BlockSpec((1, tr, c), lambda j, t, cr: (j, t, 0))],
            out_specs=pl.BlockSpec((1, tr, c), lambda j, t, cr: (j, t, 0))),
        out_shape=jax.ShapeDtypeStruct((N_SHARD, half, c), bf16),
        compiler_params=_cparams(("parallel", "parallel")),
    )(c_idx, grad, other)


def add_four(own, parts, jc_idx, name):
    _, half, c = parts.shape
    tr = half // 2 if (half // 2) % 8 == 0 else half

    def body(jc_ref, own_ref, p_ref, out_ref):
        acc = own_ref[0].astype(f32)
        for k in range(N_SHARD - 1):
            acc = acc + p_ref[k].astype(f32)
        out_ref[0] = acc

    return pl.pallas_call(
        body, name=name,
        grid_spec=pltpu.PrefetchScalarGridSpec(
            num_scalar_prefetch=1, grid=(half // tr,),
            in_specs=[pl.BlockSpec((1, tr, c), lambda t, jc: (jc[0], t, 0)),
                      pl.BlockSpec((N_SHARD - 1, tr, c), lambda t, jc: (0, t, 0))],
            out_specs=pl.BlockSpec((1, tr, c), lambda t, jc: (jc[1], t, 0))),
        out_shape=jax.ShapeDtypeStruct((2, half, c), f32),
        compiler_params=_cparams(("parallel",)),
    )(jc_idx, own, parts)


def sum_devices(gathered):
    def body(g_ref, out_ref):
        acc = g_ref[0]
        for d in range(1, N_DEV):
            acc = acc + g_ref[d]
        out_ref[...] = acc

    return pl.pallas_call(body, name="sum_devices", out_shape=jax.ShapeDtypeStruct(gathered.shape[1:], f32))(gathered)


def _rows128(a, rows):
    flat = a.reshape(-1, BLOCK) if a.size % BLOCK == 0 else jnp.pad(a.reshape(1, -1), ((0, 0), (0, BLOCK - a.size)))
    return jnp.pad(flat, ((0, rows - flat.shape[0]), (0, 0)))


def kernel(x, meta_tokens, ln_emb_g, ln_emb_b, w_in, hg_lower_bounds, hg_norm_g, attn_sinks, w_branch_hg, w_branch_attn, w_out, ln1_g, ln1_b, w_ffn_in, w_ffn_out, ln2_g, ln2_b, loss_target, m_meta_tokens, m_ln_emb_g, m_ln_emb_b, m_w_in, m_hg_lower_bounds, m_hg_norm_g, m_attn_sinks, m_w_branch_hg, m_w_branch_attn, m_w_out, m_ln1_g, m_ln1_b, m_w_ffn_in, m_w_ffn_out, m_ln2_g, m_ln2_b, v_meta_tokens, v_ln_emb_g, v_ln_emb_b, v_w_in, v_hg_lower_bounds, v_hg_norm_g, v_attn_sinks, v_w_branch_hg, v_w_branch_attn, v_w_out, v_ln1_g, v_ln1_b, v_w_ffn_in, v_w_ffn_out, v_ln2_g, v_ln2_b):
    seq = x.shape[1]
    nb = seq // BLOCK + 1
    xs = x[0]
    ts = loss_target[0]
    ix, iy, ic = _me()
    shard = 2 * ix + iy
    vec = lambda a: a.reshape(1, D_MODEL)

    g_in, g_meta, s_bh, s_ba, s_out, s_fi, s_fo = gather_weights(
        [w_in[0], meta_tokens], [w_branch_hg[0], w_branch_attn[0], w_out[0], w_ffn_in[0], w_ffn_out[0]])
    by_cols = lambda g: g.transpose(1, 0, 2).reshape(g.shape[1], N_SHARD * g.shape[2])
    wf_in = by_cols(g_in)
    metablk = jnp.pad(by_cols(g_meta), ((PAD, 0), (0, 0)))

    pos = jnp.arange(nb * BLOCK, dtype=jnp.int32) - PAD
    half = HEAD_DIM // 2
    inv = ROPE_THETA ** (-jnp.arange(half, dtype=f32) / half)
    ang = pos.astype(f32)[:, None] * inv[None, :]
    cos = jnp.tile(jnp.cos(ang), (1, BLOCK // half))
    sin = jnp.tile(jnp.sin(ang), (1, BLOCK // half))
    sinks8 = jnp.broadcast_to(attn_sinks.reshape(ATT_HEADS, 1), (ATT_HEADS, BLOCK))
    ng = hg_norm_g.reshape(1, HG_K)

    h0, h0b, pa, pg = emb_inproj(xs, metablk, vec(ln_emb_g), vec(ln_emb_b), wf_in, nb)
    og, sprev, g_fi = hgrn_fwd(pa, hg_lower_bounds, ng, nb, [s_fi])
    oatt, g_fo, g_out, g_bh, g_ba = attn_fwd(pa, cos, sin, sinks8, nb, [s_fo, s_out, s_bh, s_ba])
    wf_bh, wf_ba, wf_fi = by_cols(g_bh), by_cols(g_ba), by_cols(g_fi)
    wf_out = g_out.reshape(D_MODEL, D_MODEL)
    wf_fo = g_fo.reshape(D_FF, D_MODEL)
    (dh0p, dpg, dog, doa, dyh, dya, mixin, dr1, h1b, dau, sact, dr2,
     loss_part, dg1, db1, dg2, db2) = mid_rows(h0, pg, og, oatt, ts, wf_bh, wf_ba, wf_out, wf_fi, wf_fo,
                                              ln1_g, ln1_b, ln2_g, ln2_b, nb)
    tp = BLOCK * (5 if nb % 5 == 0 else 1)
    pieces = lambda g: g.reshape(N_SHARD, 2, -1, g.shape[-1])
    gb_bh = pieces(wgrad(og, dyh, "wgrad_bh", 512, 256, tp, True, bf16))
    gb_ba = pieces(wgrad(oatt, dya, "wgrad_ba", 512, 256, tp, True, bf16))
    gb_out = pieces(wgrad(mixin, dr1, "wgrad_out", D_MODEL, D_MODEL, tp, False, bf16))
    gb_fi = pieces(wgrad(h1b, dau, "wgrad_fi", D_MODEL, 2 * D_FF // N_SHARD, tp, True, bf16))
    gb_fo = pieces(wgrad(sact, dr2, "wgrad_fo", D_FF // 2, D_MODEL, tp, False, bf16))
    dhq, dhf, dhi, dhg, dlb4, dng, r_fi, r_fo = hgrn_bwd(pa, hg_lower_bounds, ng, sprev, dog, nb, [gb_fi, gb_fo])
    daq, dkc, dkp, dvc, dvp, dkm, dvm, dsk, r_out, r_bh, r_ba = attn_bwd(pa, cos, sin, sinks8, doa, nb,
                                                                         [gb_out, gb_bh, gb_ba])
    dproj, dxp, dlg, dlb = inproj_bwd(dh0p, dhq, dhf, dhi, dhg, daq, dkc, dkp, dvc, dvp, dkm, dvm, dpg,
                                      wf_in, xs, metablk, vec(ln_emb_g), vec(ln_emb_b), nb)
    gw_in = wgrad(h0b, dproj, "wgrad_in", D_MODEL, IN_W // 2, tp, False)
    gw_in = gw_in.reshape(D_MODEL, N_SHARD, IN_W // N_SHARD).transpose(1, 0, 2)

    parts = [(dlg, 8), (dlb, 8), (dlb4, 8), (dng, 8), (dsk[:, 0], 8),
             (dg1, 8), (db1, 8), (dg2, 8), (db2, 8), (dxp[PAD:BLOCK], BLOCK)]
    small = jnp.concatenate([_rows128(a, r) for a, r in parts], axis=0)

    c_idx = jnp.reshape(ic, (1,)).astype(jnp.int32)
    jc_idx = jnp.stack([shard, ic]).astype(jnp.int32)
    other_in, gathered = pair_exchange_halves([gw_in], small)
    sum_in = add_pair(gw_in, other_in, c_idx, "add_pair_in")
    quad_in, = chip_exchange([sum_in])
    halves = [add_four(sum_in, quad_in, jc_idx, "add_four_in")]
    halves += [add_eight(g, r, jc_idx, "add_eight_" + nm) for nm, g, r in
               (("bh", gb_bh, r_bh), ("ba", gb_ba, r_ba), ("out", gb_out, r_out), ("fi", gb_fi, r_fi),
                ("fo", gb_fo, r_fo))]
    red = [r.reshape(-1, r.shape[-1]) for r in pair_exchange_results(halves)]
    small_sum = sum_devices(gathered)

    offs, acc = [], 0
    for _, r in parts:
        offs.append(acc)
        acc += r
    take = lambda n, size: small_sum[offs[n]:offs[n] + parts[n][1]].reshape(-1)[:size]
    g_meta_full = take(9, N_META * D_MODEL).reshape(N_META, D_MODEL)
    g_small = {
        "meta_tokens": lax.dynamic_slice_in_dim(g_meta_full, shard * (D_MODEL // N_SHARD), D_MODEL // N_SHARD, axis=1),
        "ln_emb_g": take(0, D_MODEL), "ln_emb_b": take(1, D_MODEL),
        "hg_lower_bounds": take(2, 2 * HG_HEADS * HG_K).reshape(2, HG_HEADS * HG_K),
        "hg_norm_g": take(3, HG_K).reshape(1, HG_K), "attn_sinks": take(4, ATT_HEADS).reshape(1, ATT_HEADS),
        "ln1_g": take(5, D_MODEL).reshape(1, D_MODEL), "ln1_b": take(6, D_MODEL).reshape(1, D_MODEL),
        "ln2_g": take(7, D_MODEL).reshape(1, D_MODEL), "ln2_b": take(8, D_MODEL).reshape(1, D_MODEL),
    }
    g_big = {"w_in": red[0], "w_branch_hg": red[1], "w_branch_attn": red[2], "w_out": red[3],
             "w_ffn_in": red[4], "w_ffn_out": red[5]}

    names = ["meta_tokens", "ln_emb_g", "ln_emb_b", "w_in", "hg_lower_bounds", "hg_norm_g", "attn_sinks",
             "w_branch_hg", "w_branch_attn", "w_out", "ln1_g", "ln1_b", "w_ffn_in", "w_ffn_out", "ln2_g", "ln2_b"]
    given = dict(
        meta_tokens=(meta_tokens, m_meta_tokens, v_meta_tokens), ln_emb_g=(ln_emb_g, m_ln_emb_g, v_ln_emb_g),
        ln_emb_b=(ln_emb_b, m_ln_emb_b, v_ln_emb_b), w_in=(w_in, m_w_in, v_w_in),
        hg_lower_bounds=(hg_lower_bounds, m_hg_lower_bounds, v_hg_lower_bounds),
        hg_norm_g=(hg_norm_g, m_hg_norm_g, v_hg_norm_g), attn_sinks=(attn_sinks, m_attn_sinks, v_attn_sinks),
        w_branch_hg=(w_branch_hg, m_w_branch_hg, v_w_branch_hg),
        w_branch_attn=(w_branch_attn, m_w_branch_attn, v_w_branch_attn), w_out=(w_out, m_w_out, v_w_out),
        ln1_g=(ln1_g, m_ln1_g, v_ln1_g), ln1_b=(ln1_b, m_ln1_b, v_ln1_b), w_ffn_in=(w_ffn_in, m_w_ffn_in, v_w_ffn_in),
        w_ffn_out=(w_ffn_out, m_w_ffn_out, v_w_ffn_out), ln2_g=(ln2_g, m_ln2_g, v_ln2_g), ln2_b=(ln2_b, m_ln2_b, v_ln2_b))
    out_g, out_d, out_m, out_v = [], [], [], []
    for nm in names:
        w, m, v = given[nm]
        shape = w.shape
        g = g_big[nm] if nm in g_big else g_small[nm]
        two_d = (lambda a: a.reshape(8, BLOCK)) if w.ndim == 1 else (lambda a: a.reshape(a.shape[-2], a.shape[-1]))
        d, mn, vn = adamw(two_d(w), two_d(g), two_d(m), two_d(v), "adamw_" + nm)
        out_g.append(g.reshape(shape))
        out_d.append(d.reshape(shape))
        out_m.append(mn.reshape(shape))
        out_v.append(vn.reshape(shape))

    loss = lax.psum(loss_part[0, 0], ("x", "y", "c"))
    grad_x = dxp[BLOCK:].reshape(x.shape)
    return (loss, grad_x, *out_g, *out_d, *out_m, *out_v)
```

```python
import functools

import jax
import jax.numpy as jnp
from jax import lax
from jax.experimental import pallas as pl
from jax.experimental.pallas import tpu as pltpu

f32 = jnp.float32
bf16 = jnp.bfloat16

D_MODEL = 1024
BLOCK = 128
N_META = 16
PAD = BLOCK - N_META
HG_HEADS = 4
HG_K = 128
SUB = 16
ATT_HEADS = 8
HEAD_DIM = 64
ATT_QW = ATT_HEADS * HEAD_DIM
D_FF = 2816
EPS = 1e-5
ALPHA = 2.0 ** 0.25
ROPE_THETA = 10000.0
N_A = 2816
N_G = 2048
IN_W = N_A + N_G
N_SHARD = 4
N_DEV = 8

ADAM_LR = 0.001
ADAM_B1 = 0.9
ADAM_B2 = 0.999
ADAM_EPS = 1e-08
ADAM_WD = 0.01
ADAM_STEP = 10

TM = 256
LEAD = TM // BLOCK - 1

VMEM_LIMIT = 56 * 1024 * 1024
MESH = pl.DeviceIdType.MESH


def _cparams(sem, vmem=VMEM_LIMIT):
    return pltpu.CompilerParams(dimension_semantics=sem, vmem_limit_bytes=vmem)


def _const_spec(shape):
    zeros = (0,) * len(shape)
    return pl.BlockSpec(shape, lambda *_: zeros, pipeline_mode=pl.Buffered(1))


def _dot(a, b, ca, cb):
    return lax.dot_general(a.astype(bf16), b.astype(bf16), (((ca,), (cb,)), ((), ())),
                           preferred_element_type=f32)


@jax.custom_vjp
def mm(a, b):
    return _dot(a, b, 1, 0)


mm.defvjp(lambda a, b: (_dot(a, b, 1, 0), (a, b)),
          lambda r, g: (_dot(g, r[1], 1, 1), _dot(r[0], g, 0, 0)))


@jax.custom_vjp
def mm_nt(a, b):
    return _dot(a, b, 1, 1)


mm_nt.defvjp(lambda a, b: (_dot(a, b, 1, 1), (a, b)),
             lambda r, g: (_dot(g, r[1], 1, 0), _dot(g, r[0], 0, 0)))


@jax.custom_vjp
def mm_tn(a, b):
    return _dot(a, b, 0, 0)


mm_tn.defvjp(lambda a, b: (_dot(a, b, 0, 0), (a, b)),
             lambda r, g: (_dot(r[1], g, 1, 1), _dot(r[0], g, 1, 0)))


@functools.partial(jax.custom_vjp, nondiff_argnums=(1,))
def roll_lanes(x, shift):
    return pltpu.roll(x, shift, 1)


roll_lanes.defvjp(lambda x, shift: (pltpu.roll(x, shift, 1), None),
                  lambda shift, _, g: (pltpu.roll(g, (128 - shift) % 128, 1),))


def _sigmoid(x):
    return 1.0 / (1.0 + jnp.exp(-x))


def _ln_stats(x):
    mu = jnp.mean(x, axis=-1, keepdims=True)
    xc = x - mu
    var = jnp.mean(xc * xc, axis=-1, keepdims=True)
    rs = lax.rsqrt(var + EPS)
    return xc * rs, rs


def _ln_bwd(dy, xh, rs, g):
    dxh = dy * g
    m1 = jnp.mean(dxh, axis=-1, keepdims=True)
    m2 = jnp.mean(dxh * xh, axis=-1, keepdims=True)
    return rs * (dxh - m1 - xh * m2)


def _row_ids(i):
    return i * BLOCK + lax.broadcasted_iota(jnp.int32, (BLOCK, 1), 0)


def _tm_rows(i):
    return i * TM + lax.broadcasted_iota(jnp.int32, (TM, 1), 0)


def _tm_row(n):
    return pl.BlockSpec((TM, n), lambda i: (i, 0))


def _tm_tokens():
    return pl.BlockSpec((TM, D_MODEL), lambda i: (jnp.maximum(i - 1, 0), 0))


def emb_inproj(x, metablk, g, b, w_in):
    nsteps = x.shape[0] // TM + 1

    def body(x_ref, mb_ref, g_ref, b_ref, w_ref, h0_ref, h0b_ref, pa_ref, pg_ref):
        i = pl.program_id(0)
        xb = jnp.where(i == 0, mb_ref[...], x_ref[...])
        xh, _ = _ln_stats(xb)
        y = xh * g_ref[...] + b_ref[...]
        y = jnp.where(_tm_rows(i) >= TM - N_META, y, 0.0)
        h0_ref[...] = y
        yb = y.astype(bf16)
        h0b_ref[...] = yb
        pa_ref[...] = jnp.dot(yb, w_ref[:, :N_A], preferred_element_type=f32)
        pg_ref[...] = jnp.dot(yb, w_ref[:, N_A:], preferred_element_type=f32)

    p = nsteps * TM
    row = _tm_row
    return pl.pallas_call(
        body, name="emb_inproj", grid=(nsteps,),
        in_specs=[_tm_tokens(),
                  _const_spec((TM, D_MODEL)), _const_spec((1, D_MODEL)), _const_spec((1, D_MODEL)),
                  _const_spec((D_MODEL, IN_W))],
        out_specs=[row(D_MODEL), row(D_MODEL), row(N_A), row(N_G)],
        out_shape=[jax.ShapeDtypeStruct((p, D_MODEL), f32), jax.ShapeDtypeStruct((p, D_MODEL), bf16),
                   jax.ShapeDtypeStruct((p, N_A), f32), jax.ShapeDtypeStruct((p, N_G), f32)],
        compiler_params=_cparams(("parallel",)),
    )(x, metablk, g, b, w_in)


def _hgrn_chunk(valid, st, hq, hf, hi, hg, lbraw, ng):
    lb = _sigmoid(lbraw[0:1] - lbraw[1:2])
    q = hq * _sigmoid(hq)
    fg = lb + (1.0 - lb) * _sigmoid(hf)
    logf = jnp.where(valid, jnp.log(fg), 0.0)
    k = jnp.where(valid, 1.0 - fg, 0.0)
    v = hi
    r = lax.broadcasted_iota(jnp.int32, (BLOCK, BLOCK), 0)
    c = lax.broadcasted_iota(jnp.int32, (BLOCK, BLOCK), 1)
    tril = (c <= r).astype(f32)
    bcum = jnp.dot(tril, logf, precision=lax.Precision.HIGHEST, preferred_element_type=f32)
    blast = bcum[BLOCK - 1:BLOCK]
    rows = lax.broadcasted_iota(jnp.int32, (BLOCK, 1), 0)
    sub8 = lax.broadcasted_iota(jnp.int32, (BLOCK // 8, 8, HG_K), 1)
    b8 = bcum.reshape(BLOCK // 8, 8, HG_K)
    row_of_8 = lambda j: jnp.broadcast_to(b8[:, j:j + 1, :], b8.shape)
    a = jnp.where(r == c, jnp.sum(q * k, axis=-1, keepdims=True), 0.0)
    seg = BLOCK
    while seg >= 2:
        half = seg // 2
        if seg >= 8:
            bs = bcum.reshape(BLOCK // seg, seg, HG_K)
            ref = jnp.broadcast_to(bs[:, half - 1:half, :], bs.shape)
        elif seg == 4:
            ref = jnp.where(sub8 < 4, row_of_8(1), row_of_8(5))
        else:
            ref = jnp.where(sub8 < 2, row_of_8(0), jnp.where(sub8 < 4, row_of_8(2),
                                                             jnp.where(sub8 < 6, row_of_8(4), row_of_8(6))))
        ref = ref.reshape(BLOCK, HG_K)
        upper = (rows % seg) >= half
        q_up = q * jnp.exp(jnp.where(upper, bcum - ref, -jnp.inf))
        k_lo = k * jnp.exp(jnp.where(upper, -jnp.inf, ref - bcum))
        a = a + jnp.where((r // seg) == (c // seg), mm_nt(q_up, k_lo), 0.0)
        seg = half
    o = mm_nt(q * jnp.exp(bcum), st) + mm(a, v)
    st_new = st * jnp.exp(blast) + mm_tn(v, k * jnp.exp(blast - bcum))
    on = o * lax.rsqrt(jnp.mean(o * o, axis=-1, keepdims=True) + EPS) * ng
    return st_new, on * (hg * _sigmoid(hg))


def _hgrn_in_specs(rowmap):
    wide = lambda col: pl.BlockSpec((BLOCK, HG_HEADS * HG_K), lambda i: (rowmap(i) + LEAD, col))
    return [wide(0), wide(1), wide(2), wide(3), _const_spec((2, HG_HEADS * HG_K)), _const_spec((1, HG_K))]


def _head(ref, h):
    return ref[:, h * HG_K:(h + 1) * HG_K]


def hgrn_fwd(pa, lbraw, ng, nb, shards):
    n = len(shards)

    def body(hq_ref, hf_ref, hi_ref, hg_ref, lb_ref, ng_ref, *rest):
        srcs, (og_ref, sp_ref), dsts = rest[:n], rest[n:n + 2], rest[n + 2:2 * n + 2]
        st_ref = rest[2 * n + 2]
        start, wait = _shard_push(srcs, dsts, *rest[2 * n + 3:])
        i = pl.program_id(0)

        @pl.when(i == 0)
        def _():
            st_ref[...] = jnp.zeros_like(st_ref)
            start()

        @pl.when(i == nb - 1)
        def _():
            wait()

        valid = _row_ids(i) >= PAD
        for h in range(HG_HEADS):
            st = st_ref[h]
            sp_ref[0, h] = st
            st_new, out = _hgrn_chunk(valid, st, _head(hq_ref, h), _head(hf_ref, h), _head(hi_ref, h),
                                      _head(hg_ref, h), _head(lb_ref, h), ng_ref[...])
            st_ref[h] = st_new
            og_ref[:, h * HG_K:(h + 1) * HG_K] = out.astype(bf16)

    p = (nb + LEAD) * BLOCK
    push_in, push_out, push_shape, push_scratch = _push_specs(shards)
    return pl.pallas_call(
        body, name="hgrn_fwd", grid=(nb,),
        in_specs=_hgrn_in_specs(lambda i: i) + push_in,
        out_specs=[pl.BlockSpec((BLOCK, HG_HEADS * HG_K), lambda i: (i + LEAD, 0)),
                   pl.BlockSpec((1, HG_HEADS, HG_K, HG_K), lambda i: (i, 0, 0, 0))] + push_out,
        out_shape=[jax.ShapeDtypeStruct((p, HG_HEADS * HG_K), bf16),
                   jax.ShapeDtypeStruct((nb, HG_HEADS, HG_K, HG_K), f32)] + push_shape,
        scratch_shapes=[pltpu.VMEM((HG_HEADS, HG_K, HG_K), f32)] + push_scratch,
        compiler_params=_cparams(("arbitrary",)),
    )(pa, pa, pa, pa, lbraw, ng, *shards)


def hgrn_bwd(pa, lbraw, ng, sprev, dog, nb, grads):
    n = len(grads)

    def body(hq_ref, hf_ref, hi_ref, hg_ref, lb_ref, ng_ref, sp_ref, do_ref, *rest):
        srcs, rest = rest[:n], rest[n:]
        dq_ref, df_ref, di_ref, dg_ref, dlb_ref, dng_ref = rest[:6]
        dsts, dst_ref = rest[6:6 + n], rest[6 + n]
        start, wait = _grad_push(srcs, dsts, *rest[7 + n:])
        i = pl.program_id(0)

        @pl.when(i == 0)
        def _():
            dst_ref[...] = jnp.zeros_like(dst_ref)
            dlb_ref[...] = jnp.zeros_like(dlb_ref)
            dng_ref[...] = jnp.zeros_like(dng_ref)
            start()

        valid = _row_ids(nb - 1 - i) >= PAD
        dng_sum = jnp.zeros((1, HG_K), f32)
        for h in range(HG_HEADS):
            cols = slice(h * HG_K, (h + 1) * HG_K)
            _, vjp = jax.vjp(functools.partial(_hgrn_chunk, valid), sp_ref[0, h], _head(hq_ref, h), _head(hf_ref, h),
                             _head(hi_ref, h), _head(hg_ref, h), _head(lb_ref, h), ng_ref[...])
            dst, dq, df, di, dg, dlb, dng = vjp((dst_ref[h], _head(do_ref, h)))
            dst_ref[h] = dst
            dq_ref[:, cols] = dq.astype(bf16)
            df_ref[:, cols] = df.astype(bf16)
            di_ref[:, cols] = di.astype(bf16)
            dg_ref[:, cols] = dg.astype(bf16)
            dlb_ref[:, cols] += dlb
            dng_sum = dng_sum + dng
        dng_ref[...] += dng_sum
        pl.when(i == nb - 1)(wait)

    p = (nb + LEAD) * BLOCK
    rev = lambda i: nb - 1 - i
    hw = HG_HEADS * HG_K
    blk = pl.BlockSpec((BLOCK, hw), lambda i: (rev(i) + LEAD, 0))
    wide = jax.ShapeDtypeStruct((p, hw), bf16)
    push_in, push_out, push_shape, push_scratch = _grad_push_specs(grads)
    return pl.pallas_call(
        body, name="hgrn_bwd", grid=(nb,),
        in_specs=_hgrn_in_specs(rev) + [pl.BlockSpec((1, HG_HEADS, HG_K, HG_K), lambda i: (rev(i), 0, 0, 0)), blk]
        + push_in,
        out_specs=[blk, blk, blk, blk, pl.BlockSpec((2, hw), lambda i: (0, 0)), pl.BlockSpec((1, HG_K), lambda i: (0, 0))]
        + push_out,
        out_shape=[wide, wide, wide, wide, jax.ShapeDtypeStruct((2, hw), f32), jax.ShapeDtypeStruct((1, HG_K), f32)]
        + push_shape,
        scratch_shapes=[pltpu.VMEM((HG_HEADS, HG_K, HG_K), f32)] + push_scratch,
        compiler_params=_cparams(("arbitrary",)),
    )(pa, pa, pa, pa, lbraw, ng, sprev, dog, *grads)


def _rope(x, cos, sin):
    lane = lax.broadcasted_iota(jnp.int32, x.shape, 1)
    rot = jnp.where(lane % HEAD_DIM < HEAD_DIM // 2, -roll_lanes(x, BLOCK - HEAD_DIM // 2),
                    roll_lanes(x, HEAD_DIM // 2))
    return x * cos + rot * sin


def _both_halves(x, g):
    lo = lax.broadcasted_iota(jnp.int32, x.shape, 1) < HEAD_DIM
    sw = roll_lanes(x, HEAD_DIM)
    return jnp.where(lo, x, sw) if g == 0 else jnp.where(lo, sw, x)


def _attn_block(band_ok, meta_ok, tabs, q, kp, kc, vp, vc, km, vm, *sinks):
    cq, sq, cp, sp, cm, sm = tabs
    neg = jnp.finfo(f32).min
    scale = HEAD_DIM ** -0.5
    group = ATT_HEADS // 2
    kpr, kcr, kmr = _rope(kp, cp, sp), _rope(kc, cq, sq), _rope(km, cm, sm)
    lo = lax.broadcasted_iota(jnp.int32, (BLOCK, BLOCK), 1) < HEAD_DIM
    t = lax.broadcasted_iota(jnp.int32, (group * BLOCK, BLOCK), 0) % BLOCK
    own = lax.broadcasted_iota(jnp.int32, (group * BLOCK, BLOCK), 1) <= t
    qr = [_rope(q[:, m * BLOCK:(m + 1) * BLOCK], cq, sq) for m in range(ATT_HEADS // 2)]
    slabs = []
    for g in range(2):
        kp_g, kc_g, vp_g, vc_g, km_g, vm_g = [_both_halves(a, g) for a in (kpr, kcr, vp, vc, kmr, vm)]
        qs = jnp.concatenate([jnp.where(lo if h % 2 == 0 else ~lo, qr[2 * g + h // 2], 0.0) for h in range(group)],
                             axis=0)
        sink = jnp.concatenate([jnp.broadcast_to(sinks[group * g + h], (BLOCK, 1)) for h in range(group)], axis=0)
        sb = jnp.where(band_ok, jnp.where(own, mm_nt(qs, kc_g), mm_nt(qs, kp_g)) * scale, neg)
        sme = jnp.where(meta_ok, mm_nt(qs, km_g) * scale, neg)
        mx = lax.stop_gradient(jnp.maximum(jnp.maximum(jnp.max(sb, axis=-1, keepdims=True),
                                                       jnp.max(sme, axis=-1, keepdims=True)), sink))
        eb, em = jnp.exp(sb - mx), jnp.exp(sme - mx)
        inv = 1.0 / (jnp.sum(eb, axis=-1, keepdims=True) + jnp.sum(em, axis=-1, keepdims=True) + jnp.exp(sink - mx))
        pb = eb * inv
        o = mm(jnp.where(own, pb, 0.0), vc_g) + mm(jnp.where(own, 0.0, pb), vp_g) + mm(em * inv, vm_g)
        for m in range(2):
            slabs.append(jnp.where(lo, o[2 * m * BLOCK:(2 * m + 1) * BLOCK], o[(2 * m + 1) * BLOCK:(2 * m + 2) * BLOCK]))
    return jnp.concatenate(slabs, axis=1)


def _attn_masks(i):
    group = ATT_HEADS // 2
    t = lax.broadcasted_iota(jnp.int32, (group * BLOCK, BLOCK), 0) % BLOCK
    s = lax.broadcasted_iota(jnp.int32, (group * BLOCK, BLOCK), 1)
    kpos = jnp.where(s <= t, i * BLOCK - PAD + s, jnp.where(i > 0, (i - 1) * BLOCK - PAD + s, -1))
    band_ok = kpos >= N_META
    qpos = i * BLOCK - PAD + lax.broadcasted_iota(jnp.int32, (group * BLOCK, 1), 0) % BLOCK
    meta_ok = lax.broadcasted_iota(jnp.int32, (1, N_META), 1) <= qpos
    return band_ok, meta_ok


def _attn_in_specs():
    prev = lambda i: jnp.maximum(i - 1, 0)
    kcol, vcol = N_A // BLOCK - 2, N_A // BLOCK - 1
    blk = lambda rowmap, col: pl.BlockSpec((BLOCK, BLOCK), lambda i: (rowmap(i) + LEAD, col))
    tab = lambda rowmap: pl.BlockSpec((BLOCK, BLOCK), lambda i: (rowmap(i), 0))
    cur, first = (lambda i: i), (lambda i: 0)
    return [pl.BlockSpec((BLOCK, ATT_QW), lambda i: (i + LEAD, 4)),
            blk(prev, kcol), blk(cur, kcol), blk(prev, vcol), blk(cur, vcol), blk(first, kcol), blk(first, vcol),
            tab(cur), tab(cur), tab(prev), tab(prev), tab(first), tab(first),
            _const_spec((ATT_HEADS, BLOCK))]


def _attn_row(n):
    return pl.BlockSpec((BLOCK, n), lambda i: (i + LEAD, 0))


def _attn_operands(q_ref, kp_ref, kc_ref, vp_ref, vc_ref, km_ref, vm_ref, cq, sq, cp, sp, cm, sm, sk_ref):
    tabs = (cq[...], sq[...], cp[...], sp[...], cm[PAD:, :], sm[PAD:, :])
    args = (q_ref[...], kp_ref[...], kc_ref[...], vp_ref[...], vc_ref[...], km_ref[PAD:, :], vm_ref[PAD:, :])
    sinks = tuple(sk_ref[j:j + 1, 0:1] for j in range(ATT_HEADS))
    return tabs, args + sinks


def attn_fwd(pa, cos, sin, sinks8, nb, shards):
    n = len(shards)
    n_in = 14

    def body(*refs):
        srcs, o_ref, dsts = refs[n_in:n_in + n], refs[n_in + n], refs[n_in + n + 1:n_in + 2 * n + 1]
        start, wait = _shard_push(srcs, dsts, *refs[n_in + 2 * n + 1:])
        i = pl.program_id(0)
        pl.when(i == 0)(start)
        band_ok, meta_ok = _attn_masks(i)
        tabs, args = _attn_operands(*refs[:n_in])
        o_ref[...] = _attn_block(band_ok, meta_ok, tabs, *args).astype(bf16)
        pl.when(i == nb - 1)(wait)

    push_in, push_out, push_shape, push_scratch = _push_specs(shards)
    return pl.pallas_call(
        body, name="attn_fwd", grid=(nb,), in_specs=_attn_in_specs() + push_in,
        out_specs=[_attn_row(ATT_QW)] + push_out,
        out_shape=[jax.ShapeDtypeStruct(((nb + LEAD) * BLOCK, ATT_QW), bf16)] + push_shape,
        scratch_shapes=push_scratch,
        compiler_params=_cparams(("arbitrary",)),
    )(pa, pa, pa, pa, pa, pa, pa, cos, sin, cos, sin, cos, sin, sinks8, *shards)


def attn_bwd(pa, cos, sin, sinks8, do, nb, grads):
    n = len(grads)

    def body(*refs):
        do_ref, srcs = refs[14], refs[15:15 + n]
        dq_ref, dkc_ref, dkp_ref, dvc_ref, dvp_ref, dkm_ref, dvm_ref, dsk_ref = refs[15 + n:23 + n]
        start, wait = _grad_push(srcs, refs[23 + n:23 + 2 * n], *refs[23 + 2 * n:])
        i = pl.program_id(0)

        @pl.when(i == 0)
        def _():
            dkm_ref[...] = jnp.zeros((N_META, BLOCK), f32)
            dvm_ref[...] = jnp.zeros((N_META, BLOCK), f32)
            dsk_ref[...] = jnp.zeros((ATT_HEADS, BLOCK), f32)
            start()

        band_ok, meta_ok = _attn_masks(i)
        tabs, args = _attn_operands(*refs[:14])
        _, vjp = jax.vjp(functools.partial(_attn_block, band_ok, meta_ok, tabs), *args)
        grads = vjp(do_ref[...])
        dq_ref[...] = grads[0].astype(bf16)
        dkp_ref[...] = grads[1]
        dkc_ref[...] = grads[2]
        dvp_ref[...] = grads[3]
        dvc_ref[...] = grads[4]
        dkm_ref[...] += grads[5]
        dvm_ref[...] += grads[6]
        for j in range(ATT_HEADS):
            dsk_ref[j:j + 1, :] += jnp.broadcast_to(grads[7 + j], (1, BLOCK))
        pl.when(i == nb - 1)(wait)

    p = (nb + LEAD) * BLOCK
    row = _attn_row(BLOCK)
    const = lambda r: pl.BlockSpec((r, BLOCK), lambda i: (0, 0))
    part = jax.ShapeDtypeStruct((p, BLOCK), f32)
    push_in, push_out, push_shape, push_scratch = _grad_push_specs(grads)
    return pl.pallas_call(
        body, name="attn_bwd", grid=(nb,),
        in_specs=_attn_in_specs() + [_attn_row(ATT_QW)] + push_in,
        out_specs=[_attn_row(ATT_QW), row, row, row, row,
                   const(N_META), const(N_META), const(ATT_HEADS)] + push_out,
        out_shape=[jax.ShapeDtypeStruct((p, ATT_QW), bf16), part, part, part, part,
                   jax.ShapeDtypeStruct((N_META, BLOCK), f32), jax.ShapeDtypeStruct((N_META, BLOCK), f32),
                   jax.ShapeDtypeStruct((ATT_HEADS, BLOCK), f32)] + push_shape,
        scratch_shapes=push_scratch,
        compiler_params=_cparams(("arbitrary",)),
    )(pa, pa, pa, pa, pa, pa, pa, cos, sin, cos, sin, cos, sin, sinks8, do, *grads)


def _mid_forward(h0_ref, pg_ref, og, oa, wbh_ref, wba_ref, wo_ref, g1, b1):
    yh = jnp.dot(og, wbh_ref[...], preferred_element_type=f32)
    ya = jnp.dot(oa, wba_ref[...], preferred_element_type=f32)
    gh = _sigmoid(pg_ref[:, :D_MODEL])
    ga = _sigmoid(pg_ref[:, D_MODEL:])
    mixin = (gh * yh + ga * ya).astype(bf16)
    r1 = ALPHA * h0_ref[...] + jnp.dot(mixin, wo_ref[...], preferred_element_type=f32)
    xh1, rs1 = _ln_stats(r1)
    return yh, ya, gh, ga, mixin, xh1, rs1, xh1 * g1 + b1


def _mid_weight_specs():
    hw = HG_HEADS * HG_K
    return [_const_spec((hw, D_MODEL)), _const_spec((ATT_QW, D_MODEL)), _const_spec((D_MODEL, D_MODEL)),
            _const_spec((1, D_MODEL)), _const_spec((1, D_MODEL))]


def mid_front(h0, pg, og, oatt, wbh, wba, wout, ln1g, ln1b):
    def body(h0_ref, pg_ref, og_ref, oa_ref, wbh_ref, wba_ref, wo_ref, g1_ref, b1_ref,
             h1_ref, h1b_ref, mix_ref, ogc_ref, oac_ref):
        used = _tm_rows(pl.program_id(0)) >= LEAD * BLOCK
        og = jnp.where(used, og_ref[...], jnp.zeros_like(og_ref))
        oa = jnp.where(used, oa_ref[...], jnp.zeros_like(oa_ref))
        ogc_ref[...] = og
        oac_ref[...] = oa
        *_, mixin, _, _, h1 = _mid_forward(h0_ref, pg_ref, og, oa, wbh_ref, wba_ref, wo_ref, g1_ref[...], b1_ref[...])
        mix_ref[...] = mixin
        h1_ref[...] = h1
        h1b_ref[...] = h1.astype(bf16)

    p = h0.shape[0]
    hw = HG_HEADS * HG_K
    sds = lambda n, dt: jax.ShapeDtypeStruct((p, n), dt)
    return pl.pallas_call(
        body, name="mid_front", grid=(p // TM,),
        in_specs=[_tm_row(D_MODEL), _tm_row(N_G), _tm_row(hw), _tm_row(ATT_QW)] + _mid_weight_specs(),
        out_specs=[_tm_row(D_MODEL), _tm_row(D_MODEL), _tm_row(D_MODEL), _tm_row(hw), _tm_row(ATT_QW)],
        out_shape=[sds(D_MODEL, f32), sds(D_MODEL, bf16), sds(D_MODEL, bf16), sds(hw, bf16), sds(ATT_QW, bf16)],
        compiler_params=_cparams(("parallel",)),
    )(h0, pg, og, oatt, wbh, wba, wout, ln1g, ln1b)


def mid_ffn(h1, target, wfi, wfo, ln2g, ln2b):
    def body(h1_ref, t_ref, wfi_ref, wfo_ref, g2_ref, b2_ref,
             dh1_ref, dau_ref, s_ref, dr2_ref, loss_ref, dg2_ref, db2_ref):
        i = pl.program_id(0)

        @pl.when(i == 0)
        def _():
            for r in (loss_ref, dg2_ref, db2_ref):
                r[...] = jnp.zeros_like(r)

        g2, b2 = g2_ref[...], b2_ref[...]
        h1 = h1_ref[...]
        au = jnp.dot(h1.astype(bf16), wfi_ref[...], preferred_element_type=f32)
        a, u = au[:, :D_FF], au[:, D_FF:]
        sg = _sigmoid(a)
        sa = a * sg
        s = (sa * u).astype(bf16)
        s_ref[...] = s
        r2 = ALPHA * h1 + jnp.dot(s, wfo_ref[...], preferred_element_type=f32)
        xh2, rs2 = _ln_stats(r2)
        diff = jnp.where(i > 0, xh2 * g2 + b2 - t_ref[...], 0.0)
        loss_ref[...] += jnp.sum(diff * diff) * (0.5 / D_MODEL)
        dy = diff * (1.0 / D_MODEL)
        dg2_ref[...] += jnp.sum(dy * xh2, axis=0, keepdims=True)
        db2_ref[...] += jnp.sum(dy, axis=0, keepdims=True)
        dr2 = _ln_bwd(dy, xh2, rs2, g2)
        dr2b = dr2.astype(bf16)
        dr2_ref[...] = dr2b
        ds = _dot(dr2b, wfo_ref[...], 1, 1)
        da = (ds * u) * (sg * (1.0 + a * (1.0 - sg)))
        du = ds * sa
        dau = jnp.concatenate([da, du], axis=1).astype(bf16)
        dau_ref[...] = dau
        dh1_ref[...] = ALPHA * dr2 + _dot(dau, wfi_ref[...], 1, 1)

    p = h1.shape[0]
    vec = lambda: pl.BlockSpec((1, D_MODEL), lambda i: (0, 0))
    sds = lambda n, dt: jax.ShapeDtypeStruct((p, n), dt)
    return pl.pallas_call(
        body, name="mid_ffn", grid=(p // TM,),
        in_specs=[_tm_row(D_MODEL), _tm_tokens(), _const_spec((D_MODEL, 2 * D_FF)), _const_spec((D_FF, D_MODEL)),
                  _const_spec((1, D_MODEL)), _const_spec((1, D_MODEL))],
        out_specs=[_tm_row(D_MODEL), _tm_row(2 * D_FF), _tm_row(D_FF), _tm_row(D_MODEL),
                   pl.BlockSpec((1, 1), lambda i: (0, 0)), vec(), vec()],
        out_shape=[sds(D_MODEL, f32), sds(2 * D_FF, bf16), sds(D_FF, bf16), sds(D_MODEL, bf16),
                   jax.ShapeDtypeStruct((1, 1), f32)] + [jax.ShapeDtypeStruct((1, D_MODEL), f32)] * 2,
        compiler_params=_cparams(("arbitrary",)),
    )(h1, target, wfi, wfo, ln2g, ln2b)


def mid_back(dh1, h0, pg, ogc, oac, wbh, wba, wout, ln1g, ln1b):
    def body(dh1_ref, h0_ref, pg_ref, og_ref, oa_ref, wbh_ref, wba_ref, wo_ref, g1_ref, b1_ref,
             dh0_ref, dpg_ref, dog_ref, doa_ref, dyh_ref, dya_ref, dr1_ref, dg1_ref, db1_ref):
        @pl.when(pl.program_id(0) == 0)
        def _():
            dg1_ref[...] = jnp.zeros_like(dg1_ref)
            db1_ref[...] = jnp.zeros_like(db1_ref)

        g1 = g1_ref[...]
        yh, ya, gh, ga, _, xh1, rs1, _ = _mid_forward(h0_ref, pg_ref, og_ref[...], oa_ref[...], wbh_ref, wba_ref,
                                                      wo_ref, g1, b1_ref[...])
        dh1 = dh1_ref[...]
        dg1_ref[...] += jnp.sum(dh1 * xh1, axis=0, keepdims=True)
        db1_ref[...] += jnp.sum(dh1, axis=0, keepdims=True)
        dr1 = _ln_bwd(dh1, xh1, rs1, g1)
        dr1b = dr1.astype(bf16)
        dr1_ref[...] = dr1b
        dh0_ref[...] = ALPHA * dr1
        dmix = _dot(dr1b, wo_ref[...], 1, 1)
        dyh = (dmix * gh).astype(bf16)
        dya = (dmix * ga).astype(bf16)
        dyh_ref[...] = dyh
        dya_ref[...] = dya
        dpg_ref[:, :D_MODEL] = (dmix * yh * gh * (1.0 - gh)).astype(bf16)
        dpg_ref[:, D_MODEL:] = (dmix * ya * ga * (1.0 - ga)).astype(bf16)
        dog_ref[...] = _dot(dyh, wbh_ref[...], 1, 1)
        doa_ref[...] = _dot(dya, wba_ref[...], 1, 1)

    p = h0.shape[0]
    hw = HG_HEADS * HG_K
    vec = lambda: pl.BlockSpec((1, D_MODEL), lambda i: (0, 0))
    sds = lambda n, dt: jax.ShapeDtypeStruct((p, n), dt)
    return pl.pallas_call(
        body, name="mid_back", grid=(p // TM,),
        in_specs=[_tm_row(D_MODEL), _tm_row(D_MODEL), _tm_row(N_G), _tm_row(hw), _tm_row(ATT_QW)] + _mid_weight_specs(),
        out_specs=[_tm_row(D_MODEL), _tm_row(N_G), _tm_row(hw), _tm_row(ATT_QW), _tm_row(D_MODEL), _tm_row(D_MODEL),
                   _tm_row(D_MODEL), vec(), vec()],
        out_shape=[sds(D_MODEL, f32), sds(N_G, bf16), sds(hw, f32), sds(ATT_QW, f32), sds(D_MODEL, bf16),
                   sds(D_MODEL, bf16), sds(D_MODEL, bf16)] + [jax.ShapeDtypeStruct((1, D_MODEL), f32)] * 2,
        compiler_params=_cparams(("arbitrary",)),
    )(dh1, h0, pg, ogc, oac, wbh, wba, wout, ln1g, ln1b)


def inproj_bwd(dh0p, dhq, dhf, dhi, dhg, daq, dkc, dkp, dvc, dvp, dkm, dvm, dpg, w_in, x, metablk, g, b):
    p = dh0p.shape[0]
    nbk = p // BLOCK
    per = TM // BLOCK

    def body(dh0_ref, dq_ref, df_ref, di_ref, dg_ref, daq_ref, dkc_ref, *rest):
        dkp_refs, dvc_ref, dvp_refs = rest[:per], rest[per], rest[per + 1:2 * per + 1]
        (dkm_ref, dvm_ref, dpg_ref, w_ref, x_ref, mb_ref, g_ref, b_ref,
         dproj_ref, dx_ref, dmeta_ref, dlg_ref, dlb_ref) = rest[2 * per + 1:]
        i = pl.program_id(0)

        @pl.when(i == 0)
        def _():
            dlg_ref[...] = jnp.zeros_like(dlg_ref)
            dlb_ref[...] = jnp.zeros_like(dlb_ref)

        zero_pad = jnp.zeros((TM - N_META, BLOCK), f32)
        first = i == 0
        rows = _tm_rows(i)

        def keys(cur_ref, next_refs, meta_ref):
            nxt = jnp.concatenate([jnp.where(per * i + 1 + m < nbk, next_refs[m][...], 0.0) for m in range(per)], axis=0)
            t = cur_ref[...] + nxt
            return t + jnp.where(first, jnp.concatenate([zero_pad, meta_ref[...]], axis=0), 0.0)

        dproj = jnp.concatenate(
            [dq_ref[...], df_ref[...], di_ref[...], dg_ref[...], daq_ref[...],
             keys(dkc_ref, dkp_refs, dkm_ref).astype(bf16), keys(dvc_ref, dvp_refs, dvm_ref).astype(bf16),
             dpg_ref[...]], axis=1)
        dproj = jnp.where(rows >= LEAD * BLOCK, dproj, jnp.zeros_like(dproj))
        dproj_ref[...] = dproj
        valid = rows >= TM - N_META
        dh0 = jnp.where(valid, dh0_ref[...] + _dot(dproj, w_ref[...], 1, 1), 0.0)
        xb = jnp.where(first, mb_ref[...], x_ref[...])
        xh, rs = _ln_stats(xb)
        dlg_ref[...] += jnp.sum(dh0 * xh, axis=0, keepdims=True)
        dlb_ref[...] += jnp.sum(dh0, axis=0, keepdims=True)
        dx = jnp.where(valid, _ln_bwd(dh0, xh, rs, g_ref[...]), 0.0)
        dx_ref[...] = dx

        @pl.when(first)
        def _():
            dmeta_ref[...] = dx[TM - N_META:, :]

    row = _tm_row
    nxt = [pl.BlockSpec((BLOCK, BLOCK), functools.partial(lambda i, m: (jnp.minimum(per * i + 1 + m, nbk - 1), 0), m=m))
           for m in range(per)]
    hw = HG_HEADS * HG_K
    vec = lambda: pl.BlockSpec((1, D_MODEL), lambda i: (0, 0))
    return pl.pallas_call(
        body, name="inproj_bwd", grid=(p // TM,),
        in_specs=[row(D_MODEL), row(hw), row(hw), row(hw), row(hw), row(ATT_QW),
                  row(BLOCK)] + nxt + [row(BLOCK)] + nxt + [_const_spec((N_META, BLOCK)), _const_spec((N_META, BLOCK)),
                  row(N_G), _const_spec((D_MODEL, IN_W)), _tm_tokens(),
                  _const_spec((TM, D_MODEL)), _const_spec((1, D_MODEL)), _const_spec((1, D_MODEL))],
        out_specs=[row(IN_W), _tm_tokens(), pl.BlockSpec((N_META, D_MODEL), lambda i: (0, 0)), vec(), vec()],
        out_shape=[jax.ShapeDtypeStruct((p, IN_W), bf16), jax.ShapeDtypeStruct((p - TM, D_MODEL), f32),
                   jax.ShapeDtypeStruct((N_META, D_MODEL), f32),
                   jax.ShapeDtypeStruct((1, D_MODEL), f32), jax.ShapeDtypeStruct((1, D_MODEL), f32)],
        compiler_params=_cparams(("arbitrary",)),
    )(dh0p, dhq, dhf, dhi, dhg, daq, dkc, *([dkp] * per), dvc, *([dvp] * per), dkm, dvm, dpg, w_in, x, metablk, g, b)


def wgrad(a, b, name, tk, tn, tp, by_cols, out_dtype=f32):
    p, k = a.shape
    n = b.shape[1]
    nsteps = p // tp

    def body(a_ref, b_ref, o_ref, acc_ref):
        ip = pl.program_id(2)

        @pl.when(ip == 0)
        def _():
            acc_ref[...] = jnp.zeros_like(acc_ref)

        acc_ref[...] += _dot(a_ref[...], b_ref[...], 0, 0)

        @pl.when(ip == nsteps - 1)
        def _():
            o_ref[0] = acc_ref[...].astype(out_dtype)

    if by_cols:
        shard_n = n // N_SHARD
        per = shard_n // tn
        out_shape = (N_SHARD, k, shard_n)
        omap = lambda ik, jn, ip: (jn // per, ik, jn % per)
    else:
        out_shape = (1, k, n)
        omap = lambda ik, jn, ip: (0, ik, jn)
    return pl.pallas_call(
        body, name=name, grid=(k // tk, n // tn, nsteps),
        in_specs=[pl.BlockSpec((tp, tk), lambda ik, jn, ip: (ip, ik)),
                  pl.BlockSpec((tp, tn), lambda ik, jn, ip: (ip, jn))],
        out_specs=pl.BlockSpec((1, tk, tn), omap),
        out_shape=jax.ShapeDtypeStruct(out_shape, out_dtype),
        scratch_shapes=[pltpu.VMEM((tk, tn), f32)],
        compiler_params=_cparams(("parallel", "parallel", "arbitrary")),
    )(a, b)


def adamw(w, g, m, v, name):
    r, c = w.shape
    tr = r
    for cand in (256, 176, 128):
        if r > cand and r % cand == 0:
            tr = cand
            break

    def body(w_ref, g_ref, m_ref, v_ref, d_ref, mo_ref, vo_ref):
        gg = g_ref[...]
        mn = ADAM_B1 * m_ref[...] + (1.0 - ADAM_B1) * gg
        vn = ADAM_B2 * v_ref[...] + (1.0 - ADAM_B2) * (gg * gg)
        m_hat = mn / (1.0 - ADAM_B1 ** ADAM_STEP)
        v_hat = vn / (1.0 - ADAM_B2 ** ADAM_STEP)
        d_ref[...] = -ADAM_LR * (m_hat / (jnp.sqrt(v_hat) + ADAM_EPS) + ADAM_WD * w_ref[...])
        mo_ref[...] = mn
        vo_ref[...] = vn

    spec = pl.BlockSpec((tr, c), lambda i: (i, 0))
    sds = jax.ShapeDtypeStruct((r, c), f32)
    return pl.pallas_call(
        body, name=name, grid=(r // tr,), in_specs=[spec] * 4, out_specs=[spec] * 3, out_shape=[sds] * 3,
        compiler_params=_cparams(("parallel",)),
    )(w, g, m, v)


def _me():
    return lax.axis_index("x"), lax.axis_index("y"), lax.axis_index("c")


def _chip_peer(x, y, c, k):
    return (x ^ (k >> 1), y ^ (k & 1), c)


ANY = pl.BlockSpec(memory_space=pl.ANY)


def gather_weights(now, later):
    n, n_later = len(now), len(later)
    out_dtypes = [bf16 if s.size > 16 * 256 else f32 for s in now]

    def body(*refs):
        ins, later_ins = refs[:n], refs[n:n + n_later]
        outs, later_outs = refs[n + n_later:2 * n + n_later], refs[2 * n + n_later:2 * (n + n_later)]
        stage = refs[2 * (n + n_later):3 * n + 2 * n_later]
        send_sems, recv_sems, local_sems = refs[3 * n + 2 * n_later:]
        x, y, c = _me()
        j = 2 * x + y
        for w in range(n):
            stage[w][...] = ins[w][...].astype(out_dtypes[w])
        sends, locs = [], []
        for w in range(n):
            loc = pltpu.make_async_copy(stage[w], outs[w].at[j], local_sems.at[w])
            loc.start()
            locs.append(loc)
            for k in (1, 2, 3):
                cp = pltpu.make_async_remote_copy(
                    src_ref=stage[w], dst_ref=outs[w].at[j], send_sem=send_sems.at[w, k - 1],
                    recv_sem=recv_sems.at[w, k - 1], device_id=_chip_peer(x, y, c, k), device_id_type=MESH)
                cp.start()
                sends.append(cp)
        for w in range(n_later):
            later_outs[w][...] = later_ins[w][...].astype(bf16)
        for w in range(n):
            for k in (1, 2, 3):
                pltpu.make_async_remote_copy(
                    src_ref=stage[w], dst_ref=outs[w].at[j ^ k], send_sem=send_sems.at[w, k - 1],
                    recv_sem=recv_sems.at[w, k - 1], device_id=_chip_peer(x, y, c, k), device_id_type=MESH).wait_recv()
        for cp in sends:
            cp.wait_send()
        for loc in locs:
            loc.wait()

    vmem = pl.BlockSpec(memory_space=pltpu.VMEM)
    return pl.pallas_call(
        body, name="gather_weights",
        in_specs=[vmem] * (n + n_later), out_specs=[ANY] * n + [vmem] * n_later,
        out_shape=[jax.ShapeDtypeStruct((N_SHARD,) + s.shape, dt) for s, dt in zip(now, out_dtypes)]
        + [jax.ShapeDtypeStruct(s.shape, bf16) for s in later],
        scratch_shapes=[pltpu.VMEM(s.shape, dt) for s, dt in zip(now, out_dtypes)]
        + [pltpu.SemaphoreType.DMA((n, 3)), pltpu.SemaphoreType.DMA((n, 3)), pltpu.SemaphoreType.DMA((n,))],
        compiler_params=pltpu.CompilerParams(vmem_limit_bytes=VMEM_LIMIT),
    )(*now, *later)


def _shard_push(srcs, dsts, send_sems, recv_sems, local_sems):
    def remote(w, k, slot):
        x, y, c = _me()
        return pltpu.make_async_remote_copy(
            src_ref=srcs[w], dst_ref=dsts[w].at[slot], send_sem=send_sems.at[w, k - 1],
            recv_sem=recv_sems.at[w, k - 1], device_id=_chip_peer(x, y, c, k), device_id_type=MESH)

    def local(w):
        x, y, _ = _me()
        return pltpu.make_async_copy(srcs[w], dsts[w].at[2 * x + y], local_sems.at[w])

    def start():
        x, y, _ = _me()
        for w in range(len(srcs)):
            local(w).start()
            for k in (1, 2, 3):
                remote(w, k, 2 * x + y).start()

    def wait():
        x, y, _ = _me()
        for w in range(len(srcs)):
            for k in (1, 2, 3):
                remote(w, k, (2 * x + y) ^ k).wait_recv()
        for w in range(len(srcs)):
            for k in (1, 2, 3):
                remote(w, k, 2 * x + y).wait_send()
            local(w).wait()

    return start, wait


def _grad_push(srcs, dsts, send_sems, recv_sems):
    def copy(w, k):
        x, y, c = _me()
        px, py, pc = x ^ (k >> 2), y ^ ((k >> 1) & 1), c ^ (k & 1)
        return pltpu.make_async_remote_copy(
            src_ref=srcs[w].at[2 * px + py, pc], dst_ref=dsts[w].at[k - 1], send_sem=send_sems.at[w, k - 1],
            recv_sem=recv_sems.at[w, k - 1], device_id=(px, py, pc), device_id_type=MESH)

    def start():
        for w in range(len(srcs)):
            for k in range(1, N_DEV):
                copy(w, k).start()

    def wait():
        for w in range(len(srcs)):
            for k in range(1, N_DEV):
                copy(w, k).wait_recv()
        for w in range(len(srcs)):
            for k in range(1, N_DEV):
                copy(w, k).wait_send()

    return start, wait


def _grad_push_specs(grads):
    n = len(grads)
    return ([ANY] * n, [ANY] * n, [jax.ShapeDtypeStruct((N_DEV - 1,) + g.shape[2:], g.dtype) for g in grads],
            [pltpu.SemaphoreType.DMA((n, N_DEV - 1)), pltpu.SemaphoreType.DMA((n, N_DEV - 1))])


def add_eight(own, parts, jc_idx, name):
    _, half, c = parts.shape
    tr = half // 2 if (half // 2) % 16 == 0 else half

    def body(jc_ref, own_ref, p_ref, out_ref):
        acc = own_ref[0, 0].astype(f32)
        for k in range(N_DEV - 1):
            acc = acc + p_ref[k].astype(f32)
        out_ref[0] = acc

    return pl.pallas_call(
        body, name=name,
        grid_spec=pltpu.PrefetchScalarGridSpec(
            num_scalar_prefetch=1, grid=(half // tr,),
            in_specs=[pl.BlockSpec((1, 1, tr, c), lambda t, jc: (jc[0], jc[1], t, 0)),
                      pl.BlockSpec((N_DEV - 1, tr, c), lambda t, jc: (0, t, 0))],
            out_specs=pl.BlockSpec((1, tr, c), lambda t, jc: (jc[1], t, 0))),
        out_shape=jax.ShapeDtypeStruct((2, half, c), f32),
        compiler_params=_cparams(("parallel",)),
    )(jc_idx, own, parts)


def _push_specs(shards):
    n = len(shards)
    return ([ANY] * n, [ANY] * n, [jax.ShapeDtypeStruct((N_SHARD,) + s.shape, s.dtype) for s in shards],
            [pltpu.SemaphoreType.DMA((n, 3)), pltpu.SemaphoreType.DMA((n, 3)), pltpu.SemaphoreType.DMA((n,))])


def pair_exchange_halves(grads, small):
    n = len(grads)

    def body(*refs):
        ins, small_ref = refs[:n], refs[n]
        outs, gath = refs[n + 1:2 * n + 1], refs[2 * n + 1]
        send_sems, recv_sems, s_send, s_recv, local_sem = refs[2 * n + 2:]
        x, y, c = _me()
        me = 4 * x + 2 * y + c
        sends = []
        for w in range(n):
            half = ins[w].shape[1] // 2
            cp = pltpu.make_async_remote_copy(
                src_ref=ins[w].at[:, pl.ds((1 - c) * half, half), :], dst_ref=outs[w],
                send_sem=send_sems.at[w], recv_sem=recv_sems.at[w], device_id=(x, y, 1 - c), device_id_type=MESH)
            cp.start()
            sends.append(cp)
        loc = pltpu.make_async_copy(small_ref, gath.at[me], local_sem)
        loc.start()
        for k in range(1, N_DEV):
            cp = pltpu.make_async_remote_copy(
                src_ref=small_ref, dst_ref=gath.at[me], send_sem=s_send.at[k - 1], recv_sem=s_recv.at[k - 1],
                device_id=(x ^ (k >> 2), y ^ ((k >> 1) & 1), c ^ (k & 1)), device_id_type=MESH)
            cp.start()
            sends.append(cp)
        for w in range(n):
            half = ins[w].shape[1] // 2
            pltpu.make_async_remote_copy(
                src_ref=ins[w].at[:, pl.ds(0, half), :], dst_ref=outs[w], send_sem=send_sems.at[w],
                recv_sem=recv_sems.at[w], device_id=(x, y, 1 - c), device_id_type=MESH).wait_recv()
        for k in range(1, N_DEV):
            pltpu.make_async_remote_copy(
                src_ref=small_ref, dst_ref=gath.at[me ^ k], send_sem=s_send.at[k - 1], recv_sem=s_recv.at[k - 1],
                device_id=(x ^ (k >> 2), y ^ ((k >> 1) & 1), c ^ (k & 1)), device_id_type=MESH).wait_recv()
        for cp in sends:
            cp.wait_send()
        loc.wait()

    return pl.pallas_call(
        body, name="pair_exchange_halves", in_specs=[ANY] * (n + 1), out_specs=[ANY] * (n + 1),
        out_shape=[jax.ShapeDtypeStruct((g.shape[0], g.shape[1] // 2, g.shape[2]), f32) for g in grads]
        + [jax.ShapeDtypeStruct((N_DEV,) + small.shape, f32)],
        scratch_shapes=[pltpu.SemaphoreType.DMA((n,)), pltpu.SemaphoreType.DMA((n,)),
                        pltpu.SemaphoreType.DMA((N_DEV - 1,)), pltpu.SemaphoreType.DMA((N_DEV - 1,)),
                        pltpu.SemaphoreType.DMA],
    )(*grads, small)


def chip_exchange(sums):
    n = len(sums)

    def body(*refs):
        ins, outs = refs[:n], refs[n:2 * n]
        send_sems, recv_sems = refs[2 * n:]
        x, y, c = _me()
        j = 2 * x + y
        sends = []
        for w in range(n):
            for k in (1, 2, 3):
                cp = pltpu.make_async_remote_copy(
                    src_ref=ins[w].at[j ^ k], dst_ref=outs[w].at[k - 1], send_sem=send_sems.at[w, k - 1],
                    recv_sem=recv_sems.at[w, k - 1], device_id=_chip_peer(x, y, c, k), device_id_type=MESH)
                cp.start()
                sends.append(cp)
        for w in range(n):
            for k in (1, 2, 3):
                pltpu.make_async_remote_copy(
                    src_ref=ins[w].at[0], dst_ref=outs[w].at[k - 1], send_sem=send_sems.at[w, k - 1],
                    recv_sem=recv_sems.at[w, k - 1], device_id=_chip_peer(x, y, c, k), device_id_type=MESH).wait_recv()
        for cp in sends:
            cp.wait_send()

    return pl.pallas_call(
        body, name="chip_exchange", in_specs=[ANY] * n, out_specs=[ANY] * n,
        out_shape=[jax.ShapeDtypeStruct((N_SHARD - 1,) + s.shape[1:], s.dtype) for s in sums],
        scratch_shapes=[pltpu.SemaphoreType.DMA((n, 3)), pltpu.SemaphoreType.DMA((n, 3))],
    )(*sums)


def pair_exchange_results(halves):
    n = len(halves)

    def body(*refs):
        ins, outs = refs[:n], refs[n:2 * n]
        send_sems, recv_sems = refs[2 * n:]
        x, y, c = _me()
        sends = []
        for w in range(n):
            cp = pltpu.make_async_remote_copy(
                src_ref=ins[w].at[c], dst_ref=outs[w].at[c], send_sem=send_sems.at[w], recv_sem=recv_sems.at[w],
                device_id=(x, y, 1 - c), device_id_type=MESH)
            cp.start()
            sends.append(cp)
        for w in range(n):
            pltpu.make_async_remote_copy(
                src_ref=ins[w].at[c], dst_ref=outs[w].at[1 - c], send_sem=send_sems.at[w],
                recv_sem=recv_sems.at[w], device_id=(x, y, 1 - c), device_id_type=MESH).wait_recv()
        for cp in sends:
            cp.wait_send()

    return pl.pallas_call(
        body, name="pair_exchange_results", in_specs=[ANY] * n, out_specs=[ANY] * n,
        out_shape=[jax.ShapeDtypeStruct(h.shape, f32) for h in halves],
        input_output_aliases={w: w for w in range(n)},
        scratch_shapes=[pltpu.SemaphoreType.DMA((n,)), pltpu.SemaphoreType.DMA((n,))],
    )(*halves)


def add_pair(grad, other, c_idx, name):
    _, r, c = grad.shape
    half = r // 2
    tr = half // 2 if (half // 2) % 8 == 0 else half
    per = half // tr

    def body(c_ref, g_ref, o_ref, out_ref):
        out_ref[...] = (g_ref[...] + o_ref[...]).astype(bf16)

    return pl.pallas_call(
        body, name=name,
        grid_spec=pltpu.PrefetchScalarGridSpec(
            num_scalar_prefetch=1, grid=(N_SHARD, per),
            in_specs=[pl.BlockSpec((1, tr, c), lambda j, t, cr: (j, cr[0] * per + t, 0)),
                      pl.BlockSpec((1, tr, c), lambda j, t, cr: (j, t, 0))],
            out_specs=pl.BlockSpec((1, tr, c), lambda j, t, cr: (j, t, 0))),
        out_shape=jax.ShapeDtypeStruct((N_SHARD, half, c), bf16),
        compiler_params=_cparams(("parallel", "parallel")),
    )(c_idx, grad, other)


def add_four(own, parts, jc_idx, name):
    _, half, c = parts.shape
    tr = half // 2 if (half // 2) % 8 == 0 else half

    def body(jc_ref, own_ref, p_ref, out_ref):
        acc = own_ref[0].astype(f32)
        for k in range(N_SHARD - 1):
            acc = acc + p_ref[k].astype(f32)
        out_ref[0] = acc

    return pl.pallas_call(
        body, name=name,
        grid_spec=pltpu.PrefetchScalarGridSpec(
            num_scalar_prefetch=1, grid=(half // tr,),
            in_specs=[pl.BlockSpec((1, tr, c), lambda t, jc: (jc[0], t, 0)),
                      pl.BlockSpec((N_SHARD - 1, tr, c), lambda t, jc: (0, t, 0))],
            out_specs=pl.BlockSpec((1, tr, c), lambda t, jc: (jc[1], t, 0))),
        out_shape=jax.ShapeDtypeStruct((2, half, c), f32),
        compiler_params=_cparams(("parallel",)),
    )(jc_idx, own, parts)


def sum_devices(gathered):
    def body(g_ref, out_ref):
        acc = g_ref[0]
        for d in range(1, N_DEV):
            acc = acc + g_ref[d]
        out_ref[...] = acc

    return pl.pallas_call(body, name="sum_devices", out_shape=jax.ShapeDtypeStruct(gathered.shape[1:], f32))(gathered)


def _rows128(a, rows):
    flat = a.reshape(-1, BLOCK) if a.size % BLOCK == 0 else jnp.pad(a.reshape(1, -1), ((0, 0), (0, BLOCK - a.size)))
    return jnp.pad(flat, ((0, rows - flat.shape[0]), (0, 0)))


def kernel(x, meta_tokens, ln_emb_g, ln_emb_b, w_in, hg_lower_bounds, hg_norm_g, attn_sinks, w_branch_hg, w_branch_attn, w_out, ln1_g, ln1_b, w_ffn_in, w_ffn_out, ln2_g, ln2_b, loss_target, m_meta_tokens, m_ln_emb_g, m_ln_emb_b, m_w_in, m_hg_lower_bounds, m_hg_norm_g, m_attn_sinks, m_w_branch_hg, m_w_branch_attn, m_w_out, m_ln1_g, m_ln1_b, m_w_ffn_in, m_w_ffn_out, m_ln2_g, m_ln2_b, v_meta_tokens, v_ln_emb_g, v_ln_emb_b, v_w_in, v_hg_lower_bounds, v_hg_norm_g, v_attn_sinks, v_w_branch_hg, v_w_branch_attn, v_w_out, v_ln1_g, v_ln1_b, v_w_ffn_in, v_w_ffn_out, v_ln2_g, v_ln2_b):
    seq = x.shape[1]
    nb = seq // BLOCK + 1
    xs = x[0]
    ts = loss_target[0]
    ix, iy, ic = _me()
    shard = 2 * ix + iy
    vec = lambda a: a.reshape(1, D_MODEL)

    g_in, g_meta, s_bh, s_ba, s_out, s_fi, s_fo = gather_weights(
        [w_in[0], meta_tokens], [w_branch_hg[0], w_branch_attn[0], w_out[0], w_ffn_in[0], w_ffn_out[0]])
    by_cols = lambda g: g.transpose(1, 0, 2).reshape(g.shape[1], N_SHARD * g.shape[2])
    wf_in = by_cols(g_in)
    metablk = jnp.pad(by_cols(g_meta), ((TM - N_META, 0), (0, 0)))

    pos = jnp.arange(nb * BLOCK, dtype=jnp.int32) - PAD
    half = HEAD_DIM // 2
    inv = ROPE_THETA ** (-jnp.arange(half, dtype=f32) / half)
    ang = pos.astype(f32)[:, None] * inv[None, :]
    cos = jnp.tile(jnp.cos(ang), (1, BLOCK // half))
    sin = jnp.tile(jnp.sin(ang), (1, BLOCK // half))
    sinks8 = jnp.broadcast_to(attn_sinks.reshape(ATT_HEADS, 1), (ATT_HEADS, BLOCK))
    ng = hg_norm_g.reshape(1, HG_K)

    h0, h0b, pa, pg = emb_inproj(xs, metablk, vec(ln_emb_g), vec(ln_emb_b), wf_in)
    og, sprev, g_fi = hgrn_fwd(pa, hg_lower_bounds, ng, nb, [s_fi])
    oatt, g_fo, g_out, g_bh, g_ba = attn_fwd(pa, cos, sin, sinks8, nb, [s_fo, s_out, s_bh, s_ba])
    wf_bh, wf_ba, wf_fi = by_cols(g_bh), by_cols(g_ba), by_cols(g_fi)
    wf_out = g_out.reshape(D_MODEL, D_MODEL)
    wf_fo = g_fo.reshape(D_FF, D_MODEL)
    h1, h1b, mixin, og, oatt = mid_front(h0, pg, og, oatt, wf_bh, wf_ba, wf_out, ln1_g, ln1_b)
    dh1, dau, sact, dr2, loss_part, dg2, db2 = mid_ffn(h1, ts, wf_fi, wf_fo, ln2_g, ln2_b)
    dh0p, dpg, dog, doa, dyh, dya, dr1, dg1, db1 = mid_back(dh1, h0, pg, og, oatt, wf_bh, wf_ba, wf_out, ln1_g, ln1_b)
    steps = h0.shape[0] // TM
    tp = TM * max(k for k in (3, 2, 1) if steps % k == 0)
    pieces = lambda g: g.reshape(N_SHARD, 2, -1, g.shape[-1])
    gb_bh = pieces(wgrad(og, dyh, "wgrad_bh", 512, 256, tp, True, bf16))
    gb_ba = pieces(wgrad(oatt, dya, "wgrad_ba", 512, 256, tp, True, bf16))
    gb_out = pieces(wgrad(mixin, dr1, "wgrad_out", D_MODEL, D_MODEL, tp, False, bf16))
    gb_fi = pieces(wgrad(h1b, dau, "wgrad_fi", D_MODEL, 2 * D_FF // N_SHARD, tp, True, bf16))
    gb_fo = pieces(wgrad(sact, dr2, "wgrad_fo", D_FF // 2, D_MODEL, tp, False, bf16))
    dhq, dhf, dhi, dhg, dlb4, dng, r_fi, r_fo = hgrn_bwd(pa, hg_lower_bounds, ng, sprev, dog, nb, [gb_fi, gb_fo])
    daq, dkc, dkp, dvc, dvp, dkm, dvm, dsk, r_out, r_bh, r_ba = attn_bwd(pa, cos, sin, sinks8, doa, nb,
                                                                         [gb_out, gb_bh, gb_ba])
    dproj, dx, dmeta, dlg, dlb = inproj_bwd(dh0p, dhq, dhf, dhi, dhg, daq, dkc, dkp, dvc, dvp, dkm, dvm, dpg,
                                      wf_in, xs, metablk, vec(ln_emb_g), vec(ln_emb_b))
    gw_in = wgrad(h0b, dproj, "wgrad_in", D_MODEL, IN_W // 2, tp, False)
    gw_in = gw_in.reshape(D_MODEL, N_SHARD, IN_W // N_SHARD).transpose(1, 0, 2)

    parts = [(dlg, 8), (dlb, 8), (dlb4, 8), (dng, 8), (dsk[:, 0], 8),
             (dg1, 8), (db1, 8), (dg2, 8), (db2, 8), (dmeta, BLOCK)]
    small = jnp.concatenate([_rows128(a, r) for a, r in parts], axis=0)

    c_idx = jnp.reshape(ic, (1,)).astype(jnp.int32)
    jc_idx = jnp.stack([shard, ic]).astype(jnp.int32)
    other_in, gathered = pair_exchange_halves([gw_in], small)
    sum_in = add_pair(gw_in, other_in, c_idx, "add_pair_in")
    quad_in, = chip_exchange([sum_in])
    halves = [add_four(sum_in, quad_in, jc_idx, "add_four_in")]
    halves += [add_eight(g, r, jc_idx, "add_eight_" + nm) for nm, g, r in
               (("bh", gb_bh, r_bh), ("ba", gb_ba, r_ba), ("out", gb_out, r_out), ("fi", gb_fi, r_fi),
                ("fo", gb_fo, r_fo))]
    red = [r.reshape(-1, r.shape[-1]) for r in pair_exchange_results(halves)]
    small_sum = sum_devices(gathered)

    offs, acc = [], 0
    for _, r in parts:
        offs.append(acc)
        acc += r
    take = lambda n, size: small_sum[offs[n]:offs[n] + parts[n][1]].reshape(-1)[:size]
    g_meta_full = take(9, N_META * D_MODEL).reshape(N_META, D_MODEL)
    g_small = {
        "meta_tokens": lax.dynamic_slice_in_dim(g_meta_full, shard * (D_MODEL // N_SHARD), D_MODEL // N_SHARD, axis=1),
        "ln_emb_g": take(0, D_MODEL), "ln_emb_b": take(1, D_MODEL),
        "hg_lower_bounds": take(2, 2 * HG_HEADS * HG_K).reshape(2, HG_HEADS * HG_K),
        "hg_norm_g": take(3, HG_K).reshape(1, HG_K), "attn_sinks": take(4, ATT_HEADS).reshape(1, ATT_HEADS),
        "ln1_g": take(5, D_MODEL).reshape(1, D_MODEL), "ln1_b": take(6, D_MODEL).reshape(1, D_MODEL),
        "ln2_g": take(7, D_MODEL).reshape(1, D_MODEL), "ln2_b": take(8, D_MODEL).reshape(1, D_MODEL),
    }
    g_big = {"w_in": red[0], "w_branch_hg": red[1], "w_branch_attn": red[2], "w_out": red[3],
             "w_ffn_in": red[4], "w_ffn_out": red[5]}

    names = ["meta_tokens", "ln_emb_g", "ln_emb_b", "w_in", "hg_lower_bounds", "hg_norm_g", "attn_sinks",
             "w_branch_hg", "w_branch_attn", "w_out", "ln1_g", "ln1_b", "w_ffn_in", "w_ffn_out", "ln2_g", "ln2_b"]
    given = dict(
        meta_tokens=(meta_tokens, m_meta_tokens, v_meta_tokens), ln_emb_g=(ln_emb_g, m_ln_emb_g, v_ln_emb_g),
        ln_emb_b=(ln_emb_b, m_ln_emb_b, v_ln_emb_b), w_in=(w_in, m_w_in, v_w_in),
        hg_lower_bounds=(hg_lower_bounds, m_hg_lower_bounds, v_hg_lower_bounds),
        hg_norm_g=(hg_norm_g, m_hg_norm_g, v_hg_norm_g), attn_sinks=(attn_sinks, m_attn_sinks, v_attn_sinks),
        w_branch_hg=(w_branch_hg, m_w_branch_hg, v_w_branch_hg),
        w_branch_attn=(w_branch_attn, m_w_branch_attn, v_w_branch_attn), w_out=(w_out, m_w_out, v_w_out),
        ln1_g=(ln1_g, m_ln1_g, v_ln1_g), ln1_b=(ln1_b, m_ln1_b, v_ln1_b), w_ffn_in=(w_ffn_in, m_w_ffn_in, v_w_ffn_in),
        w_ffn_out=(w_ffn_out, m_w_ffn_out, v_w_ffn_out), ln2_g=(ln2_g, m_ln2_g, v_ln2_g), ln2_b=(ln2_b, m_ln2_b, v_ln2_b))
    out_g, out_d, out_m, out_v = [], [], [], []
    for nm in names:
        w, m, v = given[nm]
        shape = w.shape
        g = g_big[nm] if nm in g_big else g_small[nm]
        two_d = (lambda a: a.reshape(8, BLOCK)) if w.ndim == 1 else (lambda a: a.reshape(a.shape[-2], a.shape[-1]))
        d, mn, vn = adamw(two_d(w), two_d(g), two_d(m), two_d(v), "adamw_" + nm)
        out_g.append(g.reshape(shape))
        out_d.append(d.reshape(shape))
        out_m.append(mn.reshape(shape))
        out_v.append(vn.reshape(shape))

    loss = lax.psum(loss_part[0, 0], ("x", "y", "c"))
    grad_x = dx.reshape(x.shape)
    return (loss, grad_x, *out_g, *out_d, *out_m, *out_v)
```

```python
import functools

import jax
import jax.numpy as jnp
from jax import lax
from jax.experimental import pallas as pl
from jax.experimental.pallas import tpu as pltpu

f32 = jnp.float32
bf16 = jnp.bfloat16

D_MODEL = 1024
BLOCK = 128
N_META = 16
PAD = BLOCK - N_META
HG_HEADS = 4
HG_K = 128
SUB = 16
ATT_HEADS = 8
HEAD_DIM = 64
ATT_QW = ATT_HEADS * HEAD_DIM
D_FF = 2816
EPS = 1e-5
ALPHA = 2.0 ** 0.25
ROPE_THETA = 10000.0
N_A = 2816
N_G = 2048
IN_W = N_A + N_G
N_SHARD = 4
N_DEV = 8

ADAM_LR = 0.001
ADAM_B1 = 0.9
ADAM_B2 = 0.999
ADAM_EPS = 1e-08
ADAM_WD = 0.01
ADAM_STEP = 10

TM = 256
LEAD = TM // BLOCK - 1

VMEM_LIMIT = 56 * 1024 * 1024
MESH = pl.DeviceIdType.MESH


def _cparams(sem, vmem=VMEM_LIMIT):
    return pltpu.CompilerParams(dimension_semantics=sem, vmem_limit_bytes=vmem)


def _const_spec(shape):
    zeros = (0,) * len(shape)
    return pl.BlockSpec(shape, lambda *_: zeros, pipeline_mode=pl.Buffered(1))


def _dot(a, b, ca, cb):
    return lax.dot_general(a.astype(bf16), b.astype(bf16), (((ca,), (cb,)), ((), ())),
                           preferred_element_type=f32)


@jax.custom_vjp
def mm(a, b):
    return _dot(a, b, 1, 0)


mm.defvjp(lambda a, b: (_dot(a, b, 1, 0), (a, b)),
          lambda r, g: (_dot(g, r[1], 1, 1), _dot(r[0], g, 0, 0)))


@jax.custom_vjp
def mm_nt(a, b):
    return _dot(a, b, 1, 1)


mm_nt.defvjp(lambda a, b: (_dot(a, b, 1, 1), (a, b)),
             lambda r, g: (_dot(g, r[1], 1, 0), _dot(g, r[0], 0, 0)))


@jax.custom_vjp
def mm_tn(a, b):
    return _dot(a, b, 0, 0)


mm_tn.defvjp(lambda a, b: (_dot(a, b, 0, 0), (a, b)),
             lambda r, g: (_dot(r[1], g, 1, 1), _dot(r[0], g, 1, 0)))


@functools.partial(jax.custom_vjp, nondiff_argnums=(1,))
def roll_lanes(x, shift):
    return pltpu.roll(x, shift, 1)


roll_lanes.defvjp(lambda x, shift: (pltpu.roll(x, shift, 1), None),
                  lambda shift, _, g: (pltpu.roll(g, (128 - shift) % 128, 1),))


def _sigmoid(x):
    return 1.0 / (1.0 + jnp.exp(-x))


def _ln_stats(x):
    mu = jnp.mean(x, axis=-1, keepdims=True)
    xc = x - mu
    var = jnp.mean(xc * xc, axis=-1, keepdims=True)
    rs = lax.rsqrt(var + EPS)
    return xc * rs, rs


def _ln_bwd(dy, xh, rs, g):
    dxh = dy * g
    m1 = jnp.mean(dxh, axis=-1, keepdims=True)
    m2 = jnp.mean(dxh * xh, axis=-1, keepdims=True)
    return rs * (dxh - m1 - xh * m2)


def _row_ids(i):
    return i * BLOCK + lax.broadcasted_iota(jnp.int32, (BLOCK, 1), 0)


def _tm_rows(i):
    return i * TM + lax.broadcasted_iota(jnp.int32, (TM, 1), 0)


def _tm_row(n):
    return pl.BlockSpec((TM, n), lambda i: (i, 0))


def _tm_tokens():
    return pl.BlockSpec((TM, D_MODEL), lambda i: (jnp.maximum(i - 1, 0), 0))


def emb_inproj(x, metablk, g, b, w_in):
    nsteps = x.shape[0] // TM + 1

    def body(x_ref, mb_ref, g_ref, b_ref, w_ref, h0_ref, h0b_ref, pa_ref, pg_ref):
        i = pl.program_id(0)
        xb = jnp.where(i == 0, mb_ref[...], x_ref[...])
        xh, _ = _ln_stats(xb)
        y = xh * g_ref[...] + b_ref[...]
        y = jnp.where(_tm_rows(i) >= TM - N_META, y, 0.0)
        h0_ref[...] = y
        yb = y.astype(bf16)
        h0b_ref[...] = yb
        pa_ref[...] = jnp.dot(yb, w_ref[:, :N_A], preferred_element_type=f32)
        pg_ref[...] = jnp.dot(yb, w_ref[:, N_A:], preferred_element_type=f32)

    p = nsteps * TM
    row = _tm_row
    return pl.pallas_call(
        body, name="emb_inproj", grid=(nsteps,),
        in_specs=[_tm_tokens(),
                  _const_spec((TM, D_MODEL)), _const_spec((1, D_MODEL)), _const_spec((1, D_MODEL)),
                  _const_spec((D_MODEL, IN_W))],
        out_specs=[row(D_MODEL), row(D_MODEL), row(N_A), row(N_G)],
        out_shape=[jax.ShapeDtypeStruct((p, D_MODEL), f32), jax.ShapeDtypeStruct((p, D_MODEL), bf16),
                   jax.ShapeDtypeStruct((p, N_A), f32), jax.ShapeDtypeStruct((p, N_G), f32)],
        compiler_params=_cparams(("parallel",)),
    )(x, metablk, g, b, w_in)


def _hgrn_chunk(valid, st, hq, hf, hi, hg, lbraw, ng):
    lb = _sigmoid(lbraw[0:1] - lbraw[1:2])
    q = hq * _sigmoid(hq)
    fg = lb + (1.0 - lb) * _sigmoid(hf)
    logf = jnp.where(valid, jnp.log(fg), 0.0)
    k = jnp.where(valid, 1.0 - fg, 0.0)
    v = hi
    r = lax.broadcasted_iota(jnp.int32, (BLOCK, BLOCK), 0)
    c = lax.broadcasted_iota(jnp.int32, (BLOCK, BLOCK), 1)
    tril = (c <= r).astype(f32)
    bcum = jnp.dot(tril, logf, precision=lax.Precision.HIGHEST, preferred_element_type=f32)
    blast = bcum[BLOCK - 1:BLOCK]
    rows = lax.broadcasted_iota(jnp.int32, (BLOCK, 1), 0)
    sub8 = lax.broadcasted_iota(jnp.int32, (BLOCK // 8, 8, HG_K), 1)
    b8 = bcum.reshape(BLOCK // 8, 8, HG_K)
    row_of_8 = lambda j: jnp.broadcast_to(b8[:, j:j + 1, :], b8.shape)
    a = jnp.where(r == c, jnp.sum(q * k, axis=-1, keepdims=True), 0.0)
    seg = BLOCK
    while seg >= 2:
        half = seg // 2
        if seg >= 8:
            bs = bcum.reshape(BLOCK // seg, seg, HG_K)
            ref = jnp.broadcast_to(bs[:, half - 1:half, :], bs.shape)
        elif seg == 4:
            ref = jnp.where(sub8 < 4, row_of_8(1), row_of_8(5))
        else:
            ref = jnp.where(sub8 < 2, row_of_8(0), jnp.where(sub8 < 4, row_of_8(2),
                                                             jnp.where(sub8 < 6, row_of_8(4), row_of_8(6))))
        ref = ref.reshape(BLOCK, HG_K)
        upper = (rows % seg) >= half
        q_up = q * jnp.exp(jnp.where(upper, bcum - ref, -jnp.inf))
        k_lo = k * jnp.exp(jnp.where(upper, -jnp.inf, ref - bcum))
        a = a + jnp.where((r // seg) == (c // seg), mm_nt(q_up, k_lo), 0.0)
        seg = half
    o = mm_nt(q * jnp.exp(bcum), st) + mm(a, v)
    st_new = st * jnp.exp(blast) + mm_tn(v, k * jnp.exp(blast - bcum))
    on = o * lax.rsqrt(jnp.mean(o * o, axis=-1, keepdims=True) + EPS) * ng
    return st_new, on * (hg * _sigmoid(hg))


def _hgrn_in_specs(rowmap):
    wide = lambda col: pl.BlockSpec((BLOCK, HG_HEADS * HG_K), lambda i: (rowmap(i) + LEAD, col))
    return [wide(0), wide(1), wide(2), wide(3), _const_spec((2, HG_HEADS * HG_K)), _const_spec((1, HG_K))]


def _head(ref, h):
    return ref[:, h * HG_K:(h + 1) * HG_K]


def hgrn_fwd(pa, lbraw, ng, nb, shards):
    n = len(shards)

    def body(hq_ref, hf_ref, hi_ref, hg_ref, lb_ref, ng_ref, *rest):
        srcs, (og_ref, sp_ref), dsts = rest[:n], rest[n:n + 2], rest[n + 2:2 * n + 2]
        st_ref = rest[2 * n + 2]
        start, wait = _shard_push(srcs, dsts, *rest[2 * n + 3:])
        i = pl.program_id(0)

        @pl.when(i == 0)
        def _():
            st_ref[...] = jnp.zeros_like(st_ref)
            start()

        @pl.when(i == nb - 1)
        def _():
            wait()

        valid = _row_ids(i) >= PAD
        for h in range(HG_HEADS):
            st = st_ref[h]
            sp_ref[0, h] = st
            st_new, out = _hgrn_chunk(valid, st, _head(hq_ref, h), _head(hf_ref, h), _head(hi_ref, h),
                                      _head(hg_ref, h), _head(lb_ref, h), ng_ref[...])
            st_ref[h] = st_new
            og_ref[:, h * HG_K:(h + 1) * HG_K] = out.astype(bf16)

    p = (nb + LEAD) * BLOCK
    push_in, push_out, push_shape, push_scratch = _push_specs(shards)
    return pl.pallas_call(
        body, name="hgrn_fwd", grid=(nb,),
        in_specs=_hgrn_in_specs(lambda i: i) + push_in,
        out_specs=[pl.BlockSpec((BLOCK, HG_HEADS * HG_K), lambda i: (i + LEAD, 0)),
                   pl.BlockSpec((1, HG_HEADS, HG_K, HG_K), lambda i: (i, 0, 0, 0))] + push_out,
        out_shape=[jax.ShapeDtypeStruct((p, HG_HEADS * HG_K), bf16),
                   jax.ShapeDtypeStruct((nb, HG_HEADS, HG_K, HG_K), f32)] + push_shape,
        scratch_shapes=[pltpu.VMEM((HG_HEADS, HG_K, HG_K), f32)] + push_scratch,
        compiler_params=_cparams(("arbitrary",)),
    )(pa, pa, pa, pa, lbraw, ng, *shards)


def hgrn_bwd(pa, lbraw, ng, sprev, dog, nb, grads):
    n = len(grads)

    def body(hq_ref, hf_ref, hi_ref, hg_ref, lb_ref, ng_ref, sp_ref, do_ref, *rest):
        srcs, rest = rest[:n], rest[n:]
        dq_ref, df_ref, di_ref, dg_ref, dlb_ref, dng_ref = rest[:6]
        dsts, dst_ref = rest[6:6 + n], rest[6 + n]
        start, wait = _grad_push(srcs, dsts, *rest[7 + n:])
        i = pl.program_id(0)

        @pl.when(i == 0)
        def _():
            dst_ref[...] = jnp.zeros_like(dst_ref)
            dlb_ref[...] = jnp.zeros_like(dlb_ref)
            dng_ref[...] = jnp.zeros_like(dng_ref)
            start()

        valid = _row_ids(nb - 1 - i) >= PAD
        dng_sum = jnp.zeros((1, HG_K), f32)
        for h in range(HG_HEADS):
            cols = slice(h * HG_K, (h + 1) * HG_K)
            _, vjp = jax.vjp(functools.partial(_hgrn_chunk, valid), sp_ref[0, h], _head(hq_ref, h), _head(hf_ref, h),
                             _head(hi_ref, h), _head(hg_ref, h), _head(lb_ref, h), ng_ref[...])
            dst, dq, df, di, dg, dlb, dng = vjp((dst_ref[h], _head(do_ref, h)))
            dst_ref[h] = dst
            dq_ref[:, cols] = dq.astype(bf16)
            df_ref[:, cols] = df.astype(bf16)
            di_ref[:, cols] = di.astype(bf16)
            dg_ref[:, cols] = dg.astype(bf16)
            dlb_ref[:, cols] += dlb
            dng_sum = dng_sum + dng
        dng_ref[...] += dng_sum
        pl.when(i == nb - 1)(wait)

    p = (nb + LEAD) * BLOCK
    rev = lambda i: nb - 1 - i
    hw = HG_HEADS * HG_K
    blk = pl.BlockSpec((BLOCK, hw), lambda i: (rev(i) + LEAD, 0))
    wide = jax.ShapeDtypeStruct((p, hw), bf16)
    push_in, push_out, push_shape, push_scratch = _grad_push_specs(grads)
    return pl.pallas_call(
        body, name="hgrn_bwd", grid=(nb,),
        in_specs=_hgrn_in_specs(rev) + [pl.BlockSpec((1, HG_HEADS, HG_K, HG_K), lambda i: (rev(i), 0, 0, 0)), blk]
        + push_in,
        out_specs=[blk, blk, blk, blk, pl.BlockSpec((2, hw), lambda i: (0, 0)), pl.BlockSpec((1, HG_K), lambda i: (0, 0))]
        + push_out,
        out_shape=[wide, wide, wide, wide, jax.ShapeDtypeStruct((2, hw), f32), jax.ShapeDtypeStruct((1, HG_K), f32)]
        + push_shape,
        scratch_shapes=[pltpu.VMEM((HG_HEADS, HG_K, HG_K), f32)] + push_scratch,
        compiler_params=_cparams(("arbitrary",)),
    )(pa, pa, pa, pa, lbraw, ng, sprev, dog, *grads)


def _rope(x, cos, sin):
    lane = lax.broadcasted_iota(jnp.int32, x.shape, 1)
    rot = jnp.where(lane % HEAD_DIM < HEAD_DIM // 2, -roll_lanes(x, BLOCK - HEAD_DIM // 2),
                    roll_lanes(x, HEAD_DIM // 2))
    return x * cos + rot * sin


def _both_halves(x, g):
    lo = lax.broadcasted_iota(jnp.int32, x.shape, 1) < HEAD_DIM
    sw = roll_lanes(x, HEAD_DIM)
    return jnp.where(lo, x, sw) if g == 0 else jnp.where(lo, sw, x)


def _attn_block(band_ok, meta_ok, tabs, q, kp, kc, vp, vc, km, vm, *sinks):
    cq, sq, cp, sp, cm, sm = tabs
    neg = jnp.finfo(f32).min
    scale = HEAD_DIM ** -0.5
    group = ATT_HEADS // 2
    kpr, kcr, kmr = _rope(kp, cp, sp), _rope(kc, cq, sq), _rope(km, cm, sm)
    lo = lax.broadcasted_iota(jnp.int32, (BLOCK, BLOCK), 1) < HEAD_DIM
    t = lax.broadcasted_iota(jnp.int32, (group * BLOCK, BLOCK), 0) % BLOCK
    own = lax.broadcasted_iota(jnp.int32, (group * BLOCK, BLOCK), 1) <= t
    qr = [_rope(q[:, m * BLOCK:(m + 1) * BLOCK], cq, sq) for m in range(ATT_HEADS // 2)]
    slabs = []
    for g in range(2):
        kp_g, kc_g, vp_g, vc_g, km_g, vm_g = [_both_halves(a, g) for a in (kpr, kcr, vp, vc, kmr, vm)]
        qs = jnp.concatenate([jnp.where(lo if h % 2 == 0 else ~lo, qr[2 * g + h // 2], 0.0) for h in range(group)],
                             axis=0)
        sink = jnp.concatenate([jnp.broadcast_to(sinks[group * g + h], (BLOCK, 1)) for h in range(group)], axis=0)
        sb = jnp.where(band_ok, jnp.where(own, mm_nt(qs, kc_g), mm_nt(qs, kp_g)) * scale, neg)
        sme = jnp.where(meta_ok, mm_nt(qs, km_g) * scale, neg)
        mx = lax.stop_gradient(jnp.maximum(jnp.maximum(jnp.max(sb, axis=-1, keepdims=True),
                                                       jnp.max(sme, axis=-1, keepdims=True)), sink))
        eb, em = jnp.exp(sb - mx), jnp.exp(sme - mx)
        inv = 1.0 / (jnp.sum(eb, axis=-1, keepdims=True) + jnp.sum(em, axis=-1, keepdims=True) + jnp.exp(sink - mx))
        pb = eb * inv
        o = mm(jnp.where(own, pb, 0.0), vc_g) + mm(jnp.where(own, 0.0, pb), vp_g) + mm(em * inv, vm_g)
        for m in range(2):
            slabs.append(jnp.where(lo, o[2 * m * BLOCK:(2 * m + 1) * BLOCK], o[(2 * m + 1) * BLOCK:(2 * m + 2) * BLOCK]))
    return jnp.concatenate(slabs, axis=1)


def _attn_masks(i):
    group = ATT_HEADS // 2
    t = lax.broadcasted_iota(jnp.int32, (group * BLOCK, BLOCK), 0) % BLOCK
    s = lax.broadcasted_iota(jnp.int32, (group * BLOCK, BLOCK), 1)
    kpos = jnp.where(s <= t, i * BLOCK - PAD + s, jnp.where(i > 0, (i - 1) * BLOCK - PAD + s, -1))
    band_ok = kpos >= N_META
    qpos = i * BLOCK - PAD + lax.broadcasted_iota(jnp.int32, (group * BLOCK, 1), 0) % BLOCK
    meta_ok = lax.broadcasted_iota(jnp.int32, (1, N_META), 1) <= qpos
    return band_ok, meta_ok


def _attn_in_specs(cur=lambda i: i):
    prev = lambda i: jnp.maximum(cur(i) - 1, 0)
    kcol, vcol = N_A // BLOCK - 2, N_A // BLOCK - 1
    blk = lambda rowmap, col: pl.BlockSpec((BLOCK, BLOCK), lambda i: (rowmap(i) + LEAD, col))
    tab = lambda rowmap: pl.BlockSpec((BLOCK, BLOCK), lambda i: (rowmap(i), 0))
    first = lambda i: 0
    return [pl.BlockSpec((BLOCK, ATT_QW), lambda i: (cur(i) + LEAD, 4)),
            blk(prev, kcol), blk(cur, kcol), blk(prev, vcol), blk(cur, vcol), blk(first, kcol), blk(first, vcol),
            tab(cur), tab(cur), tab(prev), tab(prev), tab(first), tab(first),
            _const_spec((ATT_HEADS, BLOCK))]


def _attn_row(n, cur=lambda i: i):
    return pl.BlockSpec((BLOCK, n), lambda i: (cur(i) + LEAD, 0))


def _attn_operands(q_ref, kp_ref, kc_ref, vp_ref, vc_ref, km_ref, vm_ref, cq, sq, cp, sp, cm, sm, sk_ref):
    tabs = (cq[...], sq[...], cp[...], sp[...], cm[PAD:, :], sm[PAD:, :])
    args = (q_ref[...], kp_ref[...], kc_ref[...], vp_ref[...], vc_ref[...], km_ref[PAD:, :], vm_ref[PAD:, :])
    sinks = tuple(sk_ref[j:j + 1, 0:1] for j in range(ATT_HEADS))
    return tabs, args + sinks


def attn_fwd(pa, cos, sin, sinks8, nb, shards):
    n = len(shards)
    n_in = 14

    def body(*refs):
        srcs, o_ref, dsts = refs[n_in:n_in + n], refs[n_in + n], refs[n_in + n + 1:n_in + 2 * n + 1]
        start, wait = _shard_push(srcs, dsts, *refs[n_in + 2 * n + 1:])
        i = pl.program_id(0)
        pl.when(i == 0)(start)
        band_ok, meta_ok = _attn_masks(i)
        tabs, args = _attn_operands(*refs[:n_in])
        o_ref[...] = _attn_block(band_ok, meta_ok, tabs, *args).astype(bf16)
        pl.when(i == nb - 1)(wait)

    push_in, push_out, push_shape, push_scratch = _push_specs(shards)
    return pl.pallas_call(
        body, name="attn_fwd", grid=(nb,), in_specs=_attn_in_specs() + push_in,
        out_specs=[_attn_row(ATT_QW)] + push_out,
        out_shape=[jax.ShapeDtypeStruct(((nb + LEAD) * BLOCK, ATT_QW), bf16)] + push_shape,
        scratch_shapes=push_scratch,
        compiler_params=_cparams(("arbitrary",)),
    )(pa, pa, pa, pa, pa, pa, pa, cos, sin, cos, sin, cos, sin, sinks8, *shards)


def attn_bwd(pa, cos, sin, sinks8, do, nb, grads):
    n = len(grads)

    def body(*refs):
        do_ref, srcs = refs[14], refs[15:15 + n]
        dq_ref, dkc_ref, dkp_ref, dvc_ref, dvp_ref, dkm_ref, dvm_ref, dsk_ref = refs[15 + n:23 + n]
        start, wait = _grad_push(srcs, refs[23 + n:23 + 2 * n], *refs[23 + 2 * n:])
        i = pl.program_id(0)

        @pl.when(i == 0)
        def _():
            dkm_ref[...] = jnp.zeros((N_META, BLOCK), f32)
            dvm_ref[...] = jnp.zeros((N_META, BLOCK), f32)
            dsk_ref[...] = jnp.zeros((ATT_HEADS, BLOCK), f32)
            start()

        band_ok, meta_ok = _attn_masks(i)
        tabs, args = _attn_operands(*refs[:14])
        _, vjp = jax.vjp(functools.partial(_attn_block, band_ok, meta_ok, tabs), *args)
        grads = vjp(do_ref[...])
        dq_ref[...] = grads[0].astype(bf16)
        dkp_ref[...] = grads[1]
        dkc_ref[...] = grads[2]
        dvp_ref[...] = grads[3]
        dvc_ref[...] = grads[4]
        dkm_ref[...] += grads[5]
        dvm_ref[...] += grads[6]
        for j in range(ATT_HEADS):
            dsk_ref[j:j + 1, :] += jnp.broadcast_to(grads[7 + j], (1, BLOCK))
        pl.when(i == nb - 1)(wait)

    p = (nb + LEAD) * BLOCK
    row = _attn_row(BLOCK)
    const = lambda r: pl.BlockSpec((r, BLOCK), lambda i: (0, 0))
    part = jax.ShapeDtypeStruct((p, BLOCK), f32)
    push_in, push_out, push_shape, push_scratch = _grad_push_specs(grads)
    return pl.pallas_call(
        body, name="attn_bwd", grid=(nb,),
        in_specs=_attn_in_specs() + [_attn_row(ATT_QW)] + push_in,
        out_specs=[_attn_row(ATT_QW), row, row, row, row,
                   const(N_META), const(N_META), const(ATT_HEADS)] + push_out,
        out_shape=[jax.ShapeDtypeStruct((p, ATT_QW), bf16), part, part, part, part,
                   jax.ShapeDtypeStruct((N_META, BLOCK), f32), jax.ShapeDtypeStruct((N_META, BLOCK), f32),
                   jax.ShapeDtypeStruct((ATT_HEADS, BLOCK), f32)] + push_shape,
        scratch_shapes=push_scratch,
        compiler_params=_cparams(("arbitrary",)),
    )(pa, pa, pa, pa, pa, pa, pa, cos, sin, cos, sin, cos, sin, sinks8, do, *grads)


def _mid_forward(h0_ref, pg_ref, og, oa, wbh_ref, wba_ref, wo_ref, g1, b1):
    yh = jnp.dot(og, wbh_ref[...], preferred_element_type=f32)
    ya = jnp.dot(oa, wba_ref[...], preferred_element_type=f32)
    gh = _sigmoid(pg_ref[:, :D_MODEL])
    ga = _sigmoid(pg_ref[:, D_MODEL:])
    mixin = (gh * yh + ga * ya).astype(bf16)
    r1 = ALPHA * h0_ref[...] + jnp.dot(mixin, wo_ref[...], preferred_element_type=f32)
    xh1, rs1 = _ln_stats(r1)
    return yh, ya, gh, ga, mixin, xh1, rs1, xh1 * g1 + b1


def _mid_weight_specs():
    hw = HG_HEADS * HG_K
    return [_const_spec((hw, D_MODEL)), _const_spec((ATT_QW, D_MODEL)), _const_spec((D_MODEL, D_MODEL)),
            _const_spec((1, D_MODEL)), _const_spec((1, D_MODEL))]


def mid_front(h0, pg, og, oatt, wbh, wba, wout, ln1g, ln1b):
    def body(h0_ref, pg_ref, og_ref, oa_ref, wbh_ref, wba_ref, wo_ref, g1_ref, b1_ref,
             h1_ref, h1b_ref, mix_ref, ogc_ref, oac_ref):
        used = _tm_rows(pl.program_id(0)) >= LEAD * BLOCK
        og = jnp.where(used, og_ref[...], jnp.zeros_like(og_ref))
        oa = jnp.where(used, oa_ref[...], jnp.zeros_like(oa_ref))
        ogc_ref[...] = og
        oac_ref[...] = oa
        *_, mixin, _, _, h1 = _mid_forward(h0_ref, pg_ref, og, oa, wbh_ref, wba_ref, wo_ref, g1_ref[...], b1_ref[...])
        mix_ref[...] = mixin
        h1_ref[...] = h1
        h1b_ref[...] = h1.astype(bf16)

    p = h0.shape[0]
    hw = HG_HEADS * HG_K
    sds = lambda n, dt: jax.ShapeDtypeStruct((p, n), dt)
    return pl.pallas_call(
        body, name="mid_front", grid=(p // TM,),
        in_specs=[_tm_row(D_MODEL), _tm_row(N_G), _tm_row(hw), _tm_row(ATT_QW)] + _mid_weight_specs(),
        out_specs=[_tm_row(D_MODEL), _tm_row(D_MODEL), _tm_row(D_MODEL), _tm_row(hw), _tm_row(ATT_QW)],
        out_shape=[sds(D_MODEL, f32), sds(D_MODEL, bf16), sds(D_MODEL, bf16), sds(hw, bf16), sds(ATT_QW, bf16)],
        compiler_params=_cparams(("parallel",)),
    )(h0, pg, og, oatt, wbh, wba, wout, ln1g, ln1b)


def mid_ffn(h1, target, wfi, wfo, ln2g, ln2b):
    def body(h1_ref, t_ref, wfi_ref, wfo_ref, g2_ref, b2_ref,
             dh1_ref, dau_ref, s_ref, dr2_ref, loss_ref, dg2_ref, db2_ref):
        i = pl.program_id(0)

        @pl.when(i == 0)
        def _():
            for r in (loss_ref, dg2_ref, db2_ref):
                r[...] = jnp.zeros_like(r)

        g2, b2 = g2_ref[...], b2_ref[...]
        h1 = h1_ref[...]
        au = jnp.dot(h1.astype(bf16), wfi_ref[...], preferred_element_type=f32)
        a, u = au[:, :D_FF], au[:, D_FF:]
        sg = _sigmoid(a)
        sa = a * sg
        s = (sa * u).astype(bf16)
        s_ref[...] = s
        r2 = ALPHA * h1 + jnp.dot(s, wfo_ref[...], preferred_element_type=f32)
        xh2, rs2 = _ln_stats(r2)
        diff = jnp.where(i > 0, xh2 * g2 + b2 - t_ref[...], 0.0)
        loss_ref[...] += jnp.sum(diff * diff) * (0.5 / D_MODEL)
        dy = diff * (1.0 / D_MODEL)
        dg2_ref[...] += jnp.sum(dy * xh2, axis=0, keepdims=True)
        db2_ref[...] += jnp.sum(dy, axis=0, keepdims=True)
        dr2 = _ln_bwd(dy, xh2, rs2, g2)
        dr2b = dr2.astype(bf16)
        dr2_ref[...] = dr2b
        ds = _dot(dr2b, wfo_ref[...], 1, 1)
        da = (ds * u) * (sg * (1.0 + a * (1.0 - sg)))
        du = ds * sa
        dau = jnp.concatenate([da, du], axis=1).astype(bf16)
        dau_ref[...] = dau
        dh1_ref[...] = ALPHA * dr2 + _dot(dau, wfi_ref[...], 1, 1)

    p = h1.shape[0]
    vec = lambda: pl.BlockSpec((1, D_MODEL), lambda i: (0, 0))
    sds = lambda n, dt: jax.ShapeDtypeStruct((p, n), dt)
    return pl.pallas_call(
        body, name="mid_ffn", grid=(p // TM,),
        in_specs=[_tm_row(D_MODEL), _tm_tokens(), _const_spec((D_MODEL, 2 * D_FF)), _const_spec((D_FF, D_MODEL)),
                  _const_spec((1, D_MODEL)), _const_spec((1, D_MODEL))],
        out_specs=[_tm_row(D_MODEL), _tm_row(2 * D_FF), _tm_row(D_FF), _tm_row(D_MODEL),
                   pl.BlockSpec((1, 1), lambda i: (0, 0)), vec(), vec()],
        out_shape=[sds(D_MODEL, f32), sds(2 * D_FF, bf16), sds(D_FF, bf16), sds(D_MODEL, bf16),
                   jax.ShapeDtypeStruct((1, 1), f32)] + [jax.ShapeDtypeStruct((1, D_MODEL), f32)] * 2,
        compiler_params=_cparams(("arbitrary",)),
    )(h1, target, wfi, wfo, ln2g, ln2b)


def mid_back(dh1, h0, pg, ogc, oac, wbh, wba, wout, ln1g, ln1b):
    def body(dh1_ref, h0_ref, pg_ref, og_ref, oa_ref, wbh_ref, wba_ref, wo_ref, g1_ref, b1_ref,
             dh0_ref, dpg_ref, dog_ref, doa_ref, dyh_ref, dya_ref, dr1_ref, dg1_ref, db1_ref):
        @pl.when(pl.program_id(0) == 0)
        def _():
            dg1_ref[...] = jnp.zeros_like(dg1_ref)
            db1_ref[...] = jnp.zeros_like(db1_ref)

        g1 = g1_ref[...]
        yh, ya, gh, ga, _, xh1, rs1, _ = _mid_forward(h0_ref, pg_ref, og_ref[...], oa_ref[...], wbh_ref, wba_ref,
                                                      wo_ref, g1, b1_ref[...])
        dh1 = dh1_ref[...]
        dg1_ref[...] += jnp.sum(dh1 * xh1, axis=0, keepdims=True)
        db1_ref[...] += jnp.sum(dh1, axis=0, keepdims=True)
        dr1 = _ln_bwd(dh1, xh1, rs1, g1)
        dr1b = dr1.astype(bf16)
        dr1_ref[...] = dr1b
        dh0_ref[...] = ALPHA * dr1
        dmix = _dot(dr1b, wo_ref[...], 1, 1)
        dyh = (dmix * gh).astype(bf16)
        dya = (dmix * ga).astype(bf16)
        dyh_ref[...] = dyh
        dya_ref[...] = dya
        dpg_ref[:, :D_MODEL] = (dmix * yh * gh * (1.0 - gh)).astype(bf16)
        dpg_ref[:, D_MODEL:] = (dmix * ya * ga * (1.0 - ga)).astype(bf16)
        dog_ref[...] = _dot(dyh, wbh_ref[...], 1, 1)
        doa_ref[...] = _dot(dya, wba_ref[...], 1, 1)

    p = h0.shape[0]
    hw = HG_HEADS * HG_K
    vec = lambda: pl.BlockSpec((1, D_MODEL), lambda i: (0, 0))
    sds = lambda n, dt: jax.ShapeDtypeStruct((p, n), dt)
    return pl.pallas_call(
        body, name="mid_back", grid=(p // TM,),
        in_specs=[_tm_row(D_MODEL), _tm_row(D_MODEL), _tm_row(N_G), _tm_row(hw), _tm_row(ATT_QW)] + _mid_weight_specs(),
        out_specs=[_tm_row(D_MODEL), _tm_row(N_G), _tm_row(hw), _tm_row(ATT_QW), _tm_row(D_MODEL), _tm_row(D_MODEL),
                   _tm_row(D_MODEL), vec(), vec()],
        out_shape=[sds(D_MODEL, f32), sds(N_G, bf16), sds(hw, f32), sds(ATT_QW, f32), sds(D_MODEL, bf16),
                   sds(D_MODEL, bf16), sds(D_MODEL, bf16)] + [jax.ShapeDtypeStruct((1, D_MODEL), f32)] * 2,
        compiler_params=_cparams(("arbitrary",)),
    )(dh1, h0, pg, ogc, oac, wbh, wba, wout, ln1g, ln1b)


def inproj_bwd(dh0p, dhq, dhf, dhi, dhg, daq, dkc, dkp, dvc, dvp, dkm, dvm, dpg, w_in, x, metablk, g, b):
    p = dh0p.shape[0]
    nbk = p // BLOCK
    per = TM // BLOCK

    def body(dh0_ref, dq_ref, df_ref, di_ref, dg_ref, daq_ref, dkc_ref, *rest):
        dkp_refs, dvc_ref, dvp_refs = rest[:per], rest[per], rest[per + 1:2 * per + 1]
        (dkm_ref, dvm_ref, dpg_ref, w_ref, x_ref, mb_ref, g_ref, b_ref,
         dproj_ref, dx_ref, dmeta_ref, dlg_ref, dlb_ref) = rest[2 * per + 1:]
        i = pl.program_id(0)

        @pl.when(i == 0)
        def _():
            dlg_ref[...] = jnp.zeros_like(dlg_ref)
            dlb_ref[...] = jnp.zeros_like(dlb_ref)

        zero_pad = jnp.zeros((TM - N_META, BLOCK), f32)
        first = i == 0
        rows = _tm_rows(i)

        def keys(cur_ref, next_refs, meta_ref):
            nxt = jnp.concatenate([jnp.where(per * i + 1 + m < nbk, next_refs[m][...], 0.0) for m in range(per)], axis=0)
            t = cur_ref[...] + nxt
            return t + jnp.where(first, jnp.concatenate([zero_pad, meta_ref[...]], axis=0), 0.0)

        dproj = jnp.concatenate(
            [dq_ref[...], df_ref[...], di_ref[...], dg_ref[...], daq_ref[...],
             keys(dkc_ref, dkp_refs, dkm_ref).astype(bf16), keys(dvc_ref, dvp_refs, dvm_ref).astype(bf16),
             dpg_ref[...]], axis=1)
        dproj = jnp.where(rows >= LEAD * BLOCK, dproj, jnp.zeros_like(dproj))
        dproj_ref[...] = dproj
        valid = rows >= TM - N_META
        dh0 = jnp.where(valid, dh0_ref[...] + _dot(dproj, w_ref[...], 1, 1), 0.0)
        xb = jnp.where(first, mb_ref[...], x_ref[...])
        xh, rs = _ln_stats(xb)
        dlg_ref[...] += jnp.sum(dh0 * xh, axis=0, keepdims=True)
        dlb_ref[...] += jnp.sum(dh0, axis=0, keepdims=True)
        dx = jnp.where(valid, _ln_bwd(dh0, xh, rs, g_ref[...]), 0.0)
        dx_ref[...] = dx

        @pl.when(first)
        def _():
            dmeta_ref[...] = dx[TM - N_META:, :]

    row = _tm_row
    nxt = [pl.BlockSpec((BLOCK, BLOCK), functools.partial(lambda i, m: (jnp.minimum(per * i + 1 + m, nbk - 1), 0), m=m))
           for m in range(per)]
    hw = HG_HEADS * HG_K
    vec = lambda: pl.BlockSpec((1, D_MODEL), lambda i: (0, 0))
    return pl.pallas_call(
        body, name="inproj_bwd", grid=(p // TM,),
        in_specs=[row(D_MODEL), row(hw), row(hw), row(hw), row(hw), row(ATT_QW),
                  row(BLOCK)] + nxt + [row(BLOCK)] + nxt + [_const_spec((N_META, BLOCK)), _const_spec((N_META, BLOCK)),
                  row(N_G), _const_spec((D_MODEL, IN_W)), _tm_tokens(),
                  _const_spec((TM, D_MODEL)), _const_spec((1, D_MODEL)), _const_spec((1, D_MODEL))],
        out_specs=[row(IN_W), _tm_tokens(), pl.BlockSpec((N_META, D_MODEL), lambda i: (0, 0)), vec(), vec()],
        out_shape=[jax.ShapeDtypeStruct((p, IN_W), bf16), jax.ShapeDtypeStruct((p - TM, D_MODEL), f32),
                   jax.ShapeDtypeStruct((N_META, D_MODEL), f32),
                   jax.ShapeDtypeStruct((1, D_MODEL), f32), jax.ShapeDtypeStruct((1, D_MODEL), f32)],
        compiler_params=_cparams(("arbitrary",)),
    )(dh0p, dhq, dhf, dhi, dhg, daq, dkc, *([dkp] * per), dvc, *([dvp] * per), dkm, dvm, dpg, w_in, x, metablk, g, b)


def wgrad(a, b, name, tk, tn, tp, by_cols, out_dtype=f32):
    p, k = a.shape
    n = b.shape[1]
    nsteps = p // tp

    def body(a_ref, b_ref, o_ref, acc_ref):
        ip = pl.program_id(2)

        @pl.when(ip == 0)
        def _():
            acc_ref[...] = jnp.zeros_like(acc_ref)

        acc_ref[...] += _dot(a_ref[...], b_ref[...], 0, 0)

        @pl.when(ip == nsteps - 1)
        def _():
            for j in range(span):
                o_ref[j] = acc_ref[:, j * width:(j + 1) * width].astype(out_dtype)

    span, width = 1, tn
    if by_cols:
        shard_n = n // N_SHARD
        out_shape = (N_SHARD, k, shard_n)
        if tn >= shard_n:
            span, width = tn // shard_n, shard_n
            omap = lambda ik, jn, ip: (jn, ik, 0)
        else:
            per = shard_n // tn
            omap = lambda ik, jn, ip: (jn // per, ik, jn % per)
    else:
        out_shape = (1, k, n)
        omap = lambda ik, jn, ip: (0, ik, jn)
    return pl.pallas_call(
        body, name=name, grid=(k // tk, n // tn, nsteps),
        in_specs=[pl.BlockSpec((tp, tk), lambda ik, jn, ip: (ip, ik)),
                  pl.BlockSpec((tp, tn), lambda ik, jn, ip: (ip, jn))],
        out_specs=pl.BlockSpec((span, tk, width), omap),
        out_shape=jax.ShapeDtypeStruct(out_shape, out_dtype),
        scratch_shapes=[pltpu.VMEM((tk, tn), f32)],
        compiler_params=_cparams(("parallel", "parallel", "arbitrary")),
    )(a, b)


def adamw(w, g, m, v, name):
    r, c = w.shape
    tr = r
    for cand in (256, 176, 128):
        if r > cand and r % cand == 0:
            tr = cand
            break

    def body(w_ref, g_ref, m_ref, v_ref, d_ref, mo_ref, vo_ref):
        gg = g_ref[...]
        mn = ADAM_B1 * m_ref[...] + (1.0 - ADAM_B1) * gg
        vn = ADAM_B2 * v_ref[...] + (1.0 - ADAM_B2) * (gg * gg)
        m_hat = mn / (1.0 - ADAM_B1 ** ADAM_STEP)
        v_hat = vn / (1.0 - ADAM_B2 ** ADAM_STEP)
        d_ref[...] = -ADAM_LR * (m_hat / (jnp.sqrt(v_hat) + ADAM_EPS) + ADAM_WD * w_ref[...])
        mo_ref[...] = mn
        vo_ref[...] = vn

    spec = pl.BlockSpec((tr, c), lambda i: (i, 0))
    sds = jax.ShapeDtypeStruct((r, c), f32)
    return pl.pallas_call(
        body, name=name, grid=(r // tr,), in_specs=[spec] * 4, out_specs=[spec] * 3, out_shape=[sds] * 3,
        compiler_params=_cparams(("parallel",)),
    )(w, g, m, v)


def _me():
    return lax.axis_index("x"), lax.axis_index("y"), lax.axis_index("c")


def _chip_peer(x, y, c, k):
    return (x ^ (k >> 1), y ^ (k & 1), c)


ANY = pl.BlockSpec(memory_space=pl.ANY)


def gather_weights(now, later):
    n, n_later = len(now), len(later)
    out_dtypes = [bf16 if s.size > 16 * 256 else f32 for s in now]
    halves = [(2, s.shape[0] // 2, s.shape[1]) for s in now]

    def body(*refs):
        ins, later_ins = refs[:n], refs[n:n + n_later]
        outs, later_outs = refs[n + n_later:2 * n + n_later], refs[2 * n + n_later:2 * (n + n_later)]
        stage = refs[2 * (n + n_later):3 * n + 2 * n_later]
        send_sems, recv_sems, pass_send_sems, pass_recv_sems, local_sems = refs[3 * n + 2 * n_later:]
        x, y, c = _me()
        j = 2 * x + y
        sibling = (x, y, 1 - c)

        def over_ici(w, k, slot):
            return pltpu.make_async_remote_copy(
                src_ref=stage[w].at[c], dst_ref=outs[w].at[slot, c], send_sem=send_sems.at[w, k - 1],
                recv_sem=recv_sems.at[w, k - 1], device_id=_chip_peer(x, y, c, k), device_id_type=MESH)

        def passed_on(w, k, half):
            return pltpu.make_async_remote_copy(
                src_ref=outs[w].at[j ^ k, half], dst_ref=outs[w].at[j ^ k, half], send_sem=pass_send_sems.at[w, k - 1],
                recv_sem=pass_recv_sems.at[w, k - 1], device_id=sibling, device_id_type=MESH)

        for w in range(n):
            stage[w][...] = ins[w][...].astype(out_dtypes[w]).reshape(halves[w])
        locs = []
        for w in range(n):
            loc = pltpu.make_async_copy(stage[w], outs[w].at[j], local_sems.at[w])
            loc.start()
            locs.append(loc)
            for k in (1, 2, 3):
                over_ici(w, k, j).start()
        for w in range(n_later):
            later_outs[w][...] = later_ins[w][...].astype(bf16)
        for w in range(n):
            for k in (1, 2, 3):
                over_ici(w, k, j ^ k).wait_recv()
                passed_on(w, k, c).start()
        for w in range(n):
            for k in (1, 2, 3):
                passed_on(w, k, 1 - c).wait_recv()
        for w in range(n):
            for k in (1, 2, 3):
                over_ici(w, k, j).wait_send()
                passed_on(w, k, c).wait_send()
        for loc in locs:
            loc.wait()

    vmem = pl.BlockSpec(memory_space=pltpu.VMEM)
    sem3 = pltpu.SemaphoreType.DMA((n, 3))
    return pl.pallas_call(
        body, name="gather_weights",
        in_specs=[vmem] * (n + n_later), out_specs=[ANY] * n + [vmem] * n_later,
        out_shape=[jax.ShapeDtypeStruct((N_SHARD,) + h, dt) for h, dt in zip(halves, out_dtypes)]
        + [jax.ShapeDtypeStruct(s.shape, bf16) for s in later],
        scratch_shapes=[pltpu.VMEM(h, dt) for h, dt in zip(halves, out_dtypes)]
        + [sem3, sem3, sem3, sem3, pltpu.SemaphoreType.DMA((n,))],
        compiler_params=pltpu.CompilerParams(vmem_limit_bytes=VMEM_LIMIT),
    )(*now, *later)


def _shard_push(srcs, dsts, send_sems, recv_sems, local_sems):
    def remote(w, k, slot):
        x, y, c = _me()
        return pltpu.make_async_remote_copy(
            src_ref=srcs[w], dst_ref=dsts[w].at[slot], send_sem=send_sems.at[w, k - 1],
            recv_sem=recv_sems.at[w, k - 1], device_id=_chip_peer(x, y, c, k), device_id_type=MESH)

    def local(w):
        x, y, _ = _me()
        return pltpu.make_async_copy(srcs[w], dsts[w].at[2 * x + y], local_sems.at[w])

    def start():
        x, y, _ = _me()
        for w in range(len(srcs)):
            local(w).start()
            for k in (1, 2, 3):
                remote(w, k, 2 * x + y).start()

    def wait():
        x, y, _ = _me()
        for w in range(len(srcs)):
            for k in (1, 2, 3):
                remote(w, k, (2 * x + y) ^ k).wait_recv()
        for w in range(len(srcs)):
            for k in (1, 2, 3):
                remote(w, k, 2 * x + y).wait_send()
            local(w).wait()

    return start, wait


def _grad_push(srcs, dsts, send_sems, recv_sems):
    def copy(w, k):
        x, y, c = _me()
        px, py, pc = x ^ (k >> 2), y ^ ((k >> 1) & 1), c ^ (k & 1)
        return pltpu.make_async_remote_copy(
            src_ref=srcs[w].at[2 * px + py, pc], dst_ref=dsts[w].at[k - 1], send_sem=send_sems.at[w, k - 1],
            recv_sem=recv_sems.at[w, k - 1], device_id=(px, py, pc), device_id_type=MESH)

    def start():
        for w in range(len(srcs)):
            for k in range(1, N_DEV):
                copy(w, k).start()

    def wait():
        for w in range(len(srcs)):
            for k in range(1, N_DEV):
                copy(w, k).wait_recv()
        for w in range(len(srcs)):
            for k in range(1, N_DEV):
                copy(w, k).wait_send()

    return start, wait


def _grad_push_specs(grads):
    n = len(grads)
    return ([ANY] * n, [ANY] * n, [jax.ShapeDtypeStruct((N_DEV - 1,) + g.shape[2:], g.dtype) for g in grads],
            [pltpu.SemaphoreType.DMA((n, N_DEV - 1)), pltpu.SemaphoreType.DMA((n, N_DEV - 1))])


def add_eight(own, parts, jc_idx, name):
    _, half, c = parts.shape
    tr = half // 2 if (half // 2) % 16 == 0 else half

    def body(jc_ref, own_ref, p_ref, out_ref):
        acc = own_ref[0, 0].astype(f32)
        for k in range(N_DEV - 1):
            acc = acc + p_ref[k].astype(f32)
        out_ref[0] = acc

    return pl.pallas_call(
        body, name=name,
        grid_spec=pltpu.PrefetchScalarGridSpec(
            num_scalar_prefetch=1, grid=(half // tr,),
            in_specs=[pl.BlockSpec((1, 1, tr, c), lambda t, jc: (jc[0], jc[1], t, 0)),
                      pl.BlockSpec((N_DEV - 1, tr, c), lambda t, jc: (0, t, 0))],
            out_specs=pl.BlockSpec((1, tr, c), lambda t, jc: (jc[1], t, 0))),
        out_shape=jax.ShapeDtypeStruct((2, half, c), f32),
        compiler_params=_cparams(("parallel",)),
    )(jc_idx, own, parts)


def _push_specs(shards):
    n = len(shards)
    return ([ANY] * n, [ANY] * n, [jax.ShapeDtypeStruct((N_SHARD,) + s.shape, s.dtype) for s in shards],
            [pltpu.SemaphoreType.DMA((n, 3)), pltpu.SemaphoreType.DMA((n, 3)), pltpu.SemaphoreType.DMA((n,))])


def pair_exchange_halves(grads, small):
    n = len(grads)

    def body(*refs):
        ins, small_ref = refs[:n], refs[n]
        outs, gath = refs[n + 1:2 * n + 1], refs[2 * n + 1]
        send_sems, recv_sems, s_send, s_recv, local_sem = refs[2 * n + 2:]
        x, y, c = _me()
        me = 4 * x + 2 * y + c
        sends = []
        for w in range(n):
            half = ins[w].shape[1] // 2
            cp = pltpu.make_async_remote_copy(
                src_ref=ins[w].at[:, pl.ds((1 - c) * half, half), :], dst_ref=outs[w],
                send_sem=send_sems.at[w], recv_sem=recv_sems.at[w], device_id=(x, y, 1 - c), device_id_type=MESH)
            cp.start()
            sends.append(cp)
        loc = pltpu.make_async_copy(small_ref, gath.at[me], local_sem)
        loc.start()
        for k in range(1, N_DEV):
            cp = pltpu.make_async_remote_copy(
                src_ref=small_ref, dst_ref=gath.at[me], send_sem=s_send.at[k - 1], recv_sem=s_recv.at[k - 1],
                device_id=(x ^ (k >> 2), y ^ ((k >> 1) & 1), c ^ (k & 1)), device_id_type=MESH)
            cp.start()
            sends.append(cp)
        for w in range(n):
            half = ins[w].shape[1] // 2
            pltpu.make_async_remote_copy(
                src_ref=ins[w].at[:, pl.ds(0, half), :], dst_ref=outs[w], send_sem=send_sems.at[w],
                recv_sem=recv_sems.at[w], device_id=(x, y, 1 - c), device_id_type=MESH).wait_recv()
        for k in range(1, N_DEV):
            pltpu.make_async_remote_copy(
                src_ref=small_ref, dst_ref=gath.at[me ^ k], send_sem=s_send.at[k - 1], recv_sem=s_recv.at[k - 1],
                device_id=(x ^ (k >> 2), y ^ ((k >> 1) & 1), c ^ (k & 1)), device_id_type=MESH).wait_recv()
        for cp in sends:
            cp.wait_send()
        loc.wait()

    return pl.pallas_call(
        body, name="pair_exchange_halves", in_specs=[ANY] * (n + 1), out_specs=[ANY] * (n + 1),
        out_shape=[jax.ShapeDtypeStruct((g.shape[0], g.shape[1] // 2, g.shape[2]), f32) for g in grads]
        + [jax.ShapeDtypeStruct((N_DEV,) + small.shape, f32)],
        scratch_shapes=[pltpu.SemaphoreType.DMA((n,)), pltpu.SemaphoreType.DMA((n,)),
                        pltpu.SemaphoreType.DMA((N_DEV - 1,)), pltpu.SemaphoreType.DMA((N_DEV - 1,)),
                        pltpu.SemaphoreType.DMA],
    )(*grads, small)


def chip_exchange(sums):
    n = len(sums)

    def body(*refs):
        ins, outs = refs[:n], refs[n:2 * n]
        send_sems, recv_sems = refs[2 * n:]
        x, y, c = _me()
        j = 2 * x + y
        sends = []
        for w in range(n):
            for k in (1, 2, 3):
                cp = pltpu.make_async_remote_copy(
                    src_ref=ins[w].at[j ^ k], dst_ref=outs[w].at[k - 1], send_sem=send_sems.at[w, k - 1],
                    recv_sem=recv_sems.at[w, k - 1], device_id=_chip_peer(x, y, c, k), device_id_type=MESH)
                cp.start()
                sends.append(cp)
        for w in range(n):
            for k in (1, 2, 3):
                pltpu.make_async_remote_copy(
                    src_ref=ins[w].at[0], dst_ref=outs[w].at[k - 1], send_sem=send_sems.at[w, k - 1],
                    recv_sem=recv_sems.at[w, k - 1], device_id=_chip_peer(x, y, c, k), device_id_type=MESH).wait_recv()
        for cp in sends:
            cp.wait_send()

    return pl.pallas_call(
        body, name="chip_exchange", in_specs=[ANY] * n, out_specs=[ANY] * n,
        out_shape=[jax.ShapeDtypeStruct((N_SHARD - 1,) + s.shape[1:], s.dtype) for s in sums],
        scratch_shapes=[pltpu.SemaphoreType.DMA((n, 3)), pltpu.SemaphoreType.DMA((n, 3))],
    )(*sums)


def pair_exchange_results(halves):
    n = len(halves)

    def body(*refs):
        ins, outs = refs[:n], refs[n:2 * n]
        send_sems, recv_sems = refs[2 * n:]
        x, y, c = _me()
        sends = []
        for w in range(n):
            cp = pltpu.make_async_remote_copy(
                src_ref=ins[w].at[c], dst_ref=outs[w].at[c], send_sem=send_sems.at[w], recv_sem=recv_sems.at[w],
                device_id=(x, y, 1 - c), device_id_type=MESH)
            cp.start()
            sends.append(cp)
        for w in range(n):
            pltpu.make_async_remote_copy(
                src_ref=ins[w].at[c], dst_ref=outs[w].at[1 - c], send_sem=send_sems.at[w],
                recv_sem=recv_sems.at[w], device_id=(x, y, 1 - c), device_id_type=MESH).wait_recv()
        for cp in sends:
            cp.wait_send()

    return pl.pallas_call(
        body, name="pair_exchange_results", in_specs=[ANY] * n, out_specs=[ANY] * n,
        out_shape=[jax.ShapeDtypeStruct(h.shape, f32) for h in halves],
        input_output_aliases={w: w for w in range(n)},
        scratch_shapes=[pltpu.SemaphoreType.DMA((n,)), pltpu.SemaphoreType.DMA((n,))],
    )(*halves)


def add_pair(grad, other, c_idx, name):
    _, r, c = grad.shape
    half = r // 2
    tr = half // 2 if (half // 2) % 8 == 0 else half
    per = half // tr

    def body(c_ref, g_ref, o_ref, out_ref):
        out_ref[...] = (g_ref[...] + o_ref[...]).astype(bf16)

    return pl.pallas_call(
        body, name=name,
        grid_spec=pltpu.PrefetchScalarGridSpec(
            num_scalar_prefetch=1, grid=(N_SHARD, per),
            in_specs=[pl.BlockSpec((1, tr, c), lambda j, t, cr: (j, cr[0] * per + t, 0)),
                      pl.BlockSpec((1, tr, c), lambda j, t, cr: (j, t, 0))],
            out_specs=pl.BlockSpec((1, tr, c), lambda j, t, cr: (j, t, 0))),
        out_shape=jax.ShapeDtypeStruct((N_SHARD, half, c), bf16),
        compiler_params=_cparams(("parallel", "parallel")),
    )(c_idx, grad, other)


def add_four(own, parts, jc_idx, name):
    _, half, c = parts.shape
    tr = half // 2 if (half // 2) % 8 == 0 else half

    def body(jc_ref, own_ref, p_ref, out_ref):
        acc = own_ref[0].astype(f32)
        for k in range(N_SHARD - 1):
            acc = acc + p_ref[k].astype(f32)
        out_ref[0] = acc

    return pl.pallas_call(
        body, name=name,
        grid_spec=pltpu.PrefetchScalarGridSpec(
            num_scalar_prefetch=1, grid=(half // tr,),
            in_specs=[pl.BlockSpec((1, tr, c), lambda t, jc: (jc[0], t, 0)),
                      pl.BlockSpec((N_SHARD - 1, tr, c), lambda t, jc: (0, t, 0))],
            out_specs=pl.BlockSpec((1, tr, c), lambda t, jc: (jc[1], t, 0))),
        out_shape=jax.ShapeDtypeStruct((2, half, c), f32),
        compiler_params=_cparams(("parallel",)),
    )(jc_idx, own, parts)


def sum_devices(gathered):
    def body(g_ref, out_ref):
        acc = g_ref[0]
        for d in range(1, N_DEV):
            acc = acc + g_ref[d]
        out_ref[...] = acc

    return pl.pallas_call(body, name="sum_devices", out_shape=jax.ShapeDtypeStruct(gathered.shape[1:], f32))(gathered)


def _rows128(a, rows):
    flat = a.reshape(-1, BLOCK) if a.size % BLOCK == 0 else jnp.pad(a.reshape(1, -1), ((0, 0), (0, BLOCK - a.size)))
    return jnp.pad(flat, ((0, rows - flat.shape[0]), (0, 0)))


def kernel(x, meta_tokens, ln_emb_g, ln_emb_b, w_in, hg_lower_bounds, hg_norm_g, attn_sinks, w_branch_hg, w_branch_attn, w_out, ln1_g, ln1_b, w_ffn_in, w_ffn_out, ln2_g, ln2_b, loss_target, m_meta_tokens, m_ln_emb_g, m_ln_emb_b, m_w_in, m_hg_lower_bounds, m_hg_norm_g, m_attn_sinks, m_w_branch_hg, m_w_branch_attn, m_w_out, m_ln1_g, m_ln1_b, m_w_ffn_in, m_w_ffn_out, m_ln2_g, m_ln2_b, v_meta_tokens, v_ln_emb_g, v_ln_emb_b, v_w_in, v_hg_lower_bounds, v_hg_norm_g, v_attn_sinks, v_w_branch_hg, v_w_branch_attn, v_w_out, v_ln1_g, v_ln1_b, v_w_ffn_in, v_w_ffn_out, v_ln2_g, v_ln2_b):
    seq = x.shape[1]
    nb = seq // BLOCK + 1
    xs = x[0]
    ts = loss_target[0]
    ix, iy, ic = _me()
    shard = 2 * ix + iy
    vec = lambda a: a.reshape(1, D_MODEL)

    g_in, g_meta, s_bh, s_ba, s_out, s_fi, s_fo = gather_weights(
        [w_in[0], meta_tokens], [w_branch_hg[0], w_branch_attn[0], w_out[0], w_ffn_in[0], w_ffn_out[0]])
    by_cols = lambda g: g.reshape(N_SHARD, -1, g.shape[-1]).transpose(1, 0, 2).reshape(-1, N_SHARD * g.shape[-1])
    wf_in = by_cols(g_in)
    metablk = jnp.pad(by_cols(g_meta), ((TM - N_META, 0), (0, 0)))

    pos = jnp.arange(nb * BLOCK, dtype=jnp.int32) - PAD
    half = HEAD_DIM // 2
    inv = ROPE_THETA ** (-jnp.arange(half, dtype=f32) / half)
    ang = pos.astype(f32)[:, None] * inv[None, :]
    cos = jnp.tile(jnp.cos(ang), (1, BLOCK // half))
    sin = jnp.tile(jnp.sin(ang), (1, BLOCK // half))
    sinks8 = jnp.broadcast_to(attn_sinks.reshape(ATT_HEADS, 1), (ATT_HEADS, BLOCK))
    ng = hg_norm_g.reshape(1, HG_K)

    h0, h0b, pa, pg = emb_inproj(xs, metablk, vec(ln_emb_g), vec(ln_emb_b), wf_in)
    og, sprev, g_fi = hgrn_fwd(pa, hg_lower_bounds, ng, nb, [s_fi])
    oatt, g_fo, g_out, g_bh, g_ba = attn_fwd(pa, cos, sin, sinks8, nb, [s_fo, s_out, s_bh, s_ba])
    wf_bh, wf_ba, wf_fi = by_cols(g_bh), by_cols(g_ba), by_cols(g_fi)
    wf_out = g_out.reshape(D_MODEL, D_MODEL)
    wf_fo = g_fo.reshape(D_FF, D_MODEL)
    h1, h1b, mixin, og, oatt = mid_front(h0, pg, og, oatt, wf_bh, wf_ba, wf_out, ln1_g, ln1_b)
    dh1, dau, sact, dr2, loss_part, dg2, db2 = mid_ffn(h1, ts, wf_fi, wf_fo, ln2_g, ln2_b)
    dh0p, dpg, dog, doa, dyh, dya, dr1, dg1, db1 = mid_back(dh1, h0, pg, og, oatt, wf_bh, wf_ba, wf_out, ln1_g, ln1_b)
    steps = h0.shape[0] // TM
    tp = TM * max(k for k in (3, 2, 1) if steps % k == 0)
    pieces = lambda g: g.reshape(N_SHARD, 2, -1, g.shape[-1])
    gb_bh = pieces(wgrad(og, dyh, "wgrad_bh", 512, D_MODEL, tp, True, bf16))
    gb_ba = pieces(wgrad(oatt, dya, "wgrad_ba", 512, D_MODEL, tp, True, bf16))
    gb_out = pieces(wgrad(mixin, dr1, "wgrad_out", D_MODEL, D_MODEL, tp, False, bf16))
    gb_fi = pieces(wgrad(h1b, dau, "wgrad_fi", D_MODEL, D_FF, tp, True, bf16))
    gb_fo = pieces(wgrad(sact, dr2, "wgrad_fo", D_FF // 2, D_MODEL, tp, False, bf16))
    dhq, dhf, dhi, dhg, dlb4, dng, r_fi, r_fo = hgrn_bwd(pa, hg_lower_bounds, ng, sprev, dog, nb, [gb_fi, gb_fo])
    daq, dkc, dkp, dvc, dvp, dkm, dvm, dsk, r_out, r_bh, r_ba = attn_bwd(pa, cos, sin, sinks8, doa, nb,
                                                                         [gb_out, gb_bh, gb_ba])
    dproj, dx, dmeta, dlg, dlb = inproj_bwd(dh0p, dhq, dhf, dhi, dhg, daq, dkc, dkp, dvc, dvp, dkm, dvm, dpg,
                                      wf_in, xs, metablk, vec(ln_emb_g), vec(ln_emb_b))
    gw_in = wgrad(h0b, dproj, "wgrad_in", D_MODEL, IN_W // 2, tp, False)
    gw_in = gw_in.reshape(D_MODEL, N_SHARD, IN_W // N_SHARD).transpose(1, 0, 2)

    parts = [(dlg, 8), (dlb, 8), (dlb4, 8), (dng, 8), (dsk[:, 0], 8),
             (dg1, 8), (db1, 8), (dg2, 8), (db2, 8), (dmeta, BLOCK), (loss_part, 8)]
    small = jnp.concatenate([_rows128(a, r) for a, r in parts], axis=0)

    c_idx = jnp.reshape(ic, (1,)).astype(jnp.int32)
    jc_idx = jnp.stack([shard, ic]).astype(jnp.int32)
    other_in, gathered = pair_exchange_halves([gw_in], small)
    sum_in = add_pair(gw_in, other_in, c_idx, "add_pair_in")
    quad_in, = chip_exchange([sum_in])
    halves = [add_four(sum_in, quad_in, jc_idx, "add_four_in")]
    halves += [add_eight(g, r, jc_idx, "add_eight_" + nm) for nm, g, r in
               (("bh", gb_bh, r_bh), ("ba", gb_ba, r_ba), ("out", gb_out, r_out), ("fi", gb_fi, r_fi),
                ("fo", gb_fo, r_fo))]
    red = [r.reshape(-1, r.shape[-1]) for r in pair_exchange_results(halves)]
    small_sum = sum_devices(gathered)

    offs, acc = [], 0
    for _, r in parts:
        offs.append(acc)
        acc += r
    take = lambda n, size: small_sum[offs[n]:offs[n] + parts[n][1]].reshape(-1)[:size]
    g_meta_full = take(9, N_META * D_MODEL).reshape(N_META, D_MODEL)
    g_small = {
        "meta_tokens": lax.dynamic_slice_in_dim(g_meta_full, shard * (D_MODEL // N_SHARD), D_MODEL // N_SHARD, axis=1),
        "ln_emb_g": take(0, D_MODEL), "ln_emb_b": take(1, D_MODEL),
        "hg_lower_bounds": take(2, 2 * HG_HEADS * HG_K).reshape(2, HG_HEADS * HG_K),
        "hg_norm_g": take(3, HG_K).reshape(1, HG_K), "attn_sinks": take(4, ATT_HEADS).reshape(1, ATT_HEADS),
        "ln1_g": take(5, D_MODEL).reshape(1, D_MODEL), "ln1_b": take(6, D_MODEL).reshape(1, D_MODEL),
        "ln2_g": take(7, D_MODEL).reshape(1, D_MODEL), "ln2_b": take(8, D_MODEL).reshape(1, D_MODEL),
    }
    g_big = {"w_in": red[0], "w_branch_hg": red[1], "w_branch_attn": red[2], "w_out": red[3],
             "w_ffn_in": red[4], "w_ffn_out": red[5]}

    names = ["meta_tokens", "ln_emb_g", "ln_emb_b", "w_in", "hg_lower_bounds", "hg_norm_g", "attn_sinks",
             "w_branch_hg", "w_branch_attn", "w_out", "ln1_g", "ln1_b", "w_ffn_in", "w_ffn_out", "ln2_g", "ln2_b"]
    given = dict(
        meta_tokens=(meta_tokens, m_meta_tokens, v_meta_tokens), ln_emb_g=(ln_emb_g, m_ln_emb_g, v_ln_emb_g),
        ln_emb_b=(ln_emb_b, m_ln_emb_b, v_ln_emb_b), w_in=(w_in, m_w_in, v_w_in),
        hg_lower_bounds=(hg_lower_bounds, m_hg_lower_bounds, v_hg_lower_bounds),
        hg_norm_g=(hg_norm_g, m_hg_norm_g, v_hg_norm_g), attn_sinks=(attn_sinks, m_attn_sinks, v_attn_sinks),
        w_branch_hg=(w_branch_hg, m_w_branch_hg, v_w_branch_hg),
        w_branch_attn=(w_branch_attn, m_w_branch_attn, v_w_branch_attn), w_out=(w_out, m_w_out, v_w_out),
        ln1_g=(ln1_g, m_ln1_g, v_ln1_g), ln1_b=(ln1_b, m_ln1_b, v_ln1_b), w_ffn_in=(w_ffn_in, m_w_ffn_in, v_w_ffn_in),
        w_ffn_out=(w_ffn_out, m_w_ffn_out, v_w_ffn_out), ln2_g=(ln2_g, m_ln2_g, v_ln2_g), ln2_b=(ln2_b, m_ln2_b, v_ln2_b))
    out_g, out_d, out_m, out_v = [], [], [], []
    for nm in names:
        w, m, v = given[nm]
        shape = w.shape
        g = g_big[nm] if nm in g_big else g_small[nm]
        two_d = (lambda a: a.reshape(8, BLOCK)) if w.ndim == 1 else (lambda a: a.reshape(a.shape[-2], a.shape[-1]))
        d, mn, vn = adamw(two_d(w), two_d(g), two_d(m), two_d(v), "adamw_" + nm)
        out_g.append(g.reshape(shape))
        out_d.append(d.reshape(shape))
        out_m.append(mn.reshape(shape))
        out_v.append(vn.reshape(shape))

    loss = take(10, 1)[0]
    grad_x = dx.reshape(x.shape)
    return (loss, grad_x, *out_g, *out_d, *out_m, *out_v)
```

```python
import functools

import jax
import jax.numpy as jnp
from jax import lax
from jax.experimental import pallas as pl
from jax.experimental.pallas import tpu as pltpu

f32 = jnp.float32
bf16 = jnp.bfloat16

D_MODEL = 1024
BLOCK = 128
N_META = 16
PAD = BLOCK - N_META
HG_HEADS = 4
HG_K = 128
SUB = 16
ATT_HEADS = 8
HEAD_DIM = 64
ATT_QW = ATT_HEADS * HEAD_DIM
D_FF = 2816
EPS = 1e-5
ALPHA = 2.0 ** 0.25
ROPE_THETA = 10000.0
N_A = 2816
N_G = 2048
IN_W = N_A + N_G
N_SHARD = 4
N_DEV = 8

ADAM_LR = 0.001
ADAM_B1 = 0.9
ADAM_B2 = 0.999
ADAM_EPS = 1e-08
ADAM_WD = 0.01
ADAM_STEP = 10

TM = 256
LEAD = TM // BLOCK - 1

VMEM_LIMIT = 56 * 1024 * 1024
MESH = pl.DeviceIdType.MESH


def _cparams(sem, vmem=VMEM_LIMIT):
    return pltpu.CompilerParams(dimension_semantics=sem, vmem_limit_bytes=vmem)


def _const_spec(shape):
    zeros = (0,) * len(shape)
    return pl.BlockSpec(shape, lambda *_: zeros, pipeline_mode=pl.Buffered(1))


def _dot(a, b, ca, cb):
    return lax.dot_general(a.astype(bf16), b.astype(bf16), (((ca,), (cb,)), ((), ())),
                           preferred_element_type=f32)


@jax.custom_vjp
def mm(a, b):
    return _dot(a, b, 1, 0)


mm.defvjp(lambda a, b: (_dot(a, b, 1, 0), (a, b)),
          lambda r, g: (_dot(g, r[1], 1, 1), _dot(r[0], g, 0, 0)))


@jax.custom_vjp
def mm_nt(a, b):
    return _dot(a, b, 1, 1)


mm_nt.defvjp(lambda a, b: (_dot(a, b, 1, 1), (a, b)),
             lambda r, g: (_dot(g, r[1], 1, 0), _dot(g, r[0], 0, 0)))


@jax.custom_vjp
def mm_tn(a, b):
    return _dot(a, b, 0, 0)


mm_tn.defvjp(lambda a, b: (_dot(a, b, 0, 0), (a, b)),
             lambda r, g: (_dot(r[1], g, 1, 1), _dot(r[0], g, 1, 0)))


@functools.partial(jax.custom_vjp, nondiff_argnums=(1,))
def roll_lanes(x, shift):
    return pltpu.roll(x, shift, 1)


roll_lanes.defvjp(lambda x, shift: (pltpu.roll(x, shift, 1), None),
                  lambda shift, _, g: (pltpu.roll(g, (128 - shift) % 128, 1),))


@jax.custom_vjp
def _sigmoid(x):
    return 1.0 / (1.0 + jnp.exp(-x))


def _sigmoid_fwd(x):
    s = 1.0 / (1.0 + jnp.exp(-x))
    return s, s


_sigmoid.defvjp(_sigmoid_fwd, lambda s, g: (g * s * (1.0 - s),))


@jax.custom_vjp
def _recip(x):
    return 1.0 / x


def _recip_fwd(x):
    r = 1.0 / x
    return r, r


_recip.defvjp(_recip_fwd, lambda r, g: (-g * r * r,))


def _ln_stats(x):
    mu = jnp.mean(x, axis=-1, keepdims=True)
    xc = x - mu
    var = jnp.mean(xc * xc, axis=-1, keepdims=True)
    rs = lax.rsqrt(var + EPS)
    return xc * rs, rs


def _ln_bwd(dy, xh, rs, g):
    dxh = dy * g
    m1 = jnp.mean(dxh, axis=-1, keepdims=True)
    m2 = jnp.mean(dxh * xh, axis=-1, keepdims=True)
    return rs * (dxh - m1 - xh * m2)


def _row_ids(i):
    return i * BLOCK + lax.broadcasted_iota(jnp.int32, (BLOCK, 1), 0)


def _tm_rows(i):
    return i * TM + lax.broadcasted_iota(jnp.int32, (TM, 1), 0)


def _tm_row(n):
    return pl.BlockSpec((TM, n), lambda i: (i, 0))


def _tm_tokens():
    return pl.BlockSpec((TM, D_MODEL), lambda i: (jnp.maximum(i - 1, 0), 0))


def emb_inproj(x, metablk, g, b, w_in):
    nsteps = x.shape[0] // TM + 1

    def body(x_ref, mb_ref, g_ref, b_ref, w_ref, h0_ref, h0b_ref, pa_ref, pg_ref):
        i = pl.program_id(0)
        xb = jnp.where(i == 0, mb_ref[...], x_ref[...])
        xh, _ = _ln_stats(xb)
        y = xh * g_ref[...] + b_ref[...]
        y = jnp.where(_tm_rows(i) >= TM - N_META, y, 0.0)
        h0_ref[...] = y
        yb = y.astype(bf16)
        h0b_ref[...] = yb
        pa_ref[...] = jnp.dot(yb, w_ref[:, :N_A], preferred_element_type=f32)
        pg_ref[...] = jnp.dot(yb, w_ref[:, N_A:], preferred_element_type=f32)

    p = nsteps * TM
    row = _tm_row
    return pl.pallas_call(
        body, name="emb_inproj", grid=(nsteps,),
        in_specs=[_tm_tokens(),
                  _const_spec((TM, D_MODEL)), _const_spec((1, D_MODEL)), _const_spec((1, D_MODEL)),
                  _const_spec((D_MODEL, IN_W))],
        out_specs=[row(D_MODEL), row(D_MODEL), row(N_A), row(N_G)],
        out_shape=[jax.ShapeDtypeStruct((p, D_MODEL), f32), jax.ShapeDtypeStruct((p, D_MODEL), bf16),
                   jax.ShapeDtypeStruct((p, N_A), f32), jax.ShapeDtypeStruct((p, N_G), f32)],
        compiler_params=_cparams(("parallel",)),
    )(x, metablk, g, b, w_in)


def _hgrn_chunk(valid, st, hq, hf, hi, hg, lbraw, ng):
    lb = _sigmoid(lbraw[0:1] - lbraw[1:2])
    q = hq * _sigmoid(hq)
    fg = lb + (1.0 - lb) * _sigmoid(hf)
    logf = jnp.where(valid, jnp.log(fg), 0.0)
    k = jnp.where(valid, 1.0 - fg, 0.0)
    v = hi
    r = lax.broadcasted_iota(jnp.int32, (BLOCK, BLOCK), 0)
    c = lax.broadcasted_iota(jnp.int32, (BLOCK, BLOCK), 1)
    tril = (c <= r).astype(f32)
    bcum = jnp.dot(tril, logf, precision=lax.Precision.HIGHEST, preferred_element_type=f32)
    blast = bcum[BLOCK - 1:BLOCK]
    rows = lax.broadcasted_iota(jnp.int32, (BLOCK, 1), 0)
    sub8 = lax.broadcasted_iota(jnp.int32, (BLOCK // 8, 8, HG_K), 1)
    b8 = bcum.reshape(BLOCK // 8, 8, HG_K)
    row_of_8 = lambda j: jnp.broadcast_to(b8[:, j:j + 1, :], b8.shape)
    a = jnp.where(r == c, jnp.sum(q * k, axis=-1, keepdims=True), 0.0)
    seg = BLOCK
    while seg >= 2:
        half = seg // 2
        if seg >= 8:
            bs = bcum.reshape(BLOCK // seg, seg, HG_K)
            ref = jnp.broadcast_to(bs[:, half - 1:half, :], bs.shape)
        elif seg == 4:
            ref = jnp.where(sub8 < 4, row_of_8(1), row_of_8(5))
        else:
            ref = jnp.where(sub8 < 2, row_of_8(0), jnp.where(sub8 < 4, row_of_8(2),
                                                             jnp.where(sub8 < 6, row_of_8(4), row_of_8(6))))
        ref = ref.reshape(BLOCK, HG_K)
        upper = (rows % seg) >= half
        q_up = q * jnp.exp(jnp.where(upper, bcum - ref, -jnp.inf))
        k_lo = k * jnp.exp(jnp.where(upper, -jnp.inf, ref - bcum))
        a = a + jnp.where((r // seg) == (c // seg), mm_nt(q_up, k_lo), 0.0)
        seg = half
    o = mm_nt(q * jnp.exp(bcum), st) + mm(a, v)
    st_new = st * jnp.exp(blast) + mm_tn(v, k * jnp.exp(blast - bcum))
    on = o * lax.rsqrt(jnp.mean(o * o, axis=-1, keepdims=True) + EPS) * ng
    return st_new, on * (hg * _sigmoid(hg))


def _hgrn_in_specs(rowmap):
    wide = lambda col: pl.BlockSpec((BLOCK, HG_HEADS * HG_K), lambda i: (rowmap(i) + LEAD, col))
    return [wide(0), wide(1), wide(2), wide(3), _const_spec((2, HG_HEADS * HG_K)), _const_spec((1, HG_K))]


def _head(ref, h):
    return ref[:, h * HG_K:(h + 1) * HG_K]


def hgrn_fwd(pa, lbraw, ng, nb, shards):
    n = len(shards)

    def body(hq_ref, hf_ref, hi_ref, hg_ref, lb_ref, ng_ref, *rest):
        srcs, (og_ref, sp_ref), dsts = rest[:n], rest[n:n + 2], rest[n + 2:2 * n + 2]
        st_ref = rest[2 * n + 2]
        start, wait = _shard_push(srcs, dsts, *rest[2 * n + 3:])
        i = pl.program_id(0)

        @pl.when(i == 0)
        def _():
            st_ref[...] = jnp.zeros_like(st_ref)
            start()

        @pl.when(i == nb - 1)
        def _():
            wait()

        valid = _row_ids(i) >= PAD
        for h in range(HG_HEADS):
            st = st_ref[h]
            sp_ref[0, h] = st
            st_new, out = _hgrn_chunk(valid, st, _head(hq_ref, h), _head(hf_ref, h), _head(hi_ref, h),
                                      _head(hg_ref, h), _head(lb_ref, h), ng_ref[...])
            st_ref[h] = st_new
            og_ref[:, h * HG_K:(h + 1) * HG_K] = out.astype(bf16)

    p = (nb + LEAD) * BLOCK
    push_in, push_out, push_shape, push_scratch = _push_specs(shards)
    return pl.pallas_call(
        body, name="hgrn_fwd", grid=(nb,),
        in_specs=_hgrn_in_specs(lambda i: i) + push_in,
        out_specs=[pl.BlockSpec((BLOCK, HG_HEADS * HG_K), lambda i: (i + LEAD, 0)),
                   pl.BlockSpec((1, HG_HEADS, HG_K, HG_K), lambda i: (i, 0, 0, 0))] + push_out,
        out_shape=[jax.ShapeDtypeStruct((p, HG_HEADS * HG_K), bf16),
                   jax.ShapeDtypeStruct((nb, HG_HEADS, HG_K, HG_K), f32)] + push_shape,
        scratch_shapes=[pltpu.VMEM((HG_HEADS, HG_K, HG_K), f32)] + push_scratch,
        compiler_params=_cparams(("arbitrary",)),
    )(pa, pa, pa, pa, lbraw, ng, *shards)


def hgrn_bwd(pa, lbraw, ng, sprev, dog, nb, grads):
    n = len(grads)

    def body(hq_ref, hf_ref, hi_ref, hg_ref, lb_ref, ng_ref, sp_ref, do_ref, *rest):
        srcs, rest = rest[:n], rest[n:]
        dq_ref, df_ref, di_ref, dg_ref, dlb_ref, dng_ref = rest[:6]
        dsts, dst_ref = rest[6:6 + n], rest[6 + n]
        start, wait = _grad_push(srcs, dsts, *rest[7 + n:])
        i = pl.program_id(0)

        @pl.when(i == 0)
        def _():
            dst_ref[...] = jnp.zeros_like(dst_ref)
            dlb_ref[...] = jnp.zeros_like(dlb_ref)
            dng_ref[...] = jnp.zeros_like(dng_ref)
            start()

        valid = _row_ids(nb - 1 - i) >= PAD
        dng_sum = jnp.zeros((1, HG_K), f32)
        for h in range(HG_HEADS):
            cols = slice(h * HG_K, (h + 1) * HG_K)
            _, vjp = jax.vjp(functools.partial(_hgrn_chunk, valid), sp_ref[0, h], _head(hq_ref, h), _head(hf_ref, h),
                             _head(hi_ref, h), _head(hg_ref, h), _head(lb_ref, h), ng_ref[...])
            dst, dq, df, di, dg, dlb, dng = vjp((dst_ref[h], _head(do_ref, h)))
            dst_ref[h] = dst
            dq_ref[:, cols] = dq.astype(bf16)
            df_ref[:, cols] = df.astype(bf16)
            di_ref[:, cols] = di.astype(bf16)
            dg_ref[:, cols] = dg.astype(bf16)
            dlb_ref[:, cols] += dlb
            dng_sum = dng_sum + dng
        dng_ref[...] += dng_sum
        pl.when(i == nb - 1)(wait)

    p = (nb + LEAD) * BLOCK
    rev = lambda i: nb - 1 - i
    hw = HG_HEADS * HG_K
    blk = pl.BlockSpec((BLOCK, hw), lambda i: (rev(i) + LEAD, 0))
    wide = jax.ShapeDtypeStruct((p, hw), bf16)
    push_in, push_out, push_shape, push_scratch = _grad_push_specs(grads)
    return pl.pallas_call(
        body, name="hgrn_bwd", grid=(nb,),
        in_specs=_hgrn_in_specs(rev) + [pl.BlockSpec((1, HG_HEADS, HG_K, HG_K), lambda i: (rev(i), 0, 0, 0)), blk]
        + push_in,
        out_specs=[blk, blk, blk, blk, pl.BlockSpec((2, hw), lambda i: (0, 0)), pl.BlockSpec((1, HG_K), lambda i: (0, 0))]
        + push_out,
        out_shape=[wide, wide, wide, wide, jax.ShapeDtypeStruct((2, hw), f32), jax.ShapeDtypeStruct((1, HG_K), f32)]
        + push_shape,
        scratch_shapes=[pltpu.VMEM((HG_HEADS, HG_K, HG_K), f32)] + push_scratch,
        compiler_params=_cparams(("arbitrary",)),
    )(pa, pa, pa, pa, lbraw, ng, sprev, dog, *grads)


def _rope(x, cos, sin):
    lane = lax.broadcasted_iota(jnp.int32, x.shape, 1)
    rot = jnp.where(lane % HEAD_DIM < HEAD_DIM // 2, -roll_lanes(x, BLOCK - HEAD_DIM // 2),
                    roll_lanes(x, HEAD_DIM // 2))
    return x * cos + rot * sin


def _both_halves(x, g):
    lo = lax.broadcasted_iota(jnp.int32, x.shape, 1) < HEAD_DIM
    sw = roll_lanes(x, HEAD_DIM)
    return jnp.where(lo, x, sw) if g == 0 else jnp.where(lo, sw, x)


def _attn_block(band_ok, meta_ok, tabs, q, kp, kc, vp, vc, km, vm, *sinks):
    cq, sq, cp, sp, cm, sm = tabs
    neg = jnp.finfo(f32).min
    scale = HEAD_DIM ** -0.5
    group = ATT_HEADS // 2
    kpr, kcr, kmr = _rope(kp, cp, sp), _rope(kc, cq, sq), _rope(km, cm, sm)
    lo = lax.broadcasted_iota(jnp.int32, (BLOCK, BLOCK), 1) < HEAD_DIM
    t = lax.broadcasted_iota(jnp.int32, (group * BLOCK, BLOCK), 0) % BLOCK
    col = lax.broadcasted_iota(jnp.int32, (group * BLOCK, BLOCK), 1)
    own = col <= t
    is_sink = col == N_META
    qr =[_rope(q[:, m * BLOCK:(m + 1) * BLOCK], cq, sq) for m in range(ATT_HEADS // 2)]
    slabs = []
    for g in range(2):
        kp_g, kc_g, vp_g, vc_g, km_g, vm_g = [_both_halves(a, g) for a in (kpr, kcr, vp, vc, kmr, vm)]
        qs = jnp.concatenate([jnp.where(lo if h % 2 == 0 else ~lo, qr[2 * g + h // 2], 0.0) for h in range(group)],
                             axis=0)
        sink = jnp.concatenate([jnp.broadcast_to(sinks[group * g + h], (BLOCK, 1)) for h in range(group)], axis=0)
        sb = jnp.where(band_ok, jnp.where(own, mm_nt(qs, kc_g), mm_nt(qs, kp_g)) * scale, neg)
        no_keys = jnp.zeros((BLOCK - N_META, BLOCK), f32)
        sme = jnp.where(meta_ok, mm_nt(qs, jnp.concatenate([km_g, no_keys], axis=0)) * scale,
                        jnp.where(is_sink, sink, neg))
        mx = lax.stop_gradient(jnp.max(jnp.maximum(sb, sme), axis=-1, keepdims=True))
        eb, em = jnp.exp(sb - mx), jnp.exp(sme - mx)
        inv = _recip(jnp.sum(eb + em, axis=-1, keepdims=True))
        pb = eb * inv
        o = (mm(jnp.where(own, pb, 0.0), vc_g) + mm(jnp.where(own, 0.0, pb), vp_g)
             + mm(em * inv, jnp.concatenate([vm_g, no_keys], axis=0)))
        for m in range(2):
            slabs.append(jnp.where(lo, o[2 * m * BLOCK:(2 * m + 1) * BLOCK], o[(2 * m + 1) * BLOCK:(2 * m + 2) * BLOCK]))
    return jnp.concatenate(slabs, axis=1)


def _attn_masks(i):
    group = ATT_HEADS // 2
    t = lax.broadcasted_iota(jnp.int32, (group * BLOCK, BLOCK), 0) % BLOCK
    s = lax.broadcasted_iota(jnp.int32, (group * BLOCK, BLOCK), 1)
    kpos = jnp.where(s <= t, i * BLOCK - PAD + s, jnp.where(i > 0, (i - 1) * BLOCK - PAD + s, -1))
    band_ok = kpos >= N_META
    qpos = i * BLOCK - PAD + lax.broadcasted_iota(jnp.int32, (group * BLOCK, 1), 0) % BLOCK
    meta_ok = (s < N_META) & (s <= qpos)
    return band_ok, meta_ok


def _attn_in_specs(cur=lambda i: i):
    prev = lambda i: jnp.maximum(cur(i) - 1, 0)
    kcol, vcol = N_A // BLOCK - 2, N_A // BLOCK - 1
    blk = lambda rowmap, col: pl.BlockSpec((BLOCK, BLOCK), lambda i: (rowmap(i) + LEAD, col))
    tab = lambda rowmap: pl.BlockSpec((BLOCK, BLOCK), lambda i: (rowmap(i), 0))
    first = lambda i: 0
    return [pl.BlockSpec((BLOCK, ATT_QW), lambda i: (cur(i) + LEAD, 4)),
            blk(prev, kcol), blk(cur, kcol), blk(prev, vcol), blk(cur, vcol), blk(first, kcol), blk(first, vcol),
            tab(cur), tab(cur), tab(prev), tab(prev), tab(first), tab(first),
            _const_spec((ATT_HEADS, BLOCK))]


def _attn_row(n, cur=lambda i: i):
    return pl.BlockSpec((BLOCK, n), lambda i: (cur(i) + LEAD, 0))


def _attn_operands(q_ref, kp_ref, kc_ref, vp_ref, vc_ref, km_ref, vm_ref, cq, sq, cp, sp, cm, sm, sk_ref):
    tabs = (cq[...], sq[...], cp[...], sp[...], cm[PAD:, :], sm[PAD:, :])
    args = (q_ref[...], kp_ref[...], kc_ref[...], vp_ref[...], vc_ref[...], km_ref[PAD:, :], vm_ref[PAD:, :])
    sinks = tuple(sk_ref[j:j + 1, 0:1] for j in range(ATT_HEADS))
    return tabs, args + sinks


def attn_fwd(pa, cos, sin, sinks8, nb, shards):
    n = len(shards)
    n_in = 14

    def body(*refs):
        srcs, o_ref, dsts = refs[n_in:n_in + n], refs[n_in + n], refs[n_in + n + 1:n_in + 2 * n + 1]
        start, wait = _shard_push(srcs, dsts, *refs[n_in + 2 * n + 1:])
        i = pl.program_id(0)
        pl.when(i == 0)(start)
        band_ok, meta_ok = _attn_masks(i)
        tabs, args = _attn_operands(*refs[:n_in])
        o_ref[...] = _attn_block(band_ok, meta_ok, tabs, *args).astype(bf16)
        pl.when(i == nb - 1)(wait)

    push_in, push_out, push_shape, push_scratch = _push_specs(shards)
    return pl.pallas_call(
        body, name="attn_fwd", grid=(nb,), in_specs=_attn_in_specs() + push_in,
        out_specs=[_attn_row(ATT_QW)] + push_out,
        out_shape=[jax.ShapeDtypeStruct(((nb + LEAD) * BLOCK, ATT_QW), bf16)] + push_shape,
        scratch_shapes=push_scratch,
        compiler_params=_cparams(("arbitrary",)),
    )(pa, pa, pa, pa, pa, pa, pa, cos, sin, cos, sin, cos, sin, sinks8, *shards)


def attn_bwd(pa, cos, sin, sinks8, do, nb, grads):
    n = len(grads)

    def body(*refs):
        do_ref, srcs = refs[14], refs[15:15 + n]
        dq_ref, dkc_ref, dkp_ref, dvc_ref, dvp_ref, dkm_ref, dvm_ref, dsk_ref = refs[15 + n:23 + n]
        start, wait = _grad_push(srcs, refs[23 + n:23 + 2 * n], *refs[23 + 2 * n:])
        i = pl.program_id(0)

        @pl.when(i == 0)
        def _():
            dkm_ref[...] = jnp.zeros((N_META, BLOCK), f32)
            dvm_ref[...] = jnp.zeros((N_META, BLOCK), f32)
            dsk_ref[...] = jnp.zeros((ATT_HEADS, BLOCK), f32)
            start()

        band_ok, meta_ok = _attn_masks(i)
        tabs, args = _attn_operands(*refs[:14])
        _, vjp = jax.vjp(functools.partial(_attn_block, band_ok, meta_ok, tabs), *args)
        grads = vjp(do_ref[...])
        dq_ref[...] = grads[0].astype(bf16)
        dkp_ref[...] = grads[1]
        dkc_ref[...] = grads[2]
        dvp_ref[...] = grads[3]
        dvc_ref[...] = grads[4]
        dkm_ref[...] += grads[5]
        dvm_ref[...] += grads[6]
        for j in range(ATT_HEADS):
            dsk_ref[j:j + 1, :] += jnp.broadcast_to(grads[7 + j], (1, BLOCK))
        pl.when(i == nb - 1)(wait)

    p = (nb + LEAD) * BLOCK
    row = _attn_row(BLOCK)
    const = lambda r: pl.BlockSpec((r, BLOCK), lambda i: (0, 0))
    part = jax.ShapeDtypeStruct((p, BLOCK), f32)
    push_in, push_out, push_shape, push_scratch = _grad_push_specs(grads)
    return pl.pallas_call(
        body, name="attn_bwd", grid=(nb,),
        in_specs=_attn_in_specs() + [_attn_row(ATT_QW)] + push_in,
        out_specs=[_attn_row(ATT_QW), row, row, row, row,
                   const(N_META), const(N_META), const(ATT_HEADS)] + push_out,
        out_shape=[jax.ShapeDtypeStruct((p, ATT_QW), bf16), part, part, part, part,
                   jax.ShapeDtypeStruct((N_META, BLOCK), f32), jax.ShapeDtypeStruct((N_META, BLOCK), f32),
                   jax.ShapeDtypeStruct((ATT_HEADS, BLOCK), f32)] + push_shape,
        scratch_shapes=push_scratch,
        compiler_params=_cparams(("arbitrary",)),
    )(pa, pa, pa, pa, pa, pa, pa, cos, sin, cos, sin, cos, sin, sinks8, do, *grads)


def _mid_forward(h0_ref, pg_ref, og, oa, wbh_ref, wba_ref, wo_ref, g1, b1):
    yh = jnp.dot(og, wbh_ref[...], preferred_element_type=f32)
    ya = jnp.dot(oa, wba_ref[...], preferred_element_type=f32)
    gh = _sigmoid(pg_ref[:, :D_MODEL])
    ga = _sigmoid(pg_ref[:, D_MODEL:])
    mixin = (gh * yh + ga * ya).astype(bf16)
    r1 = ALPHA * h0_ref[...] + jnp.dot(mixin, wo_ref[...], preferred_element_type=f32)
    xh1, rs1 = _ln_stats(r1)
    return yh, ya, gh, ga, mixin, xh1, rs1, xh1 * g1 + b1


def _mid_weight_specs():
    hw = HG_HEADS * HG_K
    return [_const_spec((hw, D_MODEL)), _const_spec((ATT_QW, D_MODEL)), _const_spec((D_MODEL, D_MODEL)),
            _const_spec((1, D_MODEL)), _const_spec((1, D_MODEL))]


def mid_front(h0, pg, og, oatt, wbh, wba, wout, ln1g, ln1b):
    def body(h0_ref, pg_ref, og_ref, oa_ref, wbh_ref, wba_ref, wo_ref, g1_ref, b1_ref,
             h1_ref, h1b_ref, mix_ref, ogc_ref, oac_ref):
        used = _tm_rows(pl.program_id(0)) >= LEAD * BLOCK
        og = jnp.where(used, og_ref[...], jnp.zeros_like(og_ref))
        oa = jnp.where(used, oa_ref[...], jnp.zeros_like(oa_ref))
        ogc_ref[...] = og
        oac_ref[...] = oa
        *_, mixin, _, _, h1 = _mid_forward(h0_ref, pg_ref, og, oa, wbh_ref, wba_ref, wo_ref, g1_ref[...], b1_ref[...])
        mix_ref[...] = mixin
        h1_ref[...] = h1
        h1b_ref[...] = h1.astype(bf16)

    p = h0.shape[0]
    hw = HG_HEADS * HG_K
    sds = lambda n, dt: jax.ShapeDtypeStruct((p, n), dt)
    return pl.pallas_call(
        body, name="mid_front", grid=(p // TM,),
        in_specs=[_tm_row(D_MODEL), _tm_row(N_G), _tm_row(hw), _tm_row(ATT_QW)] + _mid_weight_specs(),
        out_specs=[_tm_row(D_MODEL), _tm_row(D_MODEL), _tm_row(D_MODEL), _tm_row(hw), _tm_row(ATT_QW)],
        out_shape=[sds(D_MODEL, f32), sds(D_MODEL, bf16), sds(D_MODEL, bf16), sds(hw, bf16), sds(ATT_QW, bf16)],
        compiler_params=_cparams(("parallel",)),
    )(h0, pg, og, oatt, wbh, wba, wout, ln1g, ln1b)


def mid_ffn(h1, target, wfi, wfo, ln2g, ln2b):
    def body(h1_ref, t_ref, wfi_ref, wfo_ref, g2_ref, b2_ref,
             dh1_ref, dau_ref, s_ref, dr2_ref, loss_ref, dg2_ref, db2_ref):
        i = pl.program_id(0)

        @pl.when(i == 0)
        def _():
            for r in (loss_ref, dg2_ref, db2_ref):
                r[...] = jnp.zeros_like(r)

        g2, b2 = g2_ref[...], b2_ref[...]
        h1 = h1_ref[...]
        au = jnp.dot(h1.astype(bf16), wfi_ref[...], preferred_element_type=f32)
        a, u = au[:, :D_FF], au[:, D_FF:]
        sg = _sigmoid(a)
        sa = a * sg
        s = (sa * u).astype(bf16)
        s_ref[...] = s
        r2 = ALPHA * h1 + jnp.dot(s, wfo_ref[...], preferred_element_type=f32)
        xh2, rs2 = _ln_stats(r2)
        diff = jnp.where(i > 0, xh2 * g2 + b2 - t_ref[...], 0.0)
        loss_ref[...] += jnp.sum(diff * diff) * (0.5 / D_MODEL)
        dy = diff * (1.0 / D_MODEL)
        dg2_ref[...] += jnp.sum(dy * xh2, axis=0, keepdims=True)
        db2_ref[...] += jnp.sum(dy, axis=0, keepdims=True)
        dr2 = _ln_bwd(dy, xh2, rs2, g2)
        dr2b = dr2.astype(bf16)
        dr2_ref[...] = dr2b
        ds = _dot(dr2b, wfo_ref[...], 1, 1)
        da = (ds * u) * (sg * (1.0 + a * (1.0 - sg)))
        du = ds * sa
        dau = jnp.concatenate([da, du], axis=1).astype(bf16)
        dau_ref[...] = dau
        dh1_ref[...] = ALPHA * dr2 + _dot(dau, wfi_ref[...], 1, 1)

    p = h1.shape[0]
    vec = lambda: pl.BlockSpec((1, D_MODEL), lambda i: (0, 0))
    sds = lambda n, dt: jax.ShapeDtypeStruct((p, n), dt)
    return pl.pallas_call(
        body, name="mid_ffn", grid=(p // TM,),
        in_specs=[_tm_row(D_MODEL), _tm_tokens(), _const_spec((D_MODEL, 2 * D_FF)), _const_spec((D_FF, D_MODEL)),
                  _const_spec((1, D_MODEL)), _const_spec((1, D_MODEL))],
        out_specs=[_tm_row(D_MODEL), _tm_row(2 * D_FF), _tm_row(D_FF), _tm_row(D_MODEL),
                   pl.BlockSpec((1, 1), lambda i: (0, 0)), vec(), vec()],
        out_shape=[sds(D_MODEL, f32), sds(2 * D_FF, bf16), sds(D_FF, bf16), sds(D_MODEL, bf16),
                   jax.ShapeDtypeStruct((1, 1), f32)] + [jax.ShapeDtypeStruct((1, D_MODEL), f32)] * 2,
        compiler_params=_cparams(("arbitrary",)),
    )(h1, target, wfi, wfo, ln2g, ln2b)


def mid_back(dh1, h0, pg, ogc, oac, wbh, wba, wout, ln1g, ln1b):
    def body(dh1_ref, h0_ref, pg_ref, og_ref, oa_ref, wbh_ref, wba_ref, wo_ref, g1_ref, b1_ref,
             dh0_ref, dpg_ref, dog_ref, doa_ref, dyh_ref, dya_ref, dr1_ref, dg1_ref, db1_ref):
        @pl.when(pl.program_id(0) == 0)
        def _():
            dg1_ref[...] = jnp.zeros_like(dg1_ref)
            db1_ref[...] = jnp.zeros_like(db1_ref)

        g1 = g1_ref[...]
        yh, ya, gh, ga, _, xh1, rs1, _ = _mid_forward(h0_ref, pg_ref, og_ref[...], oa_ref[...], wbh_ref, wba_ref,
                                                      wo_ref, g1, b1_ref[...])
        dh1 = dh1_ref[...]
        dg1_ref[...] += jnp.sum(dh1 * xh1, axis=0, keepdims=True)
        db1_ref[...] += jnp.sum(dh1, axis=0, keepdims=True)
        dr1 = _ln_bwd(dh1, xh1, rs1, g1)
        dr1b = dr1.astype(bf16)
        dr1_ref[...] = dr1b
        dh0_ref[...] = ALPHA * dr1
        dmix = _dot(dr1b, wo_ref[...], 1, 1)
        dyh = (dmix * gh).astype(bf16)
        dya = (dmix * ga).astype(bf16)
        dyh_ref[...] = dyh
        dya_ref[...] = dya
        dpg_ref[:, :D_MODEL] = (dmix * yh * gh * (1.0 - gh)).astype(bf16)
        dpg_ref[:, D_MODEL:] = (dmix * ya * ga * (1.0 - ga)).astype(bf16)
        dog_ref[...] = _dot(dyh, wbh_ref[...], 1, 1)
        doa_ref[...] = _dot(dya, wba_ref[...], 1, 1)

    p = h0.shape[0]
    hw = HG_HEADS * HG_K
    vec = lambda: pl.BlockSpec((1, D_MODEL), lambda i: (0, 0))
    sds = lambda n, dt: jax.ShapeDtypeStruct((p, n), dt)
    return pl.pallas_call(
        body, name="mid_back", grid=(p // TM,),
        in_specs=[_tm_row(D_MODEL), _tm_row(D_MODEL), _tm_row(N_G), _tm_row(hw), _tm_row(ATT_QW)] + _mid_weight_specs(),
        out_specs=[_tm_row(D_MODEL), _tm_row(N_G), _tm_row(hw), _tm_row(ATT_QW), _tm_row(D_MODEL), _tm_row(D_MODEL),
                   _tm_row(D_MODEL), vec(), vec()],
        out_shape=[sds(D_MODEL, f32), sds(N_G, bf16), sds(hw, f32), sds(ATT_QW, f32), sds(D_MODEL, bf16),
                   sds(D_MODEL, bf16), sds(D_MODEL, bf16)] + [jax.ShapeDtypeStruct((1, D_MODEL), f32)] * 2,
        compiler_params=_cparams(("arbitrary",)),
    )(dh1, h0, pg, ogc, oac, wbh, wba, wout, ln1g, ln1b)


def inproj_bwd(dh0p, dhq, dhf, dhi, dhg, daq, dkc, dkp, dvc, dvp, dkm, dvm, dpg, w_in, x, metablk, g, b):
    p = dh0p.shape[0]
    nbk = p // BLOCK
    per = TM // BLOCK

    def body(dh0_ref, dq_ref, df_ref, di_ref, dg_ref, daq_ref, dkc_ref, *rest):
        dkp_refs, dvc_ref, dvp_refs = rest[:per], rest[per], rest[per + 1:2 * per + 1]
        (dkm_ref, dvm_ref, dpg_ref, w_ref, x_ref, mb_ref, g_ref, b_ref,
         dproj_ref, dx_ref, dmeta_ref, dlg_ref, dlb_ref) = rest[2 * per + 1:]
        i = pl.program_id(0)

        @pl.when(i == 0)
        def _():
            dlg_ref[...] = jnp.zeros_like(dlg_ref)
            dlb_ref[...] = jnp.zeros_like(dlb_ref)

        zero_pad = jnp.zeros((TM - N_META, BLOCK), f32)
        first = i == 0
        rows = _tm_rows(i)

        def keys(cur_ref, next_refs, meta_ref):
            nxt = jnp.concatenate([jnp.where(per * i + 1 + m < nbk, next_refs[m][...], 0.0) for m in range(per)], axis=0)
            t = cur_ref[...] + nxt
            return t + jnp.where(first, jnp.concatenate([zero_pad, meta_ref[...]], axis=0), 0.0)

        dproj = jnp.concatenate(
            [dq_ref[...], df_ref[...], di_ref[...], dg_ref[...], daq_ref[...],
             keys(dkc_ref, dkp_refs, dkm_ref).astype(bf16), keys(dvc_ref, dvp_refs, dvm_ref).astype(bf16),
             dpg_ref[...]], axis=1)
        dproj = jnp.where(rows >= LEAD * BLOCK, dproj, jnp.zeros_like(dproj))
        dproj_ref[...] = dproj
        valid = rows >= TM - N_META
        dh0 = jnp.where(valid, dh0_ref[...] + _dot(dproj, w_ref[...], 1, 1), 0.0)
        xb = jnp.where(first, mb_ref[...], x_ref[...])
        xh, rs = _ln_stats(xb)
        dlg_ref[...] += jnp.sum(dh0 * xh, axis=0, keepdims=True)
        dlb_ref[...] += jnp.sum(dh0, axis=0, keepdims=True)
        dx = jnp.where(valid, _ln_bwd(dh0, xh, rs, g_ref[...]), 0.0)
        dx_ref[...] = dx

        @pl.when(first)
        def _():
            dmeta_ref[...] = dx[TM - N_META:, :]

    row = _tm_row
    nxt = [pl.BlockSpec((BLOCK, BLOCK), functools.partial(lambda i, m: (jnp.minimum(per * i + 1 + m, nbk - 1), 0), m=m))
           for m in range(per)]
    hw = HG_HEADS * HG_K
    vec = lambda: pl.BlockSpec((1, D_MODEL), lambda i: (0, 0))
    return pl.pallas_call(
        body, name="inproj_bwd", grid=(p // TM,),
        in_specs=[row(D_MODEL), row(hw), row(hw), row(hw), row(hw), row(ATT_QW),
                  row(BLOCK)] + nxt + [row(BLOCK)] + nxt + [_const_spec((N_META, BLOCK)), _const_spec((N_META, BLOCK)),
                  row(N_G), _const_spec((D_MODEL, IN_W)), _tm_tokens(),
                  _const_spec((TM, D_MODEL)), _const_spec((1, D_MODEL)), _const_spec((1, D_MODEL))],
        out_specs=[row(IN_W), _tm_tokens(), pl.BlockSpec((N_META, D_MODEL), lambda i: (0, 0)), vec(), vec()],
        out_shape=[jax.ShapeDtypeStruct((p, IN_W), bf16), jax.ShapeDtypeStruct((p - TM, D_MODEL), f32),
                   jax.ShapeDtypeStruct((N_META, D_MODEL), f32),
                   jax.ShapeDtypeStruct((1, D_MODEL), f32), jax.ShapeDtypeStruct((1, D_MODEL), f32)],
        compiler_params=_cparams(("arbitrary",)),
    )(dh0p, dhq, dhf, dhi, dhg, daq, dkc, *([dkp] * per), dvc, *([dvp] * per), dkm, dvm, dpg, w_in, x, metablk, g, b)


def wgrad(a, b, name, tk, tn, tp, by_cols, out_dtype=f32):
    p, k = a.shape
    n = b.shape[1]
    nsteps = p // tp

    def body(a_ref, b_ref, o_ref, acc_ref):
        ip = pl.program_id(2)

        @pl.when(ip == 0)
        def _():
            acc_ref[...] = jnp.zeros_like(acc_ref)

        acc_ref[...] += _dot(a_ref[...], b_ref[...], 0, 0)

        @pl.when(ip == nsteps - 1)
        def _():
            for j in range(span):
                o_ref[j] = acc_ref[:, j * width:(j + 1) * width].astype(out_dtype)

    span, width = 1, tn
    if by_cols:
        shard_n = n // N_SHARD
        out_shape = (N_SHARD, k, shard_n)
        if tn >= shard_n:
            span, width = tn // shard_n, shard_n
            omap = lambda ik, jn, ip: (jn, ik, 0)
        else:
            per = shard_n // tn
            omap = lambda ik, jn, ip: (jn // per, ik, jn % per)
    else:
        out_shape = (1, k, n)
        omap = lambda ik, jn, ip: (0, ik, jn)
    return pl.pallas_call(
        body, name=name, grid=(k // tk, n // tn, nsteps),
        in_specs=[pl.BlockSpec((tp, tk), lambda ik, jn, ip: (ip, ik)),
                  pl.BlockSpec((tp, tn), lambda ik, jn, ip: (ip, jn))],
        out_specs=pl.BlockSpec((span, tk, width), omap),
        out_shape=jax.ShapeDtypeStruct(out_shape, out_dtype),
        scratch_shapes=[pltpu.VMEM((tk, tn), f32)],
        compiler_params=_cparams(("parallel", "parallel", "arbitrary")),
    )(a, b)


def adamw(w, g, m, v, name):
    r, c = w.shape
    tr = r
    for cand in (256, 176, 128):
        if r > cand and r % cand == 0:
            tr = cand
            break

    def body(w_ref, g_ref, m_ref, v_ref, d_ref, mo_ref, vo_ref):
        gg = g_ref[...]
        mn = ADAM_B1 * m_ref[...] + (1.0 - ADAM_B1) * gg
        vn = ADAM_B2 * v_ref[...] + (1.0 - ADAM_B2) * (gg * gg)
        m_hat = mn / (1.0 - ADAM_B1 ** ADAM_STEP)
        v_hat = vn / (1.0 - ADAM_B2 ** ADAM_STEP)
        d_ref[...] = -ADAM_LR * (m_hat / (jnp.sqrt(v_hat) + ADAM_EPS) + ADAM_WD * w_ref[...])
        mo_ref[...] = mn
        vo_ref[...] = vn

    spec = pl.BlockSpec((tr, c), lambda i: (i, 0))
    sds = jax.ShapeDtypeStruct((r, c), f32)
    return pl.pallas_call(
        body, name=name, grid=(r // tr,), in_specs=[spec] * 4, out_specs=[spec] * 3, out_shape=[sds] * 3,
        compiler_params=_cparams(("parallel",)),
    )(w, g, m, v)


def _me():
    return lax.axis_index("x"), lax.axis_index("y"), lax.axis_index("c")


def _chip_peer(x, y, c, k):
    return (x ^ (k >> 1), y ^ (k & 1), c)


ANY = pl.BlockSpec(memory_space=pl.ANY)


def gather_weights(now, later):
    n, n_later = len(now), len(later)
    out_dtypes = [bf16 if s.size > 16 * 256 else f32 for s in now]
    halves = [(2, s.shape[0] // 2, s.shape[1]) for s in now]

    def body(*refs):
        ins, later_ins = refs[:n], refs[n:n + n_later]
        outs, later_outs = refs[n + n_later:2 * n + n_later], refs[2 * n + n_later:2 * (n + n_later)]
        stage = refs[2 * (n + n_later):3 * n + 2 * n_later]
        send_sems, recv_sems, pass_send_sems, pass_recv_sems, local_sems = refs[3 * n + 2 * n_later:]
        x, y, c = _me()
        j = 2 * x + y
        sibling = (x, y, 1 - c)

        def over_ici(w, k, slot):
            return pltpu.make_async_remote_copy(
                src_ref=stage[w].at[c], dst_ref=outs[w].at[slot, c], send_sem=send_sems.at[w, k - 1],
                recv_sem=recv_sems.at[w, k - 1], device_id=_chip_peer(x, y, c, k), device_id_type=MESH)

        def passed_on(w, k, half):
            return pltpu.make_async_remote_copy(
                src_ref=outs[w].at[j ^ k, half], dst_ref=outs[w].at[j ^ k, half], send_sem=pass_send_sems.at[w, k - 1],
                recv_sem=pass_recv_sems.at[w, k - 1], device_id=sibling, device_id_type=MESH)

        for w in range(n):
            stage[w][...] = ins[w][...].astype(out_dtypes[w]).reshape(halves[w])
        locs = []
        for w in range(n):
            loc = pltpu.make_async_copy(stage[w], outs[w].at[j], local_sems.at[w])
            loc.start()
            locs.append(loc)
            for k in (1, 2, 3):
                over_ici(w, k, j).start()
        for w in range(n_later):
            later_outs[w][...] = later_ins[w][...].astype(bf16)
        for w in range(n):
            for k in (1, 2, 3):
                over_ici(w, k, j ^ k).wait_recv()
                passed_on(w, k, c).start()
        for w in range(n):
            for k in (1, 2, 3):
                passed_on(w, k, 1 - c).wait_recv()
        for w in range(n):
            for k in (1, 2, 3):
                over_ici(w, k, j).wait_send()
                passed_on(w, k, c).wait_send()
        for loc in locs:
            loc.wait()

    vmem = pl.BlockSpec(memory_space=pltpu.VMEM)
    sem3 = pltpu.SemaphoreType.DMA((n, 3))
    return pl.pallas_call(
        body, name="gather_weights",
        in_specs=[vmem] * (n + n_later), out_specs=[ANY] * n + [vmem] * n_later,
        out_shape=[jax.ShapeDtypeStruct((N_SHARD,) + h, dt) for h, dt in zip(halves, out_dtypes)]
        + [jax.ShapeDtypeStruct(s.shape, bf16) for s in later],
        scratch_shapes=[pltpu.VMEM(h, dt) for h, dt in zip(halves, out_dtypes)]
        + [sem3, sem3, sem3, sem3, pltpu.SemaphoreType.DMA((n,))],
        compiler_params=pltpu.CompilerParams(vmem_limit_bytes=VMEM_LIMIT),
    )(*now, *later)


def _shard_push(srcs, dsts, send_sems, recv_sems, local_sems):
    def remote(w, k, slot):
        x, y, c = _me()
        return pltpu.make_async_remote_copy(
            src_ref=srcs[w], dst_ref=dsts[w].at[slot], send_sem=send_sems.at[w, k - 1],
            recv_sem=recv_sems.at[w, k - 1], device_id=_chip_peer(x, y, c, k), device_id_type=MESH)

    def local(w):
        x, y, _ = _me()
        return pltpu.make_async_copy(srcs[w], dsts[w].at[2 * x + y], local_sems.at[w])

    def start():
        x, y, _ = _me()
        for w in range(len(srcs)):
            local(w).start()
            for k in (1, 2, 3):
                remote(w, k, 2 * x + y).start()

    def wait():
        x, y, _ = _me()
        for w in range(len(srcs)):
            for k in (1, 2, 3):
                remote(w, k, (2 * x + y) ^ k).wait_recv()
        for w in range(len(srcs)):
            for k in (1, 2, 3):
                remote(w, k, 2 * x + y).wait_send()
            local(w).wait()

    return start, wait


def _grad_push(srcs, dsts, send_sems, recv_sems):
    def copy(w, k):
        x, y, c = _me()
        px, py, pc = x ^ (k >> 2), y ^ ((k >> 1) & 1), c ^ (k & 1)
        return pltpu.make_async_remote_copy(
            src_ref=srcs[w].at[2 * px + py, pc], dst_ref=dsts[w].at[k - 1], send_sem=send_sems.at[w, k - 1],
            recv_sem=recv_sems.at[w, k - 1], device_id=(px, py, pc), device_id_type=MESH)

    def start():
        for w in range(len(srcs)):
            for k in range(1, N_DEV):
                copy(w, k).start()

    def wait():
        for w in range(len(srcs)):
            for k in range(1, N_DEV):
                copy(w, k).wait_recv()
        for w in range(len(srcs)):
            for k in range(1, N_DEV):
                copy(w, k).wait_send()

    return start, wait


def _grad_push_specs(grads):
    n = len(grads)
    return ([ANY] * n, [ANY] * n, [jax.ShapeDtypeStruct((N_DEV - 1,) + g.shape[2:], g.dtype) for g in grads],
            [pltpu.SemaphoreType.DMA((n, N_DEV - 1)), pltpu.SemaphoreType.DMA((n, N_DEV - 1))])


def add_eight(own, parts, jc_idx, name):
    _, half, c = parts.shape
    tr = half // 2 if (half // 2) % 16 == 0 else half

    def body(jc_ref, own_ref, p_ref, out_ref):
        acc = own_ref[0, 0].astype(f32)
        for k in range(N_DEV - 1):
            acc = acc + p_ref[k].astype(f32)
        out_ref[0] = acc

    return pl.pallas_call(
        body, name=name,
        grid_spec=pltpu.PrefetchScalarGridSpec(
            num_scalar_prefetch=1, grid=(half // tr,),
            in_specs=[pl.BlockSpec((1, 1, tr, c), lambda t, jc: (jc[0], jc[1], t, 0)),
                      pl.BlockSpec((N_DEV - 1, tr, c), lambda t, jc: (0, t, 0))],
            out_specs=pl.BlockSpec((1, tr, c), lambda t, jc: (jc[1], t, 0))),
        out_shape=jax.ShapeDtypeStruct((2, half, c), f32),
        compiler_params=_cparams(("parallel",)),
    )(jc_idx, own, parts)


def _push_specs(shards):
    n = len(shards)
    return ([ANY] * n, [ANY] * n, [jax.ShapeDtypeStruct((N_SHARD,) + s.shape, s.dtype) for s in shards],
            [pltpu.SemaphoreType.DMA((n, 3)), pltpu.SemaphoreType.DMA((n, 3)), pltpu.SemaphoreType.DMA((n,))])


def pair_exchange_halves(grads, small):
    n = len(grads)

    def body(*refs):
        ins, small_ref = refs[:n], refs[n]
        outs, gath = refs[n + 1:2 * n + 1], refs[2 * n + 1]
        send_sems, recv_sems, s_send, s_recv, local_sem = refs[2 * n + 2:]
        x, y, c = _me()
        me = 4 * x + 2 * y + c
        sends = []
        for w in range(n):
            half = ins[w].shape[1] // 2
            cp = pltpu.make_async_remote_copy(
                src_ref=ins[w].at[:, pl.ds((1 - c) * half, half), :], dst_ref=outs[w],
                send_sem=send_sems.at[w], recv_sem=recv_sems.at[w], device_id=(x, y, 1 - c), device_id_type=MESH)
            cp.start()
            sends.append(cp)
        loc = pltpu.make_async_copy(small_ref, gath.at[me], local_sem)
        loc.start()
        for k in range(1, N_DEV):
            cp = pltpu.make_async_remote_copy(
                src_ref=small_ref, dst_ref=gath.at[me], send_sem=s_send.at[k - 1], recv_sem=s_recv.at[k - 1],
                device_id=(x ^ (k >> 2), y ^ ((k >> 1) & 1), c ^ (k & 1)), device_id_type=MESH)
            cp.start()
            sends.append(cp)
        for w in range(n):
            half = ins[w].shape[1] // 2
            pltpu.make_async_remote_copy(
                src_ref=ins[w].at[:, pl.ds(0, half), :], dst_ref=outs[w], send_sem=send_sems.at[w],
                recv_sem=recv_sems.at[w], device_id=(x, y, 1 - c), device_id_type=MESH).wait_recv()
        for k in range(1, N_DEV):
            pltpu.make_async_remote_copy(
                src_ref=small_ref, dst_ref=gath.at[me ^ k], send_sem=s_send.at[k - 1], recv_sem=s_recv.at[k - 1],
                device_id=(x ^ (k >> 2), y ^ ((k >> 1) & 1), c ^ (k & 1)), device_id_type=MESH).wait_recv()
        for cp in sends:
            cp.wait_send()
        loc.wait()

    return pl.pallas_call(
        body, name="pair_exchange_halves", in_specs=[ANY] * (n + 1), out_specs=[ANY] * (n + 1),
        out_shape=[jax.ShapeDtypeStruct((g.shape[0], g.shape[1] // 2, g.shape[2]), f32) for g in grads]
        + [jax.ShapeDtypeStruct((N_DEV,) + small.shape, f32)],
        scratch_shapes=[pltpu.SemaphoreType.DMA((n,)), pltpu.SemaphoreType.DMA((n,)),
                        pltpu.SemaphoreType.DMA((N_DEV - 1,)), pltpu.SemaphoreType.DMA((N_DEV - 1,)),
                        pltpu.SemaphoreType.DMA],
    )(*grads, small)


def chip_exchange(sums):
    n = len(sums)

    def body(*refs):
        ins, outs = refs[:n], refs[n:2 * n]
        send_sems, recv_sems = refs[2 * n:]
        x, y, c = _me()
        j = 2 * x + y
        sends = []
        for w in range(n):
            for k in (1, 2, 3):
                cp = pltpu.make_async_remote_copy(
                    src_ref=ins[w].at[j ^ k], dst_ref=outs[w].at[k - 1], send_sem=send_sems.at[w, k - 1],
                    recv_sem=recv_sems.at[w, k - 1], device_id=_chip_peer(x, y, c, k), device_id_type=MESH)
                cp.start()
                sends.append(cp)
        for w in range(n):
            for k in (1, 2, 3):
                pltpu.make_async_remote_copy(
                    src_ref=ins[w].at[0], dst_ref=outs[w].at[k - 1], send_sem=send_sems.at[w, k - 1],
                    recv_sem=recv_sems.at[w, k - 1], device_id=_chip_peer(x, y, c, k), device_id_type=MESH).wait_recv()
        for cp in sends:
            cp.wait_send()

    return pl.pallas_call(
        body, name="chip_exchange", in_specs=[ANY] * n, out_specs=[ANY] * n,
        out_shape=[jax.ShapeDtypeStruct((N_SHARD - 1,) + s.shape[1:], s.dtype) for s in sums],
        scratch_shapes=[pltpu.SemaphoreType.DMA((n, 3)), pltpu.SemaphoreType.DMA((n, 3))],
    )(*sums)


def pair_exchange_results(halves):
    n = len(halves)

    def body(*refs):
        ins, outs = refs[:n], refs[n:2 * n]
        send_sems, recv_sems = refs[2 * n:]
        x, y, c = _me()
        sends = []
        for w in range(n):
            cp = pltpu.make_async_remote_copy(
                src_ref=ins[w].at[c], dst_ref=outs[w].at[c], send_sem=send_sems.at[w], recv_sem=recv_sems.at[w],
                device_id=(x, y, 1 - c), device_id_type=MESH)
            cp.start()
            sends.append(cp)
        for w in range(n):
            pltpu.make_async_remote_copy(
                src_ref=ins[w].at[c], dst_ref=outs[w].at[1 - c], send_sem=send_sems.at[w],
                recv_sem=recv_sems.at[w], device_id=(x, y, 1 - c), device_id_type=MESH).wait_recv()
        for cp in sends:
            cp.wait_send()

    return pl.pallas_call(
        body, name="pair_exchange_results", in_specs=[ANY] * n, out_specs=[ANY] * n,
        out_shape=[jax.ShapeDtypeStruct(h.shape, f32) for h in halves],
        input_output_aliases={w: w for w in range(n)},
        scratch_shapes=[pltpu.SemaphoreType.DMA((n,)), pltpu.SemaphoreType.DMA((n,))],
    )(*halves)


def add_pair(grad, other, c_idx, name):
    _, r, c = grad.shape
    half = r // 2
    tr = half // 2 if (half // 2) % 8 == 0 else half
    per = half // tr

    def body(c_ref, g_ref, o_ref, out_ref):
        out_ref[...] = (g_ref[...] + o_ref[...]).astype(bf16)

    return pl.pallas_call(
        body, name=name,
        grid_spec=pltpu.PrefetchScalarGridSpec(
            num_scalar_prefetch=1, grid=(N_SHARD, per),
            in_specs=[pl.BlockSpec((1, tr, c), lambda j, t, cr: (j, cr[0] * per + t, 0)),
                      pl.BlockSpec((1, tr, c), lambda j, t, cr: (j, t, 0))],
            out_specs=pl.BlockSpec((1, tr, c), lambda j, t, cr: (j, t, 0))),
        out_shape=jax.ShapeDtypeStruct((N_SHARD, half, c), bf16),
        compiler_params=_cparams(("parallel", "parallel")),
    )(c_idx, grad, other)


def add_four(own, parts, jc_idx, name):
    _, half, c = parts.shape
    tr = half // 2 if (half // 2) % 8 == 0 else half

    def body(jc_ref, own_ref, p_ref, out_ref):
        acc = own_ref[0].astype(f32)
        for k in range(N_SHARD - 1):
            acc = acc + p_ref[k].astype(f32)
        out_ref[0] = acc

    return pl.pallas_call(
        body, name=name,
        grid_spec=pltpu.PrefetchScalarGridSpec(
            num_scalar_prefetch=1, grid=(half // tr,),
            in_specs=[pl.BlockSpec((1, tr, c), lambda t, jc: (jc[0], t, 0)),
                      pl.BlockSpec((N_SHARD - 1, tr, c), lambda t, jc: (0, t, 0))],
            out_specs=pl.BlockSpec((1, tr, c), lambda t, jc: (jc[1], t, 0))),
        out_shape=jax.ShapeDtypeStruct((2, half, c), f32),
        compiler_params=_cparams(("parallel",)),
    )(jc_idx, own, parts)


def sum_devices(gathered):
    def body(g_ref, out_ref):
        acc = g_ref[0]
        for d in range(1, N_DEV):
            acc = acc + g_ref[d]
        out_ref[...] = acc

    return pl.pallas_call(body, name="sum_devices", out_shape=jax.ShapeDtypeStruct(gathered.shape[1:], f32))(gathered)


def _rows128(a, rows):
    flat = a.reshape(-1, BLOCK) if a.size % BLOCK == 0 else jnp.pad(a.reshape(1, -1), ((0, 0), (0, BLOCK - a.size)))
    return jnp.pad(flat, ((0, rows - flat.shape[0]), (0, 0)))


def kernel(x, meta_tokens, ln_emb_g, ln_emb_b, w_in, hg_lower_bounds, hg_norm_g, attn_sinks, w_branch_hg, w_branch_attn, w_out, ln1_g, ln1_b, w_ffn_in, w_ffn_out, ln2_g, ln2_b, loss_target, m_meta_tokens, m_ln_emb_g, m_ln_emb_b, m_w_in, m_hg_lower_bounds, m_hg_norm_g, m_attn_sinks, m_w_branch_hg, m_w_branch_attn, m_w_out, m_ln1_g, m_ln1_b, m_w_ffn_in, m_w_ffn_out, m_ln2_g, m_ln2_b, v_meta_tokens, v_ln_emb_g, v_ln_emb_b, v_w_in, v_hg_lower_bounds, v_hg_norm_g, v_attn_sinks, v_w_branch_hg, v_w_branch_attn, v_w_out, v_ln1_g, v_ln1_b, v_w_ffn_in, v_w_ffn_out, v_ln2_g, v_ln2_b):
    seq = x.shape[1]
    nb = seq // BLOCK + 1
    xs = x[0]
    ts = loss_target[0]
    ix, iy, ic = _me()
    shard = 2 * ix + iy
    vec = lambda a: a.reshape(1, D_MODEL)

    g_in, g_meta, s_bh, s_ba, s_out, s_fi, s_fo = gather_weights(
        [w_in[0], meta_tokens], [w_branch_hg[0], w_branch_attn[0], w_out[0], w_ffn_in[0], w_ffn_out[0]])
    by_cols = lambda g: g.reshape(N_SHARD, -1, g.shape[-1]).transpose(1, 0, 2).reshape(-1, N_SHARD * g.shape[-1])
    wf_in = by_cols(g_in)
    metablk = jnp.pad(by_cols(g_meta), ((TM - N_META, 0), (0, 0)))

    pos = jnp.arange(nb * BLOCK, dtype=jnp.int32) - PAD
    half = HEAD_DIM // 2
    inv = ROPE_THETA ** (-jnp.arange(half, dtype=f32) / half)
    ang = pos.astype(f32)[:, None] * inv[None, :]
    cos = jnp.tile(jnp.cos(ang), (1, BLOCK // half))
    sin = jnp.tile(jnp.sin(ang), (1, BLOCK // half))
    sinks8 = jnp.broadcast_to(attn_sinks.reshape(ATT_HEADS, 1), (ATT_HEADS, BLOCK))
    ng = hg_norm_g.reshape(1, HG_K)

    h0, h0b, pa, pg = emb_inproj(xs, metablk, vec(ln_emb_g), vec(ln_emb_b), wf_in)
    og, sprev, g_fi = hgrn_fwd(pa, hg_lower_bounds, ng, nb, [s_fi])
    oatt, g_fo, g_out, g_bh, g_ba = attn_fwd(pa, cos, sin, sinks8, nb, [s_fo, s_out, s_bh, s_ba])
    wf_bh, wf_ba, wf_fi = by_cols(g_bh), by_cols(g_ba), by_cols(g_fi)
    wf_out = g_out.reshape(D_MODEL, D_MODEL)
    wf_fo = g_fo.reshape(D_FF, D_MODEL)
    h1, h1b, mixin, og, oatt = mid_front(h0, pg, og, oatt, wf_bh, wf_ba, wf_out, ln1_g, ln1_b)
    dh1, dau, sact, dr2, loss_part, dg2, db2 = mid_ffn(h1, ts, wf_fi, wf_fo, ln2_g, ln2_b)
    dh0p, dpg, dog, doa, dyh, dya, dr1, dg1, db1 = mid_back(dh1, h0, pg, og, oatt, wf_bh, wf_ba, wf_out, ln1_g, ln1_b)
    steps = h0.shape[0] // TM
    tp = TM * max(k for k in (3, 2, 1) if steps % k == 0)
    pieces = lambda g: g.reshape(N_SHARD, 2, -1, g.shape[-1])
    gb_bh = pieces(wgrad(og, dyh, "wgrad_bh", 512, D_MODEL, tp, True, bf16))
    gb_ba = pieces(wgrad(oatt, dya, "wgrad_ba", 512, D_MODEL, tp, True, bf16))
    gb_out = pieces(wgrad(mixin, dr1, "wgrad_out", D_MODEL, D_MODEL, tp, False, bf16))
    gb_fi = pieces(wgrad(h1b, dau, "wgrad_fi", D_MODEL, D_FF, tp, True, bf16))
    gb_fo = pieces(wgrad(sact, dr2, "wgrad_fo", D_FF // 2, D_MODEL, tp, False, bf16))
    dhq, dhf, dhi, dhg, dlb4, dng, r_fi, r_fo = hgrn_bwd(pa, hg_lower_bounds, ng, sprev, dog, nb, [gb_fi, gb_fo])
    daq, dkc, dkp, dvc, dvp, dkm, dvm, dsk, r_out, r_bh, r_ba = attn_bwd(pa, cos, sin, sinks8, doa, nb,
                                                                         [gb_out, gb_bh, gb_ba])
    dproj, dx, dmeta, dlg, dlb = inproj_bwd(dh0p, dhq, dhf, dhi, dhg, daq, dkc, dkp, dvc, dvp, dkm, dvm, dpg,
                                      wf_in, xs, metablk, vec(ln_emb_g), vec(ln_emb_b))
    gw_in = wgrad(h0b, dproj, "wgrad_in", D_MODEL, IN_W // 2, tp, True)

    parts = [(dlg, 8), (dlb, 8), (dlb4, 8), (dng, 8), (dsk[:, 0], 8),
             (dg1, 8), (db1, 8), (dg2, 8), (db2, 8), (dmeta, BLOCK), (loss_part, 8)]
    small = jnp.concatenate([_rows128(a, r) for a, r in parts], axis=0)

    c_idx = jnp.reshape(ic, (1,)).astype(jnp.int32)
    jc_idx = jnp.stack([shard, ic]).astype(jnp.int32)
    other_in, gathered = pair_exchange_halves([gw_in], small)
    sum_in = add_pair(gw_in, other_in, c_idx, "add_pair_in")
    quad_in, = chip_exchange([sum_in])
    halves = [add_four(sum_in, quad_in, jc_idx, "add_four_in")]
    halves += [add_eight(g, r, jc_idx, "add_eight_" + nm) for nm, g, r in
               (("bh", gb_bh, r_bh), ("ba", gb_ba, r_ba), ("out", gb_out, r_out), ("fi", gb_fi, r_fi),
                ("fo", gb_fo, r_fo))]
    red = [r.reshape(-1, r.shape[-1]) for r in pair_exchange_results(halves)]
    small_sum = sum_devices(gathered)

    offs, acc = [], 0
    for _, r in parts:
        offs.append(acc)
        acc += r
    take = lambda n, size: small_sum[offs[n]:offs[n] + parts[n][1]].reshape(-1)[:size]
    g_meta_full = take(9, N_META * D_MODEL).reshape(N_META, D_MODEL)
    g_small = {
        "meta_tokens": lax.dynamic_slice_in_dim(g_meta_full, shard * (D_MODEL // N_SHARD), D_MODEL // N_SHARD, axis=1),
        "ln_emb_g": take(0, D_MODEL), "ln_emb_b": take(1, D_MODEL),
        "hg_lower_bounds": take(2, 2 * HG_HEADS * HG_K).reshape(2, HG_HEADS * HG_K),
        "hg_norm_g": take(3, HG_K).reshape(1, HG_K), "attn_sinks": take(4, ATT_HEADS).reshape(1, ATT_HEADS),
        "ln1_g": take(5, D_MODEL).reshape(1, D_MODEL), "ln1_b": take(6, D_MODEL).reshape(1, D_MODEL),
        "ln2_g": take(7, D_MODEL).reshape(1, D_MODEL), "ln2_b": take(8, D_MODEL).reshape(1, D_MODEL),
    }
    g_big = {"w_in": red[0], "w_branch_hg": red[1], "w_branch_attn": red[2], "w_out": red[3],
             "w_ffn_in": red[4], "w_ffn_out": red[5]}

    names = ["meta_tokens", "ln_emb_g", "ln_emb_b", "w_in", "hg_lower_bounds", "hg_norm_g", "attn_sinks",
             "w_branch_hg", "w_branch_attn", "w_out", "ln1_g", "ln1_b", "w_ffn_in", "w_ffn_out", "ln2_g", "ln2_b"]
    given = dict(
        meta_tokens=(meta_tokens, m_meta_tokens, v_meta_tokens), ln_emb_g=(ln_emb_g, m_ln_emb_g, v_ln_emb_g),
        ln_emb_b=(ln_emb_b, m_ln_emb_b, v_ln_emb_b), w_in=(w_in, m_w_in, v_w_in),
        hg_lower_bounds=(hg_lower_bounds, m_hg_lower_bounds, v_hg_lower_bounds),
        hg_norm_g=(hg_norm_g, m_hg_norm_g, v_hg_norm_g), attn_sinks=(attn_sinks, m_attn_sinks, v_attn_sinks),
        w_branch_hg=(w_branch_hg, m_w_branch_hg, v_w_branch_hg),
        w_branch_attn=(w_branch_attn, m_w_branch_attn, v_w_branch_attn), w_out=(w_out, m_w_out, v_w_out),
        ln1_g=(ln1_g, m_ln1_g, v_ln1_g), ln1_b=(ln1_b, m_ln1_b, v_ln1_b), w_ffn_in=(w_ffn_in, m_w_ffn_in, v_w_ffn_in),
        w_ffn_out=(w_ffn_out, m_w_ffn_out, v_w_ffn_out), ln2_g=(ln2_g, m_ln2_g, v_ln2_g), ln2_b=(ln2_b, m_ln2_b, v_ln2_b))
    out_g, out_d, out_m, out_v = [], [], [], []
    for nm in names:
        w, m, v = given[nm]
        shape = w.shape
        g = g_big[nm] if nm in g_big else g_small[nm]
        two_d = (lambda a: a.reshape(8, BLOCK)) if w.ndim == 1 else (lambda a: a.reshape(a.shape[-2], a.shape[-1]))
        d, mn, vn = adamw(two_d(w), two_d(g), two_d(m), two_d(v), "adamw_" + nm)
        out_g.append(g.reshape(shape))
        out_d.append(d.reshape(shape))
        out_m.append(mn.reshape(shape))
        out_v.append(vn.reshape(shape))

    loss = take(10, 1)[0]
    grad_x = dx.reshape(x.shape)
    return (loss, grad_x, *out_g, *out_d, *out_m, *out_v)
```

```python
import functools

import jax
import jax.numpy as jnp
from jax import lax
from jax.experimental import pallas as pl
from jax.experimental.pallas import tpu as pltpu

f32 = jnp.float32
bf16 = jnp.bfloat16

D_MODEL = 1024
BLOCK = 128
N_META = 16
PAD = BLOCK - N_META
HG_HEADS = 4
HG_K = 128
SUB = 16
ATT_HEADS = 8
HEAD_DIM = 64
ATT_QW = ATT_HEADS * HEAD_DIM
D_FF = 2816
EPS = 1e-5
ALPHA = 2.0 ** 0.25
ROPE_THETA = 10000.0
N_A = 2816
N_G = 2048
IN_W = N_A + N_G
N_SHARD = 4
N_DEV = 8

ADAM_LR = 0.001
ADAM_B1 = 0.9
ADAM_B2 = 0.999
ADAM_EPS = 1e-08
ADAM_WD = 0.01
ADAM_STEP = 10

TM = 256
LEAD = TM // BLOCK - 1

VMEM_LIMIT = 56 * 1024 * 1024
MESH = pl.DeviceIdType.MESH


def _cparams(sem, vmem=VMEM_LIMIT):
    return pltpu.CompilerParams(dimension_semantics=sem, vmem_limit_bytes=vmem)


def _const_spec(shape):
    zeros = (0,) * len(shape)
    return pl.BlockSpec(shape, lambda *_: zeros, pipeline_mode=pl.Buffered(1))


def _dot(a, b, ca, cb):
    return lax.dot_general(a.astype(bf16), b.astype(bf16), (((ca,), (cb,)), ((), ())),
                           preferred_element_type=f32)


@jax.custom_vjp
def mm(a, b):
    return _dot(a, b, 1, 0)


mm.defvjp(lambda a, b: (_dot(a, b, 1, 0), (a, b)),
          lambda r, g: (_dot(g, r[1], 1, 1), _dot(r[0], g, 0, 0)))


@jax.custom_vjp
def mm_nt(a, b):
    return _dot(a, b, 1, 1)


mm_nt.defvjp(lambda a, b: (_dot(a, b, 1, 1), (a, b)),
             lambda r, g: (_dot(g, r[1], 1, 0), _dot(g, r[0], 0, 0)))


@jax.custom_vjp
def mm_tn(a, b):
    return _dot(a, b, 0, 0)


mm_tn.defvjp(lambda a, b: (_dot(a, b, 0, 0), (a, b)),
             lambda r, g: (_dot(r[1], g, 1, 1), _dot(r[0], g, 1, 0)))


@functools.partial(jax.custom_vjp, nondiff_argnums=(1,))
def roll_lanes(x, shift):
    return pltpu.roll(x, shift, 1)


roll_lanes.defvjp(lambda x, shift: (pltpu.roll(x, shift, 1), None),
                  lambda shift, _, g: (pltpu.roll(g, (128 - shift) % 128, 1),))


@jax.custom_vjp
def _sigmoid(x):
    return 1.0 / (1.0 + jnp.exp(-x))


def _sigmoid_fwd(x):
    s = 1.0 / (1.0 + jnp.exp(-x))
    return s, s


_sigmoid.defvjp(_sigmoid_fwd, lambda s, g: (g * s * (1.0 - s),))


@jax.custom_vjp
def _recip(x):
    return 1.0 / x


def _recip_fwd(x):
    r = 1.0 / x
    return r, r


_recip.defvjp(_recip_fwd, lambda r, g: (-g * r * r,))


def _ln_stats(x):
    mu = jnp.mean(x, axis=-1, keepdims=True)
    xc = x - mu
    var = jnp.mean(xc * xc, axis=-1, keepdims=True)
    rs = lax.rsqrt(var + EPS)
    return xc * rs, rs


def _ln_bwd(dy, xh, rs, g):
    dxh = dy * g
    m1 = jnp.mean(dxh, axis=-1, keepdims=True)
    m2 = jnp.mean(dxh * xh, axis=-1, keepdims=True)
    return rs * (dxh - m1 - xh * m2)


def _row_ids(i):
    return i * BLOCK + lax.broadcasted_iota(jnp.int32, (BLOCK, 1), 0)


def _tm_rows(i):
    return i * TM + lax.broadcasted_iota(jnp.int32, (TM, 1), 0)


def _tm_row(n):
    return pl.BlockSpec((TM, n), lambda i: (i, 0))


def _tm_tokens():
    return pl.BlockSpec((TM, D_MODEL), lambda i: (jnp.maximum(i - 1, 0), 0))


def emb_inproj(x, metablk, g, b, w_in):
    nsteps = x.shape[0] // TM + 1

    def body(x_ref, mb_ref, g_ref, b_ref, w_ref, h0_ref, h0b_ref, pa_ref, pg_ref):
        i = pl.program_id(0)
        xb = jnp.where(i == 0, mb_ref[...], x_ref[...])
        xh, _ = _ln_stats(xb)
        y = xh * g_ref[...] + b_ref[...]
        y = jnp.where(_tm_rows(i) >= TM - N_META, y, 0.0)
        h0_ref[...] = y
        yb = y.astype(bf16)
        h0b_ref[...] = yb
        pa_ref[...] = jnp.dot(yb, w_ref[:, :N_A], preferred_element_type=f32)
        pg_ref[...] = jnp.dot(yb, w_ref[:, N_A:], preferred_element_type=f32)

    p = nsteps * TM
    row = _tm_row
    return pl.pallas_call(
        body, name="emb_inproj", grid=(nsteps,),
        in_specs=[_tm_tokens(),
                  _const_spec((TM, D_MODEL)), _const_spec((1, D_MODEL)), _const_spec((1, D_MODEL)),
                  _const_spec((D_MODEL, IN_W))],
        out_specs=[row(D_MODEL), row(D_MODEL), row(N_A), row(N_G)],
        out_shape=[jax.ShapeDtypeStruct((p, D_MODEL), f32), jax.ShapeDtypeStruct((p, D_MODEL), bf16),
                   jax.ShapeDtypeStruct((p, N_A), f32), jax.ShapeDtypeStruct((p, N_G), f32)],
        compiler_params=_cparams(("parallel",)),
    )(x, metablk, g, b, w_in)


def _hgrn_chunk(valid, st, hq, hf, hi, hg, lbraw, ng):
    lb = _sigmoid(lbraw[0:1] - lbraw[1:2])
    q = hq * _sigmoid(hq)
    fg = lb + (1.0 - lb) * _sigmoid(hf)
    logf = jnp.where(valid, jnp.log(fg), 0.0)
    k = jnp.where(valid, 1.0 - fg, 0.0)
    v = hi
    r = lax.broadcasted_iota(jnp.int32, (BLOCK, BLOCK), 0)
    c = lax.broadcasted_iota(jnp.int32, (BLOCK, BLOCK), 1)
    tril = (c <= r).astype(f32)
    bcum = jnp.dot(tril, logf, precision=lax.Precision.HIGHEST, preferred_element_type=f32)
    blast = bcum[BLOCK - 1:BLOCK]
    rows = lax.broadcasted_iota(jnp.int32, (BLOCK, 1), 0)
    sub8 = lax.broadcasted_iota(jnp.int32, (BLOCK // 8, 8, HG_K), 1)
    b8 = bcum.reshape(BLOCK // 8, 8, HG_K)
    row_of_8 = lambda j: jnp.broadcast_to(b8[:, j:j + 1, :], b8.shape)
    a = jnp.where(r == c, jnp.sum(q * k, axis=-1, keepdims=True), 0.0)
    seg = BLOCK
    while seg >= 2:
        half = seg // 2
        if seg >= 8:
            bs = bcum.reshape(BLOCK // seg, seg, HG_K)
            ref = jnp.broadcast_to(bs[:, half - 1:half, :], bs.shape)
        elif seg == 4:
            ref = jnp.where(sub8 < 4, row_of_8(1), row_of_8(5))
        else:
            ref = jnp.where(sub8 < 2, row_of_8(0), jnp.where(sub8 < 4, row_of_8(2),
                                                             jnp.where(sub8 < 6, row_of_8(4), row_of_8(6))))
        ref = ref.reshape(BLOCK, HG_K)
        upper = (rows % seg) >= half
        q_up = q * jnp.exp(jnp.where(upper, bcum - ref, -jnp.inf))
        k_lo = k * jnp.exp(jnp.where(upper, -jnp.inf, ref - bcum))
        a = a + jnp.where((r // seg) == (c // seg), mm_nt(q_up, k_lo), 0.0)
        seg = half
    o = mm_nt(q * jnp.exp(bcum), st) + mm(a, v)
    st_new = st * jnp.exp(blast) + mm_tn(v, k * jnp.exp(blast - bcum))
    on = o * lax.rsqrt(jnp.mean(o * o, axis=-1, keepdims=True) + EPS) * ng
    return st_new, on * (hg * _sigmoid(hg))


def _hgrn_in_specs(rowmap):
    wide = lambda col: pl.BlockSpec((BLOCK, HG_HEADS * HG_K), lambda i: (rowmap(i) + LEAD, col))
    return [wide(0), wide(1), wide(2), wide(3), _const_spec((2, HG_HEADS * HG_K)), _const_spec((1, HG_K))]


def _head(ref, h):
    return ref[:, h * HG_K:(h + 1) * HG_K]


def hgrn_fwd(pa, lbraw, ng, nb, shards):
    n = len(shards)

    def body(hq_ref, hf_ref, hi_ref, hg_ref, lb_ref, ng_ref, *rest):
        srcs, (og_ref, sp_ref), dsts = rest[:n], rest[n:n + 2], rest[n + 2:2 * n + 2]
        st_ref = rest[2 * n + 2]
        start, wait = _shard_push(srcs, dsts, *rest[2 * n + 3:])
        i = pl.program_id(0)

        @pl.when(i == 0)
        def _():
            st_ref[...] = jnp.zeros_like(st_ref)
            start()

        @pl.when(i == nb - 1)
        def _():
            wait()

        valid = _row_ids(i) >= PAD
        for h in range(HG_HEADS):
            st = st_ref[h]
            sp_ref[0, h] = st
            st_new, out = _hgrn_chunk(valid, st, _head(hq_ref, h), _head(hf_ref, h), _head(hi_ref, h),
                                      _head(hg_ref, h), _head(lb_ref, h), ng_ref[...])
            st_ref[h] = st_new
            og_ref[:, h * HG_K:(h + 1) * HG_K] = out.astype(bf16)

    p = (nb + LEAD) * BLOCK
    push_in, push_out, push_shape, push_scratch = _push_specs(shards)
    return pl.pallas_call(
        body, name="hgrn_fwd", grid=(nb,),
        in_specs=_hgrn_in_specs(lambda i: i) + push_in,
        out_specs=[pl.BlockSpec((BLOCK, HG_HEADS * HG_K), lambda i: (i + LEAD, 0)),
                   pl.BlockSpec((1, HG_HEADS, HG_K, HG_K), lambda i: (i, 0, 0, 0))] + push_out,
        out_shape=[jax.ShapeDtypeStruct((p, HG_HEADS * HG_K), bf16),
                   jax.ShapeDtypeStruct((nb, HG_HEADS, HG_K, HG_K), f32)] + push_shape,
        scratch_shapes=[pltpu.VMEM((HG_HEADS, HG_K, HG_K), f32)] + push_scratch,
        compiler_params=_cparams(("arbitrary",)),
    )(pa, pa, pa, pa, lbraw, ng, *shards)


def hgrn_bwd(pa, lbraw, ng, sprev, dog, nb, grads):
    n = len(grads)

    def body(hq_ref, hf_ref, hi_ref, hg_ref, lb_ref, ng_ref, sp_ref, do_ref, *rest):
        srcs, rest = rest[:n], rest[n:]
        dq_ref, df_ref, di_ref, dg_ref, dlb_ref, dng_ref = rest[:6]
        dsts, dst_ref = rest[6:6 + n], rest[6 + n]
        start, wait = _grad_push(srcs, dsts, *rest[7 + n:])
        i = pl.program_id(0)

        @pl.when(i == 0)
        def _():
            dst_ref[...] = jnp.zeros_like(dst_ref)
            dlb_ref[...] = jnp.zeros_like(dlb_ref)
            dng_ref[...] = jnp.zeros_like(dng_ref)
            start()

        valid = _row_ids(nb - 1 - i) >= PAD
        dng_sum = jnp.zeros((1, HG_K), f32)
        for h in range(HG_HEADS):
            cols = slice(h * HG_K, (h + 1) * HG_K)
            _, vjp = jax.vjp(functools.partial(_hgrn_chunk, valid), sp_ref[0, h], _head(hq_ref, h), _head(hf_ref, h),
                             _head(hi_ref, h), _head(hg_ref, h), _head(lb_ref, h), ng_ref[...])
            dst, dq, df, di, dg, dlb, dng = vjp((dst_ref[h], _head(do_ref, h)))
            dst_ref[h] = dst
            dq_ref[:, cols] = dq.astype(bf16)
            df_ref[:, cols] = df.astype(bf16)
            di_ref[:, cols] = di.astype(bf16)
            dg_ref[:, cols] = dg.astype(bf16)
            dlb_ref[:, cols] += dlb
            dng_sum = dng_sum + dng
        dng_ref[...] += dng_sum
        pl.when(i == nb - 1)(wait)

    p = (nb + LEAD) * BLOCK
    rev = lambda i: nb - 1 - i
    hw = HG_HEADS * HG_K
    blk = pl.BlockSpec((BLOCK, hw), lambda i: (rev(i) + LEAD, 0))
    wide = jax.ShapeDtypeStruct((p, hw), bf16)
    push_in, push_out, push_shape, push_scratch = _grad_push_specs(grads)
    return pl.pallas_call(
        body, name="hgrn_bwd", grid=(nb,),
        in_specs=_hgrn_in_specs(rev) + [pl.BlockSpec((1, HG_HEADS, HG_K, HG_K), lambda i: (rev(i), 0, 0, 0)), blk]
        + push_in,
        out_specs=[blk, blk, blk, blk, pl.BlockSpec((2, hw), lambda i: (0, 0)), pl.BlockSpec((1, HG_K), lambda i: (0, 0))]
        + push_out,
        out_shape=[wide, wide, wide, wide, jax.ShapeDtypeStruct((2, hw), f32), jax.ShapeDtypeStruct((1, HG_K), f32)]
        + push_shape,
        scratch_shapes=[pltpu.VMEM((HG_HEADS, HG_K, HG_K), f32)] + push_scratch,
        compiler_params=_cparams(("arbitrary",)),
    )(pa, pa, pa, pa, lbraw, ng, sprev, dog, *grads)


def _rope(x, cos, sin):
    lane = lax.broadcasted_iota(jnp.int32, x.shape, 1)
    rot = jnp.where(lane % HEAD_DIM < HEAD_DIM // 2, -roll_lanes(x, BLOCK - HEAD_DIM // 2),
                    roll_lanes(x, HEAD_DIM // 2))
    return x * cos + rot * sin


def _both_halves(x, g):
    lo = lax.broadcasted_iota(jnp.int32, x.shape, 1) < HEAD_DIM
    sw = roll_lanes(x, HEAD_DIM)
    return jnp.where(lo, x, sw) if g == 0 else jnp.where(lo, sw, x)


def _attn_block(band_ok, meta_ok, tabs, q, kp, kc, vp, vc, km, vm, *sinks):
    cq, sq, cp, sp, cm, sm = tabs
    neg = jnp.finfo(f32).min
    scale = HEAD_DIM ** -0.5
    group = ATT_HEADS // 2
    kpr, kcr, kmr = _rope(kp, cp, sp), _rope(kc, cq, sq), _rope(km, cm, sm)
    lo = lax.broadcasted_iota(jnp.int32, (BLOCK, BLOCK), 1) < HEAD_DIM
    t = lax.broadcasted_iota(jnp.int32, (group * BLOCK, BLOCK), 0) % BLOCK
    col = lax.broadcasted_iota(jnp.int32, (group * BLOCK, BLOCK), 1)
    own = col <= t
    is_sink = col == N_META
    qr =[_rope(q[:, m * BLOCK:(m + 1) * BLOCK], cq, sq) for m in range(ATT_HEADS // 2)]
    slabs = []
    for g in range(2):
        kp_g, kc_g, vp_g, vc_g, km_g, vm_g = [_both_halves(a, g) for a in (kpr, kcr, vp, vc, kmr, vm)]
        qs = jnp.concatenate([jnp.where(lo if h % 2 == 0 else ~lo, qr[2 * g + h // 2], 0.0) for h in range(group)],
                             axis=0)
        sink = jnp.concatenate([jnp.broadcast_to(sinks[group * g + h], (BLOCK, 1)) for h in range(group)], axis=0)
        sb = jnp.where(band_ok, jnp.where(own, mm_nt(qs, kc_g), mm_nt(qs, kp_g)) * scale, neg)
        no_keys = jnp.zeros((BLOCK - N_META, BLOCK), f32)
        sme = jnp.where(meta_ok, mm_nt(qs, jnp.concatenate([km_g, no_keys], axis=0)) * scale,
                        jnp.where(is_sink, sink, neg))
        mx = lax.stop_gradient(jnp.max(jnp.maximum(sb, sme), axis=-1, keepdims=True))
        eb, em = jnp.exp(sb - mx), jnp.exp(sme - mx)
        inv = _recip(jnp.sum(eb + em, axis=-1, keepdims=True))
        pb = eb * inv
        o = (mm(jnp.where(own, pb, 0.0), vc_g) + mm(jnp.where(own, 0.0, pb), vp_g)
             + mm(em * inv, jnp.concatenate([vm_g, no_keys], axis=0)))
        for m in range(2):
            slabs.append(jnp.where(lo, o[2 * m * BLOCK:(2 * m + 1) * BLOCK], o[(2 * m + 1) * BLOCK:(2 * m + 2) * BLOCK]))
    return jnp.concatenate(slabs, axis=1)


def _attn_masks(i):
    group = ATT_HEADS // 2
    t = lax.broadcasted_iota(jnp.int32, (group * BLOCK, BLOCK), 0) % BLOCK
    s = lax.broadcasted_iota(jnp.int32, (group * BLOCK, BLOCK), 1)
    kpos = jnp.where(s <= t, i * BLOCK - PAD + s, jnp.where(i > 0, (i - 1) * BLOCK - PAD + s, -1))
    band_ok = kpos >= N_META
    qpos = i * BLOCK - PAD + lax.broadcasted_iota(jnp.int32, (group * BLOCK, 1), 0) % BLOCK
    meta_ok = (s < N_META) & (s <= qpos)
    return band_ok, meta_ok


def _attn_in_specs(cur=lambda i: i):
    prev = lambda i: jnp.maximum(cur(i) - 1, 0)
    kcol, vcol = N_A // BLOCK - 2, N_A // BLOCK - 1
    blk = lambda rowmap, col: pl.BlockSpec((BLOCK, BLOCK), lambda i: (rowmap(i) + LEAD, col))
    tab = lambda rowmap: pl.BlockSpec((BLOCK, BLOCK), lambda i: (rowmap(i), 0))
    first = lambda i: 0
    return [pl.BlockSpec((BLOCK, ATT_QW), lambda i: (cur(i) + LEAD, 4)),
            blk(prev, kcol), blk(cur, kcol), blk(prev, vcol), blk(cur, vcol), blk(first, kcol), blk(first, vcol),
            tab(cur), tab(cur), tab(prev), tab(prev), tab(first), tab(first),
            _const_spec((ATT_HEADS, BLOCK))]


def _attn_row(n, cur=lambda i: i):
    return pl.BlockSpec((BLOCK, n), lambda i: (cur(i) + LEAD, 0))


def _attn_operands(q_ref, kp_ref, kc_ref, vp_ref, vc_ref, km_ref, vm_ref, cq, sq, cp, sp, cm, sm, sk_ref):
    tabs = (cq[...], sq[...], cp[...], sp[...], cm[PAD:, :], sm[PAD:, :])
    args = (q_ref[...], kp_ref[...], kc_ref[...], vp_ref[...], vc_ref[...], km_ref[PAD:, :], vm_ref[PAD:, :])
    sinks = tuple(sk_ref[j:j + 1, 0:1] for j in range(ATT_HEADS))
    return tabs, args + sinks


def attn_fwd(pa, cos, sin, sinks8, nb, shards):
    n = len(shards)
    n_in = 14

    def body(*refs):
        srcs, o_ref, dsts = refs[n_in:n_in + n], refs[n_in + n], refs[n_in + n + 1:n_in + 2 * n + 1]
        start, wait = _shard_push(srcs, dsts, *refs[n_in + 2 * n + 1:])
        i = pl.program_id(0)
        pl.when(i == 0)(start)
        band_ok, meta_ok = _attn_masks(i)
        tabs, args = _attn_operands(*refs[:n_in])
        o_ref[...] = _attn_block(band_ok, meta_ok, tabs, *args).astype(bf16)
        pl.when(i == nb - 1)(wait)

    push_in, push_out, push_shape, push_scratch = _push_specs(shards)
    return pl.pallas_call(
        body, name="attn_fwd", grid=(nb,), in_specs=_attn_in_specs() + push_in,
        out_specs=[_attn_row(ATT_QW)] + push_out,
        out_shape=[jax.ShapeDtypeStruct(((nb + LEAD) * BLOCK, ATT_QW), bf16)] + push_shape,
        scratch_shapes=push_scratch,
        compiler_params=_cparams(("arbitrary",)),
    )(pa, pa, pa, pa, pa, pa, pa, cos, sin, cos, sin, cos, sin, sinks8, *shards)


def attn_bwd(pa, cos, sin, sinks8, do, nb, grads):
    n = len(grads)

    def body(*refs):
        do_ref, srcs = refs[14], refs[15:15 + n]
        dq_ref, dkc_ref, dkp_ref, dvc_ref, dvp_ref, dkm_ref, dvm_ref, dsk_ref = refs[15 + n:23 + n]
        start, wait = _grad_push(srcs, refs[23 + n:23 + 2 * n], *refs[23 + 2 * n:])
        i = pl.program_id(0)

        @pl.when(i == 0)
        def _():
            dkm_ref[...] = jnp.zeros((N_META, BLOCK), f32)
            dvm_ref[...] = jnp.zeros((N_META, BLOCK), f32)
            dsk_ref[...] = jnp.zeros((ATT_HEADS, BLOCK), f32)
            start()

        band_ok, meta_ok = _attn_masks(i)
        tabs, args = _attn_operands(*refs[:14])
        _, vjp = jax.vjp(functools.partial(_attn_block, band_ok, meta_ok, tabs), *args)
        grads = vjp(do_ref[...])
        dq_ref[...] = grads[0].astype(bf16)
        dkp_ref[...] = grads[1]
        dkc_ref[...] = grads[2]
        dvp_ref[...] = grads[3]
        dvc_ref[...] = grads[4]
        dkm_ref[...] += grads[5]
        dvm_ref[...] += grads[6]
        for j in range(ATT_HEADS):
            dsk_ref[j:j + 1, :] += jnp.broadcast_to(grads[7 + j], (1, BLOCK))
        pl.when(i == nb - 1)(wait)

    p = (nb + LEAD) * BLOCK
    row = _attn_row(BLOCK)
    const = lambda r: pl.BlockSpec((r, BLOCK), lambda i: (0, 0))
    part = jax.ShapeDtypeStruct((p, BLOCK), f32)
    push_in, push_out, push_shape, push_scratch = _grad_push_specs(grads)
    return pl.pallas_call(
        body, name="attn_bwd", grid=(nb,),
        in_specs=_attn_in_specs() + [_attn_row(ATT_QW)] + push_in,
        out_specs=[_attn_row(ATT_QW), row, row, row, row,
                   const(N_META), const(N_META), const(ATT_HEADS)] + push_out,
        out_shape=[jax.ShapeDtypeStruct((p, ATT_QW), bf16), part, part, part, part,
                   jax.ShapeDtypeStruct((N_META, BLOCK), f32), jax.ShapeDtypeStruct((N_META, BLOCK), f32),
                   jax.ShapeDtypeStruct((ATT_HEADS, BLOCK), f32)] + push_shape,
        scratch_shapes=push_scratch,
        compiler_params=_cparams(("arbitrary",)),
    )(pa, pa, pa, pa, pa, pa, pa, cos, sin, cos, sin, cos, sin, sinks8, do, *grads)


def _mid_forward(h0_ref, pg_ref, og, oa, wbh_ref, wba_ref, wo_ref, g1, b1):
    yh = jnp.dot(og, wbh_ref[...], preferred_element_type=f32)
    ya = jnp.dot(oa, wba_ref[...], preferred_element_type=f32)
    gh = _sigmoid(pg_ref[:, :D_MODEL])
    ga = _sigmoid(pg_ref[:, D_MODEL:])
    mixin = (gh * yh + ga * ya).astype(bf16)
    r1 = ALPHA * h0_ref[...] + jnp.dot(mixin, wo_ref[...], preferred_element_type=f32)
    xh1, rs1 = _ln_stats(r1)
    return yh, ya, gh, ga, mixin, xh1, rs1, xh1 * g1 + b1


def _mid_weight_specs():
    hw = HG_HEADS * HG_K
    return [_const_spec((hw, D_MODEL)), _const_spec((ATT_QW, D_MODEL)), _const_spec((D_MODEL, D_MODEL)),
            _const_spec((1, D_MODEL)), _const_spec((1, D_MODEL))]


def mid_front(h0, pg, og, oatt, wbh, wba, wout, ln1g, ln1b):
    def body(h0_ref, pg_ref, og_ref, oa_ref, wbh_ref, wba_ref, wo_ref, g1_ref, b1_ref,
             h1_ref, h1b_ref, mix_ref, ogc_ref, oac_ref):
        used = _tm_rows(pl.program_id(0)) >= LEAD * BLOCK
        og = jnp.where(used, og_ref[...], jnp.zeros_like(og_ref))
        oa = jnp.where(used, oa_ref[...], jnp.zeros_like(oa_ref))
        ogc_ref[...] = og
        oac_ref[...] = oa
        *_, mixin, _, _, h1 = _mid_forward(h0_ref, pg_ref, og, oa, wbh_ref, wba_ref, wo_ref, g1_ref[...], b1_ref[...])
        mix_ref[...] = mixin
        h1_ref[...] = h1
        h1b_ref[...] = h1.astype(bf16)

    p = h0.shape[0]
    hw = HG_HEADS * HG_K
    sds = lambda n, dt: jax.ShapeDtypeStruct((p, n), dt)
    return pl.pallas_call(
        body, name="mid_front", grid=(p // TM,),
        in_specs=[_tm_row(D_MODEL), _tm_row(N_G), _tm_row(hw), _tm_row(ATT_QW)] + _mid_weight_specs(),
        out_specs=[_tm_row(D_MODEL), _tm_row(D_MODEL), _tm_row(D_MODEL), _tm_row(hw), _tm_row(ATT_QW)],
        out_shape=[sds(D_MODEL, f32), sds(D_MODEL, bf16), sds(D_MODEL, bf16), sds(hw, bf16), sds(ATT_QW, bf16)],
        compiler_params=_cparams(("parallel",)),
    )(h0, pg, og, oatt, wbh, wba, wout, ln1g, ln1b)


def mid_ffn(h1, target, wfi, wfo, ln2g, ln2b):
    def body(h1_ref, t_ref, wfi_ref, wfo_ref, g2_ref, b2_ref,
             dh1_ref, dau_ref, s_ref, dr2_ref, loss_ref, dg2_ref, db2_ref):
        i = pl.program_id(0)

        @pl.when(i == 0)
        def _():
            for r in (loss_ref, dg2_ref, db2_ref):
                r[...] = jnp.zeros_like(r)

        g2, b2 = g2_ref[...], b2_ref[...]
        h1 = h1_ref[...]
        au = jnp.dot(h1.astype(bf16), wfi_ref[...], preferred_element_type=f32)
        a, u = au[:, :D_FF], au[:, D_FF:]
        sg = _sigmoid(a)
        sa = a * sg
        s = (sa * u).astype(bf16)
        s_ref[...] = s
        r2 = ALPHA * h1 + jnp.dot(s, wfo_ref[...], preferred_element_type=f32)
        xh2, rs2 = _ln_stats(r2)
        diff = jnp.where(i > 0, xh2 * g2 + b2 - t_ref[...], 0.0)
        loss_ref[...] += jnp.sum(diff * diff) * (0.5 / D_MODEL)
        dy = diff * (1.0 / D_MODEL)
        dg2_ref[...] += jnp.sum(dy * xh2, axis=0, keepdims=True)
        db2_ref[...] += jnp.sum(dy, axis=0, keepdims=True)
        dr2 = _ln_bwd(dy, xh2, rs2, g2)
        dr2b = dr2.astype(bf16)
        dr2_ref[...] = dr2b
        ds = _dot(dr2b, wfo_ref[...], 1, 1)
        da = (ds * u) * (sg * (1.0 + a * (1.0 - sg)))
        du = ds * sa
        dau = jnp.concatenate([da, du], axis=1).astype(bf16)
        dau_ref[...] = dau
        dh1_ref[...] = ALPHA * dr2 + _dot(dau, wfi_ref[...], 1, 1)

    p = h1.shape[0]
    vec = lambda: pl.BlockSpec((1, D_MODEL), lambda i: (0, 0))
    sds = lambda n, dt: jax.ShapeDtypeStruct((p, n), dt)
    return pl.pallas_call(
        body, name="mid_ffn", grid=(p // TM,),
        in_specs=[_tm_row(D_MODEL), _tm_tokens(), _const_spec((D_MODEL, 2 * D_FF)), _const_spec((D_FF, D_MODEL)),
                  _const_spec((1, D_MODEL)), _const_spec((1, D_MODEL))],
        out_specs=[_tm_row(D_MODEL), _tm_row(2 * D_FF), _tm_row(D_FF), _tm_row(D_MODEL),
                   pl.BlockSpec((1, 1), lambda i: (0, 0)), vec(), vec()],
        out_shape=[sds(D_MODEL, f32), sds(2 * D_FF, bf16), sds(D_FF, bf16), sds(D_MODEL, bf16),
                   jax.ShapeDtypeStruct((1, 1), f32)] + [jax.ShapeDtypeStruct((1, D_MODEL), f32)] * 2,
        compiler_params=_cparams(("arbitrary",)),
    )(h1, target, wfi, wfo, ln2g, ln2b)


def mid_back(dh1, h0, pg, ogc, oac, wbh, wba, wout, ln1g, ln1b):
    def body(dh1_ref, h0_ref, pg_ref, og_ref, oa_ref, wbh_ref, wba_ref, wo_ref, g1_ref, b1_ref,
             dh0_ref, dpg_ref, dog_ref, doa_ref, dyh_ref, dya_ref, dr1_ref, dg1_ref, db1_ref):
        @pl.when(pl.program_id(0) == 0)
        def _():
            dg1_ref[...] = jnp.zeros_like(dg1_ref)
            db1_ref[...] = jnp.zeros_like(db1_ref)

        g1 = g1_ref[...]
        yh, ya, gh, ga, _, xh1, rs1, _ = _mid_forward(h0_ref, pg_ref, og_ref[...], oa_ref[...], wbh_ref, wba_ref,
                                                      wo_ref, g1, b1_ref[...])
        dh1 = dh1_ref[...]
        dg1_ref[...] += jnp.sum(dh1 * xh1, axis=0, keepdims=True)
        db1_ref[...] += jnp.sum(dh1, axis=0, keepdims=True)
        dr1 = _ln_bwd(dh1, xh1, rs1, g1)
        dr1b = dr1.astype(bf16)
        dr1_ref[...] = dr1b
        dh0_ref[...] = ALPHA * dr1
        dmix = _dot(dr1b, wo_ref[...], 1, 1)
        dyh = (dmix * gh).astype(bf16)
        dya = (dmix * ga).astype(bf16)
        dyh_ref[...] = dyh
        dya_ref[...] = dya
        dpg_ref[:, :D_MODEL] = (dmix * yh * gh * (1.0 - gh)).astype(bf16)
        dpg_ref[:, D_MODEL:] = (dmix * ya * ga * (1.0 - ga)).astype(bf16)
        dog_ref[...] = _dot(dyh, wbh_ref[...], 1, 1)
        doa_ref[...] = _dot(dya, wba_ref[...], 1, 1)

    p = h0.shape[0]
    hw = HG_HEADS * HG_K
    vec = lambda: pl.BlockSpec((1, D_MODEL), lambda i: (0, 0))
    sds = lambda n, dt: jax.ShapeDtypeStruct((p, n), dt)
    return pl.pallas_call(
        body, name="mid_back", grid=(p // TM,),
        in_specs=[_tm_row(D_MODEL), _tm_row(D_MODEL), _tm_row(N_G), _tm_row(hw), _tm_row(ATT_QW)] + _mid_weight_specs(),
        out_specs=[_tm_row(D_MODEL), _tm_row(N_G), _tm_row(hw), _tm_row(ATT_QW), _tm_row(D_MODEL), _tm_row(D_MODEL),
                   _tm_row(D_MODEL), vec(), vec()],
        out_shape=[sds(D_MODEL, f32), sds(N_G, bf16), sds(hw, f32), sds(ATT_QW, f32), sds(D_MODEL, bf16),
                   sds(D_MODEL, bf16), sds(D_MODEL, bf16)] + [jax.ShapeDtypeStruct((1, D_MODEL), f32)] * 2,
        compiler_params=_cparams(("arbitrary",)),
    )(dh1, h0, pg, ogc, oac, wbh, wba, wout, ln1g, ln1b)


def inproj_bwd(dh0p, dhq, dhf, dhi, dhg, daq, dkc, dkp, dvc, dvp, dkm, dvm, dpg, w_in, x, metablk, g, b):
    p = dh0p.shape[0]
    nbk = p // BLOCK
    per = TM // BLOCK

    def body(dh0_ref, dq_ref, df_ref, di_ref, dg_ref, daq_ref, dkc_ref, *rest):
        dkp_refs, dvc_ref, dvp_refs = rest[:per], rest[per], rest[per + 1:2 * per + 1]
        (dkm_ref, dvm_ref, dpg_ref, w_ref, x_ref, mb_ref, g_ref, b_ref,
         dproj_ref, dx_ref, dmeta_ref, dlg_ref, dlb_ref) = rest[2 * per + 1:]
        i = pl.program_id(0)

        @pl.when(i == 0)
        def _():
            dlg_ref[...] = jnp.zeros_like(dlg_ref)
            dlb_ref[...] = jnp.zeros_like(dlb_ref)

        zero_pad = jnp.zeros((TM - N_META, BLOCK), f32)
        first = i == 0
        rows = _tm_rows(i)

        def keys(cur_ref, next_refs, meta_ref):
            nxt = jnp.concatenate([jnp.where(per * i + 1 + m < nbk, next_refs[m][...], 0.0) for m in range(per)], axis=0)
            t = cur_ref[...] + nxt
            return t + jnp.where(first, jnp.concatenate([zero_pad, meta_ref[...]], axis=0), 0.0)

        dproj = jnp.concatenate(
            [dq_ref[...], df_ref[...], di_ref[...], dg_ref[...], daq_ref[...],
             keys(dkc_ref, dkp_refs, dkm_ref).astype(bf16), keys(dvc_ref, dvp_refs, dvm_ref).astype(bf16),
             dpg_ref[...]], axis=1)
        dproj = jnp.where(rows >= LEAD * BLOCK, dproj, jnp.zeros_like(dproj))
        dproj_ref[...] = dproj
        valid = rows >= TM - N_META
        dh0 = jnp.where(valid, dh0_ref[...] + _dot(dproj, w_ref[...], 1, 1), 0.0)
        xb = jnp.where(first, mb_ref[...], x_ref[...])
        xh, rs = _ln_stats(xb)
        dlg_ref[...] += jnp.sum(dh0 * xh, axis=0, keepdims=True)
        dlb_ref[...] += jnp.sum(dh0, axis=0, keepdims=True)
        dx = jnp.where(valid, _ln_bwd(dh0, xh, rs, g_ref[...]), 0.0)
        dx_ref[...] = dx

        @pl.when(first)
        def _():
            dmeta_ref[...] = dx[TM - N_META:, :]

    row = _tm_row
    nxt = [pl.BlockSpec((BLOCK, BLOCK), functools.partial(lambda i, m: (jnp.minimum(per * i + 1 + m, nbk - 1), 0), m=m))
           for m in range(per)]
    hw = HG_HEADS * HG_K
    vec = lambda: pl.BlockSpec((1, D_MODEL), lambda i: (0, 0))
    return pl.pallas_call(
        body, name="inproj_bwd", grid=(p // TM,),
        in_specs=[row(D_MODEL), row(hw), row(hw), row(hw), row(hw), row(ATT_QW),
                  row(BLOCK)] + nxt + [row(BLOCK)] + nxt + [_const_spec((N_META, BLOCK)), _const_spec((N_META, BLOCK)),
                  row(N_G), _const_spec((D_MODEL, IN_W)), _tm_tokens(),
                  _const_spec((TM, D_MODEL)), _const_spec((1, D_MODEL)), _const_spec((1, D_MODEL))],
        out_specs=[row(IN_W), _tm_tokens(), pl.BlockSpec((N_META, D_MODEL), lambda i: (0, 0)), vec(), vec()],
        out_shape=[jax.ShapeDtypeStruct((p, IN_W), bf16), jax.ShapeDtypeStruct((p - TM, D_MODEL), f32),
                   jax.ShapeDtypeStruct((N_META, D_MODEL), f32),
                   jax.ShapeDtypeStruct((1, D_MODEL), f32), jax.ShapeDtypeStruct((1, D_MODEL), f32)],
        compiler_params=_cparams(("arbitrary",)),
    )(dh0p, dhq, dhf, dhi, dhg, daq, dkc, *([dkp] * per), dvc, *([dvp] * per), dkm, dvm, dpg, w_in, x, metablk, g, b)


def wgrad(a, b, name, tk, tn, tp, by_cols, out_dtype=f32):
    p, k = a.shape
    n = b.shape[1]
    nsteps = p // tp

    def body(a_ref, b_ref, o_ref, acc_ref):
        ip = pl.program_id(2)

        @pl.when(ip == 0)
        def _():
            acc_ref[...] = jnp.zeros_like(acc_ref)

        acc_ref[...] += _dot(a_ref[...], b_ref[...], 0, 0)

        @pl.when(ip == nsteps - 1)
        def _():
            for j in range(span):
                o_ref[j] = acc_ref[:, j * width:(j + 1) * width].astype(out_dtype)

    span, width = 1, tn
    if by_cols:
        shard_n = n // N_SHARD
        out_shape = (N_SHARD, k, shard_n)
        if tn >= shard_n:
            span, width = tn // shard_n, shard_n
            omap = lambda ik, jn, ip: (jn, ik, 0)
        else:
            per = shard_n // tn
            omap = lambda ik, jn, ip: (jn // per, ik, jn % per)
    else:
        out_shape = (1, k, n)
        omap = lambda ik, jn, ip: (0, ik, jn)
    return pl.pallas_call(
        body, name=name, grid=(k // tk, n // tn, nsteps),
        in_specs=[pl.BlockSpec((tp, tk), lambda ik, jn, ip: (ip, ik)),
                  pl.BlockSpec((tp, tn), lambda ik, jn, ip: (ip, jn))],
        out_specs=pl.BlockSpec((span, tk, width), omap),
        out_shape=jax.ShapeDtypeStruct(out_shape, out_dtype),
        scratch_shapes=[pltpu.VMEM((tk, tn), f32)],
        compiler_params=_cparams(("parallel", "parallel", "arbitrary")),
    )(a, b)


def _adamw_math(w, g, m, v):
    mn = ADAM_B1 * m + (1.0 - ADAM_B1) * g
    vn = ADAM_B2 * v + (1.0 - ADAM_B2) * (g * g)
    m_hat = mn / (1.0 - ADAM_B1 ** ADAM_STEP)
    v_hat = vn / (1.0 - ADAM_B2 ** ADAM_STEP)
    return -ADAM_LR * (m_hat / (jnp.sqrt(v_hat) + ADAM_EPS) + ADAM_WD * w), mn, vn


def adamw(w, g, m, v, name):
    r, c = w.shape
    tr = r
    for cand in (256, 176, 128):
        if r > cand and r % cand == 0:
            tr = cand
            break

    def body(w_ref, g_ref, m_ref, v_ref, go_ref, d_ref, mo_ref, vo_ref):
        gg = g_ref[...]
        go_ref[...] = gg
        d_ref[...], mo_ref[...], vo_ref[...] = _adamw_math(w_ref[...], gg, m_ref[...], v_ref[...])

    spec = pl.BlockSpec((tr, c), lambda i: (i, 0))
    sds = jax.ShapeDtypeStruct((r, c), f32)
    return pl.pallas_call(
        body, name=name, grid=(r // tr,), in_specs=[spec] * 4, out_specs=[spec] * 4, out_shape=[sds] * 4,
        compiler_params=_cparams(("parallel",)),
    )(w, g, m, v)


def adamw_small(ws, gs, ms, vs):
    n = len(ws)

    def body(*refs):
        ins, outs = refs[:4 * n], refs[4 * n:]
        for k in range(n):
            outs[k][...], outs[n + k][...], outs[2 * n + k][...] = _adamw_math(
                ins[k][...], ins[n + k][...], ins[2 * n + k][...], ins[3 * n + k][...])

    out = pl.pallas_call(body, name="adamw_small",
                         out_shape=[jax.ShapeDtypeStruct(w.shape, f32) for w in ws] * 3)(*ws, *gs, *ms, *vs)
    return out[:n], out[n:2 * n], out[2 * n:]


def _me():
    return lax.axis_index("x"), lax.axis_index("y"), lax.axis_index("c")


def _chip_peer(x, y, c, k):
    return (x ^ (k >> 1), y ^ (k & 1), c)


ANY = pl.BlockSpec(memory_space=pl.ANY)


def gather_weights(now, later):
    n, n_later = len(now), len(later)
    out_dtypes = [bf16 if s.size > 16 * 256 else f32 for s in now]
    halves = [(2, s.shape[0] // 2, s.shape[1]) for s in now]

    def body(*refs):
        ins, later_ins = refs[:n], refs[n:n + n_later]
        outs, later_outs = refs[n + n_later:2 * n + n_later], refs[2 * n + n_later:2 * (n + n_later)]
        stage = refs[2 * (n + n_later):3 * n + 2 * n_later]
        send_sems, recv_sems, pass_send_sems, pass_recv_sems, local_sems = refs[3 * n + 2 * n_later:]
        x, y, c = _me()
        j = 2 * x + y
        sibling = (x, y, 1 - c)

        def over_ici(w, k, slot):
            return pltpu.make_async_remote_copy(
                src_ref=stage[w].at[c], dst_ref=outs[w].at[slot, c], send_sem=send_sems.at[w, k - 1],
                recv_sem=recv_sems.at[w, k - 1], device_id=_chip_peer(x, y, c, k), device_id_type=MESH)

        def passed_on(w, k, half):
            return pltpu.make_async_remote_copy(
                src_ref=outs[w].at[j ^ k, half], dst_ref=outs[w].at[j ^ k, half], send_sem=pass_send_sems.at[w, k - 1],
                recv_sem=pass_recv_sems.at[w, k - 1], device_id=sibling, device_id_type=MESH)

        for w in range(n):
            stage[w][...] = ins[w][...].astype(out_dtypes[w]).reshape(halves[w])
        locs = []
        for w in range(n):
            loc = pltpu.make_async_copy(stage[w], outs[w].at[j], local_sems.at[w])
            loc.start()
            locs.append(loc)
            for k in (1, 2, 3):
                over_ici(w, k, j).start()
        for w in range(n_later):
            later_outs[w][...] = later_ins[w][...].astype(bf16)
        for w in range(n):
            for k in (1, 2, 3):
                over_ici(w, k, j ^ k).wait_recv()
                passed_on(w, k, c).start()
        for w in range(n):
            for k in (1, 2, 3):
                passed_on(w, k, 1 - c).wait_recv()
        for w in range(n):
            for k in (1, 2, 3):
                over_ici(w, k, j).wait_send()
                passed_on(w, k, c).wait_send()
        for loc in locs:
            loc.wait()

    vmem = pl.BlockSpec(memory_space=pltpu.VMEM)
    sem3 = pltpu.SemaphoreType.DMA((n, 3))
    return pl.pallas_call(
        body, name="gather_weights",
        in_specs=[vmem] * (n + n_later), out_specs=[ANY] * n + [vmem] * n_later,
        out_shape=[jax.ShapeDtypeStruct((N_SHARD,) + h, dt) for h, dt in zip(halves, out_dtypes)]
        + [jax.ShapeDtypeStruct(s.shape, bf16) for s in later],
        scratch_shapes=[pltpu.VMEM(h, dt) for h, dt in zip(halves, out_dtypes)]
        + [sem3, sem3, sem3, sem3, pltpu.SemaphoreType.DMA((n,))],
        compiler_params=pltpu.CompilerParams(vmem_limit_bytes=VMEM_LIMIT),
    )(*now, *later)


def _shard_push(srcs, dsts, send_sems, recv_sems, local_sems):
    def remote(w, k, slot):
        x, y, c = _me()
        return pltpu.make_async_remote_copy(
            src_ref=srcs[w], dst_ref=dsts[w].at[slot], send_sem=send_sems.at[w, k - 1],
            recv_sem=recv_sems.at[w, k - 1], device_id=_chip_peer(x, y, c, k), device_id_type=MESH)

    def local(w):
        x, y, _ = _me()
        return pltpu.make_async_copy(srcs[w], dsts[w].at[2 * x + y], local_sems.at[w])

    def start():
        x, y, _ = _me()
        for w in range(len(srcs)):
            local(w).start()
            for k in (1, 2, 3):
                remote(w, k, 2 * x + y).start()

    def wait():
        x, y, _ = _me()
        for w in range(len(srcs)):
            for k in (1, 2, 3):
                remote(w, k, (2 * x + y) ^ k).wait_recv()
        for w in range(len(srcs)):
            for k in (1, 2, 3):
                remote(w, k, 2 * x + y).wait_send()
            local(w).wait()

    return start, wait


def _grad_push(srcs, dsts, send_sems, recv_sems):
    def copy(w, k):
        x, y, c = _me()
        px, py, pc = x ^ (k >> 2), y ^ ((k >> 1) & 1), c ^ (k & 1)
        return pltpu.make_async_remote_copy(
            src_ref=srcs[w].at[2 * px + py, pc], dst_ref=dsts[w].at[k - 1], send_sem=send_sems.at[w, k - 1],
            recv_sem=recv_sems.at[w, k - 1], device_id=(px, py, pc), device_id_type=MESH)

    def start():
        for w in range(len(srcs)):
            for k in range(1, N_DEV):
                copy(w, k).start()

    def wait():
        for w in range(len(srcs)):
            for k in range(1, N_DEV):
                copy(w, k).wait_recv()
        for w in range(len(srcs)):
            for k in range(1, N_DEV):
                copy(w, k).wait_send()

    return start, wait


def _grad_push_specs(grads):
    n = len(grads)
    return ([ANY] * n, [ANY] * n, [jax.ShapeDtypeStruct((N_DEV - 1,) + g.shape[2:], g.dtype) for g in grads],
            [pltpu.SemaphoreType.DMA((n, N_DEV - 1)), pltpu.SemaphoreType.DMA((n, N_DEV - 1))])


def add_eight(own, parts, jc_idx, name):
    _, half, c = parts.shape
    tr = half // 2 if (half // 2) % 16 == 0 else half

    def body(jc_ref, own_ref, p_ref, out_ref):
        acc = own_ref[0, 0].astype(f32)
        for k in range(N_DEV - 1):
            acc = acc + p_ref[k].astype(f32)
        out_ref[0] = acc

    return pl.pallas_call(
        body, name=name,
        grid_spec=pltpu.PrefetchScalarGridSpec(
            num_scalar_prefetch=1, grid=(half // tr,),
            in_specs=[pl.BlockSpec((1, 1, tr, c), lambda t, jc: (jc[0], jc[1], t, 0)),
                      pl.BlockSpec((N_DEV - 1, tr, c), lambda t, jc: (0, t, 0))],
            out_specs=pl.BlockSpec((1, tr, c), lambda t, jc: (jc[1], t, 0))),
        out_shape=jax.ShapeDtypeStruct((2, half, c), f32),
        compiler_params=_cparams(("parallel",)),
    )(jc_idx, own, parts)


def _push_specs(shards):
    n = len(shards)
    return ([ANY] * n, [ANY] * n, [jax.ShapeDtypeStruct((N_SHARD,) + s.shape, s.dtype) for s in shards],
            [pltpu.SemaphoreType.DMA((n, 3)), pltpu.SemaphoreType.DMA((n, 3)), pltpu.SemaphoreType.DMA((n,))])


def pair_exchange_halves(grads, small):
    n = len(grads)

    def body(*refs):
        ins, small_ref = refs[:n], refs[n]
        outs, gath = refs[n + 1:2 * n + 1], refs[2 * n + 1]
        send_sems, recv_sems, s_send, s_recv, local_sem = refs[2 * n + 2:]
        x, y, c = _me()
        me = 4 * x + 2 * y + c
        sends = []
        for w in range(n):
            half = ins[w].shape[1] // 2
            cp = pltpu.make_async_remote_copy(
                src_ref=ins[w].at[:, pl.ds((1 - c) * half, half), :], dst_ref=outs[w],
                send_sem=send_sems.at[w], recv_sem=recv_sems.at[w], device_id=(x, y, 1 - c), device_id_type=MESH)
            cp.start()
            sends.append(cp)
        loc = pltpu.make_async_copy(small_ref, gath.at[me], local_sem)
        loc.start()
        for k in range(1, N_DEV):
            cp = pltpu.make_async_remote_copy(
                src_ref=small_ref, dst_ref=gath.at[me], send_sem=s_send.at[k - 1], recv_sem=s_recv.at[k - 1],
                device_id=(x ^ (k >> 2), y ^ ((k >> 1) & 1), c ^ (k & 1)), device_id_type=MESH)
            cp.start()
            sends.append(cp)
        for w in range(n):
            half = ins[w].shape[1] // 2
            pltpu.make_async_remote_copy(
                src_ref=ins[w].at[:, pl.ds(0, half), :], dst_ref=outs[w], send_sem=send_sems.at[w],
                recv_sem=recv_sems.at[w], device_id=(x, y, 1 - c), device_id_type=MESH).wait_recv()
        for k in range(1, N_DEV):
            pltpu.make_async_remote_copy(
                src_ref=small_ref, dst_ref=gath.at[me ^ k], send_sem=s_send.at[k - 1], recv_sem=s_recv.at[k - 1],
                device_id=(x ^ (k >> 2), y ^ ((k >> 1) & 1), c ^ (k & 1)), device_id_type=MESH).wait_recv()
        for cp in sends:
            cp.wait_send()
        loc.wait()

    return pl.pallas_call(
        body, name="pair_exchange_halves", in_specs=[ANY] * (n + 1), out_specs=[ANY] * (n + 1),
        out_shape=[jax.ShapeDtypeStruct((g.shape[0], g.shape[1] // 2, g.shape[2]), f32) for g in grads]
        + [jax.ShapeDtypeStruct((N_DEV,) + small.shape, f32)],
        scratch_shapes=[pltpu.SemaphoreType.DMA((n,)), pltpu.SemaphoreType.DMA((n,)),
                        pltpu.SemaphoreType.DMA((N_DEV - 1,)), pltpu.SemaphoreType.DMA((N_DEV - 1,)),
                        pltpu.SemaphoreType.DMA],
    )(*grads, small)


def chip_exchange(sums):
    n = len(sums)

    def body(*refs):
        ins, outs = refs[:n], refs[n:2 * n]
        send_sems, recv_sems = refs[2 * n:]
        x, y, c = _me()
        j = 2 * x + y
        sends = []
        for w in range(n):
            for k in (1, 2, 3):
                cp = pltpu.make_async_remote_copy(
                    src_ref=ins[w].at[j ^ k], dst_ref=outs[w].at[k - 1], send_sem=send_sems.at[w, k - 1],
                    recv_sem=recv_sems.at[w, k - 1], device_id=_chip_peer(x, y, c, k), device_id_type=MESH)
                cp.start()
                sends.append(cp)
        for w in range(n):
            for k in (1, 2, 3):
                pltpu.make_async_remote_copy(
                    src_ref=ins[w].at[0], dst_ref=outs[w].at[k - 1], send_sem=send_sems.at[w, k - 1],
                    recv_sem=recv_sems.at[w, k - 1], device_id=_chip_peer(x, y, c, k), device_id_type=MESH).wait_recv()
        for cp in sends:
            cp.wait_send()

    return pl.pallas_call(
        body, name="chip_exchange", in_specs=[ANY] * n, out_specs=[ANY] * n,
        out_shape=[jax.ShapeDtypeStruct((N_SHARD - 1,) + s.shape[1:], s.dtype) for s in sums],
        scratch_shapes=[pltpu.SemaphoreType.DMA((n, 3)), pltpu.SemaphoreType.DMA((n, 3))],
    )(*sums)


def pair_exchange_results(halves):
    n = len(halves)

    def body(*refs):
        ins, outs = refs[:n], refs[n:2 * n]
        send_sems, recv_sems = refs[2 * n:]
        x, y, c = _me()
        sends = []
        for w in range(n):
            cp = pltpu.make_async_remote_copy(
                src_ref=ins[w].at[c], dst_ref=outs[w].at[c], send_sem=send_sems.at[w], recv_sem=recv_sems.at[w],
                device_id=(x, y, 1 - c), device_id_type=MESH)
            cp.start()
            sends.append(cp)
        for w in range(n):
            pltpu.make_async_remote_copy(
                src_ref=ins[w].at[c], dst_ref=outs[w].at[1 - c], send_sem=send_sems.at[w],
                recv_sem=recv_sems.at[w], device_id=(x, y, 1 - c), device_id_type=MESH).wait_recv()
        for cp in sends:
            cp.wait_send()

    return pl.pallas_call(
        body, name="pair_exchange_results", in_specs=[ANY] * n, out_specs=[ANY] * n,
        out_shape=[jax.ShapeDtypeStruct(h.shape, f32) for h in halves],
        input_output_aliases={w: w for w in range(n)},
        scratch_shapes=[pltpu.SemaphoreType.DMA((n,)), pltpu.SemaphoreType.DMA((n,))],
    )(*halves)


def add_pair(grad, other, c_idx, name):
    _, r, c = grad.shape
    half = r // 2
    tr = half // 2 if (half // 2) % 8 == 0 else half
    per = half // tr

    def body(c_ref, g_ref, o_ref, out_ref):
        out_ref[...] = (g_ref[...] + o_ref[...]).astype(bf16)

    return pl.pallas_call(
        body, name=name,
        grid_spec=pltpu.PrefetchScalarGridSpec(
            num_scalar_prefetch=1, grid=(N_SHARD, per),
            in_specs=[pl.BlockSpec((1, tr, c), lambda j, t, cr: (j, cr[0] * per + t, 0)),
                      pl.BlockSpec((1, tr, c), lambda j, t, cr: (j, t, 0))],
            out_specs=pl.BlockSpec((1, tr, c), lambda j, t, cr: (j, t, 0))),
        out_shape=jax.ShapeDtypeStruct((N_SHARD, half, c), bf16),
        compiler_params=_cparams(("parallel", "parallel")),
    )(c_idx, grad, other)


def add_four(own, parts, jc_idx, name):
    _, half, c = parts.shape
    tr = half // 2 if (half // 2) % 8 == 0 else half

    def body(jc_ref, own_ref, p_ref, out_ref):
        acc = own_ref[0].astype(f32)
        for k in range(N_SHARD - 1):
            acc = acc + p_ref[k].astype(f32)
        out_ref[0] = acc

    return pl.pallas_call(
        body, name=name,
        grid_spec=pltpu.PrefetchScalarGridSpec(
            num_scalar_prefetch=1, grid=(half // tr,),
            in_specs=[pl.BlockSpec((1, tr, c), lambda t, jc: (jc[0], t, 0)),
                      pl.BlockSpec((N_SHARD - 1, tr, c), lambda t, jc: (0, t, 0))],
            out_specs=pl.BlockSpec((1, tr, c), lambda t, jc: (jc[1], t, 0))),
        out_shape=jax.ShapeDtypeStruct((2, half, c), f32),
        compiler_params=_cparams(("parallel",)),
    )(jc_idx, own, parts)


def sum_devices(gathered):
    def body(g_ref, out_ref):
        acc = g_ref[0]
        for d in range(1, N_DEV):
            acc = acc + g_ref[d]
        out_ref[...] = acc

    return pl.pallas_call(body, name="sum_devices", out_shape=jax.ShapeDtypeStruct(gathered.shape[1:], f32))(gathered)


def _rows128(a, rows):
    flat = a.reshape(-1, BLOCK) if a.size % BLOCK == 0 else jnp.pad(a.reshape(1, -1), ((0, 0), (0, BLOCK - a.size)))
    return jnp.pad(flat, ((0, rows - flat.shape[0]), (0, 0)))


def kernel(x, meta_tokens, ln_emb_g, ln_emb_b, w_in, hg_lower_bounds, hg_norm_g, attn_sinks, w_branch_hg, w_branch_attn, w_out, ln1_g, ln1_b, w_ffn_in, w_ffn_out, ln2_g, ln2_b, loss_target, m_meta_tokens, m_ln_emb_g, m_ln_emb_b, m_w_in, m_hg_lower_bounds, m_hg_norm_g, m_attn_sinks, m_w_branch_hg, m_w_branch_attn, m_w_out, m_ln1_g, m_ln1_b, m_w_ffn_in, m_w_ffn_out, m_ln2_g, m_ln2_b, v_meta_tokens, v_ln_emb_g, v_ln_emb_b, v_w_in, v_hg_lower_bounds, v_hg_norm_g, v_attn_sinks, v_w_branch_hg, v_w_branch_attn, v_w_out, v_ln1_g, v_ln1_b, v_w_ffn_in, v_w_ffn_out, v_ln2_g, v_ln2_b):
    seq = x.shape[1]
    nb = seq // BLOCK + 1
    xs = x[0]
    ts = loss_target[0]
    ix, iy, ic = _me()
    shard = 2 * ix + iy
    vec = lambda a: a.reshape(1, D_MODEL)

    g_in, g_meta, s_bh, s_ba, s_out, s_fi, s_fo = gather_weights(
        [w_in[0], meta_tokens], [w_branch_hg[0], w_branch_attn[0], w_out[0], w_ffn_in[0], w_ffn_out[0]])
    by_cols = lambda g: g.reshape(N_SHARD, -1, g.shape[-1]).transpose(1, 0, 2).reshape(-1, N_SHARD * g.shape[-1])
    wf_in = by_cols(g_in)
    metablk = jnp.pad(by_cols(g_meta), ((TM - N_META, 0), (0, 0)))

    pos = jnp.arange(nb * BLOCK, dtype=jnp.int32) - PAD
    half = HEAD_DIM // 2
    inv = ROPE_THETA ** (-jnp.arange(half, dtype=f32) / half)
    ang = pos.astype(f32)[:, None] * inv[None, :]
    cos = jnp.tile(jnp.cos(ang), (1, BLOCK // half))
    sin = jnp.tile(jnp.sin(ang), (1, BLOCK // half))
    sinks8 = jnp.broadcast_to(attn_sinks.reshape(ATT_HEADS, 1), (ATT_HEADS, BLOCK))
    ng = hg_norm_g.reshape(1, HG_K)

    h0, h0b, pa, pg = emb_inproj(xs, metablk, vec(ln_emb_g), vec(ln_emb_b), wf_in)
    og, sprev, g_fi = hgrn_fwd(pa, hg_lower_bounds, ng, nb, [s_fi])
    oatt, g_fo, g_out, g_bh, g_ba = attn_fwd(pa, cos, sin, sinks8, nb, [s_fo, s_out, s_bh, s_ba])
    wf_bh, wf_ba, wf_fi = by_cols(g_bh), by_cols(g_ba), by_cols(g_fi)
    wf_out = g_out.reshape(D_MODEL, D_MODEL)
    wf_fo = g_fo.reshape(D_FF, D_MODEL)
    h1, h1b, mixin, og, oatt = mid_front(h0, pg, og, oatt, wf_bh, wf_ba, wf_out, ln1_g, ln1_b)
    dh1, dau, sact, dr2, loss_part, dg2, db2 = mid_ffn(h1, ts, wf_fi, wf_fo, ln2_g, ln2_b)
    dh0p, dpg, dog, doa, dyh, dya, dr1, dg1, db1 = mid_back(dh1, h0, pg, og, oatt, wf_bh, wf_ba, wf_out, ln1_g, ln1_b)
    steps = h0.shape[0] // TM
    tp = TM * max(k for k in (3, 2, 1) if steps % k == 0)
    pieces = lambda g: g.reshape(N_SHARD, 2, -1, g.shape[-1])
    gb_bh = pieces(wgrad(og, dyh, "wgrad_bh", 512, D_MODEL, tp, True, bf16))
    gb_ba = pieces(wgrad(oatt, dya, "wgrad_ba", 512, D_MODEL, tp, True, bf16))
    gb_out = pieces(wgrad(mixin, dr1, "wgrad_out", D_MODEL, D_MODEL, tp, False, bf16))
    gb_fi = pieces(wgrad(h1b, dau, "wgrad_fi", D_MODEL, D_FF, tp, True, bf16))
    gb_fo = pieces(wgrad(sact, dr2, "wgrad_fo", D_FF // 2, D_MODEL, tp, False, bf16))
    dhq, dhf, dhi, dhg, dlb4, dng, r_fi, r_fo = hgrn_bwd(pa, hg_lower_bounds, ng, sprev, dog, nb, [gb_fi, gb_fo])
    daq, dkc, dkp, dvc, dvp, dkm, dvm, dsk, r_out, r_bh, r_ba = attn_bwd(pa, cos, sin, sinks8, doa, nb,
                                                                         [gb_out, gb_bh, gb_ba])
    dproj, dx, dmeta, dlg, dlb = inproj_bwd(dh0p, dhq, dhf, dhi, dhg, daq, dkc, dkp, dvc, dvp, dkm, dvm, dpg,
                                      wf_in, xs, metablk, vec(ln_emb_g), vec(ln_emb_b))
    gw_in = wgrad(h0b, dproj, "wgrad_in", D_MODEL, IN_W // 2, tp, True)

    parts = [(dlg, 8), (dlb, 8), (dlb4, 8), (dng, 8), (dsk[:, 0], 8),
             (dg1, 8), (db1, 8), (dg2, 8), (db2, 8), (dmeta, BLOCK), (loss_part, 8)]
    small = jnp.concatenate([_rows128(a, r) for a, r in parts], axis=0)

    c_idx = jnp.reshape(ic, (1,)).astype(jnp.int32)
    jc_idx = jnp.stack([shard, ic]).astype(jnp.int32)
    other_in, gathered = pair_exchange_halves([gw_in], small)
    sum_in = add_pair(gw_in, other_in, c_idx, "add_pair_in")
    quad_in, = chip_exchange([sum_in])
    halves = [add_four(sum_in, quad_in, jc_idx, "add_four_in")]
    halves += [add_eight(g, r, jc_idx, "add_eight_" + nm) for nm, g, r in
               (("bh", gb_bh, r_bh), ("ba", gb_ba, r_ba), ("out", gb_out, r_out), ("fi", gb_fi, r_fi),
                ("fo", gb_fo, r_fo))]
    red = [r.reshape(-1, r.shape[-1]) for r in pair_exchange_results(halves)]
    small_sum = sum_devices(gathered)

    offs, acc = [], 0
    for _, r in parts:
        offs.append(acc)
        acc += r
    take = lambda n, size: small_sum[offs[n]:offs[n] + parts[n][1]].reshape(-1)[:size]
    g_meta_full = take(9, N_META * D_MODEL).reshape(N_META, D_MODEL)
    g_small = {
        "meta_tokens": lax.dynamic_slice_in_dim(g_meta_full, shard * (D_MODEL // N_SHARD), D_MODEL // N_SHARD, axis=1),
        "ln_emb_g": take(0, D_MODEL), "ln_emb_b": take(1, D_MODEL),
        "hg_lower_bounds": take(2, 2 * HG_HEADS * HG_K).reshape(2, HG_HEADS * HG_K),
        "hg_norm_g": take(3, HG_K).reshape(1, HG_K), "attn_sinks": take(4, ATT_HEADS).reshape(1, ATT_HEADS),
        "ln1_g": take(5, D_MODEL).reshape(1, D_MODEL), "ln1_b": take(6, D_MODEL).reshape(1, D_MODEL),
        "ln2_g": take(7, D_MODEL).reshape(1, D_MODEL), "ln2_b": take(8, D_MODEL).reshape(1, D_MODEL),
    }
    g_big = {"w_in": red[0], "w_branch_hg": red[1], "w_branch_attn": red[2], "w_out": red[3],
             "w_ffn_in": red[4], "w_ffn_out": red[5]}

    names = ["meta_tokens", "ln_emb_g", "ln_emb_b", "w_in", "hg_lower_bounds", "hg_norm_g", "attn_sinks",
             "w_branch_hg", "w_branch_attn", "w_out", "ln1_g", "ln1_b", "w_ffn_in", "w_ffn_out", "ln2_g", "ln2_b"]
    given = dict(
        meta_tokens=(meta_tokens, m_meta_tokens, v_meta_tokens), ln_emb_g=(ln_emb_g, m_ln_emb_g, v_ln_emb_g),
        ln_emb_b=(ln_emb_b, m_ln_emb_b, v_ln_emb_b), w_in=(w_in, m_w_in, v_w_in),
        hg_lower_bounds=(hg_lower_bounds, m_hg_lower_bounds, v_hg_lower_bounds),
        hg_norm_g=(hg_norm_g, m_hg_norm_g, v_hg_norm_g), attn_sinks=(attn_sinks, m_attn_sinks, v_attn_sinks),
        w_branch_hg=(w_branch_hg, m_w_branch_hg, v_w_branch_hg),
        w_branch_attn=(w_branch_attn, m_w_branch_attn, v_w_branch_attn), w_out=(w_out, m_w_out, v_w_out),
        ln1_g=(ln1_g, m_ln1_g, v_ln1_g), ln1_b=(ln1_b, m_ln1_b, v_ln1_b), w_ffn_in=(w_ffn_in, m_w_ffn_in, v_w_ffn_in),
        w_ffn_out=(w_ffn_out, m_w_ffn_out, v_w_ffn_out), ln2_g=(ln2_g, m_ln2_g, v_ln2_g), ln2_b=(ln2_b, m_ln2_b, v_ln2_b))
    two_d = lambda a: a.reshape(8, BLOCK) if a.ndim == 1 else a.reshape(a.shape[-2], a.shape[-1])
    small_names = [nm for nm in names if nm not in g_big]
    small_d, small_m, small_v = adamw_small([two_d(given[nm][0]) for nm in small_names],
                                            [two_d(g_small[nm]) for nm in small_names],
                                            [two_d(given[nm][1]) for nm in small_names],
                                            [two_d(given[nm][2]) for nm in small_names])
    out_g, out_d, out_m, out_v = [], [], [], []
    for nm in names:
        w, m, v = given[nm]
        shape = w.shape
        if nm in g_big:
            g, d, mn, vn = adamw(two_d(w), g_big[nm], two_d(m), two_d(v), "adamw_" + nm)
        else:
            k = small_names.index(nm)
            g, d, mn, vn = g_small[nm], small_d[k], small_m[k], small_v[k]
        out_g.append(g.reshape(shape))
        out_d.append(d.reshape(shape))
        out_m.append(mn.reshape(shape))
        out_v.append(vn.reshape(shape))

    loss = take(10, 1)[0]
    grad_x = dx.reshape(x.shape)
    return (loss, grad_x, *out_g, *out_d, *out_m, *out_v)
```

```python
import functools

import jax
import jax.numpy as jnp
from jax import lax
from jax.experimental import pallas as pl
from jax.experimental.pallas import tpu as pltpu

f32 = jnp.float32
bf16 = jnp.bfloat16

D_MODEL = 1024
BLOCK = 128
N_META = 16
PAD = BLOCK - N_META
HG_HEADS = 4
HG_K = 128
SUB = 16
ATT_HEADS = 8
HEAD_DIM = 64
ATT_QW = ATT_HEADS * HEAD_DIM
D_FF = 2816
EPS = 1e-5
ALPHA = 2.0 ** 0.25
ROPE_THETA = 10000.0
N_A = 2816
N_G = 2048
IN_W = N_A + N_G
N_SHARD = 4
N_DEV = 8

ADAM_LR = 0.001
ADAM_B1 = 0.9
ADAM_B2 = 0.999
ADAM_EPS = 1e-08
ADAM_WD = 0.01
ADAM_STEP = 10

TM = 256
LEAD = TM // BLOCK - 1

VMEM_LIMIT = 56 * 1024 * 1024
MESH = pl.DeviceIdType.MESH


def _cparams(sem, vmem=VMEM_LIMIT):
    return pltpu.CompilerParams(dimension_semantics=sem, vmem_limit_bytes=vmem)


def _const_spec(shape):
    zeros = (0,) * len(shape)
    return pl.BlockSpec(shape, lambda *_: zeros, pipeline_mode=pl.Buffered(1))


def _dot(a, b, ca, cb):
    return lax.dot_general(a.astype(bf16), b.astype(bf16), (((ca,), (cb,)), ((), ())),
                           preferred_element_type=f32)


@jax.custom_vjp
def mm(a, b):
    return _dot(a, b, 1, 0)


mm.defvjp(lambda a, b: (_dot(a, b, 1, 0), (a, b)),
          lambda r, g: (_dot(g, r[1], 1, 1), _dot(r[0], g, 0, 0)))


@jax.custom_vjp
def mm_nt(a, b):
    return _dot(a, b, 1, 1)


mm_nt.defvjp(lambda a, b: (_dot(a, b, 1, 1), (a, b)),
             lambda r, g: (_dot(g, r[1], 1, 0), _dot(g, r[0], 0, 0)))


@jax.custom_vjp
def mm_tn(a, b):
    return _dot(a, b, 0, 0)


mm_tn.defvjp(lambda a, b: (_dot(a, b, 0, 0), (a, b)),
             lambda r, g: (_dot(r[1], g, 1, 1), _dot(r[0], g, 1, 0)))


@functools.partial(jax.custom_vjp, nondiff_argnums=(1,))
def roll_lanes(x, shift):
    return pltpu.roll(x, shift, 1)


roll_lanes.defvjp(lambda x, shift: (pltpu.roll(x, shift, 1), None),
                  lambda shift, _, g: (pltpu.roll(g, (128 - shift) % 128, 1),))


@jax.custom_vjp
def _sigmoid(x):
    return 1.0 / (1.0 + jnp.exp(-x))


def _sigmoid_fwd(x):
    s = 1.0 / (1.0 + jnp.exp(-x))
    return s, s


_sigmoid.defvjp(_sigmoid_fwd, lambda s, g: (g * s * (1.0 - s),))


@jax.custom_vjp
def _recip(x):
    return 1.0 / x


def _recip_fwd(x):
    r = 1.0 / x
    return r, r


_recip.defvjp(_recip_fwd, lambda r, g: (-g * r * r,))


def _ln_stats(x):
    mu = jnp.mean(x, axis=-1, keepdims=True)
    xc = x - mu
    var = jnp.mean(xc * xc, axis=-1, keepdims=True)
    rs = lax.rsqrt(var + EPS)
    return xc * rs, rs


def _ln_bwd(dy, xh, rs, g):
    dxh = dy * g
    m1 = jnp.mean(dxh, axis=-1, keepdims=True)
    m2 = jnp.mean(dxh * xh, axis=-1, keepdims=True)
    return rs * (dxh - m1 - xh * m2)


def _row_ids(i):
    return i * BLOCK + lax.broadcasted_iota(jnp.int32, (BLOCK, 1), 0)


def _tm_rows(i):
    return i * TM + lax.broadcasted_iota(jnp.int32, (TM, 1), 0)


def _tm_row(n):
    return pl.BlockSpec((TM, n), lambda i: (i, 0))


def _tm_tokens():
    return pl.BlockSpec((TM, D_MODEL), lambda i: (jnp.maximum(i - 1, 0), 0))


def emb_inproj(x, metablk, g, b, w_in):
    nsteps = x.shape[0] // TM + 1

    def body(x_ref, mb_ref, g_ref, b_ref, w_ref, h0_ref, h0b_ref, pa_ref, pg_ref):
        i = pl.program_id(0)
        xb = jnp.where(i == 0, mb_ref[...], x_ref[...])
        xh, _ = _ln_stats(xb)
        y = xh * g_ref[...] + b_ref[...]
        y = jnp.where(_tm_rows(i) >= TM - N_META, y, 0.0)
        h0_ref[...] = y
        yb = y.astype(bf16)
        h0b_ref[...] = yb
        pa_ref[...] = _dot(yb, w_ref[:N_A, :], 1, 1)
        pg_ref[...] = _dot(yb, w_ref[N_A:, :], 1, 1)

    p = nsteps * TM
    row = _tm_row
    return pl.pallas_call(
        body, name="emb_inproj", grid=(nsteps,),
        in_specs=[_tm_tokens(),
                  _const_spec((TM, D_MODEL)), _const_spec((1, D_MODEL)), _const_spec((1, D_MODEL)),
                  _const_spec((IN_W, D_MODEL))],
        out_specs=[row(D_MODEL), row(D_MODEL), row(N_A), row(N_G)],
        out_shape=[jax.ShapeDtypeStruct((p, D_MODEL), f32), jax.ShapeDtypeStruct((p, D_MODEL), bf16),
                   jax.ShapeDtypeStruct((p, N_A), f32), jax.ShapeDtypeStruct((p, N_G), f32)],
        compiler_params=_cparams(("parallel",)),
    )(x, metablk, g, b, w_in)


def _hgrn_chunk(valid, st, hq, hf, hi, hg, lbraw, ng):
    lb = _sigmoid(lbraw[0:1] - lbraw[1:2])
    q = hq * _sigmoid(hq)
    fg = lb + (1.0 - lb) * _sigmoid(hf)
    logf = jnp.where(valid, jnp.log(fg), 0.0)
    k = jnp.where(valid, 1.0 - fg, 0.0)
    v = hi
    r = lax.broadcasted_iota(jnp.int32, (BLOCK, BLOCK), 0)
    c = lax.broadcasted_iota(jnp.int32, (BLOCK, BLOCK), 1)
    tril = (c <= r).astype(f32)
    bcum = jnp.dot(tril, logf, precision=lax.Precision.HIGHEST, preferred_element_type=f32)
    blast = bcum[BLOCK - 1:BLOCK]
    rows = lax.broadcasted_iota(jnp.int32, (BLOCK, 1), 0)
    sub8 = lax.broadcasted_iota(jnp.int32, (BLOCK // 8, 8, HG_K), 1)
    b8 = bcum.reshape(BLOCK // 8, 8, HG_K)
    row_of_8 = lambda j: jnp.broadcast_to(b8[:, j:j + 1, :], b8.shape)
    a = jnp.where(r == c, jnp.sum(q * k, axis=-1, keepdims=True), 0.0)
    seg = BLOCK
    while seg >= 2:
        half = seg // 2
        if seg >= 8:
            bs = bcum.reshape(BLOCK // seg, seg, HG_K)
            ref = jnp.broadcast_to(bs[:, half - 1:half, :], bs.shape)
        elif seg == 4:
            ref = jnp.where(sub8 < 4, row_of_8(1), row_of_8(5))
        else:
            ref = jnp.where(sub8 < 2, row_of_8(0), jnp.where(sub8 < 4, row_of_8(2),
                                                             jnp.where(sub8 < 6, row_of_8(4), row_of_8(6))))
        ref = ref.reshape(BLOCK, HG_K)
        upper = (rows % seg) >= half
        q_up = q * jnp.exp(jnp.where(upper, bcum - ref, -jnp.inf))
        k_lo = k * jnp.exp(jnp.where(upper, -jnp.inf, ref - bcum))
        a = a + jnp.where((r // seg) == (c // seg), mm_nt(q_up, k_lo), 0.0)
        seg = half
    o = mm_nt(q * jnp.exp(bcum), st) + mm(a, v)
    st_new = st * jnp.exp(blast) + mm_tn(v, k * jnp.exp(blast - bcum))
    on = o * lax.rsqrt(jnp.mean(o * o, axis=-1, keepdims=True) + EPS) * ng
    return st_new, on * (hg * _sigmoid(hg))


def _hgrn_in_specs(rowmap):
    wide = lambda col: pl.BlockSpec((BLOCK, HG_HEADS * HG_K), lambda i: (rowmap(i) + LEAD, col))
    return [wide(0), wide(1), wide(2), wide(3), _const_spec((2, HG_HEADS * HG_K)), _const_spec((1, HG_K))]


def _head(ref, h):
    return ref[:, h * HG_K:(h + 1) * HG_K]


def hgrn_fwd(pa, lbraw, ng, nb, shards):
    n = len(shards)

    def body(hq_ref, hf_ref, hi_ref, hg_ref, lb_ref, ng_ref, *rest):
        srcs, (og_ref, sp_ref), dsts = rest[:n], rest[n:n + 2], rest[n + 2:2 * n + 2]
        st_ref = rest[2 * n + 2]
        start, wait = _shard_push(srcs, dsts, *rest[2 * n + 3:])
        i = pl.program_id(0)

        @pl.when(i == 0)
        def _():
            st_ref[...] = jnp.zeros_like(st_ref)
            start()

        @pl.when(i == nb - 1)
        def _():
            wait()

        valid = _row_ids(i) >= PAD
        for h in range(HG_HEADS):
            st = st_ref[h]
            sp_ref[0, h] = st
            st_new, out = _hgrn_chunk(valid, st, _head(hq_ref, h), _head(hf_ref, h), _head(hi_ref, h),
                                      _head(hg_ref, h), _head(lb_ref, h), ng_ref[...])
            st_ref[h] = st_new
            og_ref[:, h * HG_K:(h + 1) * HG_K] = out.astype(bf16)

    p = (nb + LEAD) * BLOCK
    push_in, push_out, push_shape, push_scratch = _push_specs(shards)
    return pl.pallas_call(
        body, name="hgrn_fwd", grid=(nb,),
        in_specs=_hgrn_in_specs(lambda i: i) + push_in,
        out_specs=[pl.BlockSpec((BLOCK, HG_HEADS * HG_K), lambda i: (i + LEAD, 0)),
                   pl.BlockSpec((1, HG_HEADS, HG_K, HG_K), lambda i: (i, 0, 0, 0))] + push_out,
        out_shape=[jax.ShapeDtypeStruct((p, HG_HEADS * HG_K), bf16),
                   jax.ShapeDtypeStruct((nb, HG_HEADS, HG_K, HG_K), f32)] + push_shape,
        scratch_shapes=[pltpu.VMEM((HG_HEADS, HG_K, HG_K), f32)] + push_scratch,
        compiler_params=_cparams(("arbitrary",)),
    )(pa, pa, pa, pa, lbraw, ng, *shards)


def hgrn_bwd(pa, lbraw, ng, sprev, dog, nb, grads):
    n = len(grads)

    def body(hq_ref, hf_ref, hi_ref, hg_ref, lb_ref, ng_ref, sp_ref, do_ref, *rest):
        srcs, rest = rest[:n], rest[n:]
        dq_ref, df_ref, di_ref, dg_ref, dlb_ref, dng_ref = rest[:6]
        dsts, dst_ref = rest[6:6 + n], rest[6 + n]
        start, wait = _grad_push(srcs, dsts, *rest[7 + n:])
        i = pl.program_id(0)

        @pl.when(i == 0)
        def _():
            dst_ref[...] = jnp.zeros_like(dst_ref)
            dlb_ref[...] = jnp.zeros_like(dlb_ref)
            dng_ref[...] = jnp.zeros_like(dng_ref)
            start()

        valid = _row_ids(nb - 1 - i) >= PAD
        dng_sum = jnp.zeros((1, HG_K), f32)
        for h in range(HG_HEADS):
            cols = slice(h * HG_K, (h + 1) * HG_K)
            _, vjp = jax.vjp(functools.partial(_hgrn_chunk, valid), sp_ref[0, h], _head(hq_ref, h), _head(hf_ref, h),
                             _head(hi_ref, h), _head(hg_ref, h), _head(lb_ref, h), ng_ref[...])
            dst, dq, df, di, dg, dlb, dng = vjp((dst_ref[h], _head(do_ref, h)))
            dst_ref[h] = dst
            dq_ref[:, cols] = dq.astype(bf16)
            df_ref[:, cols] = df.astype(bf16)
            di_ref[:, cols] = di.astype(bf16)
            dg_ref[:, cols] = dg.astype(bf16)
            dlb_ref[:, cols] += dlb
            dng_sum = dng_sum + dng
        dng_ref[...] += dng_sum
        pl.when(i == nb - 1)(wait)

    p = (nb + LEAD) * BLOCK
    rev = lambda i: nb - 1 - i
    hw = HG_HEADS * HG_K
    blk = pl.BlockSpec((BLOCK, hw), lambda i: (rev(i) + LEAD, 0))
    wide = jax.ShapeDtypeStruct((p, hw), bf16)
    push_in, push_out, push_shape, push_scratch = _grad_push_specs(grads)
    return pl.pallas_call(
        body, name="hgrn_bwd", grid=(nb,),
        in_specs=_hgrn_in_specs(rev) + [pl.BlockSpec((1, HG_HEADS, HG_K, HG_K), lambda i: (rev(i), 0, 0, 0)), blk]
        + push_in,
        out_specs=[blk, blk, blk, blk, pl.BlockSpec((2, hw), lambda i: (0, 0)), pl.BlockSpec((1, HG_K), lambda i: (0, 0))]
        + push_out,
        out_shape=[wide, wide, wide, wide, jax.ShapeDtypeStruct((2, hw), f32), jax.ShapeDtypeStruct((1, HG_K), f32)]
        + push_shape,
        scratch_shapes=[pltpu.VMEM((HG_HEADS, HG_K, HG_K), f32)] + push_scratch,
        compiler_params=_cparams(("arbitrary",)),
    )(pa, pa, pa, pa, lbraw, ng, sprev, dog, *grads)


def _rope(x, cos, sin):
    lane = lax.broadcasted_iota(jnp.int32, x.shape, 1)
    rot = jnp.where(lane % HEAD_DIM < HEAD_DIM // 2, -roll_lanes(x, BLOCK - HEAD_DIM // 2),
                    roll_lanes(x, HEAD_DIM // 2))
    return x * cos + rot * sin


def _both_halves(x, g):
    lo = lax.broadcasted_iota(jnp.int32, x.shape, 1) < HEAD_DIM
    sw = roll_lanes(x, HEAD_DIM)
    return jnp.where(lo, x, sw) if g == 0 else jnp.where(lo, sw, x)


def _attn_block(band_ok, meta_ok, tabs, q, kp, kc, vp, vc, km, vm, *sinks):
    cq, sq, cp, sp, cm, sm = tabs
    neg = jnp.finfo(f32).min
    scale = HEAD_DIM ** -0.5
    group = ATT_HEADS // 2
    kpr, kcr, kmr = _rope(kp, cp, sp), _rope(kc, cq, sq), _rope(km, cm, sm)
    lo = lax.broadcasted_iota(jnp.int32, (BLOCK, BLOCK), 1) < HEAD_DIM
    t = lax.broadcasted_iota(jnp.int32, (group * BLOCK, BLOCK), 0) % BLOCK
    col = lax.broadcasted_iota(jnp.int32, (group * BLOCK, BLOCK), 1)
    own = col <= t
    is_sink = col == N_META
    qr =[_rope(q[:, m * BLOCK:(m + 1) * BLOCK], cq, sq) for m in range(ATT_HEADS // 2)]
    slabs = []
    for g in range(2):
        kp_g, kc_g, vp_g, vc_g, km_g, vm_g = [_both_halves(a, g) for a in (kpr, kcr, vp, vc, kmr, vm)]
        qs = jnp.concatenate([jnp.where(lo if h % 2 == 0 else ~lo, qr[2 * g + h // 2], 0.0) for h in range(group)],
                             axis=0)
        sink = jnp.concatenate([jnp.broadcast_to(sinks[group * g + h], (BLOCK, 1)) for h in range(group)], axis=0)
        sb = jnp.where(band_ok, jnp.where(own, mm_nt(qs, kc_g), mm_nt(qs, kp_g)) * scale, neg)
        no_keys = jnp.zeros((BLOCK - N_META, BLOCK), f32)
        sme = jnp.where(meta_ok, mm_nt(qs, jnp.concatenate([km_g, no_keys], axis=0)) * scale,
                        jnp.where(is_sink, sink, neg))
        mx = lax.stop_gradient(jnp.max(jnp.maximum(sb, sme), axis=-1, keepdims=True))
        eb, em = jnp.exp(sb - mx), jnp.exp(sme - mx)
        inv = _recip(jnp.sum(eb + em, axis=-1, keepdims=True))
        pb = eb * inv
        o = (mm(jnp.where(own, pb, 0.0), vc_g) + mm(jnp.where(own, 0.0, pb), vp_g)
             + mm(em * inv, jnp.concatenate([vm_g, no_keys], axis=0)))
        for m in range(2):
            slabs.append(jnp.where(lo, o[2 * m * BLOCK:(2 * m + 1) * BLOCK], o[(2 * m + 1) * BLOCK:(2 * m + 2) * BLOCK]))
    return jnp.concatenate(slabs, axis=1)


def _attn_masks(i):
    group = ATT_HEADS // 2
    t = lax.broadcasted_iota(jnp.int32, (group * BLOCK, BLOCK), 0) % BLOCK
    s = lax.broadcasted_iota(jnp.int32, (group * BLOCK, BLOCK), 1)
    kpos = jnp.where(s <= t, i * BLOCK - PAD + s, jnp.where(i > 0, (i - 1) * BLOCK - PAD + s, -1))
    band_ok = kpos >= N_META
    qpos = i * BLOCK - PAD + lax.broadcasted_iota(jnp.int32, (group * BLOCK, 1), 0) % BLOCK
    meta_ok = (s < N_META) & (s <= qpos)
    return band_ok, meta_ok


def _attn_in_specs(cur=lambda i: i):
    prev = lambda i: jnp.maximum(cur(i) - 1, 0)
    kcol, vcol = N_A // BLOCK - 2, N_A // BLOCK - 1
    blk = lambda rowmap, col: pl.BlockSpec((BLOCK, BLOCK), lambda i: (rowmap(i) + LEAD, col))
    tab = lambda rowmap: pl.BlockSpec((BLOCK, BLOCK), lambda i: (rowmap(i), 0))
    first = lambda i: 0
    return [pl.BlockSpec((BLOCK, ATT_QW), lambda i: (cur(i) + LEAD, 4)),
            blk(prev, kcol), blk(cur, kcol), blk(prev, vcol), blk(cur, vcol), blk(first, kcol), blk(first, vcol),
            tab(cur), tab(cur), tab(prev), tab(prev), tab(first), tab(first),
            _const_spec((ATT_HEADS, BLOCK))]


def _attn_row(n, cur=lambda i: i):
    return pl.BlockSpec((BLOCK, n), lambda i: (cur(i) + LEAD, 0))


def _attn_operands(q_ref, kp_ref, kc_ref, vp_ref, vc_ref, km_ref, vm_ref, cq, sq, cp, sp, cm, sm, sk_ref):
    tabs = (cq[...], sq[...], cp[...], sp[...], cm[PAD:, :], sm[PAD:, :])
    args = (q_ref[...], kp_ref[...], kc_ref[...], vp_ref[...], vc_ref[...], km_ref[PAD:, :], vm_ref[PAD:, :])
    sinks = tuple(sk_ref[j:j + 1, 0:1] for j in range(ATT_HEADS))
    return tabs, args + sinks


def attn_fwd(pa, cos, sin, sinks8, nb, shards):
    n = len(shards)
    n_in = 14

    def body(*refs):
        srcs, o_ref, dsts = refs[n_in:n_in + n], refs[n_in + n], refs[n_in + n + 1:n_in + 2 * n + 1]
        start, wait = _shard_push(srcs, dsts, *refs[n_in + 2 * n + 1:])
        i = pl.program_id(0)
        pl.when(i == 0)(start)
        band_ok, meta_ok = _attn_masks(i)
        tabs, args = _attn_operands(*refs[:n_in])
        o_ref[...] = _attn_block(band_ok, meta_ok, tabs, *args).astype(bf16)
        pl.when(i == nb - 1)(wait)

    push_in, push_out, push_shape, push_scratch = _push_specs(shards)
    return pl.pallas_call(
        body, name="attn_fwd", grid=(nb,), in_specs=_attn_in_specs() + push_in,
        out_specs=[_attn_row(ATT_QW)] + push_out,
        out_shape=[jax.ShapeDtypeStruct(((nb + LEAD) * BLOCK, ATT_QW), bf16)] + push_shape,
        scratch_shapes=push_scratch,
        compiler_params=_cparams(("arbitrary",)),
    )(pa, pa, pa, pa, pa, pa, pa, cos, sin, cos, sin, cos, sin, sinks8, *shards)


def attn_bwd(pa, cos, sin, sinks8, do, nb, grads):
    n = len(grads)

    def body(*refs):
        do_ref, srcs = refs[14], refs[15:15 + n]
        dq_ref, dkc_ref, dkp_ref, dvc_ref, dvp_ref, dkm_ref, dvm_ref, dsk_ref = refs[15 + n:23 + n]
        start, wait = _grad_push(srcs, refs[23 + n:23 + 2 * n], *refs[23 + 2 * n:])
        i = pl.program_id(0)

        @pl.when(i == 0)
        def _():
            dkm_ref[...] = jnp.zeros((N_META, BLOCK), f32)
            dvm_ref[...] = jnp.zeros((N_META, BLOCK), f32)
            dsk_ref[...] = jnp.zeros((ATT_HEADS, BLOCK), f32)
            start()

        band_ok, meta_ok = _attn_masks(i)
        tabs, args = _attn_operands(*refs[:14])
        _, vjp = jax.vjp(functools.partial(_attn_block, band_ok, meta_ok, tabs), *args)
        grads = vjp(do_ref[...])
        dq_ref[...] = grads[0].astype(bf16)
        dkp_ref[...] = grads[1]
        dkc_ref[...] = grads[2]
        dvp_ref[...] = grads[3]
        dvc_ref[...] = grads[4]
        dkm_ref[...] += grads[5]
        dvm_ref[...] += grads[6]
        for j in range(ATT_HEADS):
            dsk_ref[j:j + 1, :] += jnp.broadcast_to(grads[7 + j], (1, BLOCK))
        pl.when(i == nb - 1)(wait)

    p = (nb + LEAD) * BLOCK
    row = _attn_row(BLOCK)
    const = lambda r: pl.BlockSpec((r, BLOCK), lambda i: (0, 0))
    part = jax.ShapeDtypeStruct((p, BLOCK), f32)
    push_in, push_out, push_shape, push_scratch = _grad_push_specs(grads)
    return pl.pallas_call(
        body, name="attn_bwd", grid=(nb,),
        in_specs=_attn_in_specs() + [_attn_row(ATT_QW)] + push_in,
        out_specs=[_attn_row(ATT_QW), row, row, row, row,
                   const(N_META), const(N_META), const(ATT_HEADS)] + push_out,
        out_shape=[jax.ShapeDtypeStruct((p, ATT_QW), bf16), part, part, part, part,
                   jax.ShapeDtypeStruct((N_META, BLOCK), f32), jax.ShapeDtypeStruct((N_META, BLOCK), f32),
                   jax.ShapeDtypeStruct((ATT_HEADS, BLOCK), f32)] + push_shape,
        scratch_shapes=push_scratch,
        compiler_params=_cparams(("arbitrary",)),
    )(pa, pa, pa, pa, pa, pa, pa, cos, sin, cos, sin, cos, sin, sinks8, do, *grads)


def _mid_forward(h0_ref, pg_ref, og, oa, wbh_ref, wba_ref, wo_ref, g1, b1):
    yh = jnp.dot(og, wbh_ref[...], preferred_element_type=f32)
    ya = jnp.dot(oa, wba_ref[...], preferred_element_type=f32)
    gh = _sigmoid(pg_ref[:, :D_MODEL])
    ga = _sigmoid(pg_ref[:, D_MODEL:])
    mixin = (gh * yh + ga * ya).astype(bf16)
    r1 = ALPHA * h0_ref[...] + jnp.dot(mixin, wo_ref[...], preferred_element_type=f32)
    xh1, rs1 = _ln_stats(r1)
    return yh, ya, gh, ga, mixin, xh1, rs1, xh1 * g1 + b1


def _mid_weight_specs():
    hw = HG_HEADS * HG_K
    return [_const_spec((hw, D_MODEL)), _const_spec((ATT_QW, D_MODEL)), _const_spec((D_MODEL, D_MODEL)),
            _const_spec((1, D_MODEL)), _const_spec((1, D_MODEL))]


def mid_front(h0, pg, og, oatt, wbh, wba, wout, ln1g, ln1b):
    def body(h0_ref, pg_ref, og_ref, oa_ref, wbh_ref, wba_ref, wo_ref, g1_ref, b1_ref,
             h1_ref, h1b_ref, mix_ref, ogc_ref, oac_ref):
        used = _tm_rows(pl.program_id(0)) >= LEAD * BLOCK
        og = jnp.where(used, og_ref[...], jnp.zeros_like(og_ref))
        oa = jnp.where(used, oa_ref[...], jnp.zeros_like(oa_ref))
        ogc_ref[...] = og
        oac_ref[...] = oa
        *_, mixin, _, _, h1 = _mid_forward(h0_ref, pg_ref, og, oa, wbh_ref, wba_ref, wo_ref, g1_ref[...], b1_ref[...])
        mix_ref[...] = mixin
        h1_ref[...] = h1
        h1b_ref[...] = h1.astype(bf16)

    p = h0.shape[0]
    hw = HG_HEADS * HG_K
    sds = lambda n, dt: jax.ShapeDtypeStruct((p, n), dt)
    return pl.pallas_call(
        body, name="mid_front", grid=(p // TM,),
        in_specs=[_tm_row(D_MODEL), _tm_row(N_G), _tm_row(hw), _tm_row(ATT_QW)] + _mid_weight_specs(),
        out_specs=[_tm_row(D_MODEL), _tm_row(D_MODEL), _tm_row(D_MODEL), _tm_row(hw), _tm_row(ATT_QW)],
        out_shape=[sds(D_MODEL, f32), sds(D_MODEL, bf16), sds(D_MODEL, bf16), sds(hw, bf16), sds(ATT_QW, bf16)],
        compiler_params=_cparams(("parallel",)),
    )(h0, pg, og, oatt, wbh, wba, wout, ln1g, ln1b)


def mid_ffn(h1, target, wfi, wfo, ln2g, ln2b):
    def body(h1_ref, t_ref, wfi_ref, wfo_ref, g2_ref, b2_ref,
             dh1_ref, dau_ref, s_ref, dr2_ref, loss_ref, dg2_ref, db2_ref):
        i = pl.program_id(0)

        @pl.when(i == 0)
        def _():
            for r in (loss_ref, dg2_ref, db2_ref):
                r[...] = jnp.zeros_like(r)

        g2, b2 = g2_ref[...], b2_ref[...]
        h1 = h1_ref[...]
        au = jnp.dot(h1.astype(bf16), wfi_ref[...], preferred_element_type=f32)
        a, u = au[:, :D_FF], au[:, D_FF:]
        sg = _sigmoid(a)
        sa = a * sg
        s = (sa * u).astype(bf16)
        s_ref[...] = s
        r2 = ALPHA * h1 + jnp.dot(s, wfo_ref[...], preferred_element_type=f32)
        xh2, rs2 = _ln_stats(r2)
        diff = jnp.where(i > 0, xh2 * g2 + b2 - t_ref[...], 0.0)
        loss_ref[...] += jnp.sum(diff * diff) * (0.5 / D_MODEL)
        dy = diff * (1.0 / D_MODEL)
        dg2_ref[...] += jnp.sum(dy * xh2, axis=0, keepdims=True)
        db2_ref[...] += jnp.sum(dy, axis=0, keepdims=True)
        dr2 = _ln_bwd(dy, xh2, rs2, g2)
        dr2b = dr2.astype(bf16)
        dr2_ref[...] = dr2b
        ds = _dot(dr2b, wfo_ref[...], 1, 1)
        da = (ds * u) * (sg * (1.0 + a * (1.0 - sg)))
        du = ds * sa
        dau = jnp.concatenate([da, du], axis=1).astype(bf16)
        dau_ref[...] = dau
        dh1_ref[...] = ALPHA * dr2 + _dot(dau, wfi_ref[...], 1, 1)

    p = h1.shape[0]
    vec = lambda: pl.BlockSpec((1, D_MODEL), lambda i: (0, 0))
    sds = lambda n, dt: jax.ShapeDtypeStruct((p, n), dt)
    return pl.pallas_call(
        body, name="mid_ffn", grid=(p // TM,),
        in_specs=[_tm_row(D_MODEL), _tm_tokens(), _const_spec((D_MODEL, 2 * D_FF)), _const_spec((D_FF, D_MODEL)),
                  _const_spec((1, D_MODEL)), _const_spec((1, D_MODEL))],
        out_specs=[_tm_row(D_MODEL), _tm_row(2 * D_FF), _tm_row(D_FF), _tm_row(D_MODEL),
                   pl.BlockSpec((1, 1), lambda i: (0, 0)), vec(), vec()],
        out_shape=[sds(D_MODEL, f32), sds(2 * D_FF, bf16), sds(D_FF, bf16), sds(D_MODEL, bf16),
                   jax.ShapeDtypeStruct((1, 1), f32)] + [jax.ShapeDtypeStruct((1, D_MODEL), f32)] * 2,
        compiler_params=_cparams(("arbitrary",)),
    )(h1, target, wfi, wfo, ln2g, ln2b)


def mid_back(dh1, h0, pg, ogc, oac, wbh, wba, wout, ln1g, ln1b):
    def body(dh1_ref, h0_ref, pg_ref, og_ref, oa_ref, wbh_ref, wba_ref, wo_ref, g1_ref, b1_ref,
             dh0_ref, dpg_ref, dog_ref, doa_ref, dyh_ref, dya_ref, dr1_ref, dg1_ref, db1_ref):
        @pl.when(pl.program_id(0) == 0)
        def _():
            dg1_ref[...] = jnp.zeros_like(dg1_ref)
            db1_ref[...] = jnp.zeros_like(db1_ref)

        g1 = g1_ref[...]
        yh, ya, gh, ga, _, xh1, rs1, _ = _mid_forward(h0_ref, pg_ref, og_ref[...], oa_ref[...], wbh_ref, wba_ref,
                                                      wo_ref, g1, b1_ref[...])
        dh1 = dh1_ref[...]
        dg1_ref[...] += jnp.sum(dh1 * xh1, axis=0, keepdims=True)
        db1_ref[...] += jnp.sum(dh1, axis=0, keepdims=True)
        dr1 = _ln_bwd(dh1, xh1, rs1, g1)
        dr1b = dr1.astype(bf16)
        dr1_ref[...] = dr1b
        dh0_ref[...] = ALPHA * dr1
        dmix = _dot(dr1b, wo_ref[...], 1, 1)
        dyh = (dmix * gh).astype(bf16)
        dya = (dmix * ga).astype(bf16)
        dyh_ref[...] = dyh
        dya_ref[...] = dya
        dpg_ref[:, :D_MODEL] = (dmix * yh * gh * (1.0 - gh)).astype(bf16)
        dpg_ref[:, D_MODEL:] = (dmix * ya * ga * (1.0 - ga)).astype(bf16)
        dog_ref[...] = _dot(dyh, wbh_ref[...], 1, 1)
        doa_ref[...] = _dot(dya, wba_ref[...], 1, 1)

    p = h0.shape[0]
    hw = HG_HEADS * HG_K
    vec = lambda: pl.BlockSpec((1, D_MODEL), lambda i: (0, 0))
    sds = lambda n, dt: jax.ShapeDtypeStruct((p, n), dt)
    return pl.pallas_call(
        body, name="mid_back", grid=(p // TM,),
        in_specs=[_tm_row(D_MODEL), _tm_row(D_MODEL), _tm_row(N_G), _tm_row(hw), _tm_row(ATT_QW)] + _mid_weight_specs(),
        out_specs=[_tm_row(D_MODEL), _tm_row(N_G), _tm_row(hw), _tm_row(ATT_QW), _tm_row(D_MODEL), _tm_row(D_MODEL),
                   _tm_row(D_MODEL), vec(), vec()],
        out_shape=[sds(D_MODEL, f32), sds(N_G, bf16), sds(hw, f32), sds(ATT_QW, f32), sds(D_MODEL, bf16),
                   sds(D_MODEL, bf16), sds(D_MODEL, bf16)] + [jax.ShapeDtypeStruct((1, D_MODEL), f32)] * 2,
        compiler_params=_cparams(("arbitrary",)),
    )(dh1, h0, pg, ogc, oac, wbh, wba, wout, ln1g, ln1b)


def inproj_bwd(dh0p, dhq, dhf, dhi, dhg, daq, dkc, dkp, dvc, dvp, dkm, dvm, dpg, w_in, x, metablk, g, b):
    p = dh0p.shape[0]
    nbk = p // BLOCK
    per = TM // BLOCK

    def body(dh0_ref, dq_ref, df_ref, di_ref, dg_ref, daq_ref, dkc_ref, *rest):
        dkp_refs, dvc_ref, dvp_refs = rest[:per], rest[per], rest[per + 1:2 * per + 1]
        (dkm_ref, dvm_ref, dpg_ref, w_ref, x_ref, mb_ref, g_ref, b_ref,
         dproj_ref, dx_ref, dmeta_ref, dlg_ref, dlb_ref) = rest[2 * per + 1:]
        i = pl.program_id(0)

        @pl.when(i == 0)
        def _():
            dlg_ref[...] = jnp.zeros_like(dlg_ref)
            dlb_ref[...] = jnp.zeros_like(dlb_ref)

        zero_pad = jnp.zeros((TM - N_META, BLOCK), f32)
        first = i == 0
        rows = _tm_rows(i)

        def keys(cur_ref, next_refs, meta_ref):
            nxt = jnp.concatenate([jnp.where(per * i + 1 + m < nbk, next_refs[m][...], 0.0) for m in range(per)], axis=0)
            t = cur_ref[...] + nxt
            return t + jnp.where(first, jnp.concatenate([zero_pad, meta_ref[...]], axis=0), 0.0)

        dproj = jnp.concatenate(
            [dq_ref[...], df_ref[...], di_ref[...], dg_ref[...], daq_ref[...],
             keys(dkc_ref, dkp_refs, dkm_ref).astype(bf16), keys(dvc_ref, dvp_refs, dvm_ref).astype(bf16),
             dpg_ref[...]], axis=1)
        dproj = jnp.where(rows >= LEAD * BLOCK, dproj, jnp.zeros_like(dproj))
        dproj_ref[...] = dproj
        valid = rows >= TM - N_META
        dh0 = jnp.where(valid, dh0_ref[...] + _dot(dproj, w_ref[...], 1, 0), 0.0)
        xb = jnp.where(first, mb_ref[...], x_ref[...])
        xh, rs = _ln_stats(xb)
        dlg_ref[...] += jnp.sum(dh0 * xh, axis=0, keepdims=True)
        dlb_ref[...] += jnp.sum(dh0, axis=0, keepdims=True)
        dx = jnp.where(valid, _ln_bwd(dh0, xh, rs, g_ref[...]), 0.0)
        dx_ref[...] = dx

        @pl.when(first)
        def _():
            dmeta_ref[...] = dx[TM - N_META:, :]

    row = _tm_row
    nxt = [pl.BlockSpec((BLOCK, BLOCK), functools.partial(lambda i, m: (jnp.minimum(per * i + 1 + m, nbk - 1), 0), m=m))
           for m in range(per)]
    hw = HG_HEADS * HG_K
    vec = lambda: pl.BlockSpec((1, D_MODEL), lambda i: (0, 0))
    return pl.pallas_call(
        body, name="inproj_bwd", grid=(p // TM,),
        in_specs=[row(D_MODEL), row(hw), row(hw), row(hw), row(hw), row(ATT_QW),
                  row(BLOCK)] + nxt + [row(BLOCK)] + nxt + [_const_spec((N_META, BLOCK)), _const_spec((N_META, BLOCK)),
                  row(N_G), _const_spec((IN_W, D_MODEL)), _tm_tokens(),
                  _const_spec((TM, D_MODEL)), _const_spec((1, D_MODEL)), _const_spec((1, D_MODEL))],
        out_specs=[row(IN_W), _tm_tokens(), pl.BlockSpec((N_META, D_MODEL), lambda i: (0, 0)), vec(), vec()],
        out_shape=[jax.ShapeDtypeStruct((p, IN_W), bf16), jax.ShapeDtypeStruct((p - TM, D_MODEL), f32),
                   jax.ShapeDtypeStruct((N_META, D_MODEL), f32),
                   jax.ShapeDtypeStruct((1, D_MODEL), f32), jax.ShapeDtypeStruct((1, D_MODEL), f32)],
        compiler_params=_cparams(("arbitrary",)),
    )(dh0p, dhq, dhf, dhi, dhg, daq, dkc, *([dkp] * per), dvc, *([dvp] * per), dkm, dvm, dpg, w_in, x, metablk, g, b)


def wgrad(a, b, name, tk, tn, tp, by_cols, out_dtype=f32):
    p, k = a.shape
    n = b.shape[1]
    nsteps = p // tp

    def body(a_ref, b_ref, o_ref, acc_ref):
        ip = pl.program_id(2)

        @pl.when(ip == 0)
        def _():
            acc_ref[...] = jnp.zeros_like(acc_ref)

        acc_ref[...] += _dot(a_ref[...], b_ref[...], 0, 0)

        @pl.when(ip == nsteps - 1)
        def _():
            for j in range(span):
                o_ref[j] = acc_ref[:, j * width:(j + 1) * width].astype(out_dtype)

    span, width = 1, tn
    if by_cols:
        shard_n = n // N_SHARD
        out_shape = (N_SHARD, k, shard_n)
        if tn >= shard_n:
            span, width = tn // shard_n, shard_n
            omap = lambda ik, jn, ip: (jn, ik, 0)
        else:
            per = shard_n // tn
            omap = lambda ik, jn, ip: (jn // per, ik, jn % per)
    else:
        out_shape = (1, k, n)
        omap = lambda ik, jn, ip: (0, ik, jn)
    return pl.pallas_call(
        body, name=name, grid=(k // tk, n // tn, nsteps),
        in_specs=[pl.BlockSpec((tp, tk), lambda ik, jn, ip: (ip, ik)),
                  pl.BlockSpec((tp, tn), lambda ik, jn, ip: (ip, jn))],
        out_specs=pl.BlockSpec((span, tk, width), omap),
        out_shape=jax.ShapeDtypeStruct(out_shape, out_dtype),
        scratch_shapes=[pltpu.VMEM((tk, tn), f32)],
        compiler_params=_cparams(("parallel", "parallel", "arbitrary")),
    )(a, b)


def _adamw_math(w, g, m, v):
    mn = ADAM_B1 * m + (1.0 - ADAM_B1) * g
    vn = ADAM_B2 * v + (1.0 - ADAM_B2) * (g * g)
    m_hat = mn / (1.0 - ADAM_B1 ** ADAM_STEP)
    v_hat = vn / (1.0 - ADAM_B2 ** ADAM_STEP)
    return -ADAM_LR * (m_hat / (jnp.sqrt(v_hat) + ADAM_EPS) + ADAM_WD * w), mn, vn


def adamw(w, g, m, v, name):
    r, c = w.shape
    tr = r
    for cand in (256, 176, 152, 128):
        if r > cand and r % cand == 0:
            tr = cand
            break

    def body(w_ref, g_ref, m_ref, v_ref, go_ref, d_ref, mo_ref, vo_ref):
        gg = g_ref[...]
        go_ref[...] = gg
        d_ref[...], mo_ref[...], vo_ref[...] = _adamw_math(w_ref[...], gg, m_ref[...], v_ref[...])

    spec = pl.BlockSpec((tr, c), lambda i: (i, 0))
    sds = jax.ShapeDtypeStruct((r, c), f32)
    return pl.pallas_call(
        body, name=name, grid=(r // tr,), in_specs=[spec] * 4, out_specs=[spec] * 4, out_shape=[sds] * 4,
        compiler_params=_cparams(("parallel",)),
    )(w, g, m, v)


def adamw_small(ws, gs, ms, vs):
    n = len(ws)

    def body(*refs):
        ins, outs = refs[:4 * n], refs[4 * n:]
        for k in range(n):
            outs[k][...], outs[n + k][...], outs[2 * n + k][...] = _adamw_math(
                ins[k][...], ins[n + k][...], ins[2 * n + k][...], ins[3 * n + k][...])

    out = pl.pallas_call(body, name="adamw_small",
                         out_shape=[jax.ShapeDtypeStruct(w.shape, f32) for w in ws] * 3)(*ws, *gs, *ms, *vs)
    return out[:n], out[n:2 * n], out[2 * n:]


def _me():
    return lax.axis_index("x"), lax.axis_index("y"), lax.axis_index("c")


def _chip_peer(x, y, c, k):
    return (x ^ (k >> 1), y ^ (k & 1), c)


ANY = pl.BlockSpec(memory_space=pl.ANY)


def gather_weights(now, later):
    n, n_later = len(now), len(later)
    out_dtypes = [bf16 if s.size > 16 * 256 else f32 for s in now]
    halves = [(2, s.shape[0] // 2, s.shape[1]) for s in now]

    def body(*refs):
        ins, later_ins = refs[:n], refs[n:n + n_later]
        outs, later_outs = refs[n + n_later:2 * n + n_later], refs[2 * n + n_later:2 * (n + n_later)]
        stage = refs[2 * (n + n_later):3 * n + 2 * n_later]
        send_sems, recv_sems, pass_send_sems, pass_recv_sems, local_sems = refs[3 * n + 2 * n_later:]
        x, y, c = _me()
        j = 2 * x + y
        sibling = (x, y, 1 - c)

        def over_ici(w, k, slot):
            return pltpu.make_async_remote_copy(
                src_ref=stage[w].at[c], dst_ref=outs[w].at[slot, c], send_sem=send_sems.at[w, k - 1],
                recv_sem=recv_sems.at[w, k - 1], device_id=_chip_peer(x, y, c, k), device_id_type=MESH)

        def passed_on(w, k, half):
            return pltpu.make_async_remote_copy(
                src_ref=outs[w].at[j ^ k, half], dst_ref=outs[w].at[j ^ k, half], send_sem=pass_send_sems.at[w, k - 1],
                recv_sem=pass_recv_sems.at[w, k - 1], device_id=sibling, device_id_type=MESH)

        for w in range(n):
            stage[w][...] = ins[w][...].astype(out_dtypes[w]).reshape(halves[w])
        locs = []
        for w in range(n):
            loc = pltpu.make_async_copy(stage[w], outs[w].at[j], local_sems.at[w])
            loc.start()
            locs.append(loc)
            for k in (1, 2, 3):
                over_ici(w, k, j).start()
        for w in range(n_later):
            later_outs[w][...] = later_ins[w][...].astype(bf16)
        for w in range(n):
            for k in (1, 2, 3):
                over_ici(w, k, j ^ k).wait_recv()
                passed_on(w, k, c).start()
        for w in range(n):
            for k in (1, 2, 3):
                passed_on(w, k, 1 - c).wait_recv()
        for w in range(n):
            for k in (1, 2, 3):
                over_ici(w, k, j).wait_send()
                passed_on(w, k, c).wait_send()
        for loc in locs:
            loc.wait()

    vmem = pl.BlockSpec(memory_space=pltpu.VMEM)
    sem3 = pltpu.SemaphoreType.DMA((n, 3))
    return pl.pallas_call(
        body, name="gather_weights",
        in_specs=[vmem] * (n + n_later), out_specs=[ANY] * n + [vmem] * n_later,
        out_shape=[jax.ShapeDtypeStruct((N_SHARD,) + h, dt) for h, dt in zip(halves, out_dtypes)]
        + [jax.ShapeDtypeStruct(s.shape, bf16) for s in later],
        scratch_shapes=[pltpu.VMEM(h, dt) for h, dt in zip(halves, out_dtypes)]
        + [sem3, sem3, sem3, sem3, pltpu.SemaphoreType.DMA((n,))],
        compiler_params=pltpu.CompilerParams(vmem_limit_bytes=VMEM_LIMIT),
    )(*now, *later)


def _shard_push(srcs, dsts, send_sems, recv_sems, local_sems):
    def remote(w, k, slot):
        x, y, c = _me()
        return pltpu.make_async_remote_copy(
            src_ref=srcs[w], dst_ref=dsts[w].at[slot], send_sem=send_sems.at[w, k - 1],
            recv_sem=recv_sems.at[w, k - 1], device_id=_chip_peer(x, y, c, k), device_id_type=MESH)

    def local(w):
        x, y, _ = _me()
        return pltpu.make_async_copy(srcs[w], dsts[w].at[2 * x + y], local_sems.at[w])

    def start():
        x, y, _ = _me()
        for w in range(len(srcs)):
            local(w).start()
            for k in (1, 2, 3):
                remote(w, k, 2 * x + y).start()

    def wait():
        x, y, _ = _me()
        for w in range(len(srcs)):
            for k in (1, 2, 3):
                remote(w, k, (2 * x + y) ^ k).wait_recv()
        for w in range(len(srcs)):
            for k in (1, 2, 3):
                remote(w, k, 2 * x + y).wait_send()
            local(w).wait()

    return start, wait


def _grad_push(srcs, dsts, send_sems, recv_sems):
    def copy(w, k):
        x, y, c = _me()
        px, py, pc = x ^ (k >> 2), y ^ ((k >> 1) & 1), c ^ (k & 1)
        return pltpu.make_async_remote_copy(
            src_ref=srcs[w].at[2 * px + py, pc], dst_ref=dsts[w].at[k - 1], send_sem=send_sems.at[w, k - 1],
            recv_sem=recv_sems.at[w, k - 1], device_id=(px, py, pc), device_id_type=MESH)

    def start():
        for w in range(len(srcs)):
            for k in range(1, N_DEV):
                copy(w, k).start()

    def wait():
        for w in range(len(srcs)):
            for k in range(1, N_DEV):
                copy(w, k).wait_recv()
        for w in range(len(srcs)):
            for k in range(1, N_DEV):
                copy(w, k).wait_send()

    return start, wait


def _grad_push_specs(grads):
    n = len(grads)
    return ([ANY] * n, [ANY] * n, [jax.ShapeDtypeStruct((N_DEV - 1,) + g.shape[2:], g.dtype) for g in grads],
            [pltpu.SemaphoreType.DMA((n, N_DEV - 1)), pltpu.SemaphoreType.DMA((n, N_DEV - 1))])


def add_eight(own, parts, jc_idx, name):
    _, half, c = parts.shape
    tr = half // 2 if (half // 2) % 16 == 0 else half

    def body(jc_ref, own_ref, p_ref, out_ref):
        acc = own_ref[0, 0].astype(f32)
        for k in range(N_DEV - 1):
            acc = acc + p_ref[k].astype(f32)
        out_ref[0] = acc

    return pl.pallas_call(
        body, name=name,
        grid_spec=pltpu.PrefetchScalarGridSpec(
            num_scalar_prefetch=1, grid=(half // tr,),
            in_specs=[pl.BlockSpec((1, 1, tr, c), lambda t, jc: (jc[0], jc[1], t, 0)),
                      pl.BlockSpec((N_DEV - 1, tr, c), lambda t, jc: (0, t, 0))],
            out_specs=pl.BlockSpec((1, tr, c), lambda t, jc: (jc[1], t, 0))),
        out_shape=jax.ShapeDtypeStruct((2, half, c), f32),
        compiler_params=_cparams(("parallel",)),
    )(jc_idx, own, parts)


def _push_specs(shards):
    n = len(shards)
    return ([ANY] * n, [ANY] * n, [jax.ShapeDtypeStruct((N_SHARD,) + s.shape, s.dtype) for s in shards],
            [pltpu.SemaphoreType.DMA((n, 3)), pltpu.SemaphoreType.DMA((n, 3)), pltpu.SemaphoreType.DMA((n,))])


def pair_exchange_halves(grads, small):
    n = len(grads)

    def body(*refs):
        ins, small_ref = refs[:n], refs[n]
        outs, gath = refs[n + 1:2 * n + 1], refs[2 * n + 1]
        send_sems, recv_sems, s_send, s_recv, local_sem = refs[2 * n + 2:]
        x, y, c = _me()
        me = 4 * x + 2 * y + c
        sends = []
        for w in range(n):
            half = ins[w].shape[1] // 2
            cp = pltpu.make_async_remote_copy(
                src_ref=ins[w].at[:, pl.ds((1 - c) * half, half), :], dst_ref=outs[w],
                send_sem=send_sems.at[w], recv_sem=recv_sems.at[w], device_id=(x, y, 1 - c), device_id_type=MESH)
            cp.start()
            sends.append(cp)
        loc = pltpu.make_async_copy(small_ref, gath.at[me], local_sem)
        loc.start()
        for k in range(1, N_DEV):
            cp = pltpu.make_async_remote_copy(
                src_ref=small_ref, dst_ref=gath.at[me], send_sem=s_send.at[k - 1], recv_sem=s_recv.at[k - 1],
                device_id=(x ^ (k >> 2), y ^ ((k >> 1) & 1), c ^ (k & 1)), device_id_type=MESH)
            cp.start()
            sends.append(cp)
        for w in range(n):
            half = ins[w].shape[1] // 2
            pltpu.make_async_remote_copy(
                src_ref=ins[w].at[:, pl.ds(0, half), :], dst_ref=outs[w], send_sem=send_sems.at[w],
                recv_sem=recv_sems.at[w], device_id=(x, y, 1 - c), device_id_type=MESH).wait_recv()
        for k in range(1, N_DEV):
            pltpu.make_async_remote_copy(
                src_ref=small_ref, dst_ref=gath.at[me ^ k], send_sem=s_send.at[k - 1], recv_sem=s_recv.at[k - 1],
                device_id=(x ^ (k >> 2), y ^ ((k >> 1) & 1), c ^ (k & 1)), device_id_type=MESH).wait_recv()
        for cp in sends:
            cp.wait_send()
        loc.wait()

    return pl.pallas_call(
        body, name="pair_exchange_halves", in_specs=[ANY] * (n + 1), out_specs=[ANY] * (n + 1),
        out_shape=[jax.ShapeDtypeStruct((g.shape[0], g.shape[1] // 2, g.shape[2]), f32) for g in grads]
        + [jax.ShapeDtypeStruct((N_DEV,) + small.shape, f32)],
        scratch_shapes=[pltpu.SemaphoreType.DMA((n,)), pltpu.SemaphoreType.DMA((n,)),
                        pltpu.SemaphoreType.DMA((N_DEV - 1,)), pltpu.SemaphoreType.DMA((N_DEV - 1,)),
                        pltpu.SemaphoreType.DMA],
    )(*grads, small)


def chip_exchange(sums):
    n = len(sums)

    def body(*refs):
        ins, outs = refs[:n], refs[n:2 * n]
        send_sems, recv_sems = refs[2 * n:]
        x, y, c = _me()
        j = 2 * x + y
        sends = []
        for w in range(n):
            for k in (1, 2, 3):
                cp = pltpu.make_async_remote_copy(
                    src_ref=ins[w].at[j ^ k], dst_ref=outs[w].at[k - 1], send_sem=send_sems.at[w, k - 1],
                    recv_sem=recv_sems.at[w, k - 1], device_id=_chip_peer(x, y, c, k), device_id_type=MESH)
                cp.start()
                sends.append(cp)
        for w in range(n):
            for k in (1, 2, 3):
                pltpu.make_async_remote_copy(
                    src_ref=ins[w].at[0], dst_ref=outs[w].at[k - 1], send_sem=send_sems.at[w, k - 1],
                    recv_sem=recv_sems.at[w, k - 1], device_id=_chip_peer(x, y, c, k), device_id_type=MESH).wait_recv()
        for cp in sends:
            cp.wait_send()

    return pl.pallas_call(
        body, name="chip_exchange", in_specs=[ANY] * n, out_specs=[ANY] * n,
        out_shape=[jax.ShapeDtypeStruct((N_SHARD - 1,) + s.shape[1:], s.dtype) for s in sums],
        scratch_shapes=[pltpu.SemaphoreType.DMA((n, 3)), pltpu.SemaphoreType.DMA((n, 3))],
    )(*sums)


def pair_exchange_results(halves):
    n = len(halves)

    def body(*refs):
        ins, outs = refs[:n], refs[n:2 * n]
        send_sems, recv_sems = refs[2 * n:]
        x, y, c = _me()
        sends = []
        for w in range(n):
            cp = pltpu.make_async_remote_copy(
                src_ref=ins[w].at[c], dst_ref=outs[w].at[c], send_sem=send_sems.at[w], recv_sem=recv_sems.at[w],
                device_id=(x, y, 1 - c), device_id_type=MESH)
            cp.start()
            sends.append(cp)
        for w in range(n):
            pltpu.make_async_remote_copy(
                src_ref=ins[w].at[c], dst_ref=outs[w].at[1 - c], send_sem=send_sems.at[w],
                recv_sem=recv_sems.at[w], device_id=(x, y, 1 - c), device_id_type=MESH).wait_recv()
        for cp in sends:
            cp.wait_send()

    return pl.pallas_call(
        body, name="pair_exchange_results", in_specs=[ANY] * n, out_specs=[ANY] * n,
        out_shape=[jax.ShapeDtypeStruct(h.shape, f32) for h in halves],
        input_output_aliases={w: w for w in range(n)},
        scratch_shapes=[pltpu.SemaphoreType.DMA((n,)), pltpu.SemaphoreType.DMA((n,))],
    )(*halves)


def add_pair(grad, other, c_idx, name):
    _, r, c = grad.shape
    half = r // 2
    tr = half // 2 if (half // 2) % 8 == 0 else half
    per = half // tr

    def body(c_ref, g_ref, o_ref, out_ref):
        out_ref[...] = (g_ref[...] + o_ref[...]).astype(bf16)

    return pl.pallas_call(
        body, name=name,
        grid_spec=pltpu.PrefetchScalarGridSpec(
            num_scalar_prefetch=1, grid=(N_SHARD, per),
            in_specs=[pl.BlockSpec((1, tr, c), lambda j, t, cr: (j, cr[0] * per + t, 0)),
                      pl.BlockSpec((1, tr, c), lambda j, t, cr: (j, t, 0))],
            out_specs=pl.BlockSpec((1, tr, c), lambda j, t, cr: (j, t, 0))),
        out_shape=jax.ShapeDtypeStruct((N_SHARD, half, c), bf16),
        compiler_params=_cparams(("parallel", "parallel")),
    )(c_idx, grad, other)


def add_four(own, parts, jc_idx, name):
    _, half, c = parts.shape
    tr = half // 2 if (half // 2) % 8 == 0 else half

    def body(jc_ref, own_ref, p_ref, out_ref):
        acc = own_ref[0].astype(f32)
        for k in range(N_SHARD - 1):
            acc = acc + p_ref[k].astype(f32)
        out_ref[0] = acc

    return pl.pallas_call(
        body, name=name,
        grid_spec=pltpu.PrefetchScalarGridSpec(
            num_scalar_prefetch=1, grid=(half // tr,),
            in_specs=[pl.BlockSpec((1, tr, c), lambda t, jc: (jc[0], t, 0)),
                      pl.BlockSpec((N_SHARD - 1, tr, c), lambda t, jc: (0, t, 0))],
            out_specs=pl.BlockSpec((1, tr, c), lambda t, jc: (jc[1], t, 0))),
        out_shape=jax.ShapeDtypeStruct((2, half, c), f32),
        compiler_params=_cparams(("parallel",)),
    )(jc_idx, own, parts)


def sum_devices(gathered):
    def body(g_ref, out_ref):
        acc = g_ref[0]
        for d in range(1, N_DEV):
            acc = acc + g_ref[d]
        out_ref[...] = acc

    return pl.pallas_call(body, name="sum_devices", out_shape=jax.ShapeDtypeStruct(gathered.shape[1:], f32))(gathered)


def _rows128(a, rows):
    flat = a.reshape(-1, BLOCK) if a.size % BLOCK == 0 else jnp.pad(a.reshape(1, -1), ((0, 0), (0, BLOCK - a.size)))
    return jnp.pad(flat, ((0, rows - flat.shape[0]), (0, 0)))


def kernel(x, meta_tokens, ln_emb_g, ln_emb_b, w_in, hg_lower_bounds, hg_norm_g, attn_sinks, w_branch_hg, w_branch_attn, w_out, ln1_g, ln1_b, w_ffn_in, w_ffn_out, ln2_g, ln2_b, loss_target, m_meta_tokens, m_ln_emb_g, m_ln_emb_b, m_w_in, m_hg_lower_bounds, m_hg_norm_g, m_attn_sinks, m_w_branch_hg, m_w_branch_attn, m_w_out, m_ln1_g, m_ln1_b, m_w_ffn_in, m_w_ffn_out, m_ln2_g, m_ln2_b, v_meta_tokens, v_ln_emb_g, v_ln_emb_b, v_w_in, v_hg_lower_bounds, v_hg_norm_g, v_attn_sinks, v_w_branch_hg, v_w_branch_attn, v_w_out, v_ln1_g, v_ln1_b, v_w_ffn_in, v_w_ffn_out, v_ln2_g, v_ln2_b):
    seq = x.shape[1]
    nb = seq // BLOCK + 1
    xs = x[0]
    ts = loss_target[0]
    ix, iy, ic = _me()
    shard = 2 * ix + iy
    vec = lambda a: a.reshape(1, D_MODEL)

    w_in_t = jnp.swapaxes(w_in[0], 0, 1)
    g_in, g_meta, s_bh, s_ba, s_out, s_fi, s_fo = gather_weights(
        [w_in_t, meta_tokens], [w_branch_hg[0], w_branch_attn[0], w_out[0], w_ffn_in[0], w_ffn_out[0]])
    by_cols = lambda g: g.reshape(N_SHARD, -1, g.shape[-1]).transpose(1, 0, 2).reshape(-1, N_SHARD * g.shape[-1])
    wf_in = g_in.reshape(IN_W, D_MODEL)
    metablk = jnp.pad(by_cols(g_meta), ((TM - N_META, 0), (0, 0)))

    pos = jnp.arange(nb * BLOCK, dtype=jnp.int32) - PAD
    half = HEAD_DIM // 2
    inv = ROPE_THETA ** (-jnp.arange(half, dtype=f32) / half)
    ang = pos.astype(f32)[:, None] * inv[None, :]
    cos = jnp.tile(jnp.cos(ang), (1, BLOCK // half))
    sin = jnp.tile(jnp.sin(ang), (1, BLOCK // half))
    sinks8 = jnp.broadcast_to(attn_sinks.reshape(ATT_HEADS, 1), (ATT_HEADS, BLOCK))
    ng = hg_norm_g.reshape(1, HG_K)

    h0, h0b, pa, pg = emb_inproj(xs, metablk, vec(ln_emb_g), vec(ln_emb_b), wf_in)
    og, sprev, g_fi = hgrn_fwd(pa, hg_lower_bounds, ng, nb, [s_fi])
    oatt, g_fo, g_out, g_bh, g_ba = attn_fwd(pa, cos, sin, sinks8, nb, [s_fo, s_out, s_bh, s_ba])
    wf_bh, wf_ba, wf_fi = by_cols(g_bh), by_cols(g_ba), by_cols(g_fi)
    wf_out = g_out.reshape(D_MODEL, D_MODEL)
    wf_fo = g_fo.reshape(D_FF, D_MODEL)
    h1, h1b, mixin, og, oatt = mid_front(h0, pg, og, oatt, wf_bh, wf_ba, wf_out, ln1_g, ln1_b)
    dh1, dau, sact, dr2, loss_part, dg2, db2 = mid_ffn(h1, ts, wf_fi, wf_fo, ln2_g, ln2_b)
    dh0p, dpg, dog, doa, dyh, dya, dr1, dg1, db1 = mid_back(dh1, h0, pg, og, oatt, wf_bh, wf_ba, wf_out, ln1_g, ln1_b)
    steps = h0.shape[0] // TM
    tp = TM * max(k for k in (3, 2, 1) if steps % k == 0)
    pieces = lambda g: g.reshape(N_SHARD, 2, -1, g.shape[-1])
    gb_bh = pieces(wgrad(og, dyh, "wgrad_bh", 512, D_MODEL, tp, True, bf16))
    gb_ba = pieces(wgrad(oatt, dya, "wgrad_ba", 512, D_MODEL, tp, True, bf16))
    gb_out = pieces(wgrad(mixin, dr1, "wgrad_out", D_MODEL, D_MODEL, tp, False, bf16))
    gb_fi = pieces(wgrad(h1b, dau, "wgrad_fi", D_MODEL, D_FF, tp, True, bf16))
    gb_fo = pieces(wgrad(sact, dr2, "wgrad_fo", D_FF // 2, D_MODEL, tp, False, bf16))
    dhq, dhf, dhi, dhg, dlb4, dng, r_fi, r_fo = hgrn_bwd(pa, hg_lower_bounds, ng, sprev, dog, nb, [gb_fi, gb_fo])
    daq, dkc, dkp, dvc, dvp, dkm, dvm, dsk, r_out, r_bh, r_ba = attn_bwd(pa, cos, sin, sinks8, doa, nb,
                                                                         [gb_out, gb_bh, gb_ba])
    dproj, dx, dmeta, dlg, dlb = inproj_bwd(dh0p, dhq, dhf, dhi, dhg, daq, dkc, dkp, dvc, dvp, dkm, dvm, dpg,
                                      wf_in, xs, metablk, vec(ln_emb_g), vec(ln_emb_b))
    gw_in = wgrad(dproj, h0b, "wgrad_in", IN_W // 2, D_MODEL, tp, False).reshape(N_SHARD, -1, D_MODEL)

    parts = [(dlg, 8), (dlb, 8), (dlb4, 8), (dng, 8), (dsk[:, 0], 8),
             (dg1, 8), (db1, 8), (dg2, 8), (db2, 8), (dmeta, BLOCK), (loss_part, 8)]
    small = jnp.concatenate([_rows128(a, r) for a, r in parts], axis=0)

    c_idx = jnp.reshape(ic, (1,)).astype(jnp.int32)
    jc_idx = jnp.stack([shard, ic]).astype(jnp.int32)
    other_in, gathered = pair_exchange_halves([gw_in], small)
    sum_in = add_pair(gw_in, other_in, c_idx, "add_pair_in")
    quad_in, = chip_exchange([sum_in])
    halves = [add_four(sum_in, quad_in, jc_idx, "add_four_in")]
    halves += [add_eight(g, r, jc_idx, "add_eight_" + nm) for nm, g, r in
               (("bh", gb_bh, r_bh), ("ba", gb_ba, r_ba), ("out", gb_out, r_out), ("fi", gb_fi, r_fi),
                ("fo", gb_fo, r_fo))]
    red = [r.reshape(-1, r.shape[-1]) for r in pair_exchange_results(halves)]
    small_sum = sum_devices(gathered)

    offs, acc = [], 0
    for _, r in parts:
        offs.append(acc)
        acc += r
    take = lambda n, size: small_sum[offs[n]:offs[n] + parts[n][1]].reshape(-1)[:size]
    g_meta_full = take(9, N_META * D_MODEL).reshape(N_META, D_MODEL)
    g_small = {
        "meta_tokens": lax.dynamic_slice_in_dim(g_meta_full, shard * (D_MODEL // N_SHARD), D_MODEL // N_SHARD, axis=1),
        "ln_emb_g": take(0, D_MODEL), "ln_emb_b": take(1, D_MODEL),
        "hg_lower_bounds": take(2, 2 * HG_HEADS * HG_K).reshape(2, HG_HEADS * HG_K),
        "hg_norm_g": take(3, HG_K).reshape(1, HG_K), "attn_sinks": take(4, ATT_HEADS).reshape(1, ATT_HEADS),
        "ln1_g": take(5, D_MODEL).reshape(1, D_MODEL), "ln1_b": take(6, D_MODEL).reshape(1, D_MODEL),
        "ln2_g": take(7, D_MODEL).reshape(1, D_MODEL), "ln2_b": take(8, D_MODEL).reshape(1, D_MODEL),
    }
    g_big = {"w_in": red[0], "w_branch_hg": red[1], "w_branch_attn": red[2], "w_out": red[3],
             "w_ffn_in": red[4], "w_ffn_out": red[5]}

    names = ["meta_tokens", "ln_emb_g", "ln_emb_b", "w_in", "hg_lower_bounds", "hg_norm_g", "attn_sinks",
             "w_branch_hg", "w_branch_attn", "w_out", "ln1_g", "ln1_b", "w_ffn_in", "w_ffn_out", "ln2_g", "ln2_b"]
    given = dict(
        meta_tokens=(meta_tokens, m_meta_tokens, v_meta_tokens), ln_emb_g=(ln_emb_g, m_ln_emb_g, v_ln_emb_g),
        ln_emb_b=(ln_emb_b, m_ln_emb_b, v_ln_emb_b), w_in=(w_in, m_w_in, v_w_in),
        hg_lower_bounds=(hg_lower_bounds, m_hg_lower_bounds, v_hg_lower_bounds),
        hg_norm_g=(hg_norm_g, m_hg_norm_g, v_hg_norm_g), attn_sinks=(attn_sinks, m_attn_sinks, v_attn_sinks),
        w_branch_hg=(w_branch_hg, m_w_branch_hg, v_w_branch_hg),
        w_branch_attn=(w_branch_attn, m_w_branch_attn, v_w_branch_attn), w_out=(w_out, m_w_out, v_w_out),
        ln1_g=(ln1_g, m_ln1_g, v_ln1_g), ln1_b=(ln1_b, m_ln1_b, v_ln1_b), w_ffn_in=(w_ffn_in, m_w_ffn_in, v_w_ffn_in),
        w_ffn_out=(w_ffn_out, m_w_ffn_out, v_w_ffn_out), ln2_g=(ln2_g, m_ln2_g, v_ln2_g), ln2_b=(ln2_b, m_ln2_b, v_ln2_b))
    two_d = lambda a: a.reshape(8, BLOCK) if a.ndim == 1 else a.reshape(a.shape[-2], a.shape[-1])
    small_names = [nm for nm in names if nm not in g_big]
    small_d, small_m, small_v = adamw_small([two_d(given[nm][0]) for nm in small_names],
                                            [two_d(g_small[nm]) for nm in small_names],
                                            [two_d(given[nm][1]) for nm in small_names],
                                            [two_d(given[nm][2]) for nm in small_names])
    out_g, out_d, out_m, out_v = [], [], [], []
    for nm in names:
        w, m, v = given[nm]
        shape = w.shape
        if nm == "w_in":
            t = lambda a: jnp.swapaxes(two_d(a), 0, 1)
            g, d, mn, vn = [t(a) for a in adamw(t(w), g_big[nm], t(m), t(v), "adamw_" + nm)]
        elif nm in g_big:
            g, d, mn, vn = adamw(two_d(w), g_big[nm], two_d(m), two_d(v), "adamw_" + nm)
        else:
            k = small_names.index(nm)
            g, d, mn, vn = g_small[nm], small_d[k], small_m[k], small_v[k]
        out_g.append(g.reshape(shape))
        out_d.append(d.reshape(shape))
        out_m.append(mn.reshape(shape))
        out_v.append(vn.reshape(shape))

    loss = take(10, 1)[0]
    grad_x = dx.reshape(x.shape)
    return (loss, grad_x, *out_g, *out_d, *out_m, *out_v)
```

```python
import functools

import jax
import jax.numpy as jnp
from jax import lax
from jax.experimental import pallas as pl
from jax.experimental.pallas import tpu as pltpu

f32 = jnp.float32
bf16 = jnp.bfloat16

D_MODEL = 1024
BLOCK = 128
N_META = 16
PAD = BLOCK - N_META
HG_HEADS = 4
HG_K = 128
SUB = 16
ATT_HEADS = 8
HEAD_DIM = 64
ATT_QW = ATT_HEADS * HEAD_DIM
D_FF = 2816
EPS = 1e-5
ALPHA = 2.0 ** 0.25
ROPE_THETA = 10000.0
N_A = 2816
N_G = 2048
IN_W = N_A + N_G
N_SHARD = 4
N_DEV = 8

ADAM_LR = 0.001
ADAM_B1 = 0.9
ADAM_B2 = 0.999
ADAM_EPS = 1e-08
ADAM_WD = 0.01
ADAM_STEP = 10

TM = 256
LEAD = TM // BLOCK - 1

VMEM_LIMIT = 56 * 1024 * 1024
MESH = pl.DeviceIdType.MESH


def _cparams(sem, vmem=VMEM_LIMIT):
    return pltpu.CompilerParams(dimension_semantics=sem, vmem_limit_bytes=vmem)


def _const_spec(shape):
    zeros = (0,) * len(shape)
    return pl.BlockSpec(shape, lambda *_: zeros, pipeline_mode=pl.Buffered(1))


def _dot(a, b, ca, cb):
    return lax.dot_general(a.astype(bf16), b.astype(bf16), (((ca,), (cb,)), ((), ())),
                           preferred_element_type=f32)


@jax.custom_vjp
def mm(a, b):
    return _dot(a, b, 1, 0)


mm.defvjp(lambda a, b: (_dot(a, b, 1, 0), (a, b)),
          lambda r, g: (_dot(g, r[1], 1, 1), _dot(r[0], g, 0, 0)))


@jax.custom_vjp
def mm_nt(a, b):
    return _dot(a, b, 1, 1)


mm_nt.defvjp(lambda a, b: (_dot(a, b, 1, 1), (a, b)),
             lambda r, g: (_dot(g, r[1], 1, 0), _dot(g, r[0], 0, 0)))


@jax.custom_vjp
def mm_tn(a, b):
    return _dot(a, b, 0, 0)


mm_tn.defvjp(lambda a, b: (_dot(a, b, 0, 0), (a, b)),
             lambda r, g: (_dot(r[1], g, 1, 1), _dot(r[0], g, 1, 0)))


@functools.partial(jax.custom_vjp, nondiff_argnums=(1,))
def roll_lanes(x, shift):
    return pltpu.roll(x, shift, 1)


roll_lanes.defvjp(lambda x, shift: (pltpu.roll(x, shift, 1), None),
                  lambda shift, _, g: (pltpu.roll(g, (128 - shift) % 128, 1),))


@jax.custom_vjp
def _sigmoid(x):
    return 1.0 / (1.0 + jnp.exp(-x))


def _sigmoid_fwd(x):
    s = 1.0 / (1.0 + jnp.exp(-x))
    return s, s


_sigmoid.defvjp(_sigmoid_fwd, lambda s, g: (g * s * (1.0 - s),))


@jax.custom_vjp
def _recip(x):
    return 1.0 / x


def _recip_fwd(x):
    r = 1.0 / x
    return r, r


_recip.defvjp(_recip_fwd, lambda r, g: (-g * r * r,))


def _ln_stats(x):
    mu = jnp.mean(x, axis=-1, keepdims=True)
    xc = x - mu
    var = jnp.mean(xc * xc, axis=-1, keepdims=True)
    rs = lax.rsqrt(var + EPS)
    return xc * rs, rs


def _ln_bwd(dy, xh, rs, g):
    dxh = dy * g
    m1 = jnp.mean(dxh, axis=-1, keepdims=True)
    m2 = jnp.mean(dxh * xh, axis=-1, keepdims=True)
    return rs * (dxh - m1 - xh * m2)


def _row_ids(i):
    return i * BLOCK + lax.broadcasted_iota(jnp.int32, (BLOCK, 1), 0)


def _tm_rows(i):
    return i * TM + lax.broadcasted_iota(jnp.int32, (TM, 1), 0)


def _tm_row(n):
    return pl.BlockSpec((TM, n), lambda i: (i, 0))


def _tm_tokens():
    return pl.BlockSpec((TM, D_MODEL), lambda i: (jnp.maximum(i - 1, 0), 0))


Q_COL, V_COL = 4 * HG_HEADS * HG_K, N_A - BLOCK


def emb_inproj(x, metablk, g, b, w_in, cos, sin):
    nsteps = x.shape[0] // TM + 1

    def body(x_ref, mb_ref, g_ref, b_ref, w_ref, cos_ref, sin_ref, h0_ref, h0b_ref, pa_ref, pg_ref):
        i = pl.program_id(0)
        xb = jnp.where(i == 0, mb_ref[...], x_ref[...])
        xh, _ = _ln_stats(xb)
        y = xh * g_ref[...] + b_ref[...]
        y = jnp.where(_tm_rows(i) >= TM - N_META, y, 0.0)
        h0_ref[...] = y
        yb = y.astype(bf16)
        h0b_ref[...] = yb
        pa = _dot(yb, w_ref[:N_A, :], 1, 1)
        cos, sin = cos_ref[...], sin_ref[...]
        pa_ref[:, :Q_COL] = pa[:, :Q_COL]
        for c0 in range(Q_COL, V_COL, BLOCK):
            pa_ref[:, c0:c0 + BLOCK] = _rope(pa[:, c0:c0 + BLOCK], cos, sin)
        pa_ref[:, V_COL:] = pa[:, V_COL:]
        pg_ref[...] = _dot(yb, w_ref[N_A:, :], 1, 1)

    p = nsteps * TM
    row = _tm_row
    return pl.pallas_call(
        body, name="emb_inproj", grid=(nsteps,),
        in_specs=[_tm_tokens(),
                  _const_spec((TM, D_MODEL)), _const_spec((1, D_MODEL)), _const_spec((1, D_MODEL)),
                  _const_spec((IN_W, D_MODEL)), _tm_row(BLOCK), _tm_row(BLOCK)],
        out_specs=[row(D_MODEL), row(D_MODEL), row(N_A), row(N_G)],
        out_shape=[jax.ShapeDtypeStruct((p, D_MODEL), f32), jax.ShapeDtypeStruct((p, D_MODEL), bf16),
                   jax.ShapeDtypeStruct((p, N_A), f32), jax.ShapeDtypeStruct((p, N_G), f32)],
        compiler_params=_cparams(("parallel",)),
    )(x, metablk, g, b, w_in, cos, sin)


def _hgrn_chunk(valid, st, hq, hf, hi, hg, lbraw, ng):
    lb = _sigmoid(lbraw[0:1] - lbraw[1:2])
    q = hq * _sigmoid(hq)
    fg = lb + (1.0 - lb) * _sigmoid(hf)
    logf = jnp.where(valid, jnp.log(fg), 0.0)
    k = jnp.where(valid, 1.0 - fg, 0.0)
    v = hi
    r = lax.broadcasted_iota(jnp.int32, (BLOCK, BLOCK), 0)
    c = lax.broadcasted_iota(jnp.int32, (BLOCK, BLOCK), 1)
    tril = (c <= r).astype(f32)
    bcum = jnp.dot(tril, logf, precision=lax.Precision.HIGHEST, preferred_element_type=f32)
    blast = bcum[BLOCK - 1:BLOCK]
    rows = lax.broadcasted_iota(jnp.int32, (BLOCK, 1), 0)
    sub8 = lax.broadcasted_iota(jnp.int32, (BLOCK // 8, 8, HG_K), 1)
    b8 = bcum.reshape(BLOCK // 8, 8, HG_K)
    row_of_8 = lambda j: jnp.broadcast_to(b8[:, j:j + 1, :], b8.shape)
    a = jnp.where(r == c, jnp.sum(q * k, axis=-1, keepdims=True), 0.0)
    seg = BLOCK
    while seg >= 2:
        half = seg // 2
        if seg >= 8:
            bs = bcum.reshape(BLOCK // seg, seg, HG_K)
            ref = jnp.broadcast_to(bs[:, half - 1:half, :], bs.shape)
        elif seg == 4:
            ref = jnp.where(sub8 < 4, row_of_8(1), row_of_8(5))
        else:
            ref = jnp.where(sub8 < 2, row_of_8(0), jnp.where(sub8 < 4, row_of_8(2),
                                                             jnp.where(sub8 < 6, row_of_8(4), row_of_8(6))))
        ref = ref.reshape(BLOCK, HG_K)
        upper = (rows % seg) >= half
        q_up = q * jnp.exp(jnp.where(upper, bcum - ref, -jnp.inf))
        k_lo = k * jnp.exp(jnp.where(upper, -jnp.inf, ref - bcum))
        a = a + jnp.where((r // seg) == (c // seg), mm_nt(q_up, k_lo), 0.0)
        seg = half
    o = mm_nt(q * jnp.exp(bcum), st) + mm(a, v)
    st_new = st * jnp.exp(blast) + mm_tn(v, k * jnp.exp(blast - bcum))
    on = o * lax.rsqrt(jnp.mean(o * o, axis=-1, keepdims=True) + EPS) * ng
    return st_new, on * (hg * _sigmoid(hg))


def _hgrn_in_specs(rowmap):
    wide = lambda col: pl.BlockSpec((BLOCK, HG_HEADS * HG_K), lambda i: (rowmap(i) + LEAD, col))
    return [wide(0), wide(1), wide(2), wide(3), _const_spec((2, HG_HEADS * HG_K)), _const_spec((1, HG_K))]


def _head(ref, h):
    return ref[:, h * HG_K:(h + 1) * HG_K]


def hgrn_fwd(pa, lbraw, ng, nb, shards):
    n = len(shards)

    def body(hq_ref, hf_ref, hi_ref, hg_ref, lb_ref, ng_ref, *rest):
        srcs, (og_ref, sp_ref), dsts = rest[:n], rest[n:n + 2], rest[n + 2:2 * n + 2]
        st_ref = rest[2 * n + 2]
        start, wait = _shard_push(srcs, dsts, *rest[2 * n + 3:])
        i = pl.program_id(0)

        @pl.when(i == 0)
        def _():
            st_ref[...] = jnp.zeros_like(st_ref)
            start()

        @pl.when(i == nb - 1)
        def _():
            wait()

        valid = _row_ids(i) >= PAD
        for h in range(HG_HEADS):
            st = st_ref[h]
            sp_ref[0, h] = st
            st_new, out = _hgrn_chunk(valid, st, _head(hq_ref, h), _head(hf_ref, h), _head(hi_ref, h),
                                      _head(hg_ref, h), _head(lb_ref, h), ng_ref[...])
            st_ref[h] = st_new
            og_ref[:, h * HG_K:(h + 1) * HG_K] = out.astype(bf16)

    p = (nb + LEAD) * BLOCK
    push_in, push_out, push_shape, push_scratch = _push_specs(shards)
    return pl.pallas_call(
        body, name="hgrn_fwd", grid=(nb,),
        in_specs=_hgrn_in_specs(lambda i: i) + push_in,
        out_specs=[pl.BlockSpec((BLOCK, HG_HEADS * HG_K), lambda i: (i + LEAD, 0)),
                   pl.BlockSpec((1, HG_HEADS, HG_K, HG_K), lambda i: (i, 0, 0, 0))] + push_out,
        out_shape=[jax.ShapeDtypeStruct((p, HG_HEADS * HG_K), bf16),
                   jax.ShapeDtypeStruct((nb, HG_HEADS, HG_K, HG_K), f32)] + push_shape,
        scratch_shapes=[pltpu.VMEM((HG_HEADS, HG_K, HG_K), f32)] + push_scratch,
        compiler_params=_cparams(("arbitrary",)),
    )(pa, pa, pa, pa, lbraw, ng, *shards)


def hgrn_bwd(pa, lbraw, ng, sprev, dog, nb, grads):
    n = len(grads)

    def body(hq_ref, hf_ref, hi_ref, hg_ref, lb_ref, ng_ref, sp_ref, do_ref, *rest):
        srcs, rest = rest[:n], rest[n:]
        dq_ref, df_ref, di_ref, dg_ref, dlb_ref, dng_ref = rest[:6]
        dsts, dst_ref = rest[6:6 + n], rest[6 + n]
        start, wait = _grad_push(srcs, dsts, *rest[7 + n:])
        i = pl.program_id(0)

        @pl.when(i == 0)
        def _():
            dst_ref[...] = jnp.zeros_like(dst_ref)
            dlb_ref[...] = jnp.zeros_like(dlb_ref)
            dng_ref[...] = jnp.zeros_like(dng_ref)
            start()

        valid = _row_ids(nb - 1 - i) >= PAD
        dng_sum = jnp.zeros((1, HG_K), f32)
        for h in range(HG_HEADS):
            cols = slice(h * HG_K, (h + 1) * HG_K)
            _, vjp = jax.vjp(functools.partial(_hgrn_chunk, valid), sp_ref[0, h], _head(hq_ref, h), _head(hf_ref, h),
                             _head(hi_ref, h), _head(hg_ref, h), _head(lb_ref, h), ng_ref[...])
            dst, dq, df, di, dg, dlb, dng = vjp((dst_ref[h], _head(do_ref, h)))
            dst_ref[h] = dst
            dq_ref[:, cols] = dq.astype(bf16)
            df_ref[:, cols] = df.astype(bf16)
            di_ref[:, cols] = di.astype(bf16)
            dg_ref[:, cols] = dg.astype(bf16)
            dlb_ref[:, cols] += dlb
            dng_sum = dng_sum + dng
        dng_ref[...] += dng_sum
        pl.when(i == nb - 1)(wait)

    p = (nb + LEAD) * BLOCK
    rev = lambda i: nb - 1 - i
    hw = HG_HEADS * HG_K
    blk = pl.BlockSpec((BLOCK, hw), lambda i: (rev(i) + LEAD, 0))
    wide = jax.ShapeDtypeStruct((p, hw), bf16)
    push_in, push_out, push_shape, push_scratch = _grad_push_specs(grads)
    return pl.pallas_call(
        body, name="hgrn_bwd", grid=(nb,),
        in_specs=_hgrn_in_specs(rev) + [pl.BlockSpec((1, HG_HEADS, HG_K, HG_K), lambda i: (rev(i), 0, 0, 0)), blk]
        + push_in,
        out_specs=[blk, blk, blk, blk, pl.BlockSpec((2, hw), lambda i: (0, 0)), pl.BlockSpec((1, HG_K), lambda i: (0, 0))]
        + push_out,
        out_shape=[wide, wide, wide, wide, jax.ShapeDtypeStruct((2, hw), f32), jax.ShapeDtypeStruct((1, HG_K), f32)]
        + push_shape,
        scratch_shapes=[pltpu.VMEM((HG_HEADS, HG_K, HG_K), f32)] + push_scratch,
        compiler_params=_cparams(("arbitrary",)),
    )(pa, pa, pa, pa, lbraw, ng, sprev, dog, *grads)


def _rot_half(x):
    lane = lax.broadcasted_iota(jnp.int32, x.shape, 1)
    return jnp.where(lane % HEAD_DIM < HEAD_DIM // 2, -pltpu.roll(x, BLOCK - HEAD_DIM // 2, 1),
                     pltpu.roll(x, HEAD_DIM // 2, 1))


def _rope(x, cos, sin):
    return x * cos + _rot_half(x) * sin


def _rope_transposed(g, cos, sin):
    return g * cos - _rot_half(g * sin)


def _both_halves(x, g):
    lo = lax.broadcasted_iota(jnp.int32, x.shape, 1) < HEAD_DIM
    sw = roll_lanes(x, HEAD_DIM)
    return jnp.where(lo, x, sw) if g == 0 else jnp.where(lo, sw, x)


def _attn_block(band_ok, meta_ok, q, kp, kc, vp, vc, km, vm, *sinks):
    neg = jnp.finfo(f32).min
    scale = HEAD_DIM ** -0.5
    group = ATT_HEADS // 2
    lo = lax.broadcasted_iota(jnp.int32, (BLOCK, BLOCK), 1) < HEAD_DIM
    t = lax.broadcasted_iota(jnp.int32, (group * BLOCK, BLOCK), 0) % BLOCK
    col = lax.broadcasted_iota(jnp.int32, (group * BLOCK, BLOCK), 1)
    own = col <= t
    is_sink = col == N_META
    qr = [q[:, m * BLOCK:(m + 1) * BLOCK] for m in range(ATT_HEADS // 2)]
    slabs = []
    for g in range(2):
        kp_g, kc_g, vp_g, vc_g, km_g, vm_g = [_both_halves(a, g) for a in (kp, kc, vp, vc, km, vm)]
        qs = jnp.concatenate([jnp.where(lo if h % 2 == 0 else ~lo, qr[2 * g + h // 2], 0.0) for h in range(group)],
                             axis=0)
        sink = jnp.concatenate([jnp.broadcast_to(sinks[group * g + h], (BLOCK, 1)) for h in range(group)], axis=0)
        sb = jnp.where(band_ok, jnp.where(own, mm_nt(qs, kc_g), mm_nt(qs, kp_g)) * scale, neg)
        no_keys = jnp.zeros((BLOCK - N_META, BLOCK), f32)
        sme = jnp.where(meta_ok, mm_nt(qs, jnp.concatenate([km_g, no_keys], axis=0)) * scale,
                        jnp.where(is_sink, sink, neg))
        mx = lax.stop_gradient(jnp.max(jnp.maximum(sb, sme), axis=-1, keepdims=True))
        eb, em = jnp.exp(sb - mx), jnp.exp(sme - mx)
        inv = _recip(jnp.sum(eb + em, axis=-1, keepdims=True))
        pb = eb * inv
        o = (mm(jnp.where(own, pb, 0.0), vc_g) + mm(jnp.where(own, 0.0, pb), vp_g)
             + mm(em * inv, jnp.concatenate([vm_g, no_keys], axis=0)))
        for m in range(2):
            slabs.append(jnp.where(lo, o[2 * m * BLOCK:(2 * m + 1) * BLOCK], o[(2 * m + 1) * BLOCK:(2 * m + 2) * BLOCK]))
    return jnp.concatenate(slabs, axis=1)


def _attn_masks(i):
    group = ATT_HEADS // 2
    t = lax.broadcasted_iota(jnp.int32, (group * BLOCK, BLOCK), 0) % BLOCK
    s = lax.broadcasted_iota(jnp.int32, (group * BLOCK, BLOCK), 1)
    kpos = jnp.where(s <= t, i * BLOCK - PAD + s, jnp.where(i > 0, (i - 1) * BLOCK - PAD + s, -1))
    band_ok = kpos >= N_META
    qpos = i * BLOCK - PAD + lax.broadcasted_iota(jnp.int32, (group * BLOCK, 1), 0) % BLOCK
    meta_ok = (s < N_META) & (s <= qpos)
    return band_ok, meta_ok


def _attn_in_specs(cur=lambda i: i):
    prev = lambda i: jnp.maximum(cur(i) - 1, 0)
    kcol, vcol = N_A // BLOCK - 2, N_A // BLOCK - 1
    blk = lambda rowmap, col: pl.BlockSpec((BLOCK, BLOCK), lambda i: (rowmap(i) + LEAD, col))
    first = lambda i: 0
    return [pl.BlockSpec((BLOCK, ATT_QW), lambda i: (cur(i) + LEAD, 4)),
            blk(prev, kcol), blk(cur, kcol), blk(prev, vcol), blk(cur, vcol), blk(first, kcol), blk(first, vcol),
            _const_spec((ATT_HEADS, BLOCK))]


def _attn_row(n, cur=lambda i: i):
    return pl.BlockSpec((BLOCK, n), lambda i: (cur(i) + LEAD, 0))


def _attn_operands(q_ref, kp_ref, kc_ref, vp_ref, vc_ref, km_ref, vm_ref, sk_ref):
    args = (q_ref[...], kp_ref[...], kc_ref[...], vp_ref[...], vc_ref[...], km_ref[PAD:, :], vm_ref[PAD:, :])
    sinks = tuple(sk_ref[j:j + 1, 0:1] for j in range(ATT_HEADS))
    return args + sinks


def attn_fwd(pa, sinks8, nb, shards):
    n = len(shards)
    n_in = 8

    def body(*refs):
        srcs, o_ref, dsts = refs[n_in:n_in + n], refs[n_in + n], refs[n_in + n + 1:n_in + 2 * n + 1]
        start, wait = _shard_push(srcs, dsts, *refs[n_in + 2 * n + 1:])
        i = pl.program_id(0)
        pl.when(i == 0)(start)
        band_ok, meta_ok = _attn_masks(i)
        o_ref[...] = _attn_block(band_ok, meta_ok, *_attn_operands(*refs[:n_in])).astype(bf16)
        pl.when(i == nb - 1)(wait)

    push_in, push_out, push_shape, push_scratch = _push_specs(shards)
    return pl.pallas_call(
        body, name="attn_fwd", grid=(nb,), in_specs=_attn_in_specs() + push_in,
        out_specs=[_attn_row(ATT_QW)] + push_out,
        out_shape=[jax.ShapeDtypeStruct(((nb + LEAD) * BLOCK, ATT_QW), bf16)] + push_shape,
        scratch_shapes=push_scratch,
        compiler_params=_cparams(("arbitrary",)),
    )(pa, pa, pa, pa, pa, pa, pa, sinks8, *shards)


def attn_bwd(pa, sinks8, do, nb, grads):
    n = len(grads)

    def body(*refs):
        do_ref, srcs = refs[8], refs[9:9 + n]
        dq_ref, dkc_ref, dkp_ref, dvc_ref, dvp_ref, dkm_ref, dvm_ref, dsk_ref = refs[9 + n:17 + n]
        start, wait = _grad_push(srcs, refs[17 + n:17 + 2 * n], *refs[17 + 2 * n:])
        i = pl.program_id(0)

        @pl.when(i == 0)
        def _():
            dkm_ref[...] = jnp.zeros((N_META, BLOCK), f32)
            dvm_ref[...] = jnp.zeros((N_META, BLOCK), f32)
            dsk_ref[...] = jnp.zeros((ATT_HEADS, BLOCK), f32)
            start()

        band_ok, meta_ok = _attn_masks(i)
        _, vjp = jax.vjp(functools.partial(_attn_block, band_ok, meta_ok), *_attn_operands(*refs[:8]))
        grads = vjp(do_ref[...])
        dq_ref[...] = grads[0]
        dkp_ref[...] = grads[1]
        dkc_ref[...] = grads[2]
        dvp_ref[...] = grads[3]
        dvc_ref[...] = grads[4]
        dkm_ref[...] += grads[5]
        dvm_ref[...] += grads[6]
        for j in range(ATT_HEADS):
            dsk_ref[j:j + 1, :] += jnp.broadcast_to(grads[7 + j], (1, BLOCK))
        pl.when(i == nb - 1)(wait)

    p = (nb + LEAD) * BLOCK
    row = _attn_row(BLOCK)
    const = lambda r: pl.BlockSpec((r, BLOCK), lambda i: (0, 0))
    part = jax.ShapeDtypeStruct((p, BLOCK), f32)
    push_in, push_out, push_shape, push_scratch = _grad_push_specs(grads)
    return pl.pallas_call(
        body, name="attn_bwd", grid=(nb,),
        in_specs=_attn_in_specs() + [_attn_row(ATT_QW)] + push_in,
        out_specs=[_attn_row(ATT_QW), row, row, row, row,
                   const(N_META), const(N_META), const(ATT_HEADS)] + push_out,
        out_shape=[jax.ShapeDtypeStruct((p, ATT_QW), f32), part, part, part, part,
                   jax.ShapeDtypeStruct((N_META, BLOCK), f32), jax.ShapeDtypeStruct((N_META, BLOCK), f32),
                   jax.ShapeDtypeStruct((ATT_HEADS, BLOCK), f32)] + push_shape,
        scratch_shapes=push_scratch,
        compiler_params=_cparams(("arbitrary",)),
    )(pa, pa, pa, pa, pa, pa, pa, sinks8, do, *grads)


def _mid_forward(h0_ref, pg_ref, og, oa, wbh_ref, wba_ref, wo_ref, g1, b1):
    yh = jnp.dot(og, wbh_ref[...], preferred_element_type=f32)
    ya = jnp.dot(oa, wba_ref[...], preferred_element_type=f32)
    gh = _sigmoid(pg_ref[:, :D_MODEL])
    ga = _sigmoid(pg_ref[:, D_MODEL:])
    mixin = (gh * yh + ga * ya).astype(bf16)
    r1 = ALPHA * h0_ref[...] + jnp.dot(mixin, wo_ref[...], preferred_element_type=f32)
    xh1, rs1 = _ln_stats(r1)
    return yh, ya, gh, ga, mixin, xh1, rs1, xh1 * g1 + b1


def _mid_weight_specs():
    hw = HG_HEADS * HG_K
    return [_const_spec((hw, D_MODEL)), _const_spec((ATT_QW, D_MODEL)), _const_spec((D_MODEL, D_MODEL)),
            _const_spec((1, D_MODEL)), _const_spec((1, D_MODEL))]


def mid_front(h0, pg, og, oatt, wbh, wba, wout, ln1g, ln1b):
    def body(h0_ref, pg_ref, og_ref, oa_ref, wbh_ref, wba_ref, wo_ref, g1_ref, b1_ref,
             h1_ref, h1b_ref, mix_ref, ogc_ref, oac_ref):
        used = _tm_rows(pl.program_id(0)) >= LEAD * BLOCK
        og = jnp.where(used, og_ref[...], jnp.zeros_like(og_ref))
        oa = jnp.where(used, oa_ref[...], jnp.zeros_like(oa_ref))
        ogc_ref[...] = og
        oac_ref[...] = oa
        *_, mixin, _, _, h1 = _mid_forward(h0_ref, pg_ref, og, oa, wbh_ref, wba_ref, wo_ref, g1_ref[...], b1_ref[...])
        mix_ref[...] = mixin
        h1_ref[...] = h1
        h1b_ref[...] = h1.astype(bf16)

    p = h0.shape[0]
    hw = HG_HEADS * HG_K
    sds = lambda n, dt: jax.ShapeDtypeStruct((p, n), dt)
    return pl.pallas_call(
        body, name="mid_front", grid=(p // TM,),
        in_specs=[_tm_row(D_MODEL), _tm_row(N_G), _tm_row(hw), _tm_row(ATT_QW)] + _mid_weight_specs(),
        out_specs=[_tm_row(D_MODEL), _tm_row(D_MODEL), _tm_row(D_MODEL), _tm_row(hw), _tm_row(ATT_QW)],
        out_shape=[sds(D_MODEL, f32), sds(D_MODEL, bf16), sds(D_MODEL, bf16), sds(hw, bf16), sds(ATT_QW, bf16)],
        compiler_params=_cparams(("parallel",)),
    )(h0, pg, og, oatt, wbh, wba, wout, ln1g, ln1b)


def mid_ffn(h1, target, wfi, wfo, ln2g, ln2b):
    def body(h1_ref, t_ref, wfi_ref, wfo_ref, g2_ref, b2_ref,
             dh1_ref, dau_ref, s_ref, dr2_ref, loss_ref, dg2_ref, db2_ref):
        i = pl.program_id(0)

        @pl.when(i == 0)
        def _():
            for r in (loss_ref, dg2_ref, db2_ref):
                r[...] = jnp.zeros_like(r)

        g2, b2 = g2_ref[...], b2_ref[...]
        h1 = h1_ref[...]
        au = jnp.dot(h1.astype(bf16), wfi_ref[...], preferred_element_type=f32)
        a, u = au[:, :D_FF], au[:, D_FF:]
        sg = _sigmoid(a)
        sa = a * sg
        s = (sa * u).astype(bf16)
        s_ref[...] = s
        r2 = ALPHA * h1 + jnp.dot(s, wfo_ref[...], preferred_element_type=f32)
        xh2, rs2 = _ln_stats(r2)
        diff = jnp.where(i > 0, xh2 * g2 + b2 - t_ref[...], 0.0)
        loss_ref[...] += jnp.sum(diff * diff) * (0.5 / D_MODEL)
        dy = diff * (1.0 / D_MODEL)
        dg2_ref[...] += jnp.sum(dy * xh2, axis=0, keepdims=True)
        db2_ref[...] += jnp.sum(dy, axis=0, keepdims=True)
        dr2 = _ln_bwd(dy, xh2, rs2, g2)
        dr2b = dr2.astype(bf16)
        dr2_ref[...] = dr2b
        ds = _dot(dr2b, wfo_ref[...], 1, 1)
        da = (ds * u) * (sg * (1.0 + a * (1.0 - sg)))
        du = ds * sa
        dau = jnp.concatenate([da, du], axis=1).astype(bf16)
        dau_ref[...] = dau
        dh1_ref[...] = ALPHA * dr2 + _dot(dau, wfi_ref[...], 1, 1)

    p = h1.shape[0]
    vec = lambda: pl.BlockSpec((1, D_MODEL), lambda i: (0, 0))
    sds = lambda n, dt: jax.ShapeDtypeStruct((p, n), dt)
    return pl.pallas_call(
        body, name="mid_ffn", grid=(p // TM,),
        in_specs=[_tm_row(D_MODEL), _tm_tokens(), _const_spec((D_MODEL, 2 * D_FF)), _const_spec((D_FF, D_MODEL)),
                  _const_spec((1, D_MODEL)), _const_spec((1, D_MODEL))],
        out_specs=[_tm_row(D_MODEL), _tm_row(2 * D_FF), _tm_row(D_FF), _tm_row(D_MODEL),
                   pl.BlockSpec((1, 1), lambda i: (0, 0)), vec(), vec()],
        out_shape=[sds(D_MODEL, f32), sds(2 * D_FF, bf16), sds(D_FF, bf16), sds(D_MODEL, bf16),
                   jax.ShapeDtypeStruct((1, 1), f32)] + [jax.ShapeDtypeStruct((1, D_MODEL), f32)] * 2,
        compiler_params=_cparams(("arbitrary",)),
    )(h1, target, wfi, wfo, ln2g, ln2b)


def mid_back(dh1, h0, pg, ogc, oac, wbh, wba, wout, ln1g, ln1b):
    def body(dh1_ref, h0_ref, pg_ref, og_ref, oa_ref, wbh_ref, wba_ref, wo_ref, g1_ref, b1_ref,
             dh0_ref, dpg_ref, dog_ref, doa_ref, dyh_ref, dya_ref, dr1_ref, dg1_ref, db1_ref):
        @pl.when(pl.program_id(0) == 0)
        def _():
            dg1_ref[...] = jnp.zeros_like(dg1_ref)
            db1_ref[...] = jnp.zeros_like(db1_ref)

        g1 = g1_ref[...]
        yh, ya, gh, ga, _, xh1, rs1, _ = _mid_forward(h0_ref, pg_ref, og_ref[...], oa_ref[...], wbh_ref, wba_ref,
                                                      wo_ref, g1, b1_ref[...])
        dh1 = dh1_ref[...]
        dg1_ref[...] += jnp.sum(dh1 * xh1, axis=0, keepdims=True)
        db1_ref[...] += jnp.sum(dh1, axis=0, keepdims=True)
        dr1 = _ln_bwd(dh1, xh1, rs1, g1)
        dr1b = dr1.astype(bf16)
        dr1_ref[...] = dr1b
        dh0_ref[...] = ALPHA * dr1
        dmix = _dot(dr1b, wo_ref[...], 1, 1)
        dyh = (dmix * gh).astype(bf16)
        dya = (dmix * ga).astype(bf16)
        dyh_ref[...] = dyh
        dya_ref[...] = dya
        dpg_ref[:, :D_MODEL] = (dmix * yh * gh * (1.0 - gh)).astype(bf16)
        dpg_ref[:, D_MODEL:] = (dmix * ya * ga * (1.0 - ga)).astype(bf16)
        dog_ref[...] = _dot(dyh, wbh_ref[...], 1, 1)
        doa_ref[...] = _dot(dya, wba_ref[...], 1, 1)

    p = h0.shape[0]
    hw = HG_HEADS * HG_K
    vec = lambda: pl.BlockSpec((1, D_MODEL), lambda i: (0, 0))
    sds = lambda n, dt: jax.ShapeDtypeStruct((p, n), dt)
    return pl.pallas_call(
        body, name="mid_back", grid=(p // TM,),
        in_specs=[_tm_row(D_MODEL), _tm_row(D_MODEL), _tm_row(N_G), _tm_row(hw), _tm_row(ATT_QW)] + _mid_weight_specs(),
        out_specs=[_tm_row(D_MODEL), _tm_row(N_G), _tm_row(hw), _tm_row(ATT_QW), _tm_row(D_MODEL), _tm_row(D_MODEL),
                   _tm_row(D_MODEL), vec(), vec()],
        out_shape=[sds(D_MODEL, f32), sds(N_G, bf16), sds(hw, f32), sds(ATT_QW, f32), sds(D_MODEL, bf16),
                   sds(D_MODEL, bf16), sds(D_MODEL, bf16)] + [jax.ShapeDtypeStruct((1, D_MODEL), f32)] * 2,
        compiler_params=_cparams(("arbitrary",)),
    )(dh1, h0, pg, ogc, oac, wbh, wba, wout, ln1g, ln1b)


def inproj_bwd(dh0p, dhq, dhf, dhi, dhg, daq, dkc, dkp, dvc, dvp, dkm, dvm, dpg, w_in, x, metablk, g, b, cos, sin):
    p = dh0p.shape[0]
    nbk = p // BLOCK
    per = TM // BLOCK

    def body(dh0_ref, dq_ref, df_ref, di_ref, dg_ref, daq_ref, dkc_ref, *rest):
        dkp_refs, dvc_ref, dvp_refs = rest[:per], rest[per], rest[per + 1:2 * per + 1]
        (dkm_ref, dvm_ref, dpg_ref, w_ref, x_ref, mb_ref, g_ref, b_ref, cos_ref, sin_ref,
         dproj_ref, dx_ref, dmeta_ref, dlg_ref, dlb_ref) = rest[2 * per + 1:]
        i = pl.program_id(0)

        @pl.when(i == 0)
        def _():
            dlg_ref[...] = jnp.zeros_like(dlg_ref)
            dlb_ref[...] = jnp.zeros_like(dlb_ref)

        zero_pad = jnp.zeros((TM - N_META, BLOCK), f32)
        first = i == 0
        rows = _tm_rows(i)

        def keys(cur_ref, next_refs, meta_ref):
            nxt = jnp.concatenate([jnp.where(per * i + 1 + m < nbk, next_refs[m][...], 0.0) for m in range(per)], axis=0)
            t = cur_ref[...] + nxt
            return t + jnp.where(first, jnp.concatenate([zero_pad, meta_ref[...]], axis=0), 0.0)

        cos, sin = cos_ref[...], sin_ref[...]
        unrotate = lambda t: _rope_transposed(t, cos, sin).astype(bf16)
        dproj = jnp.concatenate(
            [dq_ref[...], df_ref[...], di_ref[...], dg_ref[...]]
            + [unrotate(daq_ref[:, m * BLOCK:(m + 1) * BLOCK]) for m in range(ATT_QW // BLOCK)]
            + [unrotate(keys(dkc_ref, dkp_refs, dkm_ref)), keys(dvc_ref, dvp_refs, dvm_ref).astype(bf16),
               dpg_ref[...]], axis=1)
        dproj = jnp.where(rows >= LEAD * BLOCK, dproj, jnp.zeros_like(dproj))
        dproj_ref[...] = dproj
        valid = rows >= TM - N_META
        dh0 = jnp.where(valid, dh0_ref[...] + _dot(dproj, w_ref[...], 1, 0), 0.0)
        xb = jnp.where(first, mb_ref[...], x_ref[...])
        xh, rs = _ln_stats(xb)
        dlg_ref[...] += jnp.sum(dh0 * xh, axis=0, keepdims=True)
        dlb_ref[...] += jnp.sum(dh0, axis=0, keepdims=True)
        dx = jnp.where(valid, _ln_bwd(dh0, xh, rs, g_ref[...]), 0.0)
        dx_ref[...] = dx

        @pl.when(first)
        def _():
            dmeta_ref[...] = dx[TM - N_META:, :]

    row = _tm_row
    nxt = [pl.BlockSpec((BLOCK, BLOCK), functools.partial(lambda i, m: (jnp.minimum(per * i + 1 + m, nbk - 1), 0), m=m))
           for m in range(per)]
    hw = HG_HEADS * HG_K
    vec = lambda: pl.BlockSpec((1, D_MODEL), lambda i: (0, 0))
    return pl.pallas_call(
        body, name="inproj_bwd", grid=(p // TM,),
        in_specs=[row(D_MODEL), row(hw), row(hw), row(hw), row(hw), row(ATT_QW),
                  row(BLOCK)] + nxt + [row(BLOCK)] + nxt + [_const_spec((N_META, BLOCK)), _const_spec((N_META, BLOCK)),
                  row(N_G), _const_spec((IN_W, D_MODEL)), _tm_tokens(),
                  _const_spec((TM, D_MODEL)), _const_spec((1, D_MODEL)), _const_spec((1, D_MODEL)),
                  row(BLOCK), row(BLOCK)],
        out_specs=[row(IN_W), _tm_tokens(), pl.BlockSpec((N_META, D_MODEL), lambda i: (0, 0)), vec(), vec()],
        out_shape=[jax.ShapeDtypeStruct((p, IN_W), bf16), jax.ShapeDtypeStruct((p - TM, D_MODEL), f32),
                   jax.ShapeDtypeStruct((N_META, D_MODEL), f32),
                   jax.ShapeDtypeStruct((1, D_MODEL), f32), jax.ShapeDtypeStruct((1, D_MODEL), f32)],
        compiler_params=_cparams(("arbitrary",)),
    )(dh0p, dhq, dhf, dhi, dhg, daq, dkc, *([dkp] * per), dvc, *([dvp] * per), dkm, dvm, dpg, w_in, x, metablk, g, b,
      cos, sin)


def wgrad(a, b, name, tk, tn, tp, by_cols, out_dtype=f32):
    p, k = a.shape
    n = b.shape[1]
    nsteps = p // tp

    def body(a_ref, b_ref, o_ref, acc_ref):
        ip = pl.program_id(2)

        @pl.when(ip == 0)
        def _():
            acc_ref[...] = jnp.zeros_like(acc_ref)

        acc_ref[...] += _dot(a_ref[...], b_ref[...], 0, 0)

        @pl.when(ip == nsteps - 1)
        def _():
            for j in range(span):
                o_ref[j] = acc_ref[:, j * width:(j + 1) * width].astype(out_dtype)

    span, width = 1, tn
    if by_cols:
        shard_n = n // N_SHARD
        out_shape = (N_SHARD, k, shard_n)
        if tn >= shard_n:
            span, width = tn // shard_n, shard_n
            omap = lambda ik, jn, ip: (jn, ik, 0)
        else:
            per = shard_n // tn
            omap = lambda ik, jn, ip: (jn // per, ik, jn % per)
    else:
        out_shape = (1, k, n)
        omap = lambda ik, jn, ip: (0, ik, jn)
    return pl.pallas_call(
        body, name=name, grid=(k // tk, n // tn, nsteps),
        in_specs=[pl.BlockSpec((tp, tk), lambda ik, jn, ip: (ip, ik)),
                  pl.BlockSpec((tp, tn), lambda ik, jn, ip: (ip, jn))],
        out_specs=pl.BlockSpec((span, tk, width), omap),
        out_shape=jax.ShapeDtypeStruct(out_shape, out_dtype),
        scratch_shapes=[pltpu.VMEM((tk, tn), f32)],
        compiler_params=_cparams(("parallel", "parallel", "arbitrary")),
    )(a, b)


def _adamw_math(w, g, m, v):
    mn = ADAM_B1 * m + (1.0 - ADAM_B1) * g
    vn = ADAM_B2 * v + (1.0 - ADAM_B2) * (g * g)
    m_hat = mn / (1.0 - ADAM_B1 ** ADAM_STEP)
    v_hat = vn / (1.0 - ADAM_B2 ** ADAM_STEP)
    return -ADAM_LR * (m_hat / (jnp.sqrt(v_hat) + ADAM_EPS) + ADAM_WD * w), mn, vn


def adamw(w, g, m, v, name):
    r, c = w.shape
    tr = r
    for cand in (256, 176, 152, 128):
        if r > cand and r % cand == 0:
            tr = cand
            break

    def body(w_ref, g_ref, m_ref, v_ref, go_ref, d_ref, mo_ref, vo_ref):
        gg = g_ref[...]
        go_ref[...] = gg
        d_ref[...], mo_ref[...], vo_ref[...] = _adamw_math(w_ref[...], gg, m_ref[...], v_ref[...])

    spec = pl.BlockSpec((tr, c), lambda i: (i, 0))
    sds = jax.ShapeDtypeStruct((r, c), f32)
    return pl.pallas_call(
        body, name=name, grid=(r // tr,), in_specs=[spec] * 4, out_specs=[spec] * 4, out_shape=[sds] * 4,
        compiler_params=_cparams(("parallel",)),
    )(w, g, m, v)


def adamw_small(ws, gs, ms, vs):
    n = len(ws)

    def body(*refs):
        ins, outs = refs[:4 * n], refs[4 * n:]
        for k in range(n):
            outs[k][...], outs[n + k][...], outs[2 * n + k][...] = _adamw_math(
                ins[k][...], ins[n + k][...], ins[2 * n + k][...], ins[3 * n + k][...])

    out = pl.pallas_call(body, name="adamw_small",
                         out_shape=[jax.ShapeDtypeStruct(w.shape, f32) for w in ws] * 3)(*ws, *gs, *ms, *vs)
    return out[:n], out[n:2 * n], out[2 * n:]


def _me():
    return lax.axis_index("x"), lax.axis_index("y"), lax.axis_index("c")


def _chip_peer(x, y, c, k):
    return (x ^ (k >> 1), y ^ (k & 1), c)


ANY = pl.BlockSpec(memory_space=pl.ANY)


def gather_weights(now, later):
    n, n_later = len(now), len(later)
    out_dtypes = [bf16 if s.size > 16 * 256 else f32 for s in now]
    halves = [(2, s.shape[0] // 2, s.shape[1]) for s in now]

    def body(*refs):
        ins, later_ins = refs[:n], refs[n:n + n_later]
        outs, later_outs = refs[n + n_later:2 * n + n_later], refs[2 * n + n_later:2 * (n + n_later)]
        stage = refs[2 * (n + n_later):3 * n + 2 * n_later]
        send_sems, recv_sems, pass_send_sems, pass_recv_sems, local_sems = refs[3 * n + 2 * n_later:]
        x, y, c = _me()
        j = 2 * x + y
        sibling = (x, y, 1 - c)

        def over_ici(w, k, slot):
            return pltpu.make_async_remote_copy(
                src_ref=stage[w].at[c], dst_ref=outs[w].at[slot, c], send_sem=send_sems.at[w, k - 1],
                recv_sem=recv_sems.at[w, k - 1], device_id=_chip_peer(x, y, c, k), device_id_type=MESH)

        def passed_on(w, k, half):
            return pltpu.make_async_remote_copy(
                src_ref=outs[w].at[j ^ k, half], dst_ref=outs[w].at[j ^ k, half], send_sem=pass_send_sems.at[w, k - 1],
                recv_sem=pass_recv_sems.at[w, k - 1], device_id=sibling, device_id_type=MESH)

        for w in range(n):
            stage[w][...] = ins[w][...].astype(out_dtypes[w]).reshape(halves[w])
        locs = []
        for w in range(n):
            loc = pltpu.make_async_copy(stage[w], outs[w].at[j], local_sems.at[w])
            loc.start()
            locs.append(loc)
            for k in (1, 2, 3):
                over_ici(w, k, j).start()
        for w in range(n_later):
            later_outs[w][...] = later_ins[w][...].astype(bf16)
        for w in range(n):
            for k in (1, 2, 3):
                over_ici(w, k, j ^ k).wait_recv()
                passed_on(w, k, c).start()
        for w in range(n):
            for k in (1, 2, 3):
                passed_on(w, k, 1 - c).wait_recv()
        for w in range(n):
            for k in (1, 2, 3):
                over_ici(w, k, j).wait_send()
                passed_on(w, k, c).wait_send()
        for loc in locs:
            loc.wait()

    vmem = pl.BlockSpec(memory_space=pltpu.VMEM)
    sem3 = pltpu.SemaphoreType.DMA((n, 3))
    return pl.pallas_call(
        body, name="gather_weights",
        in_specs=[vmem] * (n + n_later), out_specs=[ANY] * n + [vmem] * n_later,
        out_shape=[jax.ShapeDtypeStruct((N_SHARD,) + h, dt) for h, dt in zip(halves, out_dtypes)]
        + [jax.ShapeDtypeStruct(s.shape, bf16) for s in later],
        scratch_shapes=[pltpu.VMEM(h, dt) for h, dt in zip(halves, out_dtypes)]
        + [sem3, sem3, sem3, sem3, pltpu.SemaphoreType.DMA((n,))],
        compiler_params=pltpu.CompilerParams(vmem_limit_bytes=VMEM_LIMIT),
    )(*now, *later)


def _shard_push(srcs, dsts, send_sems, recv_sems, local_sems):
    def remote(w, k, slot):
        x, y, c = _me()
        return pltpu.make_async_remote_copy(
            src_ref=srcs[w], dst_ref=dsts[w].at[slot], send_sem=send_sems.at[w, k - 1],
            recv_sem=recv_sems.at[w, k - 1], device_id=_chip_peer(x, y, c, k), device_id_type=MESH)

    def local(w):
        x, y, _ = _me()
        return pltpu.make_async_copy(srcs[w], dsts[w].at[2 * x + y], local_sems.at[w])

    def start():
        x, y, _ = _me()
        for w in range(len(srcs)):
            local(w).start()
            for k in (1, 2, 3):
                remote(w, k, 2 * x + y).start()

    def wait():
        x, y, _ = _me()
        for w in range(len(srcs)):
            for k in (1, 2, 3):
                remote(w, k, (2 * x + y) ^ k).wait_recv()
        for w in range(len(srcs)):
            for k in (1, 2, 3):
                remote(w, k, 2 * x + y).wait_send()
            local(w).wait()

    return start, wait


def _grad_push(srcs, dsts, send_sems, recv_sems):
    def copy(w, k):
        x, y, c = _me()
        px, py, pc = x ^ (k >> 2), y ^ ((k >> 1) & 1), c ^ (k & 1)
        return pltpu.make_async_remote_copy(
            src_ref=srcs[w].at[2 * px + py, pc], dst_ref=dsts[w].at[k - 1], send_sem=send_sems.at[w, k - 1],
            recv_sem=recv_sems.at[w, k - 1], device_id=(px, py, pc), device_id_type=MESH)

    def start():
        for w in range(len(srcs)):
            for k in range(1, N_DEV):
                copy(w, k).start()

    def wait():
        for w in range(len(srcs)):
            for k in range(1, N_DEV):
                copy(w, k).wait_recv()
        for w in range(len(srcs)):
            for k in range(1, N_DEV):
                copy(w, k).wait_send()

    return start, wait


def _grad_push_specs(grads):
    n = len(grads)
    return ([ANY] * n, [ANY] * n, [jax.ShapeDtypeStruct((N_DEV - 1,) + g.shape[2:], g.dtype) for g in grads],
            [pltpu.SemaphoreType.DMA((n, N_DEV - 1)), pltpu.SemaphoreType.DMA((n, N_DEV - 1))])


def add_eight(own, parts, jc_idx, name):
    _, half, c = parts.shape
    tr = half // 2 if (half // 2) % 16 == 0 else half

    def body(jc_ref, own_ref, p_ref, out_ref):
        acc = own_ref[0, 0].astype(f32)
        for k in range(N_DEV - 1):
            acc = acc + p_ref[k].astype(f32)
        out_ref[0] = acc

    return pl.pallas_call(
        body, name=name,
        grid_spec=pltpu.PrefetchScalarGridSpec(
            num_scalar_prefetch=1, grid=(half // tr,),
            in_specs=[pl.BlockSpec((1, 1, tr, c), lambda t, jc: (jc[0], jc[1], t, 0)),
                      pl.BlockSpec((N_DEV - 1, tr, c), lambda t, jc: (0, t, 0))],
            out_specs=pl.BlockSpec((1, tr, c), lambda t, jc: (jc[1], t, 0))),
        out_shape=jax.ShapeDtypeStruct((2, half, c), f32),
        compiler_params=_cparams(("parallel",)),
    )(jc_idx, own, parts)


def _push_specs(shards):
    n = len(shards)
    return ([ANY] * n, [ANY] * n, [jax.ShapeDtypeStruct((N_SHARD,) + s.shape, s.dtype) for s in shards],
            [pltpu.SemaphoreType.DMA((n, 3)), pltpu.SemaphoreType.DMA((n, 3)), pltpu.SemaphoreType.DMA((n,))])


def pair_exchange_halves(grads, small):
    n = len(grads)

    def body(*refs):
        ins, small_ref = refs[:n], refs[n]
        outs, gath = refs[n + 1:2 * n + 1], refs[2 * n + 1]
        send_sems, recv_sems, s_send, s_recv, local_sem = refs[2 * n + 2:]
        x, y, c = _me()
        me = 4 * x + 2 * y + c
        sends = []
        for w in range(n):
            half = ins[w].shape[1] // 2
            cp = pltpu.make_async_remote_copy(
                src_ref=ins[w].at[:, pl.ds((1 - c) * half, half), :], dst_ref=outs[w],
                send_sem=send_sems.at[w], recv_sem=recv_sems.at[w], device_id=(x, y, 1 - c), device_id_type=MESH)
            cp.start()
            sends.append(cp)
        loc = pltpu.make_async_copy(small_ref, gath.at[me], local_sem)
        loc.start()
        for k in range(1, N_DEV):
            cp = pltpu.make_async_remote_copy(
                src_ref=small_ref, dst_ref=gath.at[me], send_sem=s_send.at[k - 1], recv_sem=s_recv.at[k - 1],
                device_id=(x ^ (k >> 2), y ^ ((k >> 1) & 1), c ^ (k & 1)), device_id_type=MESH)
            cp.start()
            sends.append(cp)
        for w in range(n):
            half = ins[w].shape[1] // 2
            pltpu.make_async_remote_copy(
                src_ref=ins[w].at[:, pl.ds(0, half), :], dst_ref=outs[w], send_sem=send_sems.at[w],
                recv_sem=recv_sems.at[w], device_id=(x, y, 1 - c), device_id_type=MESH).wait_recv()
        for k in range(1, N_DEV):
            pltpu.make_async_remote_copy(
                src_ref=small_ref, dst_ref=gath.at[me ^ k], send_sem=s_send.at[k - 1], recv_sem=s_recv.at[k - 1],
                device_id=(x ^ (k >> 2), y ^ ((k >> 1) & 1), c ^ (k & 1)), device_id_type=MESH).wait_recv()
        for cp in sends:
            cp.wait_send()
        loc.wait()

    return pl.pallas_call(
        body, name="pair_exchange_halves", in_specs=[ANY] * (n + 1), out_specs=[ANY] * (n + 1),
        out_shape=[jax.ShapeDtypeStruct((g.shape[0], g.shape[1] // 2, g.shape[2]), f32) for g in grads]
        + [jax.ShapeDtypeStruct((N_DEV,) + small.shape, f32)],
        scratch_shapes=[pltpu.SemaphoreType.DMA((n,)), pltpu.SemaphoreType.DMA((n,)),
                        pltpu.SemaphoreType.DMA((N_DEV - 1,)), pltpu.SemaphoreType.DMA((N_DEV - 1,)),
                        pltpu.SemaphoreType.DMA],
    )(*grads, small)


def chip_exchange(sums):
    n = len(sums)

    def body(*refs):
        ins, outs = refs[:n], refs[n:2 * n]
        send_sems, recv_sems = refs[2 * n:]
        x, y, c = _me()
        j = 2 * x + y
        sends = []
        for w in range(n):
            for k in (1, 2, 3):
                cp = pltpu.make_async_remote_copy(
                    src_ref=ins[w].at[j ^ k], dst_ref=outs[w].at[k - 1], send_sem=send_sems.at[w, k - 1],
                    recv_sem=recv_sems.at[w, k - 1], device_id=_chip_peer(x, y, c, k), device_id_type=MESH)
                cp.start()
                sends.append(cp)
        for w in range(n):
            for k in (1, 2, 3):
                pltpu.make_async_remote_copy(
                    src_ref=ins[w].at[0], dst_ref=outs[w].at[k - 1], send_sem=send_sems.at[w, k - 1],
                    recv_sem=recv_sems.at[w, k - 1], device_id=_chip_peer(x, y, c, k), device_id_type=MESH).wait_recv()
        for cp in sends:
            cp.wait_send()

    return pl.pallas_call(
        body, name="chip_exchange", in_specs=[ANY] * n, out_specs=[ANY] * n,
        out_shape=[jax.ShapeDtypeStruct((N_SHARD - 1,) + s.shape[1:], s.dtype) for s in sums],
        scratch_shapes=[pltpu.SemaphoreType.DMA((n, 3)), pltpu.SemaphoreType.DMA((n, 3))],
    )(*sums)


def pair_exchange_results(halves):
    n = len(halves)

    def body(*refs):
        ins, outs = refs[:n], refs[n:2 * n]
        send_sems, recv_sems = refs[2 * n:]
        x, y, c = _me()
        sends = []
        for w in range(n):
            cp = pltpu.make_async_remote_copy(
                src_ref=ins[w].at[c], dst_ref=outs[w].at[c], send_sem=send_sems.at[w], recv_sem=recv_sems.at[w],
                device_id=(x, y, 1 - c), device_id_type=MESH)
            cp.start()
            sends.append(cp)
        for w in range(n):
            pltpu.make_async_remote_copy(
                src_ref=ins[w].at[c], dst_ref=outs[w].at[1 - c], send_sem=send_sems.at[w],
                recv_sem=recv_sems.at[w], device_id=(x, y, 1 - c), device_id_type=MESH).wait_recv()
        for cp in sends:
            cp.wait_send()

    return pl.pallas_call(
        body, name="pair_exchange_results", in_specs=[ANY] * n, out_specs=[ANY] * n,
        out_shape=[jax.ShapeDtypeStruct(h.shape, f32) for h in halves],
        input_output_aliases={w: w for w in range(n)},
        scratch_shapes=[pltpu.SemaphoreType.DMA((n,)), pltpu.SemaphoreType.DMA((n,))],
    )(*halves)


def add_pair(grad, other, c_idx, name):
    _, r, c = grad.shape
    half = r // 2
    tr = half // 2 if (half // 2) % 8 == 0 else half
    per = half // tr

    def body(c_ref, g_ref, o_ref, out_ref):
        out_ref[...] = (g_ref[...] + o_ref[...]).astype(bf16)

    return pl.pallas_call(
        body, name=name,
        grid_spec=pltpu.PrefetchScalarGridSpec(
            num_scalar_prefetch=1, grid=(N_SHARD, per),
            in_specs=[pl.BlockSpec((1, tr, c), lambda j, t, cr: (j, cr[0] * per + t, 0)),
                      pl.BlockSpec((1, tr, c), lambda j, t, cr: (j, t, 0))],
            out_specs=pl.BlockSpec((1, tr, c), lambda j, t, cr: (j, t, 0))),
        out_shape=jax.ShapeDtypeStruct((N_SHARD, half, c), bf16),
        compiler_params=_cparams(("parallel", "parallel")),
    )(c_idx, grad, other)


def add_four(own, parts, jc_idx, name):
    _, half, c = parts.shape
    tr = half // 2 if (half // 2) % 8 == 0 else half

    def body(jc_ref, own_ref, p_ref, out_ref):
        acc = own_ref[0].astype(f32)
        for k in range(N_SHARD - 1):
            acc = acc + p_ref[k].astype(f32)
        out_ref[0] = acc

    return pl.pallas_call(
        body, name=name,
        grid_spec=pltpu.PrefetchScalarGridSpec(
            num_scalar_prefetch=1, grid=(half // tr,),
            in_specs=[pl.BlockSpec((1, tr, c), lambda t, jc: (jc[0], t, 0)),
                      pl.BlockSpec((N_SHARD - 1, tr, c), lambda t, jc: (0, t, 0))],
            out_specs=pl.BlockSpec((1, tr, c), lambda t, jc: (jc[1], t, 0))),
        out_shape=jax.ShapeDtypeStruct((2, half, c), f32),
        compiler_params=_cparams(("parallel",)),
    )(jc_idx, own, parts)


def sum_devices(gathered):
    def body(g_ref, out_ref):
        acc = g_ref[0]
        for d in range(1, N_DEV):
            acc = acc + g_ref[d]
        out_ref[...] = acc

    return pl.pallas_call(body, name="sum_devices", out_shape=jax.ShapeDtypeStruct(gathered.shape[1:], f32))(gathered)


def _rows128(a, rows):
    flat = a.reshape(-1, BLOCK) if a.size % BLOCK == 0 else jnp.pad(a.reshape(1, -1), ((0, 0), (0, BLOCK - a.size)))
    return jnp.pad(flat, ((0, rows - flat.shape[0]), (0, 0)))


def kernel(x, meta_tokens, ln_emb_g, ln_emb_b, w_in, hg_lower_bounds, hg_norm_g, attn_sinks, w_branch_hg, w_branch_attn, w_out, ln1_g, ln1_b, w_ffn_in, w_ffn_out, ln2_g, ln2_b, loss_target, m_meta_tokens, m_ln_emb_g, m_ln_emb_b, m_w_in, m_hg_lower_bounds, m_hg_norm_g, m_attn_sinks, m_w_branch_hg, m_w_branch_attn, m_w_out, m_ln1_g, m_ln1_b, m_w_ffn_in, m_w_ffn_out, m_ln2_g, m_ln2_b, v_meta_tokens, v_ln_emb_g, v_ln_emb_b, v_w_in, v_hg_lower_bounds, v_hg_norm_g, v_attn_sinks, v_w_branch_hg, v_w_branch_attn, v_w_out, v_ln1_g, v_ln1_b, v_w_ffn_in, v_w_ffn_out, v_ln2_g, v_ln2_b):
    seq = x.shape[1]
    nb = seq // BLOCK + 1
    xs = x[0]
    ts = loss_target[0]
    ix, iy, ic = _me()
    shard = 2 * ix + iy
    vec = lambda a: a.reshape(1, D_MODEL)

    w_in_t = jnp.swapaxes(w_in[0], 0, 1)
    g_in, g_meta, s_bh, s_ba, s_out, s_fi, s_fo = gather_weights(
        [w_in_t, meta_tokens], [w_branch_hg[0], w_branch_attn[0], w_out[0], w_ffn_in[0], w_ffn_out[0]])
    by_cols = lambda g: g.reshape(N_SHARD, -1, g.shape[-1]).transpose(1, 0, 2).reshape(-1, N_SHARD * g.shape[-1])
    wf_in = g_in.reshape(IN_W, D_MODEL)
    metablk = jnp.pad(by_cols(g_meta), ((TM - N_META, 0), (0, 0)))

    pos = jnp.arange((nb + LEAD) * BLOCK, dtype=jnp.int32) - (LEAD * BLOCK + PAD)
    half = HEAD_DIM // 2
    inv = ROPE_THETA ** (-jnp.arange(half, dtype=f32) / half)
    ang = pos.astype(f32)[:, None] * inv[None, :]
    cos = jnp.tile(jnp.cos(ang), (1, BLOCK // half))
    sin = jnp.tile(jnp.sin(ang), (1, BLOCK // half))
    sinks8 = jnp.broadcast_to(attn_sinks.reshape(ATT_HEADS, 1), (ATT_HEADS, BLOCK))
    ng = hg_norm_g.reshape(1, HG_K)

    h0, h0b, pa, pg = emb_inproj(xs, metablk, vec(ln_emb_g), vec(ln_emb_b), wf_in, cos, sin)
    og, sprev, g_fi = hgrn_fwd(pa, hg_lower_bounds, ng, nb, [s_fi])
    oatt, g_fo, g_out, g_bh, g_ba = attn_fwd(pa, sinks8, nb, [s_fo, s_out, s_bh, s_ba])
    wf_bh, wf_ba, wf_fi = by_cols(g_bh), by_cols(g_ba), by_cols(g_fi)
    wf_out = g_out.reshape(D_MODEL, D_MODEL)
    wf_fo = g_fo.reshape(D_FF, D_MODEL)
    h1, h1b, mixin, og, oatt = mid_front(h0, pg, og, oatt, wf_bh, wf_ba, wf_out, ln1_g, ln1_b)
    dh1, dau, sact, dr2, loss_part, dg2, db2 = mid_ffn(h1, ts, wf_fi, wf_fo, ln2_g, ln2_b)
    dh0p, dpg, dog, doa, dyh, dya, dr1, dg1, db1 = mid_back(dh1, h0, pg, og, oatt, wf_bh, wf_ba, wf_out, ln1_g, ln1_b)
    tp = max(t for t in (1408, 768, 512, TM) if h0.shape[0] % t == 0)
    pieces = lambda g: g.reshape(N_SHARD, 2, -1, g.shape[-1])
    gb_bh = pieces(wgrad(og, dyh, "wgrad_bh", 512, D_MODEL, tp, True, bf16))
    gb_ba = pieces(wgrad(oatt, dya, "wgrad_ba", 512, D_MODEL, tp, True, bf16))
    gb_out = pieces(wgrad(mixin, dr1, "wgrad_out", D_MODEL, D_MODEL, tp, False, bf16))
    gb_fi = pieces(wgrad(h1b, dau, "wgrad_fi", D_MODEL, D_FF, tp, True, bf16))
    gb_fo = pieces(wgrad(sact, dr2, "wgrad_fo", D_FF // 2, D_MODEL, tp, False, bf16))
    dhq, dhf, dhi, dhg, dlb4, dng, r_fi, r_fo = hgrn_bwd(pa, hg_lower_bounds, ng, sprev, dog, nb, [gb_fi, gb_fo])
    daq, dkc, dkp, dvc, dvp, dkm, dvm, dsk, r_out, r_bh, r_ba = attn_bwd(pa, sinks8, doa, nb,
                                                                         [gb_out, gb_bh, gb_ba])
    dproj, dx, dmeta, dlg, dlb = inproj_bwd(dh0p, dhq, dhf, dhi, dhg, daq, dkc, dkp, dvc, dvp, dkm, dvm, dpg,
                                      wf_in, xs, metablk, vec(ln_emb_g), vec(ln_emb_b), cos, sin)
    tp_in = max(t for t in (768, 512, TM) if h0.shape[0] % t == 0)
    gw_in = wgrad(dproj, h0b, "wgrad_in", IN_W // 2, D_MODEL, tp_in, False).reshape(N_SHARD, -1, D_MODEL)

    parts = [(dlg, 8), (dlb, 8), (dlb4, 8), (dng, 8), (dsk[:, 0], 8),
             (dg1, 8), (db1, 8), (dg2, 8), (db2, 8), (dmeta, BLOCK), (loss_part, 8)]
    small = jnp.concatenate([_rows128(a, r) for a, r in parts], axis=0)

    c_idx = jnp.reshape(ic, (1,)).astype(jnp.int32)
    jc_idx = jnp.stack([shard, ic]).astype(jnp.int32)
    other_in, gathered = pair_exchange_halves([gw_in], small)
    sum_in = add_pair(gw_in, other_in, c_idx, "add_pair_in")
    quad_in, = chip_exchange([sum_in])
    halves = [add_four(sum_in, quad_in, jc_idx, "add_four_in")]
    halves += [add_eight(g, r, jc_idx, "add_eight_" + nm) for nm, g, r in
               (("bh", gb_bh, r_bh), ("ba", gb_ba, r_ba), ("out", gb_out, r_out), ("fi", gb_fi, r_fi),
                ("fo", gb_fo, r_fo))]
    red = [r.reshape(-1, r.shape[-1]) for r in pair_exchange_results(halves)]
    small_sum = sum_devices(gathered)

    offs, acc = [], 0
    for _, r in parts:
        offs.append(acc)
        acc += r
    take = lambda n, size: small_sum[offs[n]:offs[n] + parts[n][1]].reshape(-1)[:size]
    g_meta_full = take(9, N_META * D_MODEL).reshape(N_META, D_MODEL)
    g_small = {
        "meta_tokens": lax.dynamic_slice_in_dim(g_meta_full, shard * (D_MODEL // N_SHARD), D_MODEL // N_SHARD, axis=1),
        "ln_emb_g": take(0, D_MODEL), "ln_emb_b": take(1, D_MODEL),
        "hg_lower_bounds": take(2, 2 * HG_HEADS * HG_K).reshape(2, HG_HEADS * HG_K),
        "hg_norm_g": take(3, HG_K).reshape(1, HG_K), "attn_sinks": take(4, ATT_HEADS).reshape(1, ATT_HEADS),
        "ln1_g": take(5, D_MODEL).reshape(1, D_MODEL), "ln1_b": take(6, D_MODEL).reshape(1, D_MODEL),
        "ln2_g": take(7, D_MODEL).reshape(1, D_MODEL), "ln2_b": take(8, D_MODEL).reshape(1, D_MODEL),
    }
    g_big = {"w_in": red[0], "w_branch_hg": red[1], "w_branch_attn": red[2], "w_out": red[3],
             "w_ffn_in": red[4], "w_ffn_out": red[5]}

    names = ["meta_tokens", "ln_emb_g", "ln_emb_b", "w_in", "hg_lower_bounds", "hg_norm_g", "attn_sinks",
             "w_branch_hg", "w_branch_attn", "w_out", "ln1_g", "ln1_b", "w_ffn_in", "w_ffn_out", "ln2_g", "ln2_b"]
    given = dict(
        meta_tokens=(meta_tokens, m_meta_tokens, v_meta_tokens), ln_emb_g=(ln_emb_g, m_ln_emb_g, v_ln_emb_g),
        ln_emb_b=(ln_emb_b, m_ln_emb_b, v_ln_emb_b), w_in=(w_in, m_w_in, v_w_in),
        hg_lower_bounds=(hg_lower_bounds, m_hg_lower_bounds, v_hg_lower_bounds),
        hg_norm_g=(hg_norm_g, m_hg_norm_g, v_hg_norm_g), attn_sinks=(attn_sinks, m_attn_sinks, v_attn_sinks),
        w_branch_hg=(w_branch_hg, m_w_branch_hg, v_w_branch_hg),
        w_branch_attn=(w_branch_attn, m_w_branch_attn, v_w_branch_attn), w_out=(w_out, m_w_out, v_w_out),
        ln1_g=(ln1_g, m_ln1_g, v_ln1_g), ln1_b=(ln1_b, m_ln1_b, v_ln1_b), w_ffn_in=(w_ffn_in, m_w_ffn_in, v_w_ffn_in),
        w_ffn_out=(w_ffn_out, m_w_ffn_out, v_w_ffn_out), ln2_g=(ln2_g, m_ln2_g, v_ln2_g), ln2_b=(ln2_b, m_ln2_b, v_ln2_b))
    two_d = lambda a: a.reshape(8, BLOCK) if a.ndim == 1 else a.reshape(a.shape[-2], a.shape[-1])
    small_names = [nm for nm in names if nm not in g_big]
    small_d, small_m, small_v = adamw_small([two_d(given[nm][0]) for nm in small_names],
                                            [two_d(g_small[nm]) for nm in small_names],
                                            [two_d(given[nm][1]) for nm in small_names],
                                            [two_d(given[nm][2]) for nm in small_names])
    out_g, out_d, out_m, out_v = [], [], [], []
    for nm in names:
        w, m, v = given[nm]
        shape = w.shape
        if nm == "w_in":
            t = lambda a: jnp.swapaxes(two_d(a), 0, 1)
            g, d, mn, vn = [t(a) for a in adamw(t(w), g_big[nm], t(m), t(v), "adamw_" + nm)]
        elif nm in g_big:
            g, d, mn, vn = adamw(two_d(w), g_big[nm], two_d(m), two_d(v), "adamw_" + nm)
        else:
            k = small_names.index(nm)
            g, d, mn, vn = g_small[nm], small_d[k], small_m[k], small_v[k]
        out_g.append(g.reshape(shape))
        out_d.append(d.reshape(shape))
        out_m.append(mn.reshape(shape))
        out_v.append(vn.reshape(shape))

    loss = take(10, 1)[0]
    grad_x = dx.reshape(x.shape)
    return (loss, grad_x, *out_g, *out_d, *out_m, *out_v)
```

```python
import functools

import jax
import jax.numpy as jnp
from jax import lax
from jax.experimental import pallas as pl
from jax.experimental.pallas import tpu as pltpu

f32 = jnp.float32
bf16 = jnp.bfloat16

D_MODEL = 1024
BLOCK = 128
N_META = 16
PAD = BLOCK - N_META
HG_HEADS = 4
HG_K = 128
SUB = 16
ATT_HEADS = 8
HEAD_DIM = 64
ATT_QW = ATT_HEADS * HEAD_DIM
D_FF = 2816
EPS = 1e-5
ALPHA = 2.0 ** 0.25
ROPE_THETA = 10000.0
N_A = 2816
N_G = 2048
IN_W = N_A + N_G
N_SHARD = 4
N_DEV = 8

ADAM_LR = 0.001
ADAM_B1 = 0.9
ADAM_B2 = 0.999
ADAM_EPS = 1e-08
ADAM_WD = 0.01
ADAM_STEP = 10

TM = 256
LEAD = TM // BLOCK - 1

VMEM_LIMIT = 56 * 1024 * 1024
MESH = pl.DeviceIdType.MESH


def _cparams(sem, vmem=VMEM_LIMIT):
    return pltpu.CompilerParams(dimension_semantics=sem, vmem_limit_bytes=vmem)


def _const_spec(shape):
    zeros = (0,) * len(shape)
    return pl.BlockSpec(shape, lambda *_: zeros, pipeline_mode=pl.Buffered(1))


def _dot(a, b, ca, cb):
    return lax.dot_general(a.astype(bf16), b.astype(bf16), (((ca,), (cb,)), ((), ())),
                           preferred_element_type=f32)


@jax.custom_vjp
def mm(a, b):
    return _dot(a, b, 1, 0)


mm.defvjp(lambda a, b: (_dot(a, b, 1, 0), (a, b)),
          lambda r, g: (_dot(g, r[1], 1, 1), _dot(r[0], g, 0, 0)))


@jax.custom_vjp
def mm_nt(a, b):
    return _dot(a, b, 1, 1)


mm_nt.defvjp(lambda a, b: (_dot(a, b, 1, 1), (a, b)),
             lambda r, g: (_dot(g, r[1], 1, 0), _dot(g, r[0], 0, 0)))


@jax.custom_vjp
def mm_tn(a, b):
    return _dot(a, b, 0, 0)


mm_tn.defvjp(lambda a, b: (_dot(a, b, 0, 0), (a, b)),
             lambda r, g: (_dot(r[1], g, 1, 1), _dot(r[0], g, 1, 0)))


@functools.partial(jax.custom_vjp, nondiff_argnums=(1,))
def roll_lanes(x, shift):
    return pltpu.roll(x, shift, 1)


roll_lanes.defvjp(lambda x, shift: (pltpu.roll(x, shift, 1), None),
                  lambda shift, _, g: (pltpu.roll(g, (128 - shift) % 128, 1),))


@jax.custom_vjp
def _sigmoid(x):
    return 1.0 / (1.0 + jnp.exp(-x))


def _sigmoid_fwd(x):
    s = 1.0 / (1.0 + jnp.exp(-x))
    return s, s


_sigmoid.defvjp(_sigmoid_fwd, lambda s, g: (g * s * (1.0 - s),))


@jax.custom_vjp
def _recip(x):
    return 1.0 / x


def _recip_fwd(x):
    r = 1.0 / x
    return r, r


_recip.defvjp(_recip_fwd, lambda r, g: (-g * r * r,))


def _ln_stats(x):
    mu = jnp.mean(x, axis=-1, keepdims=True)
    xc = x - mu
    var = jnp.mean(xc * xc, axis=-1, keepdims=True)
    rs = lax.rsqrt(var + EPS)
    return xc * rs, rs


def _ln_bwd(dy, xh, rs, g):
    dxh = dy * g
    m1 = jnp.mean(dxh, axis=-1, keepdims=True)
    m2 = jnp.mean(dxh * xh, axis=-1, keepdims=True)
    return rs * (dxh - m1 - xh * m2)


def _row_ids(i):
    return i * BLOCK + lax.broadcasted_iota(jnp.int32, (BLOCK, 1), 0)


def _tm_rows(i):
    return i * TM + lax.broadcasted_iota(jnp.int32, (TM, 1), 0)


def _tm_row(n):
    return pl.BlockSpec((TM, n), lambda i: (i, 0))


def _tm_tokens():
    return pl.BlockSpec((TM, D_MODEL), lambda i: (jnp.maximum(i - 1, 0), 0))


Q_COL, V_COL = 4 * HG_HEADS * HG_K, N_A - BLOCK


def emb_inproj(x, metablk, g, b, w_in, cos, sin):
    nsteps = x.shape[0] // TM + 1

    def body(x_ref, mb_ref, g_ref, b_ref, w_ref, cos_ref, sin_ref, h0_ref, h0b_ref, pa_ref, pg_ref):
        i = pl.program_id(0)
        xb = jnp.where(i == 0, mb_ref[...], x_ref[...])
        xh, _ = _ln_stats(xb)
        y = xh * g_ref[...] + b_ref[...]
        y = jnp.where(_tm_rows(i) >= TM - N_META, y, 0.0)
        h0_ref[...] = y
        yb = y.astype(bf16)
        h0b_ref[...] = yb
        pa = _dot(yb, w_ref[:N_A, :], 1, 1)
        cos, sin = cos_ref[...], sin_ref[...]
        pa_ref[:, :Q_COL] = pa[:, :Q_COL]
        for c0 in range(Q_COL, V_COL, BLOCK):
            pa_ref[:, c0:c0 + BLOCK] = _rope(pa[:, c0:c0 + BLOCK], cos, sin)
        pa_ref[:, V_COL:] = pa[:, V_COL:]
        pg_ref[...] = _dot(yb, w_ref[N_A:, :], 1, 1)

    p = nsteps * TM
    row = _tm_row
    return pl.pallas_call(
        body, name="emb_inproj", grid=(nsteps,),
        in_specs=[_tm_tokens(),
                  _const_spec((TM, D_MODEL)), _const_spec((1, D_MODEL)), _const_spec((1, D_MODEL)),
                  _const_spec((IN_W, D_MODEL)), _tm_row(BLOCK), _tm_row(BLOCK)],
        out_specs=[row(D_MODEL), row(D_MODEL), row(N_A), row(N_G)],
        out_shape=[jax.ShapeDtypeStruct((p, D_MODEL), f32), jax.ShapeDtypeStruct((p, D_MODEL), bf16),
                   jax.ShapeDtypeStruct((p, N_A), f32), jax.ShapeDtypeStruct((p, N_G), f32)],
        compiler_params=_cparams(("parallel",)),
    )(x, metablk, g, b, w_in, cos, sin)


def _hgrn_chunk(valid, st, hq, hf, hi, hg, lbraw, ng):
    lb = _sigmoid(lbraw[0:1] - lbraw[1:2])
    q = hq * _sigmoid(hq)
    fg = lb + (1.0 - lb) * _sigmoid(hf)
    logf = jnp.where(valid, jnp.log(fg), 0.0)
    k = jnp.where(valid, 1.0 - fg, 0.0)
    v = hi
    r = lax.broadcasted_iota(jnp.int32, (BLOCK, BLOCK), 0)
    c = lax.broadcasted_iota(jnp.int32, (BLOCK, BLOCK), 1)
    tril = (c <= r).astype(f32)
    bcum = jnp.dot(tril, logf, precision=lax.Precision.HIGHEST, preferred_element_type=f32)
    blast = bcum[BLOCK - 1:BLOCK]
    rows = lax.broadcasted_iota(jnp.int32, (BLOCK, 1), 0)
    sub8 = lax.broadcasted_iota(jnp.int32, (BLOCK // 8, 8, HG_K), 1)
    b8 = bcum.reshape(BLOCK // 8, 8, HG_K)
    row_of_8 = lambda j: jnp.broadcast_to(b8[:, j:j + 1, :], b8.shape)
    a = jnp.where(r == c, jnp.sum(q * k, axis=-1, keepdims=True), 0.0)
    seg = BLOCK
    while seg >= 2:
        half = seg // 2
        if seg >= 8:
            bs = bcum.reshape(BLOCK // seg, seg, HG_K)
            ref = jnp.broadcast_to(bs[:, half - 1:half, :], bs.shape)
        elif seg == 4:
            ref = jnp.where(sub8 < 4, row_of_8(1), row_of_8(5))
        else:
            ref = jnp.where(sub8 < 2, row_of_8(0), jnp.where(sub8 < 4, row_of_8(2),
                                                             jnp.where(sub8 < 6, row_of_8(4), row_of_8(6))))
        ref = ref.reshape(BLOCK, HG_K)
        upper = (rows % seg) >= half
        q_up = q * jnp.exp(jnp.where(upper, bcum - ref, -jnp.inf))
        k_lo = k * jnp.exp(jnp.where(upper, -jnp.inf, ref - bcum))
        a = a + jnp.where((r // seg) == (c // seg), mm_nt(q_up, k_lo), 0.0)
        seg = half
    o = mm_nt(q * jnp.exp(bcum), st) + mm(a, v)
    st_new = st * jnp.exp(blast) + mm_tn(v, k * jnp.exp(blast - bcum))
    on = o * lax.rsqrt(jnp.mean(o * o, axis=-1, keepdims=True) + EPS) * ng
    return st_new, on * (hg * _sigmoid(hg))


def _hgrn_in_specs(rowmap):
    wide = lambda col: pl.BlockSpec((BLOCK, HG_HEADS * HG_K), lambda i: (rowmap(i) + LEAD, col))
    return [wide(0), wide(1), wide(2), wide(3), _const_spec((2, HG_HEADS * HG_K)), _const_spec((1, HG_K))]


def _head(ref, h):
    return ref[:, h * HG_K:(h + 1) * HG_K]


def hgrn_fwd(pa, lbraw, ng, nb, shards):
    n = len(shards)

    def body(hq_ref, hf_ref, hi_ref, hg_ref, lb_ref, ng_ref, *rest):
        srcs, (og_ref, sp_ref), dsts = rest[:n], rest[n:n + 2], rest[n + 2:2 * n + 2]
        st_ref = rest[2 * n + 2]
        start, wait = _shard_push(srcs, dsts, *rest[2 * n + 3:])
        i = pl.program_id(0)

        @pl.when(i == 0)
        def _():
            st_ref[...] = jnp.zeros_like(st_ref)
            start()

        @pl.when(i == nb - 1)
        def _():
            wait()

        valid = _row_ids(i) >= PAD
        for h in range(HG_HEADS):
            st = st_ref[h]
            sp_ref[0, h] = st
            st_new, out = _hgrn_chunk(valid, st, _head(hq_ref, h), _head(hf_ref, h), _head(hi_ref, h),
                                      _head(hg_ref, h), _head(lb_ref, h), ng_ref[...])
            st_ref[h] = st_new
            og_ref[:, h * HG_K:(h + 1) * HG_K] = out.astype(bf16)

    p = (nb + LEAD) * BLOCK
    push_in, push_out, push_shape, push_scratch = _push_specs(shards)
    return pl.pallas_call(
        body, name="hgrn_fwd", grid=(nb,),
        in_specs=_hgrn_in_specs(lambda i: i) + push_in,
        out_specs=[pl.BlockSpec((BLOCK, HG_HEADS * HG_K), lambda i: (i + LEAD, 0)),
                   pl.BlockSpec((1, HG_HEADS, HG_K, HG_K), lambda i: (i, 0, 0, 0))] + push_out,
        out_shape=[jax.ShapeDtypeStruct((p, HG_HEADS * HG_K), bf16),
                   jax.ShapeDtypeStruct((nb, HG_HEADS, HG_K, HG_K), f32)] + push_shape,
        scratch_shapes=[pltpu.VMEM((HG_HEADS, HG_K, HG_K), f32)] + push_scratch,
        compiler_params=_cparams(("arbitrary",)),
    )(pa, pa, pa, pa, lbraw, ng, *shards)


def hgrn_bwd(pa, lbraw, ng, sprev, dog, nb, grads):
    n = len(grads)

    def body(hq_ref, hf_ref, hi_ref, hg_ref, lb_ref, ng_ref, sp_ref, do_ref, *rest):
        srcs, rest = rest[:n], rest[n:]
        dq_ref, df_ref, di_ref, dg_ref, dlb_ref, dng_ref = rest[:6]
        dsts, dst_ref = rest[6:6 + n], rest[6 + n]
        start, wait = _grad_push(srcs, dsts, *rest[7 + n:])
        i = pl.program_id(0)

        @pl.when(i == 0)
        def _():
            dst_ref[...] = jnp.zeros_like(dst_ref)
            dlb_ref[...] = jnp.zeros_like(dlb_ref)
            dng_ref[...] = jnp.zeros_like(dng_ref)
            start()

        valid = _row_ids(nb - 1 - i) >= PAD
        dng_sum = jnp.zeros((1, HG_K), f32)
        for h in range(HG_HEADS):
            cols = slice(h * HG_K, (h + 1) * HG_K)
            _, vjp = jax.vjp(functools.partial(_hgrn_chunk, valid), sp_ref[0, h], _head(hq_ref, h), _head(hf_ref, h),
                             _head(hi_ref, h), _head(hg_ref, h), _head(lb_ref, h), ng_ref[...])
            dst, dq, df, di, dg, dlb, dng = vjp((dst_ref[h], _head(do_ref, h)))
            dst_ref[h] = dst
            dq_ref[:, cols] = dq.astype(bf16)
            df_ref[:, cols] = df.astype(bf16)
            di_ref[:, cols] = di.astype(bf16)
            dg_ref[:, cols] = dg.astype(bf16)
            dlb_ref[:, cols] += dlb
            dng_sum = dng_sum + dng
        dng_ref[...] += dng_sum
        pl.when(i == nb - 1)(wait)

    p = (nb + LEAD) * BLOCK
    rev = lambda i: nb - 1 - i
    hw = HG_HEADS * HG_K
    blk = pl.BlockSpec((BLOCK, hw), lambda i: (rev(i) + LEAD, 0))
    wide = jax.ShapeDtypeStruct((p, hw), bf16)
    push_in, push_out, push_shape, push_scratch = _grad_push_specs(grads)
    return pl.pallas_call(
        body, name="hgrn_bwd", grid=(nb,),
        in_specs=_hgrn_in_specs(rev) + [pl.BlockSpec((1, HG_HEADS, HG_K, HG_K), lambda i: (rev(i), 0, 0, 0)), blk]
        + push_in,
        out_specs=[blk, blk, blk, blk, pl.BlockSpec((2, hw), lambda i: (0, 0)), pl.BlockSpec((1, HG_K), lambda i: (0, 0))]
        + push_out,
        out_shape=[wide, wide, wide, wide, jax.ShapeDtypeStruct((2, hw), f32), jax.ShapeDtypeStruct((1, HG_K), f32)]
        + push_shape,
        scratch_shapes=[pltpu.VMEM((HG_HEADS, HG_K, HG_K), f32)] + push_scratch,
        compiler_params=_cparams(("arbitrary",)),
    )(pa, pa, pa, pa, lbraw, ng, sprev, dog, *grads)


def _rot_half(x):
    lane = lax.broadcasted_iota(jnp.int32, x.shape, 1)
    return jnp.where(lane % HEAD_DIM < HEAD_DIM // 2, -pltpu.roll(x, BLOCK - HEAD_DIM // 2, 1),
                     pltpu.roll(x, HEAD_DIM // 2, 1))


def _rope(x, cos, sin):
    return x * cos + _rot_half(x) * sin


def _rope_transposed(g, cos, sin):
    return g * cos - _rot_half(g * sin)


def _both_halves(x, g):
    lo = lax.broadcasted_iota(jnp.int32, x.shape, 1) < HEAD_DIM
    sw = roll_lanes(x, HEAD_DIM)
    return jnp.where(lo, x, sw) if g == 0 else jnp.where(lo, sw, x)


def _attn_block(band_ok, meta_ok, q, kp, kc, vp, vc, km, vm, *sinks):
    neg = jnp.finfo(f32).min
    scale = HEAD_DIM ** -0.5
    group = ATT_HEADS // 2
    lo = lax.broadcasted_iota(jnp.int32, (BLOCK, BLOCK), 1) < HEAD_DIM
    t = lax.broadcasted_iota(jnp.int32, (group * BLOCK, BLOCK), 0) % BLOCK
    col = lax.broadcasted_iota(jnp.int32, (group * BLOCK, BLOCK), 1)
    own = col <= t
    is_sink = col == N_META
    qr = [q[:, m * BLOCK:(m + 1) * BLOCK] for m in range(ATT_HEADS // 2)]
    slabs = []
    for g in range(2):
        kp_g, kc_g, vp_g, vc_g, km_g, vm_g = [_both_halves(a, g) for a in (kp, kc, vp, vc, km, vm)]
        qs = jnp.concatenate([jnp.where(lo if h % 2 == 0 else ~lo, qr[2 * g + h // 2], 0.0) for h in range(group)],
                             axis=0)
        sink = jnp.concatenate([jnp.broadcast_to(sinks[group * g + h], (BLOCK, 1)) for h in range(group)], axis=0)
        sb = jnp.where(band_ok, jnp.where(own, mm_nt(qs, kc_g), mm_nt(qs, kp_g)) * scale, neg)
        no_keys = jnp.zeros((BLOCK - N_META, BLOCK), f32)
        sme = jnp.where(meta_ok, mm_nt(qs, jnp.concatenate([km_g, no_keys], axis=0)) * scale,
                        jnp.where(is_sink, sink, neg))
        mx = lax.stop_gradient(jnp.max(jnp.maximum(sb, sme), axis=-1, keepdims=True))
        eb, em = jnp.exp(sb - mx), jnp.exp(sme - mx)
        inv = _recip(jnp.sum(eb + em, axis=-1, keepdims=True))
        pb = eb * inv
        o = (mm(jnp.where(own, pb, 0.0), vc_g) + mm(jnp.where(own, 0.0, pb), vp_g)
             + mm(em * inv, jnp.concatenate([vm_g, no_keys], axis=0)))
        for m in range(2):
            slabs.append(jnp.where(lo, o[2 * m * BLOCK:(2 * m + 1) * BLOCK], o[(2 * m + 1) * BLOCK:(2 * m + 2) * BLOCK]))
    return jnp.concatenate(slabs, axis=1)


def _attn_masks(i):
    group = ATT_HEADS // 2
    t = lax.broadcasted_iota(jnp.int32, (group * BLOCK, BLOCK), 0) % BLOCK
    s = lax.broadcasted_iota(jnp.int32, (group * BLOCK, BLOCK), 1)
    kpos = jnp.where(s <= t, i * BLOCK - PAD + s, jnp.where(i > 0, (i - 1) * BLOCK - PAD + s, -1))
    band_ok = kpos >= N_META
    qpos = i * BLOCK - PAD + lax.broadcasted_iota(jnp.int32, (group * BLOCK, 1), 0) % BLOCK
    meta_ok = (s < N_META) & (s <= qpos)
    return band_ok, meta_ok


def _attn_in_specs(cur=lambda i: i):
    prev = lambda i: jnp.maximum(cur(i) - 1, 0)
    kcol, vcol = N_A // BLOCK - 2, N_A // BLOCK - 1
    blk = lambda rowmap, col: pl.BlockSpec((BLOCK, BLOCK), lambda i: (rowmap(i) + LEAD, col))
    first = lambda i: 0
    return [pl.BlockSpec((BLOCK, ATT_QW), lambda i: (cur(i) + LEAD, 4)),
            blk(prev, kcol), blk(cur, kcol), blk(prev, vcol), blk(cur, vcol), blk(first, kcol), blk(first, vcol),
            _const_spec((ATT_HEADS, BLOCK))]


def _attn_row(n, cur=lambda i: i):
    return pl.BlockSpec((BLOCK, n), lambda i: (cur(i) + LEAD, 0))


def _attn_operands(q_ref, kp_ref, kc_ref, vp_ref, vc_ref, km_ref, vm_ref, sk_ref):
    args = (q_ref[...], kp_ref[...], kc_ref[...], vp_ref[...], vc_ref[...], km_ref[PAD:, :], vm_ref[PAD:, :])
    sinks = tuple(sk_ref[j:j + 1, 0:1] for j in range(ATT_HEADS))
    return args + sinks


def attn_fwd(pa, sinks8, nb, shards):
    n = len(shards)
    n_in = 8

    def body(*refs):
        srcs, o_ref, dsts = refs[n_in:n_in + n], refs[n_in + n], refs[n_in + n + 1:n_in + 2 * n + 1]
        start, wait = _shard_push(srcs, dsts, *refs[n_in + 2 * n + 1:])
        i = pl.program_id(0)
        pl.when(i == 0)(start)
        band_ok, meta_ok = _attn_masks(i)
        o_ref[...] = _attn_block(band_ok, meta_ok, *_attn_operands(*refs[:n_in])).astype(bf16)
        pl.when(i == nb - 1)(wait)

    push_in, push_out, push_shape, push_scratch = _push_specs(shards)
    return pl.pallas_call(
        body, name="attn_fwd", grid=(nb,), in_specs=_attn_in_specs() + push_in,
        out_specs=[_attn_row(ATT_QW)] + push_out,
        out_shape=[jax.ShapeDtypeStruct(((nb + LEAD) * BLOCK, ATT_QW), bf16)] + push_shape,
        scratch_shapes=push_scratch,
        compiler_params=_cparams(("arbitrary",)),
    )(pa, pa, pa, pa, pa, pa, pa, sinks8, *shards)


def attn_bwd(pa, sinks8, do, nb, grads):
    n = len(grads)

    def body(*refs):
        do_ref, srcs = refs[8], refs[9:9 + n]
        dq_ref, dkc_ref, dkp_ref, dvc_ref, dvp_ref, dkm_ref, dvm_ref, dsk_ref = refs[9 + n:17 + n]
        start, wait = _grad_push(srcs, refs[17 + n:17 + 2 * n], *refs[17 + 2 * n:])
        i = pl.program_id(0)

        @pl.when(i == 0)
        def _():
            dkm_ref[...] = jnp.zeros((N_META, BLOCK), f32)
            dvm_ref[...] = jnp.zeros((N_META, BLOCK), f32)
            dsk_ref[...] = jnp.zeros((ATT_HEADS, BLOCK), f32)
            start()

        band_ok, meta_ok = _attn_masks(i)
        _, vjp = jax.vjp(functools.partial(_attn_block, band_ok, meta_ok), *_attn_operands(*refs[:8]))
        grads = vjp(do_ref[...])
        dq_ref[...] = grads[0]
        dkp_ref[...] = grads[1]
        dkc_ref[...] = grads[2]
        dvp_ref[...] = grads[3]
        dvc_ref[...] = grads[4]
        dkm_ref[...] += grads[5]
        dvm_ref[...] += grads[6]
        for j in range(ATT_HEADS):
            dsk_ref[j:j + 1, :] += jnp.broadcast_to(grads[7 + j], (1, BLOCK))
        pl.when(i == nb - 1)(wait)

    p = (nb + LEAD) * BLOCK
    row = _attn_row(BLOCK)
    const = lambda r: pl.BlockSpec((r, BLOCK), lambda i: (0, 0))
    part = jax.ShapeDtypeStruct((p, BLOCK), f32)
    push_in, push_out, push_shape, push_scratch = _grad_push_specs(grads)
    return pl.pallas_call(
        body, name="attn_bwd", grid=(nb,),
        in_specs=_attn_in_specs() + [_attn_row(ATT_QW)] + push_in,
        out_specs=[_attn_row(ATT_QW), row, row, row, row,
                   const(N_META), const(N_META), const(ATT_HEADS)] + push_out,
        out_shape=[jax.ShapeDtypeStruct((p, ATT_QW), f32), part, part, part, part,
                   jax.ShapeDtypeStruct((N_META, BLOCK), f32), jax.ShapeDtypeStruct((N_META, BLOCK), f32),
                   jax.ShapeDtypeStruct((ATT_HEADS, BLOCK), f32)] + push_shape,
        scratch_shapes=push_scratch,
        compiler_params=_cparams(("arbitrary",)),
    )(pa, pa, pa, pa, pa, pa, pa, sinks8, do, *grads)


def _mid_forward(h0_ref, pg_ref, og, oa, wbh_ref, wba_ref, wo_ref, g1, b1):
    yh = jnp.dot(og, wbh_ref[...], preferred_element_type=f32)
    ya = jnp.dot(oa, wba_ref[...], preferred_element_type=f32)
    gh = _sigmoid(pg_ref[:, :D_MODEL])
    ga = _sigmoid(pg_ref[:, D_MODEL:])
    mixin = (gh * yh + ga * ya).astype(bf16)
    r1 = ALPHA * h0_ref[...] + jnp.dot(mixin, wo_ref[...], preferred_element_type=f32)
    xh1, rs1 = _ln_stats(r1)
    return yh, ya, gh, ga, mixin, xh1, rs1, xh1 * g1 + b1


def _mid_weight_specs():
    hw = HG_HEADS * HG_K
    return [_const_spec((hw, D_MODEL)), _const_spec((ATT_QW, D_MODEL)), _const_spec((D_MODEL, D_MODEL)),
            _const_spec((1, D_MODEL)), _const_spec((1, D_MODEL))]


def mid_front(h0, pg, og, oatt, wbh, wba, wout, ln1g, ln1b):
    def body(h0_ref, pg_ref, og_ref, oa_ref, wbh_ref, wba_ref, wo_ref, g1_ref, b1_ref,
             h1_ref, h1b_ref, mix_ref, ogc_ref, oac_ref):
        used = _tm_rows(pl.program_id(0)) >= LEAD * BLOCK
        og = jnp.where(used, og_ref[...], jnp.zeros_like(og_ref))
        oa = jnp.where(used, oa_ref[...], jnp.zeros_like(oa_ref))
        ogc_ref[...] = og
        oac_ref[...] = oa
        *_, mixin, _, _, h1 = _mid_forward(h0_ref, pg_ref, og, oa, wbh_ref, wba_ref, wo_ref, g1_ref[...], b1_ref[...])
        mix_ref[...] = mixin
        h1_ref[...] = h1
        h1b_ref[...] = h1.astype(bf16)

    p = h0.shape[0]
    hw = HG_HEADS * HG_K
    sds = lambda n, dt: jax.ShapeDtypeStruct((p, n), dt)
    return pl.pallas_call(
        body, name="mid_front", grid=(p // TM,),
        in_specs=[_tm_row(D_MODEL), _tm_row(N_G), _tm_row(hw), _tm_row(ATT_QW)] + _mid_weight_specs(),
        out_specs=[_tm_row(D_MODEL), _tm_row(D_MODEL), _tm_row(D_MODEL), _tm_row(hw), _tm_row(ATT_QW)],
        out_shape=[sds(D_MODEL, f32), sds(D_MODEL, bf16), sds(D_MODEL, bf16), sds(hw, bf16), sds(ATT_QW, bf16)],
        compiler_params=_cparams(("parallel",)),
    )(h0, pg, og, oatt, wbh, wba, wout, ln1g, ln1b)


def mid_ffn(h1, target, wfi, wfo, ln2g, ln2b):
    def body(h1_ref, t_ref, wfi_ref, wfo_ref, g2_ref, b2_ref,
             dh1_ref, dau_ref, s_ref, dr2_ref, loss_ref, dg2_ref, db2_ref):
        i = pl.program_id(0)

        @pl.when(i == 0)
        def _():
            for r in (loss_ref, dg2_ref, db2_ref):
                r[...] = jnp.zeros_like(r)

        g2, b2 = g2_ref[...], b2_ref[...]
        h1 = h1_ref[...]
        au = jnp.dot(h1.astype(bf16), wfi_ref[...], preferred_element_type=f32)
        a, u = au[:, :D_FF], au[:, D_FF:]
        sg = _sigmoid(a)
        sa = a * sg
        s = (sa * u).astype(bf16)
        s_ref[...] = s
        r2 = ALPHA * h1 + jnp.dot(s, wfo_ref[...], preferred_element_type=f32)
        xh2, rs2 = _ln_stats(r2)
        diff = jnp.where(i > 0, xh2 * g2 + b2 - t_ref[...], 0.0)
        loss_ref[...] += jnp.sum(diff * diff) * (0.5 / D_MODEL)
        dy = diff * (1.0 / D_MODEL)
        dg2_ref[...] += jnp.sum(dy * xh2, axis=0, keepdims=True)
        db2_ref[...] += jnp.sum(dy, axis=0, keepdims=True)
        dr2 = _ln_bwd(dy, xh2, rs2, g2)
        dr2b = dr2.astype(bf16)
        dr2_ref[...] = dr2b
        ds = _dot(dr2b, wfo_ref[...], 1, 1)
        da = (ds * u) * (sg * (1.0 + a * (1.0 - sg)))
        du = ds * sa
        dau = jnp.concatenate([da, du], axis=1).astype(bf16)
        dau_ref[...] = dau
        dh1_ref[...] = ALPHA * dr2 + _dot(dau, wfi_ref[...], 1, 1)

    p = h1.shape[0]
    vec = lambda: pl.BlockSpec((1, D_MODEL), lambda i: (0, 0))
    sds = lambda n, dt: jax.ShapeDtypeStruct((p, n), dt)
    return pl.pallas_call(
        body, name="mid_ffn", grid=(p // TM,),
        in_specs=[_tm_row(D_MODEL), _tm_tokens(), _const_spec((D_MODEL, 2 * D_FF)), _const_spec((D_FF, D_MODEL)),
                  _const_spec((1, D_MODEL)), _const_spec((1, D_MODEL))],
        out_specs=[_tm_row(D_MODEL), _tm_row(2 * D_FF), _tm_row(D_FF), _tm_row(D_MODEL),
                   pl.BlockSpec((1, 1), lambda i: (0, 0)), vec(), vec()],
        out_shape=[sds(D_MODEL, f32), sds(2 * D_FF, bf16), sds(D_FF, bf16), sds(D_MODEL, bf16),
                   jax.ShapeDtypeStruct((1, 1), f32)] + [jax.ShapeDtypeStruct((1, D_MODEL), f32)] * 2,
        compiler_params=_cparams(("arbitrary",)),
    )(h1, target, wfi, wfo, ln2g, ln2b)


def mid_back(dh1, h0, pg, ogc, oac, wbh, wba, wout, ln1g, ln1b):
    def body(dh1_ref, h0_ref, pg_ref, og_ref, oa_ref, wbh_ref, wba_ref, wo_ref, g1_ref, b1_ref,
             dh0_ref, dpg_ref, dog_ref, doa_ref, dyh_ref, dya_ref, dr1_ref, dg1_ref, db1_ref):
        @pl.when(pl.program_id(0) == 0)
        def _():
            dg1_ref[...] = jnp.zeros_like(dg1_ref)
            db1_ref[...] = jnp.zeros_like(db1_ref)

        g1 = g1_ref[...]
        yh, ya, gh, ga, _, xh1, rs1, _ = _mid_forward(h0_ref, pg_ref, og_ref[...], oa_ref[...], wbh_ref, wba_ref,
                                                      wo_ref, g1, b1_ref[...])
        dh1 = dh1_ref[...]
        dg1_ref[...] += jnp.sum(dh1 * xh1, axis=0, keepdims=True)
        db1_ref[...] += jnp.sum(dh1, axis=0, keepdims=True)
        dr1 = _ln_bwd(dh1, xh1, rs1, g1)
        dr1b = dr1.astype(bf16)
        dr1_ref[...] = dr1b
        dh0_ref[...] = ALPHA * dr1
        dmix = _dot(dr1b, wo_ref[...], 1, 1)
        dyh = (dmix * gh).astype(bf16)
        dya = (dmix * ga).astype(bf16)
        dyh_ref[...] = dyh
        dya_ref[...] = dya
        dpg_ref[:, :D_MODEL] = (dmix * yh * gh * (1.0 - gh)).astype(bf16)
        dpg_ref[:, D_MODEL:] = (dmix * ya * ga * (1.0 - ga)).astype(bf16)
        dog_ref[...] = _dot(dyh, wbh_ref[...], 1, 1)
        doa_ref[...] = _dot(dya, wba_ref[...], 1, 1)

    p = h0.shape[0]
    hw = HG_HEADS * HG_K
    vec = lambda: pl.BlockSpec((1, D_MODEL), lambda i: (0, 0))
    sds = lambda n, dt: jax.ShapeDtypeStruct((p, n), dt)
    return pl.pallas_call(
        body, name="mid_back", grid=(p // TM,),
        in_specs=[_tm_row(D_MODEL), _tm_row(D_MODEL), _tm_row(N_G), _tm_row(hw), _tm_row(ATT_QW)] + _mid_weight_specs(),
        out_specs=[_tm_row(D_MODEL), _tm_row(N_G), _tm_row(hw), _tm_row(ATT_QW), _tm_row(D_MODEL), _tm_row(D_MODEL),
                   _tm_row(D_MODEL), vec(), vec()],
        out_shape=[sds(D_MODEL, f32), sds(N_G, bf16), sds(hw, f32), sds(ATT_QW, f32), sds(D_MODEL, bf16),
                   sds(D_MODEL, bf16), sds(D_MODEL, bf16)] + [jax.ShapeDtypeStruct((1, D_MODEL), f32)] * 2,
        compiler_params=_cparams(("arbitrary",)),
    )(dh1, h0, pg, ogc, oac, wbh, wba, wout, ln1g, ln1b)


def inproj_bwd(dh0p, dhq, dhf, dhi, dhg, daq, dkc, dkp, dvc, dvp, dkm, dvm, dpg, w_in, x, metablk, g, b, cos, sin):
    p = dh0p.shape[0]
    nbk = p // BLOCK
    per = TM // BLOCK

    def body(dh0_ref, dq_ref, df_ref, di_ref, dg_ref, daq_ref, dkc_ref, *rest):
        dkp_refs, dvc_ref, dvp_refs = rest[:per], rest[per], rest[per + 1:2 * per + 1]
        (dkm_ref, dvm_ref, dpg_ref, w_ref, x_ref, mb_ref, g_ref, b_ref, cos_ref, sin_ref,
         dproj_ref, dx_ref, dmeta_ref, dlg_ref, dlb_ref) = rest[2 * per + 1:]
        i = pl.program_id(0)

        @pl.when(i == 0)
        def _():
            dlg_ref[...] = jnp.zeros_like(dlg_ref)
            dlb_ref[...] = jnp.zeros_like(dlb_ref)

        zero_pad = jnp.zeros((TM - N_META, BLOCK), f32)
        first = i == 0
        rows = _tm_rows(i)

        def keys(cur_ref, next_refs, meta_ref):
            nxt = jnp.concatenate([jnp.where(per * i + 1 + m < nbk, next_refs[m][...], 0.0) for m in range(per)], axis=0)
            t = cur_ref[...] + nxt
            return t + jnp.where(first, jnp.concatenate([zero_pad, meta_ref[...]], axis=0), 0.0)

        cos, sin = cos_ref[...], sin_ref[...]
        unrotate = lambda t: _rope_transposed(t, cos, sin).astype(bf16)
        dproj = jnp.concatenate(
            [dq_ref[...], df_ref[...], di_ref[...], dg_ref[...]]
            + [unrotate(daq_ref[:, m * BLOCK:(m + 1) * BLOCK]) for m in range(ATT_QW // BLOCK)]
            + [unrotate(keys(dkc_ref, dkp_refs, dkm_ref)), keys(dvc_ref, dvp_refs, dvm_ref).astype(bf16),
               dpg_ref[...]], axis=1)
        dproj = jnp.where(rows >= LEAD * BLOCK, dproj, jnp.zeros_like(dproj))
        dproj_ref[...] = dproj
        valid = rows >= TM - N_META
        dh0 = jnp.where(valid, dh0_ref[...] + _dot(dproj, w_ref[...], 1, 0), 0.0)
        xb = jnp.where(first, mb_ref[...], x_ref[...])
        xh, rs = _ln_stats(xb)
        dlg_ref[...] += jnp.sum(dh0 * xh, axis=0, keepdims=True)
        dlb_ref[...] += jnp.sum(dh0, axis=0, keepdims=True)
        dx = jnp.where(valid, _ln_bwd(dh0, xh, rs, g_ref[...]), 0.0)
        dx_ref[...] = dx

        @pl.when(first)
        def _():
            dmeta_ref[...] = dx[TM - N_META:, :]

    row = _tm_row
    nxt = [pl.BlockSpec((BLOCK, BLOCK), functools.partial(lambda i, m: (jnp.minimum(per * i + 1 + m, nbk - 1), 0), m=m))
           for m in range(per)]
    hw = HG_HEADS * HG_K
    vec = lambda: pl.BlockSpec((1, D_MODEL), lambda i: (0, 0))
    return pl.pallas_call(
        body, name="inproj_bwd", grid=(p // TM,),
        in_specs=[row(D_MODEL), row(hw), row(hw), row(hw), row(hw), row(ATT_QW),
                  row(BLOCK)] + nxt + [row(BLOCK)] + nxt + [_const_spec((N_META, BLOCK)), _const_spec((N_META, BLOCK)),
                  row(N_G), _const_spec((IN_W, D_MODEL)), _tm_tokens(),
                  _const_spec((TM, D_MODEL)), _const_spec((1, D_MODEL)), _const_spec((1, D_MODEL)),
                  row(BLOCK), row(BLOCK)],
        out_specs=[row(IN_W), _tm_tokens(), pl.BlockSpec((N_META, D_MODEL), lambda i: (0, 0)), vec(), vec()],
        out_shape=[jax.ShapeDtypeStruct((p, IN_W), bf16), jax.ShapeDtypeStruct((p - TM, D_MODEL), f32),
                   jax.ShapeDtypeStruct((N_META, D_MODEL), f32),
                   jax.ShapeDtypeStruct((1, D_MODEL), f32), jax.ShapeDtypeStruct((1, D_MODEL), f32)],
        compiler_params=_cparams(("arbitrary",)),
    )(dh0p, dhq, dhf, dhi, dhg, daq, dkc, *([dkp] * per), dvc, *([dvp] * per), dkm, dvm, dpg, w_in, x, metablk, g, b,
      cos, sin)


def wgrad(a, b, name, tk, tn, tp, by_cols, out_dtype=f32):
    p, k = a.shape
    n = b.shape[1]
    nsteps = p // tp

    def body(a_ref, b_ref, o_ref, acc_ref):
        ip = pl.program_id(2)

        @pl.when(ip == 0)
        def _():
            acc_ref[...] = jnp.zeros_like(acc_ref)

        acc_ref[...] += _dot(a_ref[...], b_ref[...], 0, 0)

        @pl.when(ip == nsteps - 1)
        def _():
            for j in range(span):
                o_ref[j] = acc_ref[:, j * width:(j + 1) * width].astype(out_dtype)

    span, width = 1, tn
    if by_cols:
        shard_n = n // N_SHARD
        out_shape = (N_SHARD, k, shard_n)
        if tn >= shard_n:
            span, width = tn // shard_n, shard_n
            omap = lambda ik, jn, ip: (jn, ik, 0)
        else:
            per = shard_n // tn
            omap = lambda ik, jn, ip: (jn // per, ik, jn % per)
    else:
        out_shape = (1, k, n)
        omap = lambda ik, jn, ip: (0, ik, jn)
    return pl.pallas_call(
        body, name=name, grid=(k // tk, n // tn, nsteps),
        in_specs=[pl.BlockSpec((tp, tk), lambda ik, jn, ip: (ip, ik)),
                  pl.BlockSpec((tp, tn), lambda ik, jn, ip: (ip, jn))],
        out_specs=pl.BlockSpec((span, tk, width), omap),
        out_shape=jax.ShapeDtypeStruct(out_shape, out_dtype),
        scratch_shapes=[pltpu.VMEM((tk, tn), f32)],
        compiler_params=_cparams(("parallel", "parallel", "arbitrary")),
    )(a, b)


def _adamw_math(w, g, m, v):
    mn = ADAM_B1 * m + (1.0 - ADAM_B1) * g
    vn = ADAM_B2 * v + (1.0 - ADAM_B2) * (g * g)
    m_hat = mn / (1.0 - ADAM_B1 ** ADAM_STEP)
    v_hat = vn / (1.0 - ADAM_B2 ** ADAM_STEP)
    return -ADAM_LR * (m_hat / (jnp.sqrt(v_hat) + ADAM_EPS) + ADAM_WD * w), mn, vn


def adamw(w, g, m, v, name):
    r, c = w.shape
    tr = r
    for cand in (256, 176, 152, 128):
        if r > cand and r % cand == 0:
            tr = cand
            break

    def body(w_ref, g_ref, m_ref, v_ref, go_ref, d_ref, mo_ref, vo_ref):
        gg = g_ref[...]
        go_ref[...] = gg
        d_ref[...], mo_ref[...], vo_ref[...] = _adamw_math(w_ref[...], gg, m_ref[...], v_ref[...])

    spec = pl.BlockSpec((tr, c), lambda i: (i, 0))
    sds = jax.ShapeDtypeStruct((r, c), f32)
    return pl.pallas_call(
        body, name=name, grid=(r // tr,), in_specs=[spec] * 4, out_specs=[spec] * 4, out_shape=[sds] * 4,
        compiler_params=_cparams(("parallel",)),
    )(w, g, m, v)


def adamw_small(ws, gs, ms, vs):
    n = len(ws)

    def body(*refs):
        ins, outs = refs[:4 * n], refs[4 * n:]
        for k in range(n):
            outs[k][...], outs[n + k][...], outs[2 * n + k][...] = _adamw_math(
                ins[k][...], ins[n + k][...], ins[2 * n + k][...], ins[3 * n + k][...])

    out = pl.pallas_call(body, name="adamw_small",
                         out_shape=[jax.ShapeDtypeStruct(w.shape, f32) for w in ws] * 3)(*ws, *gs, *ms, *vs)
    return out[:n], out[n:2 * n], out[2 * n:]


def _me():
    return lax.axis_index("x"), lax.axis_index("y"), lax.axis_index("c")


def _chip_peer(x, y, c, k):
    return (x ^ (k >> 1), y ^ (k & 1), c)


ANY = pl.BlockSpec(memory_space=pl.ANY)


def gather_weights(now, later):
    n, n_later = len(now), len(later)
    out_dtypes = [bf16 if s.size > 16 * 256 else f32 for s in now]
    halves = [(2, s.shape[0] // 2, s.shape[1]) for s in now]

    def body(*refs):
        ins, later_ins = refs[:n], refs[n:n + n_later]
        outs, later_outs = refs[n + n_later:2 * n + n_later], refs[2 * n + n_later:2 * (n + n_later)]
        stage = refs[2 * (n + n_later):3 * n + 2 * n_later]
        send_sems, recv_sems, pass_send_sems, pass_recv_sems, local_sems = refs[3 * n + 2 * n_later:]
        x, y, c = _me()
        j = 2 * x + y
        sibling = (x, y, 1 - c)

        def over_ici(w, k, slot):
            return pltpu.make_async_remote_copy(
                src_ref=stage[w].at[c], dst_ref=outs[w].at[slot, c], send_sem=send_sems.at[w, k - 1],
                recv_sem=recv_sems.at[w, k - 1], device_id=_chip_peer(x, y, c, k), device_id_type=MESH)

        def passed_on(w, k, half):
            return pltpu.make_async_remote_copy(
                src_ref=outs[w].at[j ^ k, half], dst_ref=outs[w].at[j ^ k, half], send_sem=pass_send_sems.at[w, k - 1],
                recv_sem=pass_recv_sems.at[w, k - 1], device_id=sibling, device_id_type=MESH)

        for w in range(n):
            stage[w][...] = ins[w][...].astype(out_dtypes[w]).reshape(halves[w])
        locs = []
        for w in range(n):
            loc = pltpu.make_async_copy(stage[w], outs[w].at[j], local_sems.at[w])
            loc.start()
            locs.append(loc)
            for k in (1, 2, 3):
                over_ici(w, k, j).start()
        for w in range(n_later):
            later_outs[w][...] = later_ins[w][...].astype(bf16)
        for w in range(n):
            for k in (1, 2, 3):
                over_ici(w, k, j ^ k).wait_recv()
                passed_on(w, k, c).start()
        for w in range(n):
            for k in (1, 2, 3):
                passed_on(w, k, 1 - c).wait_recv()
        for w in range(n):
            for k in (1, 2, 3):
                over_ici(w, k, j).wait_send()
                passed_on(w, k, c).wait_send()
        for loc in locs:
            loc.wait()

    vmem = pl.BlockSpec(memory_space=pltpu.VMEM)
    sem3 = pltpu.SemaphoreType.DMA((n, 3))
    return pl.pallas_call(
        body, name="gather_weights",
        in_specs=[vmem] * (n + n_later), out_specs=[ANY] * n + [vmem] * n_later,
        out_shape=[jax.ShapeDtypeStruct((N_SHARD,) + h, dt) for h, dt in zip(halves, out_dtypes)]
        + [jax.ShapeDtypeStruct(s.shape, bf16) for s in later],
        scratch_shapes=[pltpu.VMEM(h, dt) for h, dt in zip(halves, out_dtypes)]
        + [sem3, sem3, sem3, sem3, pltpu.SemaphoreType.DMA((n,))],
        compiler_params=pltpu.CompilerParams(vmem_limit_bytes=VMEM_LIMIT),
    )(*now, *later)


def _shard_push(srcs, dsts, send_sems, recv_sems, local_sems):
    def remote(w, k, slot):
        x, y, c = _me()
        return pltpu.make_async_remote_copy(
            src_ref=srcs[w], dst_ref=dsts[w].at[slot], send_sem=send_sems.at[w, k - 1],
            recv_sem=recv_sems.at[w, k - 1], device_id=_chip_peer(x, y, c, k), device_id_type=MESH)

    def local(w):
        x, y, _ = _me()
        return pltpu.make_async_copy(srcs[w], dsts[w].at[2 * x + y], local_sems.at[w])

    def start():
        x, y, _ = _me()
        for w in range(len(srcs)):
            local(w).start()
            for k in (1, 2, 3):
                remote(w, k, 2 * x + y).start()

    def wait():
        x, y, _ = _me()
        for w in range(len(srcs)):
            for k in (1, 2, 3):
                remote(w, k, (2 * x + y) ^ k).wait_recv()
        for w in range(len(srcs)):
            for k in (1, 2, 3):
                remote(w, k, 2 * x + y).wait_send()
            local(w).wait()

    return start, wait


def _grad_push(srcs, dsts, send_sems, recv_sems):
    def copy(w, k):
        x, y, c = _me()
        px, py, pc = x ^ (k >> 2), y ^ ((k >> 1) & 1), c ^ (k & 1)
        return pltpu.make_async_remote_copy(
            src_ref=srcs[w].at[2 * px + py, pc], dst_ref=dsts[w].at[k - 1], send_sem=send_sems.at[w, k - 1],
            recv_sem=recv_sems.at[w, k - 1], device_id=(px, py, pc), device_id_type=MESH)

    def start():
        for w in range(len(srcs)):
            for k in range(1, N_DEV):
                copy(w, k).start()

    def wait():
        for w in range(len(srcs)):
            for k in range(1, N_DEV):
                copy(w, k).wait_recv()
        for w in range(len(srcs)):
            for k in range(1, N_DEV):
                copy(w, k).wait_send()

    return start, wait


def _grad_push_specs(grads):
    n = len(grads)
    return ([ANY] * n, [ANY] * n, [jax.ShapeDtypeStruct((N_DEV - 1,) + g.shape[2:], g.dtype) for g in grads],
            [pltpu.SemaphoreType.DMA((n, N_DEV - 1)), pltpu.SemaphoreType.DMA((n, N_DEV - 1))])


def add_eight(own, parts, jc_idx, name):
    _, half, c = parts.shape
    tr = half // 2 if (half // 2) % 16 == 0 else half

    def body(jc_ref, own_ref, p_ref, out_ref):
        acc = own_ref[0, 0].astype(f32)
        for k in range(N_DEV - 1):
            acc = acc + p_ref[k].astype(f32)
        out_ref[0] = acc

    return pl.pallas_call(
        body, name=name,
        grid_spec=pltpu.PrefetchScalarGridSpec(
            num_scalar_prefetch=1, grid=(half // tr,),
            in_specs=[pl.BlockSpec((1, 1, tr, c), lambda t, jc: (jc[0], jc[1], t, 0)),
                      pl.BlockSpec((N_DEV - 1, tr, c), lambda t, jc: (0, t, 0))],
            out_specs=pl.BlockSpec((1, tr, c), lambda t, jc: (jc[1], t, 0))),
        out_shape=jax.ShapeDtypeStruct((2, half, c), f32),
        compiler_params=_cparams(("parallel",)),
    )(jc_idx, own, parts)


def _push_specs(shards):
    n = len(shards)
    return ([ANY] * n, [ANY] * n, [jax.ShapeDtypeStruct((N_SHARD,) + s.shape, s.dtype) for s in shards],
            [pltpu.SemaphoreType.DMA((n, 3)), pltpu.SemaphoreType.DMA((n, 3)), pltpu.SemaphoreType.DMA((n,))])


def pair_exchange_halves(grads, small):
    n = len(grads)

    def body(*refs):
        ins, small_ref = refs[:n], refs[n]
        outs, gath = refs[n + 1:2 * n + 1], refs[2 * n + 1]
        send_sems, recv_sems, s_send, s_recv, local_sem = refs[2 * n + 2:]
        x, y, c = _me()
        me = 4 * x + 2 * y + c
        sends = []
        for w in range(n):
            half = ins[w].shape[1] // 2
            cp = pltpu.make_async_remote_copy(
                src_ref=ins[w].at[:, pl.ds((1 - c) * half, half), :], dst_ref=outs[w],
                send_sem=send_sems.at[w], recv_sem=recv_sems.at[w], device_id=(x, y, 1 - c), device_id_type=MESH)
            cp.start()
            sends.append(cp)
        loc = pltpu.make_async_copy(small_ref, gath.at[me], local_sem)
        loc.start()
        for k in range(1, N_DEV):
            cp = pltpu.make_async_remote_copy(
                src_ref=small_ref, dst_ref=gath.at[me], send_sem=s_send.at[k - 1], recv_sem=s_recv.at[k - 1],
                device_id=(x ^ (k >> 2), y ^ ((k >> 1) & 1), c ^ (k & 1)), device_id_type=MESH)
            cp.start()
            sends.append(cp)
        for w in range(n):
            half = ins[w].shape[1] // 2
            pltpu.make_async_remote_copy(
                src_ref=ins[w].at[:, pl.ds(0, half), :], dst_ref=outs[w], send_sem=send_sems.at[w],
                recv_sem=recv_sems.at[w], device_id=(x, y, 1 - c), device_id_type=MESH).wait_recv()
        for k in range(1, N_DEV):
            pltpu.make_async_remote_copy(
                src_ref=small_ref, dst_ref=gath.at[me ^ k], send_sem=s_send.at[k - 1], recv_sem=s_recv.at[k - 1],
                device_id=(x ^ (k >> 2), y ^ ((k >> 1) & 1), c ^ (k & 1)), device_id_type=MESH).wait_recv()
        for cp in sends:
            cp.wait_send()
        loc.wait()

    return pl.pallas_call(
        body, name="pair_exchange_halves", in_specs=[ANY] * (n + 1), out_specs=[ANY] * (n + 1),
        out_shape=[jax.ShapeDtypeStruct((g.shape[0], g.shape[1] // 2, g.shape[2]), f32) for g in grads]
        + [jax.ShapeDtypeStruct((N_DEV,) + small.shape, f32)],
        scratch_shapes=[pltpu.SemaphoreType.DMA((n,)), pltpu.SemaphoreType.DMA((n,)),
                        pltpu.SemaphoreType.DMA((N_DEV - 1,)), pltpu.SemaphoreType.DMA((N_DEV - 1,)),
                        pltpu.SemaphoreType.DMA],
    )(*grads, small)


def chip_exchange(sums):
    n = len(sums)

    def body(*refs):
        ins, outs = refs[:n], refs[n:2 * n]
        send_sems, recv_sems = refs[2 * n:]
        x, y, c = _me()
        j = 2 * x + y
        sends = []
        for w in range(n):
            for k in (1, 2, 3):
                cp = pltpu.make_async_remote_copy(
                    src_ref=ins[w].at[j ^ k], dst_ref=outs[w].at[k - 1], send_sem=send_sems.at[w, k - 1],
                    recv_sem=recv_sems.at[w, k - 1], device_id=_chip_peer(x, y, c, k), device_id_type=MESH)
                cp.start()
                sends.append(cp)
        for w in range(n):
            for k in (1, 2, 3):
                pltpu.make_async_remote_copy(
                    src_ref=ins[w].at[0], dst_ref=outs[w].at[k - 1], send_sem=send_sems.at[w, k - 1],
                    recv_sem=recv_sems.at[w, k - 1], device_id=_chip_peer(x, y, c, k), device_id_type=MESH).wait_recv()
        for cp in sends:
            cp.wait_send()

    return pl.pallas_call(
        body, name="chip_exchange", in_specs=[ANY] * n, out_specs=[ANY] * n,
        out_shape=[jax.ShapeDtypeStruct((N_SHARD - 1,) + s.shape[1:], s.dtype) for s in sums],
        scratch_shapes=[pltpu.SemaphoreType.DMA((n, 3)), pltpu.SemaphoreType.DMA((n, 3))],
    )(*sums)


def pair_exchange_results(halves):
    n = len(halves)

    def body(*refs):
        ins, outs = refs[:n], refs[n:2 * n]
        send_sems, recv_sems = refs[2 * n:]
        x, y, c = _me()
        sends = []
        for w in range(n):
            cp = pltpu.make_async_remote_copy(
                src_ref=ins[w].at[c], dst_ref=outs[w].at[c], send_sem=send_sems.at[w], recv_sem=recv_sems.at[w],
                device_id=(x, y, 1 - c), device_id_type=MESH)
            cp.start()
            sends.append(cp)
        for w in range(n):
            pltpu.make_async_remote_copy(
                src_ref=ins[w].at[c], dst_ref=outs[w].at[1 - c], send_sem=send_sems.at[w],
                recv_sem=recv_sems.at[w], device_id=(x, y, 1 - c), device_id_type=MESH).wait_recv()
        for cp in sends:
            cp.wait_send()

    return pl.pallas_call(
        body, name="pair_exchange_results", in_specs=[ANY] * n, out_specs=[ANY] * n,
        out_shape=[jax.ShapeDtypeStruct(h.shape, f32) for h in halves],
        input_output_aliases={w: w for w in range(n)},
        scratch_shapes=[pltpu.SemaphoreType.DMA((n,)), pltpu.SemaphoreType.DMA((n,))],
    )(*halves)


def add_pair(grad, other, c_idx, name):
    _, r, c = grad.shape
    half = r // 2
    tr = half // 2 if (half // 2) % 8 == 0 else half
    per = half // tr

    def body(c_ref, g_ref, o_ref, out_ref):
        out_ref[...] = (g_ref[...] + o_ref[...]).astype(bf16)

    return pl.pallas_call(
        body, name=name,
        grid_spec=pltpu.PrefetchScalarGridSpec(
            num_scalar_prefetch=1, grid=(N_SHARD, per),
            in_specs=[pl.BlockSpec((1, tr, c), lambda j, t, cr: (j, cr[0] * per + t, 0)),
                      pl.BlockSpec((1, tr, c), lambda j, t, cr: (j, t, 0))],
            out_specs=pl.BlockSpec((1, tr, c), lambda j, t, cr: (j, t, 0))),
        out_shape=jax.ShapeDtypeStruct((N_SHARD, half, c), bf16),
        compiler_params=_cparams(("parallel", "parallel")),
    )(c_idx, grad, other)


def add_four(own, parts, jc_idx, name):
    _, half, c = parts.shape
    tr = half // 2 if (half // 2) % 8 == 0 else half

    def body(jc_ref, own_ref, p_ref, out_ref):
        acc = own_ref[0].astype(f32)
        for k in range(N_SHARD - 1):
            acc = acc + p_ref[k].astype(f32)
        out_ref[0] = acc

    return pl.pallas_call(
        body, name=name,
        grid_spec=pltpu.PrefetchScalarGridSpec(
            num_scalar_prefetch=1, grid=(half // tr,),
            in_specs=[pl.BlockSpec((1, tr, c), lambda t, jc: (jc[0], t, 0)),
                      pl.BlockSpec((N_SHARD - 1, tr, c), lambda t, jc: (0, t, 0))],
            out_specs=pl.BlockSpec((1, tr, c), lambda t, jc: (jc[1], t, 0))),
        out_shape=jax.ShapeDtypeStruct((2, half, c), f32),
        compiler_params=_cparams(("parallel",)),
    )(jc_idx, own, parts)


def sum_devices(gathered):
    def body(g_ref, out_ref):
        acc = g_ref[0]
        for d in range(1, N_DEV):
            acc = acc + g_ref[d]
        out_ref[...] = acc

    return pl.pallas_call(body, name="sum_devices", out_shape=jax.ShapeDtypeStruct(gathered.shape[1:], f32))(gathered)


def _rows128(a, rows):
    flat = a.reshape(-1, BLOCK) if a.size % BLOCK == 0 else jnp.pad(a.reshape(1, -1), ((0, 0), (0, BLOCK - a.size)))
    return jnp.pad(flat, ((0, rows - flat.shape[0]), (0, 0)))


def kernel(x, meta_tokens, ln_emb_g, ln_emb_b, w_in, hg_lower_bounds, hg_norm_g, attn_sinks, w_branch_hg, w_branch_attn, w_out, ln1_g, ln1_b, w_ffn_in, w_ffn_out, ln2_g, ln2_b, loss_target, m_meta_tokens, m_ln_emb_g, m_ln_emb_b, m_w_in, m_hg_lower_bounds, m_hg_norm_g, m_attn_sinks, m_w_branch_hg, m_w_branch_attn, m_w_out, m_ln1_g, m_ln1_b, m_w_ffn_in, m_w_ffn_out, m_ln2_g, m_ln2_b, v_meta_tokens, v_ln_emb_g, v_ln_emb_b, v_w_in, v_hg_lower_bounds, v_hg_norm_g, v_attn_sinks, v_w_branch_hg, v_w_branch_attn, v_w_out, v_ln1_g, v_ln1_b, v_w_ffn_in, v_w_ffn_out, v_ln2_g, v_ln2_b):
    seq = x.shape[1]
    nb = seq // BLOCK + 1
    xs = x[0]
    ts = loss_target[0]
    ix, iy, ic = _me()
    shard = 2 * ix + iy
    vec = lambda a: a.reshape(1, D_MODEL)

    w_in_t = jnp.swapaxes(w_in[0], 0, 1)
    g_in, g_meta, s_bh, s_ba, s_out, s_fi, s_fo = gather_weights(
        [w_in_t, meta_tokens], [w_branch_hg[0], w_branch_attn[0], w_out[0], w_ffn_in[0], w_ffn_out[0]])
    by_cols = lambda g: g.reshape(N_SHARD, -1, g.shape[-1]).transpose(1, 0, 2).reshape(-1, N_SHARD * g.shape[-1])
    wf_in = g_in.reshape(IN_W, D_MODEL)
    metablk = jnp.pad(by_cols(g_meta), ((TM - N_META, 0), (0, 0)))

    pos = jnp.arange((nb + LEAD) * BLOCK, dtype=jnp.int32) - (LEAD * BLOCK + PAD)
    half = HEAD_DIM // 2
    inv = ROPE_THETA ** (-jnp.arange(half, dtype=f32) / half)
    ang = pos.astype(f32)[:, None] * inv[None, :]
    cos = jnp.tile(jnp.cos(ang), (1, BLOCK // half))
    sin = jnp.tile(jnp.sin(ang), (1, BLOCK // half))
    sinks8 = jnp.broadcast_to(attn_sinks.reshape(ATT_HEADS, 1), (ATT_HEADS, BLOCK))
    ng = hg_norm_g.reshape(1, HG_K)

    h0, h0b, pa, pg = emb_inproj(xs, metablk, vec(ln_emb_g), vec(ln_emb_b), wf_in, cos, sin)
    og, sprev, g_fi, g_out = hgrn_fwd(pa, hg_lower_bounds, ng, nb, [s_fi, s_out])
    oatt, g_fo, g_bh, g_ba = attn_fwd(pa, sinks8, nb, [s_fo, s_bh, s_ba])
    wf_bh, wf_ba, wf_fi = by_cols(g_bh), by_cols(g_ba), by_cols(g_fi)
    wf_out = g_out.reshape(D_MODEL, D_MODEL)
    wf_fo = g_fo.reshape(D_FF, D_MODEL)
    h1, h1b, mixin, og, oatt = mid_front(h0, pg, og, oatt, wf_bh, wf_ba, wf_out, ln1_g, ln1_b)
    dh1, dau, sact, dr2, loss_part, dg2, db2 = mid_ffn(h1, ts, wf_fi, wf_fo, ln2_g, ln2_b)
    dh0p, dpg, dog, doa, dyh, dya, dr1, dg1, db1 = mid_back(dh1, h0, pg, og, oatt, wf_bh, wf_ba, wf_out, ln1_g, ln1_b)
    tp = max(t for t in (768, 512, TM) if h0.shape[0] % t == 0)
    pieces = lambda g: g.reshape(N_SHARD, 2, -1, g.shape[-1])
    gb_bh = pieces(wgrad(og, dyh, "wgrad_bh", 512, D_MODEL, tp, True, bf16))
    gb_ba = pieces(wgrad(oatt, dya, "wgrad_ba", 512, D_MODEL, tp, True, bf16))
    gb_out = pieces(wgrad(mixin, dr1, "wgrad_out", D_MODEL, D_MODEL, tp, False, bf16))
    gb_fi = pieces(wgrad(h1b, dau, "wgrad_fi", D_MODEL, D_FF, tp, True, bf16))
    gb_fo = pieces(wgrad(sact, dr2, "wgrad_fo", D_FF // 2, D_MODEL, tp, False, bf16))
    dhq, dhf, dhi, dhg, dlb4, dng, r_fi, r_fo = hgrn_bwd(pa, hg_lower_bounds, ng, sprev, dog, nb, [gb_fi, gb_fo])
    daq, dkc, dkp, dvc, dvp, dkm, dvm, dsk, r_out, r_bh, r_ba = attn_bwd(pa, sinks8, doa, nb,
                                                                         [gb_out, gb_bh, gb_ba])
    dproj, dx, dmeta, dlg, dlb = inproj_bwd(dh0p, dhq, dhf, dhi, dhg, daq, dkc, dkp, dvc, dvp, dkm, dvm, dpg,
                                      wf_in, xs, metablk, vec(ln_emb_g), vec(ln_emb_b), cos, sin)
    gw_in = wgrad(dproj, h0b, "wgrad_in", IN_W // 2, D_MODEL, tp, False).reshape(N_SHARD, -1, D_MODEL)

    parts = [(dlg, 8), (dlb, 8), (dlb4, 8), (dng, 8), (dsk[:, 0], 8),
             (dg1, 8), (db1, 8), (dg2, 8), (db2, 8), (dmeta, BLOCK), (loss_part, 8)]
    small = jnp.concatenate([_rows128(a, r) for a, r in parts], axis=0)

    c_idx = jnp.reshape(ic, (1,)).astype(jnp.int32)
    jc_idx = jnp.stack([shard, ic]).astype(jnp.int32)
    other_in, gathered = pair_exchange_halves([gw_in], small)
    sum_in = add_pair(gw_in, other_in, c_idx, "add_pair_in")
    quad_in, = chip_exchange([sum_in])
    halves = [add_four(sum_in, quad_in, jc_idx, "add_four_in")]
    halves += [add_eight(g, r, jc_idx, "add_eight_" + nm) for nm, g, r in
               (("bh", gb_bh, r_bh), ("ba", gb_ba, r_ba), ("out", gb_out, r_out), ("fi", gb_fi, r_fi),
                ("fo", gb_fo, r_fo))]
    red = [r.reshape(-1, r.shape[-1]) for r in pair_exchange_results(halves)]
    small_sum = sum_devices(gathered)

    offs, acc = [], 0
    for _, r in parts:
        offs.append(acc)
        acc += r
    take = lambda n, size: small_sum[offs[n]:offs[n] + parts[n][1]].reshape(-1)[:size]
    g_meta_full = take(9, N_META * D_MODEL).reshape(N_META, D_MODEL)
    g_small = {
        "meta_tokens": lax.dynamic_slice_in_dim(g_meta_full, shard * (D_MODEL // N_SHARD), D_MODEL // N_SHARD, axis=1),
        "ln_emb_g": take(0, D_MODEL), "ln_emb_b": take(1, D_MODEL),
        "hg_lower_bounds": take(2, 2 * HG_HEADS * HG_K).reshape(2, HG_HEADS * HG_K),
        "hg_norm_g": take(3, HG_K).reshape(1, HG_K), "attn_sinks": take(4, ATT_HEADS).reshape(1, ATT_HEADS),
        "ln1_g": take(5, D_MODEL).reshape(1, D_MODEL), "ln1_b": take(6, D_MODEL).reshape(1, D_MODEL),
        "ln2_g": take(7, D_MODEL).reshape(1, D_MODEL), "ln2_b": take(8, D_MODEL).reshape(1, D_MODEL),
    }
    g_big = {"w_in": red[0], "w_branch_hg": red[1], "w_branch_attn": red[2], "w_out": red[3],
             "w_ffn_in": red[4], "w_ffn_out": red[5]}

    names = ["meta_tokens", "ln_emb_g", "ln_emb_b", "w_in", "hg_lower_bounds", "hg_norm_g", "attn_sinks",
             "w_branch_hg", "w_branch_attn", "w_out", "ln1_g", "ln1_b", "w_ffn_in", "w_ffn_out", "ln2_g", "ln2_b"]
    given = dict(
        meta_tokens=(meta_tokens, m_meta_tokens, v_meta_tokens), ln_emb_g=(ln_emb_g, m_ln_emb_g, v_ln_emb_g),
        ln_emb_b=(ln_emb_b, m_ln_emb_b, v_ln_emb_b), w_in=(w_in, m_w_in, v_w_in),
        hg_lower_bounds=(hg_lower_bounds, m_hg_lower_bounds, v_hg_lower_bounds),
        hg_norm_g=(hg_norm_g, m_hg_norm_g, v_hg_norm_g), attn_sinks=(attn_sinks, m_attn_sinks, v_attn_sinks),
        w_branch_hg=(w_branch_hg, m_w_branch_hg, v_w_branch_hg),
        w_branch_attn=(w_branch_attn, m_w_branch_attn, v_w_branch_attn), w_out=(w_out, m_w_out, v_w_out),
        ln1_g=(ln1_g, m_ln1_g, v_ln1_g), ln1_b=(ln1_b, m_ln1_b, v_ln1_b), w_ffn_in=(w_ffn_in, m_w_ffn_in, v_w_ffn_in),
        w_ffn_out=(w_ffn_out, m_w_ffn_out, v_w_ffn_out), ln2_g=(ln2_g, m_ln2_g, v_ln2_g), ln2_b=(ln2_b, m_ln2_b, v_ln2_b))
    two_d = lambda a: a.reshape(8, BLOCK) if a.ndim == 1 else a.reshape(a.shape[-2], a.shape[-1])
    small_names = [nm for nm in names if nm not in g_big]
    small_d, small_m, small_v = adamw_small([two_d(given[nm][0]) for nm in small_names],
                                            [two_d(g_small[nm]) for nm in small_names],
                                            [two_d(given[nm][1]) for nm in small_names],
                                            [two_d(given[nm][2]) for nm in small_names])
    out_g, out_d, out_m, out_v = [], [], [], []
    for nm in names:
        w, m, v = given[nm]
        shape = w.shape
        if nm == "w_in":
            t = lambda a: jnp.swapaxes(two_d(a), 0, 1)
            g, d, mn, vn = [t(a) for a in adamw(t(w), g_big[nm], t(m), t(v), "adamw_" + nm)]
        elif nm in g_big:
            g, d, mn, vn = adamw(two_d(w), g_big[nm], two_d(m), two_d(v), "adamw_" + nm)
        else:
            k = small_names.index(nm)
            g, d, mn, vn = g_small[nm], small_d[k], small_m[k], small_v[k]
        out_g.append(g.reshape(shape))
        out_d.append(d.reshape(shape))
        out_m.append(mn.reshape(shape))
        out_v.append(vn.reshape(shape))

    loss = take(10, 1)[0]
    grad_x = dx.reshape(x.shape)
    return (loss, grad_x, *out_g, *out_d, *out_m, *out_v)
```

```python
import functools

import jax
import jax.numpy as jnp
from jax import lax
from jax.experimental import pallas as pl
from jax.experimental.pallas import tpu as pltpu

f32 = jnp.float32
bf16 = jnp.bfloat16

D_MODEL = 1024
BLOCK = 128
N_META = 16
PAD = BLOCK - N_META
HG_HEADS = 4
HG_K = 128
ATT_HEADS = 8
HEAD_DIM = 64
ATT_QW = ATT_HEADS * HEAD_DIM
D_FF = 2816
EPS = 1e-5
ALPHA = 2.0 ** 0.25
ROPE_THETA = 10000.0
N_A = 2816
N_G = 2048
IN_W = N_A + N_G
N_SHARD = 4
N_DEV = 8

ADAM_LR = 0.001
ADAM_B1 = 0.9
ADAM_B2 = 0.999
ADAM_EPS = 1e-08
ADAM_WD = 0.01
ADAM_STEP = 10

TM = 256
LEAD = TM // BLOCK - 1

VMEM_LIMIT = 56 * 1024 * 1024
VMEM_LIMIT_ALL_WEIGHTS = 62 * 1024 * 1024
MESH = pl.DeviceIdType.MESH


def _cparams(sem, vmem=VMEM_LIMIT):
    return pltpu.CompilerParams(dimension_semantics=sem, vmem_limit_bytes=vmem)


def _const_spec(shape):
    zeros = (0,) * len(shape)
    return pl.BlockSpec(shape, lambda *_: zeros, pipeline_mode=pl.Buffered(1))


def _dot(a, b, ca, cb):
    return lax.dot_general(a.astype(bf16), b.astype(bf16), (((ca,), (cb,)), ((), ())),
                           preferred_element_type=f32)


@jax.custom_vjp
def mm(a, b):
    return _dot(a, b, 1, 0)


mm.defvjp(lambda a, b: (_dot(a, b, 1, 0), (a, b)),
          lambda r, g: (_dot(g, r[1], 1, 1), _dot(r[0], g, 0, 0)))


@jax.custom_vjp
def mm_nt(a, b):
    return _dot(a, b, 1, 1)


mm_nt.defvjp(lambda a, b: (_dot(a, b, 1, 1), (a, b)),
             lambda r, g: (_dot(g, r[1], 1, 0), _dot(g, r[0], 0, 0)))


@jax.custom_vjp
def mm_tn(a, b):
    return _dot(a, b, 0, 0)


mm_tn.defvjp(lambda a, b: (_dot(a, b, 0, 0), (a, b)),
             lambda r, g: (_dot(r[1], g, 1, 1), _dot(r[0], g, 1, 0)))


@functools.partial(jax.custom_vjp, nondiff_argnums=(1,))
def roll_lanes(x, shift):
    return pltpu.roll(x, shift, 1)


roll_lanes.defvjp(lambda x, shift: (pltpu.roll(x, shift, 1), None),
                  lambda shift, _, g: (pltpu.roll(g, (128 - shift) % 128, 1),))


@jax.custom_vjp
def _sigmoid(x):
    return 1.0 / (1.0 + jnp.exp(-x))


def _sigmoid_fwd(x):
    s = 1.0 / (1.0 + jnp.exp(-x))
    return s, s


_sigmoid.defvjp(_sigmoid_fwd, lambda s, g: (g * s * (1.0 - s),))


@jax.custom_vjp
def _recip(x):
    return 1.0 / x


def _recip_fwd(x):
    r = 1.0 / x
    return r, r


_recip.defvjp(_recip_fwd, lambda r, g: (-g * r * r,))


def _ln_stats(x):
    mu = jnp.mean(x, axis=-1, keepdims=True)
    xc = x - mu
    var = jnp.mean(xc * xc, axis=-1, keepdims=True)
    rs = lax.rsqrt(var + EPS)
    return xc * rs, rs


def _ln_bwd(dy, xh, rs, g):
    dxh = dy * g
    m1 = jnp.mean(dxh, axis=-1, keepdims=True)
    m2 = jnp.mean(dxh * xh, axis=-1, keepdims=True)
    return rs * (dxh - m1 - xh * m2)


def _row_ids(i):
    return i * BLOCK + lax.broadcasted_iota(jnp.int32, (BLOCK, 1), 0)


def _tm_rows(i):
    return i * TM + lax.broadcasted_iota(jnp.int32, (TM, 1), 0)


def _tm_row(n):
    return pl.BlockSpec((TM, n), lambda i: (i, 0))


def _tm_tokens():
    return pl.BlockSpec((TM, D_MODEL), lambda i: (jnp.maximum(i - 1, 0), 0))


Q_COL, V_COL = 4 * HG_HEADS * HG_K, N_A - BLOCK


def emb_inproj(x, metablk, g, b, w_in, cos, sin):
    nsteps = x.shape[0] // TM + 1

    def body(x_ref, mb_ref, g_ref, b_ref, w_ref, cos_ref, sin_ref, h0_ref, h0b_ref, pa_ref, pg_ref):
        i = pl.program_id(0)
        xb = jnp.where(i == 0, mb_ref[...], x_ref[...])
        xh, _ = _ln_stats(xb)
        y = xh * g_ref[...] + b_ref[...]
        y = jnp.where(_tm_rows(i) >= TM - N_META, y, 0.0)
        h0_ref[...] = y
        yb = y.astype(bf16)
        h0b_ref[...] = yb
        pa = _dot(yb, w_ref[:N_A, :], 1, 1)
        cos, sin = cos_ref[...], sin_ref[...]
        pa_ref[:, :Q_COL] = pa[:, :Q_COL]
        for c0 in range(Q_COL, V_COL, BLOCK):
            pa_ref[:, c0:c0 + BLOCK] = _rope(pa[:, c0:c0 + BLOCK], cos, sin)
        pa_ref[:, V_COL:] = pa[:, V_COL:]
        pg_ref[...] = _dot(yb, w_ref[N_A:, :], 1, 1)

    p = nsteps * TM
    row = _tm_row
    return pl.pallas_call(
        body, name="emb_inproj", grid=(nsteps,),
        in_specs=[_tm_tokens(),
                  _const_spec((TM, D_MODEL)), _const_spec((1, D_MODEL)), _const_spec((1, D_MODEL)),
                  _const_spec((IN_W, D_MODEL)), _tm_row(BLOCK), _tm_row(BLOCK)],
        out_specs=[row(D_MODEL), row(D_MODEL), row(N_A), row(N_G)],
        out_shape=[jax.ShapeDtypeStruct((p, D_MODEL), f32), jax.ShapeDtypeStruct((p, D_MODEL), bf16),
                   jax.ShapeDtypeStruct((p, N_A), f32), jax.ShapeDtypeStruct((p, N_G), f32)],
        compiler_params=_cparams(("parallel",)),
    )(x, metablk, g, b, w_in, cos, sin)


def _hgrn_chunk(valid, st, hq, hf, hi, hg, lbraw, ng):
    lb = _sigmoid(lbraw[0:1] - lbraw[1:2])
    q = hq * _sigmoid(hq)
    fg = lb + (1.0 - lb) * _sigmoid(hf)
    logf = jnp.where(valid, jnp.log(fg), 0.0)
    k = jnp.where(valid, 1.0 - fg, 0.0)
    v = hi
    r = lax.broadcasted_iota(jnp.int32, (BLOCK, BLOCK), 0)
    c = lax.broadcasted_iota(jnp.int32, (BLOCK, BLOCK), 1)
    tril = (c <= r).astype(f32)
    bcum = jnp.dot(tril, logf, precision=lax.Precision.HIGHEST, preferred_element_type=f32)
    blast = bcum[BLOCK - 1:BLOCK]
    rows = lax.broadcasted_iota(jnp.int32, (BLOCK, 1), 0)
    sub8 = lax.broadcasted_iota(jnp.int32, (BLOCK // 8, 8, HG_K), 1)
    b8 = bcum.reshape(BLOCK // 8, 8, HG_K)
    row_of_8 = lambda j: jnp.broadcast_to(b8[:, j:j + 1, :], b8.shape)
    a = jnp.where(r == c, jnp.sum(q * k, axis=-1, keepdims=True), 0.0)
    seg = BLOCK
    while seg >= 2:
        half = seg // 2
        if seg >= 8:
            bs = bcum.reshape(BLOCK // seg, seg, HG_K)
            ref = jnp.broadcast_to(bs[:, half - 1:half, :], bs.shape)
        elif seg == 4:
            ref = jnp.where(sub8 < 4, row_of_8(1), row_of_8(5))
        else:
            ref = jnp.where(sub8 < 2, row_of_8(0), jnp.where(sub8 < 4, row_of_8(2),
                                                             jnp.where(sub8 < 6, row_of_8(4), row_of_8(6))))
        ref = ref.reshape(BLOCK, HG_K)
        upper = (rows % seg) >= half
        q_up = q * jnp.exp(jnp.where(upper, bcum - ref, -jnp.inf))
        k_lo = k * jnp.exp(jnp.where(upper, -jnp.inf, ref - bcum))
        a = a + jnp.where((r // seg) == (c // seg), mm_nt(q_up, k_lo), 0.0)
        seg = half
    o = mm_nt(q * jnp.exp(bcum), st) + mm(a, v)
    st_new = st * jnp.exp(blast) + mm_tn(v, k * jnp.exp(blast - bcum))
    on = o * lax.rsqrt(jnp.mean(o * o, axis=-1, keepdims=True) + EPS) * ng
    return st_new, on * (hg * _sigmoid(hg))


def _hgrn_in_specs(rowmap):
    wide = lambda col: pl.BlockSpec((BLOCK, HG_HEADS * HG_K), lambda i: (rowmap(i) + LEAD, col))
    return [wide(0), wide(1), wide(2), wide(3), _const_spec((2, HG_HEADS * HG_K)), _const_spec((1, HG_K))]


def _head(ref, h):
    return ref[:, h * HG_K:(h + 1) * HG_K]


def hgrn_fwd(pa, lbraw, ng, nb, shards):
    n = len(shards)

    def body(hq_ref, hf_ref, hi_ref, hg_ref, lb_ref, ng_ref, *rest):
        srcs, (og_ref, sp_ref), dsts = rest[:n], rest[n:n + 2], rest[n + 2:2 * n + 2]
        st_ref = rest[2 * n + 2]
        start, wait = _shard_push(srcs, dsts, *rest[2 * n + 3:])
        i = pl.program_id(0)

        @pl.when(i == 0)
        def _():
            st_ref[...] = jnp.zeros_like(st_ref)
            start()

        @pl.when(i == nb - 1)
        def _():
            wait()

        valid = _row_ids(i) >= PAD
        for h in range(HG_HEADS):
            st = st_ref[h]
            sp_ref[0, h] = st
            st_new, out = _hgrn_chunk(valid, st, _head(hq_ref, h), _head(hf_ref, h), _head(hi_ref, h),
                                      _head(hg_ref, h), _head(lb_ref, h), ng_ref[...])
            st_ref[h] = st_new
            og_ref[:, h * HG_K:(h + 1) * HG_K] = out.astype(bf16)

    p = (nb + LEAD) * BLOCK
    push_in, push_out, push_shape, push_scratch = _push_specs(shards)
    return pl.pallas_call(
        body, name="hgrn_fwd", grid=(nb,),
        in_specs=_hgrn_in_specs(lambda i: i) + push_in,
        out_specs=[pl.BlockSpec((BLOCK, HG_HEADS * HG_K), lambda i: (i + LEAD, 0)),
                   pl.BlockSpec((1, HG_HEADS, HG_K, HG_K), lambda i: (i, 0, 0, 0))] + push_out,
        out_shape=[jax.ShapeDtypeStruct((p, HG_HEADS * HG_K), bf16),
                   jax.ShapeDtypeStruct((nb, HG_HEADS, HG_K, HG_K), f32)] + push_shape,
        scratch_shapes=[pltpu.VMEM((HG_HEADS, HG_K, HG_K), f32)] + push_scratch,
        compiler_params=_cparams(("arbitrary",)),
    )(pa, pa, pa, pa, lbraw, ng, *shards)


def hgrn_bwd(pa, lbraw, ng, sprev, dog, nb, grads):
    n = len(grads)

    def body(hq_ref, hf_ref, hi_ref, hg_ref, lb_ref, ng_ref, sp_ref, do_ref, *rest):
        srcs, rest = rest[:n], rest[n:]
        dq_ref, df_ref, di_ref, dg_ref, dlb_ref, dng_ref = rest[:6]
        dsts, dst_ref = rest[6:6 + n], rest[6 + n]
        start, wait = _grad_push(srcs, dsts, *rest[7 + n:])
        i = pl.program_id(0)

        @pl.when(i == 0)
        def _():
            dst_ref[...] = jnp.zeros_like(dst_ref)
            dlb_ref[...] = jnp.zeros_like(dlb_ref)
            dng_ref[...] = jnp.zeros_like(dng_ref)
            start()

        valid = _row_ids(nb - 1 - i) >= PAD
        dng_sum = jnp.zeros((1, HG_K), f32)
        for h in range(HG_HEADS):
            cols = slice(h * HG_K, (h + 1) * HG_K)
            _, vjp = jax.vjp(functools.partial(_hgrn_chunk, valid), sp_ref[0, h], _head(hq_ref, h), _head(hf_ref, h),
                             _head(hi_ref, h), _head(hg_ref, h), _head(lb_ref, h), ng_ref[...])
            dst, dq, df, di, dg, dlb, dng = vjp((dst_ref[h], _head(do_ref, h)))
            dst_ref[h] = dst
            dq_ref[:, cols] = dq.astype(bf16)
            df_ref[:, cols] = df.astype(bf16)
            di_ref[:, cols] = di.astype(bf16)
            dg_ref[:, cols] = dg.astype(bf16)
            dlb_ref[:, cols] += dlb
            dng_sum = dng_sum + dng
        dng_ref[...] += dng_sum
        pl.when(i == nb - 1)(wait)

    p = (nb + LEAD) * BLOCK
    rev = lambda i: nb - 1 - i
    hw = HG_HEADS * HG_K
    blk = pl.BlockSpec((BLOCK, hw), lambda i: (rev(i) + LEAD, 0))
    wide = jax.ShapeDtypeStruct((p, hw), bf16)
    push_in, push_out, push_shape, push_scratch = _grad_push_specs(grads)
    return pl.pallas_call(
        body, name="hgrn_bwd", grid=(nb,),
        in_specs=_hgrn_in_specs(rev) + [pl.BlockSpec((1, HG_HEADS, HG_K, HG_K), lambda i: (rev(i), 0, 0, 0)), blk]
        + push_in,
        out_specs=[blk, blk, blk, blk, pl.BlockSpec((2, hw), lambda i: (0, 0)), pl.BlockSpec((1, HG_K), lambda i: (0, 0))]
        + push_out,
        out_shape=[wide, wide, wide, wide, jax.ShapeDtypeStruct((2, hw), f32), jax.ShapeDtypeStruct((1, HG_K), f32)]
        + push_shape,
        scratch_shapes=[pltpu.VMEM((HG_HEADS, HG_K, HG_K), f32)] + push_scratch,
        compiler_params=_cparams(("arbitrary",)),
    )(pa, pa, pa, pa, lbraw, ng, sprev, dog, *grads)


def _rot_half(x):
    lane = lax.broadcasted_iota(jnp.int32, x.shape, 1)
    return jnp.where(lane % HEAD_DIM < HEAD_DIM // 2, -pltpu.roll(x, BLOCK - HEAD_DIM // 2, 1),
                     pltpu.roll(x, HEAD_DIM // 2, 1))


def _rope(x, cos, sin):
    return x * cos + _rot_half(x) * sin


def _rope_transposed(g, cos, sin):
    return g * cos - _rot_half(g * sin)


def _both_halves(x, g):
    lo = lax.broadcasted_iota(jnp.int32, x.shape, 1) < HEAD_DIM
    sw = roll_lanes(x, HEAD_DIM)
    return jnp.where(lo, x, sw) if g == 0 else jnp.where(lo, sw, x)


def _attn_block(band_ok, meta_ok, q, kp, kc, vp, vc, km, vm, *sinks):
    neg = jnp.finfo(f32).min
    scale = HEAD_DIM ** -0.5
    group = ATT_HEADS // 2
    lo = lax.broadcasted_iota(jnp.int32, (BLOCK, BLOCK), 1) < HEAD_DIM
    t = lax.broadcasted_iota(jnp.int32, (group * BLOCK, BLOCK), 0) % BLOCK
    col = lax.broadcasted_iota(jnp.int32, (group * BLOCK, BLOCK), 1)
    own = col <= t
    is_sink = col == N_META
    qr = [q[:, m * BLOCK:(m + 1) * BLOCK] for m in range(ATT_HEADS // 2)]
    slabs = []
    for g in range(2):
        kp_g, kc_g, vp_g, vc_g, km_g, vm_g = [_both_halves(a, g) for a in (kp, kc, vp, vc, km, vm)]
        qs = jnp.concatenate([jnp.where(lo if h % 2 == 0 else ~lo, qr[2 * g + h // 2], 0.0) for h in range(group)],
                             axis=0)
        sink = jnp.concatenate([jnp.broadcast_to(sinks[group * g + h], (BLOCK, 1)) for h in range(group)], axis=0)
        sb = jnp.where(band_ok, jnp.where(own, mm_nt(qs, kc_g), mm_nt(qs, kp_g)) * scale, neg)
        no_keys = jnp.zeros((BLOCK - N_META, BLOCK), f32)
        sme = jnp.where(meta_ok, mm_nt(qs, jnp.concatenate([km_g, no_keys], axis=0)) * scale,
                        jnp.where(is_sink, sink, neg))
        mx = lax.stop_gradient(jnp.max(jnp.maximum(sb, sme), axis=-1, keepdims=True))
        eb, em = jnp.exp(sb - mx), jnp.exp(sme - mx)
        inv = _recip(jnp.sum(eb + em, axis=-1, keepdims=True))
        pb = eb * inv
        o = (mm(jnp.where(own, pb, 0.0), vc_g) + mm(jnp.where(own, 0.0, pb), vp_g)
             + mm(em * inv, jnp.concatenate([vm_g, no_keys], axis=0)))
        for m in range(2):
            even, odd = o[2 * m * BLOCK:(2 * m + 1) * BLOCK], o[(2 * m + 1) * BLOCK:(2 * m + 2) * BLOCK]
            slabs.append(jnp.where(lo, even, odd))
    return jnp.concatenate(slabs, axis=1)


def _attn_masks(i):
    group = ATT_HEADS // 2
    t = lax.broadcasted_iota(jnp.int32, (group * BLOCK, BLOCK), 0) % BLOCK
    s = lax.broadcasted_iota(jnp.int32, (group * BLOCK, BLOCK), 1)
    kpos = jnp.where(s <= t, i * BLOCK - PAD + s, jnp.where(i > 0, (i - 1) * BLOCK - PAD + s, -1))
    band_ok = kpos >= N_META
    qpos = i * BLOCK - PAD + lax.broadcasted_iota(jnp.int32, (group * BLOCK, 1), 0) % BLOCK
    meta_ok = (s < N_META) & (s <= qpos)
    return band_ok, meta_ok


def _attn_in_specs():
    cur, prev, first = (lambda i: i), (lambda i: jnp.maximum(i - 1, 0)), (lambda i: 0)
    kcol, vcol = V_COL // BLOCK - 1, V_COL // BLOCK
    blk = lambda rowmap, col: pl.BlockSpec((BLOCK, BLOCK), lambda i: (rowmap(i) + LEAD, col))
    return [pl.BlockSpec((BLOCK, ATT_QW), lambda i: (i + LEAD, Q_COL // ATT_QW)),
            blk(prev, kcol), blk(cur, kcol), blk(prev, vcol), blk(cur, vcol), blk(first, kcol), blk(first, vcol),
            _const_spec((ATT_HEADS, BLOCK))]


def _attn_row(n):
    return pl.BlockSpec((BLOCK, n), lambda i: (i + LEAD, 0))


def _attn_operands(q_ref, kp_ref, kc_ref, vp_ref, vc_ref, km_ref, vm_ref, sk_ref):
    args = (q_ref[...], kp_ref[...], kc_ref[...], vp_ref[...], vc_ref[...], km_ref[PAD:, :], vm_ref[PAD:, :])
    sinks = tuple(sk_ref[j:j + 1, 0:1] for j in range(ATT_HEADS))
    return args + sinks


def attn_fwd(pa, sinks8, nb, shards):
    n = len(shards)
    n_in = 8

    def body(*refs):
        srcs, o_ref, dsts = refs[n_in:n_in + n], refs[n_in + n], refs[n_in + n + 1:n_in + 2 * n + 1]
        start, wait = _shard_push(srcs, dsts, *refs[n_in + 2 * n + 1:])
        i = pl.program_id(0)
        pl.when(i == 0)(start)
        band_ok, meta_ok = _attn_masks(i)
        o_ref[...] = _attn_block(band_ok, meta_ok, *_attn_operands(*refs[:n_in])).astype(bf16)
        pl.when(i == nb - 1)(wait)

    push_in, push_out, push_shape, push_scratch = _push_specs(shards)
    return pl.pallas_call(
        body, name="attn_fwd", grid=(nb,), in_specs=_attn_in_specs() + push_in,
        out_specs=[_attn_row(ATT_QW)] + push_out,
        out_shape=[jax.ShapeDtypeStruct(((nb + LEAD) * BLOCK, ATT_QW), bf16)] + push_shape,
        scratch_shapes=push_scratch,
        compiler_params=_cparams(("arbitrary",)),
    )(pa, pa, pa, pa, pa, pa, pa, sinks8, *shards)


def attn_bwd(pa, sinks8, do, nb, grads):
    n = len(grads)

    def body(*refs):
        do_ref, srcs = refs[8], refs[9:9 + n]
        dq_ref, dkc_ref, dkp_ref, dvc_ref, dvp_ref, dkm_ref, dvm_ref, dsk_ref = refs[9 + n:17 + n]
        start, wait = _grad_push(srcs, refs[17 + n:17 + 2 * n], *refs[17 + 2 * n:])
        i = pl.program_id(0)

        @pl.when(i == 0)
        def _():
            dkm_ref[...] = jnp.zeros((N_META, BLOCK), f32)
            dvm_ref[...] = jnp.zeros((N_META, BLOCK), f32)
            dsk_ref[...] = jnp.zeros((ATT_HEADS, BLOCK), f32)
            start()

        band_ok, meta_ok = _attn_masks(i)
        _, vjp = jax.vjp(functools.partial(_attn_block, band_ok, meta_ok), *_attn_operands(*refs[:8]))
        grads = vjp(do_ref[...])
        dq_ref[...] = grads[0]
        dkp_ref[...] = grads[1]
        dkc_ref[...] = grads[2]
        dvp_ref[...] = grads[3]
        dvc_ref[...] = grads[4]
        dkm_ref[...] += grads[5]
        dvm_ref[...] += grads[6]
        for j in range(ATT_HEADS):
            dsk_ref[j:j + 1, :] += jnp.broadcast_to(grads[7 + j], (1, BLOCK))
        pl.when(i == nb - 1)(wait)

    p = (nb + LEAD) * BLOCK
    row = _attn_row(BLOCK)
    const = lambda r: pl.BlockSpec((r, BLOCK), lambda i: (0, 0))
    part = jax.ShapeDtypeStruct((p, BLOCK), f32)
    push_in, push_out, push_shape, push_scratch = _grad_push_specs(grads)
    return pl.pallas_call(
        body, name="attn_bwd", grid=(nb,),
        in_specs=_attn_in_specs() + [_attn_row(ATT_QW)] + push_in,
        out_specs=[_attn_row(ATT_QW), row, row, row, row,
                   const(N_META), const(N_META), const(ATT_HEADS)] + push_out,
        out_shape=[jax.ShapeDtypeStruct((p, ATT_QW), f32), part, part, part, part,
                   jax.ShapeDtypeStruct((N_META, BLOCK), f32), jax.ShapeDtypeStruct((N_META, BLOCK), f32),
                   jax.ShapeDtypeStruct((ATT_HEADS, BLOCK), f32)] + push_shape,
        scratch_shapes=push_scratch,
        compiler_params=_cparams(("arbitrary",)),
    )(pa, pa, pa, pa, pa, pa, pa, sinks8, do, *grads)


def _mid_forward(h0_ref, pg_ref, og, oa, wbh_ref, wba_ref, wo_ref, g1, b1):
    yh = jnp.dot(og, wbh_ref[...], preferred_element_type=f32)
    ya = jnp.dot(oa, wba_ref[...], preferred_element_type=f32)
    gh = _sigmoid(pg_ref[:, :D_MODEL])
    ga = _sigmoid(pg_ref[:, D_MODEL:])
    mixin = (gh * yh + ga * ya).astype(bf16)
    r1 = ALPHA * h0_ref[...] + jnp.dot(mixin, wo_ref[...], preferred_element_type=f32)
    xh1, rs1 = _ln_stats(r1)
    return yh, ya, gh, ga, mixin, xh1, rs1, xh1 * g1 + b1


def _mid_weight_specs():
    hw = HG_HEADS * HG_K
    return [_const_spec((hw, D_MODEL)), _const_spec((ATT_QW, D_MODEL)), _const_spec((D_MODEL, D_MODEL)),
            _const_spec((1, D_MODEL)), _const_spec((1, D_MODEL))]


def mid_front_ffn(h0, pg, og, oatt, target, wbh, wba, wout, wfi, wfo, ln1g, ln1b, ln2g, ln2b):
    def body(h0_ref, pg_ref, og_ref, oa_ref, t_ref, wbh_ref, wba_ref, wo_ref, g1_ref, b1_ref, wfi_ref, wfo_ref,
             g2_ref, b2_ref, dh1_ref, dau_ref, s_ref, dr2_ref, mix_ref, h1b_ref, ogc_ref, oac_ref,
             loss_ref, dg2_ref, db2_ref):
        i = pl.program_id(0)

        @pl.when(i == 0)
        def _():
            for r in (loss_ref, dg2_ref, db2_ref):
                r[...] = jnp.zeros_like(r)

        used = _tm_rows(i) >= LEAD * BLOCK
        og = jnp.where(used, og_ref[...], jnp.zeros_like(og_ref))
        oa = jnp.where(used, oa_ref[...], jnp.zeros_like(oa_ref))
        ogc_ref[...] = og
        oac_ref[...] = oa
        *_, mixin, _, _, h1 = _mid_forward(h0_ref, pg_ref, og, oa, wbh_ref, wba_ref, wo_ref, g1_ref[...], b1_ref[...])
        mix_ref[...] = mixin
        h1b = h1.astype(bf16)
        h1b_ref[...] = h1b
        g2, b2 = g2_ref[...], b2_ref[...]
        au = jnp.dot(h1b, wfi_ref[...], preferred_element_type=f32)
        a, u = au[:, :D_FF], au[:, D_FF:]
        sg = _sigmoid(a)
        sa = a * sg
        s = (sa * u).astype(bf16)
        s_ref[...] = s
        r2 = ALPHA * h1 + jnp.dot(s, wfo_ref[...], preferred_element_type=f32)
        xh2, rs2 = _ln_stats(r2)
        diff = jnp.where(i > 0, xh2 * g2 + b2 - t_ref[...], 0.0)
        loss_ref[...] += jnp.sum(diff * diff) * (0.5 / D_MODEL)
        dy = diff * (1.0 / D_MODEL)
        dg2_ref[...] += jnp.sum(dy * xh2, axis=0, keepdims=True)
        db2_ref[...] += jnp.sum(dy, axis=0, keepdims=True)
        dr2 = _ln_bwd(dy, xh2, rs2, g2)
        dr2b = dr2.astype(bf16)
        dr2_ref[...] = dr2b
        ds = _dot(dr2b, wfo_ref[...], 1, 1)
        da = (ds * u) * (sg * (1.0 + a * (1.0 - sg)))
        du = ds * sa
        dau = jnp.concatenate([da, du], axis=1).astype(bf16)
        dau_ref[...] = dau
        dh1_ref[...] = ALPHA * dr2 + _dot(dau, wfi_ref[...], 1, 1)

    p = h0.shape[0]
    hw = HG_HEADS * HG_K
    vec = lambda: pl.BlockSpec((1, D_MODEL), lambda i: (0, 0))
    sds = lambda n, dt: jax.ShapeDtypeStruct((p, n), dt)
    return pl.pallas_call(
        body, name="mid_front_ffn", grid=(p // TM,),
        in_specs=[_tm_row(D_MODEL), _tm_row(N_G), _tm_row(hw), _tm_row(ATT_QW), _tm_tokens()] + _mid_weight_specs()
        + [_const_spec((D_MODEL, 2 * D_FF)), _const_spec((D_FF, D_MODEL)), _const_spec((1, D_MODEL)),
           _const_spec((1, D_MODEL))],
        out_specs=[_tm_row(D_MODEL), _tm_row(2 * D_FF), _tm_row(D_FF), _tm_row(D_MODEL), _tm_row(D_MODEL),
                   _tm_row(D_MODEL), _tm_row(hw), _tm_row(ATT_QW), pl.BlockSpec((1, 1), lambda i: (0, 0)), vec(), vec()],
        out_shape=[sds(D_MODEL, f32), sds(2 * D_FF, bf16), sds(D_FF, bf16), sds(D_MODEL, bf16), sds(D_MODEL, bf16),
                   sds(D_MODEL, bf16), sds(hw, bf16), sds(ATT_QW, bf16),
                   jax.ShapeDtypeStruct((1, 1), f32)] + [jax.ShapeDtypeStruct((1, D_MODEL), f32)] * 2,
        compiler_params=_cparams(("arbitrary",), VMEM_LIMIT_ALL_WEIGHTS),
    )(h0, pg, og, oatt, target, wbh, wba, wout, ln1g, ln1b, wfi, wfo, ln2g, ln2b)


def mid_back(dh1, h0, pg, ogc, oac, wbh, wba, wout, ln1g, ln1b):
    def body(dh1_ref, h0_ref, pg_ref, og_ref, oa_ref, wbh_ref, wba_ref, wo_ref, g1_ref, b1_ref,
             dh0_ref, dpg_ref, dog_ref, doa_ref, dyh_ref, dya_ref, dr1_ref, dg1_ref, db1_ref):
        @pl.when(pl.program_id(0) == 0)
        def _():
            dg1_ref[...] = jnp.zeros_like(dg1_ref)
            db1_ref[...] = jnp.zeros_like(db1_ref)

        g1 = g1_ref[...]
        yh, ya, gh, ga, _, xh1, rs1, _ = _mid_forward(h0_ref, pg_ref, og_ref[...], oa_ref[...], wbh_ref, wba_ref,
                                                      wo_ref, g1, b1_ref[...])
        dh1 = dh1_ref[...]
        dg1_ref[...] += jnp.sum(dh1 * xh1, axis=0, keepdims=True)
        db1_ref[...] += jnp.sum(dh1, axis=0, keepdims=True)
        dr1 = _ln_bwd(dh1, xh1, rs1, g1)
        dr1b = dr1.astype(bf16)
        dr1_ref[...] = dr1b
        dh0_ref[...] = ALPHA * dr1
        dmix = _dot(dr1b, wo_ref[...], 1, 1)
        dyh = (dmix * gh).astype(bf16)
        dya = (dmix * ga).astype(bf16)
        dyh_ref[...] = dyh
        dya_ref[...] = dya
        dpg_ref[:, :D_MODEL] = (dmix * yh * gh * (1.0 - gh)).astype(bf16)
        dpg_ref[:, D_MODEL:] = (dmix * ya * ga * (1.0 - ga)).astype(bf16)
        dog_ref[...] = _dot(dyh, wbh_ref[...], 1, 1)
        doa_ref[...] = _dot(dya, wba_ref[...], 1, 1)

    p = h0.shape[0]
    hw = HG_HEADS * HG_K
    vec = lambda: pl.BlockSpec((1, D_MODEL), lambda i: (0, 0))
    sds = lambda n, dt: jax.ShapeDtypeStruct((p, n), dt)
    return pl.pallas_call(
        body, name="mid_back", grid=(p // TM,),
        in_specs=[_tm_row(D_MODEL), _tm_row(D_MODEL), _tm_row(N_G), _tm_row(hw), _tm_row(ATT_QW)] + _mid_weight_specs(),
        out_specs=[_tm_row(D_MODEL), _tm_row(N_G), _tm_row(hw), _tm_row(ATT_QW), _tm_row(D_MODEL), _tm_row(D_MODEL),
                   _tm_row(D_MODEL), vec(), vec()],
        out_shape=[sds(D_MODEL, f32), sds(N_G, bf16), sds(hw, f32), sds(ATT_QW, f32), sds(D_MODEL, bf16),
                   sds(D_MODEL, bf16), sds(D_MODEL, bf16)] + [jax.ShapeDtypeStruct((1, D_MODEL), f32)] * 2,
        compiler_params=_cparams(("arbitrary",)),
    )(dh1, h0, pg, ogc, oac, wbh, wba, wout, ln1g, ln1b)


def inproj_bwd(dh0p, dhq, dhf, dhi, dhg, daq, dkc, dkp, dvc, dvp, dkm, dvm, dpg, w_in, x, metablk, g, b, cos, sin):
    p = dh0p.shape[0]
    nbk = p // BLOCK
    per = TM // BLOCK

    def body(dh0_ref, dq_ref, df_ref, di_ref, dg_ref, daq_ref, dkc_ref, *rest):
        dkp_refs, dvc_ref, dvp_refs = rest[:per], rest[per], rest[per + 1:2 * per + 1]
        (dkm_ref, dvm_ref, dpg_ref, w_ref, x_ref, mb_ref, g_ref, b_ref, cos_ref, sin_ref,
         dproj_ref, dx_ref, dmeta_ref, dlg_ref, dlb_ref) = rest[2 * per + 1:]
        i = pl.program_id(0)

        @pl.when(i == 0)
        def _():
            dlg_ref[...] = jnp.zeros_like(dlg_ref)
            dlb_ref[...] = jnp.zeros_like(dlb_ref)

        zero_pad = jnp.zeros((TM - N_META, BLOCK), f32)
        first = i == 0
        rows = _tm_rows(i)

        def keys(cur_ref, next_refs, meta_ref):
            nxt = jnp.concatenate([jnp.where(per * i + 1 + m < nbk, next_refs[m][...], 0.0) for m in range(per)],
                                  axis=0)
            t = cur_ref[...] + nxt
            return t + jnp.where(first, jnp.concatenate([zero_pad, meta_ref[...]], axis=0), 0.0)

        cos, sin = cos_ref[...], sin_ref[...]
        unrotate = lambda t: _rope_transposed(t, cos, sin).astype(bf16)
        dproj = jnp.concatenate(
            [dq_ref[...], df_ref[...], di_ref[...], dg_ref[...]]
            + [unrotate(daq_ref[:, m * BLOCK:(m + 1) * BLOCK]) for m in range(ATT_QW // BLOCK)]
            + [unrotate(keys(dkc_ref, dkp_refs, dkm_ref)), keys(dvc_ref, dvp_refs, dvm_ref).astype(bf16),
               dpg_ref[...]], axis=1)
        dproj = jnp.where(rows >= LEAD * BLOCK, dproj, jnp.zeros_like(dproj))
        dproj_ref[...] = dproj
        valid = rows >= TM - N_META
        dh0 = jnp.where(valid, dh0_ref[...] + _dot(dproj, w_ref[...], 1, 0), 0.0)
        xb = jnp.where(first, mb_ref[...], x_ref[...])
        xh, rs = _ln_stats(xb)
        dlg_ref[...] += jnp.sum(dh0 * xh, axis=0, keepdims=True)
        dlb_ref[...] += jnp.sum(dh0, axis=0, keepdims=True)
        dx = jnp.where(valid, _ln_bwd(dh0, xh, rs, g_ref[...]), 0.0)
        dx_ref[...] = dx

        @pl.when(first)
        def _():
            dmeta_ref[...] = dx[TM - N_META:, :]

    row = _tm_row
    nxt = [pl.BlockSpec((BLOCK, BLOCK), functools.partial(lambda i, m: (jnp.minimum(per * i + 1 + m, nbk - 1), 0), m=m))
           for m in range(per)]
    hw = HG_HEADS * HG_K
    vec = lambda: pl.BlockSpec((1, D_MODEL), lambda i: (0, 0))
    return pl.pallas_call(
        body, name="inproj_bwd", grid=(p // TM,),
        in_specs=[row(D_MODEL), row(hw), row(hw), row(hw), row(hw), row(ATT_QW),
                  row(BLOCK)] + nxt + [row(BLOCK)] + nxt + [_const_spec((N_META, BLOCK)), _const_spec((N_META, BLOCK)),
                  row(N_G), _const_spec((IN_W, D_MODEL)), _tm_tokens(),
                  _const_spec((TM, D_MODEL)), _const_spec((1, D_MODEL)), _const_spec((1, D_MODEL)),
                  row(BLOCK), row(BLOCK)],
        out_specs=[row(IN_W), _tm_tokens(), pl.BlockSpec((N_META, D_MODEL), lambda i: (0, 0)), vec(), vec()],
        out_shape=[jax.ShapeDtypeStruct((p, IN_W), bf16), jax.ShapeDtypeStruct((p - TM, D_MODEL), f32),
                   jax.ShapeDtypeStruct((N_META, D_MODEL), f32),
                   jax.ShapeDtypeStruct((1, D_MODEL), f32), jax.ShapeDtypeStruct((1, D_MODEL), f32)],
        compiler_params=_cparams(("arbitrary",)),
    )(dh0p, dhq, dhf, dhi, dhg, daq, dkc, *([dkp] * per), dvc, *([dvp] * per), dkm, dvm, dpg, w_in, x, metablk, g, b,
      cos, sin)


def wgrad(a, b, name, tk, tn, tp, by_cols, out_dtype=f32):
    p, k = a.shape
    n = b.shape[1]
    nsteps = p // tp

    def body(a_ref, b_ref, o_ref, acc_ref):
        ip = pl.program_id(2)

        @pl.when(ip == 0)
        def _():
            acc_ref[...] = jnp.zeros_like(acc_ref)

        acc_ref[...] += _dot(a_ref[...], b_ref[...], 0, 0)

        @pl.when(ip == nsteps - 1)
        def _():
            for j in range(span):
                o_ref[j] = acc_ref[:, j * width:(j + 1) * width].astype(out_dtype)

    span, width = 1, tn
    if by_cols:
        shard_n = n // N_SHARD
        out_shape = (N_SHARD, k, shard_n)
        if tn >= shard_n:
            span, width = tn // shard_n, shard_n
            omap = lambda ik, jn, ip: (jn, ik, 0)
        else:
            per = shard_n // tn
            omap = lambda ik, jn, ip: (jn // per, ik, jn % per)
    else:
        out_shape = (1, k, n)
        omap = lambda ik, jn, ip: (0, ik, jn)
    return pl.pallas_call(
        body, name=name, grid=(k // tk, n // tn, nsteps),
        in_specs=[pl.BlockSpec((tp, tk), lambda ik, jn, ip: (ip, ik)),
                  pl.BlockSpec((tp, tn), lambda ik, jn, ip: (ip, jn))],
        out_specs=pl.BlockSpec((span, tk, width), omap),
        out_shape=jax.ShapeDtypeStruct(out_shape, out_dtype),
        scratch_shapes=[pltpu.VMEM((tk, tn), f32)],
        compiler_params=_cparams(("parallel", "parallel", "arbitrary")),
    )(a, b)


def _adamw_math(w, g, m, v):
    mn = ADAM_B1 * m + (1.0 - ADAM_B1) * g
    vn = ADAM_B2 * v + (1.0 - ADAM_B2) * (g * g)
    m_hat = mn / (1.0 - ADAM_B1 ** ADAM_STEP)
    v_hat = vn / (1.0 - ADAM_B2 ** ADAM_STEP)
    return -ADAM_LR * (m_hat / (jnp.sqrt(v_hat) + ADAM_EPS) + ADAM_WD * w), mn, vn


def adamw(w, g, m, v, name):
    r, c = w.shape
    tr = r
    for cand in (256, 176, 152, 128):
        if r > cand and r % cand == 0:
            tr = cand
            break

    def body(w_ref, g_ref, m_ref, v_ref, go_ref, d_ref, mo_ref, vo_ref):
        gg = g_ref[...]
        go_ref[...] = gg
        d_ref[...], mo_ref[...], vo_ref[...] = _adamw_math(w_ref[...], gg, m_ref[...], v_ref[...])

    spec = pl.BlockSpec((tr, c), lambda i: (i, 0))
    sds = jax.ShapeDtypeStruct((r, c), f32)
    return pl.pallas_call(
        body, name=name, grid=(r // tr,), in_specs=[spec] * 4, out_specs=[spec] * 4, out_shape=[sds] * 4,
        compiler_params=_cparams(("parallel",)),
    )(w, g, m, v)


def adamw_small(ws, gs, ms, vs):
    n = len(ws)

    def body(*refs):
        ins, outs = refs[:4 * n], refs[4 * n:]
        for k in range(n):
            outs[k][...], outs[n + k][...], outs[2 * n + k][...] = _adamw_math(
                ins[k][...], ins[n + k][...], ins[2 * n + k][...], ins[3 * n + k][...])

    out = pl.pallas_call(body, name="adamw_small",
                         out_shape=[jax.ShapeDtypeStruct(w.shape, f32) for w in ws] * 3)(*ws, *gs, *ms, *vs)
    return out[:n], out[n:2 * n], out[2 * n:]


def _me():
    return lax.axis_index("x"), lax.axis_index("y"), lax.axis_index("c")


def _chip_peer(x, y, c, k):
    return (x ^ (k >> 1), y ^ (k & 1), c)


ANY = pl.BlockSpec(memory_space=pl.ANY)


def gather_weights(now, later):
    n, n_later = len(now), len(later)
    out_dtypes = [bf16 if s.size > 16 * 256 else f32 for s in now]
    halves = [(2, s.shape[0] // 2, s.shape[1]) for s in now]

    def body(*refs):
        ins, later_ins = refs[:n], refs[n:n + n_later]
        outs, later_outs = refs[n + n_later:2 * n + n_later], refs[2 * n + n_later:2 * (n + n_later)]
        stage = refs[2 * (n + n_later):3 * n + 2 * n_later]
        send_sems, recv_sems, pass_send_sems, pass_recv_sems, local_sems = refs[3 * n + 2 * n_later:]
        x, y, c = _me()
        j = 2 * x + y
        sibling = (x, y, 1 - c)

        def over_ici(w, k, slot):
            return pltpu.make_async_remote_copy(
                src_ref=stage[w].at[c], dst_ref=outs[w].at[slot, c], send_sem=send_sems.at[w, k - 1],
                recv_sem=recv_sems.at[w, k - 1], device_id=_chip_peer(x, y, c, k), device_id_type=MESH)

        def passed_on(w, k, half):
            return pltpu.make_async_remote_copy(
                src_ref=outs[w].at[j ^ k, half], dst_ref=outs[w].at[j ^ k, half], send_sem=pass_send_sems.at[w, k - 1],
                recv_sem=pass_recv_sems.at[w, k - 1], device_id=sibling, device_id_type=MESH)

        for w in range(n):
            stage[w][...] = ins[w][...].astype(out_dtypes[w]).reshape(halves[w])
        locs = []
        for w in range(n):
            loc = pltpu.make_async_copy(stage[w], outs[w].at[j], local_sems.at[w])
            loc.start()
            locs.append(loc)
            for k in (1, 2, 3):
                over_ici(w, k, j).start()
        for w in range(n_later):
            later_outs[w][...] = later_ins[w][...].astype(bf16)
        for w in range(n):
            for k in (1, 2, 3):
                over_ici(w, k, j ^ k).wait_recv()
                passed_on(w, k, c).start()
        for w in range(n):
            for k in (1, 2, 3):
                passed_on(w, k, 1 - c).wait_recv()
        for w in range(n):
            for k in (1, 2, 3):
                over_ici(w, k, j).wait_send()
                passed_on(w, k, c).wait_send()
        for loc in locs:
            loc.wait()

    vmem = pl.BlockSpec(memory_space=pltpu.VMEM)
    sem3 = pltpu.SemaphoreType.DMA((n, 3))
    return pl.pallas_call(
        body, name="gather_weights",
        in_specs=[vmem] * (n + n_later), out_specs=[ANY] * n + [vmem] * n_later,
        out_shape=[jax.ShapeDtypeStruct((N_SHARD,) + h, dt) for h, dt in zip(halves, out_dtypes)]
        + [jax.ShapeDtypeStruct(s.shape, bf16) for s in later],
        scratch_shapes=[pltpu.VMEM(h, dt) for h, dt in zip(halves, out_dtypes)]
        + [sem3, sem3, sem3, sem3, pltpu.SemaphoreType.DMA((n,))],
        compiler_params=pltpu.CompilerParams(vmem_limit_bytes=VMEM_LIMIT),
    )(*now, *later)


def _shard_push(srcs, dsts, send_sems, recv_sems, local_sems):
    def remote(w, k, slot):
        x, y, c = _me()
        return pltpu.make_async_remote_copy(
            src_ref=srcs[w], dst_ref=dsts[w].at[slot], send_sem=send_sems.at[w, k - 1],
            recv_sem=recv_sems.at[w, k - 1], device_id=_chip_peer(x, y, c, k), device_id_type=MESH)

    def local(w):
        x, y, _ = _me()
        return pltpu.make_async_copy(srcs[w], dsts[w].at[2 * x + y], local_sems.at[w])

    def start():
        x, y, _ = _me()
        for w in range(len(srcs)):
            local(w).start()
            for k in (1, 2, 3):
                remote(w, k, 2 * x + y).start()

    def wait():
        x, y, _ = _me()
        for w in range(len(srcs)):
            for k in (1, 2, 3):
                remote(w, k, (2 * x + y) ^ k).wait_recv()
        for w in range(len(srcs)):
            for k in (1, 2, 3):
                remote(w, k, 2 * x + y).wait_send()
            local(w).wait()

    return start, wait


def _grad_push(srcs, dsts, send_sems, recv_sems):
    def copy(w, k):
        x, y, c = _me()
        px, py, pc = x ^ (k >> 2), y ^ ((k >> 1) & 1), c ^ (k & 1)
        return pltpu.make_async_remote_copy(
            src_ref=srcs[w].at[2 * px + py, pc], dst_ref=dsts[w].at[k - 1], send_sem=send_sems.at[w, k - 1],
            recv_sem=recv_sems.at[w, k - 1], device_id=(px, py, pc), device_id_type=MESH)

    def start():
        for w in range(len(srcs)):
            for k in range(1, N_DEV):
                copy(w, k).start()

    def wait():
        for w in range(len(srcs)):
            for k in range(1, N_DEV):
                copy(w, k).wait_recv()
        for w in range(len(srcs)):
            for k in range(1, N_DEV):
                copy(w, k).wait_send()

    return start, wait


def _grad_push_specs(grads):
    n = len(grads)
    return ([ANY] * n, [ANY] * n, [jax.ShapeDtypeStruct((N_DEV - 1,) + g.shape[2:], g.dtype) for g in grads],
            [pltpu.SemaphoreType.DMA((n, N_DEV - 1)), pltpu.SemaphoreType.DMA((n, N_DEV - 1))])


def add_eight(own, parts, jc_idx, name):
    _, half, c = parts.shape
    tr = half // 2 if (half // 2) % 16 == 0 else half

    def body(jc_ref, own_ref, p_ref, out_ref):
        acc = own_ref[0, 0].astype(f32)
        for k in range(N_DEV - 1):
            acc = acc + p_ref[k].astype(f32)
        out_ref[0] = acc

    return pl.pallas_call(
        body, name=name,
        grid_spec=pltpu.PrefetchScalarGridSpec(
            num_scalar_prefetch=1, grid=(half // tr,),
            in_specs=[pl.BlockSpec((1, 1, tr, c), lambda t, jc: (jc[0], jc[1], t, 0)),
                      pl.BlockSpec((N_DEV - 1, tr, c), lambda t, jc: (0, t, 0))],
            out_specs=pl.BlockSpec((1, tr, c), lambda t, jc: (jc[1], t, 0))),
        out_shape=jax.ShapeDtypeStruct((2, half, c), f32),
        compiler_params=_cparams(("parallel",)),
    )(jc_idx, own, parts)


def _push_specs(shards):
    n = len(shards)
    return ([ANY] * n, [ANY] * n, [jax.ShapeDtypeStruct((N_SHARD,) + s.shape, s.dtype) for s in shards],
            [pltpu.SemaphoreType.DMA((n, 3)), pltpu.SemaphoreType.DMA((n, 3)), pltpu.SemaphoreType.DMA((n,))])


def pair_exchange_halves(grads, small):
    n = len(grads)

    def body(*refs):
        ins, small_ref = refs[:n], refs[n]
        outs, gath = refs[n + 1:2 * n + 1], refs[2 * n + 1]
        send_sems, recv_sems, s_send, s_recv, local_sem = refs[2 * n + 2:]
        x, y, c = _me()
        me = 4 * x + 2 * y + c
        sends = []
        for w in range(n):
            half = ins[w].shape[1] // 2
            cp = pltpu.make_async_remote_copy(
                src_ref=ins[w].at[:, pl.ds((1 - c) * half, half), :], dst_ref=outs[w],
                send_sem=send_sems.at[w], recv_sem=recv_sems.at[w], device_id=(x, y, 1 - c), device_id_type=MESH)
            cp.start()
            sends.append(cp)
        loc = pltpu.make_async_copy(small_ref, gath.at[me], local_sem)
        loc.start()
        for k in range(1, N_DEV):
            cp = pltpu.make_async_remote_copy(
                src_ref=small_ref, dst_ref=gath.at[me], send_sem=s_send.at[k - 1], recv_sem=s_recv.at[k - 1],
                device_id=(x ^ (k >> 2), y ^ ((k >> 1) & 1), c ^ (k & 1)), device_id_type=MESH)
            cp.start()
            sends.append(cp)
        for w in range(n):
            half = ins[w].shape[1] // 2
            pltpu.make_async_remote_copy(
                src_ref=ins[w].at[:, pl.ds(0, half), :], dst_ref=outs[w], send_sem=send_sems.at[w],
                recv_sem=recv_sems.at[w], device_id=(x, y, 1 - c), device_id_type=MESH).wait_recv()
        for k in range(1, N_DEV):
            pltpu.make_async_remote_copy(
                src_ref=small_ref, dst_ref=gath.at[me ^ k], send_sem=s_send.at[k - 1], recv_sem=s_recv.at[k - 1],
                device_id=(x ^ (k >> 2), y ^ ((k >> 1) & 1), c ^ (k & 1)), device_id_type=MESH).wait_recv()
        for cp in sends:
            cp.wait_send()
        loc.wait()

    return pl.pallas_call(
        body, name="pair_exchange_halves", in_specs=[ANY] * (n + 1), out_specs=[ANY] * (n + 1),
        out_shape=[jax.ShapeDtypeStruct((g.shape[0], g.shape[1] // 2, g.shape[2]), f32) for g in grads]
        + [jax.ShapeDtypeStruct((N_DEV,) + small.shape, f32)],
        scratch_shapes=[pltpu.SemaphoreType.DMA((n,)), pltpu.SemaphoreType.DMA((n,)),
                        pltpu.SemaphoreType.DMA((N_DEV - 1,)), pltpu.SemaphoreType.DMA((N_DEV - 1,)),
                        pltpu.SemaphoreType.DMA],
    )(*grads, small)


def chip_exchange(sums):
    n = len(sums)

    def body(*refs):
        ins, outs = refs[:n], refs[n:2 * n]
        send_sems, recv_sems = refs[2 * n:]
        x, y, c = _me()
        j = 2 * x + y
        sends = []
        for w in range(n):
            for k in (1, 2, 3):
                cp = pltpu.make_async_remote_copy(
                    src_ref=ins[w].at[j ^ k], dst_ref=outs[w].at[k - 1], send_sem=send_sems.at[w, k - 1],
                    recv_sem=recv_sems.at[w, k - 1], device_id=_chip_peer(x, y, c, k), device_id_type=MESH)
                cp.start()
                sends.append(cp)
        for w in range(n):
            for k in (1, 2, 3):
                pltpu.make_async_remote_copy(
                    src_ref=ins[w].at[0], dst_ref=outs[w].at[k - 1], send_sem=send_sems.at[w, k - 1],
                    recv_sem=recv_sems.at[w, k - 1], device_id=_chip_peer(x, y, c, k), device_id_type=MESH).wait_recv()
        for cp in sends:
            cp.wait_send()

    return pl.pallas_call(
        body, name="chip_exchange", in_specs=[ANY] * n, out_specs=[ANY] * n,
        out_shape=[jax.ShapeDtypeStruct((N_SHARD - 1,) + s.shape[1:], s.dtype) for s in sums],
        scratch_shapes=[pltpu.SemaphoreType.DMA((n, 3)), pltpu.SemaphoreType.DMA((n, 3))],
    )(*sums)


def pair_exchange_results(halves):
    n = len(halves)

    def body(*refs):
        ins, outs = refs[:n], refs[n:2 * n]
        send_sems, recv_sems = refs[2 * n:]
        x, y, c = _me()
        sends = []
        for w in range(n):
            cp = pltpu.make_async_remote_copy(
                src_ref=ins[w].at[c], dst_ref=outs[w].at[c], send_sem=send_sems.at[w], recv_sem=recv_sems.at[w],
                device_id=(x, y, 1 - c), device_id_type=MESH)
            cp.start()
            sends.append(cp)
        for w in range(n):
            pltpu.make_async_remote_copy(
                src_ref=ins[w].at[c], dst_ref=outs[w].at[1 - c], send_sem=send_sems.at[w],
                recv_sem=recv_sems.at[w], device_id=(x, y, 1 - c), device_id_type=MESH).wait_recv()
        for cp in sends:
            cp.wait_send()

    return pl.pallas_call(
        body, name="pair_exchange_results", in_specs=[ANY] * n, out_specs=[ANY] * n,
        out_shape=[jax.ShapeDtypeStruct(h.shape, f32) for h in halves],
        input_output_aliases={w: w for w in range(n)},
        scratch_shapes=[pltpu.SemaphoreType.DMA((n,)), pltpu.SemaphoreType.DMA((n,))],
    )(*halves)


def add_pair(grad, other, c_idx, name):
    _, r, c = grad.shape
    half = r // 2
    tr = half // 2 if (half // 2) % 8 == 0 else half
    per = half // tr

    def body(c_ref, g_ref, o_ref, out_ref):
        out_ref[...] = (g_ref[...] + o_ref[...]).astype(bf16)

    return pl.pallas_call(
        body, name=name,
        grid_spec=pltpu.PrefetchScalarGridSpec(
            num_scalar_prefetch=1, grid=(N_SHARD, per),
            in_specs=[pl.BlockSpec((1, tr, c), lambda j, t, cr: (j, cr[0] * per + t, 0)),
                      pl.BlockSpec((1, tr, c), lambda j, t, cr: (j, t, 0))],
            out_specs=pl.BlockSpec((1, tr, c), lambda j, t, cr: (j, t, 0))),
        out_shape=jax.ShapeDtypeStruct((N_SHARD, half, c), bf16),
        compiler_params=_cparams(("parallel", "parallel")),
    )(c_idx, grad, other)


def add_four(own, parts, jc_idx, name):
    _, half, c = parts.shape
    tr = half // 2 if (half // 2) % 8 == 0 else half

    def body(jc_ref, own_ref, p_ref, out_ref):
        acc = own_ref[0].astype(f32)
        for k in range(N_SHARD - 1):
            acc = acc + p_ref[k].astype(f32)
        out_ref[0] = acc

    return pl.pallas_call(
        body, name=name,
        grid_spec=pltpu.PrefetchScalarGridSpec(
            num_scalar_prefetch=1, grid=(half // tr,),
            in_specs=[pl.BlockSpec((1, tr, c), lambda t, jc: (jc[0], t, 0)),
                      pl.BlockSpec((N_SHARD - 1, tr, c), lambda t, jc: (0, t, 0))],
            out_specs=pl.BlockSpec((1, tr, c), lambda t, jc: (jc[1], t, 0))),
        out_shape=jax.ShapeDtypeStruct((2, half, c), f32),
        compiler_params=_cparams(("parallel",)),
    )(jc_idx, own, parts)


def sum_devices(gathered):
    def body(g_ref, out_ref):
        acc = g_ref[0]
        for d in range(1, N_DEV):
            acc = acc + g_ref[d]
        out_ref[...] = acc

    return pl.pallas_call(body, name="sum_devices", out_shape=jax.ShapeDtypeStruct(gathered.shape[1:], f32))(gathered)


def _rows128(a, rows):
    flat = a.reshape(-1, BLOCK) if a.size % BLOCK == 0 else jnp.pad(a.reshape(1, -1), ((0, 0), (0, BLOCK - a.size)))
    return jnp.pad(flat, ((0, rows - flat.shape[0]), (0, 0)))


def kernel(x, meta_tokens, ln_emb_g, ln_emb_b, w_in, hg_lower_bounds, hg_norm_g, attn_sinks, w_branch_hg, w_branch_attn, w_out, ln1_g, ln1_b, w_ffn_in, w_ffn_out, ln2_g, ln2_b, loss_target, m_meta_tokens, m_ln_emb_g, m_ln_emb_b, m_w_in, m_hg_lower_bounds, m_hg_norm_g, m_attn_sinks, m_w_branch_hg, m_w_branch_attn, m_w_out, m_ln1_g, m_ln1_b, m_w_ffn_in, m_w_ffn_out, m_ln2_g, m_ln2_b, v_meta_tokens, v_ln_emb_g, v_ln_emb_b, v_w_in, v_hg_lower_bounds, v_hg_norm_g, v_attn_sinks, v_w_branch_hg, v_w_branch_attn, v_w_out, v_ln1_g, v_ln1_b, v_w_ffn_in, v_w_ffn_out, v_ln2_g, v_ln2_b):
    seq = x.shape[1]
    nb = seq // BLOCK + 1
    xs = x[0]
    ts = loss_target[0]
    ix, iy, ic = _me()
    shard = 2 * ix + iy
    vec = lambda a: a.reshape(1, D_MODEL)

    w_in_t = jnp.swapaxes(w_in[0], 0, 1)
    g_in, g_meta, s_bh, s_ba, s_out, s_fi, s_fo = gather_weights(
        [w_in_t, meta_tokens], [w_branch_hg[0], w_branch_attn[0], w_out[0], w_ffn_in[0], w_ffn_out[0]])
    by_cols = lambda g: g.reshape(N_SHARD, -1, g.shape[-1]).transpose(1, 0, 2).reshape(-1, N_SHARD * g.shape[-1])
    wf_in = g_in.reshape(IN_W, D_MODEL)
    metablk = jnp.pad(by_cols(g_meta), ((TM - N_META, 0), (0, 0)))

    pos = jnp.arange((nb + LEAD) * BLOCK, dtype=jnp.int32) - (LEAD * BLOCK + PAD)
    half = HEAD_DIM // 2
    inv = ROPE_THETA ** (-jnp.arange(half, dtype=f32) / half)
    ang = pos.astype(f32)[:, None] * inv[None, :]
    cos = jnp.tile(jnp.cos(ang), (1, BLOCK // half))
    sin = jnp.tile(jnp.sin(ang), (1, BLOCK // half))
    sinks8 = jnp.broadcast_to(attn_sinks.reshape(ATT_HEADS, 1), (ATT_HEADS, BLOCK))
    ng = hg_norm_g.reshape(1, HG_K)

    h0, h0b, pa, pg = emb_inproj(xs, metablk, vec(ln_emb_g), vec(ln_emb_b), wf_in, cos, sin)
    og, sprev, g_fi, g_out = hgrn_fwd(pa, hg_lower_bounds, ng, nb, [s_fi, s_out])
    oatt, g_fo, g_bh, g_ba = attn_fwd(pa, sinks8, nb, [s_fo, s_bh, s_ba])
    wf_bh, wf_ba, wf_fi = by_cols(g_bh), by_cols(g_ba), by_cols(g_fi)
    wf_out = g_out.reshape(D_MODEL, D_MODEL)
    wf_fo = g_fo.reshape(D_FF, D_MODEL)
    dh1, dau, sact, dr2, mixin, h1b, og, oatt, loss_part, dg2, db2 = mid_front_ffn(
        h0, pg, og, oatt, ts, wf_bh, wf_ba, wf_out, wf_fi, wf_fo, ln1_g, ln1_b, ln2_g, ln2_b)
    dh0p, dpg, dog, doa, dyh, dya, dr1, dg1, db1 = mid_back(dh1, h0, pg, og, oatt, wf_bh, wf_ba, wf_out, ln1_g, ln1_b)
    tp = max(t for t in (768, 512, TM) if h0.shape[0] % t == 0)
    pieces = lambda g: g.reshape(N_SHARD, 2, -1, g.shape[-1])
    gb_bh = pieces(wgrad(og, dyh, "wgrad_bh", 512, D_MODEL, tp, True, bf16))
    gb_ba = pieces(wgrad(oatt, dya, "wgrad_ba", 512, D_MODEL, tp, True, bf16))
    gb_out = pieces(wgrad(mixin, dr1, "wgrad_out", D_MODEL, D_MODEL, tp, False, bf16))
    gb_fi = pieces(wgrad(h1b, dau, "wgrad_fi", D_MODEL, D_FF, tp, True, bf16))
    gb_fo = pieces(wgrad(sact, dr2, "wgrad_fo", D_FF // 2, D_MODEL, tp, False, bf16))
    dhq, dhf, dhi, dhg, dlb4, dng, r_fi, r_fo = hgrn_bwd(pa, hg_lower_bounds, ng, sprev, dog, nb, [gb_fi, gb_fo])
    daq, dkc, dkp, dvc, dvp, dkm, dvm, dsk, r_out, r_bh, r_ba = attn_bwd(pa, sinks8, doa, nb,
                                                                         [gb_out, gb_bh, gb_ba])
    dproj, dx, dmeta, dlg, dlb = inproj_bwd(dh0p, dhq, dhf, dhi, dhg, daq, dkc, dkp, dvc, dvp, dkm, dvm, dpg,
                                      wf_in, xs, metablk, vec(ln_emb_g), vec(ln_emb_b), cos, sin)
    gw_in = wgrad(dproj, h0b, "wgrad_in", IN_W // 2, D_MODEL, tp, False).reshape(N_SHARD, -1, D_MODEL)

    parts = [(dlg, 8), (dlb, 8), (dlb4, 8), (dng, 8), (dsk[:, 0], 8),
             (dg1, 8), (db1, 8), (dg2, 8), (db2, 8), (dmeta, BLOCK), (loss_part, 8)]
    small = jnp.concatenate([_rows128(a, r) for a, r in parts], axis=0)

    c_idx = jnp.reshape(ic, (1,)).astype(jnp.int32)
    jc_idx = jnp.stack([shard, ic]).astype(jnp.int32)
    other_in, gathered = pair_exchange_halves([gw_in], small)
    sum_in = add_pair(gw_in, other_in, c_idx, "add_pair_in")
    quad_in, = chip_exchange([sum_in])
    halves = [add_four(sum_in, quad_in, jc_idx, "add_four_in")]
    halves += [add_eight(g, r, jc_idx, "add_eight_" + nm) for nm, g, r in
               (("bh", gb_bh, r_bh), ("ba", gb_ba, r_ba), ("out", gb_out, r_out), ("fi", gb_fi, r_fi),
                ("fo", gb_fo, r_fo))]
    red = [r.reshape(-1, r.shape[-1]) for r in pair_exchange_results(halves)]
    small_sum = sum_devices(gathered)

    offs, acc = [], 0
    for _, r in parts:
        offs.append(acc)
        acc += r
    take = lambda n, size: small_sum[offs[n]:offs[n] + parts[n][1]].reshape(-1)[:size]
    g_meta_full = take(9, N_META * D_MODEL).reshape(N_META, D_MODEL)
    g_small = {
        "meta_tokens": lax.dynamic_slice_in_dim(g_meta_full, shard * (D_MODEL // N_SHARD), D_MODEL // N_SHARD, axis=1),
        "ln_emb_g": take(0, D_MODEL), "ln_emb_b": take(1, D_MODEL),
        "hg_lower_bounds": take(2, 2 * HG_HEADS * HG_K).reshape(2, HG_HEADS * HG_K),
        "hg_norm_g": take(3, HG_K).reshape(1, HG_K), "attn_sinks": take(4, ATT_HEADS).reshape(1, ATT_HEADS),
        "ln1_g": take(5, D_MODEL).reshape(1, D_MODEL), "ln1_b": take(6, D_MODEL).reshape(1, D_MODEL),
        "ln2_g": take(7, D_MODEL).reshape(1, D_MODEL), "ln2_b": take(8, D_MODEL).reshape(1, D_MODEL),
    }
    g_big = {"w_in": red[0], "w_branch_hg": red[1], "w_branch_attn": red[2], "w_out": red[3],
             "w_ffn_in": red[4], "w_ffn_out": red[5]}

    names = ["meta_tokens", "ln_emb_g", "ln_emb_b", "w_in", "hg_lower_bounds", "hg_norm_g", "attn_sinks",
             "w_branch_hg", "w_branch_attn", "w_out", "ln1_g", "ln1_b", "w_ffn_in", "w_ffn_out", "ln2_g", "ln2_b"]
    given = dict(
        meta_tokens=(meta_tokens, m_meta_tokens, v_meta_tokens), ln_emb_g=(ln_emb_g, m_ln_emb_g, v_ln_emb_g),
        ln_emb_b=(ln_emb_b, m_ln_emb_b, v_ln_emb_b), w_in=(w_in, m_w_in, v_w_in),
        hg_lower_bounds=(hg_lower_bounds, m_hg_lower_bounds, v_hg_lower_bounds),
        hg_norm_g=(hg_norm_g, m_hg_norm_g, v_hg_norm_g), attn_sinks=(attn_sinks, m_attn_sinks, v_attn_sinks),
        w_branch_hg=(w_branch_hg, m_w_branch_hg, v_w_branch_hg),
        w_branch_attn=(w_branch_attn, m_w_branch_attn, v_w_branch_attn), w_out=(w_out, m_w_out, v_w_out),
        ln1_g=(ln1_g, m_ln1_g, v_ln1_g), ln1_b=(ln1_b, m_ln1_b, v_ln1_b), w_ffn_in=(w_ffn_in, m_w_ffn_in, v_w_ffn_in),
        w_ffn_out=(w_ffn_out, m_w_ffn_out, v_w_ffn_out), ln2_g=(ln2_g, m_ln2_g, v_ln2_g), ln2_b=(ln2_b, m_ln2_b, v_ln2_b))
    two_d = lambda a: a.reshape(8, BLOCK) if a.ndim == 1 else a.reshape(a.shape[-2], a.shape[-1])
    small_names = [nm for nm in names if nm not in g_big]
    small_d, small_m, small_v = adamw_small([two_d(given[nm][0]) for nm in small_names],
                                            [two_d(g_small[nm]) for nm in small_names],
                                            [two_d(given[nm][1]) for nm in small_names],
                                            [two_d(given[nm][2]) for nm in small_names])
    out_g, out_d, out_m, out_v = [], [], [], []
    for nm in names:
        w, m, v = given[nm]
        shape = w.shape
        if nm == "w_in":
            t = lambda a: jnp.swapaxes(two_d(a), 0, 1)
            g, d, mn, vn = [t(a) for a in adamw(t(w), g_big[nm], t(m), t(v), "adamw_" + nm)]
        elif nm in g_big:
            g, d, mn, vn = adamw(two_d(w), g_big[nm], two_d(m), two_d(v), "adamw_" + nm)
        else:
            k = small_names.index(nm)
            g, d, mn, vn = g_small[nm], small_d[k], small_m[k], small_v[k]
        out_g.append(g.reshape(shape))
        out_d.append(d.reshape(shape))
        out_m.append(mn.reshape(shape))
        out_v.append(vn.reshape(shape))

    loss = take(10, 1)[0]
    grad_x = dx.reshape(x.shape)
    return (loss, grad_x, *out_g, *out_d, *out_m, *out_v)
```

```python
import functools

import jax
import jax.numpy as jnp
from jax import lax
from jax.experimental import pallas as pl
from jax.experimental.pallas import tpu as pltpu

f32 = jnp.float32
bf16 = jnp.bfloat16

D_MODEL = 1024
BLOCK = 128
N_META = 16
PAD = BLOCK - N_META
HG_HEADS = 4
HG_K = 128
ATT_HEADS = 8
HEAD_DIM = 64
ATT_QW = ATT_HEADS * HEAD_DIM
D_FF = 2816
EPS = 1e-5
ALPHA = 2.0 ** 0.25
ROPE_THETA = 10000.0
N_A = 2816
N_G = 2048
IN_W = N_A + N_G
N_SHARD = 4
N_DEV = 8

ADAM_LR = 0.001
ADAM_B1 = 0.9
ADAM_B2 = 0.999
ADAM_EPS = 1e-08
ADAM_WD = 0.01
ADAM_STEP = 10

TM = 256
LEAD = TM // BLOCK - 1

VMEM_LIMIT = 56 * 1024 * 1024
VMEM_LIMIT_ALL_WEIGHTS = 62 * 1024 * 1024
MESH = pl.DeviceIdType.MESH


def _cparams(sem, vmem=VMEM_LIMIT):
    return pltpu.CompilerParams(dimension_semantics=sem, vmem_limit_bytes=vmem)


def _const_spec(shape):
    zeros = (0,) * len(shape)
    return pl.BlockSpec(shape, lambda *_: zeros, pipeline_mode=pl.Buffered(1))


def _dot(a, b, ca, cb):
    return lax.dot_general(a.astype(bf16), b.astype(bf16), (((ca,), (cb,)), ((), ())),
                           preferred_element_type=f32)


@jax.custom_vjp
def mm(a, b):
    return _dot(a, b, 1, 0)


mm.defvjp(lambda a, b: (_dot(a, b, 1, 0), (a, b)),
          lambda r, g: (_dot(g, r[1], 1, 1), _dot(r[0], g, 0, 0)))


@jax.custom_vjp
def mm_nt(a, b):
    return _dot(a, b, 1, 1)


mm_nt.defvjp(lambda a, b: (_dot(a, b, 1, 1), (a, b)),
             lambda r, g: (_dot(g, r[1], 1, 0), _dot(g, r[0], 0, 0)))


@jax.custom_vjp
def mm_tn(a, b):
    return _dot(a, b, 0, 0)


mm_tn.defvjp(lambda a, b: (_dot(a, b, 0, 0), (a, b)),
             lambda r, g: (_dot(r[1], g, 1, 1), _dot(r[0], g, 1, 0)))


@functools.partial(jax.custom_vjp, nondiff_argnums=(1,))
def roll_lanes(x, shift):
    return pltpu.roll(x, shift, 1)


roll_lanes.defvjp(lambda x, shift: (pltpu.roll(x, shift, 1), None),
                  lambda shift, _, g: (pltpu.roll(g, (128 - shift) % 128, 1),))


@jax.custom_vjp
def _sigmoid(x):
    return 1.0 / (1.0 + jnp.exp(-x))


def _sigmoid_fwd(x):
    s = 1.0 / (1.0 + jnp.exp(-x))
    return s, s


_sigmoid.defvjp(_sigmoid_fwd, lambda s, g: (g * s * (1.0 - s),))


@jax.custom_vjp
def _recip(x):
    return 1.0 / x


def _recip_fwd(x):
    r = 1.0 / x
    return r, r


_recip.defvjp(_recip_fwd, lambda r, g: (-g * r * r,))


def _ln_stats(x):
    mu = jnp.mean(x, axis=-1, keepdims=True)
    xc = x - mu
    var = jnp.mean(xc * xc, axis=-1, keepdims=True)
    rs = lax.rsqrt(var + EPS)
    return xc * rs, rs


def _ln_bwd(dy, xh, rs, g):
    dxh = dy * g
    m1 = jnp.mean(dxh, axis=-1, keepdims=True)
    m2 = jnp.mean(dxh * xh, axis=-1, keepdims=True)
    return rs * (dxh - m1 - xh * m2)


def _row_ids(i):
    return i * BLOCK + lax.broadcasted_iota(jnp.int32, (BLOCK, 1), 0)


def _tm_rows(i):
    return i * TM + lax.broadcasted_iota(jnp.int32, (TM, 1), 0)


def _tm_row(n):
    return pl.BlockSpec((TM, n), lambda i: (i, 0))


def _tm_tokens():
    return pl.BlockSpec((TM, D_MODEL), lambda i: (jnp.maximum(i - 1, 0), 0))


Q_COL, V_COL = 4 * HG_HEADS * HG_K, N_A - BLOCK


def emb_inproj(x, metablk, g, b, w_in, cos, sin):
    nsteps = x.shape[0] // TM + 1

    def body(x_ref, mb_ref, g_ref, b_ref, w_ref, cos_ref, sin_ref, h0_ref, h0b_ref, pa_ref, pg_ref):
        i = pl.program_id(0)
        xb = jnp.where(i == 0, mb_ref[...], x_ref[...])
        xh, _ = _ln_stats(xb)
        y = xh * g_ref[...] + b_ref[...]
        y = jnp.where(_tm_rows(i) >= TM - N_META, y, 0.0)
        h0_ref[...] = y
        yb = y.astype(bf16)
        h0b_ref[...] = yb
        pa = _dot(yb, w_ref[:N_A, :], 1, 1)
        cos, sin = cos_ref[...], sin_ref[...]
        pa_ref[:, :Q_COL] = pa[:, :Q_COL]
        for c0 in range(Q_COL, V_COL, BLOCK):
            pa_ref[:, c0:c0 + BLOCK] = _rope(pa[:, c0:c0 + BLOCK], cos, sin)
        pa_ref[:, V_COL:] = pa[:, V_COL:]
        pg_ref[...] = _dot(yb, w_ref[N_A:, :], 1, 1)

    p = nsteps * TM
    row = _tm_row
    return pl.pallas_call(
        body, name="emb_inproj", grid=(nsteps,),
        in_specs=[_tm_tokens(),
                  _const_spec((TM, D_MODEL)), _const_spec((1, D_MODEL)), _const_spec((1, D_MODEL)),
                  _const_spec((IN_W, D_MODEL)), _tm_row(BLOCK), _tm_row(BLOCK)],
        out_specs=[row(D_MODEL), row(D_MODEL), row(N_A), row(N_G)],
        out_shape=[jax.ShapeDtypeStruct((p, D_MODEL), f32), jax.ShapeDtypeStruct((p, D_MODEL), bf16),
                   jax.ShapeDtypeStruct((p, N_A), f32), jax.ShapeDtypeStruct((p, N_G), f32)],
        compiler_params=_cparams(("parallel",)),
    )(x, metablk, g, b, w_in, cos, sin)


def _hgrn_chunk(valid, st, hq, hf, hi, hg, lbraw, ng):
    lb = _sigmoid(lbraw[0:1] - lbraw[1:2])
    q = hq * _sigmoid(hq)
    fg = lb + (1.0 - lb) * _sigmoid(hf)
    logf = jnp.where(valid, jnp.log(fg), 0.0)
    k = jnp.where(valid, 1.0 - fg, 0.0)
    v = hi
    r = lax.broadcasted_iota(jnp.int32, (BLOCK, BLOCK), 0)
    c = lax.broadcasted_iota(jnp.int32, (BLOCK, BLOCK), 1)
    tril = (c <= r).astype(f32)
    bcum = jnp.dot(tril, logf, precision=lax.Precision.HIGHEST, preferred_element_type=f32)
    blast = bcum[BLOCK - 1:BLOCK]
    rows = lax.broadcasted_iota(jnp.int32, (BLOCK, 1), 0)
    sub8 = lax.broadcasted_iota(jnp.int32, (BLOCK // 8, 8, HG_K), 1)
    b8 = bcum.reshape(BLOCK // 8, 8, HG_K)
    row_of_8 = lambda j: jnp.broadcast_to(b8[:, j:j + 1, :], b8.shape)
    a = jnp.where(r == c, jnp.sum(q * k, axis=-1, keepdims=True), 0.0)
    seg = BLOCK
    while seg >= 2:
        half = seg // 2
        if seg >= 8:
            bs = bcum.reshape(BLOCK // seg, seg, HG_K)
            ref = jnp.broadcast_to(bs[:, half - 1:half, :], bs.shape)
        elif seg == 4:
            ref = jnp.where(sub8 < 4, row_of_8(1), row_of_8(5))
        else:
            ref = jnp.where(sub8 < 2, row_of_8(0), jnp.where(sub8 < 4, row_of_8(2),
                                                             jnp.where(sub8 < 6, row_of_8(4), row_of_8(6))))
        ref = ref.reshape(BLOCK, HG_K)
        upper = (rows % seg) >= half
        q_up = q * jnp.exp(jnp.where(upper, bcum - ref, -jnp.inf))
        k_lo = k * jnp.exp(jnp.where(upper, -jnp.inf, ref - bcum))
        a = a + jnp.where((r // seg) == (c // seg), mm_nt(q_up, k_lo), 0.0)
        seg = half
    o = mm_nt(q * jnp.exp(bcum), st) + mm(a, v)
    st_new = st * jnp.exp(blast) + mm_tn(v, k * jnp.exp(blast - bcum))
    on = o * lax.rsqrt(jnp.mean(o * o, axis=-1, keepdims=True) + EPS) * ng
    return st_new, on * (hg * _sigmoid(hg))


def _hgrn_in_specs(rowmap):
    wide = lambda col: pl.BlockSpec((BLOCK, HG_HEADS * HG_K), lambda i: (rowmap(i) + LEAD, col))
    return [wide(0), wide(1), wide(2), wide(3), _const_spec((2, HG_HEADS * HG_K)), _const_spec((1, HG_K))]


def _head(ref, h):
    return ref[:, h * HG_K:(h + 1) * HG_K]


def hgrn_fwd(pa, lbraw, ng, nb, shards):
    n = len(shards)

    def body(hq_ref, hf_ref, hi_ref, hg_ref, lb_ref, ng_ref, *rest):
        srcs, (og_ref, sp_ref), dsts = rest[:n], rest[n:n + 2], rest[n + 2:2 * n + 2]
        st_ref = rest[2 * n + 2]
        start, wait = _shard_push(srcs, dsts, *rest[2 * n + 3:])
        i = pl.program_id(0)

        @pl.when(i == 0)
        def _():
            st_ref[...] = jnp.zeros_like(st_ref)
            start()

        @pl.when(i == nb - 1)
        def _():
            wait()

        valid = _row_ids(i) >= PAD
        for h in range(HG_HEADS):
            st = st_ref[h]
            sp_ref[0, h] = st
            st_new, out = _hgrn_chunk(valid, st, _head(hq_ref, h), _head(hf_ref, h), _head(hi_ref, h),
                                      _head(hg_ref, h), _head(lb_ref, h), ng_ref[...])
            st_ref[h] = st_new
            og_ref[:, h * HG_K:(h + 1) * HG_K] = out.astype(bf16)

    p = (nb + LEAD) * BLOCK
    push_in, push_out, push_shape, push_scratch = _push_specs(shards)
    return pl.pallas_call(
        body, name="hgrn_fwd", grid=(nb,),
        in_specs=_hgrn_in_specs(lambda i: i) + push_in,
        out_specs=[pl.BlockSpec((BLOCK, HG_HEADS * HG_K), lambda i: (i + LEAD, 0)),
                   pl.BlockSpec((1, HG_HEADS, HG_K, HG_K), lambda i: (i, 0, 0, 0))] + push_out,
        out_shape=[jax.ShapeDtypeStruct((p, HG_HEADS * HG_K), bf16),
                   jax.ShapeDtypeStruct((nb, HG_HEADS, HG_K, HG_K), f32)] + push_shape,
        scratch_shapes=[pltpu.VMEM((HG_HEADS, HG_K, HG_K), f32)] + push_scratch,
        compiler_params=_cparams(("arbitrary",)),
    )(pa, pa, pa, pa, lbraw, ng, *shards)


def hgrn_bwd(pa, lbraw, ng, sprev, dog, nb, grads):
    n = len(grads)

    def body(hq_ref, hf_ref, hi_ref, hg_ref, lb_ref, ng_ref, sp_ref, do_ref, *rest):
        srcs, rest = rest[:n], rest[n:]
        dq_ref, df_ref, di_ref, dg_ref, dlb_ref, dng_ref = rest[:6]
        dsts, dst_ref = rest[6:6 + n], rest[6 + n]
        start, wait = _grad_push(srcs, dsts, *rest[7 + n:])
        i = pl.program_id(0)

        @pl.when(i == 0)
        def _():
            dst_ref[...] = jnp.zeros_like(dst_ref)
            dlb_ref[...] = jnp.zeros_like(dlb_ref)
            dng_ref[...] = jnp.zeros_like(dng_ref)
            start()

        valid = _row_ids(nb - 1 - i) >= PAD
        dng_sum = jnp.zeros((1, HG_K), f32)
        for h in range(HG_HEADS):
            cols = slice(h * HG_K, (h + 1) * HG_K)
            _, vjp = jax.vjp(functools.partial(_hgrn_chunk, valid), sp_ref[0, h], _head(hq_ref, h), _head(hf_ref, h),
                             _head(hi_ref, h), _head(hg_ref, h), _head(lb_ref, h), ng_ref[...])
            dst, dq, df, di, dg, dlb, dng = vjp((dst_ref[h], _head(do_ref, h)))
            dst_ref[h] = dst
            dq_ref[:, cols] = dq.astype(bf16)
            df_ref[:, cols] = df.astype(bf16)
            di_ref[:, cols] = di.astype(bf16)
            dg_ref[:, cols] = dg.astype(bf16)
            dlb_ref[:, cols] += dlb
            dng_sum = dng_sum + dng
        dng_ref[...] += dng_sum
        pl.when(i == nb - 1)(wait)

    p = (nb + LEAD) * BLOCK
    rev = lambda i: nb - 1 - i
    hw = HG_HEADS * HG_K
    blk = pl.BlockSpec((BLOCK, hw), lambda i: (rev(i) + LEAD, 0))
    wide = jax.ShapeDtypeStruct((p, hw), bf16)
    push_in, push_out, push_shape, push_scratch = _grad_push_specs(grads)
    return pl.pallas_call(
        body, name="hgrn_bwd", grid=(nb,),
        in_specs=_hgrn_in_specs(rev) + [pl.BlockSpec((1, HG_HEADS, HG_K, HG_K), lambda i: (rev(i), 0, 0, 0)), blk]
        + push_in,
        out_specs=[blk, blk, blk, blk, pl.BlockSpec((2, hw), lambda i: (0, 0)), pl.BlockSpec((1, HG_K), lambda i: (0, 0))]
        + push_out,
        out_shape=[wide, wide, wide, wide, jax.ShapeDtypeStruct((2, hw), f32), jax.ShapeDtypeStruct((1, HG_K), f32)]
        + push_shape,
        scratch_shapes=[pltpu.VMEM((HG_HEADS, HG_K, HG_K), f32)] + push_scratch,
        compiler_params=_cparams(("arbitrary",)),
    )(pa, pa, pa, pa, lbraw, ng, sprev, dog, *grads)


def _rot_half(x):
    lane = lax.broadcasted_iota(jnp.int32, x.shape, 1)
    return jnp.where(lane % HEAD_DIM < HEAD_DIM // 2, -pltpu.roll(x, BLOCK - HEAD_DIM // 2, 1),
                     pltpu.roll(x, HEAD_DIM // 2, 1))


def _rope(x, cos, sin):
    return x * cos + _rot_half(x) * sin


def _rope_transposed(g, cos, sin):
    return g * cos - _rot_half(g * sin)


def _both_halves(x, g):
    lo = lax.broadcasted_iota(jnp.int32, x.shape, 1) < HEAD_DIM
    sw = roll_lanes(x, HEAD_DIM)
    return jnp.where(lo, x, sw) if g == 0 else jnp.where(lo, sw, x)


def _attn_block(band_ok, meta_ok, q, kp, kc, vp, vc, km, vm, *sinks):
    neg = jnp.finfo(f32).min
    scale = HEAD_DIM ** -0.5
    group = ATT_HEADS // 2
    lo = lax.broadcasted_iota(jnp.int32, (BLOCK, BLOCK), 1) < HEAD_DIM
    t = lax.broadcasted_iota(jnp.int32, (group * BLOCK, BLOCK), 0) % BLOCK
    col = lax.broadcasted_iota(jnp.int32, (group * BLOCK, BLOCK), 1)
    own = col <= t
    is_sink = col == N_META
    qr = [q[:, m * BLOCK:(m + 1) * BLOCK] for m in range(ATT_HEADS // 2)]
    slabs = []
    for g in range(2):
        kp_g, kc_g, vp_g, vc_g, km_g, vm_g = [_both_halves(a, g) for a in (kp, kc, vp, vc, km, vm)]
        qs = jnp.concatenate([jnp.where(lo if h % 2 == 0 else ~lo, qr[2 * g + h // 2], 0.0) for h in range(group)],
                             axis=0)
        sink = jnp.concatenate([jnp.broadcast_to(sinks[group * g + h], (BLOCK, 1)) for h in range(group)], axis=0)
        sb = jnp.where(band_ok, jnp.where(own, mm_nt(qs, kc_g), mm_nt(qs, kp_g)) * scale, neg)
        no_keys = jnp.zeros((BLOCK - N_META, BLOCK), f32)
        sme = jnp.where(meta_ok, mm_nt(qs, jnp.concatenate([km_g, no_keys], axis=0)) * scale,
                        jnp.where(is_sink, sink, neg))
        mx = lax.stop_gradient(jnp.max(jnp.maximum(sb, sme), axis=-1, keepdims=True))
        eb, em = jnp.exp(sb - mx), jnp.exp(sme - mx)
        inv = _recip(jnp.sum(eb + em, axis=-1, keepdims=True))
        pb = eb * inv
        o = (mm(jnp.where(own, pb, 0.0), vc_g) + mm(jnp.where(own, 0.0, pb), vp_g)
             + mm(em * inv, jnp.concatenate([vm_g, no_keys], axis=0)))
        for m in range(2):
            even, odd = o[2 * m * BLOCK:(2 * m + 1) * BLOCK], o[(2 * m + 1) * BLOCK:(2 * m + 2) * BLOCK]
            slabs.append(jnp.where(lo, even, odd))
    return jnp.concatenate(slabs, axis=1)


def _attn_masks(i):
    group = ATT_HEADS // 2
    t = lax.broadcasted_iota(jnp.int32, (group * BLOCK, BLOCK), 0) % BLOCK
    s = lax.broadcasted_iota(jnp.int32, (group * BLOCK, BLOCK), 1)
    kpos = jnp.where(s <= t, i * BLOCK - PAD + s, jnp.where(i > 0, (i - 1) * BLOCK - PAD + s, -1))
    band_ok = kpos >= N_META
    qpos = i * BLOCK - PAD + lax.broadcasted_iota(jnp.int32, (group * BLOCK, 1), 0) % BLOCK
    meta_ok = (s < N_META) & (s <= qpos)
    return band_ok, meta_ok


def _attn_in_specs():
    cur, prev, first = (lambda i: i), (lambda i: jnp.maximum(i - 1, 0)), (lambda i: 0)
    kcol, vcol = V_COL // BLOCK - 1, V_COL // BLOCK
    blk = lambda rowmap, col: pl.BlockSpec((BLOCK, BLOCK), lambda i: (rowmap(i) + LEAD, col))
    return [pl.BlockSpec((BLOCK, ATT_QW), lambda i: (i + LEAD, Q_COL // ATT_QW)),
            blk(prev, kcol), blk(cur, kcol), blk(prev, vcol), blk(cur, vcol), blk(first, kcol), blk(first, vcol),
            _const_spec((ATT_HEADS, BLOCK))]


def _attn_row(n):
    return pl.BlockSpec((BLOCK, n), lambda i: (i + LEAD, 0))


def _attn_operands(q_ref, kp_ref, kc_ref, vp_ref, vc_ref, km_ref, vm_ref, sk_ref):
    args = (q_ref[...], kp_ref[...], kc_ref[...], vp_ref[...], vc_ref[...], km_ref[PAD:, :], vm_ref[PAD:, :])
    sinks = tuple(sk_ref[j:j + 1, 0:1] for j in range(ATT_HEADS))
    return args + sinks


def attn_fwd(pa, sinks8, nb, shards):
    n = len(shards)
    n_in = 8

    def body(*refs):
        srcs, o_ref, dsts = refs[n_in:n_in + n], refs[n_in + n], refs[n_in + n + 1:n_in + 2 * n + 1]
        start, wait = _shard_push(srcs, dsts, *refs[n_in + 2 * n + 1:])
        i = pl.program_id(0)
        pl.when(i == 0)(start)
        band_ok, meta_ok = _attn_masks(i)
        o_ref[...] = _attn_block(band_ok, meta_ok, *_attn_operands(*refs[:n_in])).astype(bf16)
        pl.when(i == nb - 1)(wait)

    push_in, push_out, push_shape, push_scratch = _push_specs(shards)
    return pl.pallas_call(
        body, name="attn_fwd", grid=(nb,), in_specs=_attn_in_specs() + push_in,
        out_specs=[_attn_row(ATT_QW)] + push_out,
        out_shape=[jax.ShapeDtypeStruct(((nb + LEAD) * BLOCK, ATT_QW), bf16)] + push_shape,
        scratch_shapes=push_scratch,
        compiler_params=_cparams(("arbitrary",)),
    )(pa, pa, pa, pa, pa, pa, pa, sinks8, *shards)


def attn_bwd(pa, sinks8, do, nb, grads):
    n = len(grads)

    def body(*refs):
        do_ref, srcs = refs[8], refs[9:9 + n]
        dq_ref, dkc_ref, dkp_ref, dvc_ref, dvp_ref, dkm_ref, dvm_ref, dsk_ref = refs[9 + n:17 + n]
        start, wait = _grad_push(srcs, refs[17 + n:17 + 2 * n], *refs[17 + 2 * n:])
        i = pl.program_id(0)

        @pl.when(i == 0)
        def _():
            dkm_ref[...] = jnp.zeros((N_META, BLOCK), f32)
            dvm_ref[...] = jnp.zeros((N_META, BLOCK), f32)
            dsk_ref[...] = jnp.zeros((ATT_HEADS, BLOCK), f32)
            start()

        band_ok, meta_ok = _attn_masks(i)
        _, vjp = jax.vjp(functools.partial(_attn_block, band_ok, meta_ok), *_attn_operands(*refs[:8]))
        grads = vjp(do_ref[...])
        dq_ref[...] = grads[0]
        dkp_ref[...] = grads[1]
        dkc_ref[...] = grads[2]
        dvp_ref[...] = grads[3]
        dvc_ref[...] = grads[4]
        dkm_ref[...] += grads[5]
        dvm_ref[...] += grads[6]
        for j in range(ATT_HEADS):
            dsk_ref[j:j + 1, :] += jnp.broadcast_to(grads[7 + j], (1, BLOCK))
        pl.when(i == nb - 1)(wait)

    p = (nb + LEAD) * BLOCK
    row = _attn_row(BLOCK)
    const = lambda r: pl.BlockSpec((r, BLOCK), lambda i: (0, 0))
    part = jax.ShapeDtypeStruct((p, BLOCK), f32)
    push_in, push_out, push_shape, push_scratch = _grad_push_specs(grads)
    return pl.pallas_call(
        body, name="attn_bwd", grid=(nb,),
        in_specs=_attn_in_specs() + [_attn_row(ATT_QW)] + push_in,
        out_specs=[_attn_row(ATT_QW), row, row, row, row,
                   const(N_META), const(N_META), const(ATT_HEADS)] + push_out,
        out_shape=[jax.ShapeDtypeStruct((p, ATT_QW), f32), part, part, part, part,
                   jax.ShapeDtypeStruct((N_META, BLOCK), f32), jax.ShapeDtypeStruct((N_META, BLOCK), f32),
                   jax.ShapeDtypeStruct((ATT_HEADS, BLOCK), f32)] + push_shape,
        scratch_shapes=push_scratch,
        compiler_params=_cparams(("arbitrary",)),
    )(pa, pa, pa, pa, pa, pa, pa, sinks8, do, *grads)


def _mid_forward(h0_ref, pg_ref, og, oa, wbh_ref, wba_ref, wo_ref, g1, b1):
    yh = jnp.dot(og, wbh_ref[...], preferred_element_type=f32)
    ya = jnp.dot(oa, wba_ref[...], preferred_element_type=f32)
    gh = _sigmoid(pg_ref[:, :D_MODEL])
    ga = _sigmoid(pg_ref[:, D_MODEL:])
    mixin = (gh * yh + ga * ya).astype(bf16)
    r1 = ALPHA * h0_ref[...] + jnp.dot(mixin, wo_ref[...], preferred_element_type=f32)
    xh1, rs1 = _ln_stats(r1)
    return yh, ya, gh, ga, mixin, xh1, rs1, xh1 * g1 + b1


def _mid_weight_specs():
    hw = HG_HEADS * HG_K
    return [_const_spec((hw, D_MODEL)), _const_spec((ATT_QW, D_MODEL)), _const_spec((D_MODEL, D_MODEL)),
            _const_spec((1, D_MODEL)), _const_spec((1, D_MODEL))]


def mid_front_ffn(h0, pg, og, oatt, target, wbh, wba, wout, wfi, wfo, ln1g, ln1b, ln2g, ln2b):
    def body(h0_ref, pg_ref, og_ref, oa_ref, t_ref, wbh_ref, wba_ref, wo_ref, g1_ref, b1_ref, wfi_ref, wfo_ref,
             g2_ref, b2_ref, dh1_ref, dau_ref, s_ref, dr2_ref, mix_ref, h1b_ref, ogc_ref, oac_ref,
             loss_ref, dg2_ref, db2_ref):
        i = pl.program_id(0)

        @pl.when(i == 0)
        def _():
            for r in (loss_ref, dg2_ref, db2_ref):
                r[...] = jnp.zeros_like(r)

        used = _tm_rows(i) >= LEAD * BLOCK
        og = jnp.where(used, og_ref[...], jnp.zeros_like(og_ref))
        oa = jnp.where(used, oa_ref[...], jnp.zeros_like(oa_ref))
        ogc_ref[...] = og
        oac_ref[...] = oa
        *_, mixin, _, _, h1 = _mid_forward(h0_ref, pg_ref, og, oa, wbh_ref, wba_ref, wo_ref, g1_ref[...], b1_ref[...])
        mix_ref[...] = mixin
        h1b = h1.astype(bf16)
        h1b_ref[...] = h1b
        g2, b2 = g2_ref[...], b2_ref[...]
        au = jnp.dot(h1b, wfi_ref[...], preferred_element_type=f32)
        a, u = au[:, :D_FF], au[:, D_FF:]
        sg = _sigmoid(a)
        sa = a * sg
        s = (sa * u).astype(bf16)
        s_ref[...] = s
        r2 = ALPHA * h1 + jnp.dot(s, wfo_ref[...], preferred_element_type=f32)
        xh2, rs2 = _ln_stats(r2)
        diff = jnp.where(i > 0, xh2 * g2 + b2 - t_ref[...], 0.0)
        loss_ref[...] += jnp.sum(diff * diff) * (0.5 / D_MODEL)
        dy = diff * (1.0 / D_MODEL)
        dg2_ref[...] += jnp.sum(dy * xh2, axis=0, keepdims=True)
        db2_ref[...] += jnp.sum(dy, axis=0, keepdims=True)
        dr2 = _ln_bwd(dy, xh2, rs2, g2)
        dr2b = dr2.astype(bf16)
        dr2_ref[...] = dr2b
        ds = _dot(dr2b, wfo_ref[...], 1, 1)
        da = (ds * u) * (sg * (1.0 + a * (1.0 - sg)))
        du = ds * sa
        dau = jnp.concatenate([da, du], axis=1).astype(bf16)
        dau_ref[...] = dau
        dh1_ref[...] = ALPHA * dr2 + _dot(dau, wfi_ref[...], 1, 1)

    p = h0.shape[0]
    hw = HG_HEADS * HG_K
    vec = lambda: pl.BlockSpec((1, D_MODEL), lambda i: (0, 0))
    sds = lambda n, dt: jax.ShapeDtypeStruct((p, n), dt)
    return pl.pallas_call(
        body, name="mid_front_ffn", grid=(p // TM,),
        in_specs=[_tm_row(D_MODEL), _tm_row(N_G), _tm_row(hw), _tm_row(ATT_QW), _tm_tokens()] + _mid_weight_specs()
        + [_const_spec((D_MODEL, 2 * D_FF)), _const_spec((D_FF, D_MODEL)), _const_spec((1, D_MODEL)),
           _const_spec((1, D_MODEL))],
        out_specs=[_tm_row(D_MODEL), _tm_row(2 * D_FF), _tm_row(D_FF), _tm_row(D_MODEL), _tm_row(D_MODEL),
                   _tm_row(D_MODEL), _tm_row(hw), _tm_row(ATT_QW), pl.BlockSpec((1, 1), lambda i: (0, 0)), vec(), vec()],
        out_shape=[sds(D_MODEL, f32), sds(2 * D_FF, bf16), sds(D_FF, bf16), sds(D_MODEL, bf16), sds(D_MODEL, bf16),
                   sds(D_MODEL, bf16), sds(hw, bf16), sds(ATT_QW, bf16),
                   jax.ShapeDtypeStruct((1, 1), f32)] + [jax.ShapeDtypeStruct((1, D_MODEL), f32)] * 2,
        compiler_params=_cparams(("arbitrary",), VMEM_LIMIT_ALL_WEIGHTS),
    )(h0, pg, og, oatt, target, wbh, wba, wout, ln1g, ln1b, wfi, wfo, ln2g, ln2b)


def mid_back(dh1, h0, pg, ogc, oac, wbh, wba, wout, ln1g, ln1b):
    def body(dh1_ref, h0_ref, pg_ref, og_ref, oa_ref, wbh_ref, wba_ref, wo_ref, g1_ref, b1_ref,
             dh0_ref, dpg_ref, dog_ref, doa_ref, dyh_ref, dya_ref, dr1_ref, dg1_ref, db1_ref):
        @pl.when(pl.program_id(0) == 0)
        def _():
            dg1_ref[...] = jnp.zeros_like(dg1_ref)
            db1_ref[...] = jnp.zeros_like(db1_ref)

        g1 = g1_ref[...]
        yh, ya, gh, ga, _, xh1, rs1, _ = _mid_forward(h0_ref, pg_ref, og_ref[...], oa_ref[...], wbh_ref, wba_ref,
                                                      wo_ref, g1, b1_ref[...])
        dh1 = dh1_ref[...]
        dg1_ref[...] += jnp.sum(dh1 * xh1, axis=0, keepdims=True)
        db1_ref[...] += jnp.sum(dh1, axis=0, keepdims=True)
        dr1 = _ln_bwd(dh1, xh1, rs1, g1)
        dr1b = dr1.astype(bf16)
        dr1_ref[...] = dr1b
        dh0_ref[...] = ALPHA * dr1
        dmix = _dot(dr1b, wo_ref[...], 1, 1)
        dyh = (dmix * gh).astype(bf16)
        dya = (dmix * ga).astype(bf16)
        dyh_ref[...] = dyh
        dya_ref[...] = dya
        dpg_ref[:, :D_MODEL] = (dmix * yh * gh * (1.0 - gh)).astype(bf16)
        dpg_ref[:, D_MODEL:] = (dmix * ya * ga * (1.0 - ga)).astype(bf16)
        dog_ref[...] = _dot(dyh, wbh_ref[...], 1, 1)
        doa_ref[...] = _dot(dya, wba_ref[...], 1, 1)

    p = h0.shape[0]
    hw = HG_HEADS * HG_K
    vec = lambda: pl.BlockSpec((1, D_MODEL), lambda i: (0, 0))
    sds = lambda n, dt: jax.ShapeDtypeStruct((p, n), dt)
    return pl.pallas_call(
        body, name="mid_back", grid=(p // TM,),
        in_specs=[_tm_row(D_MODEL), _tm_row(D_MODEL), _tm_row(N_G), _tm_row(hw), _tm_row(ATT_QW)] + _mid_weight_specs(),
        out_specs=[_tm_row(D_MODEL), _tm_row(N_G), _tm_row(hw), _tm_row(ATT_QW), _tm_row(D_MODEL), _tm_row(D_MODEL),
                   _tm_row(D_MODEL), vec(), vec()],
        out_shape=[sds(D_MODEL, f32), sds(N_G, bf16), sds(hw, f32), sds(ATT_QW, f32), sds(D_MODEL, bf16),
                   sds(D_MODEL, bf16), sds(D_MODEL, bf16)] + [jax.ShapeDtypeStruct((1, D_MODEL), f32)] * 2,
        compiler_params=_cparams(("arbitrary",)),
    )(dh1, h0, pg, ogc, oac, wbh, wba, wout, ln1g, ln1b)


def inproj_bwd(dh0p, dhq, dhf, dhi, dhg, daq, dkc, dkp, dvc, dvp, dkm, dvm, dpg, w_in, x, metablk, g, b, cos, sin):
    p = dh0p.shape[0]
    nbk = p // BLOCK
    per = TM // BLOCK

    def body(dh0_ref, dq_ref, df_ref, di_ref, dg_ref, daq_ref, dkc_ref, *rest):
        dkp_refs, dvc_ref, dvp_refs = rest[:per], rest[per], rest[per + 1:2 * per + 1]
        (dkm_ref, dvm_ref, dpg_ref, w_ref, x_ref, mb_ref, g_ref, b_ref, cos_ref, sin_ref,
         dproj_ref, dx_ref, dmeta_ref, dlg_ref, dlb_ref) = rest[2 * per + 1:]
        i = pl.program_id(0)

        @pl.when(i == 0)
        def _():
            dlg_ref[...] = jnp.zeros_like(dlg_ref)
            dlb_ref[...] = jnp.zeros_like(dlb_ref)

        zero_pad = jnp.zeros((TM - N_META, BLOCK), f32)
        first = i == 0
        rows = _tm_rows(i)

        def keys(cur_ref, next_refs, meta_ref):
            nxt = jnp.concatenate([jnp.where(per * i + 1 + m < nbk, next_refs[m][...], 0.0) for m in range(per)],
                                  axis=0)
            t = cur_ref[...] + nxt
            return t + jnp.where(first, jnp.concatenate([zero_pad, meta_ref[...]], axis=0), 0.0)

        cos, sin = cos_ref[...], sin_ref[...]
        unrotate = lambda t: _rope_transposed(t, cos, sin).astype(bf16)
        dproj = jnp.concatenate(
            [dq_ref[...], df_ref[...], di_ref[...], dg_ref[...]]
            + [unrotate(daq_ref[:, m * BLOCK:(m + 1) * BLOCK]) for m in range(ATT_QW // BLOCK)]
            + [unrotate(keys(dkc_ref, dkp_refs, dkm_ref)), keys(dvc_ref, dvp_refs, dvm_ref).astype(bf16),
               dpg_ref[...]], axis=1)
        dproj = jnp.where(rows >= LEAD * BLOCK, dproj, jnp.zeros_like(dproj))
        dproj_ref[...] = dproj
        valid = rows >= TM - N_META
        dh0 = jnp.where(valid, dh0_ref[...] + _dot(dproj, w_ref[...], 1, 0), 0.0)
        xb = jnp.where(first, mb_ref[...], x_ref[...])
        xh, rs = _ln_stats(xb)
        dlg_ref[...] += jnp.sum(dh0 * xh, axis=0, keepdims=True)
        dlb_ref[...] += jnp.sum(dh0, axis=0, keepdims=True)
        dx = jnp.where(valid, _ln_bwd(dh0, xh, rs, g_ref[...]), 0.0)
        dx_ref[...] = dx

        @pl.when(first)
        def _():
            dmeta_ref[...] = dx[TM - N_META:, :]

    row = _tm_row
    nxt = [pl.BlockSpec((BLOCK, BLOCK), functools.partial(lambda i, m: (jnp.minimum(per * i + 1 + m, nbk - 1), 0), m=m))
           for m in range(per)]
    hw = HG_HEADS * HG_K
    vec = lambda: pl.BlockSpec((1, D_MODEL), lambda i: (0, 0))
    return pl.pallas_call(
        body, name="inproj_bwd", grid=(p // TM,),
        in_specs=[row(D_MODEL), row(hw), row(hw), row(hw), row(hw), row(ATT_QW),
                  row(BLOCK)] + nxt + [row(BLOCK)] + nxt + [_const_spec((N_META, BLOCK)), _const_spec((N_META, BLOCK)),
                  row(N_G), _const_spec((IN_W, D_MODEL)), _tm_tokens(),
                  _const_spec((TM, D_MODEL)), _const_spec((1, D_MODEL)), _const_spec((1, D_MODEL)),
                  row(BLOCK), row(BLOCK)],
        out_specs=[row(IN_W), _tm_tokens(), pl.BlockSpec((N_META, D_MODEL), lambda i: (0, 0)), vec(), vec()],
        out_shape=[jax.ShapeDtypeStruct((p, IN_W), bf16), jax.ShapeDtypeStruct((p - TM, D_MODEL), f32),
                   jax.ShapeDtypeStruct((N_META, D_MODEL), f32),
                   jax.ShapeDtypeStruct((1, D_MODEL), f32), jax.ShapeDtypeStruct((1, D_MODEL), f32)],
        compiler_params=_cparams(("arbitrary",)),
    )(dh0p, dhq, dhf, dhi, dhg, daq, dkc, *([dkp] * per), dvc, *([dvp] * per), dkm, dvm, dpg, w_in, x, metablk, g, b,
      cos, sin)


def wgrad(a, b, name, tk, tn, tp, by_cols, out_dtype=f32):
    p, k = a.shape
    n = b.shape[1]
    nsteps = p // tp

    def body(a_ref, b_ref, o_ref, acc_ref):
        ip = pl.program_id(2)

        @pl.when(ip == 0)
        def _():
            acc_ref[...] = jnp.zeros_like(acc_ref)

        acc_ref[...] += _dot(a_ref[...], b_ref[...], 0, 0)

        @pl.when(ip == nsteps - 1)
        def _():
            for j in range(span):
                o_ref[j] = acc_ref[:, j * width:(j + 1) * width].astype(out_dtype)

    span, width = 1, tn
    if by_cols:
        shard_n = n // N_SHARD
        out_shape = (N_SHARD, k, shard_n)
        if tn >= shard_n:
            span, width = tn // shard_n, shard_n
            omap = lambda ik, jn, ip: (jn, ik, 0)
        else:
            per = shard_n // tn
            omap = lambda ik, jn, ip: (jn // per, ik, jn % per)
    else:
        out_shape = (1, k, n)
        omap = lambda ik, jn, ip: (0, ik, jn)
    return pl.pallas_call(
        body, name=name, grid=(k // tk, n // tn, nsteps),
        in_specs=[pl.BlockSpec((tp, tk), lambda ik, jn, ip: (ip, ik)),
                  pl.BlockSpec((tp, tn), lambda ik, jn, ip: (ip, jn))],
        out_specs=pl.BlockSpec((span, tk, width), omap),
        out_shape=jax.ShapeDtypeStruct(out_shape, out_dtype),
        scratch_shapes=[pltpu.VMEM((tk, tn), f32)],
        compiler_params=_cparams(("parallel", "parallel", "arbitrary")),
    )(a, b)


def _adamw_math(w, g, m, v):
    mn = ADAM_B1 * m + (1.0 - ADAM_B1) * g
    vn = ADAM_B2 * v + (1.0 - ADAM_B2) * (g * g)
    m_hat = mn / (1.0 - ADAM_B1 ** ADAM_STEP)
    v_hat = vn / (1.0 - ADAM_B2 ** ADAM_STEP)
    return -ADAM_LR * (m_hat / (jnp.sqrt(v_hat) + ADAM_EPS) + ADAM_WD * w), mn, vn


def adamw(w, g, m, v, name):
    r, c = w.shape
    tr = r
    for cand in (256, 176, 152, 128):
        if r > cand and r % cand == 0:
            tr = cand
            break

    def body(w_ref, g_ref, m_ref, v_ref, go_ref, d_ref, mo_ref, vo_ref):
        gg = g_ref[...]
        go_ref[...] = gg
        d_ref[...], mo_ref[...], vo_ref[...] = _adamw_math(w_ref[...], gg, m_ref[...], v_ref[...])

    spec = pl.BlockSpec((tr, c), lambda i: (i, 0))
    sds = jax.ShapeDtypeStruct((r, c), f32)
    return pl.pallas_call(
        body, name=name, grid=(r // tr,), in_specs=[spec] * 4, out_specs=[spec] * 4, out_shape=[sds] * 4,
        compiler_params=_cparams(("parallel",)),
    )(w, g, m, v)


def adamw_small(ws, gs, ms, vs):
    n = len(ws)

    def body(*refs):
        ins, outs = refs[:4 * n], refs[4 * n:]
        for k in range(n):
            outs[k][...], outs[n + k][...], outs[2 * n + k][...] = _adamw_math(
                ins[k][...], ins[n + k][...], ins[2 * n + k][...], ins[3 * n + k][...])

    out = pl.pallas_call(body, name="adamw_small",
                         out_shape=[jax.ShapeDtypeStruct(w.shape, f32) for w in ws] * 3)(*ws, *gs, *ms, *vs)
    return out[:n], out[n:2 * n], out[2 * n:]


def _me():
    return lax.axis_index("x"), lax.axis_index("y"), lax.axis_index("c")


def _chip_peer(x, y, c, k):
    return (x ^ (k >> 1), y ^ (k & 1), c)


ANY = pl.BlockSpec(memory_space=pl.ANY)


def gather_weights(now, later):
    n, n_later = len(now), len(later)
    out_dtypes = [bf16 if s.size > 16 * 256 else f32 for s in now]
    halves = [(2, s.shape[0] // 2, s.shape[1]) for s in now]

    def body(*refs):
        ins, later_ins = refs[:n], refs[n:n + n_later]
        outs, later_outs = refs[n + n_later:2 * n + n_later], refs[2 * n + n_later:2 * (n + n_later)]
        stage = refs[2 * (n + n_later):3 * n + 2 * n_later]
        send_sems, recv_sems, pass_send_sems, pass_recv_sems, local_sems = refs[3 * n + 2 * n_later:]
        x, y, c = _me()
        j = 2 * x + y
        sibling = (x, y, 1 - c)

        def over_ici(w, k, slot):
            return pltpu.make_async_remote_copy(
                src_ref=stage[w].at[c], dst_ref=outs[w].at[slot, c], send_sem=send_sems.at[w, k - 1],
                recv_sem=recv_sems.at[w, k - 1], device_id=_chip_peer(x, y, c, k), device_id_type=MESH)

        def passed_on(w, k, half):
            return pltpu.make_async_remote_copy(
                src_ref=outs[w].at[j ^ k, half], dst_ref=outs[w].at[j ^ k, half], send_sem=pass_send_sems.at[w, k - 1],
                recv_sem=pass_recv_sems.at[w, k - 1], device_id=sibling, device_id_type=MESH)

        for w in range(n):
            stage[w][...] = ins[w][...].astype(out_dtypes[w]).reshape(halves[w])
        locs = []
        for w in range(n):
            loc = pltpu.make_async_copy(stage[w], outs[w].at[j], local_sems.at[w])
            loc.start()
            locs.append(loc)
            for k in (1, 2, 3):
                over_ici(w, k, j).start()
        for w in range(n_later):
            later_outs[w][...] = later_ins[w][...].astype(bf16)
        for w in range(n):
            for k in (1, 2, 3):
                over_ici(w, k, j ^ k).wait_recv()
                passed_on(w, k, c).start()
        for w in range(n):
            for k in (1, 2, 3):
                passed_on(w, k, 1 - c).wait_recv()
        for w in range(n):
            for k in (1, 2, 3):
                over_ici(w, k, j).wait_send()
                passed_on(w, k, c).wait_send()
        for loc in locs:
            loc.wait()

    vmem = pl.BlockSpec(memory_space=pltpu.VMEM)
    sem3 = pltpu.SemaphoreType.DMA((n, 3))
    return pl.pallas_call(
        body, name="gather_weights",
        in_specs=[vmem] * (n + n_later), out_specs=[ANY] * n + [vmem] * n_later,
        out_shape=[jax.ShapeDtypeStruct((N_SHARD,) + h, dt) for h, dt in zip(halves, out_dtypes)]
        + [jax.ShapeDtypeStruct(s.shape, bf16) for s in later],
        scratch_shapes=[pltpu.VMEM(h, dt) for h, dt in zip(halves, out_dtypes)]
        + [sem3, sem3, sem3, sem3, pltpu.SemaphoreType.DMA((n,))],
        compiler_params=pltpu.CompilerParams(vmem_limit_bytes=VMEM_LIMIT),
    )(*now, *later)


def _shard_push(srcs, dsts, send_sems, recv_sems, local_sems):
    def remote(w, k, slot):
        x, y, c = _me()
        return pltpu.make_async_remote_copy(
            src_ref=srcs[w], dst_ref=dsts[w].at[slot], send_sem=send_sems.at[w, k - 1],
            recv_sem=recv_sems.at[w, k - 1], device_id=_chip_peer(x, y, c, k), device_id_type=MESH)

    def local(w):
        x, y, _ = _me()
        return pltpu.make_async_copy(srcs[w], dsts[w].at[2 * x + y], local_sems.at[w])

    def start():
        x, y, _ = _me()
        for w in range(len(srcs)):
            local(w).start()
            for k in (1, 2, 3):
                remote(w, k, 2 * x + y).start()

    def wait():
        x, y, _ = _me()
        for w in range(len(srcs)):
            for k in (1, 2, 3):
                remote(w, k, (2 * x + y) ^ k).wait_recv()
        for w in range(len(srcs)):
            for k in (1, 2, 3):
                remote(w, k, 2 * x + y).wait_send()
            local(w).wait()

    return start, wait


def _grad_push(srcs, dsts, send_sems, recv_sems):
    def copy(w, k):
        x, y, c = _me()
        px, py, pc = x ^ (k >> 2), y ^ ((k >> 1) & 1), c ^ (k & 1)
        return pltpu.make_async_remote_copy(
            src_ref=srcs[w].at[2 * px + py, pc], dst_ref=dsts[w].at[k - 1], send_sem=send_sems.at[w, k - 1],
            recv_sem=recv_sems.at[w, k - 1], device_id=(px, py, pc), device_id_type=MESH)

    def start():
        for w in range(len(srcs)):
            for k in range(1, N_DEV):
                copy(w, k).start()

    def wait():
        for w in range(len(srcs)):
            for k in range(1, N_DEV):
                copy(w, k).wait_recv()
        for w in range(len(srcs)):
            for k in range(1, N_DEV):
                copy(w, k).wait_send()

    return start, wait


def _grad_push_specs(grads):
    n = len(grads)
    return ([ANY] * n, [ANY] * n, [jax.ShapeDtypeStruct((N_DEV - 1,) + g.shape[2:], g.dtype) for g in grads],
            [pltpu.SemaphoreType.DMA((n, N_DEV - 1)), pltpu.SemaphoreType.DMA((n, N_DEV - 1))])


def add_eight(own, parts, jc_idx, name):
    _, half, c = parts.shape
    tr = half // 2 if (half // 2) % 16 == 0 else half

    def body(jc_ref, own_ref, p_ref, out_ref):
        acc = own_ref[0, 0].astype(f32)
        for k in range(N_DEV - 1):
            acc = acc + p_ref[k].astype(f32)
        out_ref[0] = acc

    return pl.pallas_call(
        body, name=name,
        grid_spec=pltpu.PrefetchScalarGridSpec(
            num_scalar_prefetch=1, grid=(half // tr,),
            in_specs=[pl.BlockSpec((1, 1, tr, c), lambda t, jc: (jc[0], jc[1], t, 0)),
                      pl.BlockSpec((N_DEV - 1, tr, c), lambda t, jc: (0, t, 0))],
            out_specs=pl.BlockSpec((1, tr, c), lambda t, jc: (jc[1], t, 0))),
        out_shape=jax.ShapeDtypeStruct((2, half, c), f32),
        compiler_params=_cparams(("parallel",)),
    )(jc_idx, own, parts)


def _push_specs(shards):
    n = len(shards)
    return ([ANY] * n, [ANY] * n, [jax.ShapeDtypeStruct((N_SHARD,) + s.shape, s.dtype) for s in shards],
            [pltpu.SemaphoreType.DMA((n, 3)), pltpu.SemaphoreType.DMA((n, 3)), pltpu.SemaphoreType.DMA((n,))])


def pair_exchange_halves(grads, small):
    n = len(grads)

    def body(*refs):
        ins, small_ref = refs[:n], refs[n]
        outs, gath = refs[n + 1:2 * n + 1], refs[2 * n + 1]
        send_sems, recv_sems, s_send, s_recv, local_sem = refs[2 * n + 2:]
        x, y, c = _me()
        me = 4 * x + 2 * y + c
        sends = []
        for w in range(n):
            half = ins[w].shape[1] // 2
            cp = pltpu.make_async_remote_copy(
                src_ref=ins[w].at[:, pl.ds((1 - c) * half, half), :], dst_ref=outs[w],
                send_sem=send_sems.at[w], recv_sem=recv_sems.at[w], device_id=(x, y, 1 - c), device_id_type=MESH)
            cp.start()
            sends.append(cp)
        loc = pltpu.make_async_copy(small_ref, gath.at[me], local_sem)
        loc.start()
        for k in range(1, N_DEV):
            cp = pltpu.make_async_remote_copy(
                src_ref=small_ref, dst_ref=gath.at[me], send_sem=s_send.at[k - 1], recv_sem=s_recv.at[k - 1],
                device_id=(x ^ (k >> 2), y ^ ((k >> 1) & 1), c ^ (k & 1)), device_id_type=MESH)
            cp.start()
            sends.append(cp)
        for w in range(n):
            half = ins[w].shape[1] // 2
            pltpu.make_async_remote_copy(
                src_ref=ins[w].at[:, pl.ds(0, half), :], dst_ref=outs[w], send_sem=send_sems.at[w],
                recv_sem=recv_sems.at[w], device_id=(x, y, 1 - c), device_id_type=MESH).wait_recv()
        for k in range(1, N_DEV):
            pltpu.make_async_remote_copy(
                src_ref=small_ref, dst_ref=gath.at[me ^ k], send_sem=s_send.at[k - 1], recv_sem=s_recv.at[k - 1],
                device_id=(x ^ (k >> 2), y ^ ((k >> 1) & 1), c ^ (k & 1)), device_id_type=MESH).wait_recv()
        for cp in sends:
            cp.wait_send()
        loc.wait()

    return pl.pallas_call(
        body, name="pair_exchange_halves", in_specs=[ANY] * (n + 1), out_specs=[ANY] * (n + 1),
        out_shape=[jax.ShapeDtypeStruct((g.shape[0], g.shape[1] // 2, g.shape[2]), g.dtype) for g in grads]
        + [jax.ShapeDtypeStruct((N_DEV,) + small.shape, f32)],
        scratch_shapes=[pltpu.SemaphoreType.DMA((n,)), pltpu.SemaphoreType.DMA((n,)),
                        pltpu.SemaphoreType.DMA((N_DEV - 1,)), pltpu.SemaphoreType.DMA((N_DEV - 1,)),
                        pltpu.SemaphoreType.DMA],
    )(*grads, small)


def chip_exchange(sums):
    n = len(sums)

    def body(*refs):
        ins, outs = refs[:n], refs[n:2 * n]
        send_sems, recv_sems = refs[2 * n:]
        x, y, c = _me()
        j = 2 * x + y
        sends = []
        for w in range(n):
            for k in (1, 2, 3):
                cp = pltpu.make_async_remote_copy(
                    src_ref=ins[w].at[j ^ k], dst_ref=outs[w].at[k - 1], send_sem=send_sems.at[w, k - 1],
                    recv_sem=recv_sems.at[w, k - 1], device_id=_chip_peer(x, y, c, k), device_id_type=MESH)
                cp.start()
                sends.append(cp)
        for w in range(n):
            for k in (1, 2, 3):
                pltpu.make_async_remote_copy(
                    src_ref=ins[w].at[0], dst_ref=outs[w].at[k - 1], send_sem=send_sems.at[w, k - 1],
                    recv_sem=recv_sems.at[w, k - 1], device_id=_chip_peer(x, y, c, k), device_id_type=MESH).wait_recv()
        for cp in sends:
            cp.wait_send()

    return pl.pallas_call(
        body, name="chip_exchange", in_specs=[ANY] * n, out_specs=[ANY] * n,
        out_shape=[jax.ShapeDtypeStruct((N_SHARD - 1,) + s.shape[1:], s.dtype) for s in sums],
        scratch_shapes=[pltpu.SemaphoreType.DMA((n, 3)), pltpu.SemaphoreType.DMA((n, 3))],
    )(*sums)


def pair_exchange_results(halves):
    n = len(halves)

    def body(*refs):
        ins, outs = refs[:n], refs[n:2 * n]
        send_sems, recv_sems = refs[2 * n:]
        x, y, c = _me()
        sends = []
        for w in range(n):
            cp = pltpu.make_async_remote_copy(
                src_ref=ins[w].at[c], dst_ref=outs[w].at[c], send_sem=send_sems.at[w], recv_sem=recv_sems.at[w],
                device_id=(x, y, 1 - c), device_id_type=MESH)
            cp.start()
            sends.append(cp)
        for w in range(n):
            pltpu.make_async_remote_copy(
                src_ref=ins[w].at[c], dst_ref=outs[w].at[1 - c], send_sem=send_sems.at[w],
                recv_sem=recv_sems.at[w], device_id=(x, y, 1 - c), device_id_type=MESH).wait_recv()
        for cp in sends:
            cp.wait_send()

    return pl.pallas_call(
        body, name="pair_exchange_results", in_specs=[ANY] * n, out_specs=[ANY] * n,
        out_shape=[jax.ShapeDtypeStruct(h.shape, f32) for h in halves],
        input_output_aliases={w: w for w in range(n)},
        scratch_shapes=[pltpu.SemaphoreType.DMA((n,)), pltpu.SemaphoreType.DMA((n,))],
    )(*halves)


def add_pair(grad, other, c_idx, name):
    _, r, c = grad.shape
    half = r // 2
    tr = half // 2 if (half // 2) % 8 == 0 else half
    per = half // tr

    def body(c_ref, g_ref, o_ref, out_ref):
        out_ref[...] = (g_ref[...].astype(f32) + o_ref[...].astype(f32)).astype(bf16)

    return pl.pallas_call(
        body, name=name,
        grid_spec=pltpu.PrefetchScalarGridSpec(
            num_scalar_prefetch=1, grid=(N_SHARD, per),
            in_specs=[pl.BlockSpec((1, tr, c), lambda j, t, cr: (j, cr[0] * per + t, 0)),
                      pl.BlockSpec((1, tr, c), lambda j, t, cr: (j, t, 0))],
            out_specs=pl.BlockSpec((1, tr, c), lambda j, t, cr: (j, t, 0))),
        out_shape=jax.ShapeDtypeStruct((N_SHARD, half, c), bf16),
        compiler_params=_cparams(("parallel", "parallel")),
    )(c_idx, grad, other)


def add_four(own, parts, jc_idx, name):
    _, half, c = parts.shape
    tr = half // 2 if (half // 2) % 8 == 0 else half

    def body(jc_ref, own_ref, p_ref, out_ref):
        acc = own_ref[0].astype(f32)
        for k in range(N_SHARD - 1):
            acc = acc + p_ref[k].astype(f32)
        out_ref[0] = acc

    return pl.pallas_call(
        body, name=name,
        grid_spec=pltpu.PrefetchScalarGridSpec(
            num_scalar_prefetch=1, grid=(half // tr,),
            in_specs=[pl.BlockSpec((1, tr, c), lambda t, jc: (jc[0], t, 0)),
                      pl.BlockSpec((N_SHARD - 1, tr, c), lambda t, jc: (0, t, 0))],
            out_specs=pl.BlockSpec((1, tr, c), lambda t, jc: (jc[1], t, 0))),
        out_shape=jax.ShapeDtypeStruct((2, half, c), f32),
        compiler_params=_cparams(("parallel",)),
    )(jc_idx, own, parts)


def sum_devices(gathered):
    def body(g_ref, out_ref):
        acc = g_ref[0]
        for d in range(1, N_DEV):
            acc = acc + g_ref[d]
        out_ref[...] = acc

    return pl.pallas_call(body, name="sum_devices", out_shape=jax.ShapeDtypeStruct(gathered.shape[1:], f32))(gathered)


def _rows128(a, rows):
    flat = a.reshape(-1, BLOCK) if a.size % BLOCK == 0 else jnp.pad(a.reshape(1, -1), ((0, 0), (0, BLOCK - a.size)))
    return jnp.pad(flat, ((0, rows - flat.shape[0]), (0, 0)))


def kernel(x, meta_tokens, ln_emb_g, ln_emb_b, w_in, hg_lower_bounds, hg_norm_g, attn_sinks, w_branch_hg, w_branch_attn, w_out, ln1_g, ln1_b, w_ffn_in, w_ffn_out, ln2_g, ln2_b, loss_target, m_meta_tokens, m_ln_emb_g, m_ln_emb_b, m_w_in, m_hg_lower_bounds, m_hg_norm_g, m_attn_sinks, m_w_branch_hg, m_w_branch_attn, m_w_out, m_ln1_g, m_ln1_b, m_w_ffn_in, m_w_ffn_out, m_ln2_g, m_ln2_b, v_meta_tokens, v_ln_emb_g, v_ln_emb_b, v_w_in, v_hg_lower_bounds, v_hg_norm_g, v_attn_sinks, v_w_branch_hg, v_w_branch_attn, v_w_out, v_ln1_g, v_ln1_b, v_w_ffn_in, v_w_ffn_out, v_ln2_g, v_ln2_b):
    seq = x.shape[1]
    nb = seq // BLOCK + 1
    xs = x[0]
    ts = loss_target[0]
    ix, iy, ic = _me()
    shard = 2 * ix + iy
    vec = lambda a: a.reshape(1, D_MODEL)

    w_in_t = jnp.swapaxes(w_in[0], 0, 1)
    g_in, g_meta, s_bh, s_ba, s_out, s_fi, s_fo = gather_weights(
        [w_in_t, meta_tokens], [w_branch_hg[0], w_branch_attn[0], w_out[0], w_ffn_in[0], w_ffn_out[0]])
    by_cols = lambda g: g.reshape(N_SHARD, -1, g.shape[-1]).transpose(1, 0, 2).reshape(-1, N_SHARD * g.shape[-1])
    wf_in = g_in.reshape(IN_W, D_MODEL)
    metablk = jnp.pad(by_cols(g_meta), ((TM - N_META, 0), (0, 0)))

    pos = jnp.arange((nb + LEAD) * BLOCK, dtype=jnp.int32) - (LEAD * BLOCK + PAD)
    half = HEAD_DIM // 2
    inv = ROPE_THETA ** (-jnp.arange(half, dtype=f32) / half)
    ang = pos.astype(f32)[:, None] * inv[None, :]
    cos = jnp.tile(jnp.cos(ang), (1, BLOCK // half))
    sin = jnp.tile(jnp.sin(ang), (1, BLOCK // half))
    sinks8 = jnp.broadcast_to(attn_sinks.reshape(ATT_HEADS, 1), (ATT_HEADS, BLOCK))
    ng = hg_norm_g.reshape(1, HG_K)

    h0, h0b, pa, pg = emb_inproj(xs, metablk, vec(ln_emb_g), vec(ln_emb_b), wf_in, cos, sin)
    og, sprev, g_fi, g_out = hgrn_fwd(pa, hg_lower_bounds, ng, nb, [s_fi, s_out])
    oatt, g_fo, g_bh, g_ba = attn_fwd(pa, sinks8, nb, [s_fo, s_bh, s_ba])
    wf_bh, wf_ba, wf_fi = by_cols(g_bh), by_cols(g_ba), by_cols(g_fi)
    wf_out = g_out.reshape(D_MODEL, D_MODEL)
    wf_fo = g_fo.reshape(D_FF, D_MODEL)
    dh1, dau, sact, dr2, mixin, h1b, og, oatt, loss_part, dg2, db2 = mid_front_ffn(
        h0, pg, og, oatt, ts, wf_bh, wf_ba, wf_out, wf_fi, wf_fo, ln1_g, ln1_b, ln2_g, ln2_b)
    dh0p, dpg, dog, doa, dyh, dya, dr1, dg1, db1 = mid_back(dh1, h0, pg, og, oatt, wf_bh, wf_ba, wf_out, ln1_g, ln1_b)
    tp = max(t for t in (768, 512, TM) if h0.shape[0] % t == 0)
    pieces = lambda g: g.reshape(N_SHARD, 2, -1, g.shape[-1])
    gb_bh = pieces(wgrad(og, dyh, "wgrad_bh", 512, D_MODEL, tp, True, bf16))
    gb_ba = pieces(wgrad(oatt, dya, "wgrad_ba", 512, D_MODEL, tp, True, bf16))
    gb_out = pieces(wgrad(mixin, dr1, "wgrad_out", D_MODEL, D_MODEL, tp, False, bf16))
    gb_fi = pieces(wgrad(h1b, dau, "wgrad_fi", D_MODEL, D_FF, tp, True, bf16))
    gb_fo = pieces(wgrad(sact, dr2, "wgrad_fo", D_FF // 2, D_MODEL, tp, False, bf16))
    dhq, dhf, dhi, dhg, dlb4, dng, r_fi, r_fo = hgrn_bwd(pa, hg_lower_bounds, ng, sprev, dog, nb, [gb_fi, gb_fo])
    daq, dkc, dkp, dvc, dvp, dkm, dvm, dsk, r_out, r_bh, r_ba = attn_bwd(pa, sinks8, doa, nb,
                                                                         [gb_out, gb_bh, gb_ba])
    dproj, dx, dmeta, dlg, dlb = inproj_bwd(dh0p, dhq, dhf, dhi, dhg, daq, dkc, dkp, dvc, dvp, dkm, dvm, dpg,
                                      wf_in, xs, metablk, vec(ln_emb_g), vec(ln_emb_b), cos, sin)
    gw_in = wgrad(dproj, h0b, "wgrad_in", IN_W // 2, D_MODEL, tp, False, bf16).reshape(N_SHARD, -1, D_MODEL)

    parts = [(dlg, 8), (dlb, 8), (dlb4, 8), (dng, 8), (dsk[:, 0], 8),
             (dg1, 8), (db1, 8), (dg2, 8), (db2, 8), (dmeta, BLOCK), (loss_part, 8)]
    small = jnp.concatenate([_rows128(a, r) for a, r in parts], axis=0)

    c_idx = jnp.reshape(ic, (1,)).astype(jnp.int32)
    jc_idx = jnp.stack([shard, ic]).astype(jnp.int32)
    other_in, gathered = pair_exchange_halves([gw_in], small)
    sum_in = add_pair(gw_in, other_in, c_idx, "add_pair_in")
    quad_in, = chip_exchange([sum_in])
    halves = [add_four(sum_in, quad_in, jc_idx, "add_four_in")]
    halves += [add_eight(g, r, jc_idx, "add_eight_" + nm) for nm, g, r in
               (("bh", gb_bh, r_bh), ("ba", gb_ba, r_ba), ("out", gb_out, r_out), ("fi", gb_fi, r_fi),
                ("fo", gb_fo, r_fo))]
    red = [r.reshape(-1, r.shape[-1]) for r in pair_exchange_results(halves)]
    small_sum = sum_devices(gathered)

    offs, acc = [], 0
    for _, r in parts:
        offs.append(acc)
        acc += r
    take = lambda n, size: small_sum[offs[n]:offs[n] + parts[n][1]].reshape(-1)[:size]
    g_meta_full = take(9, N_META * D_MODEL).reshape(N_META, D_MODEL)
    g_small = {
        "meta_tokens": lax.dynamic_slice_in_dim(g_meta_full, shard * (D_MODEL // N_SHARD), D_MODEL // N_SHARD, axis=1),
        "ln_emb_g": take(0, D_MODEL), "ln_emb_b": take(1, D_MODEL),
        "hg_lower_bounds": take(2, 2 * HG_HEADS * HG_K).reshape(2, HG_HEADS * HG_K),
        "hg_norm_g": take(3, HG_K).reshape(1, HG_K), "attn_sinks": take(4, ATT_HEADS).reshape(1, ATT_HEADS),
        "ln1_g": take(5, D_MODEL).reshape(1, D_MODEL), "ln1_b": take(6, D_MODEL).reshape(1, D_MODEL),
        "ln2_g": take(7, D_MODEL).reshape(1, D_MODEL), "ln2_b": take(8, D_MODEL).reshape(1, D_MODEL),
    }
    g_big = {"w_in": red[0], "w_branch_hg": red[1], "w_branch_attn": red[2], "w_out": red[3],
             "w_ffn_in": red[4], "w_ffn_out": red[5]}

    names = ["meta_tokens", "ln_emb_g", "ln_emb_b", "w_in", "hg_lower_bounds", "hg_norm_g", "attn_sinks",
             "w_branch_hg", "w_branch_attn", "w_out", "ln1_g", "ln1_b", "w_ffn_in", "w_ffn_out", "ln2_g", "ln2_b"]
    given = dict(
        meta_tokens=(meta_tokens, m_meta_tokens, v_meta_tokens), ln_emb_g=(ln_emb_g, m_ln_emb_g, v_ln_emb_g),
        ln_emb_b=(ln_emb_b, m_ln_emb_b, v_ln_emb_b), w_in=(w_in, m_w_in, v_w_in),
        hg_lower_bounds=(hg_lower_bounds, m_hg_lower_bounds, v_hg_lower_bounds),
        hg_norm_g=(hg_norm_g, m_hg_norm_g, v_hg_norm_g), attn_sinks=(attn_sinks, m_attn_sinks, v_attn_sinks),
        w_branch_hg=(w_branch_hg, m_w_branch_hg, v_w_branch_hg),
        w_branch_attn=(w_branch_attn, m_w_branch_attn, v_w_branch_attn), w_out=(w_out, m_w_out, v_w_out),
        ln1_g=(ln1_g, m_ln1_g, v_ln1_g), ln1_b=(ln1_b, m_ln1_b, v_ln1_b), w_ffn_in=(w_ffn_in, m_w_ffn_in, v_w_ffn_in),
        w_ffn_out=(w_ffn_out, m_w_ffn_out, v_w_ffn_out), ln2_g=(ln2_g, m_ln2_g, v_ln2_g), ln2_b=(ln2_b, m_ln2_b, v_ln2_b))
    two_d = lambda a: a.reshape(8, BLOCK) if a.ndim == 1 else a.reshape(a.shape[-2], a.shape[-1])
    small_names = [nm for nm in names if nm not in g_big]
    small_d, small_m, small_v = adamw_small([two_d(given[nm][0]) for nm in small_names],
                                            [two_d(g_small[nm]) for nm in small_names],
                                            [two_d(given[nm][1]) for nm in small_names],
                                            [two_d(given[nm][2]) for nm in small_names])
    out_g, out_d, out_m, out_v = [], [], [], []
    for nm in names:
        w, m, v = given[nm]
        shape = w.shape
        if nm == "w_in":
            t = lambda a: jnp.swapaxes(two_d(a), 0, 1)
            g, d, mn, vn = [t(a) for a in adamw(t(w), g_big[nm], t(m), t(v), "adamw_" + nm)]
        elif nm in g_big:
            g, d, mn, vn = adamw(two_d(w), g_big[nm], two_d(m), two_d(v), "adamw_" + nm)
        else:
            k = small_names.index(nm)
            g, d, mn, vn = g_small[nm], small_d[k], small_m[k], small_v[k]
        out_g.append(g.reshape(shape))
        out_d.append(d.reshape(shape))
        out_m.append(mn.reshape(shape))
        out_v.append(vn.reshape(shape))

    loss = take(10, 1)[0]
    grad_x = dx.reshape(x.shape)
    return (loss, grad_x, *out_g, *out_d, *out_m, *out_v)
```

```python
import functools

import jax
import jax.numpy as jnp
from jax import lax
from jax.experimental import pallas as pl
from jax.experimental.pallas import tpu as pltpu

f32 = jnp.float32
bf16 = jnp.bfloat16

D_MODEL = 1024
BLOCK = 128
N_META = 16
PAD = BLOCK - N_META
HG_HEADS = 4
HG_K = 128
ATT_HEADS = 8
HEAD_DIM = 64
ATT_QW = ATT_HEADS * HEAD_DIM
D_FF = 2816
EPS = 1e-5
ALPHA = 2.0 ** 0.25
ROPE_THETA = 10000.0
N_A = 2816
N_G = 2048
IN_W = N_A + N_G
N_SHARD = 4
N_DEV = 8

ADAM_LR = 0.001
ADAM_B1 = 0.9
ADAM_B2 = 0.999
ADAM_EPS = 1e-08
ADAM_WD = 0.01
ADAM_STEP = 10

TM = 256
LEAD = TM // BLOCK - 1

RING = 3

VMEM_LIMIT = 56 * 1024 * 1024
VMEM_LIMIT_ALL_WEIGHTS = 62 * 1024 * 1024
MESH = pl.DeviceIdType.MESH


def _cparams(sem, vmem=VMEM_LIMIT):
    return pltpu.CompilerParams(dimension_semantics=sem, vmem_limit_bytes=vmem)


def _const_spec(shape):
    zeros = (0,) * len(shape)
    return pl.BlockSpec(shape, lambda *_: zeros, pipeline_mode=pl.Buffered(1))


def _dot(a, b, ca, cb):
    return lax.dot_general(a.astype(bf16), b.astype(bf16), (((ca,), (cb,)), ((), ())),
                           preferred_element_type=f32)


@jax.custom_vjp
def mm(a, b):
    return _dot(a, b, 1, 0)


mm.defvjp(lambda a, b: (_dot(a, b, 1, 0), (a, b)),
          lambda r, g: (_dot(g, r[1], 1, 1), _dot(r[0], g, 0, 0)))


@jax.custom_vjp
def mm_nt(a, b):
    return _dot(a, b, 1, 1)


mm_nt.defvjp(lambda a, b: (_dot(a, b, 1, 1), (a, b)),
             lambda r, g: (_dot(g, r[1], 1, 0), _dot(g, r[0], 0, 0)))


@jax.custom_vjp
def mm_tn(a, b):
    return _dot(a, b, 0, 0)


mm_tn.defvjp(lambda a, b: (_dot(a, b, 0, 0), (a, b)),
             lambda r, g: (_dot(r[1], g, 1, 1), _dot(r[0], g, 1, 0)))


@functools.partial(jax.custom_vjp, nondiff_argnums=(1,))
def roll_lanes(x, shift):
    return pltpu.roll(x, shift, 1)


roll_lanes.defvjp(lambda x, shift: (pltpu.roll(x, shift, 1), None),
                  lambda shift, _, g: (pltpu.roll(g, (128 - shift) % 128, 1),))


@jax.custom_vjp
def _sigmoid(x):
    return 1.0 / (1.0 + jnp.exp(-x))


def _sigmoid_fwd(x):
    s = 1.0 / (1.0 + jnp.exp(-x))
    return s, s


_sigmoid.defvjp(_sigmoid_fwd, lambda s, g: (g * s * (1.0 - s),))


@jax.custom_vjp
def _recip(x):
    return 1.0 / x


def _recip_fwd(x):
    r = 1.0 / x
    return r, r


_recip.defvjp(_recip_fwd, lambda r, g: (-g * r * r,))


def _ln_stats(x):
    mu = jnp.mean(x, axis=-1, keepdims=True)
    xc = x - mu
    var = jnp.mean(xc * xc, axis=-1, keepdims=True)
    rs = lax.rsqrt(var + EPS)
    return xc * rs, rs


def _ln_bwd(dy, xh, rs, g):
    dxh = dy * g
    m1 = jnp.mean(dxh, axis=-1, keepdims=True)
    m2 = jnp.mean(dxh * xh, axis=-1, keepdims=True)
    return rs * (dxh - m1 - xh * m2)


def _row_ids(i):
    return i * BLOCK + lax.broadcasted_iota(jnp.int32, (BLOCK, 1), 0)


def _tm_rows(i):
    return i * TM + lax.broadcasted_iota(jnp.int32, (TM, 1), 0)


def _tm_row(n):
    return pl.BlockSpec((TM, n), lambda i: (i, 0))


def _tm_tokens():
    return pl.BlockSpec((TM, D_MODEL), lambda i: (jnp.maximum(i - 1, 0), 0))


Q_COL, V_COL = 4 * HG_HEADS * HG_K, N_A - BLOCK


def emb_inproj(x, metablk, g, b, w_in, cos, sin):
    nsteps = x.shape[0] // TM + 1

    def body(x_ref, mb_ref, g_ref, b_ref, w_ref, cos_ref, sin_ref, h0_ref, h0b_ref, pa_ref, pg_ref):
        i = pl.program_id(0)
        xb = jnp.where(i == 0, mb_ref[...], x_ref[...])
        xh, _ = _ln_stats(xb)
        y = xh * g_ref[...] + b_ref[...]
        y = jnp.where(_tm_rows(i) >= TM - N_META, y, 0.0)
        h0_ref[...] = y
        yb = y.astype(bf16)
        h0b_ref[...] = yb
        pa = _dot(yb, w_ref[:N_A, :], 1, 1)
        cos, sin = cos_ref[...], sin_ref[...]
        pa_ref[:, :Q_COL] = pa[:, :Q_COL]
        for c0 in range(Q_COL, V_COL, BLOCK):
            pa_ref[:, c0:c0 + BLOCK] = _rope(pa[:, c0:c0 + BLOCK], cos, sin)
        pa_ref[:, V_COL:] = pa[:, V_COL:]
        pg_ref[...] = _dot(yb, w_ref[N_A:, :], 1, 1)

    p = nsteps * TM
    row = _tm_row
    return pl.pallas_call(
        body, name="emb_inproj", grid=(nsteps,),
        in_specs=[_tm_tokens(),
                  _const_spec((TM, D_MODEL)), _const_spec((1, D_MODEL)), _const_spec((1, D_MODEL)),
                  _const_spec((IN_W, D_MODEL)), _tm_row(BLOCK), _tm_row(BLOCK)],
        out_specs=[row(D_MODEL), row(D_MODEL), row(N_A), row(N_G)],
        out_shape=[jax.ShapeDtypeStruct((p, D_MODEL), f32), jax.ShapeDtypeStruct((p, D_MODEL), bf16),
                   jax.ShapeDtypeStruct((p, N_A), f32), jax.ShapeDtypeStruct((p, N_G), f32)],
        compiler_params=_cparams(("parallel",)),
    )(x, metablk, g, b, w_in, cos, sin)


def _hgrn_chunk(valid, st, hq, hf, hi, hg, lbraw, ng):
    lb = _sigmoid(lbraw[0:1] - lbraw[1:2])
    q = hq * _sigmoid(hq)
    fg = lb + (1.0 - lb) * _sigmoid(hf)
    logf = jnp.where(valid, jnp.log(fg), 0.0)
    k = jnp.where(valid, 1.0 - fg, 0.0)
    v = hi
    r = lax.broadcasted_iota(jnp.int32, (BLOCK, BLOCK), 0)
    c = lax.broadcasted_iota(jnp.int32, (BLOCK, BLOCK), 1)
    tril = (c <= r).astype(f32)
    bcum = jnp.dot(tril, logf, precision=lax.Precision.HIGHEST, preferred_element_type=f32)
    blast = bcum[BLOCK - 1:BLOCK]
    rows = lax.broadcasted_iota(jnp.int32, (BLOCK, 1), 0)
    sub8 = lax.broadcasted_iota(jnp.int32, (BLOCK // 8, 8, HG_K), 1)
    b8 = bcum.reshape(BLOCK // 8, 8, HG_K)
    row_of_8 = lambda j: jnp.broadcast_to(b8[:, j:j + 1, :], b8.shape)
    a = jnp.where(r == c, jnp.sum(q * k, axis=-1, keepdims=True), 0.0)
    seg = BLOCK
    while seg >= 2:
        half = seg // 2
        if seg >= 8:
            bs = bcum.reshape(BLOCK // seg, seg, HG_K)
            ref = jnp.broadcast_to(bs[:, half - 1:half, :], bs.shape)
        elif seg == 4:
            ref = jnp.where(sub8 < 4, row_of_8(1), row_of_8(5))
        else:
            ref = jnp.where(sub8 < 2, row_of_8(0), jnp.where(sub8 < 4, row_of_8(2),
                                                             jnp.where(sub8 < 6, row_of_8(4), row_of_8(6))))
        ref = ref.reshape(BLOCK, HG_K)
        upper = (rows % seg) >= half
        q_up = q * jnp.exp(jnp.where(upper, bcum - ref, -jnp.inf))
        k_lo = k * jnp.exp(jnp.where(upper, -jnp.inf, ref - bcum))
        a = a + jnp.where((r // seg) == (c // seg), mm_nt(q_up, k_lo), 0.0)
        seg = half
    o = mm_nt(q * jnp.exp(bcum), st) + mm(a, v)
    st_new = st * jnp.exp(blast) + mm_tn(v, k * jnp.exp(blast - bcum))
    on = o * lax.rsqrt(jnp.mean(o * o, axis=-1, keepdims=True) + EPS) * ng
    return st_new, on * (hg * _sigmoid(hg))


def _hgrn_in_specs(rowmap):
    wide = lambda col: pl.BlockSpec((BLOCK, HG_HEADS * HG_K), lambda i: (rowmap(i) + LEAD, col))
    return [wide(0), wide(1), wide(2), wide(3), _const_spec((2, HG_HEADS * HG_K)), _const_spec((1, HG_K))]


def _head(ref, h):
    return ref[:, h * HG_K:(h + 1) * HG_K]


def hgrn_fwd(pa, lbraw, ng, nb, shards):
    n = len(shards)

    def body(hq_ref, hf_ref, hi_ref, hg_ref, lb_ref, ng_ref, *rest):
        srcs, (og_ref, sp_ref), dsts = rest[:n], rest[n:n + 2], rest[n + 2:2 * n + 2]
        st_ref = rest[2 * n + 2]
        start, wait = _shard_push(srcs, dsts, *rest[2 * n + 3:])
        i = pl.program_id(0)

        @pl.when(i == 0)
        def _():
            st_ref[...] = jnp.zeros_like(st_ref)
            start()

        @pl.when(i == nb - 1)
        def _():
            wait()

        valid = _row_ids(i) >= PAD
        for h in range(HG_HEADS):
            st = st_ref[h]
            sp_ref[0, h] = st
            st_new, out = _hgrn_chunk(valid, st, _head(hq_ref, h), _head(hf_ref, h), _head(hi_ref, h),
                                      _head(hg_ref, h), _head(lb_ref, h), ng_ref[...])
            st_ref[h] = st_new
            og_ref[:, h * HG_K:(h + 1) * HG_K] = out.astype(bf16)

    p = (nb + LEAD) * BLOCK
    push_in, push_out, push_shape, push_scratch = _push_specs(shards)
    return pl.pallas_call(
        body, name="hgrn_fwd", grid=(nb,),
        in_specs=_hgrn_in_specs(lambda i: i) + push_in,
        out_specs=[pl.BlockSpec((BLOCK, HG_HEADS * HG_K), lambda i: (i + LEAD, 0)),
                   pl.BlockSpec((1, HG_HEADS, HG_K, HG_K), lambda i: (i, 0, 0, 0))] + push_out,
        out_shape=[jax.ShapeDtypeStruct((p, HG_HEADS * HG_K), bf16),
                   jax.ShapeDtypeStruct((nb, HG_HEADS, HG_K, HG_K), f32)] + push_shape,
        scratch_shapes=[pltpu.VMEM((HG_HEADS, HG_K, HG_K), f32)] + push_scratch,
        compiler_params=_cparams(("arbitrary",)),
    )(pa, pa, pa, pa, lbraw, ng, *shards)


def hgrn_bwd(pa, lbraw, ng, sprev, dog, nb, grads):
    n = len(grads)

    def body(hq_ref, hf_ref, hi_ref, hg_ref, lb_ref, ng_ref, sp_ref, do_ref, *rest):
        srcs, rest = rest[:n], rest[n:]
        dq_ref, df_ref, di_ref, dg_ref, dlb_ref, dng_ref = rest[:6]
        dsts, dst_ref = rest[6:6 + n], rest[6 + n]
        start, wait = _grad_push(srcs, dsts, *rest[7 + n:])
        i = pl.program_id(0)

        @pl.when(i == 0)
        def _():
            dst_ref[...] = jnp.zeros_like(dst_ref)
            dlb_ref[...] = jnp.zeros_like(dlb_ref)
            dng_ref[...] = jnp.zeros_like(dng_ref)
            start()

        valid = _row_ids(nb - 1 - i) >= PAD
        dng_sum = jnp.zeros((1, HG_K), f32)
        for h in range(HG_HEADS):
            cols = slice(h * HG_K, (h + 1) * HG_K)
            _, vjp = jax.vjp(functools.partial(_hgrn_chunk, valid), sp_ref[0, h], _head(hq_ref, h), _head(hf_ref, h),
                             _head(hi_ref, h), _head(hg_ref, h), _head(lb_ref, h), ng_ref[...])
            dst, dq, df, di, dg, dlb, dng = vjp((dst_ref[h], _head(do_ref, h)))
            dst_ref[h] = dst
            dq_ref[:, cols] = dq.astype(bf16)
            df_ref[:, cols] = df.astype(bf16)
            di_ref[:, cols] = di.astype(bf16)
            dg_ref[:, cols] = dg.astype(bf16)
            dlb_ref[:, cols] += dlb
            dng_sum = dng_sum + dng
        dng_ref[...] += dng_sum
        pl.when(i == nb - 1)(wait)

    p = (nb + LEAD) * BLOCK
    rev = lambda i: nb - 1 - i
    hw = HG_HEADS * HG_K
    blk = pl.BlockSpec((BLOCK, hw), lambda i: (rev(i) + LEAD, 0))
    wide = jax.ShapeDtypeStruct((p, hw), bf16)
    push_in, push_out, push_shape, push_scratch = _grad_push_specs(grads)
    return pl.pallas_call(
        body, name="hgrn_bwd", grid=(nb,),
        in_specs=_hgrn_in_specs(rev) + [pl.BlockSpec((1, HG_HEADS, HG_K, HG_K), lambda i: (rev(i), 0, 0, 0)), blk]
        + push_in,
        out_specs=[blk, blk, blk, blk, pl.BlockSpec((2, hw), lambda i: (0, 0)), pl.BlockSpec((1, HG_K), lambda i: (0, 0))]
        + push_out,
        out_shape=[wide, wide, wide, wide, jax.ShapeDtypeStruct((2, hw), f32), jax.ShapeDtypeStruct((1, HG_K), f32)]
        + push_shape,
        scratch_shapes=[pltpu.VMEM((HG_HEADS, HG_K, HG_K), f32)] + push_scratch,
        compiler_params=_cparams(("arbitrary",)),
    )(pa, pa, pa, pa, lbraw, ng, sprev, dog, *grads)


def _rot_half(x):
    lane = lax.broadcasted_iota(jnp.int32, x.shape, 1)
    return jnp.where(lane % HEAD_DIM < HEAD_DIM // 2, -pltpu.roll(x, BLOCK - HEAD_DIM // 2, 1),
                     pltpu.roll(x, HEAD_DIM // 2, 1))


def _rope(x, cos, sin):
    return x * cos + _rot_half(x) * sin


def _rope_transposed(g, cos, sin):
    return g * cos - _rot_half(g * sin)


def _both_halves(x, g):
    lo = lax.broadcasted_iota(jnp.int32, x.shape, 1) < HEAD_DIM
    sw = roll_lanes(x, HEAD_DIM)
    return jnp.where(lo, x, sw) if g == 0 else jnp.where(lo, sw, x)


def _attn_block(band_ok, meta_ok, q, kp, kc, vp, vc, km, vm, *sinks):
    neg = jnp.finfo(f32).min
    scale = HEAD_DIM ** -0.5
    group = ATT_HEADS // 2
    lo = lax.broadcasted_iota(jnp.int32, (BLOCK, BLOCK), 1) < HEAD_DIM
    t = lax.broadcasted_iota(jnp.int32, (group * BLOCK, BLOCK), 0) % BLOCK
    col = lax.broadcasted_iota(jnp.int32, (group * BLOCK, BLOCK), 1)
    own = col <= t
    is_sink = col == N_META
    qr = [q[:, m * BLOCK:(m + 1) * BLOCK] for m in range(ATT_HEADS // 2)]
    slabs = []
    for g in range(2):
        kp_g, kc_g, vp_g, vc_g, km_g, vm_g = [_both_halves(a, g) for a in (kp, kc, vp, vc, km, vm)]
        qs = jnp.concatenate([jnp.where(lo if h % 2 == 0 else ~lo, qr[2 * g + h // 2], 0.0) for h in range(group)],
                             axis=0)
        sink = jnp.concatenate([jnp.broadcast_to(sinks[group * g + h], (BLOCK, 1)) for h in range(group)], axis=0)
        sb = jnp.where(band_ok, jnp.where(own, mm_nt(qs, kc_g), mm_nt(qs, kp_g)) * scale, neg)
        no_keys = jnp.zeros((BLOCK - N_META, BLOCK), f32)
        sme = jnp.where(meta_ok, mm_nt(qs, jnp.concatenate([km_g, no_keys], axis=0)) * scale,
                        jnp.where(is_sink, sink, neg))
        mx = lax.stop_gradient(jnp.max(jnp.maximum(sb, sme), axis=-1, keepdims=True))
        eb, em = jnp.exp(sb - mx), jnp.exp(sme - mx)
        inv = _recip(jnp.sum(eb + em, axis=-1, keepdims=True))
        pb = eb * inv
        o = (mm(jnp.where(own, pb, 0.0), vc_g) + mm(jnp.where(own, 0.0, pb), vp_g)
             + mm(em * inv, jnp.concatenate([vm_g, no_keys], axis=0)))
        for m in range(2):
            even, odd = o[2 * m * BLOCK:(2 * m + 1) * BLOCK], o[(2 * m + 1) * BLOCK:(2 * m + 2) * BLOCK]
            slabs.append(jnp.where(lo, even, odd))
    return jnp.concatenate(slabs, axis=1)


def _attn_masks(i):
    group = ATT_HEADS // 2
    t = lax.broadcasted_iota(jnp.int32, (group * BLOCK, BLOCK), 0) % BLOCK
    s = lax.broadcasted_iota(jnp.int32, (group * BLOCK, BLOCK), 1)
    kpos = jnp.where(s <= t, i * BLOCK - PAD + s, jnp.where(i > 0, (i - 1) * BLOCK - PAD + s, -1))
    band_ok = kpos >= N_META
    qpos = i * BLOCK - PAD + lax.broadcasted_iota(jnp.int32, (group * BLOCK, 1), 0) % BLOCK
    meta_ok = (s < N_META) & (s <= qpos)
    return band_ok, meta_ok


def _attn_in_specs():
    cur, prev, first = (lambda i: i), (lambda i: jnp.maximum(i - 1, 0)), (lambda i: 0)
    kcol, vcol = V_COL // BLOCK - 1, V_COL // BLOCK
    blk = lambda rowmap, col: pl.BlockSpec((BLOCK, BLOCK), lambda i: (rowmap(i) + LEAD, col))
    return [pl.BlockSpec((BLOCK, ATT_QW), lambda i: (i + LEAD, Q_COL // ATT_QW)),
            blk(prev, kcol), blk(cur, kcol), blk(prev, vcol), blk(cur, vcol), blk(first, kcol), blk(first, vcol),
            _const_spec((ATT_HEADS, BLOCK))]


def _attn_row(n):
    return pl.BlockSpec((BLOCK, n), lambda i: (i + LEAD, 0))


def _attn_operands(q_ref, kp_ref, kc_ref, vp_ref, vc_ref, km_ref, vm_ref, sk_ref):
    args = (q_ref[...], kp_ref[...], kc_ref[...], vp_ref[...], vc_ref[...], km_ref[PAD:, :], vm_ref[PAD:, :])
    sinks = tuple(sk_ref[j:j + 1, 0:1] for j in range(ATT_HEADS))
    return args + sinks


def attn_fwd(pa, sinks8, nb, shards):
    n = len(shards)
    n_in = 8

    def body(*refs):
        srcs, o_ref, dsts = refs[n_in:n_in + n], refs[n_in + n], refs[n_in + n + 1:n_in + 2 * n + 1]
        start, wait = _shard_push(srcs, dsts, *refs[n_in + 2 * n + 1:])
        i = pl.program_id(0)
        pl.when(i == 0)(start)
        band_ok, meta_ok = _attn_masks(i)
        o_ref[...] = _attn_block(band_ok, meta_ok, *_attn_operands(*refs[:n_in])).astype(bf16)
        pl.when(i == nb - 1)(wait)

    push_in, push_out, push_shape, push_scratch = _push_specs(shards)
    return pl.pallas_call(
        body, name="attn_fwd", grid=(nb,), in_specs=_attn_in_specs() + push_in,
        out_specs=[_attn_row(ATT_QW)] + push_out,
        out_shape=[jax.ShapeDtypeStruct(((nb + LEAD) * BLOCK, ATT_QW), bf16)] + push_shape,
        scratch_shapes=push_scratch,
        compiler_params=_cparams(("arbitrary",)),
    )(pa, pa, pa, pa, pa, pa, pa, sinks8, *shards)


def attn_bwd(pa, sinks8, do, nb, grads):
    n = len(grads)

    def body(*refs):
        do_ref, srcs = refs[8], refs[9:9 + n]
        dq_ref, dkc_ref, dkp_ref, dvc_ref, dvp_ref, dkm_ref, dvm_ref, dsk_ref = refs[9 + n:17 + n]
        start, wait = _grad_push(srcs, refs[17 + n:17 + 2 * n], *refs[17 + 2 * n:])
        i = pl.program_id(0)

        @pl.when(i == 0)
        def _():
            dkm_ref[...] = jnp.zeros((N_META, BLOCK), f32)
            dvm_ref[...] = jnp.zeros((N_META, BLOCK), f32)
            dsk_ref[...] = jnp.zeros((ATT_HEADS, BLOCK), f32)
            start()

        band_ok, meta_ok = _attn_masks(i)
        _, vjp = jax.vjp(functools.partial(_attn_block, band_ok, meta_ok), *_attn_operands(*refs[:8]))
        grads = vjp(do_ref[...])
        dq_ref[...] = grads[0]
        dkp_ref[...] = grads[1]
        dkc_ref[...] = grads[2]
        dvp_ref[...] = grads[3]
        dvc_ref[...] = grads[4]
        dkm_ref[...] += grads[5]
        dvm_ref[...] += grads[6]
        for j in range(ATT_HEADS):
            dsk_ref[j:j + 1, :] += jnp.broadcast_to(grads[7 + j], (1, BLOCK))
        pl.when(i == nb - 1)(wait)

    p = (nb + LEAD) * BLOCK
    row = _attn_row(BLOCK)
    const = lambda r: pl.BlockSpec((r, BLOCK), lambda i: (0, 0))
    part = jax.ShapeDtypeStruct((p, BLOCK), f32)
    push_in, push_out, push_shape, push_scratch = _grad_push_specs(grads)
    return pl.pallas_call(
        body, name="attn_bwd", grid=(nb,),
        in_specs=_attn_in_specs() + [_attn_row(ATT_QW)] + push_in,
        out_specs=[_attn_row(ATT_QW), row, row, row, row,
                   const(N_META), const(N_META), const(ATT_HEADS)] + push_out,
        out_shape=[jax.ShapeDtypeStruct((p, ATT_QW), f32), part, part, part, part,
                   jax.ShapeDtypeStruct((N_META, BLOCK), f32), jax.ShapeDtypeStruct((N_META, BLOCK), f32),
                   jax.ShapeDtypeStruct((ATT_HEADS, BLOCK), f32)] + push_shape,
        scratch_shapes=push_scratch,
        compiler_params=_cparams(("arbitrary",)),
    )(pa, pa, pa, pa, pa, pa, pa, sinks8, do, *grads)


def _mid_forward(h0_ref, pg_ref, og, oa, wbh_ref, wba_ref, wo_ref, g1, b1):
    yh = jnp.dot(og, wbh_ref[...], preferred_element_type=f32)
    ya = jnp.dot(oa, wba_ref[...], preferred_element_type=f32)
    gh = _sigmoid(pg_ref[:, :D_MODEL])
    ga = _sigmoid(pg_ref[:, D_MODEL:])
    mixin = (gh * yh + ga * ya).astype(bf16)
    r1 = ALPHA * h0_ref[...] + jnp.dot(mixin, wo_ref[...], preferred_element_type=f32)
    xh1, rs1 = _ln_stats(r1)
    return yh, ya, gh, ga, mixin, xh1, rs1, xh1 * g1 + b1


def _mid_weight_specs():
    hw = HG_HEADS * HG_K
    return [_const_spec((hw, D_MODEL)), _const_spec((ATT_QW, D_MODEL)), _const_spec((D_MODEL, D_MODEL)),
            _const_spec((1, D_MODEL)), _const_spec((1, D_MODEL))]


def mid_front_ffn(h0, pg, og, oatt, target, wbh, wba, wout, wfi, wfo, ln1g, ln1b, ln2g, ln2b):
    def body(h0_ref, pg_ref, og_ref, oa_ref, t_ref, wbh_ref, wba_ref, wo_ref, g1_ref, b1_ref, wfi_ref, wfo_ref,
             g2_ref, b2_ref, dh1_ref, dau_ref, s_ref, dr2_ref, mix_ref, h1b_ref, ogc_ref, oac_ref,
             loss_ref, dg2_ref, db2_ref):
        i = pl.program_id(0)

        @pl.when(i == 0)
        def _():
            for r in (loss_ref, dg2_ref, db2_ref):
                r[...] = jnp.zeros_like(r)

        used = _tm_rows(i) >= LEAD * BLOCK
        og = jnp.where(used, og_ref[...], jnp.zeros_like(og_ref))
        oa = jnp.where(used, oa_ref[...], jnp.zeros_like(oa_ref))
        ogc_ref[...] = og
        oac_ref[...] = oa
        *_, mixin, _, _, h1 = _mid_forward(h0_ref, pg_ref, og, oa, wbh_ref, wba_ref, wo_ref, g1_ref[...], b1_ref[...])
        mix_ref[...] = mixin
        h1b = h1.astype(bf16)
        h1b_ref[...] = h1b
        g2, b2 = g2_ref[...], b2_ref[...]
        au = jnp.dot(h1b, wfi_ref[...], preferred_element_type=f32)
        a, u = au[:, :D_FF], au[:, D_FF:]
        sg = _sigmoid(a)
        sa = a * sg
        s = (sa * u).astype(bf16)
        s_ref[...] = s
        r2 = ALPHA * h1 + jnp.dot(s, wfo_ref[...], preferred_element_type=f32)
        xh2, rs2 = _ln_stats(r2)
        diff = jnp.where(i > 0, xh2 * g2 + b2 - t_ref[...], 0.0)
        loss_ref[...] += jnp.sum(diff * diff) * (0.5 / D_MODEL)
        dy = diff * (1.0 / D_MODEL)
        dg2_ref[...] += jnp.sum(dy * xh2, axis=0, keepdims=True)
        db2_ref[...] += jnp.sum(dy, axis=0, keepdims=True)
        dr2 = _ln_bwd(dy, xh2, rs2, g2)
        dr2b = dr2.astype(bf16)
        dr2_ref[...] = dr2b
        ds = _dot(dr2b, wfo_ref[...], 1, 1)
        da = (ds * u) * (sg * (1.0 + a * (1.0 - sg)))
        du = ds * sa
        dau = jnp.concatenate([da, du], axis=1).astype(bf16)
        dau_ref[...] = dau
        dh1_ref[...] = ALPHA * dr2 + _dot(dau, wfi_ref[...], 1, 1)

    p = h0.shape[0]
    hw = HG_HEADS * HG_K
    vec = lambda: pl.BlockSpec((1, D_MODEL), lambda i: (0, 0))
    sds = lambda n, dt: jax.ShapeDtypeStruct((p, n), dt)
    return pl.pallas_call(
        body, name="mid_front_ffn", grid=(p // TM,),
        in_specs=[_tm_row(D_MODEL), _tm_row(N_G), _tm_row(hw), _tm_row(ATT_QW), _tm_tokens()] + _mid_weight_specs()
        + [_const_spec((D_MODEL, 2 * D_FF)), _const_spec((D_FF, D_MODEL)), _const_spec((1, D_MODEL)),
           _const_spec((1, D_MODEL))],
        out_specs=[_tm_row(D_MODEL), _tm_row(2 * D_FF), _tm_row(D_FF), _tm_row(D_MODEL), _tm_row(D_MODEL),
                   _tm_row(D_MODEL), _tm_row(hw), _tm_row(ATT_QW), pl.BlockSpec((1, 1), lambda i: (0, 0)), vec(), vec()],
        out_shape=[sds(D_MODEL, f32), sds(2 * D_FF, bf16), sds(D_FF, bf16), sds(D_MODEL, bf16), sds(D_MODEL, bf16),
                   sds(D_MODEL, bf16), sds(hw, bf16), sds(ATT_QW, bf16),
                   jax.ShapeDtypeStruct((1, 1), f32)] + [jax.ShapeDtypeStruct((1, D_MODEL), f32)] * 2,
        compiler_params=_cparams(("arbitrary",), VMEM_LIMIT_ALL_WEIGHTS),
    )(h0, pg, og, oatt, target, wbh, wba, wout, ln1g, ln1b, wfi, wfo, ln2g, ln2b)


def mid_back(dh1, h0, pg, ogc, oac, wbh, wba, wout, ln1g, ln1b):
    nsteps = h0.shape[0] // TM

    def body(dh1_hbm, h0_hbm, pg_hbm, og_ref, oa_ref, wbh_ref, wba_ref, wo_ref, g1_ref, b1_ref,
             dh0_ref, dpg_ref, dog_ref, doa_ref, dyh_ref, dya_ref, dr1_ref, dg1_ref, db1_ref,
             dh1_buf, h0_buf, pg_buf, sems):
        i = pl.program_id(0)
        streams = ((dh1_hbm, dh1_buf), (h0_hbm, h0_buf), (pg_hbm, pg_buf))

        def fetch(step):
            slot = step % RING
            rows = pl.ds(pl.multiple_of(step * TM, TM), TM)
            return [pltpu.make_async_copy(hbm.at[rows, :], buf.at[slot], sems.at[k, slot])
                    for k, (hbm, buf) in enumerate(streams)]

        @pl.when(i == 0)
        def _():
            dg1_ref[...] = jnp.zeros_like(dg1_ref)
            db1_ref[...] = jnp.zeros_like(db1_ref)
            for step in range(min(RING - 1, nsteps)):
                for cp in fetch(step):
                    cp.start()

        @pl.when(i + RING - 1 < nsteps)
        def _():
            for cp in fetch(i + RING - 1):
                cp.start()

        for cp in fetch(i):
            cp.wait()
        slot = i % RING
        g1 = g1_ref[...]
        yh, ya, gh, ga, _, xh1, rs1, _ = _mid_forward(h0_buf.at[slot], pg_buf.at[slot], og_ref[...], oa_ref[...],
                                                      wbh_ref, wba_ref, wo_ref, g1, b1_ref[...])
        dh1 = dh1_buf[slot]
        dg1_ref[...] += jnp.sum(dh1 * xh1, axis=0, keepdims=True)
        db1_ref[...] += jnp.sum(dh1, axis=0, keepdims=True)
        dr1 = _ln_bwd(dh1, xh1, rs1, g1)
        dr1b = dr1.astype(bf16)
        dr1_ref[...] = dr1b
        dh0_ref[...] = ALPHA * dr1
        dmix = _dot(dr1b, wo_ref[...], 1, 1)
        dyh = (dmix * gh).astype(bf16)
        dya = (dmix * ga).astype(bf16)
        dyh_ref[...] = dyh
        dya_ref[...] = dya
        dpg_ref[:, :D_MODEL] = (dmix * yh * gh * (1.0 - gh)).astype(bf16)
        dpg_ref[:, D_MODEL:] = (dmix * ya * ga * (1.0 - ga)).astype(bf16)
        dog_ref[...] = _dot(dyh, wbh_ref[...], 1, 1)
        doa_ref[...] = _dot(dya, wba_ref[...], 1, 1)

    p = h0.shape[0]
    hw = HG_HEADS * HG_K
    vec = lambda: pl.BlockSpec((1, D_MODEL), lambda i: (0, 0))
    sds = lambda n, dt: jax.ShapeDtypeStruct((p, n), dt)
    return pl.pallas_call(
        body, name="mid_back", grid=(nsteps,),
        in_specs=[ANY, ANY, ANY, _tm_row(hw), _tm_row(ATT_QW)] + _mid_weight_specs(),
        scratch_shapes=[pltpu.VMEM((RING, TM, D_MODEL), f32), pltpu.VMEM((RING, TM, D_MODEL), f32),
                        pltpu.VMEM((RING, TM, N_G), f32), pltpu.SemaphoreType.DMA((3, RING))],
        out_specs=[_tm_row(D_MODEL), _tm_row(N_G), _tm_row(hw), _tm_row(ATT_QW), _tm_row(D_MODEL), _tm_row(D_MODEL),
                   _tm_row(D_MODEL), vec(), vec()],
        out_shape=[sds(D_MODEL, f32), sds(N_G, bf16), sds(hw, f32), sds(ATT_QW, f32), sds(D_MODEL, bf16),
                   sds(D_MODEL, bf16), sds(D_MODEL, bf16)] + [jax.ShapeDtypeStruct((1, D_MODEL), f32)] * 2,
        compiler_params=_cparams(("arbitrary",)),
    )(dh1, h0, pg, ogc, oac, wbh, wba, wout, ln1g, ln1b)


def inproj_bwd(dh0p, dhq, dhf, dhi, dhg, daq, dkc, dkp, dvc, dvp, dkm, dvm, dpg, w_in, x, metablk, g, b, cos, sin):
    p = dh0p.shape[0]
    nbk = p // BLOCK
    per = TM // BLOCK

    def body(dh0_ref, dq_ref, df_ref, di_ref, dg_ref, daq_ref, dkc_ref, *rest):
        dkp_refs, dvc_ref, dvp_refs = rest[:per], rest[per], rest[per + 1:2 * per + 1]
        (dkm_ref, dvm_ref, dpg_ref, w_ref, x_ref, mb_ref, g_ref, b_ref, cos_ref, sin_ref,
         dproj_ref, dx_ref, dmeta_ref, dlg_ref, dlb_ref) = rest[2 * per + 1:]
        i = pl.program_id(0)

        @pl.when(i == 0)
        def _():
            dlg_ref[...] = jnp.zeros_like(dlg_ref)
            dlb_ref[...] = jnp.zeros_like(dlb_ref)

        zero_pad = jnp.zeros((TM - N_META, BLOCK), f32)
        first = i == 0
        rows = _tm_rows(i)

        def keys(cur_ref, next_refs, meta_ref):
            nxt = jnp.concatenate([jnp.where(per * i + 1 + m < nbk, next_refs[m][...], 0.0) for m in range(per)],
                                  axis=0)
            t = cur_ref[...] + nxt
            return t + jnp.where(first, jnp.concatenate([zero_pad, meta_ref[...]], axis=0), 0.0)

        cos, sin = cos_ref[...], sin_ref[...]
        unrotate = lambda t: _rope_transposed(t, cos, sin).astype(bf16)
        dproj = jnp.concatenate(
            [dq_ref[...], df_ref[...], di_ref[...], dg_ref[...]]
            + [unrotate(daq_ref[:, m * BLOCK:(m + 1) * BLOCK]) for m in range(ATT_QW // BLOCK)]
            + [unrotate(keys(dkc_ref, dkp_refs, dkm_ref)), keys(dvc_ref, dvp_refs, dvm_ref).astype(bf16),
               dpg_ref[...]], axis=1)
        dproj = jnp.where(rows >= LEAD * BLOCK, dproj, jnp.zeros_like(dproj))
        dproj_ref[...] = dproj
        valid = rows >= TM - N_META
        dh0 = jnp.where(valid, dh0_ref[...] + _dot(dproj, w_ref[...], 1, 0), 0.0)
        xb = jnp.where(first, mb_ref[...], x_ref[...])
        xh, rs = _ln_stats(xb)
        dlg_ref[...] += jnp.sum(dh0 * xh, axis=0, keepdims=True)
        dlb_ref[...] += jnp.sum(dh0, axis=0, keepdims=True)
        dx = jnp.where(valid, _ln_bwd(dh0, xh, rs, g_ref[...]), 0.0)
        dx_ref[...] = dx

        @pl.when(first)
        def _():
            dmeta_ref[...] = dx[TM - N_META:, :]

    row = _tm_row
    nxt = [pl.BlockSpec((BLOCK, BLOCK), functools.partial(lambda i, m: (jnp.minimum(per * i + 1 + m, nbk - 1), 0), m=m))
           for m in range(per)]
    hw = HG_HEADS * HG_K
    vec = lambda: pl.BlockSpec((1, D_MODEL), lambda i: (0, 0))
    return pl.pallas_call(
        body, name="inproj_bwd", grid=(p // TM,),
        in_specs=[row(D_MODEL), row(hw), row(hw), row(hw), row(hw), row(ATT_QW),
                  row(BLOCK)] + nxt + [row(BLOCK)] + nxt + [_const_spec((N_META, BLOCK)), _const_spec((N_META, BLOCK)),
                  row(N_G), _const_spec((IN_W, D_MODEL)), _tm_tokens(),
                  _const_spec((TM, D_MODEL)), _const_spec((1, D_MODEL)), _const_spec((1, D_MODEL)),
                  row(BLOCK), row(BLOCK)],
        out_specs=[row(IN_W), _tm_tokens(), pl.BlockSpec((N_META, D_MODEL), lambda i: (0, 0)), vec(), vec()],
        out_shape=[jax.ShapeDtypeStruct((p, IN_W), bf16), jax.ShapeDtypeStruct((p - TM, D_MODEL), f32),
                   jax.ShapeDtypeStruct((N_META, D_MODEL), f32),
                   jax.ShapeDtypeStruct((1, D_MODEL), f32), jax.ShapeDtypeStruct((1, D_MODEL), f32)],
        compiler_params=_cparams(("arbitrary",)),
    )(dh0p, dhq, dhf, dhi, dhg, daq, dkc, *([dkp] * per), dvc, *([dvp] * per), dkm, dvm, dpg, w_in, x, metablk, g, b,
      cos, sin)


def wgrad(a, b, name, tk, tn, tp, by_cols, out_dtype=f32):
    p, k = a.shape
    n = b.shape[1]
    nsteps = p // tp

    def body(a_ref, b_ref, o_ref, acc_ref):
        ip = pl.program_id(2)

        @pl.when(ip == 0)
        def _():
            acc_ref[...] = jnp.zeros_like(acc_ref)

        acc_ref[...] += _dot(a_ref[...], b_ref[...], 0, 0)

        @pl.when(ip == nsteps - 1)
        def _():
            for j in range(span):
                o_ref[j] = acc_ref[:, j * width:(j + 1) * width].astype(out_dtype)

    span, width = 1, tn
    if by_cols:
        shard_n = n // N_SHARD
        out_shape = (N_SHARD, k, shard_n)
        if tn >= shard_n:
            span, width = tn // shard_n, shard_n
            omap = lambda ik, jn, ip: (jn, ik, 0)
        else:
            per = shard_n // tn
            omap = lambda ik, jn, ip: (jn // per, ik, jn % per)
    else:
        out_shape = (1, k, n)
        omap = lambda ik, jn, ip: (0, ik, jn)
    return pl.pallas_call(
        body, name=name, grid=(k // tk, n // tn, nsteps),
        in_specs=[pl.BlockSpec((tp, tk), lambda ik, jn, ip: (ip, ik)),
                  pl.BlockSpec((tp, tn), lambda ik, jn, ip: (ip, jn))],
        out_specs=pl.BlockSpec((span, tk, width), omap),
        out_shape=jax.ShapeDtypeStruct(out_shape, out_dtype),
        scratch_shapes=[pltpu.VMEM((tk, tn), f32)],
        compiler_params=_cparams(("parallel", "parallel", "arbitrary")),
    )(a, b)


def _adamw_math(w, g, m, v):
    mn = ADAM_B1 * m + (1.0 - ADAM_B1) * g
    vn = ADAM_B2 * v + (1.0 - ADAM_B2) * (g * g)
    m_hat = mn / (1.0 - ADAM_B1 ** ADAM_STEP)
    v_hat = vn / (1.0 - ADAM_B2 ** ADAM_STEP)
    return -ADAM_LR * (m_hat / (jnp.sqrt(v_hat) + ADAM_EPS) + ADAM_WD * w), mn, vn


def adamw(w, g, m, v, name):
    r, c = w.shape
    tr = r
    for cand in (256, 176, 152, 128):
        if r > cand and r % cand == 0:
            tr = cand
            break

    def body(w_ref, g_ref, m_ref, v_ref, go_ref, d_ref, mo_ref, vo_ref):
        gg = g_ref[...]
        go_ref[...] = gg
        d_ref[...], mo_ref[...], vo_ref[...] = _adamw_math(w_ref[...], gg, m_ref[...], v_ref[...])

    spec = pl.BlockSpec((tr, c), lambda i: (i, 0))
    sds = jax.ShapeDtypeStruct((r, c), f32)
    return pl.pallas_call(
        body, name=name, grid=(r // tr,), in_specs=[spec] * 4, out_specs=[spec] * 4, out_shape=[sds] * 4,
        compiler_params=_cparams(("parallel",)),
    )(w, g, m, v)


def adamw_small(ws, gs, ms, vs):
    n = len(ws)

    def body(*refs):
        ins, outs = refs[:4 * n], refs[4 * n:]
        for k in range(n):
            outs[k][...], outs[n + k][...], outs[2 * n + k][...] = _adamw_math(
                ins[k][...], ins[n + k][...], ins[2 * n + k][...], ins[3 * n + k][...])

    out = pl.pallas_call(body, name="adamw_small",
                         out_shape=[jax.ShapeDtypeStruct(w.shape, f32) for w in ws] * 3)(*ws, *gs, *ms, *vs)
    return out[:n], out[n:2 * n], out[2 * n:]


def _me():
    return lax.axis_index("x"), lax.axis_index("y"), lax.axis_index("c")


def _chip_peer(x, y, c, k):
    return (x ^ (k >> 1), y ^ (k & 1), c)


ANY = pl.BlockSpec(memory_space=pl.ANY)


def gather_weights(now, later):
    n, n_later = len(now), len(later)
    out_dtypes = [bf16 if s.size > 16 * 256 else f32 for s in now]
    halves = [(2, s.shape[0] // 2, s.shape[1]) for s in now]

    def body(*refs):
        ins, later_ins = refs[:n], refs[n:n + n_later]
        outs, later_outs = refs[n + n_later:2 * n + n_later], refs[2 * n + n_later:2 * (n + n_later)]
        stage = refs[2 * (n + n_later):3 * n + 2 * n_later]
        send_sems, recv_sems, pass_send_sems, pass_recv_sems, local_sems = refs[3 * n + 2 * n_later:]
        x, y, c = _me()
        j = 2 * x + y
        sibling = (x, y, 1 - c)

        def over_ici(w, k, slot):
            return pltpu.make_async_remote_copy(
                src_ref=stage[w].at[c], dst_ref=outs[w].at[slot, c], send_sem=send_sems.at[w, k - 1],
                recv_sem=recv_sems.at[w, k - 1], device_id=_chip_peer(x, y, c, k), device_id_type=MESH)

        def passed_on(w, k, half):
            return pltpu.make_async_remote_copy(
                src_ref=outs[w].at[j ^ k, half], dst_ref=outs[w].at[j ^ k, half], send_sem=pass_send_sems.at[w, k - 1],
                recv_sem=pass_recv_sems.at[w, k - 1], device_id=sibling, device_id_type=MESH)

        for w in range(n):
            stage[w][...] = ins[w][...].astype(out_dtypes[w]).reshape(halves[w])
        locs = []
        for w in range(n):
            loc = pltpu.make_async_copy(stage[w], outs[w].at[j], local_sems.at[w])
            loc.start()
            locs.append(loc)
            for k in (1, 2, 3):
                over_ici(w, k, j).start()
        for w in range(n_later):
            later_outs[w][...] = later_ins[w][...].astype(bf16)
        for w in range(n):
            for k in (1, 2, 3):
                over_ici(w, k, j ^ k).wait_recv()
                passed_on(w, k, c).start()
        for w in range(n):
            for k in (1, 2, 3):
                passed_on(w, k, 1 - c).wait_recv()
        for w in range(n):
            for k in (1, 2, 3):
                over_ici(w, k, j).wait_send()
                passed_on(w, k, c).wait_send()
        for loc in locs:
            loc.wait()

    vmem = pl.BlockSpec(memory_space=pltpu.VMEM)
    sem3 = pltpu.SemaphoreType.DMA((n, 3))
    return pl.pallas_call(
        body, name="gather_weights",
        in_specs=[vmem] * (n + n_later), out_specs=[ANY] * n + [vmem] * n_later,
        out_shape=[jax.ShapeDtypeStruct((N_SHARD,) + h, dt) for h, dt in zip(halves, out_dtypes)]
        + [jax.ShapeDtypeStruct(s.shape, bf16) for s in later],
        scratch_shapes=[pltpu.VMEM(h, dt) for h, dt in zip(halves, out_dtypes)]
        + [sem3, sem3, sem3, sem3, pltpu.SemaphoreType.DMA((n,))],
        compiler_params=pltpu.CompilerParams(vmem_limit_bytes=VMEM_LIMIT),
    )(*now, *later)


def _shard_push(srcs, dsts, send_sems, recv_sems, local_sems):
    def remote(w, k, slot):
        x, y, c = _me()
        return pltpu.make_async_remote_copy(
            src_ref=srcs[w], dst_ref=dsts[w].at[slot], send_sem=send_sems.at[w, k - 1],
            recv_sem=recv_sems.at[w, k - 1], device_id=_chip_peer(x, y, c, k), device_id_type=MESH)

    def local(w):
        x, y, _ = _me()
        return pltpu.make_async_copy(srcs[w], dsts[w].at[2 * x + y], local_sems.at[w])

    def start():
        x, y, _ = _me()
        for w in range(len(srcs)):
            local(w).start()
            for k in (1, 2, 3):
                remote(w, k, 2 * x + y).start()

    def wait():
        x, y, _ = _me()
        for w in range(len(srcs)):
            for k in (1, 2, 3):
                remote(w, k, (2 * x + y) ^ k).wait_recv()
        for w in range(len(srcs)):
            for k in (1, 2, 3):
                remote(w, k, 2 * x + y).wait_send()
            local(w).wait()

    return start, wait


def _grad_push(srcs, dsts, send_sems, recv_sems):
    def copy(w, k):
        x, y, c = _me()
        px, py, pc = x ^ (k >> 2), y ^ ((k >> 1) & 1), c ^ (k & 1)
        return pltpu.make_async_remote_copy(
            src_ref=srcs[w].at[2 * px + py, pc], dst_ref=dsts[w].at[k - 1], send_sem=send_sems.at[w, k - 1],
            recv_sem=recv_sems.at[w, k - 1], device_id=(px, py, pc), device_id_type=MESH)

    def start():
        for w in range(len(srcs)):
            for k in range(1, N_DEV):
                copy(w, k).start()

    def wait():
        for w in range(len(srcs)):
            for k in range(1, N_DEV):
                copy(w, k).wait_recv()
        for w in range(len(srcs)):
            for k in range(1, N_DEV):
                copy(w, k).wait_send()

    return start, wait


def _grad_push_specs(grads):
    n = len(grads)
    return ([ANY] * n, [ANY] * n, [jax.ShapeDtypeStruct((N_DEV - 1,) + g.shape[2:], g.dtype) for g in grads],
            [pltpu.SemaphoreType.DMA((n, N_DEV - 1)), pltpu.SemaphoreType.DMA((n, N_DEV - 1))])


def add_eight(own, parts, jc_idx, name):
    _, half, c = parts.shape
    tr = half // 2 if (half // 2) % 16 == 0 else half

    def body(jc_ref, own_ref, p_ref, out_ref):
        acc = own_ref[0, 0].astype(f32)
        for k in range(N_DEV - 1):
            acc = acc + p_ref[k].astype(f32)
        out_ref[0] = acc

    return pl.pallas_call(
        body, name=name,
        grid_spec=pltpu.PrefetchScalarGridSpec(
            num_scalar_prefetch=1, grid=(half // tr,),
            in_specs=[pl.BlockSpec((1, 1, tr, c), lambda t, jc: (jc[0], jc[1], t, 0)),
                      pl.BlockSpec((N_DEV - 1, tr, c), lambda t, jc: (0, t, 0))],
            out_specs=pl.BlockSpec((1, tr, c), lambda t, jc: (jc[1], t, 0))),
        out_shape=jax.ShapeDtypeStruct((2, half, c), f32),
        compiler_params=_cparams(("parallel",)),
    )(jc_idx, own, parts)


def _push_specs(shards):
    n = len(shards)
    return ([ANY] * n, [ANY] * n, [jax.ShapeDtypeStruct((N_SHARD,) + s.shape, s.dtype) for s in shards],
            [pltpu.SemaphoreType.DMA((n, 3)), pltpu.SemaphoreType.DMA((n, 3)), pltpu.SemaphoreType.DMA((n,))])


def pair_exchange_halves(grads, small):
    n = len(grads)

    def body(*refs):
        ins, small_ref = refs[:n], refs[n]
        outs, gath = refs[n + 1:2 * n + 1], refs[2 * n + 1]
        send_sems, recv_sems, s_send, s_recv, local_sem = refs[2 * n + 2:]
        x, y, c = _me()
        me = 4 * x + 2 * y + c
        sends = []
        for w in range(n):
            half = ins[w].shape[1] // 2
            cp = pltpu.make_async_remote_copy(
                src_ref=ins[w].at[:, pl.ds((1 - c) * half, half), :], dst_ref=outs[w],
                send_sem=send_sems.at[w], recv_sem=recv_sems.at[w], device_id=(x, y, 1 - c), device_id_type=MESH)
            cp.start()
            sends.append(cp)
        loc = pltpu.make_async_copy(small_ref, gath.at[me], local_sem)
        loc.start()
        for k in range(1, N_DEV):
            cp = pltpu.make_async_remote_copy(
                src_ref=small_ref, dst_ref=gath.at[me], send_sem=s_send.at[k - 1], recv_sem=s_recv.at[k - 1],
                device_id=(x ^ (k >> 2), y ^ ((k >> 1) & 1), c ^ (k & 1)), device_id_type=MESH)
            cp.start()
            sends.append(cp)
        for w in range(n):
            half = ins[w].shape[1] // 2
            pltpu.make_async_remote_copy(
                src_ref=ins[w].at[:, pl.ds(0, half), :], dst_ref=outs[w], send_sem=send_sems.at[w],
                recv_sem=recv_sems.at[w], device_id=(x, y, 1 - c), device_id_type=MESH).wait_recv()
        for k in range(1, N_DEV):
            pltpu.make_async_remote_copy(
                src_ref=small_ref, dst_ref=gath.at[me ^ k], send_sem=s_send.at[k - 1], recv_sem=s_recv.at[k - 1],
                device_id=(x ^ (k >> 2), y ^ ((k >> 1) & 1), c ^ (k & 1)), device_id_type=MESH).wait_recv()
        for cp in sends:
            cp.wait_send()
        loc.wait()

    return pl.pallas_call(
        body, name="pair_exchange_halves", in_specs=[ANY] * (n + 1), out_specs=[ANY] * (n + 1),
        out_shape=[jax.ShapeDtypeStruct((g.shape[0], g.shape[1] // 2, g.shape[2]), g.dtype) for g in grads]
        + [jax.ShapeDtypeStruct((N_DEV,) + small.shape, f32)],
        scratch_shapes=[pltpu.SemaphoreType.DMA((n,)), pltpu.SemaphoreType.DMA((n,)),
                        pltpu.SemaphoreType.DMA((N_DEV - 1,)), pltpu.SemaphoreType.DMA((N_DEV - 1,)),
                        pltpu.SemaphoreType.DMA],
    )(*grads, small)


def chip_exchange(sums):
    n = len(sums)

    def body(*refs):
        ins, outs = refs[:n], refs[n:2 * n]
        send_sems, recv_sems = refs[2 * n:]
        x, y, c = _me()
        j = 2 * x + y
        sends = []
        for w in range(n):
            for k in (1, 2, 3):
                cp = pltpu.make_async_remote_copy(
                    src_ref=ins[w].at[j ^ k], dst_ref=outs[w].at[k - 1], send_sem=send_sems.at[w, k - 1],
                    recv_sem=recv_sems.at[w, k - 1], device_id=_chip_peer(x, y, c, k), device_id_type=MESH)
                cp.start()
                sends.append(cp)
        for w in range(n):
            for k in (1, 2, 3):
                pltpu.make_async_remote_copy(
                    src_ref=ins[w].at[0], dst_ref=outs[w].at[k - 1], send_sem=send_sems.at[w, k - 1],
                    recv_sem=recv_sems.at[w, k - 1], device_id=_chip_peer(x, y, c, k), device_id_type=MESH).wait_recv()
        for cp in sends:
            cp.wait_send()

    return pl.pallas_call(
        body, name="chip_exchange", in_specs=[ANY] * n, out_specs=[ANY] * n,
        out_shape=[jax.ShapeDtypeStruct((N_SHARD - 1,) + s.shape[1:], s.dtype) for s in sums],
        scratch_shapes=[pltpu.SemaphoreType.DMA((n, 3)), pltpu.SemaphoreType.DMA((n, 3))],
    )(*sums)


def pair_exchange_results(halves):
    n = len(halves)

    def body(*refs):
        ins, outs = refs[:n], refs[n:2 * n]
        send_sems, recv_sems = refs[2 * n:]
        x, y, c = _me()
        sends = []
        for w in range(n):
            cp = pltpu.make_async_remote_copy(
                src_ref=ins[w].at[c], dst_ref=outs[w].at[c], send_sem=send_sems.at[w], recv_sem=recv_sems.at[w],
                device_id=(x, y, 1 - c), device_id_type=MESH)
            cp.start()
            sends.append(cp)
        for w in range(n):
            pltpu.make_async_remote_copy(
                src_ref=ins[w].at[c], dst_ref=outs[w].at[1 - c], send_sem=send_sems.at[w],
                recv_sem=recv_sems.at[w], device_id=(x, y, 1 - c), device_id_type=MESH).wait_recv()
        for cp in sends:
            cp.wait_send()

    return pl.pallas_call(
        body, name="pair_exchange_results", in_specs=[ANY] * n, out_specs=[ANY] * n,
        out_shape=[jax.ShapeDtypeStruct(h.shape, f32) for h in halves],
        input_output_aliases={w: w for w in range(n)},
        scratch_shapes=[pltpu.SemaphoreType.DMA((n,)), pltpu.SemaphoreType.DMA((n,))],
    )(*halves)


def add_pair(grad, other, c_idx, name):
    _, r, c = grad.shape
    half = r // 2
    tr = half // 2 if (half // 2) % 8 == 0 else half
    per = half // tr

    def body(c_ref, g_ref, o_ref, out_ref):
        out_ref[...] = (g_ref[...].astype(f32) + o_ref[...].astype(f32)).astype(bf16)

    return pl.pallas_call(
        body, name=name,
        grid_spec=pltpu.PrefetchScalarGridSpec(
            num_scalar_prefetch=1, grid=(N_SHARD, per),
            in_specs=[pl.BlockSpec((1, tr, c), lambda j, t, cr: (j, cr[0] * per + t, 0)),
                      pl.BlockSpec((1, tr, c), lambda j, t, cr: (j, t, 0))],
            out_specs=pl.BlockSpec((1, tr, c), lambda j, t, cr: (j, t, 0))),
        out_shape=jax.ShapeDtypeStruct((N_SHARD, half, c), bf16),
        compiler_params=_cparams(("parallel", "parallel")),
    )(c_idx, grad, other)


def add_four(own, parts, jc_idx, name):
    _, half, c = parts.shape
    tr = half // 2 if (half // 2) % 8 == 0 else half

    def body(jc_ref, own_ref, p_ref, out_ref):
        acc = own_ref[0].astype(f32)
        for k in range(N_SHARD - 1):
            acc = acc + p_ref[k].astype(f32)
        out_ref[0] = acc

    return pl.pallas_call(
        body, name=name,
        grid_spec=pltpu.PrefetchScalarGridSpec(
            num_scalar_prefetch=1, grid=(half // tr,),
            in_specs=[pl.BlockSpec((1, tr, c), lambda t, jc: (jc[0], t, 0)),
                      pl.BlockSpec((N_SHARD - 1, tr, c), lambda t, jc: (0, t, 0))],
            out_specs=pl.BlockSpec((1, tr, c), lambda t, jc: (jc[1], t, 0))),
        out_shape=jax.ShapeDtypeStruct((2, half, c), f32),
        compiler_params=_cparams(("parallel",)),
    )(jc_idx, own, parts)


def sum_devices(gathered):
    def body(g_ref, out_ref):
        acc = g_ref[0]
        for d in range(1, N_DEV):
            acc = acc + g_ref[d]
        out_ref[...] = acc

    return pl.pallas_call(body, name="sum_devices", out_shape=jax.ShapeDtypeStruct(gathered.shape[1:], f32))(gathered)


def _rows128(a, rows):
    flat = a.reshape(-1, BLOCK) if a.size % BLOCK == 0 else jnp.pad(a.reshape(1, -1), ((0, 0), (0, BLOCK - a.size)))
    return jnp.pad(flat, ((0, rows - flat.shape[0]), (0, 0)))


def kernel(x, meta_tokens, ln_emb_g, ln_emb_b, w_in, hg_lower_bounds, hg_norm_g, attn_sinks, w_branch_hg, w_branch_attn, w_out, ln1_g, ln1_b, w_ffn_in, w_ffn_out, ln2_g, ln2_b, loss_target, m_meta_tokens, m_ln_emb_g, m_ln_emb_b, m_w_in, m_hg_lower_bounds, m_hg_norm_g, m_attn_sinks, m_w_branch_hg, m_w_branch_attn, m_w_out, m_ln1_g, m_ln1_b, m_w_ffn_in, m_w_ffn_out, m_ln2_g, m_ln2_b, v_meta_tokens, v_ln_emb_g, v_ln_emb_b, v_w_in, v_hg_lower_bounds, v_hg_norm_g, v_attn_sinks, v_w_branch_hg, v_w_branch_attn, v_w_out, v_ln1_g, v_ln1_b, v_w_ffn_in, v_w_ffn_out, v_ln2_g, v_ln2_b):
    seq = x.shape[1]
    nb = seq // BLOCK + 1
    xs = x[0]
    ts = loss_target[0]
    ix, iy, ic = _me()
    shard = 2 * ix + iy
    vec = lambda a: a.reshape(1, D_MODEL)

    w_in_t = jnp.swapaxes(w_in[0], 0, 1)
    g_in, g_meta, s_bh, s_ba, s_out, s_fi, s_fo = gather_weights(
        [w_in_t, meta_tokens], [w_branch_hg[0], w_branch_attn[0], w_out[0], w_ffn_in[0], w_ffn_out[0]])
    by_cols = lambda g: g.reshape(N_SHARD, -1, g.shape[-1]).transpose(1, 0, 2).reshape(-1, N_SHARD * g.shape[-1])
    wf_in = g_in.reshape(IN_W, D_MODEL)
    metablk = jnp.pad(by_cols(g_meta), ((TM - N_META, 0), (0, 0)))

    pos = jnp.arange((nb + LEAD) * BLOCK, dtype=jnp.int32) - (LEAD * BLOCK + PAD)
    half = HEAD_DIM // 2
    inv = ROPE_THETA ** (-jnp.arange(half, dtype=f32) / half)
    ang = pos.astype(f32)[:, None] * inv[None, :]
    cos = jnp.tile(jnp.cos(ang), (1, BLOCK // half))
    sin = jnp.tile(jnp.sin(ang), (1, BLOCK // half))
    sinks8 = jnp.broadcast_to(attn_sinks.reshape(ATT_HEADS, 1), (ATT_HEADS, BLOCK))
    ng = hg_norm_g.reshape(1, HG_K)

    h0, h0b, pa, pg = emb_inproj(xs, metablk, vec(ln_emb_g), vec(ln_emb_b), wf_in, cos, sin)
    og, sprev, g_fi, g_out = hgrn_fwd(pa, hg_lower_bounds, ng, nb, [s_fi, s_out])
    oatt, g_fo, g_bh, g_ba = attn_fwd(pa, sinks8, nb, [s_fo, s_bh, s_ba])
    wf_bh, wf_ba, wf_fi = by_cols(g_bh), by_cols(g_ba), by_cols(g_fi)
    wf_out = g_out.reshape(D_MODEL, D_MODEL)
    wf_fo = g_fo.reshape(D_FF, D_MODEL)
    dh1, dau, sact, dr2, mixin, h1b, og, oatt, loss_part, dg2, db2 = mid_front_ffn(
        h0, pg, og, oatt, ts, wf_bh, wf_ba, wf_out, wf_fi, wf_fo, ln1_g, ln1_b, ln2_g, ln2_b)
    dh0p, dpg, dog, doa, dyh, dya, dr1, dg1, db1 = mid_back(dh1, h0, pg, og, oatt, wf_bh, wf_ba, wf_out, ln1_g, ln1_b)
    tp = max(t for t in (768, 512, TM) if h0.shape[0] % t == 0)
    pieces = lambda g: g.reshape(N_SHARD, 2, -1, g.shape[-1])
    gb_bh = pieces(wgrad(og, dyh, "wgrad_bh", 512, D_MODEL, tp, True, bf16))
    gb_ba = pieces(wgrad(oatt, dya, "wgrad_ba", 512, D_MODEL, tp, True, bf16))
    gb_out = pieces(wgrad(mixin, dr1, "wgrad_out", D_MODEL, D_MODEL, tp, False, bf16))
    gb_fi = pieces(wgrad(h1b, dau, "wgrad_fi", D_MODEL, D_FF, tp, True, bf16))
    gb_fo = pieces(wgrad(sact, dr2, "wgrad_fo", D_FF // 2, D_MODEL, tp, False, bf16))
    dhq, dhf, dhi, dhg, dlb4, dng, r_fi, r_fo = hgrn_bwd(pa, hg_lower_bounds, ng, sprev, dog, nb, [gb_fi, gb_fo])
    daq, dkc, dkp, dvc, dvp, dkm, dvm, dsk, r_out, r_bh, r_ba = attn_bwd(pa, sinks8, doa, nb,
                                                                         [gb_out, gb_bh, gb_ba])
    dproj, dx, dmeta, dlg, dlb = inproj_bwd(dh0p, dhq, dhf, dhi, dhg, daq, dkc, dkp, dvc, dvp, dkm, dvm, dpg,
                                      wf_in, xs, metablk, vec(ln_emb_g), vec(ln_emb_b), cos, sin)
    gw_in = wgrad(dproj, h0b, "wgrad_in", IN_W // 2, D_MODEL, tp, False, bf16).reshape(N_SHARD, -1, D_MODEL)

    parts = [(dlg, 8), (dlb, 8), (dlb4, 8), (dng, 8), (dsk[:, 0], 8),
             (dg1, 8), (db1, 8), (dg2, 8), (db2, 8), (dmeta, BLOCK), (loss_part, 8)]
    small = jnp.concatenate([_rows128(a, r) for a, r in parts], axis=0)

    c_idx = jnp.reshape(ic, (1,)).astype(jnp.int32)
    jc_idx = jnp.stack([shard, ic]).astype(jnp.int32)
    other_in, gathered = pair_exchange_halves([gw_in], small)
    sum_in = add_pair(gw_in, other_in, c_idx, "add_pair_in")
    quad_in, = chip_exchange([sum_in])
    halves = [add_four(sum_in, quad_in, jc_idx, "add_four_in")]
    halves += [add_eight(g, r, jc_idx, "add_eight_" + nm) for nm, g, r in
               (("bh", gb_bh, r_bh), ("ba", gb_ba, r_ba), ("out", gb_out, r_out), ("fi", gb_fi, r_fi),
                ("fo", gb_fo, r_fo))]
    red = [r.reshape(-1, r.shape[-1]) for r in pair_exchange_results(halves)]
    small_sum = sum_devices(gathered)

    offs, acc = [], 0
    for _, r in parts:
        offs.append(acc)
        acc += r
    take = lambda n, size: small_sum[offs[n]:offs[n] + parts[n][1]].reshape(-1)[:size]
    g_meta_full = take(9, N_META * D_MODEL).reshape(N_META, D_MODEL)
    g_small = {
        "meta_tokens": lax.dynamic_slice_in_dim(g_meta_full, shard * (D_MODEL // N_SHARD), D_MODEL // N_SHARD, axis=1),
        "ln_emb_g": take(0, D_MODEL), "ln_emb_b": take(1, D_MODEL),
        "hg_lower_bounds": take(2, 2 * HG_HEADS * HG_K).reshape(2, HG_HEADS * HG_K),
        "hg_norm_g": take(3, HG_K).reshape(1, HG_K), "attn_sinks": take(4, ATT_HEADS).reshape(1, ATT_HEADS),
        "ln1_g": take(5, D_MODEL).reshape(1, D_MODEL), "ln1_b": take(6, D_MODEL).reshape(1, D_MODEL),
        "ln2_g": take(7, D_MODEL).reshape(1, D_MODEL), "ln2_b": take(8, D_MODEL).reshape(1, D_MODEL),
    }
    g_big = {"w_in": red[0], "w_branch_hg": red[1], "w_branch_attn": red[2], "w_out": red[3],
             "w_ffn_in": red[4], "w_ffn_out": red[5]}

    names = ["meta_tokens", "ln_emb_g", "ln_emb_b", "w_in", "hg_lower_bounds", "hg_norm_g", "attn_sinks",
             "w_branch_hg", "w_branch_attn", "w_out", "ln1_g", "ln1_b", "w_ffn_in", "w_ffn_out", "ln2_g", "ln2_b"]
    given = dict(
        meta_tokens=(meta_tokens, m_meta_tokens, v_meta_tokens), ln_emb_g=(ln_emb_g, m_ln_emb_g, v_ln_emb_g),
        ln_emb_b=(ln_emb_b, m_ln_emb_b, v_ln_emb_b), w_in=(w_in, m_w_in, v_w_in),
        hg_lower_bounds=(hg_lower_bounds, m_hg_lower_bounds, v_hg_lower_bounds),
        hg_norm_g=(hg_norm_g, m_hg_norm_g, v_hg_norm_g), attn_sinks=(attn_sinks, m_attn_sinks, v_attn_sinks),
        w_branch_hg=(w_branch_hg, m_w_branch_hg, v_w_branch_hg),
        w_branch_attn=(w_branch_attn, m_w_branch_attn, v_w_branch_attn), w_out=(w_out, m_w_out, v_w_out),
        ln1_g=(ln1_g, m_ln1_g, v_ln1_g), ln1_b=(ln1_b, m_ln1_b, v_ln1_b), w_ffn_in=(w_ffn_in, m_w_ffn_in, v_w_ffn_in),
        w_ffn_out=(w_ffn_out, m_w_ffn_out, v_w_ffn_out), ln2_g=(ln2_g, m_ln2_g, v_ln2_g), ln2_b=(ln2_b, m_ln2_b, v_ln2_b))
    two_d = lambda a: a.reshape(8, BLOCK) if a.ndim == 1 else a.reshape(a.shape[-2], a.shape[-1])
    small_names = [nm for nm in names if nm not in g_big]
    small_d, small_m, small_v = adamw_small([two_d(given[nm][0]) for nm in small_names],
                                            [two_d(g_small[nm]) for nm in small_names],
                                            [two_d(given[nm][1]) for nm in small_names],
                                            [two_d(given[nm][2]) for nm in small_names])
    out_g, out_d, out_m, out_v = [], [], [], []
    for nm in names:
        w, m, v = given[nm]
        shape = w.shape
        if nm == "w_in":
            t = lambda a: jnp.swapaxes(two_d(a), 0, 1)
            g, d, mn, vn = [t(a) for a in adamw(t(w), g_big[nm], t(m), t(v), "adamw_" + nm)]
        elif nm in g_big:
            g, d, mn, vn = adamw(two_d(w), g_big[nm], two_d(m), two_d(v), "adamw_" + nm)
        else:
            k = small_names.index(nm)
            g, d, mn, vn = g_small[nm], small_d[k], small_m[k], small_v[k]
        out_g.append(g.reshape(shape))
        out_d.append(d.reshape(shape))
        out_m.append(mn.reshape(shape))
        out_v.append(vn.reshape(shape))

    loss = take(10, 1)[0]
    grad_x = dx.reshape(x.shape)
    return (loss, grad_x, *out_g, *out_d, *out_m, *out_v)
```

```python
import functools

import jax
import jax.numpy as jnp
from jax import lax
from jax.experimental import pallas as pl
from jax.experimental.pallas import tpu as pltpu

f32 = jnp.float32
bf16 = jnp.bfloat16

D_MODEL = 1024
BLOCK = 128
N_META = 16
PAD = BLOCK - N_META
HG_HEADS = 4
HG_K = 128
ATT_HEADS = 8
HEAD_DIM = 64
ATT_QW = ATT_HEADS * HEAD_DIM
D_FF = 2816
EPS = 1e-5
ALPHA = 2.0 ** 0.25
ROPE_THETA = 10000.0
N_A = 2816
N_G = 2048
IN_W = N_A + N_G
N_SHARD = 4
N_DEV = 8

ADAM_LR = 0.001
ADAM_B1 = 0.9
ADAM_B2 = 0.999
ADAM_EPS = 1e-08
ADAM_WD = 0.01
ADAM_STEP = 10

TM = 256
LEAD = TM // BLOCK - 1

VMEM_LIMIT = 56 * 1024 * 1024
VMEM_LIMIT_ALL_WEIGHTS = 62 * 1024 * 1024
MESH = pl.DeviceIdType.MESH


def _cparams(sem, vmem=VMEM_LIMIT):
    return pltpu.CompilerParams(dimension_semantics=sem, vmem_limit_bytes=vmem)


def _const_spec(shape):
    zeros = (0,) * len(shape)
    return pl.BlockSpec(shape, lambda *_: zeros, pipeline_mode=pl.Buffered(1))


def _dot(a, b, ca, cb):
    return lax.dot_general(a.astype(bf16), b.astype(bf16), (((ca,), (cb,)), ((), ())),
                           preferred_element_type=f32)


@jax.custom_vjp
def mm(a, b):
    return _dot(a, b, 1, 0)


mm.defvjp(lambda a, b: (_dot(a, b, 1, 0), (a, b)),
          lambda r, g: (_dot(g, r[1], 1, 1), _dot(r[0], g, 0, 0)))


@jax.custom_vjp
def mm_nt(a, b):
    return _dot(a, b, 1, 1)


mm_nt.defvjp(lambda a, b: (_dot(a, b, 1, 1), (a, b)),
             lambda r, g: (_dot(g, r[1], 1, 0), _dot(g, r[0], 0, 0)))


@jax.custom_vjp
def mm_tn(a, b):
    return _dot(a, b, 0, 0)


mm_tn.defvjp(lambda a, b: (_dot(a, b, 0, 0), (a, b)),
             lambda r, g: (_dot(r[1], g, 1, 1), _dot(r[0], g, 1, 0)))


@functools.partial(jax.custom_vjp, nondiff_argnums=(1,))
def roll_lanes(x, shift):
    return pltpu.roll(x, shift, 1)


roll_lanes.defvjp(lambda x, shift: (pltpu.roll(x, shift, 1), None),
                  lambda shift, _, g: (pltpu.roll(g, (128 - shift) % 128, 1),))


@jax.custom_vjp
def _sigmoid(x):
    return 1.0 / (1.0 + jnp.exp(-x))


def _sigmoid_fwd(x):
    s = 1.0 / (1.0 + jnp.exp(-x))
    return s, s


_sigmoid.defvjp(_sigmoid_fwd, lambda s, g: (g * s * (1.0 - s),))


@jax.custom_vjp
def _recip(x):
    return 1.0 / x


def _recip_fwd(x):
    r = 1.0 / x
    return r, r


_recip.defvjp(_recip_fwd, lambda r, g: (-g * r * r,))


def _ln_stats(x):
    mu = jnp.mean(x, axis=-1, keepdims=True)
    xc = x - mu
    var = jnp.mean(xc * xc, axis=-1, keepdims=True)
    rs = lax.rsqrt(var + EPS)
    return xc * rs, rs


def _ln_bwd(dy, xh, rs, g):
    dxh = dy * g
    m1 = jnp.mean(dxh, axis=-1, keepdims=True)
    m2 = jnp.mean(dxh * xh, axis=-1, keepdims=True)
    return rs * (dxh - m1 - xh * m2)


def _row_ids(i):
    return i * BLOCK + lax.broadcasted_iota(jnp.int32, (BLOCK, 1), 0)


def _tm_rows(i):
    return i * TM + lax.broadcasted_iota(jnp.int32, (TM, 1), 0)


def _tm_row(n):
    return pl.BlockSpec((TM, n), lambda i: (i, 0))


def _tm_tokens():
    return pl.BlockSpec((TM, D_MODEL), lambda i: (jnp.maximum(i - 1, 0), 0))


Q_COL, V_COL = 4 * HG_HEADS * HG_K, N_A - BLOCK


def emb_inproj(x, metablk, g, b, w_in, cos, sin):
    nsteps = x.shape[0] // TM + 1

    def body(x_ref, mb_ref, g_ref, b_ref, w_ref, cos_ref, sin_ref, h0_ref, h0b_ref, pa_ref, pg_ref):
        i = pl.program_id(0)
        xb = jnp.where(i == 0, mb_ref[...], x_ref[...])
        xh, _ = _ln_stats(xb)
        y = xh * g_ref[...] + b_ref[...]
        y = jnp.where(_tm_rows(i) >= TM - N_META, y, 0.0)
        h0_ref[...] = y
        yb = y.astype(bf16)
        h0b_ref[...] = yb
        pa = _dot(yb, w_ref[:N_A, :], 1, 1)
        cos, sin = cos_ref[...], sin_ref[...]
        pa_ref[:, :Q_COL] = pa[:, :Q_COL]
        for c0 in range(Q_COL, V_COL, BLOCK):
            pa_ref[:, c0:c0 + BLOCK] = _rope(pa[:, c0:c0 + BLOCK], cos, sin)
        pa_ref[:, V_COL:] = pa[:, V_COL:]
        pg_ref[...] = _dot(yb, w_ref[N_A:, :], 1, 1)

    p = nsteps * TM
    row = _tm_row
    return pl.pallas_call(
        body, name="emb_inproj", grid=(nsteps,),
        in_specs=[_tm_tokens(),
                  _const_spec((TM, D_MODEL)), _const_spec((1, D_MODEL)), _const_spec((1, D_MODEL)),
                  _const_spec((IN_W, D_MODEL)), _tm_row(BLOCK), _tm_row(BLOCK)],
        out_specs=[row(D_MODEL), row(D_MODEL), row(N_A), row(N_G)],
        out_shape=[jax.ShapeDtypeStruct((p, D_MODEL), f32), jax.ShapeDtypeStruct((p, D_MODEL), bf16),
                   jax.ShapeDtypeStruct((p, N_A), f32), jax.ShapeDtypeStruct((p, N_G), f32)],
        compiler_params=_cparams(("parallel",)),
    )(x, metablk, g, b, w_in, cos, sin)


def _hgrn_chunk(valid, st, hq, hf, hi, hg, lbraw, ng):
    lb = _sigmoid(lbraw[0:1] - lbraw[1:2])
    q = hq * _sigmoid(hq)
    fg = lb + (1.0 - lb) * _sigmoid(hf)
    logf = jnp.where(valid, jnp.log(fg), 0.0)
    k = jnp.where(valid, 1.0 - fg, 0.0)
    v = hi
    r = lax.broadcasted_iota(jnp.int32, (BLOCK, BLOCK), 0)
    c = lax.broadcasted_iota(jnp.int32, (BLOCK, BLOCK), 1)
    tril = (c <= r).astype(f32)
    bcum = jnp.dot(tril, logf, precision=lax.Precision.HIGHEST, preferred_element_type=f32)
    blast = bcum[BLOCK - 1:BLOCK]
    rows = lax.broadcasted_iota(jnp.int32, (BLOCK, 1), 0)
    sub8 = lax.broadcasted_iota(jnp.int32, (BLOCK // 8, 8, HG_K), 1)
    b8 = bcum.reshape(BLOCK // 8, 8, HG_K)
    row_of_8 = lambda j: jnp.broadcast_to(b8[:, j:j + 1, :], b8.shape)
    a = jnp.where(r == c, jnp.sum(q * k, axis=-1, keepdims=True), 0.0)
    seg = BLOCK
    while seg >= 2:
        half = seg // 2
        if seg >= 8:
            bs = bcum.reshape(BLOCK // seg, seg, HG_K)
            ref = jnp.broadcast_to(bs[:, half - 1:half, :], bs.shape)
        elif seg == 4:
            ref = jnp.where(sub8 < 4, row_of_8(1), row_of_8(5))
        else:
            ref = jnp.where(sub8 < 2, row_of_8(0), jnp.where(sub8 < 4, row_of_8(2),
                                                             jnp.where(sub8 < 6, row_of_8(4), row_of_8(6))))
        ref = ref.reshape(BLOCK, HG_K)
        upper = (rows % seg) >= half
        q_up = q * jnp.exp(jnp.where(upper, bcum - ref, -jnp.inf))
        k_lo = k * jnp.exp(jnp.where(upper, -jnp.inf, ref - bcum))
        a = a + jnp.where((r // seg) == (c // seg), mm_nt(q_up, k_lo), 0.0)
        seg = half
    o = mm_nt(q * jnp.exp(bcum), st) + mm(a, v)
    st_new = st * jnp.exp(blast) + mm_tn(v, k * jnp.exp(blast - bcum))
    on = o * lax.rsqrt(jnp.mean(o * o, axis=-1, keepdims=True) + EPS) * ng
    return st_new, on * (hg * _sigmoid(hg))


def _hgrn_in_specs(rowmap):
    wide = lambda col: pl.BlockSpec((BLOCK, HG_HEADS * HG_K), lambda i: (rowmap(i) + LEAD, col))
    return [wide(0), wide(1), wide(2), wide(3), _const_spec((2, HG_HEADS * HG_K)), _const_spec((1, HG_K))]


def _head(ref, h):
    return ref[:, h * HG_K:(h + 1) * HG_K]


def hgrn_fwd(pa, lbraw, ng, nb, shards):
    n = len(shards)

    def body(hq_ref, hf_ref, hi_ref, hg_ref, lb_ref, ng_ref, *rest):
        srcs, (og_ref, sp_ref), dsts = rest[:n], rest[n:n + 2], rest[n + 2:2 * n + 2]
        st_ref = rest[2 * n + 2]
        start, wait = _shard_push(srcs, dsts, *rest[2 * n + 3:])
        i = pl.program_id(0)

        @pl.when(i == 0)
        def _():
            st_ref[...] = jnp.zeros_like(st_ref)
            start()

        @pl.when(i == nb - 1)
        def _():
            wait()

        valid = _row_ids(i) >= PAD
        for h in range(HG_HEADS):
            st = st_ref[h]
            sp_ref[0, h] = st
            st_new, out = _hgrn_chunk(valid, st, _head(hq_ref, h), _head(hf_ref, h), _head(hi_ref, h),
                                      _head(hg_ref, h), _head(lb_ref, h), ng_ref[...])
            st_ref[h] = st_new
            og_ref[:, h * HG_K:(h + 1) * HG_K] = out.astype(bf16)

    p = (nb + LEAD) * BLOCK
    push_in, push_out, push_shape, push_scratch = _push_specs(shards)
    return pl.pallas_call(
        body, name="hgrn_fwd", grid=(nb,),
        in_specs=_hgrn_in_specs(lambda i: i) + push_in,
        out_specs=[pl.BlockSpec((BLOCK, HG_HEADS * HG_K), lambda i: (i + LEAD, 0)),
                   pl.BlockSpec((1, HG_HEADS, HG_K, HG_K), lambda i: (i, 0, 0, 0))] + push_out,
        out_shape=[jax.ShapeDtypeStruct((p, HG_HEADS * HG_K), bf16),
                   jax.ShapeDtypeStruct((nb, HG_HEADS, HG_K, HG_K), f32)] + push_shape,
        scratch_shapes=[pltpu.VMEM((HG_HEADS, HG_K, HG_K), f32)] + push_scratch,
        compiler_params=_cparams(("arbitrary",)),
    )(pa, pa, pa, pa, lbraw, ng, *shards)


def hgrn_bwd(pa, lbraw, ng, sprev, dog, nb, grads):
    n = len(grads)

    def body(hq_ref, hf_ref, hi_ref, hg_ref, lb_ref, ng_ref, sp_ref, do_ref, *rest):
        srcs, rest = rest[:n], rest[n:]
        dq_ref, df_ref, di_ref, dg_ref, dlb_ref, dng_ref = rest[:6]
        dsts, dst_ref = rest[6:6 + n], rest[6 + n]
        start, wait = _grad_push(srcs, dsts, *rest[7 + n:])
        i = pl.program_id(0)

        @pl.when(i == 0)
        def _():
            dst_ref[...] = jnp.zeros_like(dst_ref)
            dlb_ref[...] = jnp.zeros_like(dlb_ref)
            dng_ref[...] = jnp.zeros_like(dng_ref)
            start()

        valid = _row_ids(nb - 1 - i) >= PAD
        dng_sum = jnp.zeros((1, HG_K), f32)
        for h in range(HG_HEADS):
            cols = slice(h * HG_K, (h + 1) * HG_K)
            _, vjp = jax.vjp(functools.partial(_hgrn_chunk, valid), sp_ref[0, h], _head(hq_ref, h), _head(hf_ref, h),
                             _head(hi_ref, h), _head(hg_ref, h), _head(lb_ref, h), ng_ref[...])
            dst, dq, df, di, dg, dlb, dng = vjp((dst_ref[h], _head(do_ref, h)))
            dst_ref[h] = dst
            dq_ref[:, cols] = dq.astype(bf16)
            df_ref[:, cols] = df.astype(bf16)
            di_ref[:, cols] = di.astype(bf16)
            dg_ref[:, cols] = dg.astype(bf16)
            dlb_ref[:, cols] += dlb
            dng_sum = dng_sum + dng
        dng_ref[...] += dng_sum
        pl.when(i == nb - 1)(wait)

    p = (nb + LEAD) * BLOCK
    rev = lambda i: nb - 1 - i
    hw = HG_HEADS * HG_K
    blk = pl.BlockSpec((BLOCK, hw), lambda i: (rev(i) + LEAD, 0))
    wide = jax.ShapeDtypeStruct((p, hw), bf16)
    push_in, push_out, push_shape, push_scratch = _grad_push_specs(grads)
    return pl.pallas_call(
        body, name="hgrn_bwd", grid=(nb,),
        in_specs=_hgrn_in_specs(rev) + [pl.BlockSpec((1, HG_HEADS, HG_K, HG_K), lambda i: (rev(i), 0, 0, 0)), blk]
        + push_in,
        out_specs=[blk, blk, blk, blk, pl.BlockSpec((2, hw), lambda i: (0, 0)), pl.BlockSpec((1, HG_K), lambda i: (0, 0))]
        + push_out,
        out_shape=[wide, wide, wide, wide, jax.ShapeDtypeStruct((2, hw), f32), jax.ShapeDtypeStruct((1, HG_K), f32)]
        + push_shape,
        scratch_shapes=[pltpu.VMEM((HG_HEADS, HG_K, HG_K), f32)] + push_scratch,
        compiler_params=_cparams(("arbitrary",)),
    )(pa, pa, pa, pa, lbraw, ng, sprev, dog, *grads)


def _rot_half(x):
    lane = lax.broadcasted_iota(jnp.int32, x.shape, 1)
    return jnp.where(lane % HEAD_DIM < HEAD_DIM // 2, -pltpu.roll(x, BLOCK - HEAD_DIM // 2, 1),
                     pltpu.roll(x, HEAD_DIM // 2, 1))


def _rope(x, cos, sin):
    return x * cos + _rot_half(x) * sin


def _rope_transposed(g, cos, sin):
    return g * cos - _rot_half(g * sin)


def _both_halves(x, g):
    lo = lax.broadcasted_iota(jnp.int32, x.shape, 1) < HEAD_DIM
    sw = roll_lanes(x, HEAD_DIM)
    return jnp.where(lo, x, sw) if g == 0 else jnp.where(lo, sw, x)


def _attn_block(band_ok, meta_ok, q, kp, kc, vp, vc, km, vm, *sinks):
    neg = jnp.finfo(f32).min
    scale = HEAD_DIM ** -0.5
    group = ATT_HEADS // 2
    lo = lax.broadcasted_iota(jnp.int32, (BLOCK, BLOCK), 1) < HEAD_DIM
    t = lax.broadcasted_iota(jnp.int32, (group * BLOCK, BLOCK), 0) % BLOCK
    col = lax.broadcasted_iota(jnp.int32, (group * BLOCK, BLOCK), 1)
    own = col <= t
    is_sink = col == N_META
    qr = [q[:, m * BLOCK:(m + 1) * BLOCK] for m in range(ATT_HEADS // 2)]
    slabs = []
    for g in range(2):
        kp_g, kc_g, vp_g, vc_g, km_g, vm_g = [_both_halves(a, g) for a in (kp, kc, vp, vc, km, vm)]
        qs = jnp.concatenate([jnp.where(lo if h % 2 == 0 else ~lo, qr[2 * g + h // 2], 0.0) for h in range(group)],
                             axis=0)
        sink = jnp.concatenate([jnp.broadcast_to(sinks[group * g + h], (BLOCK, 1)) for h in range(group)], axis=0)
        sb = jnp.where(band_ok, jnp.where(own, mm_nt(qs, kc_g), mm_nt(qs, kp_g)) * scale, neg)
        no_keys = jnp.zeros((BLOCK - N_META, BLOCK), f32)
        sme = jnp.where(meta_ok, mm_nt(qs, jnp.concatenate([km_g, no_keys], axis=0)) * scale,
                        jnp.where(is_sink, sink, neg))
        mx = lax.stop_gradient(jnp.max(jnp.maximum(sb, sme), axis=-1, keepdims=True))
        eb, em = jnp.exp(sb - mx), jnp.exp(sme - mx)
        inv = _recip(jnp.sum(eb + em, axis=-1, keepdims=True))
        pb = eb * inv
        o = (mm(jnp.where(own, pb, 0.0), vc_g) + mm(jnp.where(own, 0.0, pb), vp_g)
             + mm(em * inv, jnp.concatenate([vm_g, no_keys], axis=0)))
        for m in range(2):
            even, odd = o[2 * m * BLOCK:(2 * m + 1) * BLOCK], o[(2 * m + 1) * BLOCK:(2 * m + 2) * BLOCK]
            slabs.append(jnp.where(lo, even, odd))
    return jnp.concatenate(slabs, axis=1)


def _attn_masks(i):
    group = ATT_HEADS // 2
    t = lax.broadcasted_iota(jnp.int32, (group * BLOCK, BLOCK), 0) % BLOCK
    s = lax.broadcasted_iota(jnp.int32, (group * BLOCK, BLOCK), 1)
    kpos = jnp.where(s <= t, i * BLOCK - PAD + s, jnp.where(i > 0, (i - 1) * BLOCK - PAD + s, -1))
    band_ok = kpos >= N_META
    qpos = i * BLOCK - PAD + lax.broadcasted_iota(jnp.int32, (group * BLOCK, 1), 0) % BLOCK
    meta_ok = (s < N_META) & (s <= qpos)
    return band_ok, meta_ok


def _attn_in_specs():
    cur, prev, first = (lambda i: i), (lambda i: jnp.maximum(i - 1, 0)), (lambda i: 0)
    kcol, vcol = V_COL // BLOCK - 1, V_COL // BLOCK
    blk = lambda rowmap, col: pl.BlockSpec((BLOCK, BLOCK), lambda i: (rowmap(i) + LEAD, col))
    return [pl.BlockSpec((BLOCK, ATT_QW), lambda i: (i + LEAD, Q_COL // ATT_QW)),
            blk(prev, kcol), blk(cur, kcol), blk(prev, vcol), blk(cur, vcol), blk(first, kcol), blk(first, vcol),
            _const_spec((ATT_HEADS, BLOCK))]


def _attn_row(n):
    return pl.BlockSpec((BLOCK, n), lambda i: (i + LEAD, 0))


def _attn_operands(q_ref, kp_ref, kc_ref, vp_ref, vc_ref, km_ref, vm_ref, sk_ref):
    args = (q_ref[...], kp_ref[...], kc_ref[...], vp_ref[...], vc_ref[...], km_ref[PAD:, :], vm_ref[PAD:, :])
    sinks = tuple(sk_ref[j:j + 1, 0:1] for j in range(ATT_HEADS))
    return args + sinks


def attn_fwd(pa, sinks8, nb, shards):
    n = len(shards)
    n_in = 8

    def body(*refs):
        srcs, o_ref, dsts = refs[n_in:n_in + n], refs[n_in + n], refs[n_in + n + 1:n_in + 2 * n + 1]
        start, wait = _shard_push(srcs, dsts, *refs[n_in + 2 * n + 1:])
        i = pl.program_id(0)
        pl.when(i == 0)(start)
        band_ok, meta_ok = _attn_masks(i)
        o_ref[...] = _attn_block(band_ok, meta_ok, *_attn_operands(*refs[:n_in])).astype(bf16)
        pl.when(i == nb - 1)(wait)

    push_in, push_out, push_shape, push_scratch = _push_specs(shards)
    return pl.pallas_call(
        body, name="attn_fwd", grid=(nb,), in_specs=_attn_in_specs() + push_in,
        out_specs=[_attn_row(ATT_QW)] + push_out,
        out_shape=[jax.ShapeDtypeStruct(((nb + LEAD) * BLOCK, ATT_QW), bf16)] + push_shape,
        scratch_shapes=push_scratch,
        compiler_params=_cparams(("arbitrary",)),
    )(pa, pa, pa, pa, pa, pa, pa, sinks8, *shards)


def attn_bwd(pa, sinks8, do, nb, grads):
    n = len(grads)

    def body(*refs):
        do_ref, srcs = refs[8], refs[9:9 + n]
        dq_ref, dkc_ref, dkp_ref, dvc_ref, dvp_ref, dkm_ref, dvm_ref, dsk_ref = refs[9 + n:17 + n]
        start, wait = _grad_push(srcs, refs[17 + n:17 + 2 * n], *refs[17 + 2 * n:])
        i = pl.program_id(0)

        @pl.when(i == 0)
        def _():
            dkm_ref[...] = jnp.zeros((N_META, BLOCK), f32)
            dvm_ref[...] = jnp.zeros((N_META, BLOCK), f32)
            dsk_ref[...] = jnp.zeros((ATT_HEADS, BLOCK), f32)
            start()

        band_ok, meta_ok = _attn_masks(i)
        _, vjp = jax.vjp(functools.partial(_attn_block, band_ok, meta_ok), *_attn_operands(*refs[:8]))
        grads = vjp(do_ref[...])
        dq_ref[...] = grads[0]
        dkp_ref[...] = grads[1]
        dkc_ref[...] = grads[2]
        dvp_ref[...] = grads[3]
        dvc_ref[...] = grads[4]
        dkm_ref[...] += grads[5]
        dvm_ref[...] += grads[6]
        for j in range(ATT_HEADS):
            dsk_ref[j:j + 1, :] += jnp.broadcast_to(grads[7 + j], (1, BLOCK))
        pl.when(i == nb - 1)(wait)

    p = (nb + LEAD) * BLOCK
    row = _attn_row(BLOCK)
    const = lambda r: pl.BlockSpec((r, BLOCK), lambda i: (0, 0))
    part = jax.ShapeDtypeStruct((p, BLOCK), f32)
    push_in, push_out, push_shape, push_scratch = _grad_push_specs(grads)
    return pl.pallas_call(
        body, name="attn_bwd", grid=(nb,),
        in_specs=_attn_in_specs() + [_attn_row(ATT_QW)] + push_in,
        out_specs=[_attn_row(ATT_QW), row, row, row, row,
                   const(N_META), const(N_META), const(ATT_HEADS)] + push_out,
        out_shape=[jax.ShapeDtypeStruct((p, ATT_QW), f32), part, part, part, part,
                   jax.ShapeDtypeStruct((N_META, BLOCK), f32), jax.ShapeDtypeStruct((N_META, BLOCK), f32),
                   jax.ShapeDtypeStruct((ATT_HEADS, BLOCK), f32)] + push_shape,
        scratch_shapes=push_scratch,
        compiler_params=_cparams(("arbitrary",)),
    )(pa, pa, pa, pa, pa, pa, pa, sinks8, do, *grads)


def _mid_forward(h0_ref, pg_ref, og, oa, wbh_ref, wba_ref, wo_ref, g1, b1):
    yh = jnp.dot(og, wbh_ref[...], preferred_element_type=f32)
    ya = jnp.dot(oa, wba_ref[...], preferred_element_type=f32)
    gh = _sigmoid(pg_ref[:, :D_MODEL])
    ga = _sigmoid(pg_ref[:, D_MODEL:])
    mixin = (gh * yh + ga * ya).astype(bf16)
    r1 = ALPHA * h0_ref[...] + jnp.dot(mixin, wo_ref[...], preferred_element_type=f32)
    xh1, rs1 = _ln_stats(r1)
    return yh, ya, gh, ga, mixin, xh1, rs1, xh1 * g1 + b1


def _skip_step_without_tokens(rows, n_in):
    def body(*refs):
        i = pl.program_id(0)

        @pl.when(i == 0)
        def _():
            for r in refs[n_in:]:
                r[...] = jnp.zeros_like(r)

        pl.when(i > 0)(lambda: rows(*refs))

    return body


def _mid_weight_specs():
    hw = HG_HEADS * HG_K
    return [_const_spec((hw, D_MODEL)), _const_spec((ATT_QW, D_MODEL)), _const_spec((D_MODEL, D_MODEL)),
            _const_spec((1, D_MODEL)), _const_spec((1, D_MODEL))]


def mid_front_ffn(h0, pg, og, oatt, target, wbh, wba, wout, wfi, wfo, ln1g, ln1b, ln2g, ln2b):
    def rows(h0_ref, pg_ref, og_ref, oa_ref, t_ref, wbh_ref, wba_ref, wo_ref, g1_ref, b1_ref, wfi_ref, wfo_ref,
             g2_ref, b2_ref, dh1_ref, dau_ref, s_ref, dr2_ref, mix_ref, h1b_ref, ogc_ref, oac_ref,
             loss_ref, dg2_ref, db2_ref):
        i = pl.program_id(0)

        @pl.when(i == 0)
        def _():
            for r in (loss_ref, dg2_ref, db2_ref):
                r[...] = jnp.zeros_like(r)

        used = _tm_rows(i) >= LEAD * BLOCK
        og = jnp.where(used, og_ref[...], jnp.zeros_like(og_ref))
        oa = jnp.where(used, oa_ref[...], jnp.zeros_like(oa_ref))
        ogc_ref[...] = og
        oac_ref[...] = oa
        *_, mixin, _, _, h1 = _mid_forward(h0_ref, pg_ref, og, oa, wbh_ref, wba_ref, wo_ref, g1_ref[...], b1_ref[...])
        mix_ref[...] = mixin
        h1b = h1.astype(bf16)
        h1b_ref[...] = h1b
        g2, b2 = g2_ref[...], b2_ref[...]
        au = jnp.dot(h1b, wfi_ref[...], preferred_element_type=f32)
        a, u = au[:, :D_FF], au[:, D_FF:]
        sg = _sigmoid(a)
        sa = a * sg
        s = (sa * u).astype(bf16)
        s_ref[...] = s
        r2 = ALPHA * h1 + jnp.dot(s, wfo_ref[...], preferred_element_type=f32)
        xh2, rs2 = _ln_stats(r2)
        diff = jnp.where(i > 0, xh2 * g2 + b2 - t_ref[...], 0.0)
        loss_ref[...] += jnp.sum(diff * diff) * (0.5 / D_MODEL)
        dy = diff * (1.0 / D_MODEL)
        dg2_ref[...] += jnp.sum(dy * xh2, axis=0, keepdims=True)
        db2_ref[...] += jnp.sum(dy, axis=0, keepdims=True)
        dr2 = _ln_bwd(dy, xh2, rs2, g2)
        dr2b = dr2.astype(bf16)
        dr2_ref[...] = dr2b
        ds = _dot(dr2b, wfo_ref[...], 1, 1)
        da = (ds * u) * (sg * (1.0 + a * (1.0 - sg)))
        du = ds * sa
        dau = jnp.concatenate([da, du], axis=1).astype(bf16)
        dau_ref[...] = dau
        dh1_ref[...] = ALPHA * dr2 + _dot(dau, wfi_ref[...], 1, 1)

    body = _skip_step_without_tokens(rows, 14)
    p = h0.shape[0]
    hw = HG_HEADS * HG_K
    vec = lambda: pl.BlockSpec((1, D_MODEL), lambda i: (0, 0))
    sds = lambda n, dt: jax.ShapeDtypeStruct((p, n), dt)
    return pl.pallas_call(
        body, name="mid_front_ffn", grid=(p // TM,),
        in_specs=[_tm_row(D_MODEL), _tm_row(N_G), _tm_row(hw), _tm_row(ATT_QW), _tm_tokens()] + _mid_weight_specs()
        + [_const_spec((D_MODEL, 2 * D_FF)), _const_spec((D_FF, D_MODEL)), _const_spec((1, D_MODEL)),
           _const_spec((1, D_MODEL))],
        out_specs=[_tm_row(D_MODEL), _tm_row(2 * D_FF), _tm_row(D_FF), _tm_row(D_MODEL), _tm_row(D_MODEL),
                   _tm_row(D_MODEL), _tm_row(hw), _tm_row(ATT_QW), pl.BlockSpec((1, 1), lambda i: (0, 0)), vec(), vec()],
        out_shape=[sds(D_MODEL, f32), sds(2 * D_FF, bf16), sds(D_FF, bf16), sds(D_MODEL, bf16), sds(D_MODEL, bf16),
                   sds(D_MODEL, bf16), sds(hw, bf16), sds(ATT_QW, bf16),
                   jax.ShapeDtypeStruct((1, 1), f32)] + [jax.ShapeDtypeStruct((1, D_MODEL), f32)] * 2,
        compiler_params=_cparams(("arbitrary",), VMEM_LIMIT_ALL_WEIGHTS),
    )(h0, pg, og, oatt, target, wbh, wba, wout, ln1g, ln1b, wfi, wfo, ln2g, ln2b)


def mid_back(dh1, h0, pg, ogc, oac, wbh, wba, wout, ln1g, ln1b):
    def rows(dh1_ref, h0_ref, pg_ref, og_ref, oa_ref, wbh_ref, wba_ref, wo_ref, g1_ref, b1_ref,
             dh0_ref, dpg_ref, dog_ref, doa_ref, dyh_ref, dya_ref, dr1_ref, dg1_ref, db1_ref):
        @pl.when(pl.program_id(0) == 0)
        def _():
            dg1_ref[...] = jnp.zeros_like(dg1_ref)
            db1_ref[...] = jnp.zeros_like(db1_ref)

        g1 = g1_ref[...]
        yh, ya, gh, ga, _, xh1, rs1, _ = _mid_forward(h0_ref, pg_ref, og_ref[...], oa_ref[...], wbh_ref, wba_ref,
                                                      wo_ref, g1, b1_ref[...])
        dh1 = dh1_ref[...]
        dg1_ref[...] += jnp.sum(dh1 * xh1, axis=0, keepdims=True)
        db1_ref[...] += jnp.sum(dh1, axis=0, keepdims=True)
        dr1 = _ln_bwd(dh1, xh1, rs1, g1)
        dr1b = dr1.astype(bf16)
        dr1_ref[...] = dr1b
        dh0_ref[...] = ALPHA * dr1
        dmix = _dot(dr1b, wo_ref[...], 1, 1)
        dyh = (dmix * gh).astype(bf16)
        dya = (dmix * ga).astype(bf16)
        dyh_ref[...] = dyh
        dya_ref[...] = dya
        dpg_ref[:, :D_MODEL] = (dmix * yh * gh * (1.0 - gh)).astype(bf16)
        dpg_ref[:, D_MODEL:] = (dmix * ya * ga * (1.0 - ga)).astype(bf16)
        dog_ref[...] = _dot(dyh, wbh_ref[...], 1, 1)
        doa_ref[...] = _dot(dya, wba_ref[...], 1, 1)

    body = _skip_step_without_tokens(rows, 10)
    p = h0.shape[0]
    hw = HG_HEADS * HG_K
    vec = lambda: pl.BlockSpec((1, D_MODEL), lambda i: (0, 0))
    sds = lambda n, dt: jax.ShapeDtypeStruct((p, n), dt)
    return pl.pallas_call(
        body, name="mid_back", grid=(p // TM,),
        in_specs=[_tm_row(D_MODEL), _tm_row(D_MODEL), _tm_row(N_G), _tm_row(hw), _tm_row(ATT_QW)] + _mid_weight_specs(),
        out_specs=[_tm_row(D_MODEL), _tm_row(N_G), _tm_row(hw), _tm_row(ATT_QW), _tm_row(D_MODEL), _tm_row(D_MODEL),
                   _tm_row(D_MODEL), vec(), vec()],
        out_shape=[sds(D_MODEL, f32), sds(N_G, bf16), sds(hw, f32), sds(ATT_QW, f32), sds(D_MODEL, bf16),
                   sds(D_MODEL, bf16), sds(D_MODEL, bf16)] + [jax.ShapeDtypeStruct((1, D_MODEL), f32)] * 2,
        compiler_params=_cparams(("arbitrary",)),
    )(dh1, h0, pg, ogc, oac, wbh, wba, wout, ln1g, ln1b)


def inproj_bwd(dh0p, dhq, dhf, dhi, dhg, daq, dkc, dkp, dvc, dvp, dkm, dvm, dpg, w_in, x, metablk, g, b, cos, sin):
    p = dh0p.shape[0]
    nbk = p // BLOCK
    per = TM // BLOCK

    def body(dh0_ref, dq_ref, df_ref, di_ref, dg_ref, daq_ref, dkc_ref, *rest):
        dkp_refs, dvc_ref, dvp_refs = rest[:per], rest[per], rest[per + 1:2 * per + 1]
        (dkm_ref, dvm_ref, dpg_ref, w_ref, x_ref, mb_ref, g_ref, b_ref, cos_ref, sin_ref,
         dproj_ref, dx_ref, dmeta_ref, dlg_ref, dlb_ref) = rest[2 * per + 1:]
        i = pl.program_id(0)

        @pl.when(i == 0)
        def _():
            dlg_ref[...] = jnp.zeros_like(dlg_ref)
            dlb_ref[...] = jnp.zeros_like(dlb_ref)

        zero_pad = jnp.zeros((TM - N_META, BLOCK), f32)
        first = i == 0
        rows = _tm_rows(i)

        def keys(cur_ref, next_refs, meta_ref):
            nxt = jnp.concatenate([jnp.where(per * i + 1 + m < nbk, next_refs[m][...], 0.0) for m in range(per)],
                                  axis=0)
            t = cur_ref[...] + nxt
            return t + jnp.where(first, jnp.concatenate([zero_pad, meta_ref[...]], axis=0), 0.0)

        cos, sin = cos_ref[...], sin_ref[...]
        unrotate = lambda t: _rope_transposed(t, cos, sin).astype(bf16)
        dproj = jnp.concatenate(
            [dq_ref[...], df_ref[...], di_ref[...], dg_ref[...]]
            + [unrotate(daq_ref[:, m * BLOCK:(m + 1) * BLOCK]) for m in range(ATT_QW // BLOCK)]
            + [unrotate(keys(dkc_ref, dkp_refs, dkm_ref)), keys(dvc_ref, dvp_refs, dvm_ref).astype(bf16),
               dpg_ref[...]], axis=1)
        dproj = jnp.where(rows >= LEAD * BLOCK, dproj, jnp.zeros_like(dproj))
        dproj_ref[...] = dproj
        valid = rows >= TM - N_META
        dh0 = jnp.where(valid, dh0_ref[...] + _dot(dproj, w_ref[...], 1, 0), 0.0)
        xb = jnp.where(first, mb_ref[...], x_ref[...])
        xh, rs = _ln_stats(xb)
        dlg_ref[...] += jnp.sum(dh0 * xh, axis=0, keepdims=True)
        dlb_ref[...] += jnp.sum(dh0, axis=0, keepdims=True)
        dx = jnp.where(valid, _ln_bwd(dh0, xh, rs, g_ref[...]), 0.0)
        dx_ref[...] = dx

        @pl.when(first)
        def _():
            dmeta_ref[...] = dx[TM - N_META:, :]

    row = _tm_row
    nxt = [pl.BlockSpec((BLOCK, BLOCK), functools.partial(lambda i, m: (jnp.minimum(per * i + 1 + m, nbk - 1), 0), m=m))
           for m in range(per)]
    hw = HG_HEADS * HG_K
    vec = lambda: pl.BlockSpec((1, D_MODEL), lambda i: (0, 0))
    return pl.pallas_call(
        body, name="inproj_bwd", grid=(p // TM,),
        in_specs=[row(D_MODEL), row(hw), row(hw), row(hw), row(hw), row(ATT_QW),
                  row(BLOCK)] + nxt + [row(BLOCK)] + nxt + [_const_spec((N_META, BLOCK)), _const_spec((N_META, BLOCK)),
                  row(N_G), _const_spec((IN_W, D_MODEL)), _tm_tokens(),
                  _const_spec((TM, D_MODEL)), _const_spec((1, D_MODEL)), _const_spec((1, D_MODEL)),
                  row(BLOCK), row(BLOCK)],
        out_specs=[row(IN_W), _tm_tokens(), pl.BlockSpec((N_META, D_MODEL), lambda i: (0, 0)), vec(), vec()],
        out_shape=[jax.ShapeDtypeStruct((p, IN_W), bf16), jax.ShapeDtypeStruct((p - TM, D_MODEL), f32),
                   jax.ShapeDtypeStruct((N_META, D_MODEL), f32),
                   jax.ShapeDtypeStruct((1, D_MODEL), f32), jax.ShapeDtypeStruct((1, D_MODEL), f32)],
        compiler_params=_cparams(("arbitrary",)),
    )(dh0p, dhq, dhf, dhi, dhg, daq, dkc, *([dkp] * per), dvc, *([dvp] * per), dkm, dvm, dpg, w_in, x, metablk, g, b,
      cos, sin)


def wgrad(a, b, name, tk, tn, tp, by_cols, out_dtype=f32):
    p, k = a.shape
    n = b.shape[1]
    nsteps = p // tp

    def body(a_ref, b_ref, o_ref, acc_ref):
        ip = pl.program_id(2)

        @pl.when(ip == 0)
        def _():
            acc_ref[...] = jnp.zeros_like(acc_ref)

        acc_ref[...] += _dot(a_ref[...], b_ref[...], 0, 0)

        @pl.when(ip == nsteps - 1)
        def _():
            for j in range(span):
                o_ref[j] = acc_ref[:, j * width:(j + 1) * width].astype(out_dtype)

    span, width = 1, tn
    if by_cols:
        shard_n = n // N_SHARD
        out_shape = (N_SHARD, k, shard_n)
        if tn >= shard_n:
            span, width = tn // shard_n, shard_n
            omap = lambda ik, jn, ip: (jn, ik, 0)
        else:
            per = shard_n // tn
            omap = lambda ik, jn, ip: (jn // per, ik, jn % per)
    else:
        out_shape = (1, k, n)
        omap = lambda ik, jn, ip: (0, ik, jn)
    return pl.pallas_call(
        body, name=name, grid=(k // tk, n // tn, nsteps),
        in_specs=[pl.BlockSpec((tp, tk), lambda ik, jn, ip: (ip, ik)),
                  pl.BlockSpec((tp, tn), lambda ik, jn, ip: (ip, jn))],
        out_specs=pl.BlockSpec((span, tk, width), omap),
        out_shape=jax.ShapeDtypeStruct(out_shape, out_dtype),
        scratch_shapes=[pltpu.VMEM((tk, tn), f32)],
        compiler_params=_cparams(("parallel", "parallel", "arbitrary")),
    )(a, b)


def _adamw_math(w, g, m, v):
    mn = ADAM_B1 * m + (1.0 - ADAM_B1) * g
    vn = ADAM_B2 * v + (1.0 - ADAM_B2) * (g * g)
    m_hat = mn / (1.0 - ADAM_B1 ** ADAM_STEP)
    v_hat = vn / (1.0 - ADAM_B2 ** ADAM_STEP)
    return -ADAM_LR * (m_hat / (jnp.sqrt(v_hat) + ADAM_EPS) + ADAM_WD * w), mn, vn


def adamw(w, g, m, v, name):
    r, c = w.shape
    tr = r
    for cand in (256, 176, 152, 128):
        if r > cand and r % cand == 0:
            tr = cand
            break

    def body(w_ref, g_ref, m_ref, v_ref, go_ref, d_ref, mo_ref, vo_ref):
        gg = g_ref[...]
        go_ref[...] = gg
        d_ref[...], mo_ref[...], vo_ref[...] = _adamw_math(w_ref[...], gg, m_ref[...], v_ref[...])

    spec = pl.BlockSpec((tr, c), lambda i: (i, 0))
    sds = jax.ShapeDtypeStruct((r, c), f32)
    return pl.pallas_call(
        body, name=name, grid=(r // tr,), in_specs=[spec] * 4, out_specs=[spec] * 4, out_shape=[sds] * 4,
        compiler_params=_cparams(("parallel",)),
    )(w, g, m, v)


def adamw_small(ws, gs, ms, vs):
    n = len(ws)

    def body(*refs):
        ins, outs = refs[:4 * n], refs[4 * n:]
        for k in range(n):
            outs[k][...], outs[n + k][...], outs[2 * n + k][...] = _adamw_math(
                ins[k][...], ins[n + k][...], ins[2 * n + k][...], ins[3 * n + k][...])

    out = pl.pallas_call(body, name="adamw_small",
                         out_shape=[jax.ShapeDtypeStruct(w.shape, f32) for w in ws] * 3)(*ws, *gs, *ms, *vs)
    return out[:n], out[n:2 * n], out[2 * n:]


def _me():
    return lax.axis_index("x"), lax.axis_index("y"), lax.axis_index("c")


def _chip_peer(x, y, c, k):
    return (x ^ (k >> 1), y ^ (k & 1), c)


ANY = pl.BlockSpec(memory_space=pl.ANY)


def gather_weights(now, later):
    n, n_later = len(now), len(later)
    out_dtypes = [bf16 if s.size > 16 * 256 else f32 for s in now]
    halves = [(2, s.shape[0] // 2, s.shape[1]) for s in now]

    def body(*refs):
        ins, later_ins = refs[:n], refs[n:n + n_later]
        outs, later_outs = refs[n + n_later:2 * n + n_later], refs[2 * n + n_later:2 * (n + n_later)]
        stage = refs[2 * (n + n_later):3 * n + 2 * n_later]
        send_sems, recv_sems, pass_send_sems, pass_recv_sems, local_sems = refs[3 * n + 2 * n_later:]
        x, y, c = _me()
        j = 2 * x + y
        sibling = (x, y, 1 - c)

        def over_ici(w, k, slot):
            return pltpu.make_async_remote_copy(
                src_ref=stage[w].at[c], dst_ref=outs[w].at[slot, c], send_sem=send_sems.at[w, k - 1],
                recv_sem=recv_sems.at[w, k - 1], device_id=_chip_peer(x, y, c, k), device_id_type=MESH)

        def passed_on(w, k, half):
            return pltpu.make_async_remote_copy(
                src_ref=outs[w].at[j ^ k, half], dst_ref=outs[w].at[j ^ k, half], send_sem=pass_send_sems.at[w, k - 1],
                recv_sem=pass_recv_sems.at[w, k - 1], device_id=sibling, device_id_type=MESH)

        for w in range(n):
            stage[w][...] = ins[w][...].astype(out_dtypes[w]).reshape(halves[w])
        locs = []
        for w in range(n):
            loc = pltpu.make_async_copy(stage[w], outs[w].at[j], local_sems.at[w])
            loc.start()
            locs.append(loc)
            for k in (1, 2, 3):
                over_ici(w, k, j).start()
        for w in range(n_later):
            later_outs[w][...] = later_ins[w][...].astype(bf16)
        for w in range(n):
            for k in (1, 2, 3):
                over_ici(w, k, j ^ k).wait_recv()
                passed_on(w, k, c).start()
        for w in range(n):
            for k in (1, 2, 3):
                passed_on(w, k, 1 - c).wait_recv()
        for w in range(n):
            for k in (1, 2, 3):
                over_ici(w, k, j).wait_send()
                passed_on(w, k, c).wait_send()
        for loc in locs:
            loc.wait()

    vmem = pl.BlockSpec(memory_space=pltpu.VMEM)
    sem3 = pltpu.SemaphoreType.DMA((n, 3))
    return pl.pallas_call(
        body, name="gather_weights",
        in_specs=[vmem] * (n + n_later), out_specs=[ANY] * n + [vmem] * n_later,
        out_shape=[jax.ShapeDtypeStruct((N_SHARD,) + h, dt) for h, dt in zip(halves, out_dtypes)]
        + [jax.ShapeDtypeStruct(s.shape, bf16) for s in later],
        scratch_shapes=[pltpu.VMEM(h, dt) for h, dt in zip(halves, out_dtypes)]
        + [sem3, sem3, sem3, sem3, pltpu.SemaphoreType.DMA((n,))],
        compiler_params=pltpu.CompilerParams(vmem_limit_bytes=VMEM_LIMIT),
    )(*now, *later)


def _shard_push(srcs, dsts, send_sems, recv_sems, local_sems):
    def remote(w, k, slot):
        x, y, c = _me()
        return pltpu.make_async_remote_copy(
            src_ref=srcs[w], dst_ref=dsts[w].at[slot], send_sem=send_sems.at[w, k - 1],
            recv_sem=recv_sems.at[w, k - 1], device_id=_chip_peer(x, y, c, k), device_id_type=MESH)

    def local(w):
        x, y, _ = _me()
        return pltpu.make_async_copy(srcs[w], dsts[w].at[2 * x + y], local_sems.at[w])

    def start():
        x, y, _ = _me()
        for w in range(len(srcs)):
            local(w).start()
            for k in (1, 2, 3):
                remote(w, k, 2 * x + y).start()

    def wait():
        x, y, _ = _me()
        for w in range(len(srcs)):
            for k in (1, 2, 3):
                remote(w, k, (2 * x + y) ^ k).wait_recv()
        for w in range(len(srcs)):
            for k in (1, 2, 3):
                remote(w, k, 2 * x + y).wait_send()
            local(w).wait()

    return start, wait


def _grad_push(srcs, dsts, send_sems, recv_sems):
    def copy(w, k):
        x, y, c = _me()
        px, py, pc = x ^ (k >> 2), y ^ ((k >> 1) & 1), c ^ (k & 1)
        return pltpu.make_async_remote_copy(
            src_ref=srcs[w].at[2 * px + py, pc], dst_ref=dsts[w].at[k - 1], send_sem=send_sems.at[w, k - 1],
            recv_sem=recv_sems.at[w, k - 1], device_id=(px, py, pc), device_id_type=MESH)

    def start():
        for w in range(len(srcs)):
            for k in range(1, N_DEV):
                copy(w, k).start()

    def wait():
        for w in range(len(srcs)):
            for k in range(1, N_DEV):
                copy(w, k).wait_recv()
        for w in range(len(srcs)):
            for k in range(1, N_DEV):
                copy(w, k).wait_send()

    return start, wait


def _grad_push_specs(grads):
    n = len(grads)
    return ([ANY] * n, [ANY] * n, [jax.ShapeDtypeStruct((N_DEV - 1,) + g.shape[2:], g.dtype) for g in grads],
            [pltpu.SemaphoreType.DMA((n, N_DEV - 1)), pltpu.SemaphoreType.DMA((n, N_DEV - 1))])


def add_eight(own, parts, jc_idx, name):
    _, half, c = parts.shape
    tr = half // 2 if (half // 2) % 16 == 0 else half

    def body(jc_ref, own_ref, p_ref, out_ref):
        acc = own_ref[0, 0].astype(f32)
        for k in range(N_DEV - 1):
            acc = acc + p_ref[k].astype(f32)
        out_ref[0] = acc

    return pl.pallas_call(
        body, name=name,
        grid_spec=pltpu.PrefetchScalarGridSpec(
            num_scalar_prefetch=1, grid=(half // tr,),
            in_specs=[pl.BlockSpec((1, 1, tr, c), lambda t, jc: (jc[0], jc[1], t, 0)),
                      pl.BlockSpec((N_DEV - 1, tr, c), lambda t, jc: (0, t, 0))],
            out_specs=pl.BlockSpec((1, tr, c), lambda t, jc: (jc[1], t, 0))),
        out_shape=jax.ShapeDtypeStruct((2, half, c), f32),
        compiler_params=_cparams(("parallel",)),
    )(jc_idx, own, parts)


def _push_specs(shards):
    n = len(shards)
    return ([ANY] * n, [ANY] * n, [jax.ShapeDtypeStruct((N_SHARD,) + s.shape, s.dtype) for s in shards],
            [pltpu.SemaphoreType.DMA((n, 3)), pltpu.SemaphoreType.DMA((n, 3)), pltpu.SemaphoreType.DMA((n,))])


def pair_exchange_halves(grads, small):
    n = len(grads)

    def body(*refs):
        ins, small_ref = refs[:n], refs[n]
        outs, gath = refs[n + 1:2 * n + 1], refs[2 * n + 1]
        send_sems, recv_sems, s_send, s_recv, local_sem = refs[2 * n + 2:]
        x, y, c = _me()
        me = 4 * x + 2 * y + c
        sends = []
        for w in range(n):
            half = ins[w].shape[1] // 2
            cp = pltpu.make_async_remote_copy(
                src_ref=ins[w].at[:, pl.ds((1 - c) * half, half), :], dst_ref=outs[w],
                send_sem=send_sems.at[w], recv_sem=recv_sems.at[w], device_id=(x, y, 1 - c), device_id_type=MESH)
            cp.start()
            sends.append(cp)
        loc = pltpu.make_async_copy(small_ref, gath.at[me], local_sem)
        loc.start()
        for k in range(1, N_DEV):
            cp = pltpu.make_async_remote_copy(
                src_ref=small_ref, dst_ref=gath.at[me], send_sem=s_send.at[k - 1], recv_sem=s_recv.at[k - 1],
                device_id=(x ^ (k >> 2), y ^ ((k >> 1) & 1), c ^ (k & 1)), device_id_type=MESH)
            cp.start()
            sends.append(cp)
        for w in range(n):
            half = ins[w].shape[1] // 2
            pltpu.make_async_remote_copy(
                src_ref=ins[w].at[:, pl.ds(0, half), :], dst_ref=outs[w], send_sem=send_sems.at[w],
                recv_sem=recv_sems.at[w], device_id=(x, y, 1 - c), device_id_type=MESH).wait_recv()
        for k in range(1, N_DEV):
            pltpu.make_async_remote_copy(
                src_ref=small_ref, dst_ref=gath.at[me ^ k], send_sem=s_send.at[k - 1], recv_sem=s_recv.at[k - 1],
                device_id=(x ^ (k >> 2), y ^ ((k >> 1) & 1), c ^ (k & 1)), device_id_type=MESH).wait_recv()
        for cp in sends:
            cp.wait_send()
        loc.wait()

    return pl.pallas_call(
        body, name="pair_exchange_halves", in_specs=[ANY] * (n + 1), out_specs=[ANY] * (n + 1),
        out_shape=[jax.ShapeDtypeStruct((g.shape[0], g.shape[1] // 2, g.shape[2]), g.dtype) for g in grads]
        + [jax.ShapeDtypeStruct((N_DEV,) + small.shape, f32)],
        scratch_shapes=[pltpu.SemaphoreType.DMA((n,)), pltpu.SemaphoreType.DMA((n,)),
                        pltpu.SemaphoreType.DMA((N_DEV - 1,)), pltpu.SemaphoreType.DMA((N_DEV - 1,)),
                        pltpu.SemaphoreType.DMA],
    )(*grads, small)


def chip_exchange(sums):
    n = len(sums)

    def body(*refs):
        ins, outs = refs[:n], refs[n:2 * n]
        send_sems, recv_sems = refs[2 * n:]
        x, y, c = _me()
        j = 2 * x + y
        sends = []
        for w in range(n):
            for k in (1, 2, 3):
                cp = pltpu.make_async_remote_copy(
                    src_ref=ins[w].at[j ^ k], dst_ref=outs[w].at[k - 1], send_sem=send_sems.at[w, k - 1],
                    recv_sem=recv_sems.at[w, k - 1], device_id=_chip_peer(x, y, c, k), device_id_type=MESH)
                cp.start()
                sends.append(cp)
        for w in range(n):
            for k in (1, 2, 3):
                pltpu.make_async_remote_copy(
                    src_ref=ins[w].at[0], dst_ref=outs[w].at[k - 1], send_sem=send_sems.at[w, k - 1],
                    recv_sem=recv_sems.at[w, k - 1], device_id=_chip_peer(x, y, c, k), device_id_type=MESH).wait_recv()
        for cp in sends:
            cp.wait_send()

    return pl.pallas_call(
        body, name="chip_exchange", in_specs=[ANY] * n, out_specs=[ANY] * n,
        out_shape=[jax.ShapeDtypeStruct((N_SHARD - 1,) + s.shape[1:], s.dtype) for s in sums],
        scratch_shapes=[pltpu.SemaphoreType.DMA((n, 3)), pltpu.SemaphoreType.DMA((n, 3))],
    )(*sums)


def pair_exchange_results(halves):
    n = len(halves)

    def body(*refs):
        ins, outs = refs[:n], refs[n:2 * n]
        send_sems, recv_sems = refs[2 * n:]
        x, y, c = _me()
        sends = []
        for w in range(n):
            cp = pltpu.make_async_remote_copy(
                src_ref=ins[w].at[c], dst_ref=outs[w].at[c], send_sem=send_sems.at[w], recv_sem=recv_sems.at[w],
                device_id=(x, y, 1 - c), device_id_type=MESH)
            cp.start()
            sends.append(cp)
        for w in range(n):
            pltpu.make_async_remote_copy(
                src_ref=ins[w].at[c], dst_ref=outs[w].at[1 - c], send_sem=send_sems.at[w],
                recv_sem=recv_sems.at[w], device_id=(x, y, 1 - c), device_id_type=MESH).wait_recv()
        for cp in sends:
            cp.wait_send()

    return pl.pallas_call(
        body, name="pair_exchange_results", in_specs=[ANY] * n, out_specs=[ANY] * n,
        out_shape=[jax.ShapeDtypeStruct(h.shape, f32) for h in halves],
        input_output_aliases={w: w for w in range(n)},
        scratch_shapes=[pltpu.SemaphoreType.DMA((n,)), pltpu.SemaphoreType.DMA((n,))],
    )(*halves)


def add_pair(grad, other, c_idx, name):
    _, r, c = grad.shape
    half = r // 2
    tr = half // 2 if (half // 2) % 8 == 0 else half
    per = half // tr

    def body(c_ref, g_ref, o_ref, out_ref):
        out_ref[...] = (g_ref[...].astype(f32) + o_ref[...].astype(f32)).astype(bf16)

    return pl.pallas_call(
        body, name=name,
        grid_spec=pltpu.PrefetchScalarGridSpec(
            num_scalar_prefetch=1, grid=(N_SHARD, per),
            in_specs=[pl.BlockSpec((1, tr, c), lambda j, t, cr: (j, cr[0] * per + t, 0)),
                      pl.BlockSpec((1, tr, c), lambda j, t, cr: (j, t, 0))],
            out_specs=pl.BlockSpec((1, tr, c), lambda j, t, cr: (j, t, 0))),
        out_shape=jax.ShapeDtypeStruct((N_SHARD, half, c), bf16),
        compiler_params=_cparams(("parallel", "parallel")),
    )(c_idx, grad, other)


def add_four(own, parts, jc_idx, name):
    _, half, c = parts.shape
    tr = half // 2 if (half // 2) % 8 == 0 else half

    def body(jc_ref, own_ref, p_ref, out_ref):
        acc = own_ref[0].astype(f32)
        for k in range(N_SHARD - 1):
            acc = acc + p_ref[k].astype(f32)
        out_ref[0] = acc

    return pl.pallas_call(
        body, name=name,
        grid_spec=pltpu.PrefetchScalarGridSpec(
            num_scalar_prefetch=1, grid=(half // tr,),
            in_specs=[pl.BlockSpec((1, tr, c), lambda t, jc: (jc[0], t, 0)),
                      pl.BlockSpec((N_SHARD - 1, tr, c), lambda t, jc: (0, t, 0))],
            out_specs=pl.BlockSpec((1, tr, c), lambda t, jc: (jc[1], t, 0))),
        out_shape=jax.ShapeDtypeStruct((2, half, c), f32),
        compiler_params=_cparams(("parallel",)),
    )(jc_idx, own, parts)


def sum_devices(gathered):
    def body(g_ref, out_ref):
        acc = g_ref[0]
        for d in range(1, N_DEV):
            acc = acc + g_ref[d]
        out_ref[...] = acc

    return pl.pallas_call(body, name="sum_devices", out_shape=jax.ShapeDtypeStruct(gathered.shape[1:], f32))(gathered)


def _rows128(a, rows):
    flat = a.reshape(-1, BLOCK) if a.size % BLOCK == 0 else jnp.pad(a.reshape(1, -1), ((0, 0), (0, BLOCK - a.size)))
    return jnp.pad(flat, ((0, rows - flat.shape[0]), (0, 0)))


def kernel(x, meta_tokens, ln_emb_g, ln_emb_b, w_in, hg_lower_bounds, hg_norm_g, attn_sinks, w_branch_hg, w_branch_attn, w_out, ln1_g, ln1_b, w_ffn_in, w_ffn_out, ln2_g, ln2_b, loss_target, m_meta_tokens, m_ln_emb_g, m_ln_emb_b, m_w_in, m_hg_lower_bounds, m_hg_norm_g, m_attn_sinks, m_w_branch_hg, m_w_branch_attn, m_w_out, m_ln1_g, m_ln1_b, m_w_ffn_in, m_w_ffn_out, m_ln2_g, m_ln2_b, v_meta_tokens, v_ln_emb_g, v_ln_emb_b, v_w_in, v_hg_lower_bounds, v_hg_norm_g, v_attn_sinks, v_w_branch_hg, v_w_branch_attn, v_w_out, v_ln1_g, v_ln1_b, v_w_ffn_in, v_w_ffn_out, v_ln2_g, v_ln2_b):
    seq = x.shape[1]
    nb = seq // BLOCK + 1
    xs = x[0]
    ts = loss_target[0]
    ix, iy, ic = _me()
    shard = 2 * ix + iy
    vec = lambda a: a.reshape(1, D_MODEL)

    w_in_t = jnp.swapaxes(w_in[0], 0, 1)
    g_in, g_meta, s_bh, s_ba, s_out, s_fi, s_fo = gather_weights(
        [w_in_t, meta_tokens], [w_branch_hg[0], w_branch_attn[0], w_out[0], w_ffn_in[0], w_ffn_out[0]])
    by_cols = lambda g: g.reshape(N_SHARD, -1, g.shape[-1]).transpose(1, 0, 2).reshape(-1, N_SHARD * g.shape[-1])
    wf_in = g_in.reshape(IN_W, D_MODEL)
    metablk = jnp.pad(by_cols(g_meta), ((TM - N_META, 0), (0, 0)))

    pos = jnp.arange((nb + LEAD) * BLOCK, dtype=jnp.int32) - (LEAD * BLOCK + PAD)
    half = HEAD_DIM // 2
    inv = ROPE_THETA ** (-jnp.arange(half, dtype=f32) / half)
    ang = pos.astype(f32)[:, None] * inv[None, :]
    cos = jnp.tile(jnp.cos(ang), (1, BLOCK // half))
    sin = jnp.tile(jnp.sin(ang), (1, BLOCK // half))
    sinks8 = jnp.broadcast_to(attn_sinks.reshape(ATT_HEADS, 1), (ATT_HEADS, BLOCK))
    ng = hg_norm_g.reshape(1, HG_K)

    h0, h0b, pa, pg = emb_inproj(xs, metablk, vec(ln_emb_g), vec(ln_emb_b), wf_in, cos, sin)
    og, sprev, g_fi, g_out = hgrn_fwd(pa, hg_lower_bounds, ng, nb, [s_fi, s_out])
    oatt, g_fo, g_bh, g_ba = attn_fwd(pa, sinks8, nb, [s_fo, s_bh, s_ba])
    wf_bh, wf_ba, wf_fi = by_cols(g_bh), by_cols(g_ba), by_cols(g_fi)
    wf_out = g_out.reshape(D_MODEL, D_MODEL)
    wf_fo = g_fo.reshape(D_FF, D_MODEL)
    dh1, dau, sact, dr2, mixin, h1b, og, oatt, loss_part, dg2, db2 = mid_front_ffn(
        h0, pg, og, oatt, ts, wf_bh, wf_ba, wf_out, wf_fi, wf_fo, ln1_g, ln1_b, ln2_g, ln2_b)
    dh0p, dpg, dog, doa, dyh, dya, dr1, dg1, db1 = mid_back(dh1, h0, pg, og, oatt, wf_bh, wf_ba, wf_out, ln1_g, ln1_b)
    tp = max(t for t in (768, 512, TM) if h0.shape[0] % t == 0)
    pieces = lambda g: g.reshape(N_SHARD, 2, -1, g.shape[-1])
    gb_bh = pieces(wgrad(og, dyh, "wgrad_bh", 512, D_MODEL, tp, True, bf16))
    gb_ba = pieces(wgrad(oatt, dya, "wgrad_ba", 512, D_MODEL, tp, True, bf16))
    gb_out = pieces(wgrad(mixin, dr1, "wgrad_out", D_MODEL, D_MODEL, tp, False, bf16))
    gb_fi = pieces(wgrad(h1b, dau, "wgrad_fi", D_MODEL, D_FF, tp, True, bf16))
    gb_fo = pieces(wgrad(sact, dr2, "wgrad_fo", D_FF // 2, D_MODEL, tp, False, bf16))
    dhq, dhf, dhi, dhg, dlb4, dng, r_fi, r_fo = hgrn_bwd(pa, hg_lower_bounds, ng, sprev, dog, nb, [gb_fi, gb_fo])
    daq, dkc, dkp, dvc, dvp, dkm, dvm, dsk, r_out, r_bh, r_ba = attn_bwd(pa, sinks8, doa, nb,
                                                                         [gb_out, gb_bh, gb_ba])
    dproj, dx, dmeta, dlg, dlb = inproj_bwd(dh0p, dhq, dhf, dhi, dhg, daq, dkc, dkp, dvc, dvp, dkm, dvm, dpg,
                                      wf_in, xs, metablk, vec(ln_emb_g), vec(ln_emb_b), cos, sin)
    gw_in = wgrad(dproj, h0b, "wgrad_in", IN_W // 2, D_MODEL, tp, False, bf16).reshape(N_SHARD, -1, D_MODEL)

    parts = [(dlg, 8), (dlb, 8), (dlb4, 8), (dng, 8), (dsk[:, 0], 8),
             (dg1, 8), (db1, 8), (dg2, 8), (db2, 8), (dmeta, BLOCK), (loss_part, 8)]
    small = jnp.concatenate([_rows128(a, r) for a, r in parts], axis=0)

    c_idx = jnp.reshape(ic, (1,)).astype(jnp.int32)
    jc_idx = jnp.stack([shard, ic]).astype(jnp.int32)
    other_in, gathered = pair_exchange_halves([gw_in], small)
    sum_in = add_pair(gw_in, other_in, c_idx, "add_pair_in")
    quad_in, = chip_exchange([sum_in])
    halves = [add_four(sum_in, quad_in, jc_idx, "add_four_in")]
    halves += [add_eight(g, r, jc_idx, "add_eight_" + nm) for nm, g, r in
               (("bh", gb_bh, r_bh), ("ba", gb_ba, r_ba), ("out", gb_out, r_out), ("fi", gb_fi, r_fi),
                ("fo", gb_fo, r_fo))]
    red = [r.reshape(-1, r.shape[-1]) for r in pair_exchange_results(halves)]
    small_sum = sum_devices(gathered)

    offs, acc = [], 0
    for _, r in parts:
        offs.append(acc)
        acc += r
    take = lambda n, size: small_sum[offs[n]:offs[n] + parts[n][1]].reshape(-1)[:size]
    g_meta_full = take(9, N_META * D_MODEL).reshape(N_META, D_MODEL)
    g_small = {
        "meta_tokens": lax.dynamic_slice_in_dim(g_meta_full, shard * (D_MODEL // N_SHARD), D_MODEL // N_SHARD, axis=1),
        "ln_emb_g": take(0, D_MODEL), "ln_emb_b": take(1, D_MODEL),
        "hg_lower_bounds": take(2, 2 * HG_HEADS * HG_K).reshape(2, HG_HEADS * HG_K),
        "hg_norm_g": take(3, HG_K).reshape(1, HG_K), "attn_sinks": take(4, ATT_HEADS).reshape(1, ATT_HEADS),
        "ln1_g": take(5, D_MODEL).reshape(1, D_MODEL), "ln1_b": take(6, D_MODEL).reshape(1, D_MODEL),
        "ln2_g": take(7, D_MODEL).reshape(1, D_MODEL), "ln2_b": take(8, D_MODEL).reshape(1, D_MODEL),
    }
    g_big = {"w_in": red[0], "w_branch_hg": red[1], "w_branch_attn": red[2], "w_out": red[3],
             "w_ffn_in": red[4], "w_ffn_out": red[5]}

    names = ["meta_tokens", "ln_emb_g", "ln_emb_b", "w_in", "hg_lower_bounds", "hg_norm_g", "attn_sinks",
             "w_branch_hg", "w_branch_attn", "w_out", "ln1_g", "ln1_b", "w_ffn_in", "w_ffn_out", "ln2_g", "ln2_b"]
    given = dict(
        meta_tokens=(meta_tokens, m_meta_tokens, v_meta_tokens), ln_emb_g=(ln_emb_g, m_ln_emb_g, v_ln_emb_g),
        ln_emb_b=(ln_emb_b, m_ln_emb_b, v_ln_emb_b), w_in=(w_in, m_w_in, v_w_in),
        hg_lower_bounds=(hg_lower_bounds, m_hg_lower_bounds, v_hg_lower_bounds),
        hg_norm_g=(hg_norm_g, m_hg_norm_g, v_hg_norm_g), attn_sinks=(attn_sinks, m_attn_sinks, v_attn_sinks),
        w_branch_hg=(w_branch_hg, m_w_branch_hg, v_w_branch_hg),
        w_branch_attn=(w_branch_attn, m_w_branch_attn, v_w_branch_attn), w_out=(w_out, m_w_out, v_w_out),
        ln1_g=(ln1_g, m_ln1_g, v_ln1_g), ln1_b=(ln1_b, m_ln1_b, v_ln1_b), w_ffn_in=(w_ffn_in, m_w_ffn_in, v_w_ffn_in),
        w_ffn_out=(w_ffn_out, m_w_ffn_out, v_w_ffn_out), ln2_g=(ln2_g, m_ln2_g, v_ln2_g), ln2_b=(ln2_b, m_ln2_b, v_ln2_b))
    two_d = lambda a: a.reshape(8, BLOCK) if a.ndim == 1 else a.reshape(a.shape[-2], a.shape[-1])
    small_names = [nm for nm in names if nm not in g_big]
    small_d, small_m, small_v = adamw_small([two_d(given[nm][0]) for nm in small_names],
                                            [two_d(g_small[nm]) for nm in small_names],
                                            [two_d(given[nm][1]) for nm in small_names],
                                            [two_d(given[nm][2]) for nm in small_names])
    out_g, out_d, out_m, out_v = [], [], [], []
    for nm in names:
        w, m, v = given[nm]
        shape = w.shape
        if nm == "w_in":
            t = lambda a: jnp.swapaxes(two_d(a), 0, 1)
            g, d, mn, vn = [t(a) for a in adamw(t(w), g_big[nm], t(m), t(v), "adamw_" + nm)]
        elif nm in g_big:
            g, d, mn, vn = adamw(two_d(w), g_big[nm], two_d(m), two_d(v), "adamw_" + nm)
        else:
            k = small_names.index(nm)
            g, d, mn, vn = g_small[nm], small_d[k], small_m[k], small_v[k]
        out_g.append(g.reshape(shape))
        out_d.append(d.reshape(shape))
        out_m.append(mn.reshape(shape))
        out_v.append(vn.reshape(shape))

    loss = take(10, 1)[0]
    grad_x = dx.reshape(x.shape)
    return (loss, grad_x, *out_g, *out_d, *out_m, *out_v)
```

```python
import functools

import jax
import jax.numpy as jnp
from jax import lax
from jax.experimental import pallas as pl
from jax.experimental.pallas import tpu as pltpu

f32 = jnp.float32
bf16 = jnp.bfloat16

D_MODEL = 1024
BLOCK = 128
N_META = 16
PAD = BLOCK - N_META
HG_HEADS = 4
HG_K = 128
ATT_HEADS = 8
HEAD_DIM = 64
ATT_QW = ATT_HEADS * HEAD_DIM
D_FF = 2816
EPS = 1e-5
ALPHA = 2.0 ** 0.25
ROPE_THETA = 10000.0
N_A = 2816
N_G = 2048
IN_W = N_A + N_G
N_SHARD = 4
N_DEV = 8

ADAM_LR = 0.001
ADAM_B1 = 0.9
ADAM_B2 = 0.999
ADAM_EPS = 1e-08
ADAM_WD = 0.01
ADAM_STEP = 10

TM = 256
LEAD = TM // BLOCK - 1

VMEM_LIMIT = 56 * 1024 * 1024
VMEM_LIMIT_ALL_WEIGHTS = 62 * 1024 * 1024
MESH = pl.DeviceIdType.MESH


def _cparams(sem, vmem=VMEM_LIMIT):
    return pltpu.CompilerParams(dimension_semantics=sem, vmem_limit_bytes=vmem)


def _const_spec(shape):
    zeros = (0,) * len(shape)
    return pl.BlockSpec(shape, lambda *_: zeros, pipeline_mode=pl.Buffered(1))


def _dot(a, b, ca, cb):
    return lax.dot_general(a.astype(bf16), b.astype(bf16), (((ca,), (cb,)), ((), ())),
                           preferred_element_type=f32)


@jax.custom_vjp
def mm(a, b):
    return _dot(a, b, 1, 0)


mm.defvjp(lambda a, b: (_dot(a, b, 1, 0), (a, b)),
          lambda r, g: (_dot(g, r[1], 1, 1), _dot(r[0], g, 0, 0)))


@jax.custom_vjp
def mm_nt(a, b):
    return _dot(a, b, 1, 1)


mm_nt.defvjp(lambda a, b: (_dot(a, b, 1, 1), (a, b)),
             lambda r, g: (_dot(g, r[1], 1, 0), _dot(g, r[0], 0, 0)))


@jax.custom_vjp
def mm_tn(a, b):
    return _dot(a, b, 0, 0)


mm_tn.defvjp(lambda a, b: (_dot(a, b, 0, 0), (a, b)),
             lambda r, g: (_dot(r[1], g, 1, 1), _dot(r[0], g, 1, 0)))


@functools.partial(jax.custom_vjp, nondiff_argnums=(1,))
def roll_lanes(x, shift):
    return pltpu.roll(x, shift, 1)


roll_lanes.defvjp(lambda x, shift: (pltpu.roll(x, shift, 1), None),
                  lambda shift, _, g: (pltpu.roll(g, (128 - shift) % 128, 1),))


@jax.custom_vjp
def _sigmoid(x):
    return 1.0 / (1.0 + jnp.exp(-x))


def _sigmoid_fwd(x):
    s = 1.0 / (1.0 + jnp.exp(-x))
    return s, s


_sigmoid.defvjp(_sigmoid_fwd, lambda s, g: (g * s * (1.0 - s),))


@jax.custom_vjp
def _recip(x):
    return 1.0 / x


def _recip_fwd(x):
    r = 1.0 / x
    return r, r


_recip.defvjp(_recip_fwd, lambda r, g: (-g * r * r,))


def _ln_stats(x):
    mu = jnp.mean(x, axis=-1, keepdims=True)
    xc = x - mu
    var = jnp.mean(xc * xc, axis=-1, keepdims=True)
    rs = lax.rsqrt(var + EPS)
    return xc * rs, rs


def _ln_bwd(dy, xh, rs, g):
    dxh = dy * g
    m1 = jnp.mean(dxh, axis=-1, keepdims=True)
    m2 = jnp.mean(dxh * xh, axis=-1, keepdims=True)
    return rs * (dxh - m1 - xh * m2)


def _row_ids(i):
    return i * BLOCK + lax.broadcasted_iota(jnp.int32, (BLOCK, 1), 0)


def _tm_rows(i):
    return i * TM + lax.broadcasted_iota(jnp.int32, (TM, 1), 0)


def _tm_row(n):
    return pl.BlockSpec((TM, n), lambda i: (i, 0))


def _tm_tokens():
    return pl.BlockSpec((TM, D_MODEL), lambda i: (jnp.maximum(i - 1, 0), 0))


Q_COL, V_COL = 4 * HG_HEADS * HG_K, N_A - BLOCK


def emb_inproj(x, metablk, g, b, w_in, cos, sin):
    nsteps = x.shape[0] // TM + 1

    def body(x_ref, mb_ref, g_ref, b_ref, w_ref, cos_ref, sin_ref, h0_ref, h0b_ref, pa_ref, pg_ref):
        i = pl.program_id(0)
        xb = jnp.where(i == 0, mb_ref[...], x_ref[...])
        xh, _ = _ln_stats(xb)
        y = xh * g_ref[...] + b_ref[...]
        y = jnp.where(_tm_rows(i) >= TM - N_META, y, 0.0)
        h0_ref[...] = y
        yb = y.astype(bf16)
        h0b_ref[...] = yb
        pa = _dot(yb, w_ref[:N_A, :], 1, 1)
        cos, sin = cos_ref[...], sin_ref[...]
        pa_ref[:, :Q_COL] = pa[:, :Q_COL]
        for c0 in range(Q_COL, V_COL, BLOCK):
            pa_ref[:, c0:c0 + BLOCK] = _rope(pa[:, c0:c0 + BLOCK], cos, sin)
        pa_ref[:, V_COL:] = pa[:, V_COL:]
        pg_ref[...] = _dot(yb, w_ref[N_A:, :], 1, 1)

    p = nsteps * TM
    row = _tm_row
    return pl.pallas_call(
        body, name="emb_inproj", grid=(nsteps,),
        in_specs=[_tm_tokens(),
                  _const_spec((TM, D_MODEL)), _const_spec((1, D_MODEL)), _const_spec((1, D_MODEL)),
                  _const_spec((IN_W, D_MODEL)), _tm_row(BLOCK), _tm_row(BLOCK)],
        out_specs=[row(D_MODEL), row(D_MODEL), row(N_A), row(N_G)],
        out_shape=[jax.ShapeDtypeStruct((p, D_MODEL), f32), jax.ShapeDtypeStruct((p, D_MODEL), bf16),
                   jax.ShapeDtypeStruct((p, N_A), f32), jax.ShapeDtypeStruct((p, N_G), f32)],
        compiler_params=_cparams(("parallel",)),
    )(x, metablk, g, b, w_in, cos, sin)


def _hgrn_chunk(valid, st, hq, hf, hi, hg, lbraw, ng):
    lb = _sigmoid(lbraw[0:1] - lbraw[1:2])
    q = hq * _sigmoid(hq)
    fg = lb + (1.0 - lb) * _sigmoid(hf)
    logf = jnp.where(valid, jnp.log(fg), 0.0)
    k = jnp.where(valid, 1.0 - fg, 0.0)
    v = hi
    r = lax.broadcasted_iota(jnp.int32, (BLOCK, BLOCK), 0)
    c = lax.broadcasted_iota(jnp.int32, (BLOCK, BLOCK), 1)
    tril = (c <= r).astype(f32)
    bcum = jnp.dot(tril, logf, precision=lax.Precision.HIGHEST, preferred_element_type=f32)
    blast = bcum[BLOCK - 1:BLOCK]
    rows = lax.broadcasted_iota(jnp.int32, (BLOCK, 1), 0)
    sub8 = lax.broadcasted_iota(jnp.int32, (BLOCK // 8, 8, HG_K), 1)
    b8 = bcum.reshape(BLOCK // 8, 8, HG_K)
    row_of_8 = lambda j: jnp.broadcast_to(b8[:, j:j + 1, :], b8.shape)
    a = jnp.where(r == c, jnp.sum(q * k, axis=-1, keepdims=True), 0.0)
    seg = BLOCK
    while seg >= 2:
        half = seg // 2
        if seg >= 8:
            bs = bcum.reshape(BLOCK // seg, seg, HG_K)
            ref = jnp.broadcast_to(bs[:, half - 1:half, :], bs.shape)
        elif seg == 4:
            ref = jnp.where(sub8 < 4, row_of_8(1), row_of_8(5))
        else:
            ref = jnp.where(sub8 < 2, row_of_8(0), jnp.where(sub8 < 4, row_of_8(2),
                                                             jnp.where(sub8 < 6, row_of_8(4), row_of_8(6))))
        ref = ref.reshape(BLOCK, HG_K)
        upper = (rows % seg) >= half
        q_up = q * jnp.exp(jnp.where(upper, bcum - ref, -jnp.inf))
        k_lo = k * jnp.exp(jnp.where(upper, -jnp.inf, ref - bcum))
        a = a + jnp.where((r // seg) == (c // seg), mm_nt(q_up, k_lo), 0.0)
        seg = half
    o = mm_nt(q * jnp.exp(bcum), st) + mm(a, v)
    st_new = st * jnp.exp(blast) + mm_tn(v, k * jnp.exp(blast - bcum))
    on = o * lax.rsqrt(jnp.mean(o * o, axis=-1, keepdims=True) + EPS) * ng
    return st_new, on * (hg * _sigmoid(hg))


def _hgrn_in_specs(rowmap):
    wide = lambda col: pl.BlockSpec((BLOCK, HG_HEADS * HG_K), lambda i: (rowmap(i) + LEAD, col))
    return [wide(0), wide(1), wide(2), wide(3), _const_spec((2, HG_HEADS * HG_K)), _const_spec((1, HG_K))]


def _head(ref, h):
    return ref[:, h * HG_K:(h + 1) * HG_K]


def hgrn_fwd(pa, lbraw, ng, nb, shards):
    n = len(shards)

    def body(hq_ref, hf_ref, hi_ref, hg_ref, lb_ref, ng_ref, *rest):
        srcs, (og_ref, sp_ref), dsts = rest[:n], rest[n:n + 2], rest[n + 2:2 * n + 2]
        st_ref = rest[2 * n + 2]
        start, wait = _shard_push(srcs, dsts, *rest[2 * n + 3:])
        i = pl.program_id(0)

        @pl.when(i == 0)
        def _():
            st_ref[...] = jnp.zeros_like(st_ref)
            start()

        @pl.when(i == nb - 1)
        def _():
            wait()

        valid = _row_ids(i) >= PAD
        for h in range(HG_HEADS):
            st = st_ref[h]
            sp_ref[0, h] = st
            st_new, out = _hgrn_chunk(valid, st, _head(hq_ref, h), _head(hf_ref, h), _head(hi_ref, h),
                                      _head(hg_ref, h), _head(lb_ref, h), ng_ref[...])
            st_ref[h] = st_new
            og_ref[:, h * HG_K:(h + 1) * HG_K] = out.astype(bf16)

    p = (nb + LEAD) * BLOCK
    push_in, push_out, push_shape, push_scratch = _push_specs(shards)
    return pl.pallas_call(
        body, name="hgrn_fwd", grid=(nb,),
        in_specs=_hgrn_in_specs(lambda i: i) + push_in,
        out_specs=[pl.BlockSpec((BLOCK, HG_HEADS * HG_K), lambda i: (i + LEAD, 0)),
                   pl.BlockSpec((1, HG_HEADS, HG_K, HG_K), lambda i: (i, 0, 0, 0))] + push_out,
        out_shape=[jax.ShapeDtypeStruct((p, HG_HEADS * HG_K), bf16),
                   jax.ShapeDtypeStruct((nb, HG_HEADS, HG_K, HG_K), f32)] + push_shape,
        scratch_shapes=[pltpu.VMEM((HG_HEADS, HG_K, HG_K), f32)] + push_scratch,
        compiler_params=_cparams(("arbitrary",)),
    )(pa, pa, pa, pa, lbraw, ng, *shards)


def hgrn_bwd(pa, lbraw, ng, sprev, dog, nb, grads):
    n = len(grads)

    def body(hq_ref, hf_ref, hi_ref, hg_ref, lb_ref, ng_ref, sp_ref, do_ref, *rest):
        srcs, rest = rest[:n], rest[n:]
        dq_ref, df_ref, di_ref, dg_ref, dlb_ref, dng_ref = rest[:6]
        dsts, dst_ref = rest[6:6 + n], rest[6 + n]
        start, wait = _grad_push(srcs, dsts, *rest[7 + n:])
        i = pl.program_id(0)

        @pl.when(i == 0)
        def _():
            dst_ref[...] = jnp.zeros_like(dst_ref)
            dlb_ref[...] = jnp.zeros_like(dlb_ref)
            dng_ref[...] = jnp.zeros_like(dng_ref)
            start()

        valid = _row_ids(nb - 1 - i) >= PAD
        dng_sum = jnp.zeros((1, HG_K), f32)
        for h in range(HG_HEADS):
            cols = slice(h * HG_K, (h + 1) * HG_K)
            _, vjp = jax.vjp(functools.partial(_hgrn_chunk, valid), sp_ref[0, h], _head(hq_ref, h), _head(hf_ref, h),
                             _head(hi_ref, h), _head(hg_ref, h), _head(lb_ref, h), ng_ref[...])
            dst, dq, df, di, dg, dlb, dng = vjp((dst_ref[h], _head(do_ref, h).astype(f32)))
            dst_ref[h] = dst
            dq_ref[:, cols] = dq.astype(bf16)
            df_ref[:, cols] = df.astype(bf16)
            di_ref[:, cols] = di.astype(bf16)
            dg_ref[:, cols] = dg.astype(bf16)
            dlb_ref[:, cols] += dlb
            dng_sum = dng_sum + dng
        dng_ref[...] += dng_sum
        pl.when(i == nb - 1)(wait)

    p = (nb + LEAD) * BLOCK
    rev = lambda i: nb - 1 - i
    hw = HG_HEADS * HG_K
    blk = pl.BlockSpec((BLOCK, hw), lambda i: (rev(i) + LEAD, 0))
    wide = jax.ShapeDtypeStruct((p, hw), bf16)
    push_in, push_out, push_shape, push_scratch = _grad_push_specs(grads)
    return pl.pallas_call(
        body, name="hgrn_bwd", grid=(nb,),
        in_specs=_hgrn_in_specs(rev) + [pl.BlockSpec((1, HG_HEADS, HG_K, HG_K), lambda i: (rev(i), 0, 0, 0)), blk]
        + push_in,
        out_specs=[blk, blk, blk, blk, pl.BlockSpec((2, hw), lambda i: (0, 0)), pl.BlockSpec((1, HG_K), lambda i: (0, 0))]
        + push_out,
        out_shape=[wide, wide, wide, wide, jax.ShapeDtypeStruct((2, hw), f32), jax.ShapeDtypeStruct((1, HG_K), f32)]
        + push_shape,
        scratch_shapes=[pltpu.VMEM((HG_HEADS, HG_K, HG_K), f32)] + push_scratch,
        compiler_params=_cparams(("arbitrary",)),
    )(pa, pa, pa, pa, lbraw, ng, sprev, dog, *grads)


def _rot_half(x):
    lane = lax.broadcasted_iota(jnp.int32, x.shape, 1)
    return jnp.where(lane % HEAD_DIM < HEAD_DIM // 2, -pltpu.roll(x, BLOCK - HEAD_DIM // 2, 1),
                     pltpu.roll(x, HEAD_DIM // 2, 1))


def _rope(x, cos, sin):
    return x * cos + _rot_half(x) * sin


def _rope_transposed(g, cos, sin):
    return g * cos - _rot_half(g * sin)


def _both_halves(x, g):
    lo = lax.broadcasted_iota(jnp.int32, x.shape, 1) < HEAD_DIM
    sw = roll_lanes(x, HEAD_DIM)
    return jnp.where(lo, x, sw) if g == 0 else jnp.where(lo, sw, x)


def _attn_block(band_ok, meta_ok, q, kp, kc, vp, vc, km, vm, *sinks):
    neg = jnp.finfo(f32).min
    scale = HEAD_DIM ** -0.5
    group = ATT_HEADS // 2
    lo = lax.broadcasted_iota(jnp.int32, (BLOCK, BLOCK), 1) < HEAD_DIM
    t = lax.broadcasted_iota(jnp.int32, (group * BLOCK, BLOCK), 0) % BLOCK
    col = lax.broadcasted_iota(jnp.int32, (group * BLOCK, BLOCK), 1)
    own = col <= t
    is_sink = col == N_META
    qr = [q[:, m * BLOCK:(m + 1) * BLOCK] for m in range(ATT_HEADS // 2)]
    slabs = []
    for g in range(2):
        kp_g, kc_g, vp_g, vc_g, km_g, vm_g = [_both_halves(a, g) for a in (kp, kc, vp, vc, km, vm)]
        qs = jnp.concatenate([jnp.where(lo if h % 2 == 0 else ~lo, qr[2 * g + h // 2], 0.0) for h in range(group)],
                             axis=0)
        sink = jnp.concatenate([jnp.broadcast_to(sinks[group * g + h], (BLOCK, 1)) for h in range(group)], axis=0)
        sb = jnp.where(band_ok, jnp.where(own, mm_nt(qs, kc_g), mm_nt(qs, kp_g)) * scale, neg)
        no_keys = jnp.zeros((BLOCK - N_META, BLOCK), f32)
        sme = jnp.where(meta_ok, mm_nt(qs, jnp.concatenate([km_g, no_keys], axis=0)) * scale,
                        jnp.where(is_sink, sink, neg))
        mx = lax.stop_gradient(jnp.max(jnp.maximum(sb, sme), axis=-1, keepdims=True))
        eb, em = jnp.exp(sb - mx), jnp.exp(sme - mx)
        inv = _recip(jnp.sum(eb + em, axis=-1, keepdims=True))
        pb = eb * inv
        o = (mm(jnp.where(own, pb, 0.0), vc_g) + mm(jnp.where(own, 0.0, pb), vp_g)
             + mm(em * inv, jnp.concatenate([vm_g, no_keys], axis=0)))
        for m in range(2):
            even, odd = o[2 * m * BLOCK:(2 * m + 1) * BLOCK], o[(2 * m + 1) * BLOCK:(2 * m + 2) * BLOCK]
            slabs.append(jnp.where(lo, even, odd))
    return jnp.concatenate(slabs, axis=1)


def _attn_masks(i):
    group = ATT_HEADS // 2
    t = lax.broadcasted_iota(jnp.int32, (group * BLOCK, BLOCK), 0) % BLOCK
    s = lax.broadcasted_iota(jnp.int32, (group * BLOCK, BLOCK), 1)
    kpos = jnp.where(s <= t, i * BLOCK - PAD + s, jnp.where(i > 0, (i - 1) * BLOCK - PAD + s, -1))
    band_ok = kpos >= N_META
    qpos = i * BLOCK - PAD + lax.broadcasted_iota(jnp.int32, (group * BLOCK, 1), 0) % BLOCK
    meta_ok = (s < N_META) & (s <= qpos)
    return band_ok, meta_ok


def _attn_in_specs():
    cur, prev, first = (lambda i: i), (lambda i: jnp.maximum(i - 1, 0)), (lambda i: 0)
    kcol, vcol = V_COL // BLOCK - 1, V_COL // BLOCK
    blk = lambda rowmap, col: pl.BlockSpec((BLOCK, BLOCK), lambda i: (rowmap(i) + LEAD, col))
    return [pl.BlockSpec((BLOCK, ATT_QW), lambda i: (i + LEAD, Q_COL // ATT_QW)),
            blk(prev, kcol), blk(cur, kcol), blk(prev, vcol), blk(cur, vcol), blk(first, kcol), blk(first, vcol),
            _const_spec((ATT_HEADS, BLOCK))]


def _attn_row(n):
    return pl.BlockSpec((BLOCK, n), lambda i: (i + LEAD, 0))


def _attn_operands(q_ref, kp_ref, kc_ref, vp_ref, vc_ref, km_ref, vm_ref, sk_ref):
    args = (q_ref[...], kp_ref[...], kc_ref[...], vp_ref[...], vc_ref[...], km_ref[PAD:, :], vm_ref[PAD:, :])
    sinks = tuple(sk_ref[j:j + 1, 0:1] for j in range(ATT_HEADS))
    return args + sinks


def attn_fwd(pa, sinks8, nb, shards):
    n = len(shards)
    n_in = 8

    def body(*refs):
        srcs, o_ref, dsts = refs[n_in:n_in + n], refs[n_in + n], refs[n_in + n + 1:n_in + 2 * n + 1]
        start, wait = _shard_push(srcs, dsts, *refs[n_in + 2 * n + 1:])
        i = pl.program_id(0)
        pl.when(i == 0)(start)
        band_ok, meta_ok = _attn_masks(i)
        o_ref[...] = _attn_block(band_ok, meta_ok, *_attn_operands(*refs[:n_in])).astype(bf16)
        pl.when(i == nb - 1)(wait)

    push_in, push_out, push_shape, push_scratch = _push_specs(shards)
    return pl.pallas_call(
        body, name="attn_fwd", grid=(nb,), in_specs=_attn_in_specs() + push_in,
        out_specs=[_attn_row(ATT_QW)] + push_out,
        out_shape=[jax.ShapeDtypeStruct(((nb + LEAD) * BLOCK, ATT_QW), bf16)] + push_shape,
        scratch_shapes=push_scratch,
        compiler_params=_cparams(("arbitrary",)),
    )(pa, pa, pa, pa, pa, pa, pa, sinks8, *shards)


def attn_bwd(pa, sinks8, do, nb, grads):
    n = len(grads)

    def body(*refs):
        do_ref, srcs = refs[8], refs[9:9 + n]
        dq_ref, dkc_ref, dkp_ref, dvc_ref, dvp_ref, dkm_ref, dvm_ref, dsk_ref = refs[9 + n:17 + n]
        start, wait = _grad_push(srcs, refs[17 + n:17 + 2 * n], *refs[17 + 2 * n:])
        i = pl.program_id(0)

        @pl.when(i == 0)
        def _():
            dkm_ref[...] = jnp.zeros((N_META, BLOCK), f32)
            dvm_ref[...] = jnp.zeros((N_META, BLOCK), f32)
            dsk_ref[...] = jnp.zeros((ATT_HEADS, BLOCK), f32)
            start()

        band_ok, meta_ok = _attn_masks(i)
        _, vjp = jax.vjp(functools.partial(_attn_block, band_ok, meta_ok), *_attn_operands(*refs[:8]))
        grads = vjp(do_ref[...].astype(f32))
        dq_ref[...] = grads[0]
        dkp_ref[...] = grads[1]
        dkc_ref[...] = grads[2]
        dvp_ref[...] = grads[3]
        dvc_ref[...] = grads[4]
        dkm_ref[...] += grads[5]
        dvm_ref[...] += grads[6]
        for j in range(ATT_HEADS):
            dsk_ref[j:j + 1, :] += jnp.broadcast_to(grads[7 + j], (1, BLOCK))
        pl.when(i == nb - 1)(wait)

    p = (nb + LEAD) * BLOCK
    row = _attn_row(BLOCK)
    const = lambda r: pl.BlockSpec((r, BLOCK), lambda i: (0, 0))
    part = jax.ShapeDtypeStruct((p, BLOCK), f32)
    push_in, push_out, push_shape, push_scratch = _grad_push_specs(grads)
    return pl.pallas_call(
        body, name="attn_bwd", grid=(nb,),
        in_specs=_attn_in_specs() + [_attn_row(ATT_QW)] + push_in,
        out_specs=[_attn_row(ATT_QW), row, row, row, row,
                   const(N_META), const(N_META), const(ATT_HEADS)] + push_out,
        out_shape=[jax.ShapeDtypeStruct((p, ATT_QW), f32), part, part, part, part,
                   jax.ShapeDtypeStruct((N_META, BLOCK), f32), jax.ShapeDtypeStruct((N_META, BLOCK), f32),
                   jax.ShapeDtypeStruct((ATT_HEADS, BLOCK), f32)] + push_shape,
        scratch_shapes=push_scratch,
        compiler_params=_cparams(("arbitrary",)),
    )(pa, pa, pa, pa, pa, pa, pa, sinks8, do, *grads)


def _mid_forward(h0_ref, pg_ref, og, oa, wbh_ref, wba_ref, wo_ref, g1, b1):
    yh = jnp.dot(og, wbh_ref[...], preferred_element_type=f32)
    ya = jnp.dot(oa, wba_ref[...], preferred_element_type=f32)
    gh = _sigmoid(pg_ref[:, :D_MODEL])
    ga = _sigmoid(pg_ref[:, D_MODEL:])
    mixin = (gh * yh + ga * ya).astype(bf16)
    r1 = ALPHA * h0_ref[...] + jnp.dot(mixin, wo_ref[...], preferred_element_type=f32)
    xh1, rs1 = _ln_stats(r1)
    return yh, ya, gh, ga, mixin, xh1, rs1, xh1 * g1 + b1


def _skip_step_without_tokens(rows, n_in):
    def body(*refs):
        i = pl.program_id(0)

        @pl.when(i == 0)
        def _():
            for r in refs[n_in:]:
                r[...] = jnp.zeros_like(r)

        pl.when(i > 0)(lambda: rows(*refs))

    return body


def _mid_weight_specs():
    hw = HG_HEADS * HG_K
    return [_const_spec((hw, D_MODEL)), _const_spec((ATT_QW, D_MODEL)), _const_spec((D_MODEL, D_MODEL)),
            _const_spec((1, D_MODEL)), _const_spec((1, D_MODEL))]


def mid_front_ffn(h0, pg, og, oatt, target, wbh, wba, wout, wfi, wfo, ln1g, ln1b, ln2g, ln2b):
    def rows(h0_ref, pg_ref, og_ref, oa_ref, t_ref, wbh_ref, wba_ref, wo_ref, g1_ref, b1_ref, wfi_ref, wfo_ref,
             g2_ref, b2_ref, dh1_ref, dau_ref, s_ref, dr2_ref, mix_ref, h1b_ref, ogc_ref, oac_ref,
             loss_ref, dg2_ref, db2_ref):
        i = pl.program_id(0)

        @pl.when(i == 0)
        def _():
            for r in (loss_ref, dg2_ref, db2_ref):
                r[...] = jnp.zeros_like(r)

        used = _tm_rows(i) >= LEAD * BLOCK
        og = jnp.where(used, og_ref[...], jnp.zeros_like(og_ref))
        oa = jnp.where(used, oa_ref[...], jnp.zeros_like(oa_ref))
        ogc_ref[...] = og
        oac_ref[...] = oa
        *_, mixin, _, _, h1 = _mid_forward(h0_ref, pg_ref, og, oa, wbh_ref, wba_ref, wo_ref, g1_ref[...], b1_ref[...])
        mix_ref[...] = mixin
        h1b = h1.astype(bf16)
        h1b_ref[...] = h1b
        g2, b2 = g2_ref[...], b2_ref[...]
        au = jnp.dot(h1b, wfi_ref[...], preferred_element_type=f32)
        a, u = au[:, :D_FF], au[:, D_FF:]
        sg = _sigmoid(a)
        sa = a * sg
        s = (sa * u).astype(bf16)
        s_ref[...] = s
        r2 = ALPHA * h1 + jnp.dot(s, wfo_ref[...], preferred_element_type=f32)
        xh2, rs2 = _ln_stats(r2)
        diff = jnp.where(i > 0, xh2 * g2 + b2 - t_ref[...], 0.0)
        loss_ref[...] += jnp.sum(diff * diff) * (0.5 / D_MODEL)
        dy = diff * (1.0 / D_MODEL)
        dg2_ref[...] += jnp.sum(dy * xh2, axis=0, keepdims=True)
        db2_ref[...] += jnp.sum(dy, axis=0, keepdims=True)
        dr2 = _ln_bwd(dy, xh2, rs2, g2)
        dr2b = dr2.astype(bf16)
        dr2_ref[...] = dr2b
        ds = _dot(dr2b, wfo_ref[...], 1, 1)
        da = (ds * u) * (sg * (1.0 + a * (1.0 - sg)))
        du = ds * sa
        dau = jnp.concatenate([da, du], axis=1).astype(bf16)
        dau_ref[...] = dau
        dh1_ref[...] = ALPHA * dr2 + _dot(dau, wfi_ref[...], 1, 1)

    body = _skip_step_without_tokens(rows, 14)
    p = h0.shape[0]
    hw = HG_HEADS * HG_K
    vec = lambda: pl.BlockSpec((1, D_MODEL), lambda i: (0, 0))
    sds = lambda n, dt: jax.ShapeDtypeStruct((p, n), dt)
    return pl.pallas_call(
        body, name="mid_front_ffn", grid=(p // TM,),
        in_specs=[_tm_row(D_MODEL), _tm_row(N_G), _tm_row(hw), _tm_row(ATT_QW), _tm_tokens()] + _mid_weight_specs()
        + [_const_spec((D_MODEL, 2 * D_FF)), _const_spec((D_FF, D_MODEL)), _const_spec((1, D_MODEL)),
           _const_spec((1, D_MODEL))],
        out_specs=[_tm_row(D_MODEL), _tm_row(2 * D_FF), _tm_row(D_FF), _tm_row(D_MODEL), _tm_row(D_MODEL),
                   _tm_row(D_MODEL), _tm_row(hw), _tm_row(ATT_QW), pl.BlockSpec((1, 1), lambda i: (0, 0)), vec(), vec()],
        out_shape=[sds(D_MODEL, f32), sds(2 * D_FF, bf16), sds(D_FF, bf16), sds(D_MODEL, bf16), sds(D_MODEL, bf16),
                   sds(D_MODEL, bf16), sds(hw, bf16), sds(ATT_QW, bf16),
                   jax.ShapeDtypeStruct((1, 1), f32)] + [jax.ShapeDtypeStruct((1, D_MODEL), f32)] * 2,
        compiler_params=_cparams(("arbitrary",), VMEM_LIMIT_ALL_WEIGHTS),
    )(h0, pg, og, oatt, target, wbh, wba, wout, ln1g, ln1b, wfi, wfo, ln2g, ln2b)


def mid_back(dh1, h0, pg, ogc, oac, wbh, wba, wout, ln1g, ln1b):
    def rows(dh1_ref, h0_ref, pg_ref, og_ref, oa_ref, wbh_ref, wba_ref, wo_ref, g1_ref, b1_ref,
             dh0_ref, dpg_ref, dog_ref, doa_ref, dyh_ref, dya_ref, dr1_ref, dg1_ref, db1_ref):
        @pl.when(pl.program_id(0) == 0)
        def _():
            dg1_ref[...] = jnp.zeros_like(dg1_ref)
            db1_ref[...] = jnp.zeros_like(db1_ref)

        g1 = g1_ref[...]
        yh, ya, gh, ga, _, xh1, rs1, _ = _mid_forward(h0_ref, pg_ref, og_ref[...], oa_ref[...], wbh_ref, wba_ref,
                                                      wo_ref, g1, b1_ref[...])
        dh1 = dh1_ref[...]
        dg1_ref[...] += jnp.sum(dh1 * xh1, axis=0, keepdims=True)
        db1_ref[...] += jnp.sum(dh1, axis=0, keepdims=True)
        dr1 = _ln_bwd(dh1, xh1, rs1, g1)
        dr1b = dr1.astype(bf16)
        dr1_ref[...] = dr1b
        dh0_ref[...] = ALPHA * dr1
        dmix = _dot(dr1b, wo_ref[...], 1, 1)
        dyh = (dmix * gh).astype(bf16)
        dya = (dmix * ga).astype(bf16)
        dyh_ref[...] = dyh
        dya_ref[...] = dya
        dpg_ref[:, :D_MODEL] = (dmix * yh * gh * (1.0 - gh)).astype(bf16)
        dpg_ref[:, D_MODEL:] = (dmix * ya * ga * (1.0 - ga)).astype(bf16)
        dog_ref[...] = _dot(dyh, wbh_ref[...], 1, 1).astype(bf16)
        doa_ref[...] = _dot(dya, wba_ref[...], 1, 1).astype(bf16)

    body = _skip_step_without_tokens(rows, 10)
    p = h0.shape[0]
    hw = HG_HEADS * HG_K
    vec = lambda: pl.BlockSpec((1, D_MODEL), lambda i: (0, 0))
    sds = lambda n, dt: jax.ShapeDtypeStruct((p, n), dt)
    return pl.pallas_call(
        body, name="mid_back", grid=(p // TM,),
        in_specs=[_tm_row(D_MODEL), _tm_row(D_MODEL), _tm_row(N_G), _tm_row(hw), _tm_row(ATT_QW)] + _mid_weight_specs(),
        out_specs=[_tm_row(D_MODEL), _tm_row(N_G), _tm_row(hw), _tm_row(ATT_QW), _tm_row(D_MODEL), _tm_row(D_MODEL),
                   _tm_row(D_MODEL), vec(), vec()],
        out_shape=[sds(D_MODEL, f32), sds(N_G, bf16), sds(hw, bf16), sds(ATT_QW, bf16), sds(D_MODEL, bf16),
                   sds(D_MODEL, bf16), sds(D_MODEL, bf16)] + [jax.ShapeDtypeStruct((1, D_MODEL), f32)] * 2,
        compiler_params=_cparams(("arbitrary",)),
    )(dh1, h0, pg, ogc, oac, wbh, wba, wout, ln1g, ln1b)


def inproj_bwd(dh0p, dhq, dhf, dhi, dhg, daq, dkc, dkp, dvc, dvp, dkm, dvm, dpg, w_in, x, metablk, g, b, cos, sin):
    p = dh0p.shape[0]
    nbk = p // BLOCK
    per = TM // BLOCK

    def body(dh0_ref, dq_ref, df_ref, di_ref, dg_ref, daq_ref, dkc_ref, *rest):
        dkp_refs, dvc_ref, dvp_refs = rest[:per], rest[per], rest[per + 1:2 * per + 1]
        (dkm_ref, dvm_ref, dpg_ref, w_ref, x_ref, mb_ref, g_ref, b_ref, cos_ref, sin_ref,
         dproj_ref, dx_ref, dmeta_ref, dlg_ref, dlb_ref) = rest[2 * per + 1:]
        i = pl.program_id(0)

        @pl.when(i == 0)
        def _():
            dlg_ref[...] = jnp.zeros_like(dlg_ref)
            dlb_ref[...] = jnp.zeros_like(dlb_ref)

        zero_pad = jnp.zeros((TM - N_META, BLOCK), f32)
        first = i == 0
        rows = _tm_rows(i)

        def keys(cur_ref, next_refs, meta_ref):
            nxt = jnp.concatenate([jnp.where(per * i + 1 + m < nbk, next_refs[m][...], 0.0) for m in range(per)],
                                  axis=0)
            t = cur_ref[...] + nxt
            return t + jnp.where(first, jnp.concatenate([zero_pad, meta_ref[...]], axis=0), 0.0)

        cos, sin = cos_ref[...], sin_ref[...]
        unrotate = lambda t: _rope_transposed(t, cos, sin).astype(bf16)
        dproj = jnp.concatenate(
            [dq_ref[...], df_ref[...], di_ref[...], dg_ref[...]]
            + [unrotate(daq_ref[:, m * BLOCK:(m + 1) * BLOCK]) for m in range(ATT_QW // BLOCK)]
            + [unrotate(keys(dkc_ref, dkp_refs, dkm_ref)), keys(dvc_ref, dvp_refs, dvm_ref).astype(bf16),
               dpg_ref[...]], axis=1)
        dproj = jnp.where(rows >= LEAD * BLOCK, dproj, jnp.zeros_like(dproj))
        dproj_ref[...] = dproj
        valid = rows >= TM - N_META
        dh0 = jnp.where(valid, dh0_ref[...] + _dot(dproj, w_ref[...], 1, 0), 0.0)
        xb = jnp.where(first, mb_ref[...], x_ref[...])
        xh, rs = _ln_stats(xb)
        dlg_ref[...] += jnp.sum(dh0 * xh, axis=0, keepdims=True)
        dlb_ref[...] += jnp.sum(dh0, axis=0, keepdims=True)
        dx = jnp.where(valid, _ln_bwd(dh0, xh, rs, g_ref[...]), 0.0)
        dx_ref[...] = dx

        @pl.when(first)
        def _():
            dmeta_ref[...] = dx[TM - N_META:, :]

    row = _tm_row
    nxt = [pl.BlockSpec((BLOCK, BLOCK), functools.partial(lambda i, m: (jnp.minimum(per * i + 1 + m, nbk - 1), 0), m=m))
           for m in range(per)]
    hw = HG_HEADS * HG_K
    vec = lambda: pl.BlockSpec((1, D_MODEL), lambda i: (0, 0))
    return pl.pallas_call(
        body, name="inproj_bwd", grid=(p // TM,),
        in_specs=[row(D_MODEL), row(hw), row(hw), row(hw), row(hw), row(ATT_QW),
                  row(BLOCK)] + nxt + [row(BLOCK)] + nxt + [_const_spec((N_META, BLOCK)), _const_spec((N_META, BLOCK)),
                  row(N_G), _const_spec((IN_W, D_MODEL)), _tm_tokens(),
                  _const_spec((TM, D_MODEL)), _const_spec((1, D_MODEL)), _const_spec((1, D_MODEL)),
                  row(BLOCK), row(BLOCK)],
        out_specs=[row(IN_W), _tm_tokens(), pl.BlockSpec((N_META, D_MODEL), lambda i: (0, 0)), vec(), vec()],
        out_shape=[jax.ShapeDtypeStruct((p, IN_W), bf16), jax.ShapeDtypeStruct((p - TM, D_MODEL), f32),
                   jax.ShapeDtypeStruct((N_META, D_MODEL), f32),
                   jax.ShapeDtypeStruct((1, D_MODEL), f32), jax.ShapeDtypeStruct((1, D_MODEL), f32)],
        compiler_params=_cparams(("arbitrary",)),
    )(dh0p, dhq, dhf, dhi, dhg, daq, dkc, *([dkp] * per), dvc, *([dvp] * per), dkm, dvm, dpg, w_in, x, metablk, g, b,
      cos, sin)


def wgrad(a, b, name, tk, tn, tp, by_cols, out_dtype=f32):
    p, k = a.shape
    n = b.shape[1]
    nsteps = p // tp

    def body(a_ref, b_ref, o_ref, acc_ref):
        ip = pl.program_id(2)

        @pl.when(ip == 0)
        def _():
            acc_ref[...] = jnp.zeros_like(acc_ref)

        acc_ref[...] += _dot(a_ref[...], b_ref[...], 0, 0)

        @pl.when(ip == nsteps - 1)
        def _():
            for j in range(span):
                o_ref[j] = acc_ref[:, j * width:(j + 1) * width].astype(out_dtype)

    span, width = 1, tn
    if by_cols:
        shard_n = n // N_SHARD
        out_shape = (N_SHARD, k, shard_n)
        if tn >= shard_n:
            span, width = tn // shard_n, shard_n
            omap = lambda ik, jn, ip: (jn, ik, 0)
        else:
            per = shard_n // tn
            omap = lambda ik, jn, ip: (jn // per, ik, jn % per)
    else:
        out_shape = (1, k, n)
        omap = lambda ik, jn, ip: (0, ik, jn)
    return pl.pallas_call(
        body, name=name, grid=(k // tk, n // tn, nsteps),
        in_specs=[pl.BlockSpec((tp, tk), lambda ik, jn, ip: (ip, ik)),
                  pl.BlockSpec((tp, tn), lambda ik, jn, ip: (ip, jn))],
        out_specs=pl.BlockSpec((span, tk, width), omap),
        out_shape=jax.ShapeDtypeStruct(out_shape, out_dtype),
        scratch_shapes=[pltpu.VMEM((tk, tn), f32)],
        compiler_params=_cparams(("parallel", "parallel", "arbitrary")),
    )(a, b)


def _adamw_math(w, g, m, v):
    mn = ADAM_B1 * m + (1.0 - ADAM_B1) * g
    vn = ADAM_B2 * v + (1.0 - ADAM_B2) * (g * g)
    m_hat = mn / (1.0 - ADAM_B1 ** ADAM_STEP)
    v_hat = vn / (1.0 - ADAM_B2 ** ADAM_STEP)
    return -ADAM_LR * (m_hat / (jnp.sqrt(v_hat) + ADAM_EPS) + ADAM_WD * w), mn, vn


def adamw(w, g, m, v, name):
    r, c = w.shape
    tr = r
    for cand in (256, 176, 152, 128):
        if r > cand and r % cand == 0:
            tr = cand
            break

    def body(w_ref, g_ref, m_ref, v_ref, go_ref, d_ref, mo_ref, vo_ref):
        gg = g_ref[...]
        go_ref[...] = gg
        d_ref[...], mo_ref[...], vo_ref[...] = _adamw_math(w_ref[...], gg, m_ref[...], v_ref[...])

    spec = pl.BlockSpec((tr, c), lambda i: (i, 0))
    sds = jax.ShapeDtypeStruct((r, c), f32)
    return pl.pallas_call(
        body, name=name, grid=(r // tr,), in_specs=[spec] * 4, out_specs=[spec] * 4, out_shape=[sds] * 4,
        compiler_params=_cparams(("parallel",)),
    )(w, g, m, v)


def adamw_small(ws, gs, ms, vs):
    n = len(ws)

    def body(*refs):
        ins, outs = refs[:4 * n], refs[4 * n:]
        for k in range(n):
            outs[k][...], outs[n + k][...], outs[2 * n + k][...] = _adamw_math(
                ins[k][...], ins[n + k][...], ins[2 * n + k][...], ins[3 * n + k][...])

    out = pl.pallas_call(body, name="adamw_small",
                         out_shape=[jax.ShapeDtypeStruct(w.shape, f32) for w in ws] * 3)(*ws, *gs, *ms, *vs)
    return out[:n], out[n:2 * n], out[2 * n:]


def _me():
    return lax.axis_index("x"), lax.axis_index("y"), lax.axis_index("c")


def _chip_peer(x, y, c, k):
    return (x ^ (k >> 1), y ^ (k & 1), c)


ANY = pl.BlockSpec(memory_space=pl.ANY)


def gather_weights(now, later):
    n, n_later = len(now), len(later)
    out_dtypes = [bf16 if s.size > 16 * 256 else f32 for s in now]
    halves = [(2, s.shape[0] // 2, s.shape[1]) for s in now]

    def body(*refs):
        ins, later_ins = refs[:n], refs[n:n + n_later]
        outs, later_outs = refs[n + n_later:2 * n + n_later], refs[2 * n + n_later:2 * (n + n_later)]
        stage = refs[2 * (n + n_later):3 * n + 2 * n_later]
        send_sems, recv_sems, pass_send_sems, pass_recv_sems, local_sems = refs[3 * n + 2 * n_later:]
        x, y, c = _me()
        j = 2 * x + y
        sibling = (x, y, 1 - c)

        def over_ici(w, k, slot):
            return pltpu.make_async_remote_copy(
                src_ref=stage[w].at[c], dst_ref=outs[w].at[slot, c], send_sem=send_sems.at[w, k - 1],
                recv_sem=recv_sems.at[w, k - 1], device_id=_chip_peer(x, y, c, k), device_id_type=MESH)

        def passed_on(w, k, half):
            return pltpu.make_async_remote_copy(
                src_ref=outs[w].at[j ^ k, half], dst_ref=outs[w].at[j ^ k, half], send_sem=pass_send_sems.at[w, k - 1],
                recv_sem=pass_recv_sems.at[w, k - 1], device_id=sibling, device_id_type=MESH)

        for w in range(n):
            stage[w][...] = ins[w][...].astype(out_dtypes[w]).reshape(halves[w])
        locs = []
        for w in range(n):
            loc = pltpu.make_async_copy(stage[w], outs[w].at[j], local_sems.at[w])
            loc.start()
            locs.append(loc)
            for k in (1, 2, 3):
                over_ici(w, k, j).start()
        for w in range(n_later):
            later_outs[w][...] = later_ins[w][...].astype(bf16)
        for w in range(n):
            for k in (1, 2, 3):
                over_ici(w, k, j ^ k).wait_recv()
                passed_on(w, k, c).start()
        for w in range(n):
            for k in (1, 2, 3):
                passed_on(w, k, 1 - c).wait_recv()
        for w in range(n):
            for k in (1, 2, 3):
                over_ici(w, k, j).wait_send()
                passed_on(w, k, c).wait_send()
        for loc in locs:
            loc.wait()

    vmem = pl.BlockSpec(memory_space=pltpu.VMEM)
    sem3 = pltpu.SemaphoreType.DMA((n, 3))
    return pl.pallas_call(
        body, name="gather_weights",
        in_specs=[vmem] * (n + n_later), out_specs=[ANY] * n + [vmem] * n_later,
        out_shape=[jax.ShapeDtypeStruct((N_SHARD,) + h, dt) for h, dt in zip(halves, out_dtypes)]
        + [jax.ShapeDtypeStruct(s.shape, bf16) for s in later],
        scratch_shapes=[pltpu.VMEM(h, dt) for h, dt in zip(halves, out_dtypes)]
        + [sem3, sem3, sem3, sem3, pltpu.SemaphoreType.DMA((n,))],
        compiler_params=pltpu.CompilerParams(vmem_limit_bytes=VMEM_LIMIT),
    )(*now, *later)


def _shard_push(srcs, dsts, send_sems, recv_sems, local_sems):
    def remote(w, k, slot):
        x, y, c = _me()
        return pltpu.make_async_remote_copy(
            src_ref=srcs[w], dst_ref=dsts[w].at[slot], send_sem=send_sems.at[w, k - 1],
            recv_sem=recv_sems.at[w, k - 1], device_id=_chip_peer(x, y, c, k), device_id_type=MESH)

    def local(w):
        x, y, _ = _me()
        return pltpu.make_async_copy(srcs[w], dsts[w].at[2 * x + y], local_sems.at[w])

    def start():
        x, y, _ = _me()
        for w in range(len(srcs)):
            local(w).start()
            for k in (1, 2, 3):
                remote(w, k, 2 * x + y).start()

    def wait():
        x, y, _ = _me()
        for w in range(len(srcs)):
            for k in (1, 2, 3):
                remote(w, k, (2 * x + y) ^ k).wait_recv()
        for w in range(len(srcs)):
            for k in (1, 2, 3):
                remote(w, k, 2 * x + y).wait_send()
            local(w).wait()

    return start, wait


def _grad_push(srcs, dsts, send_sems, recv_sems):
    def copy(w, k):
        x, y, c = _me()
        px, py, pc = x ^ (k >> 2), y ^ ((k >> 1) & 1), c ^ (k & 1)
        return pltpu.make_async_remote_copy(
            src_ref=srcs[w].at[2 * px + py, pc], dst_ref=dsts[w].at[k - 1], send_sem=send_sems.at[w, k - 1],
            recv_sem=recv_sems.at[w, k - 1], device_id=(px, py, pc), device_id_type=MESH)

    def start():
        for w in range(len(srcs)):
            for k in range(1, N_DEV):
                copy(w, k).start()

    def wait():
        for w in range(len(srcs)):
            for k in range(1, N_DEV):
                copy(w, k).wait_recv()
        for w in range(len(srcs)):
            for k in range(1, N_DEV):
                copy(w, k).wait_send()

    return start, wait


def _grad_push_specs(grads):
    n = len(grads)
    return ([ANY] * n, [ANY] * n, [jax.ShapeDtypeStruct((N_DEV - 1,) + g.shape[2:], g.dtype) for g in grads],
            [pltpu.SemaphoreType.DMA((n, N_DEV - 1)), pltpu.SemaphoreType.DMA((n, N_DEV - 1))])


def add_eight(own, parts, jc_idx, name):
    _, half, c = parts.shape
    tr = half // 2 if (half // 2) % 16 == 0 else half

    def body(jc_ref, own_ref, p_ref, out_ref):
        acc = own_ref[0, 0].astype(f32)
        for k in range(N_DEV - 1):
            acc = acc + p_ref[k].astype(f32)
        out_ref[0] = acc

    return pl.pallas_call(
        body, name=name,
        grid_spec=pltpu.PrefetchScalarGridSpec(
            num_scalar_prefetch=1, grid=(half // tr,),
            in_specs=[pl.BlockSpec((1, 1, tr, c), lambda t, jc: (jc[0], jc[1], t, 0)),
                      pl.BlockSpec((N_DEV - 1, tr, c), lambda t, jc: (0, t, 0))],
            out_specs=pl.BlockSpec((1, tr, c), lambda t, jc: (jc[1], t, 0))),
        out_shape=jax.ShapeDtypeStruct((2, half, c), f32),
        compiler_params=_cparams(("parallel",)),
    )(jc_idx, own, parts)


def _push_specs(shards):
    n = len(shards)
    return ([ANY] * n, [ANY] * n, [jax.ShapeDtypeStruct((N_SHARD,) + s.shape, s.dtype) for s in shards],
            [pltpu.SemaphoreType.DMA((n, 3)), pltpu.SemaphoreType.DMA((n, 3)), pltpu.SemaphoreType.DMA((n,))])


def pair_exchange_halves(grads, small):
    n = len(grads)

    def body(*refs):
        ins, small_ref = refs[:n], refs[n]
        outs, gath = refs[n + 1:2 * n + 1], refs[2 * n + 1]
        send_sems, recv_sems, s_send, s_recv, local_sem = refs[2 * n + 2:]
        x, y, c = _me()
        me = 4 * x + 2 * y + c
        sends = []
        for w in range(n):
            half = ins[w].shape[1] // 2
            cp = pltpu.make_async_remote_copy(
                src_ref=ins[w].at[:, pl.ds((1 - c) * half, half), :], dst_ref=outs[w],
                send_sem=send_sems.at[w], recv_sem=recv_sems.at[w], device_id=(x, y, 1 - c), device_id_type=MESH)
            cp.start()
            sends.append(cp)
        loc = pltpu.make_async_copy(small_ref, gath.at[me], local_sem)
        loc.start()
        for k in range(1, N_DEV):
            cp = pltpu.make_async_remote_copy(
                src_ref=small_ref, dst_ref=gath.at[me], send_sem=s_send.at[k - 1], recv_sem=s_recv.at[k - 1],
                device_id=(x ^ (k >> 2), y ^ ((k >> 1) & 1), c ^ (k & 1)), device_id_type=MESH)
            cp.start()
            sends.append(cp)
        for w in range(n):
            half = ins[w].shape[1] // 2
            pltpu.make_async_remote_copy(
                src_ref=ins[w].at[:, pl.ds(0, half), :], dst_ref=outs[w], send_sem=send_sems.at[w],
                recv_sem=recv_sems.at[w], device_id=(x, y, 1 - c), device_id_type=MESH).wait_recv()
        for k in range(1, N_DEV):
            pltpu.make_async_remote_copy(
                src_ref=small_ref, dst_ref=gath.at[me ^ k], send_sem=s_send.at[k - 1], recv_sem=s_recv.at[k - 1],
                device_id=(x ^ (k >> 2), y ^ ((k >> 1) & 1), c ^ (k & 1)), device_id_type=MESH).wait_recv()
        for cp in sends:
            cp.wait_send()
        loc.wait()

    return pl.pallas_call(
        body, name="pair_exchange_halves", in_specs=[ANY] * (n + 1), out_specs=[ANY] * (n + 1),
        out_shape=[jax.ShapeDtypeStruct((g.shape[0], g.shape[1] // 2, g.shape[2]), g.dtype) for g in grads]
        + [jax.ShapeDtypeStruct((N_DEV,) + small.shape, f32)],
        scratch_shapes=[pltpu.SemaphoreType.DMA((n,)), pltpu.SemaphoreType.DMA((n,)),
                        pltpu.SemaphoreType.DMA((N_DEV - 1,)), pltpu.SemaphoreType.DMA((N_DEV - 1,)),
                        pltpu.SemaphoreType.DMA],
    )(*grads, small)


def chip_exchange(sums):
    n = len(sums)

    def body(*refs):
        ins, outs = refs[:n], refs[n:2 * n]
        send_sems, recv_sems = refs[2 * n:]
        x, y, c = _me()
        j = 2 * x + y
        sends = []
        for w in range(n):
            for k in (1, 2, 3):
                cp = pltpu.make_async_remote_copy(
                    src_ref=ins[w].at[j ^ k], dst_ref=outs[w].at[k - 1], send_sem=send_sems.at[w, k - 1],
                    recv_sem=recv_sems.at[w, k - 1], device_id=_chip_peer(x, y, c, k), device_id_type=MESH)
                cp.start()
                sends.append(cp)
        for w in range(n):
            for k in (1, 2, 3):
                pltpu.make_async_remote_copy(
                    src_ref=ins[w].at[0], dst_ref=outs[w].at[k - 1], send_sem=send_sems.at[w, k - 1],
                    recv_sem=recv_sems.at[w, k - 1], device_id=_chip_peer(x, y, c, k), device_id_type=MESH).wait_recv()
        for cp in sends:
            cp.wait_send()

    return pl.pallas_call(
        body, name="chip_exchange", in_specs=[ANY] * n, out_specs=[ANY] * n,
        out_shape=[jax.ShapeDtypeStruct((N_SHARD - 1,) + s.shape[1:], s.dtype) for s in sums],
        scratch_shapes=[pltpu.SemaphoreType.DMA((n, 3)), pltpu.SemaphoreType.DMA((n, 3))],
    )(*sums)


def pair_exchange_results(halves):
    n = len(halves)

    def body(*refs):
        ins, outs = refs[:n], refs[n:2 * n]
        send_sems, recv_sems = refs[2 * n:]
        x, y, c = _me()
        sends = []
        for w in range(n):
            cp = pltpu.make_async_remote_copy(
                src_ref=ins[w].at[c], dst_ref=outs[w].at[c], send_sem=send_sems.at[w], recv_sem=recv_sems.at[w],
                device_id=(x, y, 1 - c), device_id_type=MESH)
            cp.start()
            sends.append(cp)
        for w in range(n):
            pltpu.make_async_remote_copy(
                src_ref=ins[w].at[c], dst_ref=outs[w].at[1 - c], send_sem=send_sems.at[w],
                recv_sem=recv_sems.at[w], device_id=(x, y, 1 - c), device_id_type=MESH).wait_recv()
        for cp in sends:
            cp.wait_send()

    return pl.pallas_call(
        body, name="pair_exchange_results", in_specs=[ANY] * n, out_specs=[ANY] * n,
        out_shape=[jax.ShapeDtypeStruct(h.shape, f32) for h in halves],
        input_output_aliases={w: w for w in range(n)},
        scratch_shapes=[pltpu.SemaphoreType.DMA((n,)), pltpu.SemaphoreType.DMA((n,))],
    )(*halves)


def add_pair(grad, other, c_idx, name):
    _, r, c = grad.shape
    half = r // 2
    tr = half // 2 if (half // 2) % 8 == 0 else half
    per = half // tr

    def body(c_ref, g_ref, o_ref, out_ref):
        out_ref[...] = (g_ref[...].astype(f32) + o_ref[...].astype(f32)).astype(bf16)

    return pl.pallas_call(
        body, name=name,
        grid_spec=pltpu.PrefetchScalarGridSpec(
            num_scalar_prefetch=1, grid=(N_SHARD, per),
            in_specs=[pl.BlockSpec((1, tr, c), lambda j, t, cr: (j, cr[0] * per + t, 0)),
                      pl.BlockSpec((1, tr, c), lambda j, t, cr: (j, t, 0))],
            out_specs=pl.BlockSpec((1, tr, c), lambda j, t, cr: (j, t, 0))),
        out_shape=jax.ShapeDtypeStruct((N_SHARD, half, c), bf16),
        compiler_params=_cparams(("parallel", "parallel")),
    )(c_idx, grad, other)


def add_four(own, parts, jc_idx, name):
    _, half, c = parts.shape
    tr = half // 2 if (half // 2) % 8 == 0 else half

    def body(jc_ref, own_ref, p_ref, out_ref):
        acc = own_ref[0].astype(f32)
        for k in range(N_SHARD - 1):
            acc = acc + p_ref[k].astype(f32)
        out_ref[0] = acc

    return pl.pallas_call(
        body, name=name,
        grid_spec=pltpu.PrefetchScalarGridSpec(
            num_scalar_prefetch=1, grid=(half // tr,),
            in_specs=[pl.BlockSpec((1, tr, c), lambda t, jc: (jc[0], t, 0)),
                      pl.BlockSpec((N_SHARD - 1, tr, c), lambda t, jc: (0, t, 0))],
            out_specs=pl.BlockSpec((1, tr, c), lambda t, jc: (jc[1], t, 0))),
        out_shape=jax.ShapeDtypeStruct((2, half, c), f32),
        compiler_params=_cparams(("parallel",)),
    )(jc_idx, own, parts)


def sum_devices(gathered):
    def body(g_ref, out_ref):
        acc = g_ref[0]
        for d in range(1, N_DEV):
            acc = acc + g_ref[d]
        out_ref[...] = acc

    return pl.pallas_call(body, name="sum_devices", out_shape=jax.ShapeDtypeStruct(gathered.shape[1:], f32))(gathered)


def _rows128(a, rows):
    flat = a.reshape(-1, BLOCK) if a.size % BLOCK == 0 else jnp.pad(a.reshape(1, -1), ((0, 0), (0, BLOCK - a.size)))
    return jnp.pad(flat, ((0, rows - flat.shape[0]), (0, 0)))


def kernel(x, meta_tokens, ln_emb_g, ln_emb_b, w_in, hg_lower_bounds, hg_norm_g, attn_sinks, w_branch_hg, w_branch_attn, w_out, ln1_g, ln1_b, w_ffn_in, w_ffn_out, ln2_g, ln2_b, loss_target, m_meta_tokens, m_ln_emb_g, m_ln_emb_b, m_w_in, m_hg_lower_bounds, m_hg_norm_g, m_attn_sinks, m_w_branch_hg, m_w_branch_attn, m_w_out, m_ln1_g, m_ln1_b, m_w_ffn_in, m_w_ffn_out, m_ln2_g, m_ln2_b, v_meta_tokens, v_ln_emb_g, v_ln_emb_b, v_w_in, v_hg_lower_bounds, v_hg_norm_g, v_attn_sinks, v_w_branch_hg, v_w_branch_attn, v_w_out, v_ln1_g, v_ln1_b, v_w_ffn_in, v_w_ffn_out, v_ln2_g, v_ln2_b):
    seq = x.shape[1]
    nb = seq // BLOCK + 1
    xs = x[0]
    ts = loss_target[0]
    ix, iy, ic = _me()
    shard = 2 * ix + iy
    vec = lambda a: a.reshape(1, D_MODEL)

    w_in_t = jnp.swapaxes(w_in[0], 0, 1)
    g_in, g_meta, s_bh, s_ba, s_out, s_fi, s_fo = gather_weights(
        [w_in_t, meta_tokens], [w_branch_hg[0], w_branch_attn[0], w_out[0], w_ffn_in[0], w_ffn_out[0]])
    by_cols = lambda g: g.reshape(N_SHARD, -1, g.shape[-1]).transpose(1, 0, 2).reshape(-1, N_SHARD * g.shape[-1])
    wf_in = g_in.reshape(IN_W, D_MODEL)
    metablk = jnp.pad(by_cols(g_meta), ((TM - N_META, 0), (0, 0)))

    pos = jnp.arange((nb + LEAD) * BLOCK, dtype=jnp.int32) - (LEAD * BLOCK + PAD)
    half = HEAD_DIM // 2
    inv = ROPE_THETA ** (-jnp.arange(half, dtype=f32) / half)
    ang = pos.astype(f32)[:, None] * inv[None, :]
    cos = jnp.tile(jnp.cos(ang), (1, BLOCK // half))
    sin = jnp.tile(jnp.sin(ang), (1, BLOCK // half))
    sinks8 = jnp.broadcast_to(attn_sinks.reshape(ATT_HEADS, 1), (ATT_HEADS, BLOCK))
    ng = hg_norm_g.reshape(1, HG_K)

    h0, h0b, pa, pg = emb_inproj(xs, metablk, vec(ln_emb_g), vec(ln_emb_b), wf_in, cos, sin)
    og, sprev, g_fi, g_out = hgrn_fwd(pa, hg_lower_bounds, ng, nb, [s_fi, s_out])
    oatt, g_fo, g_bh, g_ba = attn_fwd(pa, sinks8, nb, [s_fo, s_bh, s_ba])
    wf_bh, wf_ba, wf_fi = by_cols(g_bh), by_cols(g_ba), by_cols(g_fi)
    wf_out = g_out.reshape(D_MODEL, D_MODEL)
    wf_fo = g_fo.reshape(D_FF, D_MODEL)
    dh1, dau, sact, dr2, mixin, h1b, og, oatt, loss_part, dg2, db2 = mid_front_ffn(
        h0, pg, og, oatt, ts, wf_bh, wf_ba, wf_out, wf_fi, wf_fo, ln1_g, ln1_b, ln2_g, ln2_b)
    dh0p, dpg, dog, doa, dyh, dya, dr1, dg1, db1 = mid_back(dh1, h0, pg, og, oatt, wf_bh, wf_ba, wf_out, ln1_g, ln1_b)
    tp = max(t for t in (768, 512, TM) if h0.shape[0] % t == 0)
    pieces = lambda g: g.reshape(N_SHARD, 2, -1, g.shape[-1])
    gb_bh = pieces(wgrad(og, dyh, "wgrad_bh", 512, D_MODEL, tp, True, bf16))
    gb_ba = pieces(wgrad(oatt, dya, "wgrad_ba", 512, D_MODEL, tp, True, bf16))
    gb_out = pieces(wgrad(mixin, dr1, "wgrad_out", D_MODEL, D_MODEL, tp, False, bf16))
    gb_fi = pieces(wgrad(h1b, dau, "wgrad_fi", D_MODEL, D_FF, tp, True, bf16))
    gb_fo = pieces(wgrad(sact, dr2, "wgrad_fo", D_FF // 2, D_MODEL, tp, False, bf16))
    dhq, dhf, dhi, dhg, dlb4, dng, r_fi, r_fo = hgrn_bwd(pa, hg_lower_bounds, ng, sprev, dog, nb, [gb_fi, gb_fo])
    daq, dkc, dkp, dvc, dvp, dkm, dvm, dsk, r_out, r_bh, r_ba = attn_bwd(pa, sinks8, doa, nb,
                                                                         [gb_out, gb_bh, gb_ba])
    dproj, dx, dmeta, dlg, dlb = inproj_bwd(dh0p, dhq, dhf, dhi, dhg, daq, dkc, dkp, dvc, dvp, dkm, dvm, dpg,
                                      wf_in, xs, metablk, vec(ln_emb_g), vec(ln_emb_b), cos, sin)
    gw_in = wgrad(dproj, h0b, "wgrad_in", IN_W // 2, D_MODEL, tp, False, bf16).reshape(N_SHARD, -1, D_MODEL)

    parts = [(dlg, 8), (dlb, 8), (dlb4, 8), (dng, 8), (dsk[:, 0], 8),
             (dg1, 8), (db1, 8), (dg2, 8), (db2, 8), (dmeta, BLOCK), (loss_part, 8)]
    small = jnp.concatenate([_rows128(a, r) for a, r in parts], axis=0)

    c_idx = jnp.reshape(ic, (1,)).astype(jnp.int32)
    jc_idx = jnp.stack([shard, ic]).astype(jnp.int32)
    other_in, gathered = pair_exchange_halves([gw_in], small)
    sum_in = add_pair(gw_in, other_in, c_idx, "add_pair_in")
    quad_in, = chip_exchange([sum_in])
    halves = [add_four(sum_in, quad_in, jc_idx, "add_four_in")]
    halves += [add_eight(g, r, jc_idx, "add_eight_" + nm) for nm, g, r in
               (("bh", gb_bh, r_bh), ("ba", gb_ba, r_ba), ("out", gb_out, r_out), ("fi", gb_fi, r_fi),
                ("fo", gb_fo, r_fo))]
    red = [r.reshape(-1, r.shape[-1]) for r in pair_exchange_results(halves)]
    small_sum = sum_devices(gathered)

    offs, acc = [], 0
    for _, r in parts:
        offs.append(acc)
        acc += r
    take = lambda n, size: small_sum[offs[n]:offs[n] + parts[n][1]].reshape(-1)[:size]
    g_meta_full = take(9, N_META * D_MODEL).reshape(N_META, D_MODEL)
    g_small = {
        "meta_tokens": lax.dynamic_slice_in_dim(g_meta_full, shard * (D_MODEL // N_SHARD), D_MODEL // N_SHARD, axis=1),
        "ln_emb_g": take(0, D_MODEL), "ln_emb_b": take(1, D_MODEL),
        "hg_lower_bounds": take(2, 2 * HG_HEADS * HG_K).reshape(2, HG_HEADS * HG_K),
        "hg_norm_g": take(3, HG_K).reshape(1, HG_K), "attn_sinks": take(4, ATT_HEADS).reshape(1, ATT_HEADS),
        "ln1_g": take(5, D_MODEL).reshape(1, D_MODEL), "ln1_b": take(6, D_MODEL).reshape(1, D_MODEL),
        "ln2_g": take(7, D_MODEL).reshape(1, D_MODEL), "ln2_b": take(8, D_MODEL).reshape(1, D_MODEL),
    }
    g_big = {"w_in": red[0], "w_branch_hg": red[1], "w_branch_attn": red[2], "w_out": red[3],
             "w_ffn_in": red[4], "w_ffn_out": red[5]}

    names = ["meta_tokens", "ln_emb_g", "ln_emb_b", "w_in", "hg_lower_bounds", "hg_norm_g", "attn_sinks",
             "w_branch_hg", "w_branch_attn", "w_out", "ln1_g", "ln1_b", "w_ffn_in", "w_ffn_out", "ln2_g", "ln2_b"]
    given = dict(
        meta_tokens=(meta_tokens, m_meta_tokens, v_meta_tokens), ln_emb_g=(ln_emb_g, m_ln_emb_g, v_ln_emb_g),
        ln_emb_b=(ln_emb_b, m_ln_emb_b, v_ln_emb_b), w_in=(w_in, m_w_in, v_w_in),
        hg_lower_bounds=(hg_lower_bounds, m_hg_lower_bounds, v_hg_lower_bounds),
        hg_norm_g=(hg_norm_g, m_hg_norm_g, v_hg_norm_g), attn_sinks=(attn_sinks, m_attn_sinks, v_attn_sinks),
        w_branch_hg=(w_branch_hg, m_w_branch_hg, v_w_branch_hg),
        w_branch_attn=(w_branch_attn, m_w_branch_attn, v_w_branch_attn), w_out=(w_out, m_w_out, v_w_out),
        ln1_g=(ln1_g, m_ln1_g, v_ln1_g), ln1_b=(ln1_b, m_ln1_b, v_ln1_b), w_ffn_in=(w_ffn_in, m_w_ffn_in, v_w_ffn_in),
        w_ffn_out=(w_ffn_out, m_w_ffn_out, v_w_ffn_out), ln2_g=(ln2_g, m_ln2_g, v_ln2_g), ln2_b=(ln2_b, m_ln2_b, v_ln2_b))
    two_d = lambda a: a.reshape(8, BLOCK) if a.ndim == 1 else a.reshape(a.shape[-2], a.shape[-1])
    small_names = [nm for nm in names if nm not in g_big]
    small_d, small_m, small_v = adamw_small([two_d(given[nm][0]) for nm in small_names],
                                            [two_d(g_small[nm]) for nm in small_names],
                                            [two_d(given[nm][1]) for nm in small_names],
                                            [two_d(given[nm][2]) for nm in small_names])
    out_g, out_d, out_m, out_v = [], [], [], []
    for nm in names:
        w, m, v = given[nm]
        shape = w.shape
        if nm == "w_in":
            t = lambda a: jnp.swapaxes(two_d(a), 0, 1)
            g, d, mn, vn = [t(a) for a in adamw(t(w), g_big[nm], t(m), t(v), "adamw_" + nm)]
        elif nm in g_big:
            g, d, mn, vn = adamw(two_d(w), g_big[nm], two_d(m), two_d(v), "adamw_" + nm)
        else:
            k = small_names.index(nm)
            g, d, mn, vn = g_small[nm], small_d[k], small_m[k], small_v[k]
        out_g.append(g.reshape(shape))
        out_d.append(d.reshape(shape))
        out_m.append(mn.reshape(shape))
        out_v.append(vn.reshape(shape))

    loss = take(10, 1)[0]
    grad_x = dx.reshape(x.shape)
    return (loss, grad_x, *out_g, *out_d, *out_m, *out_v)
```
